```python
import jax, jax.numpy as jnp
from jax import lax
import numpy as np

D_MODEL = 1024
BATCH = 8
SEQ = 16384
DEPTH = 4

N_MIXERS = 2
N_A_LAYERS = (DEPTH + 1) // 2
N_B_LAYERS = DEPTH // 2
RMS_EPS = 1e-6
NEG_INF = -1e30

A_GROUPS = ((128, 1), (512, 4), (2048, 16))
A_N_GROUPS = len(A_GROUPS)
A_HEADS = 16
A_HEAD_DIM = D_MODEL // A_HEADS
A_WIDTH = A_HEADS * A_HEAD_DIM
A_IN_WIDTH = A_N_GROUPS * 3 * A_WIDTH
ROPE_THETA = 10000.0

B_HEADS = 4
B_KEY_DIM = D_MODEL // 2 // B_HEADS
B_VAL_DIM = D_MODEL // B_HEADS
B_QK_WIDTH = B_HEADS * B_KEY_DIM
B_V_WIDTH = B_HEADS * B_VAL_DIM
B_GATE_RANK = 16
B_GATE_TAU = 16.0
B_CHUNK = 64
B_IN_WIDTH = 2 * B_QK_WIDTH + 2 * B_V_WIDTH + 2 * B_GATE_RANK

FFN_HIDDEN = -(-8 * D_MODEL // (3 * 256)) * 256

kernel_name = "hybrid_dilated_attn_gla_encoder"


def rms_norm(x, gain):
    xf = x.astype(jnp.float32)
    y = xf * lax.rsqrt(jnp.mean(xf * xf, axis=-1, keepdims=True) + RMS_EPS)
    return (y * gain.astype(jnp.float32)).astype(x.dtype)


def rope(x, positions):
    half = x.shape[-1] // 2
    inv_freq = ROPE_THETA ** (-jnp.arange(half, dtype=jnp.float32) / half)
    ang = positions.astype(jnp.float32)[:, None] * inv_freq[None, :]
    cos = jnp.cos(ang)[:, None, :]
    sin = jnp.sin(ang)[:, None, :]
    xf = x.astype(jnp.float32)
    x1, x2 = xf[..., :half], xf[..., half:]
    return jnp.concatenate([x1 * cos - x2 * sin, x2 * cos + x1 * sin], axis=-1).astype(x.dtype)


def dilated_window_attention(q, k, v, window, dilation):
    bsz, seq, nh, dh = q.shape
    half = window // (2 * dilation)
    L = seq // dilation
    nb = -(-L // half)
    Lp = nb * half

    def to_phase(t):
        t = t.reshape(bsz, L, dilation, nh, dh)
        return jnp.moveaxis(t, 2, 1).reshape(bsz * dilation, L, nh, dh)

    n = bsz * dilation
    qp = jnp.pad(to_phase(q), ((0, 0), (0, Lp - L), (0, 0), (0, 0))).reshape(n, nb, half, nh, dh)

    def key_blocks(t):
        t = jnp.pad(to_phase(t), ((0, 0), (half, Lp - L + half), (0, 0), (0, 0)))
        t = t.reshape(n, nb + 2, half, nh, dh)
        return jnp.concatenate([t[:, :-2], t[:, 1:-1], t[:, 2:]], axis=2)

    kb = key_blocks(k)
    vb = key_blocks(v)
    tq = jnp.arange(nb)[:, None] * half + jnp.arange(half)[None, :]
    tk = jnp.arange(nb)[:, None] * half + jnp.arange(3 * half)[None, :] - half
    dist = tk[:, None, :] - tq[:, :, None]
    valid = (tk[:, None, :] >= 0) & (tk[:, None, :] < L) & (jnp.abs(dist) <= half)

    scores = jnp.einsum("nbqhd,nbkhd->nbhqk", qp.astype(jnp.float32), kb.astype(jnp.float32)) * (dh ** -0.5)
    scores = jnp.where(valid[None, :, None], scores, NEG_INF)
    m = jnp.max(scores, axis=-1, keepdims=True)
    p = jnp.exp(scores - m)
    l = jnp.sum(p, axis=-1)
    o = jnp.einsum("nbhqk,nbkhd->nbqhd", p, vb.astype(jnp.float32))
    o = o / jnp.moveaxis(l, 2, 3)[..., None]
    lse = jnp.moveaxis(m[..., 0] + jnp.log(l), 2, 3)

    def from_phase(t):
        rest = t.shape[4:]
        t = t.reshape(bsz, dilation, Lp, nh, *rest)[:, :, :L]
        return jnp.moveaxis(t, 1, 2).reshape(bsz, seq, nh, *rest)

    return from_phase(o), from_phase(lse)


def dilated_attention_mixer(h, w_in, q_gain, k_gain, w_out, positions):
    bsz, seq, _ = h.shape
    qkv = (h @ w_in).reshape(bsz, seq, A_N_GROUPS, 3, A_HEADS, A_HEAD_DIM)
    outs, lses = [], []
    for g, (window, dilation) in enumerate(A_GROUPS):
        q = rope(rms_norm(qkv[:, :, g, 0], q_gain[g]), positions)
        k = rope(rms_norm(qkv[:, :, g, 1], k_gain[g]), positions)
        v = qkv[:, :, g, 2]
        o, lse = dilated_window_attention(q, k, v, window, dilation)
        outs.append(o)
        lses.append(lse)
    alpha = jax.nn.softmax(jnp.stack(lses, axis=0), axis=0)
    out = jnp.sum(alpha[..., None] * jnp.stack(outs, axis=0), axis=0)
    return out.reshape(bsz, seq, A_WIDTH).astype(h.dtype) @ w_out


def gla_chunk(q, k, v, log_a, strict):
    bsz, nh, seq, dk = q.shape
    dv = v.shape[-1]
    nc = seq // B_CHUNK
    q = q.astype(jnp.float32).reshape(bsz, nh, nc, B_CHUNK, dk)
    k = k.astype(jnp.float32).reshape(bsz, nh, nc, B_CHUNK, dk)
    v = v.astype(jnp.float32).reshape(bsz, nh, nc, B_CHUNK, dv)
    b = jnp.cumsum(log_a.astype(jnp.float32).reshape(bsz, nh, nc, B_CHUNK, dk), axis=3)
    b_last = b[..., -1:, :]
    q_t = q * jnp.exp(b)
    k_t = k * jnp.exp(-b)
    k_end = k * jnp.exp(b_last - b)
    mask = jnp.tril(jnp.ones((B_CHUNK, B_CHUNK), dtype=bool), k=-1 if strict else 0)
    attn = jnp.where(mask, jnp.einsum("bhncd,bhnsd->bhncs", q_t, k_t), 0.0)
    o_intra = jnp.einsum("bhncs,bhnse->bhnce", attn, v)
    chunk_kv = jnp.einsum("bhncd,bhnce->bhnde", k_end, v)
    decay = jnp.exp(b_last[..., 0, :])

    def step(state, inp):
        kv_n, dec_n = inp
        return dec_n[..., None] * state + kv_n, state

    init = jnp.zeros((bsz, nh, dk, dv), jnp.float32)
    _, s_in = lax.scan(step, init, (jnp.moveaxis(chunk_kv, 2, 0), jnp.moveaxis(decay, 2, 0)))
    s_in = jnp.moveaxis(s_in, 0, 2)
    o_inter = jnp.einsum("bhncd,bhnde->bhnce", q_t, s_in)
    return (o_intra + o_inter).reshape(bsz, nh, seq, dv)


def _heads(t, nh):
    bsz, seq, _ = t.shape
    return t.reshape(bsz, seq, nh, -1).transpose(0, 2, 1, 3)


def gla_mixer(h, w_in, w_gate_f, bias_gate_f, w_gate_b, bias_gate_b, out_gain, w_out):
    bsz, seq, _ = h.shape
    proj = h @ w_in
    cuts = np.cumsum([B_QK_WIDTH, B_QK_WIDTH, B_V_WIDTH, B_V_WIDTH, B_GATE_RANK]).tolist()
    q, k, v, r, zf, zb = jnp.split(proj, cuts, axis=-1)
    q = _heads(q, B_HEADS) * (B_KEY_DIM ** -0.5)
    k = _heads(k, B_HEADS)
    v = _heads(v, B_HEADS)
    log_af = jax.nn.log_sigmoid((zf @ w_gate_f + bias_gate_f).astype(jnp.float32)) / B_GATE_TAU
    log_ab = jax.nn.log_sigmoid((zb @ w_gate_b + bias_gate_b).astype(jnp.float32)) / B_GATE_TAU
    log_af = _heads(log_af, B_HEADS)
    log_ab = _heads(log_ab, B_HEADS)
    o_f = gla_chunk(q, k, v, log_af, strict=False)
    flip = lambda t: jnp.flip(t, axis=2)
    o_b = flip(gla_chunk(flip(q), flip(k), flip(v), flip(log_ab), strict=True))
    o = (o_f + o_b).transpose(0, 2, 1, 3)
    o = rms_norm(o, out_gain).reshape(bsz, seq, B_V_WIDTH)
    o = o * jax.nn.silu(r.astype(jnp.float32))
    return o.astype(h.dtype) @ w_out


def swiglu(h, w_gate_up, w_down):
    g, u = jnp.split(h @ w_gate_up, 2, axis=-1)
    return (jax.nn.silu(g) * u) @ w_down


def _fwd_setup_inputs(seed: int = 0) -> dict:
    key = jax.random.key(seed)
    ks = jax.random.split(key, 16)

    def nrm(k, shape, scale):
        return jax.random.normal(k, shape, jnp.float32) * scale

    return {
        "x": nrm(ks[0], (BATCH, SEQ, D_MODEL), 1.0),
        "attn_norm": 1.0 + nrm(ks[1], (DEPTH, D_MODEL), 0.02),
        "ffn_norm": 1.0 + nrm(ks[2], (DEPTH, D_MODEL), 0.02),
        "a_w_in": nrm(ks[3], (N_A_LAYERS, D_MODEL, A_IN_WIDTH), D_MODEL ** -0.5),
        "a_q_norm": 1.0 + nrm(ks[4], (N_A_LAYERS, A_N_GROUPS, A_HEAD_DIM), 0.02),
        "a_k_norm": 1.0 + nrm(ks[5], (N_A_LAYERS, A_N_GROUPS, A_HEAD_DIM), 0.02),
        "a_w_out": nrm(ks[6], (N_A_LAYERS, A_WIDTH, D_MODEL), A_WIDTH ** -0.5),
        "b_w_in": nrm(ks[7], (N_B_LAYERS, D_MODEL, B_IN_WIDTH), D_MODEL ** -0.5),
        "b_w_gate_f": nrm(ks[8], (N_B_LAYERS, B_GATE_RANK, B_QK_WIDTH), B_GATE_RANK ** -0.5),
        "b_gate_bias_f": nrm(ks[9], (N_B_LAYERS, B_QK_WIDTH), 0.1),
        "b_w_gate_b": nrm(ks[10], (N_B_LAYERS, B_GATE_RANK, B_QK_WIDTH), B_GATE_RANK ** -0.5),
        "b_gate_bias_b": nrm(ks[11], (N_B_LAYERS, B_QK_WIDTH), 0.1),
        "b_out_norm": 1.0 + nrm(ks[12], (N_B_LAYERS, B_HEADS, B_VAL_DIM), 0.02),
        "b_w_out": nrm(ks[13], (N_B_LAYERS, B_V_WIDTH, D_MODEL), B_V_WIDTH ** -0.5),
        "ffn_w_gate_up": nrm(ks[14], (DEPTH, D_MODEL, 2 * FFN_HIDDEN), D_MODEL ** -0.5),
        "ffn_w_down": nrm(ks[15], (DEPTH, FFN_HIDDEN, D_MODEL), FFN_HIDDEN ** -0.5),
    }


def _fwd_reference(x, attn_norm, ffn_norm, a_w_in, a_q_norm, a_k_norm, a_w_out, b_w_in, b_w_gate_f, b_gate_bias_f, b_w_gate_b, b_gate_bias_b, b_out_norm, b_w_out, ffn_w_gate_up, ffn_w_down):
    positions = jnp.arange(x.shape[1])
    h = x
    for i in range(DEPTH):
        j = i // N_MIXERS
        hn = rms_norm(h, attn_norm[i])
        if i % N_MIXERS == 0:
            mix = dilated_attention_mixer(hn, a_w_in[j], a_q_norm[j], a_k_norm[j], a_w_out[j], positions)
        else:
            mix = gla_mixer(hn, b_w_in[j], b_w_gate_f[j], b_gate_bias_f[j], b_w_gate_b[j], b_gate_bias_b[j], b_out_norm[j], b_w_out[j])
        h = h + mix.astype(h.dtype)
        h = h + swiglu(rms_norm(h, ffn_norm[i]), ffn_w_gate_up[i], ffn_w_down[i]).astype(h.dtype)
    return h


import jax as _jax
import jax.numpy as _jnp

TWIN_FORMAT = 'train_step'
FWD_PARAMS = ['x', 'attn_norm', 'ffn_norm', 'a_w_in', 'a_q_norm', 'a_k_norm', 'a_w_out', 'b_w_in', 'b_w_gate_f', 'b_gate_bias_f', 'b_w_gate_b', 'b_gate_bias_b', 'b_out_norm', 'b_w_out', 'ffn_w_gate_up', 'ffn_w_down']
TWIN_WEIGHTS = ['attn_norm', 'ffn_norm', 'a_w_in', 'a_q_norm', 'a_k_norm', 'a_w_out', 'b_w_in', 'b_w_gate_f', 'b_gate_bias_f', 'b_w_gate_b', 'b_gate_bias_b', 'b_out_norm', 'b_w_out', 'ffn_w_gate_up', 'ffn_w_down']
TWIN_DIFF_INPUT = 'x'
TWIN_INPUTS = ['x', 'attn_norm', 'ffn_norm', 'a_w_in', 'a_q_norm', 'a_k_norm', 'a_w_out', 'b_w_in', 'b_w_gate_f', 'b_gate_bias_f', 'b_w_gate_b', 'b_gate_bias_b', 'b_out_norm', 'b_w_out', 'ffn_w_gate_up', 'ffn_w_down', 'loss_target', 'm_attn_norm', 'm_ffn_norm', 'm_a_w_in', 'm_a_q_norm', 'm_a_k_norm', 'm_a_w_out', 'm_b_w_in', 'm_b_w_gate_f', 'm_b_gate_bias_f', 'm_b_w_gate_b', 'm_b_gate_bias_b', 'm_b_out_norm', 'm_b_w_out', 'm_ffn_w_gate_up', 'm_ffn_w_down', 'v_attn_norm', 'v_ffn_norm', 'v_a_w_in', 'v_a_q_norm', 'v_a_k_norm', 'v_a_w_out', 'v_b_w_in', 'v_b_w_gate_f', 'v_b_gate_bias_f', 'v_b_w_gate_b', 'v_b_gate_bias_b', 'v_b_out_norm', 'v_b_w_out', 'v_ffn_w_gate_up', 'v_ffn_w_down']
TWIN_OUTPUTS = ['loss', 'grad_x', 'grad_attn_norm', 'grad_ffn_norm', 'grad_a_w_in', 'grad_a_q_norm', 'grad_a_k_norm', 'grad_a_w_out', 'grad_b_w_in', 'grad_b_w_gate_f', 'grad_b_gate_bias_f', 'grad_b_w_gate_b', 'grad_b_gate_bias_b', 'grad_b_out_norm', 'grad_b_w_out', 'grad_ffn_w_gate_up', 'grad_ffn_w_down', 'delta_attn_norm', 'delta_ffn_norm', 'delta_a_w_in', 'delta_a_q_norm', 'delta_a_k_norm', 'delta_a_w_out', 'delta_b_w_in', 'delta_b_w_gate_f', 'delta_b_gate_bias_f', 'delta_b_w_gate_b', 'delta_b_gate_bias_b', 'delta_b_out_norm', 'delta_b_w_out', 'delta_ffn_w_gate_up', 'delta_ffn_w_down', 'new_m_attn_norm', 'new_m_ffn_norm', 'new_m_a_w_in', 'new_m_a_q_norm', 'new_m_a_k_norm', 'new_m_a_w_out', 'new_m_b_w_in', 'new_m_b_w_gate_f', 'new_m_b_gate_bias_f', 'new_m_b_w_gate_b', 'new_m_b_gate_bias_b', 'new_m_b_out_norm', 'new_m_b_w_out', 'new_m_ffn_w_gate_up', 'new_m_ffn_w_down', 'new_v_attn_norm', 'new_v_ffn_norm', 'new_v_a_w_in', 'new_v_a_q_norm', 'new_v_a_k_norm', 'new_v_a_w_out', 'new_v_b_w_in', 'new_v_b_w_gate_f', 'new_v_b_gate_bias_f', 'new_v_b_w_gate_b', 'new_v_b_gate_bias_b', 'new_v_b_out_norm', 'new_v_b_w_out', 'new_v_ffn_w_gate_up', 'new_v_ffn_w_down']
TWIN_LEAF_KINDS = {'loss': 'loss', 'grad_x': 'grad_x', 'grad_attn_norm': 'grad_w', 'grad_ffn_norm': 'grad_w', 'grad_a_w_in': 'grad_w', 'grad_a_q_norm': 'grad_w', 'grad_a_k_norm': 'grad_w', 'grad_a_w_out': 'grad_w', 'grad_b_w_in': 'grad_w', 'grad_b_w_gate_f': 'grad_w', 'grad_b_gate_bias_f': 'grad_w', 'grad_b_w_gate_b': 'grad_w', 'grad_b_gate_bias_b': 'grad_w', 'grad_b_out_norm': 'grad_w', 'grad_b_w_out': 'grad_w', 'grad_ffn_w_gate_up': 'grad_w', 'grad_ffn_w_down': 'grad_w', 'delta_attn_norm': 'delta_w', 'delta_ffn_norm': 'delta_w', 'delta_a_w_in': 'delta_w', 'delta_a_q_norm': 'delta_w', 'delta_a_k_norm': 'delta_w', 'delta_a_w_out': 'delta_w', 'delta_b_w_in': 'delta_w', 'delta_b_w_gate_f': 'delta_w', 'delta_b_gate_bias_f': 'delta_w', 'delta_b_w_gate_b': 'delta_w', 'delta_b_gate_bias_b': 'delta_w', 'delta_b_out_norm': 'delta_w', 'delta_b_w_out': 'delta_w', 'delta_ffn_w_gate_up': 'delta_w', 'delta_ffn_w_down': 'delta_w', 'new_m_attn_norm': 'new_m', 'new_m_ffn_norm': 'new_m', 'new_m_a_w_in': 'new_m', 'new_m_a_q_norm': 'new_m', 'new_m_a_k_norm': 'new_m', 'new_m_a_w_out': 'new_m', 'new_m_b_w_in': 'new_m', 'new_m_b_w_gate_f': 'new_m', 'new_m_b_gate_bias_f': 'new_m', 'new_m_b_w_gate_b': 'new_m', 'new_m_b_gate_bias_b': 'new_m', 'new_m_b_out_norm': 'new_m', 'new_m_b_w_out': 'new_m', 'new_m_ffn_w_gate_up': 'new_m', 'new_m_ffn_w_down': 'new_m', 'new_v_attn_norm': 'new_v', 'new_v_ffn_norm': 'new_v', 'new_v_a_w_in': 'new_v', 'new_v_a_q_norm': 'new_v', 'new_v_a_k_norm': 'new_v', 'new_v_a_w_out': 'new_v', 'new_v_b_w_in': 'new_v', 'new_v_b_w_gate_f': 'new_v', 'new_v_b_gate_bias_f': 'new_v', 'new_v_b_w_gate_b': 'new_v', 'new_v_b_gate_bias_b': 'new_v', 'new_v_b_out_norm': 'new_v', 'new_v_b_w_out': 'new_v', 'new_v_ffn_w_gate_up': 'new_v', 'new_v_ffn_w_down': 'new_v'}


def _forward(args):
    return _fwd_reference(*[args[k] for k in FWD_PARAMS])


def _output_shape():
    def fwd():
        inp = _fwd_setup_inputs(0)
        return _fwd_reference(*[inp[k] for k in FWD_PARAMS])
    out = _jax.eval_shape(fwd)
    return out.shape, out.dtype

N_MICROBATCH = 1
ADAM_LR = 0.001
ADAM_B1 = 0.9
ADAM_B2 = 0.999
ADAM_EPS = 1e-08
ADAM_WD = 0.01
ADAM_STEP = 10
PER_EXAMPLE_BATCH_AXIS = {'x': 0, 'loss_target': 0}
SHARED_INPUTS = []
_WEIGHT_DTYPES = {'attn_norm': _jnp.float32, 'ffn_norm': _jnp.float32, 'a_w_in': _jnp.float32, 'a_q_norm': _jnp.float32, 'a_k_norm': _jnp.float32, 'a_w_out': _jnp.float32, 'b_w_in': _jnp.float32, 'b_w_gate_f': _jnp.float32, 'b_gate_bias_f': _jnp.float32, 'b_w_gate_b': _jnp.float32, 'b_gate_bias_b': _jnp.float32, 'b_out_norm': _jnp.float32, 'b_w_out': _jnp.float32, 'ffn_w_gate_up': _jnp.float32, 'ffn_w_down': _jnp.float32}
MOMENT_SCALE = {'attn_norm': 3.619578e+01, 'ffn_norm': 9.824308e+01, 'a_w_in': 1.624210e-01, 'a_q_norm': 1.519301e+00, 'a_k_norm': 1.542594e+00, 'a_w_out': 3.012605e-01, 'b_w_in': 1.070294e+00, 'b_w_gate_f': 9.201998e-02, 'b_gate_bias_f': 3.848406e-01, 'b_w_gate_b': 9.584059e-02, 'b_gate_bias_b': 3.950345e-01, 'b_out_norm': 4.456394e+01, 'b_w_out': 1.018985e+00, 'ffn_w_gate_up': 6.738857e-01, 'ffn_w_down': 1.140499e+00}


def _to_microbatches(a, axis):
    t = _jnp.moveaxis(a, axis, 0)
    t = t.reshape((N_MICROBATCH, t.shape[0] // N_MICROBATCH) + t.shape[1:])
    return _jnp.moveaxis(t, 1, axis + 1)


def setup_inputs(seed: int = 0) -> dict:
    inp = _fwd_setup_inputs(seed)
    key = _jax.random.fold_in(_jax.random.key(seed), 7919)
    shape, _ = _output_shape()
    out = dict(inp)
    out["loss_target"] = _jax.random.normal(_jax.random.fold_in(key, 0), shape, _jnp.float32)
    for i, name in enumerate(TWIN_WEIGHTS):
        w = inp[name].astype(_jnp.float32)
        if MOMENT_SCALE is None:
            s = _jnp.sqrt(_jnp.mean(_jnp.square(w)) + 1e-30)
        else:
            s = MOMENT_SCALE[name]
        km, kv = _jax.random.split(_jax.random.fold_in(key, i + 1))
        out[name] = w
        out["m_" + name] = s * _jax.random.normal(km, w.shape, _jnp.float32)
        out["v_" + name] = (s * s) * _jax.random.uniform(kv, w.shape, _jnp.float32, 0.5, 1.5)
    if N_MICROBATCH > 1:
        for name, axis in PER_EXAMPLE_BATCH_AXIS.items():
            out[name] = _to_microbatches(out[name], axis)
    return {'x': out['x'], 'attn_norm': out['attn_norm'], 'ffn_norm': out['ffn_norm'], 'a_w_in': out['a_w_in'], 'a_q_norm': out['a_q_norm'], 'a_k_norm': out['a_k_norm'], 'a_w_out': out['a_w_out'], 'b_w_in': out['b_w_in'], 'b_w_gate_f': out['b_w_gate_f'], 'b_gate_bias_f': out['b_gate_bias_f'], 'b_w_gate_b': out['b_w_gate_b'], 'b_gate_bias_b': out['b_gate_bias_b'], 'b_out_norm': out['b_out_norm'], 'b_w_out': out['b_w_out'], 'ffn_w_gate_up': out['ffn_w_gate_up'], 'ffn_w_down': out['ffn_w_down'], 'loss_target': out['loss_target'], 'm_attn_norm': out['m_attn_norm'], 'm_ffn_norm': out['m_ffn_norm'], 'm_a_w_in': out['m_a_w_in'], 'm_a_q_norm': out['m_a_q_norm'], 'm_a_k_norm': out['m_a_k_norm'], 'm_a_w_out': out['m_a_w_out'], 'm_b_w_in': out['m_b_w_in'], 'm_b_w_gate_f': out['m_b_w_gate_f'], 'm_b_gate_bias_f': out['m_b_gate_bias_f'], 'm_b_w_gate_b': out['m_b_w_gate_b'], 'm_b_gate_bias_b': out['m_b_gate_bias_b'], 'm_b_out_norm': out['m_b_out_norm'], 'm_b_w_out': out['m_b_w_out'], 'm_ffn_w_gate_up': out['m_ffn_w_gate_up'], 'm_ffn_w_down': out['m_ffn_w_down'], 'v_attn_norm': out['v_attn_norm'], 'v_ffn_norm': out['v_ffn_norm'], 'v_a_w_in': out['v_a_w_in'], 'v_a_q_norm': out['v_a_q_norm'], 'v_a_k_norm': out['v_a_k_norm'], 'v_a_w_out': out['v_a_w_out'], 'v_b_w_in': out['v_b_w_in'], 'v_b_w_gate_f': out['v_b_w_gate_f'], 'v_b_gate_bias_f': out['v_b_gate_bias_f'], 'v_b_w_gate_b': out['v_b_w_gate_b'], 'v_b_gate_bias_b': out['v_b_gate_bias_b'], 'v_b_out_norm': out['v_b_out_norm'], 'v_b_w_out': out['v_b_w_out'], 'v_ffn_w_gate_up': out['v_ffn_w_gate_up'], 'v_ffn_w_down': out['v_ffn_w_down']}


def _loss(weights, diff, rest, loss_target):
    with _jax.named_scope("forward"):
        args = {**rest, TWIN_DIFF_INPUT: diff, **{k: w.astype(_WEIGHT_DTYPES[k]) for k, w in weights.items()}}
        y = _forward(args)
    with _jax.named_scope("loss_head"):
        err = _jnp.square(y.astype(_jnp.float32) - loss_target)
        return 0.5 * _jnp.sum(_jnp.mean(err, axis=-1)) if err.ndim else 0.5 * err


def _adamw(w, g, m, v):
    m = ADAM_B1 * m + (1.0 - ADAM_B1) * g
    v = ADAM_B2 * v + (1.0 - ADAM_B2) * _jnp.square(g)
    m_hat = m / (1.0 - ADAM_B1 ** ADAM_STEP)
    v_hat = v / (1.0 - ADAM_B2 ** ADAM_STEP)
    delta = -ADAM_LR * (m_hat / (_jnp.sqrt(v_hat) + ADAM_EPS) + ADAM_WD * w)
    return delta, m, v


def reference(x, attn_norm, ffn_norm, a_w_in, a_q_norm, a_k_norm, a_w_out, b_w_in, b_w_gate_f, b_gate_bias_f, b_w_gate_b, b_gate_bias_b, b_out_norm, b_w_out, ffn_w_gate_up, ffn_w_down, loss_target, m_attn_norm, m_ffn_norm, m_a_w_in, m_a_q_norm, m_a_k_norm, m_a_w_out, m_b_w_in, m_b_w_gate_f, m_b_gate_bias_f, m_b_w_gate_b, m_b_gate_bias_b, m_b_out_norm, m_b_w_out, m_ffn_w_gate_up, m_ffn_w_down, v_attn_norm, v_ffn_norm, v_a_w_in, v_a_q_norm, v_a_k_norm, v_a_w_out, v_b_w_in, v_b_w_gate_f, v_b_gate_bias_f, v_b_w_gate_b, v_b_gate_bias_b, v_b_out_norm, v_b_w_out, v_ffn_w_gate_up, v_ffn_w_down):
    given = dict(x=x, attn_norm=attn_norm, ffn_norm=ffn_norm, a_w_in=a_w_in, a_q_norm=a_q_norm, a_k_norm=a_k_norm, a_w_out=a_w_out, b_w_in=b_w_in, b_w_gate_f=b_w_gate_f, b_gate_bias_f=b_gate_bias_f, b_w_gate_b=b_w_gate_b, b_gate_bias_b=b_gate_bias_b, b_out_norm=b_out_norm, b_w_out=b_w_out, ffn_w_gate_up=ffn_w_gate_up, ffn_w_down=ffn_w_down, loss_target=loss_target, m_attn_norm=m_attn_norm, m_ffn_norm=m_ffn_norm, m_a_w_in=m_a_w_in, m_a_q_norm=m_a_q_norm, m_a_k_norm=m_a_k_norm, m_a_w_out=m_a_w_out, m_b_w_in=m_b_w_in, m_b_w_gate_f=m_b_w_gate_f, m_b_gate_bias_f=m_b_gate_bias_f, m_b_w_gate_b=m_b_w_gate_b, m_b_gate_bias_b=m_b_gate_bias_b, m_b_out_norm=m_b_out_norm, m_b_w_out=m_b_w_out, m_ffn_w_gate_up=m_ffn_w_gate_up, m_ffn_w_down=m_ffn_w_down, v_attn_norm=v_attn_norm, v_ffn_norm=v_ffn_norm, v_a_w_in=v_a_w_in, v_a_q_norm=v_a_q_norm, v_a_k_norm=v_a_k_norm, v_a_w_out=v_a_w_out, v_b_w_in=v_b_w_in, v_b_w_gate_f=v_b_w_gate_f, v_b_gate_bias_f=v_b_gate_bias_f, v_b_w_gate_b=v_b_w_gate_b, v_b_gate_bias_b=v_b_gate_bias_b, v_b_out_norm=v_b_out_norm, v_b_w_out=v_b_w_out, v_ffn_w_gate_up=v_ffn_w_gate_up, v_ffn_w_down=v_ffn_w_down)
    weights = {n: given[n] for n in TWIN_WEIGHTS}
    shared = {n: given[n] for n in SHARED_INPUTS}
    per_example = {n: given[n] for n in ['x']}
    grad_fn = _jax.value_and_grad(_loss, argnums=(0, 1))

    def one_microbatch(ex, loss_target):
        ex = dict(ex)
        diff = ex.pop(TWIN_DIFF_INPUT)
        return grad_fn(weights, diff, {**shared, **ex}, loss_target)

    if N_MICROBATCH == 1:
        loss, (grad_w, grad_x) = one_microbatch(per_example, given["loss_target"])
    else:
        def body(carry, xs):
            loss_sum, grad_sum = carry
            l_k, (gw_k, gx_k) = one_microbatch(xs[0], xs[1])
            with _jax.named_scope("update"):
                return (loss_sum + l_k, _jax.tree.map(_jnp.add, grad_sum, gw_k)), gx_k

        init = (_jnp.zeros((), _jnp.float32), _jax.tree.map(_jnp.zeros_like, weights))
        (loss, grad_w), grad_x = _jax.lax.scan(body, init, (per_example, given["loss_target"]))
    with _jax.named_scope("update"):
        delta_w, new_m, new_v = {}, {}, {}
        for n in TWIN_WEIGHTS:
            delta_w[n], new_m[n], new_v[n] = _adamw(weights[n], grad_w[n], given["m_" + n], given["v_" + n])
    return (loss, grad_x, *[grad_w[n] for n in TWIN_WEIGHTS], *[delta_w[n] for n in TWIN_WEIGHTS],
            *[new_m[n] for n in TWIN_WEIGHTS], *[new_v[n] for n in TWIN_WEIGHTS])
```

```python
import functools

import jax
import jax.numpy as jnp
from jax import lax
from jax.experimental import pallas as pl
from jax.experimental.pallas import tpu as pltpu

F32 = jnp.float32
BF16 = jnp.bfloat16
MXU_DTYPE = jnp.bfloat16

D_MODEL = 1024
N_DEV = 8
RMS_EPS = 1e-6
NEG_INF = -1e30
A_GROUPS = ((128, 1), (512, 4), (2048, 16))
A_HEAD_DIM = 64
A_HALF = 64
B_HEADS = 4
B_KEY_DIM = 128
B_VAL_DIM = 256
B_CHUNK = 64
B_GATE_TAU = 16.0
FFN_HIDDEN = 2816
FFN_BLK = 2 * FFN_HIDDEN // N_DEV
ADAM_LR, ADAM_B1, ADAM_B2, ADAM_EPS, ADAM_WD, ADAM_STEP = 0.001, 0.9, 0.999, 1e-08, 0.01, 10
ROPE_THETA = 10000.0

V7X_VMEM_LIMIT = 56 * 1024 * 1024
LANES = 128
MESH = pl.DeviceIdType.MESH
ANY = pl.BlockSpec(memory_space=pl.ANY)
VMEM_SPEC = pl.BlockSpec(memory_space=pltpu.VMEM)

NN = ((1,), (0,))
NT = ((1,), (1,))
TN = ((0,), (0,))


def _dot(a, b, dims):
    return lax.dot_general(a.astype(MXU_DTYPE), b.astype(MXU_DTYPE), (dims, ((), ())), preferred_element_type=F32)


@jax.custom_vjp
def dot_nn(a, b):
    return _dot(a, b, NN)


@jax.custom_vjp
def dot_nt(a, b):
    return _dot(a, b, NT)


@jax.custom_vjp
def dot_tn(a, b):
    return _dot(a, b, TN)


dot_nn.defvjp(lambda a, b: (_dot(a, b, NN), (a, b)), lambda r, g: (dot_nt(g, r[1]), dot_tn(r[0], g)))
dot_nt.defvjp(lambda a, b: (_dot(a, b, NT), (a, b)), lambda r, g: (dot_nn(g, r[1]), dot_tn(g, r[0])))
dot_tn.defvjp(lambda a, b: (_dot(a, b, TN), (a, b)), lambda r, g: (dot_nt(r[1], g), dot_nn(r[0], g)))


def _dot_f32(a, b):
    return lax.dot_general(a, b, (NN, ((), ())), precision=lax.Precision.HIGHEST, preferred_element_type=F32)


def _tri(n, upper):
    r = lax.broadcasted_iota(jnp.int32, (n, n), 0)
    c = lax.broadcasted_iota(jnp.int32, (n, n), 1)
    return jnp.where((c >= r) if upper else (c <= r), 1.0, 0.0).astype(F32)


@functools.partial(jax.custom_vjp, nondiff_argnums=(1,))
def cumsum_rows(x, reverse):
    return _dot_f32(_tri(x.shape[0], reverse), x)


cumsum_rows.defvjp(
    lambda x, reverse: (_dot_f32(_tri(x.shape[0], reverse), x), None),
    lambda reverse, _, g: (_dot_f32(_tri(g.shape[0], not reverse), g),),
)


def _head_expand_matrix(transposed):
    shape = (D_MODEL, LANES) if transposed else (LANES, D_MODEL)
    h = lax.broadcasted_iota(jnp.int32, shape, 1 if transposed else 0)
    c = lax.broadcasted_iota(jnp.int32, shape, 0 if transposed else 1)
    return jnp.where(c // A_HEAD_DIM == h, 1.0, 0.0).astype(F32)


@jax.custom_vjp
def expand_heads(a):
    return _dot_f32(a, _head_expand_matrix(False))


expand_heads.defvjp(
    lambda a: (_dot_f32(a, _head_expand_matrix(False)), None),
    lambda _, g: (_dot_f32(g, _head_expand_matrix(True)),),
)


def _swap32_raw(x):
    lane = lax.broadcasted_iota(jnp.int32, x.shape, 1)
    return jnp.where((lane % 64) < 32, pltpu.roll(x, 96, 1), pltpu.roll(x, 32, 1))


@jax.custom_vjp
def swap32(x):
    return _swap32_raw(x)


swap32.defvjp(lambda x: (_swap32_raw(x), None), lambda _, g: (_swap32_raw(g),))


def _rms(x, gain):
    return x * lax.rsqrt(jnp.mean(x * x, axis=-1, keepdims=True) + RMS_EPS) * gain


def _sigmoid(x):
    return 1.0 / (1.0 + jnp.exp(-x))


def _log_sigmoid(x):
    return jnp.minimum(x, 0.0) - jnp.log(1.0 + jnp.exp(-jnp.abs(x)))


def _qk_prep(x, tab, gain):
    lo = lax.broadcasted_iota(jnp.int32, (1, LANES), 1) < A_HEAD_DIM
    x2 = x * x
    s_lo = jnp.sum(jnp.where(lo, x2, 0.0), axis=-1, keepdims=True)
    s_hi = jnp.sum(jnp.where(lo, 0.0, x2), axis=-1, keepdims=True)
    r = jnp.where(lo, lax.rsqrt(s_lo / A_HEAD_DIM + RMS_EPS), lax.rsqrt(s_hi / A_HEAD_DIM + RMS_EPS))
    xn = (x * r) * gain
    return xn * tab[:, :LANES] + swap32(xn) * tab[:, LANES:]


def _attn_pair(qr, kr, v, tq, tk, gq, gk, valid, pair):
    lane = lax.broadcasted_iota(jnp.int32, (1, LANES), 1)
    lo = lane < A_HEAD_DIM
    q = _qk_prep(qr, tq, gq)
    k = _qk_prep(kr, tk, gk)
    out = None
    lse_slab = None
    for par in range(2):
        m = lo if par == 0 else jnp.logical_not(lo)
        s = dot_nt(jnp.where(m, q, 0.0), k) * (A_HEAD_DIM ** -0.5)
        s = jnp.where(valid, s, NEG_INF)
        mx = lax.stop_gradient(jnp.max(s, axis=-1, keepdims=True))
        p = jnp.exp(s - mx)
        l = jnp.sum(p, axis=-1, keepdims=True)
        oh = dot_nn(p, jnp.where(m, v, 0.0)) / l
        lse = jnp.where(lane == 2 * pair + par, mx + jnp.log(l), 0.0)
        out = oh if out is None else out + oh
        lse_slab = lse if lse_slab is None else lse_slab + lse
    return out, lse_slab


def _merge_groups(o0, o1, o2, l0, l1, l2):
    mx = lax.stop_gradient(jnp.maximum(jnp.maximum(l0, l1), l2))
    e0, e1, e2 = jnp.exp(l0 - mx), jnp.exp(l1 - mx), jnp.exp(l2 - mx)
    den = e0 + e1 + e2
    return expand_heads(e0 / den) * o0 + expand_heads(e1 / den) * o1 + expand_heads(e2 / den) * o2


def _gla_tile(q, k, v, la, st, reverse):
    nc = q.shape[0] // B_CHUNK
    q = q * (B_KEY_DIM ** -0.5)
    r = lax.broadcasted_iota(jnp.int32, (B_CHUNK, B_CHUNK), 0)
    c = lax.broadcasted_iota(jnp.int32, (B_CHUNK, B_CHUNK), 1)
    mask = (c > r) if reverse else (c <= r)
    outs = [None] * nc
    for ci in (range(nc - 1, -1, -1) if reverse else range(nc)):
        sl = slice(ci * B_CHUNK, (ci + 1) * B_CHUNK)
        qc, kc, vc, lc = q[sl], k[sl], v[sl], la[sl]
        b = cumsum_rows(lc, reverse)
        tot = jnp.sum(lc, axis=0, keepdims=True)
        q_t = qc * jnp.exp(b)
        k_t = kc * jnp.exp(-b)
        k_end = kc * jnp.exp(tot - b)
        attn = jnp.where(mask, dot_nt(q_t, k_t), 0.0)
        outs[ci] = dot_nn(attn, vc) + dot_nt(q_t, st)
        st = st * jnp.exp(tot) + dot_tn(vc, k_end)
    return jnp.concatenate(outs, axis=0), st


def _gla_post(o_f, o_b, r, gain):
    o = o_f + o_b
    heads = [_rms(o[:, h * B_VAL_DIM:(h + 1) * B_VAL_DIM], gain[:, h * B_VAL_DIM:(h + 1) * B_VAL_DIM]) for h in range(B_HEADS)]
    return jnp.concatenate(heads, axis=1) * (r * _sigmoid(r))


def _gate(z, wg, bias):
    return _log_sigmoid(dot_nn(z, wg) + bias) / B_GATE_TAU


def _swiglu_act(g, u):
    return (g * _sigmoid(g)) * u


def _adamw(w, g, m, v):
    m = ADAM_B1 * m + (1.0 - ADAM_B1) * g
    v = ADAM_B2 * v + (1.0 - ADAM_B2) * jnp.square(g)
    m_hat = m / (1.0 - ADAM_B1 ** ADAM_STEP)
    v_hat = v / (1.0 - ADAM_B2 ** ADAM_STEP)
    delta = -ADAM_LR * (m_hat / (jnp.sqrt(v_hat) + ADAM_EPS) + ADAM_WD * w)
    return delta, m, v


def _params(sem=None):
    return pltpu.CompilerParams(dimension_semantics=sem, vmem_limit_bytes=V7X_VMEM_LIMIT)


def _row_tile(s, want=512):
    t = min(want, s)
    assert s % t == 0
    return t


def _matmul(name, a, b, *, dims, grid, a_spec, b_spec, o_spec, out_shape, red_axis=None, res=None, res_spec=None):
    def body(*refs):
        a_ref, b_ref = refs[0], refs[1]
        r_ref = refs[2] if res is not None else None
        o_ref = refs[3] if res is not None else refs[2]
        prod = lax.dot_general(a_ref[...].astype(MXU_DTYPE), b_ref[...].astype(MXU_DTYPE), (dims, ((), ())),
                               preferred_element_type=F32)
        if red_axis is None:
            if r_ref is not None:
                prod = prod + r_ref[...]
            o_ref[...] = prod.astype(o_ref.dtype)
        else:
            k = pl.program_id(red_axis)

            @pl.when(k == 0)
            def _():
                o_ref[...] = prod + r_ref[...] if r_ref is not None else prod

            @pl.when(k > 0)
            def _():
                o_ref[...] += prod

    ins = [a, b] + ([res] if res is not None else [])
    specs = [a_spec, b_spec] + ([res_spec] if res is not None else [])
    sem = tuple("arbitrary" if i == red_axis else "parallel" for i in range(len(grid)))
    return pl.pallas_call(body, name=name, grid=grid, in_specs=specs, out_specs=o_spec, out_shape=out_shape,
                          compiler_params=_params(sem))(*ins)


def mm_nn(name, x, w, *, res=None, out_dtype=F32, tn=1024):
    m, k = x.shape
    n = w.shape[1]
    tm, tn = _row_tile(m), min(tn, n)
    return _matmul(name, x, w, dims=NN, grid=(n // tn, m // tm),
                   a_spec=pl.BlockSpec((tm, k), lambda j, i: (i, 0)), b_spec=pl.BlockSpec((k, tn), lambda j, i: (0, j)),
                   o_spec=pl.BlockSpec((tm, tn), lambda j, i: (i, j)), out_shape=jax.ShapeDtypeStruct((m, n), out_dtype),
                   res=res, res_spec=pl.BlockSpec((tm, tn), lambda j, i: (i, j)))


def mm_nt(name, dy, w, *, res=None, tn=1024):
    m, n = dy.shape
    k = w.shape[0]
    tm, tn = _row_tile(m), min(tn, n)
    return _matmul(name, dy, w, dims=NT, grid=(m // tm, n // tn), red_axis=1,
                   a_spec=pl.BlockSpec((tm, tn), lambda i, j: (i, j)), b_spec=pl.BlockSpec((k, tn), lambda i, j: (0, j)),
                   o_spec=pl.BlockSpec((tm, k), lambda i, j: (i, 0)), out_shape=jax.ShapeDtypeStruct((m, k), F32),
                   res=res, res_spec=pl.BlockSpec((tm, k), lambda i, j: (i, 0)))


def mm_tn(name, x, dy, *, tn=1024):
    m, k = x.shape
    n = dy.shape[1]
    tm, tn = _row_tile(m), min(tn, n)
    return _matmul(name, x, dy, dims=TN, grid=(n // tn, m // tm), red_axis=1,
                   a_spec=pl.BlockSpec((tm, k), lambda j, i: (i, 0)), b_spec=pl.BlockSpec((tm, tn), lambda j, i: (i, j)),
                   o_spec=pl.BlockSpec((k, tn), lambda j, i: (0, j)), out_shape=jax.ShapeDtypeStruct((k, n), F32))


def _rows_call(name, body, ins, outs, s, tile):
    in_specs = []
    for a, kind in ins:
        if kind == "row":
            in_specs.append(pl.BlockSpec((tile, a.shape[1]), lambda i: (i, 0)))
        else:
            in_specs.append(pl.BlockSpec(a.shape, lambda i, nd=a.ndim: (0,) * nd))
    out_specs, out_shape = [], []
    for cols, dt, kind in outs:
        if kind == "row":
            out_specs.append(pl.BlockSpec((tile, cols), lambda i: (i, 0)))
            out_shape.append(jax.ShapeDtypeStruct((s, cols), dt))
        else:
            out_specs.append(pl.BlockSpec((8, cols), lambda i: (0, 0)))
            out_shape.append(jax.ShapeDtypeStruct((8, cols), dt))
    has_acc = any(kind == "acc" for _, _, kind in outs)
    return pl.pallas_call(body, name=name, grid=(s // tile,), in_specs=in_specs, out_specs=out_specs, out_shape=out_shape,
                          compiler_params=_params(("arbitrary",) if has_acc else ("parallel",)))(*[a for a, _ in ins])


def _acc_rows(ref, val):
    @pl.when(pl.program_id(0) == 0)
    def _():
        ref[...] = jnp.zeros_like(ref)

    ref[...] += jnp.broadcast_to(val, ref.shape)


def rmsnorm_fwd(name, h, gain):
    s = h.shape[0]

    def body(h_ref, g_ref, o_ref):
        o_ref[...] = _rms(h_ref[...], g_ref[...]).astype(o_ref.dtype)

    return _rows_call(name, body, [(h, "row"), (gain, "full")], [(D_MODEL, BF16, "row")], s, _row_tile(s))[0]


def rmsnorm_bwd(name, h, gain, d_hn, d_res):
    s = h.shape[0]

    def body(h_ref, g_ref, dy_ref, dr_ref, dh_ref, dg_ref):
        _, vjp = jax.vjp(_rms, h_ref[...], g_ref[...])
        dh, dg = vjp(dy_ref[...])
        dh_ref[...] = dh + dr_ref[...]
        _acc_rows(dg_ref, dg)

    return _rows_call(name, body, [(h, "row"), (gain, "full"), (d_hn, "row"), (d_res, "row")],
                      [(D_MODEL, F32, "row"), (D_MODEL, F32, "acc")], s, _row_tile(s))


def loss_and_grad(y, target):
    s = y.shape[0]
    tile = _row_tile(s)

    def body(y_ref, t_ref, dy_ref, acc_ref):
        diff = y_ref[...] - t_ref[...]
        dy_ref[...] = diff * (1.0 / D_MODEL)

        @pl.when(pl.program_id(0) == 0)
        def _():
            acc_ref[...] = jnp.zeros_like(acc_ref)

        acc_ref[...] += jnp.sum((diff * diff).reshape(tile // 8, 8, D_MODEL), axis=0)

    return _rows_call("loss_head", body, [(y, "row"), (target, "row")], [(D_MODEL, F32, "row"), (D_MODEL, F32, "acc")], s, tile)


def _rope_table(s):
    half = A_HEAD_DIM // 2
    inv_freq = ROPE_THETA ** (-jnp.arange(half, dtype=F32) / half)
    ang = jnp.arange(s).astype(F32)[:, None] * inv_freq[None, :]
    cos, sin = jnp.cos(ang), jnp.sin(ang)
    return jnp.concatenate([cos, cos, cos, cos, -sin, sin, -sin, sin], axis=1)


def _attn_geometry(s, dil):
    length = s // dil
    tile = min(256, length)
    assert length % tile == 0 and tile % A_HALF == 0
    return length, tile, tile // A_HALF, length // A_HALF


def _attn_in_specs(grp, dil, tile, r, nb):
    cq, ck, cv = 3 * grp, 3 * grp + 1, 3 * grp + 2

    def main(col, width=D_MODEL, per=9):
        return pl.BlockSpec((tile, width), lambda p, i: (i, p * per + col))

    def prev(col, width=D_MODEL, per=9):
        return pl.BlockSpec((A_HALF, width), lambda p, i: (jnp.maximum(i * r - 1, 0), p * per + col))

    def nxt(col, width=D_MODEL, per=9):
        return pl.BlockSpec((A_HALF, width), lambda p, i: (jnp.minimum(i * r + r, nb - 1), p * per + col))

    return [main(cq), prev(ck), main(ck), nxt(ck), prev(cv), main(cv), nxt(cv),
            prev(0, 256, 1), main(0, 256, 1), nxt(0, 256, 1)]


def _attn_valid(i, tile, length):
    tq = i * tile + lax.broadcasted_iota(jnp.int32, (tile, 1), 0)
    tk = i * tile - A_HALF + lax.broadcasted_iota(jnp.int32, (1, tile + 2 * A_HALF), 1)
    return (tk >= 0) & (tk < length) & (jnp.abs(tk - tq) <= A_HALF)


def attn_fwd(name, qkv, tab, gq, gk, grp, dil):
    s = qkv.shape[0]
    length, tile, r, nb = _attn_geometry(s, dil)

    def body(q_ref, kp_ref, km_ref, kn_ref, vp_ref, vm_ref, vn_ref, tp_ref, tm_ref, tn_ref, gq_ref, gk_ref, o_ref, lse_ref):
        valid = _attn_valid(pl.program_id(1), tile, length)
        tk = jnp.concatenate([tp_ref[...], tm_ref[...], tn_ref[...]], axis=0)
        lse = jnp.zeros((tile, LANES), F32)
        for j in range(D_MODEL // LANES):
            sl = slice(j * LANES, (j + 1) * LANES)
            kr = jnp.concatenate([kp_ref[:, sl], km_ref[:, sl], kn_ref[:, sl]], axis=0)
            vv = jnp.concatenate([vp_ref[:, sl], vm_ref[:, sl], vn_ref[:, sl]], axis=0)
            o, lse_j = _attn_pair(q_ref[:, sl], kr, vv, tm_ref[...], tk, gq_ref[...], gk_ref[...], valid, j)
            o_ref[:, sl] = o
            lse = lse + lse_j
        lse_ref[...] = lse

    full = lambda a: pl.BlockSpec(a.shape, lambda p, i: (0, 0))
    x = qkv.reshape(length, dil * qkv.shape[1])
    t = tab.reshape(length, dil * 256)
    o, lse = pl.pallas_call(
        body, name=name, grid=(dil, length // tile),
        in_specs=_attn_in_specs(grp, dil, tile, r, nb) + [full(gq), full(gk)],
        out_specs=[pl.BlockSpec((tile, D_MODEL), lambda p, i: (i, p)), pl.BlockSpec((tile, LANES), lambda p, i: (i, p))],
        out_shape=[jax.ShapeDtypeStruct((length, dil * D_MODEL), F32), jax.ShapeDtypeStruct((length, dil * LANES), F32)],
        compiler_params=_params(("parallel", "parallel")),
    )(x, x, x, x, x, x, x, t, t, t, gq, gk)
    return o.reshape(s, D_MODEL), lse.reshape(s, LANES)


def attn_bwd(name, qkv, tab, gq, gk, d_o, d_lse, grp, dil):
    s = qkv.shape[0]
    length, tile, r, nb = _attn_geometry(s, dil)
    nt = length // tile

    def body(q_ref, kp_ref, km_ref, kn_ref, vp_ref, vm_ref, vn_ref, tp_ref, tm_ref, tn_ref, gq_ref, gk_ref, do_ref, dl_ref,
             dq_ref, dkp_ref, dkm_ref, dkn_ref, dvp_ref, dvm_ref, dvn_ref, dgq_ref, dgk_ref):
        valid = _attn_valid(pl.program_id(1), tile, length)
        tq = tm_ref[...]
        tk = jnp.concatenate([tp_ref[...], tq, tn_ref[...]], axis=0)
        d_lse_slab = dl_ref[...]
        dgq = jnp.zeros((1, LANES), F32)
        dgk = jnp.zeros((1, LANES), F32)
        for j in range(D_MODEL // LANES):
            sl = slice(j * LANES, (j + 1) * LANES)
            kr = jnp.concatenate([kp_ref[:, sl], km_ref[:, sl], kn_ref[:, sl]], axis=0)
            vv = jnp.concatenate([vp_ref[:, sl], vm_ref[:, sl], vn_ref[:, sl]], axis=0)
            pair = functools.partial(_attn_pair, tq=tq, tk=tk, valid=valid, pair=j)
            _, vjp = jax.vjp(lambda a, b, c, d, e: pair(a, b, c, gq=d, gk=e), q_ref[:, sl], kr, vv, gq_ref[...], gk_ref[...])
            dq, dk, dv, dgq_j, dgk_j = vjp((do_ref[:, sl], d_lse_slab))
            dq_ref[:, sl] = dq
            dkp_ref[:, sl] = dk[:A_HALF]
            dkm_ref[:, sl] = dk[A_HALF:A_HALF + tile]
            dkn_ref[:, sl] = dk[A_HALF + tile:]
            dvp_ref[:, sl] = dv[:A_HALF]
            dvm_ref[:, sl] = dv[A_HALF:A_HALF + tile]
            dvn_ref[:, sl] = dv[A_HALF + tile:]
            dgq = dgq + dgq_j
            dgk = dgk + dgk_j

        @pl.when((pl.program_id(0) == 0) & (pl.program_id(1) == 0))
        def _():
            dgq_ref[...] = jnp.zeros_like(dgq_ref)
            dgk_ref[...] = jnp.zeros_like(dgk_ref)

        dgq_ref[...] += jnp.broadcast_to(dgq + pltpu.roll(dgq, A_HEAD_DIM, 1), dgq_ref.shape)
        dgk_ref[...] += jnp.broadcast_to(dgk + pltpu.roll(dgk, A_HEAD_DIM, 1), dgk_ref.shape)

    full = lambda a: pl.BlockSpec(a.shape, lambda p, i: (0, 0))
    x = qkv.reshape(length, dil * qkv.shape[1])
    t = tab.reshape(length, dil * 256)
    view = lambda a: a.reshape(length, dil * a.shape[1])
    main_o = pl.BlockSpec((None, tile, D_MODEL), lambda p, i: (p, i, 0))
    edge_o = pl.BlockSpec((None, None, A_HALF, D_MODEL), lambda p, i: (p, i, 0, 0))
    main_s = jax.ShapeDtypeStruct((dil, length, D_MODEL), F32)
    edge_s = jax.ShapeDtypeStruct((dil, nt, A_HALF, D_MODEL), F32)
    acc_o = pl.BlockSpec((8, LANES), lambda p, i: (0, 0))
    acc_s = jax.ShapeDtypeStruct((8, LANES), F32)
    outs = pl.pallas_call(
        body, name=name, grid=(dil, nt),
        in_specs=_attn_in_specs(grp, dil, tile, r, nb) + [full(gq), full(gk)]
        + [pl.BlockSpec((tile, D_MODEL), lambda p, i: (i, p)), pl.BlockSpec((tile, LANES), lambda p, i: (i, p))],
        out_specs=[pl.BlockSpec((tile, D_MODEL), lambda p, i: (i, p)), edge_o, main_o, edge_o, edge_o, main_o, edge_o, acc_o, acc_o],
        out_shape=[jax.ShapeDtypeStruct((length, dil * D_MODEL), F32), edge_s, main_s, edge_s, edge_s, main_s, edge_s, acc_s, acc_s],
        compiler_params=_params(("arbitrary", "arbitrary")),
    )(x, x, x, x, x, x, x, t, t, t, gq, gk, view(d_o), view(d_lse))
    dq, dkp, dkm, dkn, dvp, dvm, dvn, dgq, dgk = outs
    return dq.reshape(s, D_MODEL), (dkp, dkm, dkn), (dvp, dvm, dvn), dgq, dgk


def attn_combine(name, parts, s, dil):
    prev_part, main_part, next_part = parts
    length, tile, r, nb = _attn_geometry(s, dil)
    nt = length // tile

    def body(m_ref, from_prev_ref, from_next_ref, o_ref):
        i = pl.program_id(1)
        o_ref[...] = m_ref[...]
        head = o_ref[0:A_HALF, :] + jnp.where(i > 0, from_prev_ref[...], 0.0)
        o_ref[0:A_HALF, :] = head
        tail = o_ref[tile - A_HALF:tile, :] + jnp.where(i < nt - 1, from_next_ref[...], 0.0)
        o_ref[tile - A_HALF:tile, :] = tail

    out = pl.pallas_call(
        body, name=name, grid=(dil, nt),
        in_specs=[pl.BlockSpec((None, tile, D_MODEL), lambda p, i: (p, i, 0)),
                  pl.BlockSpec((None, None, A_HALF, D_MODEL), lambda p, i: (p, jnp.maximum(i - 1, 0), 0, 0)),
                  pl.BlockSpec((None, None, A_HALF, D_MODEL), lambda p, i: (p, jnp.minimum(i + 1, nt - 1), 0, 0))],
        out_specs=pl.BlockSpec((tile, D_MODEL), lambda p, i: (i, p)),
        out_shape=jax.ShapeDtypeStruct((length, dil * D_MODEL), F32),
        compiler_params=_params(("parallel", "parallel")),
    )(main_part, next_part, prev_part)
    return out.reshape(s, D_MODEL)


def attn_merge_fwd(name, os_, lses):
    s = os_[0].shape[0]

    def body(o0, o1, o2, l0, l1, l2, out_ref):
        out_ref[...] = _merge_groups(o0[...], o1[...], o2[...], l0[...], l1[...], l2[...]).astype(out_ref.dtype)

    return _rows_call(name, body, [(a, "row") for a in (*os_, *lses)], [(D_MODEL, BF16, "row")], s, _row_tile(s, 256))[0]


def attn_merge_bwd(name, os_, lses, d_out):
    s = os_[0].shape[0]

    def body(o0, o1, o2, l0, l1, l2, d_ref, *outs):
        _, vjp = jax.vjp(_merge_groups, o0[...], o1[...], o2[...], l0[...], l1[...], l2[...])
        for ref, val in zip(outs, vjp(d_ref[...])):
            ref[...] = val

    return _rows_call(name, body, [(a, "row") for a in (*os_, *lses, d_out)],
                      [(D_MODEL, F32, "row")] * 3 + [(LANES, F32, "row")] * 3, s, _row_tile(s, 256))


def gla_gate_fwd(name, z, wg, bias):
    s = z.shape[0]

    def body(z_ref, w_ref, b_ref, o_ref):
        o_ref[...] = _gate(z_ref[...], w_ref[...], b_ref[...])

    return _rows_call(name, body, [(z, "row"), (wg, "full"), (bias, "full")], [(D_MODEL, F32, "row")], s, _row_tile(s))[0]


def gla_gate_bwd(name, z, wg, bias, d_la_f, d_la_b):
    s = z.shape[0]
    tile = _row_tile(s)

    def body(z_ref, w_ref, b_ref, df_ref, db_ref, dz_ref, dw_ref, dbias_ref):
        _, vjp = jax.vjp(_gate, z_ref[...], w_ref[...], b_ref[...])
        dz, dw, dbias = vjp(jnp.concatenate([df_ref[...], db_ref[...]], axis=1))
        dz_ref[...] = dz

        @pl.when(pl.program_id(0) == 0)
        def _():
            dw_ref[...] = jnp.zeros_like(dw_ref)

        dw_ref[...] += dw
        _acc_rows(dbias_ref, dbias)

    return pl.pallas_call(
        body, name=name, grid=(s // tile,),
        in_specs=[pl.BlockSpec((tile, LANES), lambda i: (i, 0)), pl.BlockSpec(wg.shape, lambda i: (0, 0)),
                  pl.BlockSpec(bias.shape, lambda i: (0, 0)), pl.BlockSpec((tile, 512), lambda i: (i, 0)),
                  pl.BlockSpec((tile, 512), lambda i: (i, 0))],
        out_specs=[pl.BlockSpec((tile, LANES), lambda i: (i, 0)), pl.BlockSpec(wg.shape, lambda i: (0, 0)),
                   pl.BlockSpec((8, D_MODEL), lambda i: (0, 0))],
        out_shape=[jax.ShapeDtypeStruct((s, LANES), F32), jax.ShapeDtypeStruct(wg.shape, F32), jax.ShapeDtypeStruct((8, D_MODEL), F32)],
        compiler_params=_params(("arbitrary",)),
    )(z, wg, bias, d_la_f, d_la_b)


def _gla_in_specs(tile, order, la_col0):
    t = order
    return [pl.BlockSpec((tile, B_KEY_DIM), lambda h, n: (t(n), h)),
            pl.BlockSpec((tile, B_KEY_DIM), lambda h, n: (t(n), B_HEADS + h)),
            pl.BlockSpec((tile, B_VAL_DIM), lambda h, n: (t(n), B_HEADS + h)),
            pl.BlockSpec((tile, B_KEY_DIM), lambda h, n: (t(n), la_col0 + h))]


def gla_fwd(name, proj, la, reverse):
    s = proj.shape[0]
    tile = _row_tile(s)
    nt = s // tile
    order = (lambda n: nt - 1 - n) if reverse else (lambda n: n)

    def body(q_ref, k_ref, v_ref, la_ref, o_ref, st_ref, st_scr):
        @pl.when(pl.program_id(1) == 0)
        def _():
            st_scr[...] = jnp.zeros_like(st_scr)

        st_ref[...] = st_scr[...]
        o, st = _gla_tile(q_ref[...], k_ref[...], v_ref[...], la_ref[...], st_scr[...], reverse)
        o_ref[...] = o
        st_scr[...] = st

    return pl.pallas_call(
        body, name=name, grid=(B_HEADS, nt), in_specs=_gla_in_specs(tile, order, B_HEADS if reverse else 0),
        out_specs=[pl.BlockSpec((tile, B_VAL_DIM), lambda h, n: (order(n), h)),
                   pl.BlockSpec((None, None, B_VAL_DIM, B_KEY_DIM), lambda h, n: (h, order(n), 0, 0))],
        out_shape=[jax.ShapeDtypeStruct((s, D_MODEL), F32), jax.ShapeDtypeStruct((B_HEADS, nt, B_VAL_DIM, B_KEY_DIM), F32)],
        scratch_shapes=[pltpu.VMEM((B_VAL_DIM, B_KEY_DIM), F32)],
        compiler_params=_params(("parallel", "arbitrary")),
    )(proj, proj, proj, la)


def gla_bwd(name, proj, la, states, d_o, reverse, prev=None):
    s = proj.shape[0]
    tile = _row_tile(s)
    nt = s // tile
    order = (lambda n: n) if reverse else (lambda n: nt - 1 - n)

    def body(*refs):
        q_ref, k_ref, v_ref, la_ref, st_ref, do_ref = refs[:6]
        rest = refs[6:]
        prev_refs = rest[:3] if prev is not None else None
        dq_ref, dk_ref, dv_ref, dla_ref, dst_scr = rest[3:] if prev is not None else rest

        @pl.when(pl.program_id(1) == 0)
        def _():
            dst_scr[...] = jnp.zeros_like(dst_scr)

        _, vjp = jax.vjp(functools.partial(_gla_tile, reverse=reverse), q_ref[...], k_ref[...], v_ref[...], la_ref[...], st_ref[...])
        dq, dk, dv, dla, dst = vjp((do_ref[...], dst_scr[...]))
        if prev_refs is not None:
            dq, dk, dv = dq + prev_refs[0][...], dk + prev_refs[1][...], dv + prev_refs[2][...]
        dq_ref[...], dk_ref[...], dv_ref[...], dla_ref[...] = dq, dk, dv, dla
        dst_scr[...] = dst

    key_spec = pl.BlockSpec((tile, B_KEY_DIM), lambda h, n: (order(n), h))
    val_spec = pl.BlockSpec((tile, B_VAL_DIM), lambda h, n: (order(n), h))
    in_specs = _gla_in_specs(tile, order, B_HEADS if reverse else 0) + [
        pl.BlockSpec((None, None, B_VAL_DIM, B_KEY_DIM), lambda h, n: (h, order(n), 0, 0)), val_spec]
    ins = [proj, proj, proj, la, states, d_o]
    if prev is not None:
        in_specs += [key_spec, key_spec, val_spec]
        ins += list(prev)
    return pl.pallas_call(
        body, name=name, grid=(B_HEADS, nt), in_specs=in_specs,
        out_specs=[key_spec, key_spec, val_spec, key_spec],
        out_shape=[jax.ShapeDtypeStruct((s, 512), F32), jax.ShapeDtypeStruct((s, 512), F32),
                   jax.ShapeDtypeStruct((s, D_MODEL), F32), jax.ShapeDtypeStruct((s, 512), F32)],
        scratch_shapes=[pltpu.VMEM((B_VAL_DIM, B_KEY_DIM), F32)],
        compiler_params=_params(("parallel", "arbitrary")),
    )(*ins)


def _r_spec(tile):
    return pl.BlockSpec((tile, D_MODEL), lambda i: (i, 2))


def gla_post_fwd(name, o_f, o_b, proj, gain):
    s = o_f.shape[0]
    tile = _row_tile(s)

    def body(of_ref, ob_ref, r_ref, g_ref, out_ref):
        out_ref[...] = _gla_post(of_ref[...], ob_ref[...], r_ref[...], g_ref[...]).astype(out_ref.dtype)

    row = pl.BlockSpec((tile, D_MODEL), lambda i: (i, 0))
    return pl.pallas_call(body, name=name, grid=(s // tile,),
                          in_specs=[row, row, _r_spec(tile), pl.BlockSpec(gain.shape, lambda i: (0, 0))], out_specs=row,
                          out_shape=jax.ShapeDtypeStruct((s, D_MODEL), BF16), compiler_params=_params(("parallel",)))(o_f, o_b, proj, gain)


def gla_post_bwd(name, o_f, o_b, proj, gain, d_out):
    s = o_f.shape[0]
    tile = _row_tile(s)

    def body(of_ref, ob_ref, r_ref, g_ref, d_ref, do_ref, dr_ref, dg_ref):
        _, vjp = jax.vjp(_gla_post, of_ref[...], ob_ref[...], r_ref[...], g_ref[...])
        d_of, _, dr, dg = vjp(d_ref[...])
        do_ref[...] = d_of
        dr_ref[...] = dr
        _acc_rows(dg_ref, dg)

    row = pl.BlockSpec((tile, D_MODEL), lambda i: (i, 0))
    return pl.pallas_call(
        body, name=name, grid=(s // tile,),
        in_specs=[row, row, _r_spec(tile), pl.BlockSpec(gain.shape, lambda i: (0, 0)), row],
        out_specs=[row, row, pl.BlockSpec((8, D_MODEL), lambda i: (0, 0))],
        out_shape=[jax.ShapeDtypeStruct((s, D_MODEL), F32), jax.ShapeDtypeStruct((s, D_MODEL), F32), jax.ShapeDtypeStruct((8, D_MODEL), F32)],
        compiler_params=_params(("arbitrary",)))(o_f, o_b, proj, gain, d_out)


def _blk(tile, where):
    return pl.BlockSpec((None, tile, FFN_BLK), where)


def ffn_fwd(name, h_mid, hn2, w_gu, w_down, layer):
    s = hn2.shape[0]
    tm = _row_tile(s)
    nt = s // tm
    gu = _matmul(name + "_gu", hn2, w_gu, dims=NN, grid=(N_DEV, nt),
                 a_spec=pl.BlockSpec((tm, D_MODEL), lambda j, i: (i, 0)),
                 b_spec=pl.BlockSpec((None, None, D_MODEL, FFN_BLK), lambda j, i: (j, layer, 0, 0)),
                 o_spec=_blk(tm, lambda j, i: (j, i, 0)), out_shape=jax.ShapeDtypeStruct((N_DEV, s, FFN_BLK), F32))

    def act_body(g_ref, u_ref, o_ref):
        o_ref[...] = _swiglu_act(g_ref[...], u_ref[...]).astype(o_ref.dtype)

    act = pl.pallas_call(act_body, name=name + "_act", grid=(4, nt),
                         in_specs=[_blk(tm, lambda j, i: (j, i, 0)), _blk(tm, lambda j, i: (j + 4, i, 0))],
                         out_specs=_blk(tm, lambda j, i: (j, i, 0)), out_shape=jax.ShapeDtypeStruct((4, s, FFN_BLK), BF16),
                         compiler_params=_params(("parallel", "parallel")))(gu, gu)
    row = pl.BlockSpec((tm, D_MODEL), lambda i, j: (i, 0))
    h_next = _matmul(name + "_down", act, w_down, dims=NN, grid=(nt, 4), red_axis=1,
                     a_spec=_blk(tm, lambda i, j: (j, i, 0)), b_spec=pl.BlockSpec((None, FFN_BLK, D_MODEL), lambda i, j: (j, 0, 0)),
                     o_spec=row, out_shape=jax.ShapeDtypeStruct((s, D_MODEL), F32), res=h_mid, res_spec=row)
    return h_next, gu, act


def ffn_bwd(name, dh_next, hn2, gu, act, w_gu, w_down, layer):
    s = hn2.shape[0]
    tm = _row_tile(s)
    nt = s // tm
    d_act = _matmul(name + "_dact", dh_next, w_down, dims=NT, grid=(4, nt),
                    a_spec=pl.BlockSpec((tm, D_MODEL), lambda j, i: (i, 0)),
                    b_spec=pl.BlockSpec((None, FFN_BLK, D_MODEL), lambda j, i: (j, 0, 0)),
                    o_spec=_blk(tm, lambda j, i: (j, i, 0)), out_shape=jax.ShapeDtypeStruct((4, s, FFN_BLK), F32))
    d_wd = _matmul(name + "_dwd", act, dh_next, dims=TN, grid=(4, nt), red_axis=1,
                   a_spec=_blk(tm, lambda j, i: (j, i, 0)), b_spec=pl.BlockSpec((tm, D_MODEL), lambda j, i: (i, 0)),
                   o_spec=pl.BlockSpec((None, FFN_BLK, D_MODEL), lambda j, i: (j, 0, 0)),
                   out_shape=jax.ShapeDtypeStruct((4, FFN_BLK, D_MODEL), F32))

    def dact_body(g_ref, u_ref, d_ref, o_ref):
        _, vjp = jax.vjp(_swiglu_act, g_ref[...], u_ref[...])
        dg, du = vjp(d_ref[...])
        o_ref[...] = jnp.where(pl.program_id(0) < 4, dg, du)

    d_gu = pl.pallas_call(dact_body, name=name + "_dgu", grid=(N_DEV, nt),
                          in_specs=[_blk(tm, lambda j, i: (j % 4, i, 0)), _blk(tm, lambda j, i: (j % 4 + 4, i, 0)),
                                    _blk(tm, lambda j, i: (j % 4, i, 0))],
                          out_specs=_blk(tm, lambda j, i: (j, i, 0)), out_shape=jax.ShapeDtypeStruct((N_DEV, s, FFN_BLK), F32),
                          compiler_params=_params(("parallel", "parallel")))(gu, gu, d_act)
    d_hn2 = _matmul(name + "_dx", d_gu, w_gu, dims=NT, grid=(nt, N_DEV), red_axis=1,
                    a_spec=_blk(tm, lambda i, j: (j, i, 0)),
                    b_spec=pl.BlockSpec((None, None, D_MODEL, FFN_BLK), lambda i, j: (j, layer, 0, 0)),
                    o_spec=pl.BlockSpec((tm, D_MODEL), lambda i, j: (i, 0)), out_shape=jax.ShapeDtypeStruct((s, D_MODEL), F32))
    d_wgu = _matmul(name + "_dwgu", hn2, d_gu, dims=TN, grid=(N_DEV, nt), red_axis=1,
                    a_spec=pl.BlockSpec((tm, D_MODEL), lambda j, i: (i, 0)), b_spec=_blk(tm, lambda j, i: (j, i, 0)),
                    o_spec=pl.BlockSpec((None, D_MODEL, FFN_BLK), lambda j, i: (j, 0, 0)),
                    out_shape=jax.ShapeDtypeStruct((N_DEV, D_MODEL, FFN_BLK), F32))
    return d_hn2, d_wgu, d_wd


def _my_place():
    return lax.axis_index("x"), lax.axis_index("y"), lax.axis_index("c")


def _flip(place, k):
    x, y, c = place
    return (1 - x if k & 4 else x, 1 - y if k & 2 else y, 1 - c if k & 1 else c)


def _index(place):
    return 4 * place[0] + 2 * place[1] + place[2]


def all_gather(arrs):
    n = len(arrs)

    def body(*refs):
        ins, outs = refs[:n], refs[n:2 * n]
        send_sems, recv_sems, local_sems = refs[2 * n:]
        me = _my_place()
        sibling = _flip(me, 1)
        chips = (4, 2, 6)

        def copy(a, k, block, to, src=None):
            dst = outs[a].at[_index(block)]
            return pltpu.make_async_remote_copy(src_ref=dst if src is None else src, dst_ref=dst, send_sem=send_sems.at[a, k],
                                                recv_sem=recv_sems.at[a, k], device_id=to, device_id_type=MESH)

        started = []
        for a in range(n):
            mine = pltpu.make_async_copy(ins[a], outs[a].at[_index(me)], local_sems.at[a])
            mine.start()
            started.append(mine)
        first = []
        for a in range(n):
            first.append(copy(a, 0, me, sibling, src=ins[a]))
            first += [copy(a, 1 + j, me, _flip(me, k), src=ins[a]) for j, k in enumerate(chips)]
        for cp in first:
            cp.start()
        passed = []
        for a in range(n):
            for j, k in enumerate(chips):
                copy(a, 1 + j, _flip(me, k), me).wait_recv()
                fwd = copy(a, 4 + j, _flip(me, k), sibling)
                fwd.start()
                passed.append(fwd)
        for a in range(n):
            copy(a, 0, sibling, me).wait_recv()
            for j, k in enumerate(chips):
                copy(a, 4 + j, _flip(sibling, k), me).wait_recv()
        for cp in first + passed:
            cp.wait_send()
        for cp in started:
            cp.wait()

    return pl.pallas_call(
        body, name="all_gather_weights", in_specs=[ANY] * n, out_specs=[ANY] * n,
        out_shape=[jax.ShapeDtypeStruct((N_DEV,) + a.shape, a.dtype) for a in arrs],
        scratch_shapes=[pltpu.SemaphoreType.DMA((n, 7)), pltpu.SemaphoreType.DMA((n, 7)), pltpu.SemaphoreType.DMA((n,))],
    )(*arrs)


def exchange_partials(g):
    def body(g_ref, out_ref, send_sems, recv_sems, local_sem):
        me = _my_place()
        mine = pltpu.make_async_copy(g_ref.at[_index(me)], out_ref.at[_index(me)], local_sem)
        mine.start()
        copies = []
        for k in range(1, N_DEV):
            peer = _flip(me, k)
            cp = pltpu.make_async_remote_copy(src_ref=g_ref.at[_index(peer)], dst_ref=out_ref.at[_index(me)],
                                              send_sem=send_sems.at[k - 1], recv_sem=recv_sems.at[k - 1],
                                              device_id=peer, device_id_type=MESH)
            cp.start()
            copies.append((cp, peer))
        for k, (cp, peer) in enumerate(copies):
            pltpu.make_async_remote_copy(src_ref=g_ref.at[_index(me)], dst_ref=out_ref.at[_index(peer)],
                                         send_sem=send_sems.at[k], recv_sem=recv_sems.at[k],
                                         device_id=peer, device_id_type=MESH).wait_recv()
        for cp, _ in copies:
            cp.wait_send()
        mine.wait()

    return pl.pallas_call(
        body, name="exchange_weight_grads", in_specs=[ANY], out_specs=ANY, out_shape=jax.ShapeDtypeStruct(g.shape, g.dtype),
        scratch_shapes=[pltpu.SemaphoreType.DMA((7,)), pltpu.SemaphoreType.DMA((7,)), pltpu.SemaphoreType.DMA],
    )(g)


def adamw_sharded(parts, w, m, v):
    rows = w.shape[0]
    tile = 48
    assert rows % tile == 0

    def body(p_ref, w_ref, m_ref, v_ref, g_ref, d_ref, nm_ref, nv_ref):
        g = p_ref[0].astype(F32)
        for src in range(1, N_DEV):
            g = g + p_ref[src].astype(F32)
        g_ref[...] = g
        d_ref[...], nm_ref[...], nv_ref[...] = _adamw(w_ref[...], g, m_ref[...], v_ref[...])

    row = pl.BlockSpec((tile, D_MODEL), lambda i: (i, 0))
    shape = jax.ShapeDtypeStruct((rows, D_MODEL), F32)
    return pl.pallas_call(body, name="adamw_sharded", grid=(rows // tile,),
                          in_specs=[pl.BlockSpec((N_DEV, tile, D_MODEL), lambda i: (0, i, 0)), row, row, row],
                          out_specs=[row] * 4, out_shape=[shape] * 4, compiler_params=_params(("parallel",)))(parts, w, m, v)


def allreduce_adamw_replicated(partial, w, m, v, n_loss_rows):
    rows = partial.shape[0]

    def body(p_ref, w_ref, m_ref, v_ref, g_ref, d_ref, nm_ref, nv_ref, loss_ref, recv_ref, send_sems, recv_sems):
        me = _my_place()
        recv_ref[_index(me)] = p_ref[...]
        copies = []
        for k in range(1, N_DEV):
            peer = _flip(me, k)
            cp = pltpu.make_async_remote_copy(src_ref=p_ref, dst_ref=recv_ref.at[_index(me)], send_sem=send_sems.at[k - 1],
                                              recv_sem=recv_sems.at[k - 1], device_id=peer, device_id_type=MESH)
            cp.start()
            copies.append((cp, peer))
        for k, (cp, peer) in enumerate(copies):
            pltpu.make_async_remote_copy(src_ref=p_ref, dst_ref=recv_ref.at[_index(peer)], send_sem=send_sems.at[k],
                                         recv_sem=recv_sems.at[k], device_id=peer, device_id_type=MESH).wait_recv()
        for cp, _ in copies:
            cp.wait_send()
        g = recv_ref[0]
        for src in range(1, N_DEV):
            g = g + recv_ref[src]
        g_ref[...] = g
        d_ref[...], nm_ref[...], nv_ref[...] = _adamw(w_ref[...], g, m_ref[...], v_ref[...])
        loss = (0.5 / D_MODEL) * jnp.sum(g[rows - n_loss_rows:, :])
        loss_ref[...] = jnp.full(loss_ref.shape, loss, F32)

    shape = jax.ShapeDtypeStruct((rows, LANES), F32)
    return pl.pallas_call(
        body, name="allreduce_adamw_replicated", in_specs=[VMEM_SPEC] * 4, out_specs=[VMEM_SPEC] * 5,
        out_shape=[shape] * 4 + [jax.ShapeDtypeStruct((8, LANES), F32)],
        scratch_shapes=[pltpu.VMEM((N_DEV, rows, LANES), F32), pltpu.SemaphoreType.DMA((7,)), pltpu.SemaphoreType.DMA((7,))],
    )(partial, w, m, v)


def _pack_rows(flat, cols):
    n = flat.shape[-1]
    rows = -(-n // cols)
    rows = -(-rows // 48) * 48
    flat = jnp.pad(flat, [(0, 0)] * (flat.ndim - 1) + [(0, rows * cols - n)])
    return flat.reshape(flat.shape[:-1] + (rows, cols))


def _unpack(flat, shapes):
    out, off = [], 0
    for shp in shapes:
        n = 1
        for d in shp:
            n *= d
        out.append(flat[off:off + n].reshape(shp))
        off += n
    return out


def _to_dev_cols(a):
    w = a.shape[-1] // N_DEV
    return jnp.moveaxis(a.reshape(a.shape[:-1] + (N_DEV, w)), -2, 0)


def _from_dev_cols(a):
    a = jnp.moveaxis(a, 0, -2)
    return a.reshape(a.shape[:-2] + (a.shape[-2] * a.shape[-1],))


def kernel(x, attn_norm, ffn_norm, a_w_in, a_q_norm, a_k_norm, a_w_out, b_w_in, b_w_gate_f, b_gate_bias_f, b_w_gate_b, b_gate_bias_b, b_out_norm, b_w_out, ffn_w_gate_up, ffn_w_down, loss_target, m_attn_norm, m_ffn_norm, m_a_w_in, m_a_q_norm, m_a_k_norm, m_a_w_out, m_b_w_in, m_b_w_gate_f, m_b_gate_bias_f, m_b_w_gate_b, m_b_gate_bias_b, m_b_out_norm, m_b_w_out, m_ffn_w_gate_up, m_ffn_w_down, v_attn_norm, v_ffn_norm, v_a_w_in, v_a_q_norm, v_a_k_norm, v_a_w_out, v_b_w_in, v_b_w_gate_f, v_b_gate_bias_f, v_b_w_gate_b, v_b_gate_bias_b, v_b_out_norm, v_b_w_out, v_ffn_w_gate_up, v_ffn_w_down):
    seq = x.shape[1]
    depth = attn_norm.shape[0]
    h = x.reshape(seq, D_MODEL)
    target = loss_target.reshape(seq, D_MODEL)
    n_a, n_b = a_w_in.shape[0], b_w_in.shape[0]

    small = jnp.concatenate([t.reshape(-1) for t in (b_w_gate_f, b_gate_bias_f, b_w_gate_b, b_gate_bias_b, b_out_norm)])
    small = _pack_rows(small, LANES)
    g_a_in, g_a_out, g_b_in, g_b_out, g_gu, g_down, g_small = all_gather(
        [a_w_in.astype(BF16), a_w_out.astype(BF16), b_w_in.astype(BF16), b_w_out.astype(BF16),
         ffn_w_gate_up.astype(BF16), ffn_w_down.astype(BF16), small])
    w_a_in = _from_dev_cols(g_a_in)
    w_a_out = jnp.moveaxis(g_a_out, 0, 1).reshape(n_a, D_MODEL, D_MODEL)
    w_b_in = _from_dev_cols(g_b_in)
    w_b_out = jnp.moveaxis(g_b_out, 0, 1).reshape(n_b, D_MODEL, D_MODEL)
    w_down = jnp.moveaxis(g_down, 0, 1).reshape(depth, 4, FFN_BLK, D_MODEL)
    small_shapes = [t.shape for t in (b_w_gate_f, b_gate_bias_f, b_w_gate_b, b_gate_bias_b, b_out_norm)]
    per_dev = [_unpack(g_small[d].reshape(-1), small_shapes) for d in range(N_DEV)]
    wgf, bgf, wgb, bgb, onorm = [_from_dev_cols(jnp.stack([per_dev[d][t] for d in range(N_DEV)])) for t in range(5)]
    w_gate = jnp.zeros((n_b, LANES, D_MODEL), F32)
    w_gate = w_gate.at[:, 0:16, 0:512].set(wgf).at[:, 16:32, 512:1024].set(wgb)
    gate_bias = jnp.concatenate([bgf, bgb], axis=1).reshape(n_b, 1, D_MODEL)
    out_gain = onorm.reshape(n_b, 1, D_MODEL)
    w_b_main = w_b_in[:, :, :3072]
    w_b_z = jnp.pad(w_b_in[:, :, 3072:], ((0, 0), (0, 0), (0, LANES - 32)))

    dh, sq_err, grads = _forward_backward(h, target, attn_norm, ffn_norm, a_q_norm, a_k_norm, w_a_in, w_a_out, w_b_main, w_b_z,
                                          w_gate, gate_bias, out_gain, w_b_out, g_gu, w_down)
    sharded_w = (a_w_in, a_w_out, b_w_in, b_w_gate_f, b_gate_bias_f, b_w_gate_b, b_gate_bias_b, b_out_norm, b_w_out, ffn_w_gate_up, ffn_w_down)
    sharded_m = (m_a_w_in, m_a_w_out, m_b_w_in, m_b_w_gate_f, m_b_gate_bias_f, m_b_w_gate_b, m_b_gate_bias_b, m_b_out_norm, m_b_w_out, m_ffn_w_gate_up, m_ffn_w_down)
    sharded_v = (v_a_w_in, v_a_w_out, v_b_w_in, v_b_w_gate_f, v_b_gate_bias_f, v_b_w_gate_b, v_b_gate_bias_b, v_b_out_norm, v_b_w_out, v_ffn_w_gate_up, v_ffn_w_down)
    rep_w = (attn_norm, ffn_norm, a_q_norm, a_k_norm)
    rep_m = (m_attn_norm, m_ffn_norm, m_a_q_norm, m_a_k_norm)
    rep_v = (v_attn_norm, v_ffn_norm, v_a_q_norm, v_a_k_norm)
    loss, outs = _reduce_and_update(grads, sq_err, sharded_w, sharded_m, sharded_v, rep_w, rep_m, rep_v)
    return (loss, dh.reshape(x.shape), *outs)


def _forward_backward(h, target, attn_norm, ffn_norm, a_q_norm, a_k_norm, w_a_in, w_a_out, w_b_main, w_b_z, w_gate, gate_bias,
                      out_gain, w_b_out, g_gu, w_down):
    seq = h.shape[0]
    depth = attn_norm.shape[0]
    n_a, n_b = w_a_in.shape[0], w_b_main.shape[0]
    tab = _rope_table(seq)
    pair_gain = lambda g: jnp.concatenate([g, g]).reshape(1, LANES)

    saved = []
    for i in range(depth):
        j = i // 2
        nm = f"l{i}"
        hn = rmsnorm_fwd(nm + "_norm1", h, attn_norm[i].reshape(1, D_MODEL))
        if i % 2 == 0:
            qkv = mm_nn(nm + "_qkv", hn, w_a_in[j])
            os_, lses = [], []
            for g, (_, dil) in enumerate(A_GROUPS):
                o, lse = attn_fwd(f"{nm}_attn{g}", qkv, tab, pair_gain(a_q_norm[j, g]), pair_gain(a_k_norm[j, g]), g, dil)
                os_.append(o)
                lses.append(lse)
            mixed = attn_merge_fwd(nm + "_merge", os_, lses)
            h_mid = mm_nn(nm + "_out", mixed, w_a_out[j], res=h)
            mix_saved = (qkv, os_, lses, mixed)
        else:
            proj = mm_nn(nm + "_proj", hn, w_b_main[j])
            z = mm_nn(nm + "_z", hn, w_b_z[j])
            la = gla_gate_fwd(nm + "_gate", z, w_gate[j], gate_bias[j])
            o_f, st_f = gla_fwd(nm + "_gla_f", proj, la, False)
            o_b, st_b = gla_fwd(nm + "_gla_b", proj, la, True)
            mixed = gla_post_fwd(nm + "_post", o_f, o_b, proj, out_gain[j])
            h_mid = mm_nn(nm + "_out", mixed, w_b_out[j], res=h)
            mix_saved = (proj, z, la, o_f, st_f, o_b, st_b, mixed)
        hn2 = rmsnorm_fwd(nm + "_norm2", h_mid, ffn_norm[i].reshape(1, D_MODEL))
        h_next, gu, act = ffn_fwd(nm + "_ffn", h_mid, hn2, g_gu, w_down[i], i)
        saved.append((h, hn, mix_saved, h_mid, hn2, gu, act))
        h = h_next

    dh, sq_err = loss_and_grad(h, target)

    g_attn_norm, g_ffn_norm = [None] * depth, [None] * depth
    g_a_w_in, g_a_w_out, g_a_q, g_a_k = [None] * n_a, [None] * n_a, [None] * n_a, [None] * n_a
    g_b_w_in, g_b_w_out, g_w_gate, g_gate_bias, g_out_gain = ([None] * n_b for _ in range(5))
    g_w_gu, g_w_down = [None] * depth, [None] * depth
    for i in reversed(range(depth)):
        j = i // 2
        nm = f"l{i}b"
        h_in, hn, mix_saved, h_mid, hn2, gu, act = saved[i]
        d_hn2, g_w_gu[i], g_w_down[i] = ffn_bwd(nm + "_ffn", dh, hn2, gu, act, g_gu, w_down[i], i)
        dh_mid, g_ffn_norm[i] = rmsnorm_bwd(nm + "_norm2", h_mid, ffn_norm[i].reshape(1, D_MODEL), d_hn2, dh)
        if i % 2 == 0:
            qkv, os_, lses, mixed = mix_saved
            d_mixed = mm_nt(nm + "_dmixed", dh_mid, w_a_out[j])
            g_a_w_out[j] = mm_tn(nm + "_dwout", mixed, dh_mid)
            d_parts = attn_merge_bwd(nm + "_merge", os_, lses, d_mixed)
            cols, gq_l, gk_l = [], [], []
            for g, (_, dil) in enumerate(A_GROUPS):
                dq, dk_parts, dv_parts, dgq, dgk = attn_bwd(f"{nm}_attn{g}", qkv, tab, pair_gain(a_q_norm[j, g]),
                                                            pair_gain(a_k_norm[j, g]), d_parts[g], d_parts[3 + g], g, dil)
                cols += [dq, attn_combine(f"{nm}_dk{g}", dk_parts, seq, dil), attn_combine(f"{nm}_dv{g}", dv_parts, seq, dil)]
                gq_l.append(dgq[0, :A_HEAD_DIM])
                gk_l.append(dgk[0, :A_HEAD_DIM])
            g_a_q[j], g_a_k[j] = jnp.stack(gq_l), jnp.stack(gk_l)
            d_qkv = jnp.concatenate(cols, axis=1)
            d_hn = mm_nt(nm + "_dhn", d_qkv, w_a_in[j])
            g_a_w_in[j] = mm_tn(nm + "_dwin", hn, d_qkv)
        else:
            proj, z, la, o_f, st_f, o_b, st_b, mixed = mix_saved
            d_mixed = mm_nt(nm + "_dmixed", dh_mid, w_b_out[j])
            g_b_w_out[j] = mm_tn(nm + "_dwout", mixed, dh_mid)
            d_o, d_r, dgain = gla_post_bwd(nm + "_post", o_f, o_b, proj, out_gain[j], d_mixed)
            g_out_gain[j] = dgain[0]
            dq, dk, dv, dla_f = gla_bwd(nm + "_gla_f", proj, la, st_f, d_o, False)
            dq, dk, dv, dla_b = gla_bwd(nm + "_gla_b", proj, la, st_b, d_o, True, prev=(dq, dk, dv))
            d_z, g_w_gate[j], dbias = gla_gate_bwd(nm + "_gate", z, w_gate[j], gate_bias[j], dla_f, dla_b)
            g_gate_bias[j] = dbias[0]
            d_proj = jnp.concatenate([dq, dk, dv, d_r], axis=1)
            d_hn = mm_nt(nm + "_dhn_z", d_z, w_b_z[j])
            d_hn = mm_nt(nm + "_dhn", d_proj, w_b_main[j], res=d_hn)
            g_b_w_in[j] = jnp.concatenate([mm_tn(nm + "_dwin", hn, d_proj), mm_tn(nm + "_dwz", hn, d_z)[:, :32]], axis=1)
        dh, g_attn_norm[i] = rmsnorm_bwd(nm + "_norm1", h_in, attn_norm[i].reshape(1, D_MODEL), d_hn, dh_mid)
    return dh, sq_err, (g_attn_norm, g_ffn_norm, g_a_w_in, g_a_w_out, g_a_q, g_a_k, g_b_w_in, g_b_w_out, g_w_gate, g_gate_bias,
                        g_out_gain, g_w_gu, g_w_down)


def _reduce_and_update(grads, sq_err, sharded_w, sharded_m, sharded_v, rep_w, rep_m, rep_v):
    (g_attn_norm, g_ffn_norm, g_a_w_in, g_a_w_out, g_a_q, g_a_k, g_b_w_in, g_b_w_out, g_w_gate, g_gate_bias, g_out_gain,
     g_w_gu, g_w_down) = grads
    depth, n_a, n_b = len(g_w_gu), len(g_a_w_in), len(g_b_w_in)

    g_w_gate = jnp.stack(g_w_gate)
    g_gate_bias = jnp.stack(g_gate_bias)
    by_dev = [
        _to_dev_cols(jnp.stack(g_a_w_in)),
        jnp.moveaxis(jnp.stack(g_a_w_out).reshape(n_a, N_DEV, -1, D_MODEL), 1, 0),
        _to_dev_cols(jnp.stack(g_b_w_in)),
        _to_dev_cols(g_w_gate[:, 0:16, 0:512]),
        _to_dev_cols(g_gate_bias[:, 0:512]),
        _to_dev_cols(g_w_gate[:, 16:32, 512:1024]),
        _to_dev_cols(g_gate_bias[:, 512:1024]),
        _to_dev_cols(jnp.stack(g_out_gain).reshape(n_b, B_HEADS, B_VAL_DIM)),
        jnp.moveaxis(jnp.stack(g_b_w_out).reshape(n_b, N_DEV, -1, D_MODEL), 1, 0),
        jnp.moveaxis(jnp.stack(g_w_gu), 1, 0),
        jnp.moveaxis(jnp.stack(g_w_down).reshape(depth, N_DEV, -1, D_MODEL), 1, 0),
    ]
    partials = _pack_rows(jnp.concatenate([t.reshape(N_DEV, -1) for t in by_dev], axis=1), D_MODEL).astype(BF16)
    received = exchange_partials(partials)
    pack = lambda ts: _pack_rows(jnp.concatenate([t.reshape(-1) for t in ts]), D_MODEL)
    packed = adamw_sharded(received, pack(sharded_w), pack(sharded_m), pack(sharded_v))
    shapes = [t.shape for t in sharded_w]
    sh_grad, sh_delta, sh_m, sh_v = [_unpack(p.reshape(-1), shapes) for p in packed]

    rep_g = (jnp.stack([t[0] for t in g_attn_norm]), jnp.stack([t[0] for t in g_ffn_norm]), jnp.stack(g_a_q), jnp.stack(g_a_k))
    n_rep = sum(t.size for t in rep_w)
    n_rep_rows = -(-n_rep // (8 * LANES)) * 8
    n_loss_rows = 8 * D_MODEL // LANES

    def pack_rep(ts, tail):
        flat = jnp.concatenate([t.reshape(-1) for t in ts])
        flat = jnp.pad(flat, (0, n_rep_rows * LANES - n_rep))
        return jnp.concatenate([flat.reshape(n_rep_rows, LANES), tail], axis=0)

    zeros_tail = jnp.zeros((n_loss_rows, LANES), F32)
    rep_out = allreduce_adamw_replicated(pack_rep(rep_g, sq_err.reshape(n_loss_rows, LANES)), pack_rep(rep_w, zeros_tail),
                                         pack_rep(rep_m, zeros_tail), pack_rep(rep_v, zeros_tail + 1.0), n_loss_rows)
    rep_shapes = [t.shape for t in rep_w]
    r_grad, r_delta, r_m, r_v = [_unpack(p.reshape(-1), rep_shapes) for p in rep_out[:4]]
    loss = rep_out[4][0, 0]

    def ordered(rep, sh):
        return [rep[0], rep[1], sh[0], rep[2], rep[3]] + list(sh[1:])

    return loss, (*ordered(r_grad, sh_grad), *ordered(r_delta, sh_delta), *ordered(r_m, sh_m), *ordered(r_v, sh_v))
```

```python
import functools

import jax
import jax.numpy as jnp
from jax import lax
from jax.experimental import pallas as pl
from jax.experimental.pallas import tpu as pltpu

F32 = jnp.float32
BF16 = jnp.bfloat16
MXU_DTYPE = jnp.bfloat16

D_MODEL = 1024
N_DEV = 8
RMS_EPS = 1e-6
NEG_INF = -1e30
A_GROUPS = ((128, 1), (512, 4), (2048, 16))
A_HEAD_DIM = 64
A_HALF = 64
B_HEADS = 4
B_KEY_DIM = 128
B_VAL_DIM = 256
B_CHUNK = 64
B_GATE_TAU = 16.0
FFN_HIDDEN = 2816
FFN_BLK = 2 * FFN_HIDDEN // N_DEV
ADAM_LR, ADAM_B1, ADAM_B2, ADAM_EPS, ADAM_WD, ADAM_STEP = 0.001, 0.9, 0.999, 1e-08, 0.01, 10
ROPE_THETA = 10000.0

V7X_VMEM_LIMIT = 56 * 1024 * 1024
LANES = 128
MESH = pl.DeviceIdType.MESH
ANY = pl.BlockSpec(memory_space=pl.ANY)
VMEM_SPEC = pl.BlockSpec(memory_space=pltpu.VMEM)

NN = ((1,), (0,))
NT = ((1,), (1,))
TN = ((0,), (0,))


def _dot(a, b, dims):
    return lax.dot_general(a.astype(MXU_DTYPE), b.astype(MXU_DTYPE), (dims, ((), ())), preferred_element_type=F32)


@jax.custom_vjp
def dot_nn(a, b):
    return _dot(a, b, NN)


@jax.custom_vjp
def dot_nt(a, b):
    return _dot(a, b, NT)


@jax.custom_vjp
def dot_tn(a, b):
    return _dot(a, b, TN)


dot_nn.defvjp(lambda a, b: (_dot(a, b, NN), (a, b)), lambda r, g: (dot_nt(g, r[1]), dot_tn(r[0], g)))
dot_nt.defvjp(lambda a, b: (_dot(a, b, NT), (a, b)), lambda r, g: (dot_nn(g, r[1]), dot_tn(g, r[0])))
dot_tn.defvjp(lambda a, b: (_dot(a, b, TN), (a, b)), lambda r, g: (dot_nt(r[1], g), dot_nn(r[0], g)))


def _dot_f32(a, b):
    return lax.dot_general(a, b, (NN, ((), ())), precision=lax.Precision.HIGHEST, preferred_element_type=F32)


def _tri(n, upper):
    r = lax.broadcasted_iota(jnp.int32, (n, n), 0)
    c = lax.broadcasted_iota(jnp.int32, (n, n), 1)
    return jnp.where((c >= r) if upper else (c <= r), 1.0, 0.0).astype(F32)


@functools.partial(jax.custom_vjp, nondiff_argnums=(1,))
def cumsum_rows(x, reverse):
    return _dot_f32(_tri(x.shape[0], reverse), x)


cumsum_rows.defvjp(
    lambda x, reverse: (_dot_f32(_tri(x.shape[0], reverse), x), None),
    lambda reverse, _, g: (_dot_f32(_tri(g.shape[0], not reverse), g),),
)


def _head_expand_matrix(transposed):
    shape = (D_MODEL, LANES) if transposed else (LANES, D_MODEL)
    h = lax.broadcasted_iota(jnp.int32, shape, 1 if transposed else 0)
    c = lax.broadcasted_iota(jnp.int32, shape, 0 if transposed else 1)
    return jnp.where(c // A_HEAD_DIM == h, 1.0, 0.0).astype(F32)


@jax.custom_vjp
def expand_heads(a):
    return _dot_f32(a, _head_expand_matrix(False))


expand_heads.defvjp(
    lambda a: (_dot_f32(a, _head_expand_matrix(False)), None),
    lambda _, g: (_dot_f32(g, _head_expand_matrix(True)),),
)


def _swap32_raw(x):
    lane = lax.broadcasted_iota(jnp.int32, x.shape, 1)
    return jnp.where((lane % 64) < 32, pltpu.roll(x, 96, 1), pltpu.roll(x, 32, 1))


@jax.custom_vjp
def swap32(x):
    return _swap32_raw(x)


swap32.defvjp(lambda x: (_swap32_raw(x), None), lambda _, g: (_swap32_raw(g),))


def _rms(x, gain):
    return x * lax.rsqrt(jnp.mean(x * x, axis=-1, keepdims=True) + RMS_EPS) * gain


def _sigmoid(x):
    return 1.0 / (1.0 + jnp.exp(-x))


def _log_sigmoid(x):
    return jnp.minimum(x, 0.0) - jnp.log(1.0 + jnp.exp(-jnp.abs(x)))


def _qk_prep(x, tab, gain):
    lo = lax.broadcasted_iota(jnp.int32, (1, LANES), 1) < A_HEAD_DIM
    x2 = x * x
    s_lo = jnp.sum(jnp.where(lo, x2, 0.0), axis=-1, keepdims=True)
    s_hi = jnp.sum(jnp.where(lo, 0.0, x2), axis=-1, keepdims=True)
    r = jnp.where(lo, lax.rsqrt(s_lo / A_HEAD_DIM + RMS_EPS), lax.rsqrt(s_hi / A_HEAD_DIM + RMS_EPS))
    xn = (x * r) * gain
    return xn * tab[:, :LANES] + swap32(xn) * tab[:, LANES:]


def _attn_pair(qr, kr, v, tq, tk, gq, gk, valid, pair):
    lane = lax.broadcasted_iota(jnp.int32, (1, LANES), 1)
    lo = lane < A_HEAD_DIM
    q = _qk_prep(qr, tq, gq)
    k = _qk_prep(kr, tk, gk)
    out = None
    lse_slab = None
    for par in range(2):
        m = lo if par == 0 else jnp.logical_not(lo)
        s = dot_nt(jnp.where(m, q, 0.0), k) * (A_HEAD_DIM ** -0.5)
        s = jnp.where(valid, s, NEG_INF)
        mx = lax.stop_gradient(jnp.max(s, axis=-1, keepdims=True))
        p = jnp.exp(s - mx)
        l = jnp.sum(p, axis=-1, keepdims=True)
        oh = dot_nn(p, jnp.where(m, v, 0.0)) / l
        lse = jnp.where(lane == 2 * pair + par, mx + jnp.log(l), 0.0)
        out = oh if out is None else out + oh
        lse_slab = lse if lse_slab is None else lse_slab + lse
    return out, lse_slab


def _merge_groups(o0, o1, o2, l0, l1, l2):
    mx = lax.stop_gradient(jnp.maximum(jnp.maximum(l0, l1), l2))
    e0, e1, e2 = jnp.exp(l0 - mx), jnp.exp(l1 - mx), jnp.exp(l2 - mx)
    den = e0 + e1 + e2
    return expand_heads(e0 / den) * o0 + expand_heads(e1 / den) * o1 + expand_heads(e2 / den) * o2


def _gla_tile(q, k, v, la, st, reverse):
    nc = q.shape[0] // B_CHUNK
    q = q * (B_KEY_DIM ** -0.5)
    r = lax.broadcasted_iota(jnp.int32, (B_CHUNK, B_CHUNK), 0)
    c = lax.broadcasted_iota(jnp.int32, (B_CHUNK, B_CHUNK), 1)
    mask = (c > r) if reverse else (c <= r)
    outs = [None] * nc
    for ci in (range(nc - 1, -1, -1) if reverse else range(nc)):
        sl = slice(ci * B_CHUNK, (ci + 1) * B_CHUNK)
        qc, kc, vc, lc = q[sl], k[sl], v[sl], la[sl]
        b = cumsum_rows(lc, reverse)
        tot = jnp.sum(lc, axis=0, keepdims=True)
        q_t = qc * jnp.exp(b)
        k_t = kc * jnp.exp(-b)
        k_end = kc * jnp.exp(tot - b)
        attn = jnp.where(mask, dot_nt(q_t, k_t), 0.0)
        outs[ci] = dot_nn(attn, vc) + dot_nt(q_t, st)
        st = st * jnp.exp(tot) + dot_tn(vc, k_end)
    return jnp.concatenate(outs, axis=0), st


def _gla_post(o_f, o_b, r, gain):
    o = o_f + o_b
    heads = [_rms(o[:, h * B_VAL_DIM:(h + 1) * B_VAL_DIM], gain[:, h * B_VAL_DIM:(h + 1) * B_VAL_DIM]) for h in range(B_HEADS)]
    return jnp.concatenate(heads, axis=1) * (r * _sigmoid(r))


def _gate(z, wg, bias):
    return _log_sigmoid(dot_nn(z, wg) + bias) / B_GATE_TAU


def _swiglu_act(g, u):
    return (g * _sigmoid(g)) * u


def _adamw(w, g, m, v):
    m = ADAM_B1 * m + (1.0 - ADAM_B1) * g
    v = ADAM_B2 * v + (1.0 - ADAM_B2) * jnp.square(g)
    m_hat = m / (1.0 - ADAM_B1 ** ADAM_STEP)
    v_hat = v / (1.0 - ADAM_B2 ** ADAM_STEP)
    delta = -ADAM_LR * (m_hat / (jnp.sqrt(v_hat) + ADAM_EPS) + ADAM_WD * w)
    return delta, m, v


def _params(sem=None):
    return pltpu.CompilerParams(dimension_semantics=sem, vmem_limit_bytes=V7X_VMEM_LIMIT)


def _row_tile(s, want=512):
    t = min(want, s)
    assert s % t == 0
    return t


def _matmul(name, a, b, *, dims, grid, a_spec, b_spec, o_spec, out_shape, red_axis=None, res=None, res_spec=None, acc_shape=None):
    n_red = grid[red_axis] if red_axis is not None else 1

    def body(*refs):
        a_ref, b_ref = refs[0], refs[1]
        r_ref = refs[2] if res is not None else None
        o_ref = refs[3] if res is not None else refs[2]
        prod = lax.dot_general(a_ref[...].astype(MXU_DTYPE), b_ref[...].astype(MXU_DTYPE), (dims, ((), ())),
                               preferred_element_type=F32)
        if red_axis is None:
            if r_ref is not None:
                prod = prod + r_ref[...]
            o_ref[...] = prod.astype(o_ref.dtype)
            return
        acc = refs[-1] if acc_shape is not None else o_ref
        k = pl.program_id(red_axis)

        @pl.when(k == 0)
        def _():
            acc[...] = prod + r_ref[...] if r_ref is not None else prod

        @pl.when(k > 0)
        def _():
            acc[...] += prod

        if acc_shape is not None:
            @pl.when(k == n_red - 1)
            def _():
                o_ref[...] = acc[...].astype(o_ref.dtype)

    ins = [a, b] + ([res] if res is not None else [])
    specs = [a_spec, b_spec] + ([res_spec] if res is not None else [])
    sem = tuple("arbitrary" if i == red_axis else "parallel" for i in range(len(grid)))
    return pl.pallas_call(body, name=name, grid=grid, in_specs=specs, out_specs=o_spec, out_shape=out_shape,
                          scratch_shapes=[pltpu.VMEM(acc_shape, F32)] if acc_shape is not None else [],
                          compiler_params=_params(sem))(*ins)


def mm_nn(name, x, w, *, res=None, out_dtype=F32, tn=1024):
    m, k = x.shape
    n = w.shape[1]
    tm, tn = _row_tile(m, 1024), min(tn, n)
    return _matmul(name, x, w, dims=NN, grid=(n // tn, m // tm),
                   a_spec=pl.BlockSpec((tm, k), lambda j, i: (i, 0)), b_spec=pl.BlockSpec((k, tn), lambda j, i: (0, j)),
                   o_spec=pl.BlockSpec((tm, tn), lambda j, i: (i, j)), out_shape=jax.ShapeDtypeStruct((m, n), out_dtype),
                   res=res, res_spec=pl.BlockSpec((tm, tn), lambda j, i: (i, j)))


def mm_nt(name, dy, w, *, res=None, tn=1024):
    m, n = dy.shape
    k = w.shape[0]
    tm, tn = _row_tile(m, 1024), min(tn, n)
    return _matmul(name, dy, w, dims=NT, grid=(m // tm, n // tn), red_axis=1,
                   a_spec=pl.BlockSpec((tm, tn), lambda i, j: (i, j)), b_spec=pl.BlockSpec((k, tn), lambda i, j: (0, j)),
                   o_spec=pl.BlockSpec((tm, k), lambda i, j: (i, 0)), out_shape=jax.ShapeDtypeStruct((m, k), F32),
                   res=res, res_spec=pl.BlockSpec((tm, k), lambda i, j: (i, 0)))


def mm_tn(name, x, dy, *, tn=1024, by_block=False):
    m, k = x.shape
    n = dy.shape[1]
    tm, tn = _row_tile(m, 1024), min(tn, n)
    if by_block:
        o_spec, out_shape, acc = pl.BlockSpec((None, k, tn), lambda j, i: (j, 0, 0)), jax.ShapeDtypeStruct((n // tn, k, tn), BF16), (k, tn)
    else:
        o_spec, out_shape, acc = pl.BlockSpec((k, tn), lambda j, i: (0, j)), jax.ShapeDtypeStruct((k, n), F32), None
    return _matmul(name, x, dy, dims=TN, grid=(n // tn, m // tm), red_axis=1,
                   a_spec=pl.BlockSpec((tm, k), lambda j, i: (i, 0)), b_spec=pl.BlockSpec((tm, tn), lambda j, i: (i, j)),
                   o_spec=o_spec, out_shape=out_shape, acc_shape=acc)


def _rows_call(name, body, ins, outs, s, tile):
    in_specs = []
    for a, kind in ins:
        if kind == "row":
            in_specs.append(pl.BlockSpec((tile, a.shape[1]), lambda i: (i, 0)))
        else:
            in_specs.append(pl.BlockSpec(a.shape, lambda i, nd=a.ndim: (0,) * nd))
    out_specs, out_shape = [], []
    for cols, dt, kind in outs:
        if kind == "row":
            out_specs.append(pl.BlockSpec((tile, cols), lambda i: (i, 0)))
            out_shape.append(jax.ShapeDtypeStruct((s, cols), dt))
        else:
            out_specs.append(pl.BlockSpec((8, cols), lambda i: (0, 0)))
            out_shape.append(jax.ShapeDtypeStruct((8, cols), dt))
    has_acc = any(kind == "acc" for _, _, kind in outs)
    return pl.pallas_call(body, name=name, grid=(s // tile,), in_specs=in_specs, out_specs=out_specs, out_shape=out_shape,
                          compiler_params=_params(("arbitrary",) if has_acc else ("parallel",)))(*[a for a, _ in ins])


def _acc_rows(ref, val):
    @pl.when(pl.program_id(0) == 0)
    def _():
        ref[...] = jnp.zeros_like(ref)

    ref[...] += jnp.broadcast_to(val, ref.shape)


def rmsnorm_fwd(name, h, gain):
    s = h.shape[0]

    def body(h_ref, g_ref, o_ref):
        o_ref[...] = _rms(h_ref[...], g_ref[...]).astype(o_ref.dtype)

    return _rows_call(name, body, [(h, "row"), (gain, "full")], [(D_MODEL, BF16, "row")], s, _row_tile(s))[0]


def rmsnorm_bwd(name, h, gain, d_hn, d_res):
    s = h.shape[0]

    def body(h_ref, g_ref, dy_ref, dr_ref, dh_ref, dg_ref):
        _, vjp = jax.vjp(_rms, h_ref[...], g_ref[...])
        dh, dg = vjp(dy_ref[...])
        dh_ref[...] = dh + dr_ref[...]
        _acc_rows(dg_ref, dg)

    return _rows_call(name, body, [(h, "row"), (gain, "full"), (d_hn, "row"), (d_res, "row")],
                      [(D_MODEL, F32, "row"), (D_MODEL, F32, "acc")], s, _row_tile(s))


def loss_and_grad(y, target):
    s = y.shape[0]
    tile = _row_tile(s)

    def body(y_ref, t_ref, dy_ref, acc_ref):
        diff = y_ref[...] - t_ref[...]
        dy_ref[...] = diff * (1.0 / D_MODEL)

        @pl.when(pl.program_id(0) == 0)
        def _():
            acc_ref[...] = jnp.zeros_like(acc_ref)

        acc_ref[...] += jnp.sum((diff * diff).reshape(tile // 8, 8, D_MODEL), axis=0)

    return _rows_call("loss_head", body, [(y, "row"), (target, "row")], [(D_MODEL, F32, "row"), (D_MODEL, F32, "acc")], s, tile)


def _rope_table(s):
    half = A_HEAD_DIM // 2
    inv_freq = ROPE_THETA ** (-jnp.arange(half, dtype=F32) / half)
    ang = jnp.arange(s).astype(F32)[:, None] * inv_freq[None, :]
    cos, sin = jnp.cos(ang), jnp.sin(ang)
    return jnp.concatenate([cos, cos, cos, cos, -sin, sin, -sin, sin], axis=1)


def _attn_geometry(s, dil):
    length = s // dil
    tile = min(256, length)
    assert length % tile == 0 and tile % A_HALF == 0
    return length, tile, tile // A_HALF, length // A_HALF


def _attn_in_specs(grp, dil, tile, r, nb):
    cq, ck, cv = 3 * grp, 3 * grp + 1, 3 * grp + 2

    def main(col, width=D_MODEL, per=9):
        return pl.BlockSpec((tile, width), lambda p, i: (i, p * per + col))

    def prev(col, width=D_MODEL, per=9):
        return pl.BlockSpec((A_HALF, width), lambda p, i: (jnp.maximum(i * r - 1, 0), p * per + col))

    def nxt(col, width=D_MODEL, per=9):
        return pl.BlockSpec((A_HALF, width), lambda p, i: (jnp.minimum(i * r + r, nb - 1), p * per + col))

    return [main(cq), prev(ck), main(ck), nxt(ck), prev(cv), main(cv), nxt(cv),
            prev(0, 256, 1), main(0, 256, 1), nxt(0, 256, 1)]


def _attn_valid(i, tile, length):
    tq = i * tile + lax.broadcasted_iota(jnp.int32, (tile, 1), 0)
    tk = i * tile - A_HALF + lax.broadcasted_iota(jnp.int32, (1, tile + 2 * A_HALF), 1)
    return (tk >= 0) & (tk < length) & (jnp.abs(tk - tq) <= A_HALF)


def attn_fwd(name, qkv, tab, gq, gk, grp, dil):
    s = qkv.shape[0]
    length, tile, r, nb = _attn_geometry(s, dil)

    def body(q_ref, kp_ref, km_ref, kn_ref, vp_ref, vm_ref, vn_ref, tp_ref, tm_ref, tn_ref, gq_ref, gk_ref, o_ref, lse_ref):
        valid = _attn_valid(pl.program_id(1), tile, length)
        tk = jnp.concatenate([tp_ref[...], tm_ref[...], tn_ref[...]], axis=0)
        lse = jnp.zeros((tile, LANES), F32)
        for j in range(D_MODEL // LANES):
            sl = slice(j * LANES, (j + 1) * LANES)
            kr = jnp.concatenate([kp_ref[:, sl], km_ref[:, sl], kn_ref[:, sl]], axis=0)
            vv = jnp.concatenate([vp_ref[:, sl], vm_ref[:, sl], vn_ref[:, sl]], axis=0)
            o, lse_j = _attn_pair(q_ref[:, sl], kr, vv, tm_ref[...], tk, gq_ref[...], gk_ref[...], valid, j)
            o_ref[:, sl] = o
            lse = lse + lse_j
        lse_ref[...] = lse

    full = lambda a: pl.BlockSpec(a.shape, lambda p, i: (0, 0))
    x = qkv.reshape(length, dil * qkv.shape[1])
    t = tab.reshape(length, dil * 256)
    o, lse = pl.pallas_call(
        body, name=name, grid=(dil, length // tile),
        in_specs=_attn_in_specs(grp, dil, tile, r, nb) + [full(gq), full(gk)],
        out_specs=[pl.BlockSpec((tile, D_MODEL), lambda p, i: (i, p)), pl.BlockSpec((tile, LANES), lambda p, i: (i, p))],
        out_shape=[jax.ShapeDtypeStruct((length, dil * D_MODEL), F32), jax.ShapeDtypeStruct((length, dil * LANES), F32)],
        compiler_params=_params(("parallel", "parallel")),
    )(x, x, x, x, x, x, x, t, t, t, gq, gk)
    return o.reshape(s, D_MODEL), lse.reshape(s, LANES)


def attn_bwd(name, qkv, tab, gq, gk, d_o, d_lse, grp, dil):
    s = qkv.shape[0]
    length, tile, r, nb = _attn_geometry(s, dil)
    nt = length // tile

    def body(q_ref, kp_ref, km_ref, kn_ref, vp_ref, vm_ref, vn_ref, tp_ref, tm_ref, tn_ref, gq_ref, gk_ref, do_ref, dl_ref,
             dq_ref, dkp_ref, dkm_ref, dkn_ref, dvp_ref, dvm_ref, dvn_ref, dgq_ref, dgk_ref):
        valid = _attn_valid(pl.program_id(1), tile, length)
        tq = tm_ref[...]
        tk = jnp.concatenate([tp_ref[...], tq, tn_ref[...]], axis=0)
        d_lse_slab = dl_ref[...]
        dgq = jnp.zeros((1, LANES), F32)
        dgk = jnp.zeros((1, LANES), F32)
        for j in range(D_MODEL // LANES):
            sl = slice(j * LANES, (j + 1) * LANES)
            kr = jnp.concatenate([kp_ref[:, sl], km_ref[:, sl], kn_ref[:, sl]], axis=0)
            vv = jnp.concatenate([vp_ref[:, sl], vm_ref[:, sl], vn_ref[:, sl]], axis=0)
            pair = functools.partial(_attn_pair, tq=tq, tk=tk, valid=valid, pair=j)
            _, vjp = jax.vjp(lambda a, b, c, d, e: pair(a, b, c, gq=d, gk=e), q_ref[:, sl], kr, vv, gq_ref[...], gk_ref[...])
            dq, dk, dv, dgq_j, dgk_j = vjp((do_ref[:, sl], d_lse_slab))
            dq_ref[:, sl] = dq
            dkp_ref[:, sl] = dk[:A_HALF]
            dkm_ref[:, sl] = dk[A_HALF:A_HALF + tile]
            dkn_ref[:, sl] = dk[A_HALF + tile:]
            dvp_ref[:, sl] = dv[:A_HALF]
            dvm_ref[:, sl] = dv[A_HALF:A_HALF + tile]
            dvn_ref[:, sl] = dv[A_HALF + tile:]
            dgq = dgq + dgq_j
            dgk = dgk + dgk_j

        @pl.when((pl.program_id(0) == 0) & (pl.program_id(1) == 0))
        def _():
            dgq_ref[...] = jnp.zeros_like(dgq_ref)
            dgk_ref[...] = jnp.zeros_like(dgk_ref)

        dgq_ref[...] += jnp.broadcast_to(dgq + pltpu.roll(dgq, A_HEAD_DIM, 1), dgq_ref.shape)
        dgk_ref[...] += jnp.broadcast_to(dgk + pltpu.roll(dgk, A_HEAD_DIM, 1), dgk_ref.shape)

    full = lambda a: pl.BlockSpec(a.shape, lambda p, i: (0, 0))
    x = qkv.reshape(length, dil * qkv.shape[1])
    t = tab.reshape(length, dil * 256)
    view = lambda a: a.reshape(length, dil * a.shape[1])
    main_o = pl.BlockSpec((None, tile, D_MODEL), lambda p, i: (p, i, 0))
    edge_o = pl.BlockSpec((None, None, A_HALF, D_MODEL), lambda p, i: (p, i, 0, 0))
    main_s = jax.ShapeDtypeStruct((dil, length, D_MODEL), F32)
    edge_s = jax.ShapeDtypeStruct((dil, nt, A_HALF, D_MODEL), F32)
    acc_o = pl.BlockSpec((8, LANES), lambda p, i: (0, 0))
    acc_s = jax.ShapeDtypeStruct((8, LANES), F32)
    outs = pl.pallas_call(
        body, name=name, grid=(dil, nt),
        in_specs=_attn_in_specs(grp, dil, tile, r, nb) + [full(gq), full(gk)]
        + [pl.BlockSpec((tile, D_MODEL), lambda p, i: (i, p)), pl.BlockSpec((tile, LANES), lambda p, i: (i, p))],
        out_specs=[pl.BlockSpec((tile, D_MODEL), lambda p, i: (i, p)), edge_o, main_o, edge_o, edge_o, main_o, edge_o, acc_o, acc_o],
        out_shape=[jax.ShapeDtypeStruct((length, dil * D_MODEL), F32), edge_s, main_s, edge_s, edge_s, main_s, edge_s, acc_s, acc_s],
        compiler_params=_params(("arbitrary", "arbitrary")),
    )(x, x, x, x, x, x, x, t, t, t, gq, gk, view(d_o), view(d_lse))
    dq, dkp, dkm, dkn, dvp, dvm, dvn, dgq, dgk = outs
    return dq.reshape(s, D_MODEL), (dkp, dkm, dkn), (dvp, dvm, dvn), dgq, dgk


def attn_combine(name, parts, s, dil):
    prev_part, main_part, next_part = parts
    length, tile, r, nb = _attn_geometry(s, dil)
    nt = length // tile

    def body(m_ref, from_prev_ref, from_next_ref, o_ref):
        i = pl.program_id(1)
        o_ref[...] = m_ref[...]
        head = o_ref[0:A_HALF, :] + jnp.where(i > 0, from_prev_ref[...], 0.0)
        o_ref[0:A_HALF, :] = head
        tail = o_ref[tile - A_HALF:tile, :] + jnp.where(i < nt - 1, from_next_ref[...], 0.0)
        o_ref[tile - A_HALF:tile, :] = tail

    out = pl.pallas_call(
        body, name=name, grid=(dil, nt),
        in_specs=[pl.BlockSpec((None, tile, D_MODEL), lambda p, i: (p, i, 0)),
                  pl.BlockSpec((None, None, A_HALF, D_MODEL), lambda p, i: (p, jnp.maximum(i - 1, 0), 0, 0)),
                  pl.BlockSpec((None, None, A_HALF, D_MODEL), lambda p, i: (p, jnp.minimum(i + 1, nt - 1), 0, 0))],
        out_specs=pl.BlockSpec((tile, D_MODEL), lambda p, i: (i, p)),
        out_shape=jax.ShapeDtypeStruct((length, dil * D_MODEL), F32),
        compiler_params=_params(("parallel", "parallel")),
    )(main_part, next_part, prev_part)
    return out.reshape(s, D_MODEL)


def attn_merge_fwd(name, os_, lses):
    s = os_[0].shape[0]

    def body(o0, o1, o2, l0, l1, l2, out_ref):
        out_ref[...] = _merge_groups(o0[...], o1[...], o2[...], l0[...], l1[...], l2[...]).astype(out_ref.dtype)

    return _rows_call(name, body, [(a, "row") for a in (*os_, *lses)], [(D_MODEL, BF16, "row")], s, _row_tile(s, 256))[0]


def attn_merge_bwd(name, os_, lses, d_out):
    s = os_[0].shape[0]

    def body(o0, o1, o2, l0, l1, l2, d_ref, *outs):
        _, vjp = jax.vjp(_merge_groups, o0[...], o1[...], o2[...], l0[...], l1[...], l2[...])
        for ref, val in zip(outs, vjp(d_ref[...])):
            ref[...] = val

    return _rows_call(name, body, [(a, "row") for a in (*os_, *lses, d_out)],
                      [(D_MODEL, F32, "row")] * 3 + [(LANES, F32, "row")] * 3, s, _row_tile(s, 256))


def gla_gate_fwd(name, z, wg, bias):
    s = z.shape[0]

    def body(z_ref, w_ref, b_ref, o_ref):
        o_ref[...] = _gate(z_ref[...], w_ref[...], b_ref[...])

    return _rows_call(name, body, [(z, "row"), (wg, "full"), (bias, "full")], [(D_MODEL, F32, "row")], s, _row_tile(s))[0]


def gla_gate_bwd(name, z, wg, bias, d_la_f, d_la_b):
    s = z.shape[0]
    tile = _row_tile(s)

    def body(z_ref, w_ref, b_ref, df_ref, db_ref, dz_ref, dw_ref, dbias_ref):
        _, vjp = jax.vjp(_gate, z_ref[...], w_ref[...], b_ref[...])
        dz, dw, dbias = vjp(jnp.concatenate([df_ref[...], db_ref[...]], axis=1))
        dz_ref[...] = dz

        @pl.when(pl.program_id(0) == 0)
        def _():
            dw_ref[...] = jnp.zeros_like(dw_ref)

        dw_ref[...] += dw
        _acc_rows(dbias_ref, dbias)

    return pl.pallas_call(
        body, name=name, grid=(s // tile,),
        in_specs=[pl.BlockSpec((tile, LANES), lambda i: (i, 0)), pl.BlockSpec(wg.shape, lambda i: (0, 0)),
                  pl.BlockSpec(bias.shape, lambda i: (0, 0)), pl.BlockSpec((tile, 512), lambda i: (i, 0)),
                  pl.BlockSpec((tile, 512), lambda i: (i, 0))],
        out_specs=[pl.BlockSpec((tile, LANES), lambda i: (i, 0)), pl.BlockSpec(wg.shape, lambda i: (0, 0)),
                   pl.BlockSpec((8, D_MODEL), lambda i: (0, 0))],
        out_shape=[jax.ShapeDtypeStruct((s, LANES), F32), jax.ShapeDtypeStruct(wg.shape, F32), jax.ShapeDtypeStruct((8, D_MODEL), F32)],
        compiler_params=_params(("arbitrary",)),
    )(z, wg, bias, d_la_f, d_la_b)


def _gla_in_specs(tile, order, la_col0):
    t = order
    return [pl.BlockSpec((tile, B_KEY_DIM), lambda h, n: (t(n), h)),
            pl.BlockSpec((tile, B_KEY_DIM), lambda h, n: (t(n), B_HEADS + h)),
            pl.BlockSpec((tile, B_VAL_DIM), lambda h, n: (t(n), B_HEADS + h)),
            pl.BlockSpec((tile, B_KEY_DIM), lambda h, n: (t(n), la_col0 + h))]


def gla_fwd(name, proj, la, reverse):
    s = proj.shape[0]
    tile = _row_tile(s)
    nt = s // tile
    order = (lambda n: nt - 1 - n) if reverse else (lambda n: n)

    def body(q_ref, k_ref, v_ref, la_ref, o_ref, st_ref, st_scr):
        @pl.when(pl.program_id(1) == 0)
        def _():
            st_scr[...] = jnp.zeros_like(st_scr)

        st_ref[...] = st_scr[...]
        o, st = _gla_tile(q_ref[...], k_ref[...], v_ref[...], la_ref[...], st_scr[...], reverse)
        o_ref[...] = o
        st_scr[...] = st

    return pl.pallas_call(
        body, name=name, grid=(B_HEADS, nt), in_specs=_gla_in_specs(tile, order, B_HEADS if reverse else 0),
        out_specs=[pl.BlockSpec((tile, B_VAL_DIM), lambda h, n: (order(n), h)),
                   pl.BlockSpec((None, None, B_VAL_DIM, B_KEY_DIM), lambda h, n: (h, order(n), 0, 0))],
        out_shape=[jax.ShapeDtypeStruct((s, D_MODEL), F32), jax.ShapeDtypeStruct((B_HEADS, nt, B_VAL_DIM, B_KEY_DIM), F32)],
        scratch_shapes=[pltpu.VMEM((B_VAL_DIM, B_KEY_DIM), F32)],
        compiler_params=_params(("parallel", "arbitrary")),
    )(proj, proj, proj, la)


def gla_bwd(name, proj, la, states, d_o, reverse, prev=None):
    s = proj.shape[0]
    tile = _row_tile(s)
    nt = s // tile
    order = (lambda n: n) if reverse else (lambda n: nt - 1 - n)

    def body(*refs):
        q_ref, k_ref, v_ref, la_ref, st_ref, do_ref = refs[:6]
        rest = refs[6:]
        prev_refs = rest[:3] if prev is not None else None
        dq_ref, dk_ref, dv_ref, dla_ref, dst_scr = rest[3:] if prev is not None else rest

        @pl.when(pl.program_id(1) == 0)
        def _():
            dst_scr[...] = jnp.zeros_like(dst_scr)

        _, vjp = jax.vjp(functools.partial(_gla_tile, reverse=reverse), q_ref[...], k_ref[...], v_ref[...], la_ref[...], st_ref[...])
        dq, dk, dv, dla, dst = vjp((do_ref[...], dst_scr[...]))
        if prev_refs is not None:
            dq, dk, dv = dq + prev_refs[0][...], dk + prev_refs[1][...], dv + prev_refs[2][...]
        dq_ref[...], dk_ref[...], dv_ref[...], dla_ref[...] = dq, dk, dv, dla
        dst_scr[...] = dst

    key_spec = pl.BlockSpec((tile, B_KEY_DIM), lambda h, n: (order(n), h))
    val_spec = pl.BlockSpec((tile, B_VAL_DIM), lambda h, n: (order(n), h))
    in_specs = _gla_in_specs(tile, order, B_HEADS if reverse else 0) + [
        pl.BlockSpec((None, None, B_VAL_DIM, B_KEY_DIM), lambda h, n: (h, order(n), 0, 0)), val_spec]
    ins = [proj, proj, proj, la, states, d_o]
    if prev is not None:
        in_specs += [key_spec, key_spec, val_spec]
        ins += list(prev)
    return pl.pallas_call(
        body, name=name, grid=(B_HEADS, nt), in_specs=in_specs,
        out_specs=[key_spec, key_spec, val_spec, key_spec],
        out_shape=[jax.ShapeDtypeStruct((s, 512), F32), jax.ShapeDtypeStruct((s, 512), F32),
                   jax.ShapeDtypeStruct((s, D_MODEL), F32), jax.ShapeDtypeStruct((s, 512), F32)],
        scratch_shapes=[pltpu.VMEM((B_VAL_DIM, B_KEY_DIM), F32)],
        compiler_params=_params(("parallel", "arbitrary")),
    )(*ins)


def _r_spec(tile):
    return pl.BlockSpec((tile, D_MODEL), lambda i: (i, 2))


def gla_post_fwd(name, o_f, o_b, proj, gain):
    s = o_f.shape[0]
    tile = _row_tile(s)

    def body(of_ref, ob_ref, r_ref, g_ref, out_ref):
        out_ref[...] = _gla_post(of_ref[...], ob_ref[...], r_ref[...], g_ref[...]).astype(out_ref.dtype)

    row = pl.BlockSpec((tile, D_MODEL), lambda i: (i, 0))
    return pl.pallas_call(body, name=name, grid=(s // tile,),
                          in_specs=[row, row, _r_spec(tile), pl.BlockSpec(gain.shape, lambda i: (0, 0))], out_specs=row,
                          out_shape=jax.ShapeDtypeStruct((s, D_MODEL), BF16), compiler_params=_params(("parallel",)))(o_f, o_b, proj, gain)


def gla_post_bwd(name, o_f, o_b, proj, gain, d_out):
    s = o_f.shape[0]
    tile = _row_tile(s)

    def body(of_ref, ob_ref, r_ref, g_ref, d_ref, do_ref, dr_ref, dg_ref):
        _, vjp = jax.vjp(_gla_post, of_ref[...], ob_ref[...], r_ref[...], g_ref[...])
        d_of, _, dr, dg = vjp(d_ref[...])
        do_ref[...] = d_of
        dr_ref[...] = dr
        _acc_rows(dg_ref, dg)

    row = pl.BlockSpec((tile, D_MODEL), lambda i: (i, 0))
    return pl.pallas_call(
        body, name=name, grid=(s // tile,),
        in_specs=[row, row, _r_spec(tile), pl.BlockSpec(gain.shape, lambda i: (0, 0)), row],
        out_specs=[row, row, pl.BlockSpec((8, D_MODEL), lambda i: (0, 0))],
        out_shape=[jax.ShapeDtypeStruct((s, D_MODEL), F32), jax.ShapeDtypeStruct((s, D_MODEL), F32), jax.ShapeDtypeStruct((8, D_MODEL), F32)],
        compiler_params=_params(("arbitrary",)))(o_f, o_b, proj, gain, d_out)


def _hid(tile, where):
    return pl.BlockSpec((None, tile, FFN_BLK), where)


def _pair(tile, where):
    return pl.BlockSpec((2, None, tile, FFN_BLK), where)


def _w_gu_spec(layer, where_j):
    return pl.BlockSpec((2, None, None, D_MODEL, FFN_BLK), lambda *g: (0, where_j(*g), layer, 0, 0))


def ffn_fwd(name, h_mid, hn2, w_gu, w_down, layer):
    s = hn2.shape[0]
    tm = _row_tile(s, 1024)
    nt = s // tm

    def gu_body(x_ref, w_ref, gu_ref, act_ref):
        x = x_ref[...]
        g = _dot(x, w_ref[0], NN)
        u = _dot(x, w_ref[1], NN)
        gu_ref[0] = g
        gu_ref[1] = u
        act_ref[...] = _swiglu_act(g, u).astype(act_ref.dtype)

    gu, act = pl.pallas_call(
        gu_body, name=name + "_gu", grid=(4, nt),
        in_specs=[pl.BlockSpec((tm, D_MODEL), lambda j, i: (i, 0)), _w_gu_spec(layer, lambda j, i: j)],
        out_specs=[_pair(tm, lambda j, i: (0, j, i, 0)), _hid(tm, lambda j, i: (j, i, 0))],
        out_shape=[jax.ShapeDtypeStruct((2, 4, s, FFN_BLK), F32), jax.ShapeDtypeStruct((4, s, FFN_BLK), BF16)],
        compiler_params=_params(("parallel", "parallel")))(hn2, w_gu)
    row = pl.BlockSpec((tm, D_MODEL), lambda i, j: (i, 0))
    h_next = _matmul(name + "_down", act, w_down, dims=NN, grid=(nt, 4), red_axis=1,
                     a_spec=_hid(tm, lambda i, j: (j, i, 0)), b_spec=pl.BlockSpec((None, FFN_BLK, D_MODEL), lambda i, j: (j, 0, 0)),
                     o_spec=row, out_shape=jax.ShapeDtypeStruct((s, D_MODEL), F32), res=h_mid, res_spec=row)
    return h_next, gu, act


def ffn_bwd(name, dh_next, hn2, gu, act, w_gu, w_down, layer):
    s = hn2.shape[0]
    tm = _row_tile(s, 1024)
    nt = s // tm
    d_wd = _matmul(name + "_dwd", act, dh_next, dims=TN, grid=(4, nt), red_axis=1,
                   a_spec=_hid(tm, lambda j, i: (j, i, 0)), b_spec=pl.BlockSpec((tm, D_MODEL), lambda j, i: (i, 0)),
                   o_spec=pl.BlockSpec((None, FFN_BLK, D_MODEL), lambda j, i: (j, 0, 0)),
                   out_shape=jax.ShapeDtypeStruct((4, FFN_BLK, D_MODEL), BF16), acc_shape=(FFN_BLK, D_MODEL))

    def dgu_body(dy_ref, wd_ref, gu_ref, dgu_ref):
        d_act = _dot(dy_ref[...], wd_ref[...], NT)
        _, vjp = jax.vjp(_swiglu_act, gu_ref[0], gu_ref[1])
        dg, du = vjp(d_act)
        dgu_ref[0] = dg.astype(dgu_ref.dtype)
        dgu_ref[1] = du.astype(dgu_ref.dtype)

    d_gu = pl.pallas_call(
        dgu_body, name=name + "_dgu", grid=(4, nt),
        in_specs=[pl.BlockSpec((tm, D_MODEL), lambda j, i: (i, 0)), pl.BlockSpec((None, FFN_BLK, D_MODEL), lambda j, i: (j, 0, 0)),
                  _pair(tm, lambda j, i: (0, j, i, 0))],
        out_specs=_pair(tm, lambda j, i: (0, j, i, 0)), out_shape=jax.ShapeDtypeStruct((2, 4, s, FFN_BLK), BF16),
        compiler_params=_params(("parallel", "parallel")))(dh_next, w_down, gu)

    def dx_body(d_ref, w_ref, o_ref):
        prod = _dot(d_ref[0], w_ref[0], NT) + _dot(d_ref[1], w_ref[1], NT)

        @pl.when(pl.program_id(1) == 0)
        def _():
            o_ref[...] = prod

        @pl.when(pl.program_id(1) > 0)
        def _():
            o_ref[...] += prod

    d_hn2 = pl.pallas_call(
        dx_body, name=name + "_dx", grid=(nt, 4),
        in_specs=[_pair(tm, lambda i, j: (0, j, i, 0)), _w_gu_spec(layer, lambda i, j: j)],
        out_specs=pl.BlockSpec((tm, D_MODEL), lambda i, j: (i, 0)), out_shape=jax.ShapeDtypeStruct((s, D_MODEL), F32),
        compiler_params=_params(("parallel", "arbitrary")))(d_gu, w_gu)

    def dw_body(x_ref, d_ref, o_ref, acc_ref):
        x = x_ref[...]
        k = pl.program_id(1)
        for t in range(2):
            prod = _dot(x, d_ref[t], TN)

            @pl.when(k == 0)
            def _():
                acc_ref[t] = prod

            @pl.when(k > 0)
            def _():
                acc_ref[t] += prod

        @pl.when(k == nt - 1)
        def _():
            o_ref[...] = acc_ref[...].astype(o_ref.dtype)

    d_wgu = pl.pallas_call(
        dw_body, name=name + "_dwgu", grid=(4, nt),
        in_specs=[pl.BlockSpec((tm, D_MODEL), lambda j, i: (i, 0)), _pair(tm, lambda j, i: (0, j, i, 0))],
        out_specs=pl.BlockSpec((2, None, D_MODEL, FFN_BLK), lambda j, i: (0, j, 0, 0)),
        out_shape=jax.ShapeDtypeStruct((2, 4, D_MODEL, FFN_BLK), BF16),
        scratch_shapes=[pltpu.VMEM((2, D_MODEL, FFN_BLK), F32)],
        compiler_params=_params(("parallel", "arbitrary")))(hn2, d_gu)
    return d_hn2, d_wgu, d_wd


def _my_place():
    return lax.axis_index("x"), lax.axis_index("y"), lax.axis_index("c")


def _flip(place, k):
    x, y, c = place
    return (1 - x if k & 4 else x, 1 - y if k & 2 else y, 1 - c if k & 1 else c)


def _index(place):
    return 4 * place[0] + 2 * place[1] + place[2]


def all_gather(arrs):
    n = len(arrs)

    def body(*refs):
        ins, outs = refs[:n], refs[n:2 * n]
        send_sems, recv_sems, local_sems = refs[2 * n:]
        me = _my_place()
        sibling = _flip(me, 1)
        chips = (4, 2, 6)

        def copy(a, k, block, to, src=None):
            dst = outs[a].at[_index(block)]
            return pltpu.make_async_remote_copy(src_ref=dst if src is None else src, dst_ref=dst, send_sem=send_sems.at[a, k],
                                                recv_sem=recv_sems.at[a, k], device_id=to, device_id_type=MESH)

        started = []
        for a in range(n):
            mine = pltpu.make_async_copy(ins[a], outs[a].at[_index(me)], local_sems.at[a])
            mine.start()
            started.append(mine)
        first = []
        for a in range(n):
            first.append(copy(a, 0, me, sibling, src=ins[a]))
            first += [copy(a, 1 + j, me, _flip(me, k), src=ins[a]) for j, k in enumerate(chips)]
        for cp in first:
            cp.start()
        passed = []
        for a in range(n):
            for j, k in enumerate(chips):
                copy(a, 1 + j, _flip(me, k), me).wait_recv()
                fwd = copy(a, 4 + j, _flip(me, k), sibling)
                fwd.start()
                passed.append(fwd)
        for a in range(n):
            copy(a, 0, sibling, me).wait_recv()
            for j, k in enumerate(chips):
                copy(a, 4 + j, _flip(sibling, k), me).wait_recv()
        for cp in first + passed:
            cp.wait_send()
        for cp in started:
            cp.wait()

    return pl.pallas_call(
        body, name="all_gather_weights", in_specs=[ANY] * n, out_specs=[ANY] * n,
        out_shape=[jax.ShapeDtypeStruct((N_DEV,) + a.shape, a.dtype) for a in arrs],
        scratch_shapes=[pltpu.SemaphoreType.DMA((n, 7)), pltpu.SemaphoreType.DMA((n, 7)), pltpu.SemaphoreType.DMA((n,))],
    )(*arrs)


def exchange_partials(arrs):
    n = len(arrs)

    def body(*refs):
        ins, outs = refs[:n], refs[n:2 * n]
        send_sems, recv_sems, local_sems = refs[2 * n:]
        me = _my_place()
        local = []
        for a in range(n):
            cp = pltpu.make_async_copy(ins[a].at[_index(me)], outs[a].at[_index(me)], local_sems.at[a])
            cp.start()
            local.append(cp)

        def copy(a, k, src_block, dst_block):
            return pltpu.make_async_remote_copy(src_ref=ins[a].at[_index(src_block)], dst_ref=outs[a].at[_index(dst_block)],
                                                send_sem=send_sems.at[a, k - 1], recv_sem=recv_sems.at[a, k - 1],
                                                device_id=_flip(me, k), device_id_type=MESH)

        sent = []
        for a in range(n):
            for k in range(1, N_DEV):
                cp = copy(a, k, _flip(me, k), me)
                cp.start()
                sent.append(cp)
        for a in range(n):
            for k in range(1, N_DEV):
                copy(a, k, me, _flip(me, k)).wait_recv()
        for cp in sent:
            cp.wait_send()
        for cp in local:
            cp.wait()

    return pl.pallas_call(
        body, name="exchange_weight_grads", in_specs=[ANY] * n, out_specs=[ANY] * n,
        out_shape=[jax.ShapeDtypeStruct(a.shape, a.dtype) for a in arrs],
        scratch_shapes=[pltpu.SemaphoreType.DMA((n, 7)), pltpu.SemaphoreType.DMA((n, 7)), pltpu.SemaphoreType.DMA((n,))],
    )(*arrs)


def adamw_shard(name, parts, w, m, v, layer, tile):
    rows, cols = parts.shape[1:]
    assert rows % tile == 0
    off = layer * (rows // tile)

    def body(p_ref, w_ref, m_ref, v_ref, g_ref, d_ref, nm_ref, nv_ref):
        g = p_ref[0].astype(F32)
        for src in range(1, N_DEV):
            g = g + p_ref[src].astype(F32)
        g_ref[...] = g
        d_ref[...], nm_ref[...], nv_ref[...] = _adamw(w_ref[...], g, m_ref[...], v_ref[...])

    src_row = pl.BlockSpec((tile, cols), lambda i: (off + i, 0))
    row = pl.BlockSpec((tile, cols), lambda i: (i, 0))
    shape = jax.ShapeDtypeStruct((rows, cols), F32)
    return pl.pallas_call(body, name=name, grid=(rows // tile,),
                          in_specs=[pl.BlockSpec((N_DEV, tile, cols), lambda i: (0, i, 0)), src_row, src_row, src_row],
                          out_specs=[row] * 4, out_shape=[shape] * 4, compiler_params=_params(("parallel",)))(parts, w, m, v)


def allreduce_adamw_replicated(partial, w, m, v, n_loss_rows):
    rows = partial.shape[0]

    def body(p_ref, w_ref, m_ref, v_ref, g_ref, d_ref, nm_ref, nv_ref, loss_ref, recv_ref, send_sems, recv_sems):
        me = _my_place()
        recv_ref[_index(me)] = p_ref[...]
        copies = []
        for k in range(1, N_DEV):
            peer = _flip(me, k)
            cp = pltpu.make_async_remote_copy(src_ref=p_ref, dst_ref=recv_ref.at[_index(me)], send_sem=send_sems.at[k - 1],
                                              recv_sem=recv_sems.at[k - 1], device_id=peer, device_id_type=MESH)
            cp.start()
            copies.append((cp, peer))
        for k, (cp, peer) in enumerate(copies):
            pltpu.make_async_remote_copy(src_ref=p_ref, dst_ref=recv_ref.at[_index(peer)], send_sem=send_sems.at[k],
                                         recv_sem=recv_sems.at[k], device_id=peer, device_id_type=MESH).wait_recv()
        for cp, _ in copies:
            cp.wait_send()
        g = recv_ref[0]
        for src in range(1, N_DEV):
            g = g + recv_ref[src]
        g_ref[...] = g
        d_ref[...], nm_ref[...], nv_ref[...] = _adamw(w_ref[...], g, m_ref[...], v_ref[...])
        loss = (0.5 / D_MODEL) * jnp.sum(g[rows - n_loss_rows:, :])
        loss_ref[...] = jnp.full(loss_ref.shape, loss, F32)

    shape = jax.ShapeDtypeStruct((rows, LANES), F32)
    return pl.pallas_call(
        body, name="allreduce_adamw_replicated", in_specs=[VMEM_SPEC] * 4, out_specs=[VMEM_SPEC] * 5,
        out_shape=[shape] * 4 + [jax.ShapeDtypeStruct((8, LANES), F32)],
        scratch_shapes=[pltpu.VMEM((N_DEV, rows, LANES), F32), pltpu.SemaphoreType.DMA((7,)), pltpu.SemaphoreType.DMA((7,))],
    )(partial, w, m, v)


def _pack_rows(flat, cols):
    n = flat.shape[-1]
    rows = -(-n // cols)
    rows = -(-rows // 48) * 48
    flat = jnp.pad(flat, [(0, 0)] * (flat.ndim - 1) + [(0, rows * cols - n)])
    return flat.reshape(flat.shape[:-1] + (rows, cols))


def _unpack(flat, shapes):
    out, off = [], 0
    for shp in shapes:
        n = 1
        for d in shp:
            n *= d
        out.append(flat[off:off + n].reshape(shp))
        off += n
    return out


def _to_dev_cols(a):
    w = a.shape[-1] // N_DEV
    return jnp.moveaxis(a.reshape(a.shape[:-1] + (N_DEV, w)), -2, 0)


def _from_dev_cols(a):
    a = jnp.moveaxis(a, 0, -2)
    return a.reshape(a.shape[:-2] + (a.shape[-2] * a.shape[-1],))


def kernel(x, attn_norm, ffn_norm, a_w_in, a_q_norm, a_k_norm, a_w_out, b_w_in, b_w_gate_f, b_gate_bias_f, b_w_gate_b, b_gate_bias_b, b_out_norm, b_w_out, ffn_w_gate_up, ffn_w_down, loss_target, m_attn_norm, m_ffn_norm, m_a_w_in, m_a_q_norm, m_a_k_norm, m_a_w_out, m_b_w_in, m_b_w_gate_f, m_b_gate_bias_f, m_b_w_gate_b, m_b_gate_bias_b, m_b_out_norm, m_b_w_out, m_ffn_w_gate_up, m_ffn_w_down, v_attn_norm, v_ffn_norm, v_a_w_in, v_a_q_norm, v_a_k_norm, v_a_w_out, v_b_w_in, v_b_w_gate_f, v_b_gate_bias_f, v_b_w_gate_b, v_b_gate_bias_b, v_b_out_norm, v_b_w_out, v_ffn_w_gate_up, v_ffn_w_down):
    seq = x.shape[1]
    depth = attn_norm.shape[0]
    h = x.reshape(seq, D_MODEL)
    target = loss_target.reshape(seq, D_MODEL)
    n_a, n_b = a_w_in.shape[0], b_w_in.shape[0]

    small = jnp.concatenate([t.reshape(-1) for t in (b_w_gate_f, b_gate_bias_f, b_w_gate_b, b_gate_bias_b, b_out_norm)])
    small = _pack_rows(small, LANES)
    g_a_in, g_a_out, g_b_in, g_b_out, g_gu, g_down, g_small = all_gather(
        [a_w_in.astype(BF16), a_w_out.astype(BF16), b_w_in.astype(BF16), b_w_out.astype(BF16),
         ffn_w_gate_up.astype(BF16), ffn_w_down.astype(BF16), small])
    w_a_in = _from_dev_cols(g_a_in)
    w_a_out = jnp.moveaxis(g_a_out, 0, 1).reshape(n_a, D_MODEL, D_MODEL)
    w_b_in = _from_dev_cols(g_b_in)
    w_b_out = jnp.moveaxis(g_b_out, 0, 1).reshape(n_b, D_MODEL, D_MODEL)
    w_down = jnp.moveaxis(g_down, 0, 1).reshape(depth, 4, FFN_BLK, D_MODEL)
    w_gu = g_gu.reshape(2, 4, depth, D_MODEL, FFN_BLK)
    small_shapes = [t.shape for t in (b_w_gate_f, b_gate_bias_f, b_w_gate_b, b_gate_bias_b, b_out_norm)]
    per_dev = [_unpack(g_small[d].reshape(-1), small_shapes) for d in range(N_DEV)]
    wgf, bgf, wgb, bgb, onorm = [_from_dev_cols(jnp.stack([per_dev[d][t] for d in range(N_DEV)])) for t in range(5)]
    w_gate = jnp.zeros((n_b, LANES, D_MODEL), F32)
    w_gate = w_gate.at[:, 0:16, 0:512].set(wgf).at[:, 16:32, 512:1024].set(wgb)
    gate_bias = jnp.concatenate([bgf, bgb], axis=1).reshape(n_b, 1, D_MODEL)
    out_gain = onorm.reshape(n_b, 1, D_MODEL)
    w_b_main = w_b_in[:, :, :3072]
    w_b_z = jnp.pad(w_b_in[:, :, 3072:], ((0, 0), (0, 0), (0, LANES - 32)))

    dh, sq_err, grads = _forward_backward(h, target, attn_norm, ffn_norm, a_q_norm, a_k_norm, w_a_in, w_a_out, w_b_main, w_b_z,
                                          w_gate, gate_bias, out_gain, w_b_out, w_gu, w_down)
    sharded_w = (a_w_in, a_w_out, b_w_in, b_w_gate_f, b_gate_bias_f, b_w_gate_b, b_gate_bias_b, b_out_norm, b_w_out, ffn_w_gate_up, ffn_w_down)
    sharded_m = (m_a_w_in, m_a_w_out, m_b_w_in, m_b_w_gate_f, m_b_gate_bias_f, m_b_w_gate_b, m_b_gate_bias_b, m_b_out_norm, m_b_w_out, m_ffn_w_gate_up, m_ffn_w_down)
    sharded_v = (v_a_w_in, v_a_w_out, v_b_w_in, v_b_w_gate_f, v_b_gate_bias_f, v_b_w_gate_b, v_b_gate_bias_b, v_b_out_norm, v_b_w_out, v_ffn_w_gate_up, v_ffn_w_down)
    rep_w = (attn_norm, ffn_norm, a_q_norm, a_k_norm)
    rep_m = (m_attn_norm, m_ffn_norm, m_a_q_norm, m_a_k_norm)
    rep_v = (v_attn_norm, v_ffn_norm, v_a_q_norm, v_a_k_norm)
    loss, outs = _reduce_and_update(grads, sq_err, sharded_w, sharded_m, sharded_v, rep_w, rep_m, rep_v)
    return (loss, dh.reshape(x.shape), *outs)


def _forward_backward(h, target, attn_norm, ffn_norm, a_q_norm, a_k_norm, w_a_in, w_a_out, w_b_main, w_b_z, w_gate, gate_bias,
                      out_gain, w_b_out, w_gu, w_down):
    seq = h.shape[0]
    depth = attn_norm.shape[0]
    n_a, n_b = w_a_in.shape[0], w_b_main.shape[0]
    tab = _rope_table(seq)
    pair_gain = lambda g: jnp.concatenate([g, g]).reshape(1, LANES)

    saved = []
    for i in range(depth):
        j = i // 2
        nm = f"l{i}"
        hn = rmsnorm_fwd(nm + "_norm1", h, attn_norm[i].reshape(1, D_MODEL))
        if i % 2 == 0:
            qkv = mm_nn(nm + "_qkv", hn, w_a_in[j])
            os_, lses = [], []
            for g, (_, dil) in enumerate(A_GROUPS):
                o, lse = attn_fwd(f"{nm}_attn{g}", qkv, tab, pair_gain(a_q_norm[j, g]), pair_gain(a_k_norm[j, g]), g, dil)
                os_.append(o)
                lses.append(lse)
            mixed = attn_merge_fwd(nm + "_merge", os_, lses)
            h_mid = mm_nn(nm + "_out", mixed, w_a_out[j], res=h)
            mix_saved = (qkv, os_, lses, mixed)
        else:
            proj = mm_nn(nm + "_proj", hn, w_b_main[j])
            z = mm_nn(nm + "_z", hn, w_b_z[j])
            la = gla_gate_fwd(nm + "_gate", z, w_gate[j], gate_bias[j])
            o_f, st_f = gla_fwd(nm + "_gla_f", proj, la, False)
            o_b, st_b = gla_fwd(nm + "_gla_b", proj, la, True)
            mixed = gla_post_fwd(nm + "_post", o_f, o_b, proj, out_gain[j])
            h_mid = mm_nn(nm + "_out", mixed, w_b_out[j], res=h)
            mix_saved = (proj, z, la, o_f, st_f, o_b, st_b, mixed)
        hn2 = rmsnorm_fwd(nm + "_norm2", h_mid, ffn_norm[i].reshape(1, D_MODEL))
        h_next, gu, act = ffn_fwd(nm + "_ffn", h_mid, hn2, w_gu, w_down[i], i)
        saved.append((h, hn, mix_saved, h_mid, hn2, gu, act))
        h = h_next

    dh, sq_err = loss_and_grad(h, target)

    g_attn_norm, g_ffn_norm = [None] * depth, [None] * depth
    g_a_w_in, g_a_w_out, g_a_q, g_a_k = [None] * n_a, [None] * n_a, [None] * n_a, [None] * n_a
    g_b_w_in, g_b_w_out, g_w_gate, g_gate_bias, g_out_gain = ([None] * n_b for _ in range(5))
    g_w_gu, g_w_down = [None] * depth, [None] * depth
    for i in reversed(range(depth)):
        j = i // 2
        nm = f"l{i}b"
        h_in, hn, mix_saved, h_mid, hn2, gu, act = saved[i]
        d_hn2, g_w_gu[i], g_w_down[i] = ffn_bwd(nm + "_ffn", dh, hn2, gu, act, w_gu, w_down[i], i)
        dh_mid, g_ffn_norm[i] = rmsnorm_bwd(nm + "_norm2", h_mid, ffn_norm[i].reshape(1, D_MODEL), d_hn2, dh)
        if i % 2 == 0:
            qkv, os_, lses, mixed = mix_saved
            d_mixed = mm_nt(nm + "_dmixed", dh_mid, w_a_out[j])
            g_a_w_out[j] = mm_tn(nm + "_dwout", mixed, dh_mid, by_block=True)
            d_parts = attn_merge_bwd(nm + "_merge", os_, lses, d_mixed)
            cols, gq_l, gk_l = [], [], []
            for g, (_, dil) in enumerate(A_GROUPS):
                dq, dk_parts, dv_parts, dgq, dgk = attn_bwd(f"{nm}_attn{g}", qkv, tab, pair_gain(a_q_norm[j, g]),
                                                            pair_gain(a_k_norm[j, g]), d_parts[g], d_parts[3 + g], g, dil)
                cols += [dq, attn_combine(f"{nm}_dk{g}", dk_parts, seq, dil), attn_combine(f"{nm}_dv{g}", dv_parts, seq, dil)]
                gq_l.append(dgq[0, :A_HEAD_DIM])
                gk_l.append(dgk[0, :A_HEAD_DIM])
            g_a_q[j], g_a_k[j] = jnp.stack(gq_l), jnp.stack(gk_l)
            d_qkv = jnp.concatenate(cols, axis=1)
            d_hn = mm_nt(nm + "_dhn", d_qkv, w_a_in[j])
            g_a_w_in[j] = mm_tn(nm + "_dwin", hn, d_qkv, tn=D_MODEL * 9 // N_DEV, by_block=True)
        else:
            proj, z, la, o_f, st_f, o_b, st_b, mixed = mix_saved
            d_mixed = mm_nt(nm + "_dmixed", dh_mid, w_b_out[j])
            g_b_w_out[j] = mm_tn(nm + "_dwout", mixed, dh_mid, by_block=True)
            d_o, d_r, dgain = gla_post_bwd(nm + "_post", o_f, o_b, proj, out_gain[j], d_mixed)
            g_out_gain[j] = dgain[0]
            dq, dk, dv, dla_f = gla_bwd(nm + "_gla_f", proj, la, st_f, d_o, False)
            dq, dk, dv, dla_b = gla_bwd(nm + "_gla_b", proj, la, st_b, d_o, True, prev=(dq, dk, dv))
            d_z, g_w_gate[j], dbias = gla_gate_bwd(nm + "_gate", z, w_gate[j], gate_bias[j], dla_f, dla_b)
            g_gate_bias[j] = dbias[0]
            d_proj = jnp.concatenate([dq, dk, dv, d_r], axis=1)
            d_hn = mm_nt(nm + "_dhn_z", d_z, w_b_z[j])
            d_hn = mm_nt(nm + "_dhn", d_proj, w_b_main[j], res=d_hn)
            g_b_w_in[j] = jnp.concatenate([mm_tn(nm + "_dwin", hn, d_proj), mm_tn(nm + "_dwz", hn, d_z)[:, :32]], axis=1)
        dh, g_attn_norm[i] = rmsnorm_bwd(nm + "_norm1", h_in, attn_norm[i].reshape(1, D_MODEL), d_hn, dh_mid)
    return dh, sq_err, (g_attn_norm, g_ffn_norm, g_a_w_in, g_a_w_out, g_a_q, g_a_k, g_b_w_in, g_b_w_out, g_w_gate, g_gate_bias,
                        g_out_gain, g_w_gu, g_w_down)


def _reduce_and_update(grads, sq_err, sharded_w, sharded_m, sharded_v, rep_w, rep_m, rep_v):
    (g_attn_norm, g_ffn_norm, g_a_w_in, g_a_w_out, g_a_q, g_a_k, g_b_w_in, g_b_w_out, g_w_gate, g_gate_bias, g_out_gain,
     g_w_gu, g_w_down) = grads
    depth, n_a, n_b = len(g_w_gu), len(g_a_w_in), len(g_b_w_in)

    g_w_gate = jnp.stack(g_w_gate)
    g_gate_bias = jnp.stack(g_gate_bias)
    small_parts = [_to_dev_cols(g_w_gate[:, 0:16, 0:512]), _to_dev_cols(g_gate_bias[:, 0:512]),
                   _to_dev_cols(g_w_gate[:, 16:32, 512:1024]), _to_dev_cols(g_gate_bias[:, 512:1024]),
                   _to_dev_cols(jnp.stack(g_out_gain).reshape(n_b, B_HEADS, B_VAL_DIM))]
    small_part = _pack_rows(jnp.concatenate([t.reshape(N_DEV, -1) for t in small_parts], axis=1), LANES).astype(BF16)
    families = [
        (0, g_a_w_in, 256),
        (1, [t.reshape(N_DEV, -1, D_MODEL) for t in g_a_w_out], 128),
        (2, [_to_dev_cols(t).astype(BF16) for t in g_b_w_in], 256),
        (8, [t.reshape(N_DEV, -1, D_MODEL) for t in g_b_w_out], 128),
        (9, [t.reshape(N_DEV, D_MODEL, FFN_BLK) for t in g_w_gu], 256),
        (10, [t.reshape(N_DEV, -1, D_MODEL) for t in g_w_down], 176),
    ]
    flat_parts = [p for _, parts, _ in families for p in parts] + [small_part]
    received = exchange_partials(flat_parts)
    sh_out = [None] * len(sharded_w)
    pos = 0
    for fam, parts, tile in families:
        w = sharded_w[fam]
        two_d = lambda t: t.reshape(-1, t.shape[-1])
        per_layer = []
        for layer in range(len(parts)):
            per_layer.append(adamw_shard(f"adamw_p{fam}_l{layer}", received[pos], two_d(w), two_d(sharded_m[fam]),
                                         two_d(sharded_v[fam]), layer, tile))
            pos += 1
        sh_out[fam] = [jnp.stack([per_layer[l][t] for l in range(len(parts))]).reshape(w.shape) for t in range(4)]
    small_ids = (3, 4, 5, 6, 7)
    pack_small = lambda ts: _pack_rows(jnp.concatenate([ts[i].reshape(-1) for i in small_ids]), LANES)
    small_out = adamw_shard("adamw_small", received[pos], pack_small(sharded_w), pack_small(sharded_m), pack_small(sharded_v), 0, 48)
    small_shapes = [sharded_w[i].shape for i in small_ids]
    for t in range(4):
        for i, val in zip(small_ids, _unpack(small_out[t].reshape(-1), small_shapes)):
            if sh_out[i] is None:
                sh_out[i] = [None] * 4
            sh_out[i][t] = val
    sh_grad, sh_delta, sh_m, sh_v = [[sh_out[i][t] for i in range(len(sharded_w))] for t in range(4)]

    rep_g = (jnp.stack([t[0] for t in g_attn_norm]), jnp.stack([t[0] for t in g_ffn_norm]), jnp.stack(g_a_q), jnp.stack(g_a_k))
    n_rep = sum(t.size for t in rep_w)
    n_rep_rows = -(-n_rep // (8 * LANES)) * 8
    n_loss_rows = 8 * D_MODEL // LANES

    def pack_rep(ts, tail):
        flat = jnp.concatenate([t.reshape(-1) for t in ts])
        flat = jnp.pad(flat, (0, n_rep_rows * LANES - n_rep))
        return jnp.concatenate([flat.reshape(n_rep_rows, LANES), tail], axis=0)

    zeros_tail = jnp.zeros((n_loss_rows, LANES), F32)
    rep_out = allreduce_adamw_replicated(pack_rep(rep_g, sq_err.reshape(n_loss_rows, LANES)), pack_rep(rep_w, zeros_tail),
                                         pack_rep(rep_m, zeros_tail), pack_rep(rep_v, zeros_tail + 1.0), n_loss_rows)
    rep_shapes = [t.shape for t in rep_w]
    r_grad, r_delta, r_m, r_v = [_unpack(p.reshape(-1), rep_shapes) for p in rep_out[:4]]
    loss = rep_out[4][0, 0]

    def ordered(rep, sh):
        return [rep[0], rep[1], sh[0], rep[2], rep[3]] + list(sh[1:])

    return loss, (*ordered(r_grad, sh_grad), *ordered(r_delta, sh_delta), *ordered(r_m, sh_m), *ordered(r_v, sh_v))
```

```python
import functools

import jax
import jax.numpy as jnp
from jax import lax
from jax.experimental import pallas as pl
from jax.experimental.pallas import tpu as pltpu

F32 = jnp.float32
BF16 = jnp.bfloat16
MXU_DTYPE = jnp.bfloat16

D_MODEL = 1024
N_DEV = 8
RMS_EPS = 1e-6
NEG_INF = -1e30
A_GROUPS = ((128, 1), (512, 4), (2048, 16))
A_HEAD_DIM = 64
A_HALF = 64
ATT_T = 128
ATT_TILE = 2048
B_HEADS = 4
B_KEY_DIM = 128
B_VAL_DIM = 256
B_CHUNK = 64
B_GATE_TAU = 16.0
FFN_HIDDEN = 2816
FFN_BLK = 2 * FFN_HIDDEN // N_DEV
ADAM_LR, ADAM_B1, ADAM_B2, ADAM_EPS, ADAM_WD, ADAM_STEP = 0.001, 0.9, 0.999, 1e-08, 0.01, 10
ROPE_THETA = 10000.0

V7X_VMEM_LIMIT = 56 * 1024 * 1024
LANES = 128
MESH = pl.DeviceIdType.MESH
ANY = pl.BlockSpec(memory_space=pl.ANY)
VMEM_SPEC = pl.BlockSpec(memory_space=pltpu.VMEM)

NN = ((1,), (0,))
NT = ((1,), (1,))
TN = ((0,), (0,))


def _dot(a, b, dims):
    return lax.dot_general(a.astype(MXU_DTYPE), b.astype(MXU_DTYPE), (dims, ((), ())), preferred_element_type=F32)


@jax.custom_vjp
def dot_nn(a, b):
    return _dot(a, b, NN)


@jax.custom_vjp
def dot_nt(a, b):
    return _dot(a, b, NT)


@jax.custom_vjp
def dot_tn(a, b):
    return _dot(a, b, TN)


dot_nn.defvjp(lambda a, b: (_dot(a, b, NN), (a, b)), lambda r, g: (dot_nt(g, r[1]), dot_tn(r[0], g)))
dot_nt.defvjp(lambda a, b: (_dot(a, b, NT), (a, b)), lambda r, g: (dot_nn(g, r[1]), dot_tn(g, r[0])))
dot_tn.defvjp(lambda a, b: (_dot(a, b, TN), (a, b)), lambda r, g: (dot_nt(r[1], g), dot_nn(r[0], g)))


def _dot_f32(a, b):
    return lax.dot_general(a, b, (NN, ((), ())), precision=lax.Precision.HIGHEST, preferred_element_type=F32)


def _tri(n, upper):
    r = lax.broadcasted_iota(jnp.int32, (n, n), 0)
    c = lax.broadcasted_iota(jnp.int32, (n, n), 1)
    return jnp.where((c >= r) if upper else (c <= r), 1.0, 0.0).astype(F32)


@functools.partial(jax.custom_vjp, nondiff_argnums=(1,))
def cumsum_rows(x, reverse):
    return _dot_f32(_tri(x.shape[0], reverse), x)


cumsum_rows.defvjp(
    lambda x, reverse: (_dot_f32(_tri(x.shape[0], reverse), x), None),
    lambda reverse, _, g: (_dot_f32(_tri(g.shape[0], not reverse), g),),
)


def _head_expand_matrix(transposed):
    shape = (D_MODEL, LANES) if transposed else (LANES, D_MODEL)
    h = lax.broadcasted_iota(jnp.int32, shape, 1 if transposed else 0)
    c = lax.broadcasted_iota(jnp.int32, shape, 0 if transposed else 1)
    return jnp.where(c // A_HEAD_DIM == h, 1.0, 0.0).astype(F32)


@jax.custom_vjp
def expand_heads(a):
    return _dot_f32(a, _head_expand_matrix(False))


expand_heads.defvjp(
    lambda a: (_dot_f32(a, _head_expand_matrix(False)), None),
    lambda _, g: (_dot_f32(g, _head_expand_matrix(True)),),
)


def _swap32_raw(x):
    lane = lax.broadcasted_iota(jnp.int32, x.shape, 1)
    return jnp.where((lane % 64) < 32, pltpu.roll(x, 96, 1), pltpu.roll(x, 32, 1))


@jax.custom_vjp
def swap32(x):
    return _swap32_raw(x)


swap32.defvjp(lambda x: (_swap32_raw(x), None), lambda _, g: (_swap32_raw(g),))


def _rms(x, gain):
    return x * lax.rsqrt(jnp.mean(x * x, axis=-1, keepdims=True) + RMS_EPS) * gain


def _sigmoid(x):
    return 1.0 / (1.0 + jnp.exp(-x))


def _log_sigmoid(x):
    return jnp.minimum(x, 0.0) - jnp.log(1.0 + jnp.exp(-jnp.abs(x)))


def _qk_prep(x, tab, gain):
    lo = lax.broadcasted_iota(jnp.int32, (1, LANES), 1) < A_HEAD_DIM
    x2 = x * x
    s_lo = jnp.sum(jnp.where(lo, x2, 0.0), axis=-1, keepdims=True)
    s_hi = jnp.sum(jnp.where(lo, 0.0, x2), axis=-1, keepdims=True)
    r = jnp.where(lo, lax.rsqrt(s_lo / A_HEAD_DIM + RMS_EPS), lax.rsqrt(s_hi / A_HEAD_DIM + RMS_EPS))
    xn = (x * r) * gain
    return xn * tab[:, :LANES] + swap32(xn) * tab[:, LANES:]


def _attn_job(q, k, v, valid, pair):
    lane = lax.broadcasted_iota(jnp.int32, (1, LANES), 1)
    lo = lane < A_HEAD_DIM
    out = None
    lse_slab = None
    for par in range(2):
        m = lo if par == 0 else jnp.logical_not(lo)
        s = dot_nt(jnp.where(m, q, 0.0), k) * (A_HEAD_DIM ** -0.5)
        s = jnp.where(valid, s, NEG_INF)
        mx = lax.stop_gradient(jnp.max(s, axis=-1, keepdims=True))
        p = jnp.exp(s - mx)
        l = jnp.sum(p, axis=-1, keepdims=True)
        oh = dot_nn(p, jnp.where(m, v, 0.0)) / l
        lse = jnp.where(lane == 2 * pair + par, mx + jnp.log(l), 0.0)
        out = oh if out is None else out + oh
        lse_slab = lse if lse_slab is None else lse_slab + lse
    return out, lse_slab


def _merge_groups(o0, o1, o2, l0, l1, l2):
    mx = lax.stop_gradient(jnp.maximum(jnp.maximum(l0, l1), l2))
    e0, e1, e2 = jnp.exp(l0 - mx), jnp.exp(l1 - mx), jnp.exp(l2 - mx)
    den = e0 + e1 + e2
    return expand_heads(e0 / den) * o0 + expand_heads(e1 / den) * o1 + expand_heads(e2 / den) * o2


def _gla_tile(q, k, v, la, st, reverse):
    nc = q.shape[0] // B_CHUNK
    q = q * (B_KEY_DIM ** -0.5)
    r = lax.broadcasted_iota(jnp.int32, (B_CHUNK, B_CHUNK), 0)
    c = lax.broadcasted_iota(jnp.int32, (B_CHUNK, B_CHUNK), 1)
    mask = (c > r) if reverse else (c <= r)
    outs = [None] * nc
    for ci in (range(nc - 1, -1, -1) if reverse else range(nc)):
        sl = slice(ci * B_CHUNK, (ci + 1) * B_CHUNK)
        qc, kc, vc, lc = q[sl], k[sl], v[sl], la[sl]
        b = cumsum_rows(lc, reverse)
        tot = jnp.sum(lc, axis=0, keepdims=True)
        q_t = qc * jnp.exp(b)
        k_t = kc * jnp.exp(-b)
        k_end = kc * jnp.exp(tot - b)
        attn = jnp.where(mask, dot_nt(q_t, k_t), 0.0)
        outs[ci] = dot_nn(attn, vc) + dot_nt(q_t, st)
        st = st * jnp.exp(tot) + dot_tn(vc, k_end)
    return jnp.concatenate(outs, axis=0), st


def _gla_post(o_f, o_b, r, gain):
    o = o_f + o_b
    heads = [_rms(o[:, h * B_VAL_DIM:(h + 1) * B_VAL_DIM], gain[:, h * B_VAL_DIM:(h + 1) * B_VAL_DIM]) for h in range(B_HEADS)]
    return jnp.concatenate(heads, axis=1) * (r * _sigmoid(r))


def _gate(z, wg, bias):
    return _log_sigmoid(dot_nn(z, wg) + bias) / B_GATE_TAU


def _swiglu_act(g, u):
    return (g * _sigmoid(g)) * u


def _adamw(w, g, m, v):
    m = ADAM_B1 * m + (1.0 - ADAM_B1) * g
    v = ADAM_B2 * v + (1.0 - ADAM_B2) * jnp.square(g)
    m_hat = m / (1.0 - ADAM_B1 ** ADAM_STEP)
    v_hat = v / (1.0 - ADAM_B2 ** ADAM_STEP)
    delta = -ADAM_LR * (m_hat / (jnp.sqrt(v_hat) + ADAM_EPS) + ADAM_WD * w)
    return delta, m, v


def _params(sem=None):
    return pltpu.CompilerParams(dimension_semantics=sem, vmem_limit_bytes=V7X_VMEM_LIMIT)


def _row_tile(s, want=512):
    t = min(want, s)
    assert s % t == 0
    return t


def _matmul(name, a, b, *, dims, grid, a_spec, b_spec, o_spec, out_shape, red_axis=None, res=None, res_spec=None, acc_shape=None):
    n_red = grid[red_axis] if red_axis is not None else 1

    def body(*refs):
        a_ref, b_ref = refs[0], refs[1]
        r_ref = refs[2] if res is not None else None
        o_ref = refs[3] if res is not None else refs[2]
        prod = lax.dot_general(a_ref[...].astype(MXU_DTYPE), b_ref[...].astype(MXU_DTYPE), (dims, ((), ())),
                               preferred_element_type=F32)
        if red_axis is None:
            if r_ref is not None:
                prod = prod + r_ref[...]
            o_ref[...] = prod.astype(o_ref.dtype)
            return
        acc = refs[-1] if acc_shape is not None else o_ref
        k = pl.program_id(red_axis)

        @pl.when(k == 0)
        def _():
            acc[...] = prod + r_ref[...] if r_ref is not None else prod

        @pl.when(k > 0)
        def _():
            acc[...] += prod

        if acc_shape is not None:
            @pl.when(k == n_red - 1)
            def _():
                o_ref[...] = acc[...].astype(o_ref.dtype)

    ins = [a, b] + ([res] if res is not None else [])
    specs = [a_spec, b_spec] + ([res_spec] if res is not None else [])
    sem = tuple("arbitrary" if i == red_axis else "parallel" for i in range(len(grid)))
    return pl.pallas_call(body, name=name, grid=grid, in_specs=specs, out_specs=o_spec, out_shape=out_shape,
                          scratch_shapes=[pltpu.VMEM(acc_shape, F32)] if acc_shape is not None else [],
                          compiler_params=_params(sem))(*ins)


def mm_nn(name, x, w, *, res=None, out_dtype=F32, tn=1024):
    m, k = x.shape
    n = w.shape[1]
    tm, tn = _row_tile(m, 1024), min(tn, n)
    return _matmul(name, x, w, dims=NN, grid=(n // tn, m // tm),
                   a_spec=pl.BlockSpec((tm, k), lambda j, i: (i, 0)), b_spec=pl.BlockSpec((k, tn), lambda j, i: (0, j)),
                   o_spec=pl.BlockSpec((tm, tn), lambda j, i: (i, j)), out_shape=jax.ShapeDtypeStruct((m, n), out_dtype),
                   res=res, res_spec=pl.BlockSpec((tm, tn), lambda j, i: (i, j)))


def mm_nt(name, dy, w, *, res=None, tn=1024):
    m, n = dy.shape
    k = w.shape[0]
    tm, tn = _row_tile(m, 1024), min(tn, n)
    return _matmul(name, dy, w, dims=NT, grid=(m // tm, n // tn), red_axis=1,
                   a_spec=pl.BlockSpec((tm, tn), lambda i, j: (i, j)), b_spec=pl.BlockSpec((k, tn), lambda i, j: (0, j)),
                   o_spec=pl.BlockSpec((tm, k), lambda i, j: (i, 0)), out_shape=jax.ShapeDtypeStruct((m, k), F32),
                   res=res, res_spec=pl.BlockSpec((tm, k), lambda i, j: (i, 0)))


def mm_tn(name, x, dy, *, tn=1024, by_block=False):
    m, k = x.shape
    n = dy.shape[1]
    tm, tn = _row_tile(m, 1024), min(tn, n)
    if by_block:
        o_spec, out_shape, acc = pl.BlockSpec((None, k, tn), lambda j, i: (j, 0, 0)), jax.ShapeDtypeStruct((n // tn, k, tn), BF16), (k, tn)
    else:
        o_spec, out_shape, acc = pl.BlockSpec((k, tn), lambda j, i: (0, j)), jax.ShapeDtypeStruct((k, n), F32), None
    return _matmul(name, x, dy, dims=TN, grid=(n // tn, m // tm), red_axis=1,
                   a_spec=pl.BlockSpec((tm, k), lambda j, i: (i, 0)), b_spec=pl.BlockSpec((tm, tn), lambda j, i: (i, j)),
                   o_spec=o_spec, out_shape=out_shape, acc_shape=acc)


def _rows_call(name, body, ins, outs, s, tile):
    in_specs = []
    for a, kind in ins:
        if kind == "row":
            in_specs.append(pl.BlockSpec((tile, a.shape[1]), lambda i: (i, 0)))
        else:
            in_specs.append(pl.BlockSpec(a.shape, lambda i, nd=a.ndim: (0,) * nd))
    out_specs, out_shape = [], []
    for cols, dt, kind in outs:
        if kind == "row":
            out_specs.append(pl.BlockSpec((tile, cols), lambda i: (i, 0)))
            out_shape.append(jax.ShapeDtypeStruct((s, cols), dt))
        else:
            out_specs.append(pl.BlockSpec((8, cols), lambda i: (0, 0)))
            out_shape.append(jax.ShapeDtypeStruct((8, cols), dt))
    has_acc = any(kind == "acc" for _, _, kind in outs)
    return pl.pallas_call(body, name=name, grid=(s // tile,), in_specs=in_specs, out_specs=out_specs, out_shape=out_shape,
                          compiler_params=_params(("arbitrary",) if has_acc else ("parallel",)))(*[a for a, _ in ins])


def _acc_rows(ref, val):
    @pl.when(pl.program_id(0) == 0)
    def _():
        ref[...] = jnp.zeros_like(ref)

    ref[...] += jnp.broadcast_to(val, ref.shape)


def rmsnorm_fwd(name, h, gain):
    s = h.shape[0]

    def body(h_ref, g_ref, o_ref):
        o_ref[...] = _rms(h_ref[...], g_ref[...]).astype(o_ref.dtype)

    return _rows_call(name, body, [(h, "row"), (gain, "full")], [(D_MODEL, BF16, "row")], s, _row_tile(s))[0]


def rmsnorm_bwd(name, h, gain, d_hn, d_res):
    s = h.shape[0]

    def body(h_ref, g_ref, dy_ref, dr_ref, dh_ref, dg_ref):
        _, vjp = jax.vjp(_rms, h_ref[...], g_ref[...])
        dh, dg = vjp(dy_ref[...])
        dh_ref[...] = dh + dr_ref[...]
        _acc_rows(dg_ref, dg)

    return _rows_call(name, body, [(h, "row"), (gain, "full"), (d_hn, "row"), (d_res, "row")],
                      [(D_MODEL, F32, "row"), (D_MODEL, F32, "acc")], s, _row_tile(s))


def loss_and_grad(y, target):
    s = y.shape[0]
    tile = _row_tile(s)

    def body(y_ref, t_ref, dy_ref, acc_ref):
        diff = y_ref[...] - t_ref[...]
        dy_ref[...] = diff * (1.0 / D_MODEL)

        @pl.when(pl.program_id(0) == 0)
        def _():
            acc_ref[...] = jnp.zeros_like(acc_ref)

        acc_ref[...] += jnp.sum((diff * diff).reshape(tile // 8, 8, D_MODEL), axis=0)

    return _rows_call("loss_head", body, [(y, "row"), (target, "row")], [(D_MODEL, F32, "row"), (D_MODEL, F32, "acc")], s, tile)


def _rope_table(s):
    half = A_HEAD_DIM // 2
    inv_freq = ROPE_THETA ** (-jnp.arange(half, dtype=F32) / half)
    ang = jnp.arange(s).astype(F32)[:, None] * inv_freq[None, :]
    cos, sin = jnp.cos(ang), jnp.sin(ang)
    return jnp.concatenate([cos, cos, cos, cos, -sin, sin, -sin, sin], axis=1)


def _attn_geometry(s, dil):
    tile = min(ATT_TILE, s)
    halo = A_HALF * dil
    assert s % tile == 0 and tile % (ATT_T * dil) == 0 and tile % halo == 0
    return tile, halo, tile // (ATT_T * dil)


def _attn_in_specs(grp, s, tile, halo):
    hb, n_hb = tile // halo, s // halo
    cq, ck, cv = (24 * grp + 8 * t for t in range(3))

    def main(col, per_pair, width=LANES):
        return pl.BlockSpec((tile, width), lambda i, j: (i, col + per_pair * j))

    def prev(col, per_pair, width=LANES):
        return pl.BlockSpec((halo, width), lambda i, j: (jnp.maximum(i * hb - 1, 0), col + per_pair * j))

    def nxt(col, per_pair, width=LANES):
        return pl.BlockSpec((halo, width), lambda i, j: (jnp.minimum((i + 1) * hb, n_hb - 1), col + per_pair * j))

    return [main(cq, 1), prev(ck, 1), main(ck, 1), nxt(ck, 1), prev(cv, 1), main(cv, 1), nxt(cv, 1),
            prev(0, 0, 256), main(0, 0, 256), nxt(0, 0, 256)]


def _attn_valid(base, length):
    tq = base + lax.broadcasted_iota(jnp.int32, (ATT_T, 1), 0)
    tk = base - A_HALF + lax.broadcasted_iota(jnp.int32, (1, ATT_T + 2 * A_HALF), 1)
    return (tk >= 0) & (tk < length) & (jnp.abs(tk - tq) <= A_HALF)


def _jobs(tile, dil, n_sub):
    return [(u * ATT_T * dil + p, u * ATT_T) for u in range(n_sub) for p in range(dil)]


def _rows(start, size, dil):
    return pl.ds(start, size, stride=dil) if dil > 1 else pl.ds(start, size)


def attn_fwd(name, qkv, tab, gq, gk, grp, dil):
    s = qkv.shape[0]
    tile, halo, n_sub = _attn_geometry(s, dil)
    length, per_tile = s // dil, tile // dil
    nk = ATT_T + 2 * A_HALF

    def body(q_ref, kp_ref, km_ref, kn_ref, vp_ref, vm_ref, vn_ref, tp_ref, tm_ref, tn_ref, gq_ref, gk_ref, o_ref, lse_ref,
             q_buf, k_buf, v_buf):
        i, pair = pl.program_id(0), pl.program_id(1)
        q_buf[...] = _qk_prep(q_ref[...], tm_ref[...], gq_ref[...])
        for ref, t_ref, lo, n in ((kp_ref, tp_ref, 0, halo), (km_ref, tm_ref, halo, tile), (kn_ref, tn_ref, halo + tile, halo)):
            k_buf[lo:lo + n, :] = _qk_prep(ref[...], t_ref[...], gk_ref[...])
        for ref, lo, n in ((vp_ref, 0, halo), (vm_ref, halo, tile), (vn_ref, halo + tile, halo)):
            v_buf[lo:lo + n, :] = ref[...]

        @pl.when(pair == 0)
        def _():
            lse_ref[...] = jnp.zeros_like(lse_ref)

        for start, t0 in _jobs(tile, dil, n_sub):
            valid = _attn_valid(i * per_tile + t0, length)
            o, lse = _attn_job(q_buf[_rows(start, ATT_T, dil), :], k_buf[_rows(start, nk, dil), :],
                               v_buf[_rows(start, nk, dil), :], valid, pair)
            o_ref[_rows(start, ATT_T, dil), :] = o
            lse_ref[_rows(start, ATT_T, dil), :] += lse

    full = lambda a: pl.BlockSpec(a.shape, lambda i, j: (0, 0))
    return pl.pallas_call(
        body, name=name, grid=(s // tile, D_MODEL // LANES),
        in_specs=_attn_in_specs(grp, s, tile, halo) + [full(gq), full(gk)],
        out_specs=[pl.BlockSpec((tile, LANES), lambda i, j: (i, j)), pl.BlockSpec((tile, LANES), lambda i, j: (i, 0))],
        out_shape=[jax.ShapeDtypeStruct((s, D_MODEL), F32), jax.ShapeDtypeStruct((s, LANES), F32)],
        scratch_shapes=[pltpu.VMEM((tile, LANES), F32), pltpu.VMEM((tile + 2 * halo, LANES), F32), pltpu.VMEM((tile + 2 * halo, LANES), F32)],
        compiler_params=_params(("parallel", "arbitrary")),
    )(qkv, qkv, qkv, qkv, qkv, qkv, qkv, tab, tab, tab, gq, gk)


def attn_bwd(name, qkv, tab, gq, gk, d_o, d_lse, grp, dil, d_qkv):
    s = qkv.shape[0]
    tile, halo, n_sub = _attn_geometry(s, dil)
    length, per_tile = s // dil, tile // dil
    nt = s // tile
    nk = ATT_T + 2 * A_HALF
    pieces = ((0, halo), (halo, tile), (halo + tile, halo))

    def body(q_ref, kp_ref, km_ref, kn_ref, vp_ref, vm_ref, vn_ref, tp_ref, tm_ref, tn_ref, gq_ref, gk_ref, do_ref, dl_ref,
             _, dq_ref, dkp_ref, dkm_ref, dkn_ref, dvp_ref, dvm_ref, dvn_ref, dgq_ref, dgk_ref, q_buf, k_buf, v_buf, dq_buf, dk_buf, dv_buf):
        i, pair = pl.program_id(0), pl.program_id(1)
        k_refs, t_refs = (kp_ref, km_ref, kn_ref), (tp_ref, tm_ref, tn_ref)
        q_buf[...] = _qk_prep(q_ref[...], tm_ref[...], gq_ref[...])
        for ref, t_ref, (lo, n) in zip(k_refs, t_refs, pieces):
            k_buf[lo:lo + n, :] = _qk_prep(ref[...], t_ref[...], gk_ref[...])
        for ref, (lo, n) in zip((vp_ref, vm_ref, vn_ref), pieces):
            v_buf[lo:lo + n, :] = ref[...]
        dk_buf[...] = jnp.zeros_like(dk_buf)
        dv_buf[...] = jnp.zeros_like(dv_buf)
        for start, t0 in _jobs(tile, dil, n_sub):
            valid = _attn_valid(i * per_tile + t0, length)
            _, vjp = jax.vjp(functools.partial(_attn_job, valid=valid, pair=pair), q_buf[_rows(start, ATT_T, dil), :],
                             k_buf[_rows(start, nk, dil), :], v_buf[_rows(start, nk, dil), :])
            dq, dk, dv = vjp((do_ref[_rows(start, ATT_T, dil), :], dl_ref[_rows(start, ATT_T, dil), :]))
            dq_buf[_rows(start, ATT_T, dil), :] = dq
            dk_buf[_rows(start, nk, dil), :] += dk
            dv_buf[_rows(start, nk, dil), :] += dv
        _, vjp = jax.vjp(lambda x, g: _qk_prep(x, tm_ref[...], g), q_ref[...], gq_ref[...])
        dq_ref[...], dgq = vjp(dq_buf[...])
        dgk = jnp.zeros((1, LANES), F32)
        for ref, t_ref, out_ref, (lo, n) in zip(k_refs, t_refs, (dkp_ref, dkm_ref, dkn_ref), pieces):
            _, vjp = jax.vjp(lambda x, g: _qk_prep(x, t_ref[...], g), ref[...], gk_ref[...])
            out_ref[...], dgk_piece = vjp(dk_buf[lo:lo + n, :])
            dgk = dgk + dgk_piece
        for out_ref, (lo, n) in zip((dvp_ref, dvm_ref, dvn_ref), pieces):
            out_ref[...] = dv_buf[lo:lo + n, :]

        @pl.when((i == 0) & (pair == 0))
        def _():
            dgq_ref[...] = jnp.zeros_like(dgq_ref)
            dgk_ref[...] = jnp.zeros_like(dgk_ref)

        dgq_ref[...] += jnp.broadcast_to(dgq + pltpu.roll(dgq, A_HEAD_DIM, 1), dgq_ref.shape)
        dgk_ref[...] += jnp.broadcast_to(dgk + pltpu.roll(dgk, A_HEAD_DIM, 1), dgk_ref.shape)

    full = lambda a: pl.BlockSpec(a.shape, lambda i, j: (0, 0))
    main_o = pl.BlockSpec((tile, LANES), lambda i, j: (i, j))
    edge_o = pl.BlockSpec((None, halo, LANES), lambda i, j: (i, 0, j))
    main_s = jax.ShapeDtypeStruct((s, D_MODEL), F32)
    edge_s = jax.ShapeDtypeStruct((nt, halo, D_MODEL), F32)
    acc_o = pl.BlockSpec((8, LANES), lambda i, j: (0, 0))
    acc_s = jax.ShapeDtypeStruct((8, LANES), F32)
    big = pltpu.VMEM((tile + 2 * halo, LANES), F32)
    outs = pl.pallas_call(
        body, name=name, grid=(nt, D_MODEL // LANES),
        in_specs=_attn_in_specs(grp, s, tile, halo) + [full(gq), full(gk), main_o, pl.BlockSpec((tile, LANES), lambda i, j: (i, 0)), ANY],
        out_specs=[pl.BlockSpec((tile, LANES), lambda i, j: (i, 24 * grp + j)), edge_o, main_o, edge_o, edge_o, main_o, edge_o, acc_o, acc_o],
        out_shape=[jax.ShapeDtypeStruct(d_qkv.shape, F32), edge_s, main_s, edge_s, edge_s, main_s, edge_s, acc_s, acc_s],
        scratch_shapes=[pltpu.VMEM((tile, LANES), F32), big, big, pltpu.VMEM((tile, LANES), F32), big, big],
        input_output_aliases={14: 0},
        compiler_params=_params(("arbitrary", "arbitrary")),
    )(qkv, qkv, qkv, qkv, qkv, qkv, qkv, tab, tab, tab, gq, gk, d_o, d_lse, d_qkv)
    d_qkv, dkp, dkm, dkn, dvp, dvm, dvn, dgq, dgk = outs
    return d_qkv, (dkp, dkm, dkn), (dvp, dvm, dvn), dgq, dgk


def attn_combine(name, parts, s, dil, d_qkv, col):
    prev_part, main_part, next_part = parts
    tile, halo, _ = _attn_geometry(s, dil)
    nt = s // tile
    cols = D_MODEL // 2

    def body(m_ref, from_prev_ref, from_next_ref, _, o_ref):
        i = pl.program_id(0)
        o_ref[...] = m_ref[...]
        head = o_ref[0:halo, :] + jnp.where(i > 0, from_prev_ref[...], 0.0)
        o_ref[0:halo, :] = head
        tail = o_ref[tile - halo:tile, :] + jnp.where(i < nt - 1, from_next_ref[...], 0.0)
        o_ref[tile - halo:tile, :] = tail

    return pl.pallas_call(
        body, name=name, grid=(nt, D_MODEL // cols),
        in_specs=[pl.BlockSpec((tile, cols), lambda i, c: (i, c)),
                  pl.BlockSpec((None, halo, cols), lambda i, c: (jnp.maximum(i - 1, 0), 0, c)),
                  pl.BlockSpec((None, halo, cols), lambda i, c: (jnp.minimum(i + 1, nt - 1), 0, c)), ANY],
        out_specs=pl.BlockSpec((tile, cols), lambda i, c: (i, (D_MODEL // cols) * col + c)),
        out_shape=jax.ShapeDtypeStruct(d_qkv.shape, F32), input_output_aliases={3: 0},
        compiler_params=_params(("parallel", "parallel")),
    )(main_part, next_part, prev_part, d_qkv)


def attn_merge_fwd(name, os_, lses):
    s = os_[0].shape[0]

    def body(o0, o1, o2, l0, l1, l2, out_ref):
        out_ref[...] = _merge_groups(o0[...], o1[...], o2[...], l0[...], l1[...], l2[...]).astype(out_ref.dtype)

    return _rows_call(name, body, [(a, "row") for a in (*os_, *lses)], [(D_MODEL, BF16, "row")], s, _row_tile(s, 256))[0]


def attn_merge_bwd(name, os_, lses, d_out):
    s = os_[0].shape[0]

    def body(o0, o1, o2, l0, l1, l2, d_ref, *outs):
        _, vjp = jax.vjp(_merge_groups, o0[...], o1[...], o2[...], l0[...], l1[...], l2[...])
        for ref, val in zip(outs, vjp(d_ref[...])):
            ref[...] = val

    return _rows_call(name, body, [(a, "row") for a in (*os_, *lses, d_out)],
                      [(D_MODEL, F32, "row")] * 3 + [(LANES, F32, "row")] * 3, s, _row_tile(s, 256))


def gla_gate_fwd(name, z, wg, bias):
    s = z.shape[0]

    def body(z_ref, w_ref, b_ref, o_ref):
        o_ref[...] = _gate(z_ref[...], w_ref[...], b_ref[...])

    return _rows_call(name, body, [(z, "row"), (wg, "full"), (bias, "full")], [(D_MODEL, F32, "row")], s, _row_tile(s))[0]


def gla_gate_bwd(name, z, wg, bias, d_la_f, d_la_b):
    s = z.shape[0]
    tile = _row_tile(s)

    def body(z_ref, w_ref, b_ref, df_ref, db_ref, dz_ref, dw_ref, dbias_ref):
        _, vjp = jax.vjp(_gate, z_ref[...], w_ref[...], b_ref[...])
        dz, dw, dbias = vjp(jnp.concatenate([df_ref[...], db_ref[...]], axis=1))
        dz_ref[...] = dz

        @pl.when(pl.program_id(0) == 0)
        def _():
            dw_ref[...] = jnp.zeros_like(dw_ref)

        dw_ref[...] += dw
        _acc_rows(dbias_ref, dbias)

    return pl.pallas_call(
        body, name=name, grid=(s // tile,),
        in_specs=[pl.BlockSpec((tile, LANES), lambda i: (i, 0)), pl.BlockSpec(wg.shape, lambda i: (0, 0)),
                  pl.BlockSpec(bias.shape, lambda i: (0, 0)), pl.BlockSpec((tile, 512), lambda i: (i, 0)),
                  pl.BlockSpec((tile, 512), lambda i: (i, 0))],
        out_specs=[pl.BlockSpec((tile, LANES), lambda i: (i, 0)), pl.BlockSpec(wg.shape, lambda i: (0, 0)),
                   pl.BlockSpec((8, D_MODEL), lambda i: (0, 0))],
        out_shape=[jax.ShapeDtypeStruct((s, LANES), F32), jax.ShapeDtypeStruct(wg.shape, F32), jax.ShapeDtypeStruct((8, D_MODEL), F32)],
        compiler_params=_params(("arbitrary",)),
    )(z, wg, bias, d_la_f, d_la_b)


def _gla_in_specs(tile, order, la_col0):
    t = order
    return [pl.BlockSpec((tile, B_KEY_DIM), lambda h, n: (t(n), h)),
            pl.BlockSpec((tile, B_KEY_DIM), lambda h, n: (t(n), B_HEADS + h)),
            pl.BlockSpec((tile, B_VAL_DIM), lambda h, n: (t(n), B_HEADS + h)),
            pl.BlockSpec((tile, B_KEY_DIM), lambda h, n: (t(n), la_col0 + h))]


def gla_fwd(name, proj, la, reverse):
    s = proj.shape[0]
    tile = _row_tile(s)
    nt = s // tile
    order = (lambda n: nt - 1 - n) if reverse else (lambda n: n)

    def body(q_ref, k_ref, v_ref, la_ref, o_ref, st_ref, st_scr):
        @pl.when(pl.program_id(1) == 0)
        def _():
            st_scr[...] = jnp.zeros_like(st_scr)

        st_ref[...] = st_scr[...]
        o, st = _gla_tile(q_ref[...], k_ref[...], v_ref[...], la_ref[...], st_scr[...], reverse)
        o_ref[...] = o
        st_scr[...] = st

    return pl.pallas_call(
        body, name=name, grid=(B_HEADS, nt), in_specs=_gla_in_specs(tile, order, B_HEADS if reverse else 0),
        out_specs=[pl.BlockSpec((tile, B_VAL_DIM), lambda h, n: (order(n), h)),
                   pl.BlockSpec((None, None, B_VAL_DIM, B_KEY_DIM), lambda h, n: (h, order(n), 0, 0))],
        out_shape=[jax.ShapeDtypeStruct((s, D_MODEL), F32), jax.ShapeDtypeStruct((B_HEADS, nt, B_VAL_DIM, B_KEY_DIM), F32)],
        scratch_shapes=[pltpu.VMEM((B_VAL_DIM, B_KEY_DIM), F32)],
        compiler_params=_params(("parallel", "arbitrary")),
    )(proj, proj, proj, la)


def gla_bwd(name, proj, la, states, d_o, reverse, prev=None):
    s = proj.shape[0]
    tile = _row_tile(s)
    nt = s // tile
    order = (lambda n: n) if reverse else (lambda n: nt - 1 - n)

    def body(*refs):
        q_ref, k_ref, v_ref, la_ref, st_ref, do_ref = refs[:6]
        rest = refs[6:]
        prev_refs = rest[:3] if prev is not None else None
        dq_ref, dk_ref, dv_ref, dla_ref, dst_scr = rest[3:] if prev is not None else rest

        @pl.when(pl.program_id(1) == 0)
        def _():
            dst_scr[...] = jnp.zeros_like(dst_scr)

        _, vjp = jax.vjp(functools.partial(_gla_tile, reverse=reverse), q_ref[...], k_ref[...], v_ref[...], la_ref[...], st_ref[...])
        dq, dk, dv, dla, dst = vjp((do_ref[...], dst_scr[...]))
        if prev_refs is not None:
            dq, dk, dv = dq + prev_refs[0][...], dk + prev_refs[1][...], dv + prev_refs[2][...]
        dq_ref[...], dk_ref[...], dv_ref[...], dla_ref[...] = dq, dk, dv, dla
        dst_scr[...] = dst

    key_spec = pl.BlockSpec((tile, B_KEY_DIM), lambda h, n: (order(n), h))
    val_spec = pl.BlockSpec((tile, B_VAL_DIM), lambda h, n: (order(n), h))
    in_specs = _gla_in_specs(tile, order, B_HEADS if reverse else 0) + [
        pl.BlockSpec((None, None, B_VAL_DIM, B_KEY_DIM), lambda h, n: (h, order(n), 0, 0)), val_spec]
    ins = [proj, proj, proj, la, states, d_o]
    if prev is not None:
        in_specs += [key_spec, key_spec, val_spec]
        ins += list(prev)
    return pl.pallas_call(
        body, name=name, grid=(B_HEADS, nt), in_specs=in_specs,
        out_specs=[key_spec, key_spec, val_spec, key_spec],
        out_shape=[jax.ShapeDtypeStruct((s, 512), F32), jax.ShapeDtypeStruct((s, 512), F32),
                   jax.ShapeDtypeStruct((s, D_MODEL), F32), jax.ShapeDtypeStruct((s, 512), F32)],
        scratch_shapes=[pltpu.VMEM((B_VAL_DIM, B_KEY_DIM), F32)],
        compiler_params=_params(("parallel", "arbitrary")),
    )(*ins)


def _r_spec(tile):
    return pl.BlockSpec((tile, D_MODEL), lambda i: (i, 2))


def gla_post_fwd(name, o_f, o_b, proj, gain):
    s = o_f.shape[0]
    tile = _row_tile(s)

    def body(of_ref, ob_ref, r_ref, g_ref, out_ref):
        out_ref[...] = _gla_post(of_ref[...], ob_ref[...], r_ref[...], g_ref[...]).astype(out_ref.dtype)

    row = pl.BlockSpec((tile, D_MODEL), lambda i: (i, 0))
    return pl.pallas_call(body, name=name, grid=(s // tile,),
                          in_specs=[row, row, _r_spec(tile), pl.BlockSpec(gain.shape, lambda i: (0, 0))], out_specs=row,
                          out_shape=jax.ShapeDtypeStruct((s, D_MODEL), BF16), compiler_params=_params(("parallel",)))(o_f, o_b, proj, gain)


def gla_post_bwd(name, o_f, o_b, proj, gain, d_out):
    s = o_f.shape[0]
    tile = _row_tile(s)

    def body(of_ref, ob_ref, r_ref, g_ref, d_ref, do_ref, dr_ref, dg_ref):
        _, vjp = jax.vjp(_gla_post, of_ref[...], ob_ref[...], r_ref[...], g_ref[...])
        d_of, _, dr, dg = vjp(d_ref[...])
        do_ref[...] = d_of
        dr_ref[...] = dr
        _acc_rows(dg_ref, dg)

    row = pl.BlockSpec((tile, D_MODEL), lambda i: (i, 0))
    return pl.pallas_call(
        body, name=name, grid=(s // tile,),
        in_specs=[row, row, _r_spec(tile), pl.BlockSpec(gain.shape, lambda i: (0, 0)), row],
        out_specs=[row, row, pl.BlockSpec((8, D_MODEL), lambda i: (0, 0))],
        out_shape=[jax.ShapeDtypeStruct((s, D_MODEL), F32), jax.ShapeDtypeStruct((s, D_MODEL), F32), jax.ShapeDtypeStruct((8, D_MODEL), F32)],
        compiler_params=_params(("arbitrary",)))(o_f, o_b, proj, gain, d_out)


def _hid(tile, where):
    return pl.BlockSpec((None, tile, FFN_BLK), where)


def _pair(tile, where):
    return pl.BlockSpec((2, None, tile, FFN_BLK), where)


def _w_gu_spec(layer, where_j):
    return pl.BlockSpec((2, None, None, D_MODEL, FFN_BLK), lambda *g: (0, where_j(*g), layer, 0, 0))


def ffn_fwd(name, h_mid, hn2, w_gu, w_down, layer):
    s = hn2.shape[0]
    tm = _row_tile(s, 1024)
    nt = s // tm

    def gu_body(x_ref, w_ref, gu_ref, act_ref):
        x = x_ref[...]
        g = _dot(x, w_ref[0], NN)
        u = _dot(x, w_ref[1], NN)
        gu_ref[0] = g
        gu_ref[1] = u
        act_ref[...] = _swiglu_act(g, u).astype(act_ref.dtype)

    gu, act = pl.pallas_call(
        gu_body, name=name + "_gu", grid=(4, nt),
        in_specs=[pl.BlockSpec((tm, D_MODEL), lambda j, i: (i, 0)), _w_gu_spec(layer, lambda j, i: j)],
        out_specs=[_pair(tm, lambda j, i: (0, j, i, 0)), _hid(tm, lambda j, i: (j, i, 0))],
        out_shape=[jax.ShapeDtypeStruct((2, 4, s, FFN_BLK), F32), jax.ShapeDtypeStruct((4, s, FFN_BLK), BF16)],
        compiler_params=_params(("parallel", "parallel")))(hn2, w_gu)
    row = pl.BlockSpec((tm, D_MODEL), lambda i, j: (i, 0))
    h_next = _matmul(name + "_down", act, w_down, dims=NN, grid=(nt, 4), red_axis=1,
                     a_spec=_hid(tm, lambda i, j: (j, i, 0)), b_spec=pl.BlockSpec((None, FFN_BLK, D_MODEL), lambda i, j: (j, 0, 0)),
                     o_spec=row, out_shape=jax.ShapeDtypeStruct((s, D_MODEL), F32), res=h_mid, res_spec=row)
    return h_next, gu, act


def ffn_bwd(name, dh_next, hn2, gu, act, w_gu, w_down, layer):
    s = hn2.shape[0]
    tm = _row_tile(s, 1024)
    nt = s // tm
    d_wd = _matmul(name + "_dwd", act, dh_next, dims=TN, grid=(4, nt), red_axis=1,
                   a_spec=_hid(tm, lambda j, i: (j, i, 0)), b_spec=pl.BlockSpec((tm, D_MODEL), lambda j, i: (i, 0)),
                   o_spec=pl.BlockSpec((None, FFN_BLK, D_MODEL), lambda j, i: (j, 0, 0)),
                   out_shape=jax.ShapeDtypeStruct((4, FFN_BLK, D_MODEL), BF16), acc_shape=(FFN_BLK, D_MODEL))

    def dgu_body(dy_ref, wd_ref, gu_ref, dgu_ref):
        d_act = _dot(dy_ref[...], wd_ref[...], NT)
        _, vjp = jax.vjp(_swiglu_act, gu_ref[0], gu_ref[1])
        dg, du = vjp(d_act)
        dgu_ref[0] = dg.astype(dgu_ref.dtype)
        dgu_ref[1] = du.astype(dgu_ref.dtype)

    d_gu = pl.pallas_call(
        dgu_body, name=name + "_dgu", grid=(4, nt),
        in_specs=[pl.BlockSpec((tm, D_MODEL), lambda j, i: (i, 0)), pl.BlockSpec((None, FFN_BLK, D_MODEL), lambda j, i: (j, 0, 0)),
                  _pair(tm, lambda j, i: (0, j, i, 0))],
        out_specs=_pair(tm, lambda j, i: (0, j, i, 0)), out_shape=jax.ShapeDtypeStruct((2, 4, s, FFN_BLK), BF16),
        compiler_params=_params(("parallel", "parallel")))(dh_next, w_down, gu)

    def dx_body(d_ref, w_ref, o_ref):
        prod = _dot(d_ref[0], w_ref[0], NT) + _dot(d_ref[1], w_ref[1], NT)

        @pl.when(pl.program_id(1) == 0)
        def _():
            o_ref[...] = prod

        @pl.when(pl.program_id(1) > 0)
        def _():
            o_ref[...] += prod

    d_hn2 = pl.pallas_call(
        dx_body, name=name + "_dx", grid=(nt, 4),
        in_specs=[_pair(tm, lambda i, j: (0, j, i, 0)), _w_gu_spec(layer, lambda i, j: j)],
        out_specs=pl.BlockSpec((tm, D_MODEL), lambda i, j: (i, 0)), out_shape=jax.ShapeDtypeStruct((s, D_MODEL), F32),
        compiler_params=_params(("parallel", "arbitrary")))(d_gu, w_gu)

    def dw_body(x_ref, d_ref, o_ref, acc_ref):
        x = x_ref[...]
        k = pl.program_id(1)
        for t in range(2):
            prod = _dot(x, d_ref[t], TN)

            @pl.when(k == 0)
            def _():
                acc_ref[t] = prod

            @pl.when(k > 0)
            def _():
                acc_ref[t] += prod

        @pl.when(k == nt - 1)
        def _():
            o_ref[...] = acc_ref[...].astype(o_ref.dtype)

    d_wgu = pl.pallas_call(
        dw_body, name=name + "_dwgu", grid=(4, nt),
        in_specs=[pl.BlockSpec((tm, D_MODEL), lambda j, i: (i, 0)), _pair(tm, lambda j, i: (0, j, i, 0))],
        out_specs=pl.BlockSpec((2, None, D_MODEL, FFN_BLK), lambda j, i: (0, j, 0, 0)),
        out_shape=jax.ShapeDtypeStruct((2, 4, D_MODEL, FFN_BLK), BF16),
        scratch_shapes=[pltpu.VMEM((2, D_MODEL, FFN_BLK), F32)],
        compiler_params=_params(("parallel", "arbitrary")))(hn2, d_gu)
    return d_hn2, d_wgu, d_wd


def _my_place():
    return lax.axis_index("x"), lax.axis_index("y"), lax.axis_index("c")


def _flip(place, k):
    x, y, c = place
    return (1 - x if k & 4 else x, 1 - y if k & 2 else y, 1 - c if k & 1 else c)


def _index(place):
    return 4 * place[0] + 2 * place[1] + place[2]


def all_gather(arrs):
    n = len(arrs)

    def body(*refs):
        ins, outs = refs[:n], refs[n:2 * n]
        send_sems, recv_sems, local_sems = refs[2 * n:]
        me = _my_place()
        sibling = _flip(me, 1)
        chips = (4, 2, 6)

        def copy(a, k, block, to, src=None):
            dst = outs[a].at[_index(block)]
            return pltpu.make_async_remote_copy(src_ref=dst if src is None else src, dst_ref=dst, send_sem=send_sems.at[a, k],
                                                recv_sem=recv_sems.at[a, k], device_id=to, device_id_type=MESH)

        started = []
        for a in range(n):
            mine = pltpu.make_async_copy(ins[a], outs[a].at[_index(me)], local_sems.at[a])
            mine.start()
            started.append(mine)
        first = []
        for a in range(n):
            first.append(copy(a, 0, me, sibling, src=ins[a]))
            first += [copy(a, 1 + j, me, _flip(me, k), src=ins[a]) for j, k in enumerate(chips)]
        for cp in first:
            cp.start()
        passed = []
        for a in range(n):
            for j, k in enumerate(chips):
                copy(a, 1 + j, _flip(me, k), me).wait_recv()
                fwd = copy(a, 4 + j, _flip(me, k), sibling)
                fwd.start()
                passed.append(fwd)
        for a in range(n):
            copy(a, 0, sibling, me).wait_recv()
            for j, k in enumerate(chips):
                copy(a, 4 + j, _flip(sibling, k), me).wait_recv()
        for cp in first + passed:
            cp.wait_send()
        for cp in started:
            cp.wait()

    return pl.pallas_call(
        body, name="all_gather_weights", in_specs=[ANY] * n, out_specs=[ANY] * n,
        out_shape=[jax.ShapeDtypeStruct((N_DEV,) + a.shape, a.dtype) for a in arrs],
        scratch_shapes=[pltpu.SemaphoreType.DMA((n, 7)), pltpu.SemaphoreType.DMA((n, 7)), pltpu.SemaphoreType.DMA((n,))],
    )(*arrs)


def exchange_partials(arrs):
    n = len(arrs)

    def body(*refs):
        ins, outs = refs[:n], refs[n:2 * n]
        send_sems, recv_sems, local_sems = refs[2 * n:]
        me = _my_place()
        local = []
        for a in range(n):
            cp = pltpu.make_async_copy(ins[a].at[_index(me)], outs[a].at[_index(me)], local_sems.at[a])
            cp.start()
            local.append(cp)

        def copy(a, k, src_block, dst_block):
            return pltpu.make_async_remote_copy(src_ref=ins[a].at[_index(src_block)], dst_ref=outs[a].at[_index(dst_block)],
                                                send_sem=send_sems.at[a, k - 1], recv_sem=recv_sems.at[a, k - 1],
                                                device_id=_flip(me, k), device_id_type=MESH)

        sent = []
        for a in range(n):
            for k in range(1, N_DEV):
                cp = copy(a, k, _flip(me, k), me)
                cp.start()
                sent.append(cp)
        for a in range(n):
            for k in range(1, N_DEV):
                copy(a, k, me, _flip(me, k)).wait_recv()
        for cp in sent:
            cp.wait_send()
        for cp in local:
            cp.wait()

    return pl.pallas_call(
        body, name="exchange_weight_grads", in_specs=[ANY] * n, out_specs=[ANY] * n,
        out_shape=[jax.ShapeDtypeStruct(a.shape, a.dtype) for a in arrs],
        scratch_shapes=[pltpu.SemaphoreType.DMA((n, 7)), pltpu.SemaphoreType.DMA((n, 7)), pltpu.SemaphoreType.DMA((n,))],
    )(*arrs)


def adamw_shard(name, parts, w, m, v, layer, tile):
    rows, cols = parts.shape[1:]
    assert rows % tile == 0
    off = layer * (rows // tile)

    def body(p_ref, w_ref, m_ref, v_ref, g_ref, d_ref, nm_ref, nv_ref):
        g = p_ref[0].astype(F32)
        for src in range(1, N_DEV):
            g = g + p_ref[src].astype(F32)
        g_ref[...] = g
        d_ref[...], nm_ref[...], nv_ref[...] = _adamw(w_ref[...], g, m_ref[...], v_ref[...])

    src_row = pl.BlockSpec((tile, cols), lambda i: (off + i, 0))
    row = pl.BlockSpec((tile, cols), lambda i: (i, 0))
    shape = jax.ShapeDtypeStruct((rows, cols), F32)
    return pl.pallas_call(body, name=name, grid=(rows // tile,),
                          in_specs=[pl.BlockSpec((N_DEV, tile, cols), lambda i: (0, i, 0)), src_row, src_row, src_row],
                          out_specs=[row] * 4, out_shape=[shape] * 4, compiler_params=_params(("parallel",)))(parts, w, m, v)


def allreduce_adamw_replicated(partial, w, m, v, n_loss_rows):
    rows = partial.shape[0]

    def body(p_ref, w_ref, m_ref, v_ref, g_ref, d_ref, nm_ref, nv_ref, loss_ref, recv_ref, send_sems, recv_sems):
        me = _my_place()
        recv_ref[_index(me)] = p_ref[...]
        copies = []
        for k in range(1, N_DEV):
            peer = _flip(me, k)
            cp = pltpu.make_async_remote_copy(src_ref=p_ref, dst_ref=recv_ref.at[_index(me)], send_sem=send_sems.at[k - 1],
                                              recv_sem=recv_sems.at[k - 1], device_id=peer, device_id_type=MESH)
            cp.start()
            copies.append((cp, peer))
        for k, (cp, peer) in enumerate(copies):
            pltpu.make_async_remote_copy(src_ref=p_ref, dst_ref=recv_ref.at[_index(peer)], send_sem=send_sems.at[k],
                                         recv_sem=recv_sems.at[k], device_id=peer, device_id_type=MESH).wait_recv()
        for cp, _ in copies:
            cp.wait_send()
        g = recv_ref[0]
        for src in range(1, N_DEV):
            g = g + recv_ref[src]
        g_ref[...] = g
        d_ref[...], nm_ref[...], nv_ref[...] = _adamw(w_ref[...], g, m_ref[...], v_ref[...])
        loss = (0.5 / D_MODEL) * jnp.sum(g[rows - n_loss_rows:, :])
        loss_ref[...] = jnp.full(loss_ref.shape, loss, F32)

    shape = jax.ShapeDtypeStruct((rows, LANES), F32)
    return pl.pallas_call(
        body, name="allreduce_adamw_replicated", in_specs=[VMEM_SPEC] * 4, out_specs=[VMEM_SPEC] * 5,
        out_shape=[shape] * 4 + [jax.ShapeDtypeStruct((8, LANES), F32)],
        scratch_shapes=[pltpu.VMEM((N_DEV, rows, LANES), F32), pltpu.SemaphoreType.DMA((7,)), pltpu.SemaphoreType.DMA((7,))],
    )(partial, w, m, v)


def _pack_rows(flat, cols):
    n = flat.shape[-1]
    rows = -(-n // cols)
    rows = -(-rows // 48) * 48
    flat = jnp.pad(flat, [(0, 0)] * (flat.ndim - 1) + [(0, rows * cols - n)])
    return flat.reshape(flat.shape[:-1] + (rows, cols))


def _unpack(flat, shapes):
    out, off = [], 0
    for shp in shapes:
        n = 1
        for d in shp:
            n *= d
        out.append(flat[off:off + n].reshape(shp))
        off += n
    return out


def _to_dev_cols(a):
    w = a.shape[-1] // N_DEV
    return jnp.moveaxis(a.reshape(a.shape[:-1] + (N_DEV, w)), -2, 0)


def _from_dev_cols(a):
    a = jnp.moveaxis(a, 0, -2)
    return a.reshape(a.shape[:-2] + (a.shape[-2] * a.shape[-1],))


def kernel(x, attn_norm, ffn_norm, a_w_in, a_q_norm, a_k_norm, a_w_out, b_w_in, b_w_gate_f, b_gate_bias_f, b_w_gate_b, b_gate_bias_b, b_out_norm, b_w_out, ffn_w_gate_up, ffn_w_down, loss_target, m_attn_norm, m_ffn_norm, m_a_w_in, m_a_q_norm, m_a_k_norm, m_a_w_out, m_b_w_in, m_b_w_gate_f, m_b_gate_bias_f, m_b_w_gate_b, m_b_gate_bias_b, m_b_out_norm, m_b_w_out, m_ffn_w_gate_up, m_ffn_w_down, v_attn_norm, v_ffn_norm, v_a_w_in, v_a_q_norm, v_a_k_norm, v_a_w_out, v_b_w_in, v_b_w_gate_f, v_b_gate_bias_f, v_b_w_gate_b, v_b_gate_bias_b, v_b_out_norm, v_b_w_out, v_ffn_w_gate_up, v_ffn_w_down):
    seq = x.shape[1]
    depth = attn_norm.shape[0]
    h = x.reshape(seq, D_MODEL)
    target = loss_target.reshape(seq, D_MODEL)
    n_a, n_b = a_w_in.shape[0], b_w_in.shape[0]

    small = jnp.concatenate([t.reshape(-1) for t in (b_w_gate_f, b_gate_bias_f, b_w_gate_b, b_gate_bias_b, b_out_norm)])
    small = _pack_rows(small, LANES)
    g_a_in, g_a_out, g_b_in, g_b_out, g_gu, g_down, g_small = all_gather(
        [a_w_in.astype(BF16), a_w_out.astype(BF16), b_w_in.astype(BF16), b_w_out.astype(BF16),
         ffn_w_gate_up.astype(BF16), ffn_w_down.astype(BF16), small])
    w_a_in = _from_dev_cols(g_a_in)
    w_a_out = jnp.moveaxis(g_a_out, 0, 1).reshape(n_a, D_MODEL, D_MODEL)
    w_b_in = _from_dev_cols(g_b_in)
    w_b_out = jnp.moveaxis(g_b_out, 0, 1).reshape(n_b, D_MODEL, D_MODEL)
    w_down = jnp.moveaxis(g_down, 0, 1).reshape(depth, 4, FFN_BLK, D_MODEL)
    w_gu = g_gu.reshape(2, 4, depth, D_MODEL, FFN_BLK)
    small_shapes = [t.shape for t in (b_w_gate_f, b_gate_bias_f, b_w_gate_b, b_gate_bias_b, b_out_norm)]
    per_dev = [_unpack(g_small[d].reshape(-1), small_shapes) for d in range(N_DEV)]
    wgf, bgf, wgb, bgb, onorm = [_from_dev_cols(jnp.stack([per_dev[d][t] for d in range(N_DEV)])) for t in range(5)]
    w_gate = jnp.zeros((n_b, LANES, D_MODEL), F32)
    w_gate = w_gate.at[:, 0:16, 0:512].set(wgf).at[:, 16:32, 512:1024].set(wgb)
    gate_bias = jnp.concatenate([bgf, bgb], axis=1).reshape(n_b, 1, D_MODEL)
    out_gain = onorm.reshape(n_b, 1, D_MODEL)
    w_b_main = w_b_in[:, :, :3072]
    w_b_z = jnp.pad(w_b_in[:, :, 3072:], ((0, 0), (0, 0), (0, LANES - 32)))

    dh, sq_err, grads = _forward_backward(h, target, attn_norm, ffn_norm, a_q_norm, a_k_norm, w_a_in, w_a_out, w_b_main, w_b_z,
                                          w_gate, gate_bias, out_gain, w_b_out, w_gu, w_down)
    sharded_w = (a_w_in, a_w_out, b_w_in, b_w_gate_f, b_gate_bias_f, b_w_gate_b, b_gate_bias_b, b_out_norm, b_w_out, ffn_w_gate_up, ffn_w_down)
    sharded_m = (m_a_w_in, m_a_w_out, m_b_w_in, m_b_w_gate_f, m_b_gate_bias_f, m_b_w_gate_b, m_b_gate_bias_b, m_b_out_norm, m_b_w_out, m_ffn_w_gate_up, m_ffn_w_down)
    sharded_v = (v_a_w_in, v_a_w_out, v_b_w_in, v_b_w_gate_f, v_b_gate_bias_f, v_b_w_gate_b, v_b_gate_bias_b, v_b_out_norm, v_b_w_out, v_ffn_w_gate_up, v_ffn_w_down)
    rep_w = (attn_norm, ffn_norm, a_q_norm, a_k_norm)
    rep_m = (m_attn_norm, m_ffn_norm, m_a_q_norm, m_a_k_norm)
    rep_v = (v_attn_norm, v_ffn_norm, v_a_q_norm, v_a_k_norm)
    loss, outs = _reduce_and_update(grads, sq_err, sharded_w, sharded_m, sharded_v, rep_w, rep_m, rep_v)
    return (loss, dh.reshape(x.shape), *outs)


def _forward_backward(h, target, attn_norm, ffn_norm, a_q_norm, a_k_norm, w_a_in, w_a_out, w_b_main, w_b_z, w_gate, gate_bias,
                      out_gain, w_b_out, w_gu, w_down):
    seq = h.shape[0]
    depth = attn_norm.shape[0]
    n_a, n_b = w_a_in.shape[0], w_b_main.shape[0]
    tab = _rope_table(seq)
    pair_gain = lambda g: jnp.concatenate([g, g]).reshape(1, LANES)

    saved = []
    for i in range(depth):
        j = i // 2
        nm = f"l{i}"
        hn = rmsnorm_fwd(nm + "_norm1", h, attn_norm[i].reshape(1, D_MODEL))
        if i % 2 == 0:
            qkv = mm_nn(nm + "_qkv", hn, w_a_in[j])
            os_, lses = [], []
            for g, (_, dil) in enumerate(A_GROUPS):
                o, lse = attn_fwd(f"{nm}_attn{g}", qkv, tab, pair_gain(a_q_norm[j, g]), pair_gain(a_k_norm[j, g]), g, dil)
                os_.append(o)
                lses.append(lse)
            mixed = attn_merge_fwd(nm + "_merge", os_, lses)
            h_mid = mm_nn(nm + "_out", mixed, w_a_out[j], res=h)
            mix_saved = (qkv, os_, lses, mixed)
        else:
            proj = mm_nn(nm + "_proj", hn, w_b_main[j])
            z = mm_nn(nm + "_z", hn, w_b_z[j])
            la = gla_gate_fwd(nm + "_gate", z, w_gate[j], gate_bias[j])
            o_f, st_f = gla_fwd(nm + "_gla_f", proj, la, False)
            o_b, st_b = gla_fwd(nm + "_gla_b", proj, la, True)
            mixed = gla_post_fwd(nm + "_post", o_f, o_b, proj, out_gain[j])
            h_mid = mm_nn(nm + "_out", mixed, w_b_out[j], res=h)
            mix_saved = (proj, z, la, o_f, st_f, o_b, st_b, mixed)
        hn2 = rmsnorm_fwd(nm + "_norm2", h_mid, ffn_norm[i].reshape(1, D_MODEL))
        h_next, gu, act = ffn_fwd(nm + "_ffn", h_mid, hn2, w_gu, w_down[i], i)
        saved.append((h, hn, mix_saved, h_mid, hn2, gu, act))
        h = h_next

    dh, sq_err = loss_and_grad(h, target)

    g_attn_norm, g_ffn_norm = [None] * depth, [None] * depth
    g_a_w_in, g_a_w_out, g_a_q, g_a_k = [None] * n_a, [None] * n_a, [None] * n_a, [None] * n_a
    g_b_w_in, g_b_w_out, g_w_gate, g_gate_bias, g_out_gain = ([None] * n_b for _ in range(5))
    g_w_gu, g_w_down = [None] * depth, [None] * depth
    for i in reversed(range(depth)):
        j = i // 2
        nm = f"l{i}b"
        h_in, hn, mix_saved, h_mid, hn2, gu, act = saved[i]
        d_hn2, g_w_gu[i], g_w_down[i] = ffn_bwd(nm + "_ffn", dh, hn2, gu, act, w_gu, w_down[i], i)
        dh_mid, g_ffn_norm[i] = rmsnorm_bwd(nm + "_norm2", h_mid, ffn_norm[i].reshape(1, D_MODEL), d_hn2, dh)
        if i % 2 == 0:
            qkv, os_, lses, mixed = mix_saved
            d_mixed = mm_nt(nm + "_dmixed", dh_mid, w_a_out[j])
            g_a_w_out[j] = mm_tn(nm + "_dwout", mixed, dh_mid, by_block=True)
            d_parts = attn_merge_bwd(nm + "_merge", os_, lses, d_mixed)
            gq_l, gk_l = [], []
            d_qkv = lax.empty(qkv.shape, F32)
            for g, (_, dil) in enumerate(A_GROUPS):
                d_qkv, dk_parts, dv_parts, dgq, dgk = attn_bwd(f"{nm}_attn{g}", qkv, tab, pair_gain(a_q_norm[j, g]),
                                                               pair_gain(a_k_norm[j, g]), d_parts[g], d_parts[3 + g], g, dil, d_qkv)
                d_qkv = attn_combine(f"{nm}_dk{g}", dk_parts, seq, dil, d_qkv, 3 * g + 1)
                d_qkv = attn_combine(f"{nm}_dv{g}", dv_parts, seq, dil, d_qkv, 3 * g + 2)
                gq_l.append(dgq[0, :A_HEAD_DIM])
                gk_l.append(dgk[0, :A_HEAD_DIM])
            g_a_q[j], g_a_k[j] = jnp.stack(gq_l), jnp.stack(gk_l)
            d_hn = mm_nt(nm + "_dhn", d_qkv, w_a_in[j])
            g_a_w_in[j] = mm_tn(nm + "_dwin", hn, d_qkv, tn=D_MODEL * 9 // N_DEV, by_block=True)
        else:
            proj, z, la, o_f, st_f, o_b, st_b, mixed = mix_saved
            d_mixed = mm_nt(nm + "_dmixed", dh_mid, w_b_out[j])
            g_b_w_out[j] = mm_tn(nm + "_dwout", mixed, dh_mid, by_block=True)
            d_o, d_r, dgain = gla_post_bwd(nm + "_post", o_f, o_b, proj, out_gain[j], d_mixed)
            g_out_gain[j] = dgain[0]
            dq, dk, dv, dla_f = gla_bwd(nm + "_gla_f", proj, la, st_f, d_o, False)
            dq, dk, dv, dla_b = gla_bwd(nm + "_gla_b", proj, la, st_b, d_o, True, prev=(dq, dk, dv))
            d_z, g_w_gate[j], dbias = gla_gate_bwd(nm + "_gate", z, w_gate[j], gate_bias[j], dla_f, dla_b)
            g_gate_bias[j] = dbias[0]
            d_proj = jnp.concatenate([dq, dk, dv, d_r], axis=1)
            d_hn = mm_nt(nm + "_dhn_z", d_z, w_b_z[j])
            d_hn = mm_nt(nm + "_dhn", d_proj, w_b_main[j], res=d_hn)
            g_b_w_in[j] = jnp.concatenate([mm_tn(nm + "_dwin", hn, d_proj), mm_tn(nm + "_dwz", hn, d_z)[:, :32]], axis=1)
        dh, g_attn_norm[i] = rmsnorm_bwd(nm + "_norm1", h_in, attn_norm[i].reshape(1, D_MODEL), d_hn, dh_mid)
    return dh, sq_err, (g_attn_norm, g_ffn_norm, g_a_w_in, g_a_w_out, g_a_q, g_a_k, g_b_w_in, g_b_w_out, g_w_gate, g_gate_bias,
                        g_out_gain, g_w_gu, g_w_down)


def _reduce_and_update(grads, sq_err, sharded_w, sharded_m, sharded_v, rep_w, rep_m, rep_v):
    (g_attn_norm, g_ffn_norm, g_a_w_in, g_a_w_out, g_a_q, g_a_k, g_b_w_in, g_b_w_out, g_w_gate, g_gate_bias, g_out_gain,
     g_w_gu, g_w_down) = grads
    depth, n_a, n_b = len(g_w_gu), len(g_a_w_in), len(g_b_w_in)

    g_w_gate = jnp.stack(g_w_gate)
    g_gate_bias = jnp.stack(g_gate_bias)
    small_parts = [_to_dev_cols(g_w_gate[:, 0:16, 0:512]), _to_dev_cols(g_gate_bias[:, 0:512]),
                   _to_dev_cols(g_w_gate[:, 16:32, 512:1024]), _to_dev_cols(g_gate_bias[:, 512:1024]),
                   _to_dev_cols(jnp.stack(g_out_gain).reshape(n_b, B_HEADS, B_VAL_DIM))]
    small_part = _pack_rows(jnp.concatenate([t.reshape(N_DEV, -1) for t in small_parts], axis=1), LANES).astype(BF16)
    families = [
        (0, g_a_w_in, 256),
        (1, [t.reshape(N_DEV, -1, D_MODEL) for t in g_a_w_out], 128),
        (2, [_to_dev_cols(t).astype(BF16) for t in g_b_w_in], 256),
        (8, [t.reshape(N_DEV, -1, D_MODEL) for t in g_b_w_out], 128),
        (9, [t.reshape(N_DEV, D_MODEL, FFN_BLK) for t in g_w_gu], 256),
        (10, [t.reshape(N_DEV, -1, D_MODEL) for t in g_w_down], 176),
    ]
    flat_parts = [p for _, parts, _ in families for p in parts] + [small_part]
    received = exchange_partials(flat_parts)
    sh_out = [None] * len(sharded_w)
    pos = 0
    for fam, parts, tile in families:
        w = sharded_w[fam]
        two_d = lambda t: t.reshape(-1, t.shape[-1])
        per_layer = []
        for layer in range(len(parts)):
            per_layer.append(adamw_shard(f"adamw_p{fam}_l{layer}", received[pos], two_d(w), two_d(sharded_m[fam]),
                                         two_d(sharded_v[fam]), layer, tile))
            pos += 1
        sh_out[fam] = [jnp.stack([per_layer[l][t] for l in range(len(parts))]).reshape(w.shape) for t in range(4)]
    small_ids = (3, 4, 5, 6, 7)
    pack_small = lambda ts: _pack_rows(jnp.concatenate([ts[i].reshape(-1) for i in small_ids]), LANES)
    small_out = adamw_shard("adamw_small", received[pos], pack_small(sharded_w), pack_small(sharded_m), pack_small(sharded_v), 0, 48)
    small_shapes = [sharded_w[i].shape for i in small_ids]
    for t in range(4):
        for i, val in zip(small_ids, _unpack(small_out[t].reshape(-1), small_shapes)):
            if sh_out[i] is None:
                sh_out[i] = [None] * 4
            sh_out[i][t] = val
    sh_grad, sh_delta, sh_m, sh_v = [[sh_out[i][t] for i in range(len(sharded_w))] for t in range(4)]

    rep_g = (jnp.stack([t[0] for t in g_attn_norm]), jnp.stack([t[0] for t in g_ffn_norm]), jnp.stack(g_a_q), jnp.stack(g_a_k))
    n_rep = sum(t.size for t in rep_w)
    n_rep_rows = -(-n_rep // (8 * LANES)) * 8
    n_loss_rows = 8 * D_MODEL // LANES

    def pack_rep(ts, tail):
        flat = jnp.concatenate([t.reshape(-1) for t in ts])
        flat = jnp.pad(flat, (0, n_rep_rows * LANES - n_rep))
        return jnp.concatenate([flat.reshape(n_rep_rows, LANES), tail], axis=0)

    zeros_tail = jnp.zeros((n_loss_rows, LANES), F32)
    rep_out = allreduce_adamw_replicated(pack_rep(rep_g, sq_err.reshape(n_loss_rows, LANES)), pack_rep(rep_w, zeros_tail),
                                         pack_rep(rep_m, zeros_tail), pack_rep(rep_v, zeros_tail + 1.0), n_loss_rows)
    rep_shapes = [t.shape for t in rep_w]
    r_grad, r_delta, r_m, r_v = [_unpack(p.reshape(-1), rep_shapes) for p in rep_out[:4]]
    loss = rep_out[4][0, 0]

    def ordered(rep, sh):
        return [rep[0], rep[1], sh[0], rep[2], rep[3]] + list(sh[1:])

    return loss, (*ordered(r_grad, sh_grad), *ordered(r_delta, sh_delta), *ordered(r_m, sh_m), *ordered(r_v, sh_v))
```

```python
import functools

import jax
import jax.numpy as jnp
from jax import lax
from jax.experimental import pallas as pl
from jax.experimental.pallas import tpu as pltpu

F32 = jnp.float32
BF16 = jnp.bfloat16
MXU_DTYPE = jnp.bfloat16

D_MODEL = 1024
N_DEV = 8
RMS_EPS = 1e-6
NEG_INF = -1e30
A_GROUPS = ((128, 1), (512, 4), (2048, 16))
A_HEAD_DIM = 64
A_HALF = 64
ATT_T = 128
ATT_TILE = 2048
B_HEADS = 4
B_KEY_DIM = 128
B_VAL_DIM = 256
B_CHUNK = 64
B_GATE_TAU = 16.0
FFN_HIDDEN = 2816
FFN_BLK = 2 * FFN_HIDDEN // N_DEV
ADAM_LR, ADAM_B1, ADAM_B2, ADAM_EPS, ADAM_WD, ADAM_STEP = 0.001, 0.9, 0.999, 1e-08, 0.01, 10
ROPE_THETA = 10000.0

V7X_VMEM_LIMIT = 56 * 1024 * 1024
LANES = 128
MESH = pl.DeviceIdType.MESH
ANY = pl.BlockSpec(memory_space=pl.ANY)
VMEM_SPEC = pl.BlockSpec(memory_space=pltpu.VMEM)

NN = ((1,), (0,))
NT = ((1,), (1,))
TN = ((0,), (0,))


def _dot(a, b, dims):
    return lax.dot_general(a.astype(MXU_DTYPE), b.astype(MXU_DTYPE), (dims, ((), ())), preferred_element_type=F32)


@jax.custom_vjp
def dot_nn(a, b):
    return _dot(a, b, NN)


@jax.custom_vjp
def dot_nt(a, b):
    return _dot(a, b, NT)


@jax.custom_vjp
def dot_tn(a, b):
    return _dot(a, b, TN)


dot_nn.defvjp(lambda a, b: (_dot(a, b, NN), (a, b)), lambda r, g: (dot_nt(g, r[1]), dot_tn(r[0], g)))
dot_nt.defvjp(lambda a, b: (_dot(a, b, NT), (a, b)), lambda r, g: (dot_nn(g, r[1]), dot_tn(g, r[0])))
dot_tn.defvjp(lambda a, b: (_dot(a, b, TN), (a, b)), lambda r, g: (dot_nt(r[1], g), dot_nn(r[0], g)))


def _dot_f32(a, b):
    return lax.dot_general(a, b, (NN, ((), ())), precision=lax.Precision.HIGHEST, preferred_element_type=F32)


def _tri(n, upper):
    r = lax.broadcasted_iota(jnp.int32, (n, n), 0)
    c = lax.broadcasted_iota(jnp.int32, (n, n), 1)
    return jnp.where((c >= r) if upper else (c <= r), 1.0, 0.0).astype(F32)


@functools.partial(jax.custom_vjp, nondiff_argnums=(1,))
def cumsum_rows(x, reverse):
    return _dot_f32(_tri(x.shape[0], reverse), x)


cumsum_rows.defvjp(
    lambda x, reverse: (_dot_f32(_tri(x.shape[0], reverse), x), None),
    lambda reverse, _, g: (_dot_f32(_tri(g.shape[0], not reverse), g),),
)


def _head_expand_matrix(transposed):
    shape = (D_MODEL, LANES) if transposed else (LANES, D_MODEL)
    h = lax.broadcasted_iota(jnp.int32, shape, 1 if transposed else 0)
    c = lax.broadcasted_iota(jnp.int32, shape, 0 if transposed else 1)
    return jnp.where(c // A_HEAD_DIM == h, 1.0, 0.0).astype(F32)


@jax.custom_vjp
def expand_heads(a):
    return _dot_f32(a, _head_expand_matrix(False))


expand_heads.defvjp(
    lambda a: (_dot_f32(a, _head_expand_matrix(False)), None),
    lambda _, g: (_dot_f32(g, _head_expand_matrix(True)),),
)


def _swap32_raw(x):
    lane = lax.broadcasted_iota(jnp.int32, x.shape, 1)
    return jnp.where((lane % 64) < 32, pltpu.roll(x, 96, 1), pltpu.roll(x, 32, 1))


@jax.custom_vjp
def swap32(x):
    return _swap32_raw(x)


swap32.defvjp(lambda x: (_swap32_raw(x), None), lambda _, g: (_swap32_raw(g),))


def _rms(x, gain):
    return x * lax.rsqrt(jnp.mean(x * x, axis=-1, keepdims=True) + RMS_EPS) * gain


def _sigmoid(x):
    return 1.0 / (1.0 + jnp.exp(-x))


def _log_sigmoid(x):
    return jnp.minimum(x, 0.0) - jnp.log(1.0 + jnp.exp(-jnp.abs(x)))


def _qk_prep(x, tab, gain):
    lo = lax.broadcasted_iota(jnp.int32, (1, LANES), 1) < A_HEAD_DIM
    x2 = x * x
    s_lo = jnp.sum(jnp.where(lo, x2, 0.0), axis=-1, keepdims=True)
    s_hi = jnp.sum(jnp.where(lo, 0.0, x2), axis=-1, keepdims=True)
    r = jnp.where(lo, lax.rsqrt(s_lo / A_HEAD_DIM + RMS_EPS), lax.rsqrt(s_hi / A_HEAD_DIM + RMS_EPS))
    xn = (x * r) * gain
    return xn * tab[:, :LANES] + swap32(xn) * tab[:, LANES:]


def _stack_heads(x):
    lo = lax.broadcasted_iota(jnp.int32, (1, LANES), 1) < A_HEAD_DIM
    return jnp.concatenate([jnp.where(lo, x, 0.0), jnp.where(lo, 0.0, x)], axis=0)


def _unstack_heads(x):
    t = x.shape[0] // 2
    lo = lax.broadcasted_iota(jnp.int32, (1, LANES), 1) < A_HEAD_DIM
    return jnp.where(lo, x[:t], x[t:])


def _attn_scores(q, k, valid):
    s = _dot(_stack_heads(q), k, NT) * (A_HEAD_DIM ** -0.5)
    return jnp.where(valid, s, NEG_INF)


def _attn_job(q, k, v, valid, pair):
    t = q.shape[0]
    lane = lax.broadcasted_iota(jnp.int32, (1, LANES), 1)
    s = _attn_scores(q, k, valid)
    mx = jnp.max(s, axis=-1, keepdims=True)
    p = jnp.exp(s - mx)
    l = jnp.sum(p, axis=-1, keepdims=True)
    out = _unstack_heads(_dot(p, v, NN) / l)
    lse = mx + jnp.log(l)
    return out, jnp.where(lane == 2 * pair, lse[:t], 0.0) + jnp.where(lane == 2 * pair + 1, lse[t:], 0.0)


def _attn_job_bwd(q, k, v, valid, d_out, out, lse, d_lse):
    qs = _stack_heads(q)
    wide = lambda x: jnp.concatenate([x, x], axis=1)
    p = jnp.exp(_attn_scores(q, k, valid) - wide(lse))
    dos = _stack_heads(d_out)
    dv = _dot(p, dos, TN)
    dp = _dot(dos, v, NT)
    delta = jnp.sum(dos * _stack_heads(out), axis=-1, keepdims=True)
    ds = p * (dp - delta + wide(d_lse)) * (A_HEAD_DIM ** -0.5)
    return _unstack_heads(_dot(ds, k, NN)), _dot(ds, qs, TN), dv


def _merge_groups(o0, o1, o2, l0, l1, l2):
    mx = lax.stop_gradient(jnp.maximum(jnp.maximum(l0, l1), l2))
    e0, e1, e2 = jnp.exp(l0 - mx), jnp.exp(l1 - mx), jnp.exp(l2 - mx)
    den = e0 + e1 + e2
    return expand_heads(e0 / den) * o0 + expand_heads(e1 / den) * o1 + expand_heads(e2 / den) * o2


def _gla_tile(q, k, v, la, st, reverse):
    nc = q.shape[0] // B_CHUNK
    q = q * (B_KEY_DIM ** -0.5)
    r = lax.broadcasted_iota(jnp.int32, (B_CHUNK, B_CHUNK), 0)
    c = lax.broadcasted_iota(jnp.int32, (B_CHUNK, B_CHUNK), 1)
    mask = (c > r) if reverse else (c <= r)
    outs = [None] * nc
    for ci in (range(nc - 1, -1, -1) if reverse else range(nc)):
        sl = slice(ci * B_CHUNK, (ci + 1) * B_CHUNK)
        qc, kc, vc, lc = q[sl], k[sl], v[sl], la[sl]
        b = cumsum_rows(lc, reverse)
        tot = jnp.sum(lc, axis=0, keepdims=True)
        q_t = qc * jnp.exp(b)
        k_t = kc * jnp.exp(-b)
        k_end = kc * jnp.exp(tot - b)
        attn = jnp.where(mask, dot_nt(q_t, k_t), 0.0)
        outs[ci] = dot_nn(attn, vc) + dot_nt(q_t, st)
        st = st * jnp.exp(tot) + dot_tn(vc, k_end)
    return jnp.concatenate(outs, axis=0), st


def _gla_post(o_f, o_b, r, gain):
    o = o_f + o_b
    heads = [_rms(o[:, h * B_VAL_DIM:(h + 1) * B_VAL_DIM], gain[:, h * B_VAL_DIM:(h + 1) * B_VAL_DIM]) for h in range(B_HEADS)]
    return jnp.concatenate(heads, axis=1) * (r * _sigmoid(r))


def _gate(z, wg, bias):
    return _log_sigmoid(dot_nn(z, wg) + bias) / B_GATE_TAU


def _swiglu_act(g, u):
    return (g * _sigmoid(g)) * u


def _adamw(w, g, m, v):
    m = ADAM_B1 * m + (1.0 - ADAM_B1) * g
    v = ADAM_B2 * v + (1.0 - ADAM_B2) * jnp.square(g)
    m_hat = m / (1.0 - ADAM_B1 ** ADAM_STEP)
    v_hat = v / (1.0 - ADAM_B2 ** ADAM_STEP)
    delta = -ADAM_LR * (m_hat / (jnp.sqrt(v_hat) + ADAM_EPS) + ADAM_WD * w)
    return delta, m, v


def _params(sem=None):
    return pltpu.CompilerParams(dimension_semantics=sem, vmem_limit_bytes=V7X_VMEM_LIMIT)


def _row_tile(s, want=512):
    t = min(want, s)
    assert s % t == 0
    return t


def _matmul(name, a, b, *, dims, grid, a_spec, b_spec, o_spec, out_shape, red_axis=None, res=None, res_spec=None, acc_shape=None):
    n_red = grid[red_axis] if red_axis is not None else 1

    def body(*refs):
        a_ref, b_ref = refs[0], refs[1]
        r_ref = refs[2] if res is not None else None
        o_ref = refs[3] if res is not None else refs[2]
        prod = lax.dot_general(a_ref[...].astype(MXU_DTYPE), b_ref[...].astype(MXU_DTYPE), (dims, ((), ())),
                               preferred_element_type=F32)
        if red_axis is None:
            if r_ref is not None:
                prod = prod + r_ref[...]
            o_ref[...] = prod.astype(o_ref.dtype)
            return
        acc = refs[-1] if acc_shape is not None else o_ref
        k = pl.program_id(red_axis)

        @pl.when(k == 0)
        def _():
            acc[...] = prod + r_ref[...] if r_ref is not None else prod

        @pl.when(k > 0)
        def _():
            acc[...] += prod

        if acc_shape is not None:
            @pl.when(k == n_red - 1)
            def _():
                o_ref[...] = acc[...].astype(o_ref.dtype)

    ins = [a, b] + ([res] if res is not None else [])
    specs = [a_spec, b_spec] + ([res_spec] if res is not None else [])
    sem = tuple("arbitrary" if i == red_axis else "parallel" for i in range(len(grid)))
    return pl.pallas_call(body, name=name, grid=grid, in_specs=specs, out_specs=o_spec, out_shape=out_shape,
                          scratch_shapes=[pltpu.VMEM(acc_shape, F32)] if acc_shape is not None else [],
                          compiler_params=_params(sem))(*ins)


def mm_nn(name, x, w, *, res=None, out_dtype=F32, tn=1024):
    m, k = x.shape
    n = w.shape[1]
    tm, tn = _row_tile(m, 1024), min(tn, n)
    return _matmul(name, x, w, dims=NN, grid=(n // tn, m // tm),
                   a_spec=pl.BlockSpec((tm, k), lambda j, i: (i, 0)), b_spec=pl.BlockSpec((k, tn), lambda j, i: (0, j)),
                   o_spec=pl.BlockSpec((tm, tn), lambda j, i: (i, j)), out_shape=jax.ShapeDtypeStruct((m, n), out_dtype),
                   res=res, res_spec=pl.BlockSpec((tm, tn), lambda j, i: (i, j)))


def mm_nt(name, dy, w, *, res=None, tn=1024):
    m, n = dy.shape
    k = w.shape[0]
    tm, tn = _row_tile(m, 1024), min(tn, n)
    return _matmul(name, dy, w, dims=NT, grid=(m // tm, n // tn), red_axis=1,
                   a_spec=pl.BlockSpec((tm, tn), lambda i, j: (i, j)), b_spec=pl.BlockSpec((k, tn), lambda i, j: (0, j)),
                   o_spec=pl.BlockSpec((tm, k), lambda i, j: (i, 0)), out_shape=jax.ShapeDtypeStruct((m, k), F32),
                   res=res, res_spec=pl.BlockSpec((tm, k), lambda i, j: (i, 0)))


def mm_tn(name, x, dy, *, tn=1024, by_block=False):
    m, k = x.shape
    n = dy.shape[1]
    tm, tn = _row_tile(m, 1024), min(tn, n)
    if by_block:
        o_spec, out_shape, acc = pl.BlockSpec((None, k, tn), lambda j, i: (j, 0, 0)), jax.ShapeDtypeStruct((n // tn, k, tn), BF16), (k, tn)
    else:
        o_spec, out_shape, acc = pl.BlockSpec((k, tn), lambda j, i: (0, j)), jax.ShapeDtypeStruct((k, n), F32), None
    return _matmul(name, x, dy, dims=TN, grid=(n // tn, m // tm), red_axis=1,
                   a_spec=pl.BlockSpec((tm, k), lambda j, i: (i, 0)), b_spec=pl.BlockSpec((tm, tn), lambda j, i: (i, j)),
                   o_spec=o_spec, out_shape=out_shape, acc_shape=acc)


def _rows_call(name, body, ins, outs, s, tile):
    in_specs = []
    for a, kind in ins:
        if kind == "row":
            in_specs.append(pl.BlockSpec((tile, a.shape[1]), lambda i: (i, 0)))
        else:
            in_specs.append(pl.BlockSpec(a.shape, lambda i, nd=a.ndim: (0,) * nd))
    out_specs, out_shape = [], []
    for cols, dt, kind in outs:
        if kind == "row":
            out_specs.append(pl.BlockSpec((tile, cols), lambda i: (i, 0)))
            out_shape.append(jax.ShapeDtypeStruct((s, cols), dt))
        else:
            out_specs.append(pl.BlockSpec((8, cols), lambda i: (0, 0)))
            out_shape.append(jax.ShapeDtypeStruct((8, cols), dt))
    has_acc = any(kind == "acc" for _, _, kind in outs)
    return pl.pallas_call(body, name=name, grid=(s // tile,), in_specs=in_specs, out_specs=out_specs, out_shape=out_shape,
                          compiler_params=_params(("arbitrary",) if has_acc else ("parallel",)))(*[a for a, _ in ins])


def _acc_rows(ref, val):
    @pl.when(pl.program_id(0) == 0)
    def _():
        ref[...] = jnp.zeros_like(ref)

    ref[...] += jnp.broadcast_to(val, ref.shape)


def rmsnorm_fwd(name, h, gain):
    s = h.shape[0]

    def body(h_ref, g_ref, o_ref):
        o_ref[...] = _rms(h_ref[...], g_ref[...]).astype(o_ref.dtype)

    return _rows_call(name, body, [(h, "row"), (gain, "full")], [(D_MODEL, BF16, "row")], s, _row_tile(s))[0]


def rmsnorm_bwd(name, h, gain, d_hn, d_res):
    s = h.shape[0]

    def body(h_ref, g_ref, dy_ref, dr_ref, dh_ref, dg_ref):
        _, vjp = jax.vjp(_rms, h_ref[...], g_ref[...])
        dh, dg = vjp(dy_ref[...])
        dh_ref[...] = dh + dr_ref[...]
        _acc_rows(dg_ref, dg)

    return _rows_call(name, body, [(h, "row"), (gain, "full"), (d_hn, "row"), (d_res, "row")],
                      [(D_MODEL, F32, "row"), (D_MODEL, F32, "acc")], s, _row_tile(s))


def loss_and_grad(y, target):
    s = y.shape[0]
    tile = _row_tile(s)

    def body(y_ref, t_ref, dy_ref, acc_ref):
        diff = y_ref[...] - t_ref[...]
        dy_ref[...] = diff * (1.0 / D_MODEL)

        @pl.when(pl.program_id(0) == 0)
        def _():
            acc_ref[...] = jnp.zeros_like(acc_ref)

        acc_ref[...] += jnp.sum((diff * diff).reshape(tile // 8, 8, D_MODEL), axis=0)

    return _rows_call("loss_head", body, [(y, "row"), (target, "row")], [(D_MODEL, F32, "row"), (D_MODEL, F32, "acc")], s, tile)


def _rope_table(s):
    half = A_HEAD_DIM // 2
    inv_freq = ROPE_THETA ** (-jnp.arange(half, dtype=F32) / half)
    ang = jnp.arange(s).astype(F32)[:, None] * inv_freq[None, :]
    cos, sin = jnp.cos(ang), jnp.sin(ang)
    return jnp.concatenate([cos, cos, cos, cos, -sin, sin, -sin, sin], axis=1)


def _attn_geometry(s, dil):
    tile = min(ATT_TILE, s)
    halo = A_HALF * dil
    assert s % tile == 0 and tile % (ATT_T * dil) == 0 and tile % halo == 0
    return tile, halo, tile // (ATT_T * dil)


def _attn_in_specs(grp, s, tile, halo):
    hb, n_hb = tile // halo, s // halo
    cq, ck, cv = (24 * grp + 8 * t for t in range(3))

    def main(col, per_pair, width=LANES):
        return pl.BlockSpec((tile, width), lambda i, j: (i, col + per_pair * j))

    def prev(col, per_pair, width=LANES):
        return pl.BlockSpec((halo, width), lambda i, j: (jnp.maximum(i * hb - 1, 0), col + per_pair * j))

    def nxt(col, per_pair, width=LANES):
        return pl.BlockSpec((halo, width), lambda i, j: (jnp.minimum((i + 1) * hb, n_hb - 1), col + per_pair * j))

    return [main(cq, 1), prev(ck, 1), main(ck, 1), nxt(ck, 1), prev(cv, 1), main(cv, 1), nxt(cv, 1),
            prev(0, 0, 256), main(0, 0, 256), nxt(0, 0, 256)]


def _attn_valid(base, length):
    r = lax.broadcasted_iota(jnp.int32, (2 * ATT_T, 1), 0)
    tq = base + jnp.where(r >= ATT_T, r - ATT_T, r)
    tk = base - A_HALF + lax.broadcasted_iota(jnp.int32, (1, ATT_T + 2 * A_HALF), 1)
    return (tk >= 0) & (tk < length) & (jnp.abs(tk - tq) <= A_HALF)


def _jobs(tile, dil, n_sub):
    return [(u * ATT_T * dil + p, u * ATT_T) for u in range(n_sub) for p in range(dil)]


def _rows(start, size, dil):
    return pl.ds(start, size, stride=dil) if dil > 1 else pl.ds(start, size)


def attn_fwd(name, qkv, tab, gq, gk, grp, dil):
    s = qkv.shape[0]
    tile, halo, n_sub = _attn_geometry(s, dil)
    length, per_tile = s // dil, tile // dil
    nk = ATT_T + 2 * A_HALF

    def body(q_ref, kp_ref, km_ref, kn_ref, vp_ref, vm_ref, vn_ref, tp_ref, tm_ref, tn_ref, gq_ref, gk_ref, o_ref, lse_ref,
             q_buf, k_buf, v_buf):
        i, pair = pl.program_id(0), pl.program_id(1)
        q_buf[...] = _qk_prep(q_ref[...], tm_ref[...], gq_ref[...])
        for ref, t_ref, lo, n in ((kp_ref, tp_ref, 0, halo), (km_ref, tm_ref, halo, tile), (kn_ref, tn_ref, halo + tile, halo)):
            k_buf[lo:lo + n, :] = _qk_prep(ref[...], t_ref[...], gk_ref[...])
        for ref, lo, n in ((vp_ref, 0, halo), (vm_ref, halo, tile), (vn_ref, halo + tile, halo)):
            v_buf[lo:lo + n, :] = ref[...]

        @pl.when(pair == 0)
        def _():
            lse_ref[...] = jnp.zeros_like(lse_ref)

        for start, t0 in _jobs(tile, dil, n_sub):
            valid = _attn_valid(i * per_tile + t0, length)
            o, lse = _attn_job(q_buf[_rows(start, ATT_T, dil), :], k_buf[_rows(start, nk, dil), :],
                               v_buf[_rows(start, nk, dil), :], valid, pair)
            o_ref[_rows(start, ATT_T, dil), :] = o
            lse_ref[_rows(start, ATT_T, dil), :] += lse

    full = lambda a: pl.BlockSpec(a.shape, lambda i, j: (0, 0))
    return pl.pallas_call(
        body, name=name, grid=(s // tile, D_MODEL // LANES),
        in_specs=_attn_in_specs(grp, s, tile, halo) + [full(gq), full(gk)],
        out_specs=[pl.BlockSpec((tile, LANES), lambda i, j: (i, j)), pl.BlockSpec((tile, LANES), lambda i, j: (i, 0))],
        out_shape=[jax.ShapeDtypeStruct((s, D_MODEL), F32), jax.ShapeDtypeStruct((s, LANES), F32)],
        scratch_shapes=[pltpu.VMEM((tile, LANES), F32), pltpu.VMEM((tile + 2 * halo, LANES), F32), pltpu.VMEM((tile + 2 * halo, LANES), F32)],
        compiler_params=_params(("parallel", "arbitrary")),
    )(qkv, qkv, qkv, qkv, qkv, qkv, qkv, tab, tab, tab, gq, gk)


def attn_bwd(name, qkv, tab, gq, gk, o, lse, d_o, d_lse, grp, dil, d_qkv):
    s = qkv.shape[0]
    tile, halo, n_sub = _attn_geometry(s, dil)
    length, per_tile = s // dil, tile // dil
    nt = s // tile
    nk = ATT_T + 2 * A_HALF
    pieces = ((0, halo), (halo, tile), (halo + tile, halo))

    def body(q_ref, kp_ref, km_ref, kn_ref, vp_ref, vm_ref, vn_ref, tp_ref, tm_ref, tn_ref, gq_ref, gk_ref, o_ref, l_ref, do_ref, dl_ref,
             _, dq_ref, dkp_ref, dkm_ref, dkn_ref, dvp_ref, dvm_ref, dvn_ref, dgq_ref, dgk_ref, q_buf, k_buf, v_buf, dq_buf, dk_buf, dv_buf,
             l_buf, dl_buf):
        i, pair = pl.program_id(0), pl.program_id(1)
        lane = lax.broadcasted_iota(jnp.int32, (1, LANES), 1)
        for src, dst in ((l_ref, l_buf), (dl_ref, dl_buf)):
            first = jnp.sum(jnp.where(lane == 2 * pair, src[...], 0.0), axis=-1, keepdims=True)
            second = jnp.sum(jnp.where(lane == 2 * pair + 1, src[...], 0.0), axis=-1, keepdims=True)
            dst[...] = jnp.where(lane < A_HEAD_DIM, first, second)

        def per_head(buf, rows):
            x = buf[rows, :]
            other = pltpu.roll(x, A_HEAD_DIM, 1)
            return jnp.concatenate([jnp.where(lane < A_HEAD_DIM, x, other), jnp.where(lane < A_HEAD_DIM, other, x)], axis=0)

        k_refs, t_refs = (kp_ref, km_ref, kn_ref), (tp_ref, tm_ref, tn_ref)
        q_buf[...] = _qk_prep(q_ref[...], tm_ref[...], gq_ref[...])
        for ref, t_ref, (lo, n) in zip(k_refs, t_refs, pieces):
            k_buf[lo:lo + n, :] = _qk_prep(ref[...], t_ref[...], gk_ref[...])
        for ref, (lo, n) in zip((vp_ref, vm_ref, vn_ref), pieces):
            v_buf[lo:lo + n, :] = ref[...]
        dk_buf[...] = jnp.zeros_like(dk_buf)
        dv_buf[...] = jnp.zeros_like(dv_buf)
        for start, t0 in _jobs(tile, dil, n_sub):
            valid = _attn_valid(i * per_tile + t0, length)
            rows = _rows(start, ATT_T, dil)
            dq, dk, dv = _attn_job_bwd(q_buf[rows, :], k_buf[_rows(start, nk, dil), :], v_buf[_rows(start, nk, dil), :], valid,
                                       do_ref[rows, :], o_ref[rows, :], per_head(l_buf, rows), per_head(dl_buf, rows))
            dq_buf[_rows(start, ATT_T, dil), :] = dq
            dk_buf[_rows(start, nk, dil), :] += dk
            dv_buf[_rows(start, nk, dil), :] += dv
        _, vjp = jax.vjp(lambda x, g: _qk_prep(x, tm_ref[...], g), q_ref[...], gq_ref[...])
        dq_ref[...], dgq = vjp(dq_buf[...])
        dgk = jnp.zeros((1, LANES), F32)
        for ref, t_ref, out_ref, (lo, n) in zip(k_refs, t_refs, (dkp_ref, dkm_ref, dkn_ref), pieces):
            _, vjp = jax.vjp(lambda x, g: _qk_prep(x, t_ref[...], g), ref[...], gk_ref[...])
            out_ref[...], dgk_piece = vjp(dk_buf[lo:lo + n, :])
            dgk = dgk + dgk_piece
        for out_ref, (lo, n) in zip((dvp_ref, dvm_ref, dvn_ref), pieces):
            out_ref[...] = dv_buf[lo:lo + n, :]

        @pl.when((i == 0) & (pair == 0))
        def _():
            dgq_ref[...] = jnp.zeros_like(dgq_ref)
            dgk_ref[...] = jnp.zeros_like(dgk_ref)

        dgq_ref[...] += jnp.broadcast_to(dgq + pltpu.roll(dgq, A_HEAD_DIM, 1), dgq_ref.shape)
        dgk_ref[...] += jnp.broadcast_to(dgk + pltpu.roll(dgk, A_HEAD_DIM, 1), dgk_ref.shape)

    full = lambda a: pl.BlockSpec(a.shape, lambda i, j: (0, 0))
    main_o = pl.BlockSpec((tile, LANES), lambda i, j: (i, j))
    edge_o = pl.BlockSpec((None, halo, LANES), lambda i, j: (i, 0, j))
    main_s = jax.ShapeDtypeStruct((s, D_MODEL), F32)
    edge_s = jax.ShapeDtypeStruct((nt, halo, D_MODEL), F32)
    acc_o = pl.BlockSpec((8, LANES), lambda i, j: (0, 0))
    acc_s = jax.ShapeDtypeStruct((8, LANES), F32)
    big = pltpu.VMEM((tile + 2 * halo, LANES), F32)
    own = pltpu.VMEM((tile, LANES), F32)
    slab = pl.BlockSpec((tile, LANES), lambda i, j: (i, 0))
    outs = pl.pallas_call(
        body, name=name, grid=(nt, D_MODEL // LANES),
        in_specs=_attn_in_specs(grp, s, tile, halo) + [full(gq), full(gk), main_o, slab, main_o, slab, ANY],
        out_specs=[pl.BlockSpec((tile, LANES), lambda i, j: (i, 24 * grp + j)), edge_o, main_o, edge_o, edge_o, main_o, edge_o, acc_o, acc_o],
        out_shape=[jax.ShapeDtypeStruct(d_qkv.shape, F32), edge_s, main_s, edge_s, edge_s, main_s, edge_s, acc_s, acc_s],
        scratch_shapes=[own, big, big, own, big, big, own, own],
        input_output_aliases={16: 0},
        compiler_params=_params(("arbitrary", "arbitrary")),
    )(qkv, qkv, qkv, qkv, qkv, qkv, qkv, tab, tab, tab, gq, gk, o, lse, d_o, d_lse, d_qkv)
    d_qkv, dkp, dkm, dkn, dvp, dvm, dvn, dgq, dgk = outs
    return d_qkv, (dkp, dkm, dkn), (dvp, dvm, dvn), dgq, dgk


def attn_combine(name, parts, s, dil, d_qkv, col):
    prev_part, main_part, next_part = parts
    tile, halo, _ = _attn_geometry(s, dil)
    nt = s // tile
    cols = D_MODEL // 2

    def body(m_ref, from_prev_ref, from_next_ref, _, o_ref):
        i = pl.program_id(0)
        o_ref[...] = m_ref[...]
        head = o_ref[0:halo, :] + jnp.where(i > 0, from_prev_ref[...], 0.0)
        o_ref[0:halo, :] = head
        tail = o_ref[tile - halo:tile, :] + jnp.where(i < nt - 1, from_next_ref[...], 0.0)
        o_ref[tile - halo:tile, :] = tail

    return pl.pallas_call(
        body, name=name, grid=(nt, D_MODEL // cols),
        in_specs=[pl.BlockSpec((tile, cols), lambda i, c: (i, c)),
                  pl.BlockSpec((None, halo, cols), lambda i, c: (jnp.maximum(i - 1, 0), 0, c)),
                  pl.BlockSpec((None, halo, cols), lambda i, c: (jnp.minimum(i + 1, nt - 1), 0, c)), ANY],
        out_specs=pl.BlockSpec((tile, cols), lambda i, c: (i, (D_MODEL // cols) * col + c)),
        out_shape=jax.ShapeDtypeStruct(d_qkv.shape, F32), input_output_aliases={3: 0},
        compiler_params=_params(("parallel", "parallel")),
    )(main_part, next_part, prev_part, d_qkv)


def attn_merge_fwd(name, os_, lses):
    s = os_[0].shape[0]

    def body(o0, o1, o2, l0, l1, l2, out_ref):
        out_ref[...] = _merge_groups(o0[...], o1[...], o2[...], l0[...], l1[...], l2[...]).astype(out_ref.dtype)

    return _rows_call(name, body, [(a, "row") for a in (*os_, *lses)], [(D_MODEL, BF16, "row")], s, _row_tile(s, 256))[0]


def attn_merge_bwd(name, os_, lses, d_out):
    s = os_[0].shape[0]

    def body(o0, o1, o2, l0, l1, l2, d_ref, *outs):
        _, vjp = jax.vjp(_merge_groups, o0[...], o1[...], o2[...], l0[...], l1[...], l2[...])
        for ref, val in zip(outs, vjp(d_ref[...])):
            ref[...] = val

    return _rows_call(name, body, [(a, "row") for a in (*os_, *lses, d_out)],
                      [(D_MODEL, F32, "row")] * 3 + [(LANES, F32, "row")] * 3, s, _row_tile(s, 256))


def gla_gate_fwd(name, z, wg, bias):
    s = z.shape[0]

    def body(z_ref, w_ref, b_ref, o_ref):
        o_ref[...] = _gate(z_ref[...], w_ref[...], b_ref[...])

    return _rows_call(name, body, [(z, "row"), (wg, "full"), (bias, "full")], [(D_MODEL, F32, "row")], s, _row_tile(s))[0]


def gla_gate_bwd(name, z, wg, bias, d_la_f, d_la_b):
    s = z.shape[0]
    tile = _row_tile(s)

    def body(z_ref, w_ref, b_ref, df_ref, db_ref, dz_ref, dw_ref, dbias_ref):
        _, vjp = jax.vjp(_gate, z_ref[...], w_ref[...], b_ref[...])
        dz, dw, dbias = vjp(jnp.concatenate([df_ref[...], db_ref[...]], axis=1))
        dz_ref[...] = dz

        @pl.when(pl.program_id(0) == 0)
        def _():
            dw_ref[...] = jnp.zeros_like(dw_ref)

        dw_ref[...] += dw
        _acc_rows(dbias_ref, dbias)

    return pl.pallas_call(
        body, name=name, grid=(s // tile,),
        in_specs=[pl.BlockSpec((tile, LANES), lambda i: (i, 0)), pl.BlockSpec(wg.shape, lambda i: (0, 0)),
                  pl.BlockSpec(bias.shape, lambda i: (0, 0)), pl.BlockSpec((tile, 512), lambda i: (i, 0)),
                  pl.BlockSpec((tile, 512), lambda i: (i, 0))],
        out_specs=[pl.BlockSpec((tile, LANES), lambda i: (i, 0)), pl.BlockSpec(wg.shape, lambda i: (0, 0)),
                   pl.BlockSpec((8, D_MODEL), lambda i: (0, 0))],
        out_shape=[jax.ShapeDtypeStruct((s, LANES), F32), jax.ShapeDtypeStruct(wg.shape, F32), jax.ShapeDtypeStruct((8, D_MODEL), F32)],
        compiler_params=_params(("arbitrary",)),
    )(z, wg, bias, d_la_f, d_la_b)


def _gla_in_specs(tile, order, la_col0):
    t = order
    return [pl.BlockSpec((tile, B_KEY_DIM), lambda h, n: (t(n), h)),
            pl.BlockSpec((tile, B_KEY_DIM), lambda h, n: (t(n), B_HEADS + h)),
            pl.BlockSpec((tile, B_VAL_DIM), lambda h, n: (t(n), B_HEADS + h)),
            pl.BlockSpec((tile, B_KEY_DIM), lambda h, n: (t(n), la_col0 + h))]


def gla_fwd(name, proj, la, reverse):
    s = proj.shape[0]
    tile = _row_tile(s)
    nt = s // tile
    order = (lambda n: nt - 1 - n) if reverse else (lambda n: n)

    def body(q_ref, k_ref, v_ref, la_ref, o_ref, st_ref, st_scr):
        @pl.when(pl.program_id(1) == 0)
        def _():
            st_scr[...] = jnp.zeros_like(st_scr)

        st_ref[...] = st_scr[...]
        o, st = _gla_tile(q_ref[...], k_ref[...], v_ref[...], la_ref[...], st_scr[...], reverse)
        o_ref[...] = o
        st_scr[...] = st

    return pl.pallas_call(
        body, name=name, grid=(B_HEADS, nt), in_specs=_gla_in_specs(tile, order, B_HEADS if reverse else 0),
        out_specs=[pl.BlockSpec((tile, B_VAL_DIM), lambda h, n: (order(n), h)),
                   pl.BlockSpec((None, None, B_VAL_DIM, B_KEY_DIM), lambda h, n: (h, order(n), 0, 0))],
        out_shape=[jax.ShapeDtypeStruct((s, D_MODEL), F32), jax.ShapeDtypeStruct((B_HEADS, nt, B_VAL_DIM, B_KEY_DIM), F32)],
        scratch_shapes=[pltpu.VMEM((B_VAL_DIM, B_KEY_DIM), F32)],
        compiler_params=_params(("parallel", "arbitrary")),
    )(proj, proj, proj, la)


def gla_bwd(name, proj, la, states, d_o, reverse, prev=None):
    s = proj.shape[0]
    tile = _row_tile(s)
    nt = s // tile
    order = (lambda n: n) if reverse else (lambda n: nt - 1 - n)

    def body(*refs):
        q_ref, k_ref, v_ref, la_ref, st_ref, do_ref = refs[:6]
        rest = refs[6:]
        prev_refs = rest[:3] if prev is not None else None
        dq_ref, dk_ref, dv_ref, dla_ref, dst_scr = rest[3:] if prev is not None else rest

        @pl.when(pl.program_id(1) == 0)
        def _():
            dst_scr[...] = jnp.zeros_like(dst_scr)

        _, vjp = jax.vjp(functools.partial(_gla_tile, reverse=reverse), q_ref[...], k_ref[...], v_ref[...], la_ref[...], st_ref[...])
        dq, dk, dv, dla, dst = vjp((do_ref[...], dst_scr[...]))
        if prev_refs is not None:
            dq, dk, dv = dq + prev_refs[0][...], dk + prev_refs[1][...], dv + prev_refs[2][...]
        dq_ref[...], dk_ref[...], dv_ref[...], dla_ref[...] = dq, dk, dv, dla
        dst_scr[...] = dst

    key_spec = pl.BlockSpec((tile, B_KEY_DIM), lambda h, n: (order(n), h))
    val_spec = pl.BlockSpec((tile, B_VAL_DIM), lambda h, n: (order(n), h))
    in_specs = _gla_in_specs(tile, order, B_HEADS if reverse else 0) + [
        pl.BlockSpec((None, None, B_VAL_DIM, B_KEY_DIM), lambda h, n: (h, order(n), 0, 0)), val_spec]
    ins = [proj, proj, proj, la, states, d_o]
    if prev is not None:
        in_specs += [key_spec, key_spec, val_spec]
        ins += list(prev)
    return pl.pallas_call(
        body, name=name, grid=(B_HEADS, nt), in_specs=in_specs,
        out_specs=[key_spec, key_spec, val_spec, key_spec],
        out_shape=[jax.ShapeDtypeStruct((s, 512), F32), jax.ShapeDtypeStruct((s, 512), F32),
                   jax.ShapeDtypeStruct((s, D_MODEL), F32), jax.ShapeDtypeStruct((s, 512), F32)],
        scratch_shapes=[pltpu.VMEM((B_VAL_DIM, B_KEY_DIM), F32)],
        compiler_params=_params(("parallel", "arbitrary")),
    )(*ins)


def _r_spec(tile):
    return pl.BlockSpec((tile, D_MODEL), lambda i: (i, 2))


def gla_post_fwd(name, o_f, o_b, proj, gain):
    s = o_f.shape[0]
    tile = _row_tile(s)

    def body(of_ref, ob_ref, r_ref, g_ref, out_ref):
        out_ref[...] = _gla_post(of_ref[...], ob_ref[...], r_ref[...], g_ref[...]).astype(out_ref.dtype)

    row = pl.BlockSpec((tile, D_MODEL), lambda i: (i, 0))
    return pl.pallas_call(body, name=name, grid=(s // tile,),
                          in_specs=[row, row, _r_spec(tile), pl.BlockSpec(gain.shape, lambda i: (0, 0))], out_specs=row,
                          out_shape=jax.ShapeDtypeStruct((s, D_MODEL), BF16), compiler_params=_params(("parallel",)))(o_f, o_b, proj, gain)


def gla_post_bwd(name, o_f, o_b, proj, gain, d_out):
    s = o_f.shape[0]
    tile = _row_tile(s)

    def body(of_ref, ob_ref, r_ref, g_ref, d_ref, do_ref, dr_ref, dg_ref):
        _, vjp = jax.vjp(_gla_post, of_ref[...], ob_ref[...], r_ref[...], g_ref[...])
        d_of, _, dr, dg = vjp(d_ref[...])
        do_ref[...] = d_of
        dr_ref[...] = dr
        _acc_rows(dg_ref, dg)

    row = pl.BlockSpec((tile, D_MODEL), lambda i: (i, 0))
    return pl.pallas_call(
        body, name=name, grid=(s // tile,),
        in_specs=[row, row, _r_spec(tile), pl.BlockSpec(gain.shape, lambda i: (0, 0)), row],
        out_specs=[row, row, pl.BlockSpec((8, D_MODEL), lambda i: (0, 0))],
        out_shape=[jax.ShapeDtypeStruct((s, D_MODEL), F32), jax.ShapeDtypeStruct((s, D_MODEL), F32), jax.ShapeDtypeStruct((8, D_MODEL), F32)],
        compiler_params=_params(("arbitrary",)))(o_f, o_b, proj, gain, d_out)


def _hid(tile, where):
    return pl.BlockSpec((None, tile, FFN_BLK), where)


def _pair(tile, where):
    return pl.BlockSpec((2, None, tile, FFN_BLK), where)


def _w_gu_spec(layer, where_j):
    return pl.BlockSpec((2, None, None, D_MODEL, FFN_BLK), lambda *g: (0, where_j(*g), layer, 0, 0))


def ffn_fwd(name, h_mid, hn2, w_gu, w_down, layer):
    s = hn2.shape[0]
    tm = _row_tile(s, 1024)
    nt = s // tm

    def gu_body(x_ref, w_ref, gu_ref, act_ref):
        x = x_ref[...]
        g = _dot(x, w_ref[0], NN)
        u = _dot(x, w_ref[1], NN)
        gu_ref[0] = g
        gu_ref[1] = u
        act_ref[...] = _swiglu_act(g, u).astype(act_ref.dtype)

    gu, act = pl.pallas_call(
        gu_body, name=name + "_gu", grid=(4, nt),
        in_specs=[pl.BlockSpec((tm, D_MODEL), lambda j, i: (i, 0)), _w_gu_spec(layer, lambda j, i: j)],
        out_specs=[_pair(tm, lambda j, i: (0, j, i, 0)), _hid(tm, lambda j, i: (j, i, 0))],
        out_shape=[jax.ShapeDtypeStruct((2, 4, s, FFN_BLK), F32), jax.ShapeDtypeStruct((4, s, FFN_BLK), BF16)],
        compiler_params=_params(("parallel", "parallel")))(hn2, w_gu)
    row = pl.BlockSpec((tm, D_MODEL), lambda i, j: (i, 0))
    h_next = _matmul(name + "_down", act, w_down, dims=NN, grid=(nt, 4), red_axis=1,
                     a_spec=_hid(tm, lambda i, j: (j, i, 0)), b_spec=pl.BlockSpec((None, FFN_BLK, D_MODEL), lambda i, j: (j, 0, 0)),
                     o_spec=row, out_shape=jax.ShapeDtypeStruct((s, D_MODEL), F32), res=h_mid, res_spec=row)
    return h_next, gu, act


def ffn_bwd(name, dh_next, hn2, gu, act, w_gu, w_down, layer):
    s = hn2.shape[0]
    tm = _row_tile(s, 1024)
    nt = s // tm
    d_wd = _matmul(name + "_dwd", act, dh_next, dims=TN, grid=(4, nt), red_axis=1,
                   a_spec=_hid(tm, lambda j, i: (j, i, 0)), b_spec=pl.BlockSpec((tm, D_MODEL), lambda j, i: (i, 0)),
                   o_spec=pl.BlockSpec((None, FFN_BLK, D_MODEL), lambda j, i: (j, 0, 0)),
                   out_shape=jax.ShapeDtypeStruct((4, FFN_BLK, D_MODEL), BF16), acc_shape=(FFN_BLK, D_MODEL))

    def dgu_body(dy_ref, wd_ref, gu_ref, dgu_ref):
        d_act = _dot(dy_ref[...], wd_ref[...], NT)
        _, vjp = jax.vjp(_swiglu_act, gu_ref[0], gu_ref[1])
        dg, du = vjp(d_act)
        dgu_ref[0] = dg.astype(dgu_ref.dtype)
        dgu_ref[1] = du.astype(dgu_ref.dtype)

    d_gu = pl.pallas_call(
        dgu_body, name=name + "_dgu", grid=(4, nt),
        in_specs=[pl.BlockSpec((tm, D_MODEL), lambda j, i: (i, 0)), pl.BlockSpec((None, FFN_BLK, D_MODEL), lambda j, i: (j, 0, 0)),
                  _pair(tm, lambda j, i: (0, j, i, 0))],
        out_specs=_pair(tm, lambda j, i: (0, j, i, 0)), out_shape=jax.ShapeDtypeStruct((2, 4, s, FFN_BLK), BF16),
        compiler_params=_params(("parallel", "parallel")))(dh_next, w_down, gu)

    def dx_body(d_ref, w_ref, o_ref):
        prod = _dot(d_ref[0], w_ref[0], NT) + _dot(d_ref[1], w_ref[1], NT)

        @pl.when(pl.program_id(1) == 0)
        def _():
            o_ref[...] = prod

        @pl.when(pl.program_id(1) > 0)
        def _():
            o_ref[...] += prod

    d_hn2 = pl.pallas_call(
        dx_body, name=name + "_dx", grid=(nt, 4),
        in_specs=[_pair(tm, lambda i, j: (0, j, i, 0)), _w_gu_spec(layer, lambda i, j: j)],
        out_specs=pl.BlockSpec((tm, D_MODEL), lambda i, j: (i, 0)), out_shape=jax.ShapeDtypeStruct((s, D_MODEL), F32),
        compiler_params=_params(("parallel", "arbitrary")))(d_gu, w_gu)

    def dw_body(x_ref, d_ref, o_ref, acc_ref):
        x = x_ref[...]
        k = pl.program_id(1)
        for t in range(2):
            prod = _dot(x, d_ref[t], TN)

            @pl.when(k == 0)
            def _():
                acc_ref[t] = prod

            @pl.when(k > 0)
            def _():
                acc_ref[t] += prod

        @pl.when(k == nt - 1)
        def _():
            o_ref[...] = acc_ref[...].astype(o_ref.dtype)

    d_wgu = pl.pallas_call(
        dw_body, name=name + "_dwgu", grid=(4, nt),
        in_specs=[pl.BlockSpec((tm, D_MODEL), lambda j, i: (i, 0)), _pair(tm, lambda j, i: (0, j, i, 0))],
        out_specs=pl.BlockSpec((2, None, D_MODEL, FFN_BLK), lambda j, i: (0, j, 0, 0)),
        out_shape=jax.ShapeDtypeStruct((2, 4, D_MODEL, FFN_BLK), BF16),
        scratch_shapes=[pltpu.VMEM((2, D_MODEL, FFN_BLK), F32)],
        compiler_params=_params(("parallel", "arbitrary")))(hn2, d_gu)
    return d_hn2, d_wgu, d_wd


def _my_place():
    return lax.axis_index("x"), lax.axis_index("y"), lax.axis_index("c")


def _flip(place, k):
    x, y, c = place
    return (1 - x if k & 4 else x, 1 - y if k & 2 else y, 1 - c if k & 1 else c)


def _index(place):
    return 4 * place[0] + 2 * place[1] + place[2]


def all_gather(arrs):
    n = len(arrs)

    def body(*refs):
        ins, outs = refs[:n], refs[n:2 * n]
        send_sems, recv_sems, local_sems = refs[2 * n:]
        me = _my_place()
        sibling = _flip(me, 1)
        chips = (4, 2, 6)

        def copy(a, k, block, to, src=None):
            dst = outs[a].at[_index(block)]
            return pltpu.make_async_remote_copy(src_ref=dst if src is None else src, dst_ref=dst, send_sem=send_sems.at[a, k],
                                                recv_sem=recv_sems.at[a, k], device_id=to, device_id_type=MESH)

        started = []
        for a in range(n):
            mine = pltpu.make_async_copy(ins[a], outs[a].at[_index(me)], local_sems.at[a])
            mine.start()
            started.append(mine)
        first = []
        for a in range(n):
            first.append(copy(a, 0, me, sibling, src=ins[a]))
            first += [copy(a, 1 + j, me, _flip(me, k), src=ins[a]) for j, k in enumerate(chips)]
        for cp in first:
            cp.start()
        passed = []
        for a in range(n):
            for j, k in enumerate(chips):
                copy(a, 1 + j, _flip(me, k), me).wait_recv()
                fwd = copy(a, 4 + j, _flip(me, k), sibling)
                fwd.start()
                passed.append(fwd)
        for a in range(n):
            copy(a, 0, sibling, me).wait_recv()
            for j, k in enumerate(chips):
                copy(a, 4 + j, _flip(sibling, k), me).wait_recv()
        for cp in first + passed:
            cp.wait_send()
        for cp in started:
            cp.wait()

    return pl.pallas_call(
        body, name="all_gather_weights", in_specs=[ANY] * n, out_specs=[ANY] * n,
        out_shape=[jax.ShapeDtypeStruct((N_DEV,) + a.shape, a.dtype) for a in arrs],
        scratch_shapes=[pltpu.SemaphoreType.DMA((n, 7)), pltpu.SemaphoreType.DMA((n, 7)), pltpu.SemaphoreType.DMA((n,))],
    )(*arrs)


def exchange_partials(arrs):
    n = len(arrs)

    def body(*refs):
        ins, outs = refs[:n], refs[n:2 * n]
        send_sems, recv_sems, local_sems = refs[2 * n:]
        me = _my_place()
        local = []
        for a in range(n):
            cp = pltpu.make_async_copy(ins[a].at[_index(me)], outs[a].at[_index(me)], local_sems.at[a])
            cp.start()
            local.append(cp)

        def copy(a, k, src_block, dst_block):
            return pltpu.make_async_remote_copy(src_ref=ins[a].at[_index(src_block)], dst_ref=outs[a].at[_index(dst_block)],
                                                send_sem=send_sems.at[a, k - 1], recv_sem=recv_sems.at[a, k - 1],
                                                device_id=_flip(me, k), device_id_type=MESH)

        sent = []
        for a in range(n):
            for k in range(1, N_DEV):
                cp = copy(a, k, _flip(me, k), me)
                cp.start()
                sent.append(cp)
        for a in range(n):
            for k in range(1, N_DEV):
                copy(a, k, me, _flip(me, k)).wait_recv()
        for cp in sent:
            cp.wait_send()
        for cp in local:
            cp.wait()

    return pl.pallas_call(
        body, name="exchange_weight_grads", in_specs=[ANY] * n, out_specs=[ANY] * n,
        out_shape=[jax.ShapeDtypeStruct(a.shape, a.dtype) for a in arrs],
        scratch_shapes=[pltpu.SemaphoreType.DMA((n, 7)), pltpu.SemaphoreType.DMA((n, 7)), pltpu.SemaphoreType.DMA((n,))],
    )(*arrs)


def adamw_shard(name, parts, w, m, v, layer, tile):
    rows, cols = parts.shape[1:]
    assert rows % tile == 0
    off = layer * (rows // tile)

    def body(p_ref, w_ref, m_ref, v_ref, g_ref, d_ref, nm_ref, nv_ref):
        g = p_ref[0].astype(F32)
        for src in range(1, N_DEV):
            g = g + p_ref[src].astype(F32)
        g_ref[...] = g
        d_ref[...], nm_ref[...], nv_ref[...] = _adamw(w_ref[...], g, m_ref[...], v_ref[...])

    src_row = pl.BlockSpec((tile, cols), lambda i: (off + i, 0))
    row = pl.BlockSpec((tile, cols), lambda i: (i, 0))
    shape = jax.ShapeDtypeStruct((rows, cols), F32)
    return pl.pallas_call(body, name=name, grid=(rows // tile,),
                          in_specs=[pl.BlockSpec((N_DEV, tile, cols), lambda i: (0, i, 0)), src_row, src_row, src_row],
                          out_specs=[row] * 4, out_shape=[shape] * 4, compiler_params=_params(("parallel",)))(parts, w, m, v)


def allreduce_adamw_replicated(partial, w, m, v, n_loss_rows):
    rows = partial.shape[0]

    def body(p_ref, w_ref, m_ref, v_ref, g_ref, d_ref, nm_ref, nv_ref, loss_ref, recv_ref, send_sems, recv_sems):
        me = _my_place()
        recv_ref[_index(me)] = p_ref[...]
        copies = []
        for k in range(1, N_DEV):
            peer = _flip(me, k)
            cp = pltpu.make_async_remote_copy(src_ref=p_ref, dst_ref=recv_ref.at[_index(me)], send_sem=send_sems.at[k - 1],
                                              recv_sem=recv_sems.at[k - 1], device_id=peer, device_id_type=MESH)
            cp.start()
            copies.append((cp, peer))
        for k, (cp, peer) in enumerate(copies):
            pltpu.make_async_remote_copy(src_ref=p_ref, dst_ref=recv_ref.at[_index(peer)], send_sem=send_sems.at[k],
                                         recv_sem=recv_sems.at[k], device_id=peer, device_id_type=MESH).wait_recv()
        for cp, _ in copies:
            cp.wait_send()
        g = recv_ref[0]
        for src in range(1, N_DEV):
            g = g + recv_ref[src]
        g_ref[...] = g
        d_ref[...], nm_ref[...], nv_ref[...] = _adamw(w_ref[...], g, m_ref[...], v_ref[...])
        loss = (0.5 / D_MODEL) * jnp.sum(g[rows - n_loss_rows:, :])
        loss_ref[...] = jnp.full(loss_ref.shape, loss, F32)

    shape = jax.ShapeDtypeStruct((rows, LANES), F32)
    return pl.pallas_call(
        body, name="allreduce_adamw_replicated", in_specs=[VMEM_SPEC] * 4, out_specs=[VMEM_SPEC] * 5,
        out_shape=[shape] * 4 + [jax.ShapeDtypeStruct((8, LANES), F32)],
        scratch_shapes=[pltpu.VMEM((N_DEV, rows, LANES), F32), pltpu.SemaphoreType.DMA((7,)), pltpu.SemaphoreType.DMA((7,))],
    )(partial, w, m, v)


def _pack_rows(flat, cols):
    n = flat.shape[-1]
    rows = -(-n // cols)
    rows = -(-rows // 48) * 48
    flat = jnp.pad(flat, [(0, 0)] * (flat.ndim - 1) + [(0, rows * cols - n)])
    return flat.reshape(flat.shape[:-1] + (rows, cols))


def _unpack(flat, shapes):
    out, off = [], 0
    for shp in shapes:
        n = 1
        for d in shp:
            n *= d
        out.append(flat[off:off + n].reshape(shp))
        off += n
    return out


def _to_dev_cols(a):
    w = a.shape[-1] // N_DEV
    return jnp.moveaxis(a.reshape(a.shape[:-1] + (N_DEV, w)), -2, 0)


def _from_dev_cols(a):
    a = jnp.moveaxis(a, 0, -2)
    return a.reshape(a.shape[:-2] + (a.shape[-2] * a.shape[-1],))


def kernel(x, attn_norm, ffn_norm, a_w_in, a_q_norm, a_k_norm, a_w_out, b_w_in, b_w_gate_f, b_gate_bias_f, b_w_gate_b, b_gate_bias_b, b_out_norm, b_w_out, ffn_w_gate_up, ffn_w_down, loss_target, m_attn_norm, m_ffn_norm, m_a_w_in, m_a_q_norm, m_a_k_norm, m_a_w_out, m_b_w_in, m_b_w_gate_f, m_b_gate_bias_f, m_b_w_gate_b, m_b_gate_bias_b, m_b_out_norm, m_b_w_out, m_ffn_w_gate_up, m_ffn_w_down, v_attn_norm, v_ffn_norm, v_a_w_in, v_a_q_norm, v_a_k_norm, v_a_w_out, v_b_w_in, v_b_w_gate_f, v_b_gate_bias_f, v_b_w_gate_b, v_b_gate_bias_b, v_b_out_norm, v_b_w_out, v_ffn_w_gate_up, v_ffn_w_down):
    seq = x.shape[1]
    depth = attn_norm.shape[0]
    h = x.reshape(seq, D_MODEL)
    target = loss_target.reshape(seq, D_MODEL)
    n_a, n_b = a_w_in.shape[0], b_w_in.shape[0]

    small = jnp.concatenate([t.reshape(-1) for t in (b_w_gate_f, b_gate_bias_f, b_w_gate_b, b_gate_bias_b, b_out_norm)])
    small = _pack_rows(small, LANES)
    g_a_in, g_a_out, g_b_in, g_b_out, g_gu, g_down, g_small = all_gather(
        [a_w_in.astype(BF16), a_w_out.astype(BF16), b_w_in.astype(BF16), b_w_out.astype(BF16),
         ffn_w_gate_up.astype(BF16), ffn_w_down.astype(BF16), small])
    w_a_in = _from_dev_cols(g_a_in)
    w_a_out = jnp.moveaxis(g_a_out, 0, 1).reshape(n_a, D_MODEL, D_MODEL)
    w_b_in = _from_dev_cols(g_b_in)
    w_b_out = jnp.moveaxis(g_b_out, 0, 1).reshape(n_b, D_MODEL, D_MODEL)
    w_down = jnp.moveaxis(g_down, 0, 1).reshape(depth, 4, FFN_BLK, D_MODEL)
    w_gu = g_gu.reshape(2, 4, depth, D_MODEL, FFN_BLK)
    small_shapes = [t.shape for t in (b_w_gate_f, b_gate_bias_f, b_w_gate_b, b_gate_bias_b, b_out_norm)]
    per_dev = [_unpack(g_small[d].reshape(-1), small_shapes) for d in range(N_DEV)]
    wgf, bgf, wgb, bgb, onorm = [_from_dev_cols(jnp.stack([per_dev[d][t] for d in range(N_DEV)])) for t in range(5)]
    w_gate = jnp.zeros((n_b, LANES, D_MODEL), F32)
    w_gate = w_gate.at[:, 0:16, 0:512].set(wgf).at[:, 16:32, 512:1024].set(wgb)
    gate_bias = jnp.concatenate([bgf, bgb], axis=1).reshape(n_b, 1, D_MODEL)
    out_gain = onorm.reshape(n_b, 1, D_MODEL)
    w_b_main = w_b_in[:, :, :3072]
    w_b_z = jnp.pad(w_b_in[:, :, 3072:], ((0, 0), (0, 0), (0, LANES - 32)))

    dh, sq_err, grads = _forward_backward(h, target, attn_norm, ffn_norm, a_q_norm, a_k_norm, w_a_in, w_a_out, w_b_main, w_b_z,
                                          w_gate, gate_bias, out_gain, w_b_out, w_gu, w_down)
    sharded_w = (a_w_in, a_w_out, b_w_in, b_w_gate_f, b_gate_bias_f, b_w_gate_b, b_gate_bias_b, b_out_norm, b_w_out, ffn_w_gate_up, ffn_w_down)
    sharded_m = (m_a_w_in, m_a_w_out, m_b_w_in, m_b_w_gate_f, m_b_gate_bias_f, m_b_w_gate_b, m_b_gate_bias_b, m_b_out_norm, m_b_w_out, m_ffn_w_gate_up, m_ffn_w_down)
    sharded_v = (v_a_w_in, v_a_w_out, v_b_w_in, v_b_w_gate_f, v_b_gate_bias_f, v_b_w_gate_b, v_b_gate_bias_b, v_b_out_norm, v_b_w_out, v_ffn_w_gate_up, v_ffn_w_down)
    rep_w = (attn_norm, ffn_norm, a_q_norm, a_k_norm)
    rep_m = (m_attn_norm, m_ffn_norm, m_a_q_norm, m_a_k_norm)
    rep_v = (v_attn_norm, v_ffn_norm, v_a_q_norm, v_a_k_norm)
    loss, outs = _reduce_and_update(grads, sq_err, sharded_w, sharded_m, sharded_v, rep_w, rep_m, rep_v)
    return (loss, dh.reshape(x.shape), *outs)


def _forward_backward(h, target, attn_norm, ffn_norm, a_q_norm, a_k_norm, w_a_in, w_a_out, w_b_main, w_b_z, w_gate, gate_bias,
                      out_gain, w_b_out, w_gu, w_down):
    seq = h.shape[0]
    depth = attn_norm.shape[0]
    n_a, n_b = w_a_in.shape[0], w_b_main.shape[0]
    tab = _rope_table(seq)
    pair_gain = lambda g: jnp.concatenate([g, g]).reshape(1, LANES)

    saved = []
    for i in range(depth):
        j = i // 2
        nm = f"l{i}"
        hn = rmsnorm_fwd(nm + "_norm1", h, attn_norm[i].reshape(1, D_MODEL))
        if i % 2 == 0:
            qkv = mm_nn(nm + "_qkv", hn, w_a_in[j])
            os_, lses = [], []
            for g, (_, dil) in enumerate(A_GROUPS):
                o, lse = attn_fwd(f"{nm}_attn{g}", qkv, tab, pair_gain(a_q_norm[j, g]), pair_gain(a_k_norm[j, g]), g, dil)
                os_.append(o)
                lses.append(lse)
            mixed = attn_merge_fwd(nm + "_merge", os_, lses)
            h_mid = mm_nn(nm + "_out", mixed, w_a_out[j], res=h)
            mix_saved = (qkv, os_, lses, mixed)
        else:
            proj = mm_nn(nm + "_proj", hn, w_b_main[j])
            z = mm_nn(nm + "_z", hn, w_b_z[j])
            la = gla_gate_fwd(nm + "_gate", z, w_gate[j], gate_bias[j])
            o_f, st_f = gla_fwd(nm + "_gla_f", proj, la, False)
            o_b, st_b = gla_fwd(nm + "_gla_b", proj, la, True)
            mixed = gla_post_fwd(nm + "_post", o_f, o_b, proj, out_gain[j])
            h_mid = mm_nn(nm + "_out", mixed, w_b_out[j], res=h)
            mix_saved = (proj, z, la, o_f, st_f, o_b, st_b, mixed)
        hn2 = rmsnorm_fwd(nm + "_norm2", h_mid, ffn_norm[i].reshape(1, D_MODEL))
        h_next, gu, act = ffn_fwd(nm + "_ffn", h_mid, hn2, w_gu, w_down[i], i)
        saved.append((h, hn, mix_saved, h_mid, hn2, gu, act))
        h = h_next

    dh, sq_err = loss_and_grad(h, target)

    g_attn_norm, g_ffn_norm = [None] * depth, [None] * depth
    g_a_w_in, g_a_w_out, g_a_q, g_a_k = [None] * n_a, [None] * n_a, [None] * n_a, [None] * n_a
    g_b_w_in, g_b_w_out, g_w_gate, g_gate_bias, g_out_gain = ([None] * n_b for _ in range(5))
    g_w_gu, g_w_down = [None] * depth, [None] * depth
    for i in reversed(range(depth)):
        j = i // 2
        nm = f"l{i}b"
        h_in, hn, mix_saved, h_mid, hn2, gu, act = saved[i]
        d_hn2, g_w_gu[i], g_w_down[i] = ffn_bwd(nm + "_ffn", dh, hn2, gu, act, w_gu, w_down[i], i)
        dh_mid, g_ffn_norm[i] = rmsnorm_bwd(nm + "_norm2", h_mid, ffn_norm[i].reshape(1, D_MODEL), d_hn2, dh)
        if i % 2 == 0:
            qkv, os_, lses, mixed = mix_saved
            d_mixed = mm_nt(nm + "_dmixed", dh_mid, w_a_out[j])
            g_a_w_out[j] = mm_tn(nm + "_dwout", mixed, dh_mid, by_block=True)
            d_parts = attn_merge_bwd(nm + "_merge", os_, lses, d_mixed)
            gq_l, gk_l = [], []
            d_qkv = lax.empty(qkv.shape, F32)
            for g, (_, dil) in enumerate(A_GROUPS):
                d_qkv, dk_parts, dv_parts, dgq, dgk = attn_bwd(f"{nm}_attn{g}", qkv, tab, pair_gain(a_q_norm[j, g]),
                                                               pair_gain(a_k_norm[j, g]), os_[g], lses[g], d_parts[g], d_parts[3 + g], g, dil, d_qkv)
                d_qkv = attn_combine(f"{nm}_dk{g}", dk_parts, seq, dil, d_qkv, 3 * g + 1)
                d_qkv = attn_combine(f"{nm}_dv{g}", dv_parts, seq, dil, d_qkv, 3 * g + 2)
                gq_l.append(dgq[0, :A_HEAD_DIM])
                gk_l.append(dgk[0, :A_HEAD_DIM])
            g_a_q[j], g_a_k[j] = jnp.stack(gq_l), jnp.stack(gk_l)
            d_hn = mm_nt(nm + "_dhn", d_qkv, w_a_in[j])
            g_a_w_in[j] = mm_tn(nm + "_dwin", hn, d_qkv, tn=D_MODEL * 9 // N_DEV, by_block=True)
        else:
            proj, z, la, o_f, st_f, o_b, st_b, mixed = mix_saved
            d_mixed = mm_nt(nm + "_dmixed", dh_mid, w_b_out[j])
            g_b_w_out[j] = mm_tn(nm + "_dwout", mixed, dh_mid, by_block=True)
            d_o, d_r, dgain = gla_post_bwd(nm + "_post", o_f, o_b, proj, out_gain[j], d_mixed)
            g_out_gain[j] = dgain[0]
            dq, dk, dv, dla_f = gla_bwd(nm + "_gla_f", proj, la, st_f, d_o, False)
            dq, dk, dv, dla_b = gla_bwd(nm + "_gla_b", proj, la, st_b, d_o, True, prev=(dq, dk, dv))
            d_z, g_w_gate[j], dbias = gla_gate_bwd(nm + "_gate", z, w_gate[j], gate_bias[j], dla_f, dla_b)
            g_gate_bias[j] = dbias[0]
            d_proj = jnp.concatenate([dq, dk, dv, d_r], axis=1)
            d_hn = mm_nt(nm + "_dhn_z", d_z, w_b_z[j])
            d_hn = mm_nt(nm + "_dhn", d_proj, w_b_main[j], res=d_hn)
            g_b_w_in[j] = jnp.concatenate([mm_tn(nm + "_dwin", hn, d_proj), mm_tn(nm + "_dwz", hn, d_z)[:, :32]], axis=1)
        dh, g_attn_norm[i] = rmsnorm_bwd(nm + "_norm1", h_in, attn_norm[i].reshape(1, D_MODEL), d_hn, dh_mid)
    return dh, sq_err, (g_attn_norm, g_ffn_norm, g_a_w_in, g_a_w_out, g_a_q, g_a_k, g_b_w_in, g_b_w_out, g_w_gate, g_gate_bias,
                        g_out_gain, g_w_gu, g_w_down)


def _reduce_and_update(grads, sq_err, sharded_w, sharded_m, sharded_v, rep_w, rep_m, rep_v):
    (g_attn_norm, g_ffn_norm, g_a_w_in, g_a_w_out, g_a_q, g_a_k, g_b_w_in, g_b_w_out, g_w_gate, g_gate_bias, g_out_gain,
     g_w_gu, g_w_down) = grads
    depth, n_a, n_b = len(g_w_gu), len(g_a_w_in), len(g_b_w_in)

    g_w_gate = jnp.stack(g_w_gate)
    g_gate_bias = jnp.stack(g_gate_bias)
    small_parts = [_to_dev_cols(g_w_gate[:, 0:16, 0:512]), _to_dev_cols(g_gate_bias[:, 0:512]),
                   _to_dev_cols(g_w_gate[:, 16:32, 512:1024]), _to_dev_cols(g_gate_bias[:, 512:1024]),
                   _to_dev_cols(jnp.stack(g_out_gain).reshape(n_b, B_HEADS, B_VAL_DIM))]
    small_part = _pack_rows(jnp.concatenate([t.reshape(N_DEV, -1) for t in small_parts], axis=1), LANES).astype(BF16)
    families = [
        (0, g_a_w_in, 256),
        (1, [t.reshape(N_DEV, -1, D_MODEL) for t in g_a_w_out], 128),
        (2, [_to_dev_cols(t).astype(BF16) for t in g_b_w_in], 256),
        (8, [t.reshape(N_DEV, -1, D_MODEL) for t in g_b_w_out], 128),
        (9, [t.reshape(N_DEV, D_MODEL, FFN_BLK) for t in g_w_gu], 256),
        (10, [t.reshape(N_DEV, -1, D_MODEL) for t in g_w_down], 176),
    ]
    flat_parts = [p for _, parts, _ in families for p in parts] + [small_part]
    received = exchange_partials(flat_parts)
    sh_out = [None] * len(sharded_w)
    pos = 0
    for fam, parts, tile in families:
        w = sharded_w[fam]
        two_d = lambda t: t.reshape(-1, t.shape[-1])
        per_layer = []
        for layer in range(len(parts)):
            per_layer.append(adamw_shard(f"adamw_p{fam}_l{layer}", received[pos], two_d(w), two_d(sharded_m[fam]),
                                         two_d(sharded_v[fam]), layer, tile))
            pos += 1
        sh_out[fam] = [jnp.stack([per_layer[l][t] for l in range(len(parts))]).reshape(w.shape) for t in range(4)]
    small_ids = (3, 4, 5, 6, 7)
    pack_small = lambda ts: _pack_rows(jnp.concatenate([ts[i].reshape(-1) for i in small_ids]), LANES)
    small_out = adamw_shard("adamw_small", received[pos], pack_small(sharded_w), pack_small(sharded_m), pack_small(sharded_v), 0, 48)
    small_shapes = [sharded_w[i].shape for i in small_ids]
    for t in range(4):
        for i, val in zip(small_ids, _unpack(small_out[t].reshape(-1), small_shapes)):
            if sh_out[i] is None:
                sh_out[i] = [None] * 4
            sh_out[i][t] = val
    sh_grad, sh_delta, sh_m, sh_v = [[sh_out[i][t] for i in range(len(sharded_w))] for t in range(4)]

    rep_g = (jnp.stack([t[0] for t in g_attn_norm]), jnp.stack([t[0] for t in g_ffn_norm]), jnp.stack(g_a_q), jnp.stack(g_a_k))
    n_rep = sum(t.size for t in rep_w)
    n_rep_rows = -(-n_rep // (8 * LANES)) * 8
    n_loss_rows = 8 * D_MODEL // LANES

    def pack_rep(ts, tail):
        flat = jnp.concatenate([t.reshape(-1) for t in ts])
        flat = jnp.pad(flat, (0, n_rep_rows * LANES - n_rep))
        return jnp.concatenate([flat.reshape(n_rep_rows, LANES), tail], axis=0)

    zeros_tail = jnp.zeros((n_loss_rows, LANES), F32)
    rep_out = allreduce_adamw_replicated(pack_rep(rep_g, sq_err.reshape(n_loss_rows, LANES)), pack_rep(rep_w, zeros_tail),
                                         pack_rep(rep_m, zeros_tail), pack_rep(rep_v, zeros_tail + 1.0), n_loss_rows)
    rep_shapes = [t.shape for t in rep_w]
    r_grad, r_delta, r_m, r_v = [_unpack(p.reshape(-1), rep_shapes) for p in rep_out[:4]]
    loss = rep_out[4][0, 0]

    def ordered(rep, sh):
        return [rep[0], rep[1], sh[0], rep[2], rep[3]] + list(sh[1:])

    return loss, (*ordered(r_grad, sh_grad), *ordered(r_delta, sh_delta), *ordered(r_m, sh_m), *ordered(r_v, sh_v))
```

```python
import functools

import jax
import jax.numpy as jnp
from jax import lax
from jax.experimental import pallas as pl
from jax.experimental.pallas import tpu as pltpu

F32 = jnp.float32
BF16 = jnp.bfloat16
MXU_DTYPE = jnp.bfloat16

D_MODEL = 1024
N_DEV = 8
RMS_EPS = 1e-6
NEG_INF = -1e30
A_GROUPS = ((128, 1), (512, 4), (2048, 16))
A_HEAD_DIM = 64
A_HALF = 64
ATT_T = 128
ATT_TILE = 2048
B_HEADS = 4
B_KEY_DIM = 128
B_VAL_DIM = 256
B_CHUNK = 64
B_GATE_TAU = 16.0
FFN_HIDDEN = 2816
FFN_BLK = 2 * FFN_HIDDEN // N_DEV
ADAM_LR, ADAM_B1, ADAM_B2, ADAM_EPS, ADAM_WD, ADAM_STEP = 0.001, 0.9, 0.999, 1e-08, 0.01, 10
ROPE_THETA = 10000.0

V7X_VMEM_LIMIT = 56 * 1024 * 1024
LANES = 128
MESH = pl.DeviceIdType.MESH
ANY = pl.BlockSpec(memory_space=pl.ANY)
VMEM_SPEC = pl.BlockSpec(memory_space=pltpu.VMEM)

NN = ((1,), (0,))
NT = ((1,), (1,))
TN = ((0,), (0,))


def _dot(a, b, dims):
    return lax.dot_general(a.astype(MXU_DTYPE), b.astype(MXU_DTYPE), (dims, ((), ())), preferred_element_type=F32)


@jax.custom_vjp
def dot_nn(a, b):
    return _dot(a, b, NN)


@jax.custom_vjp
def dot_nt(a, b):
    return _dot(a, b, NT)


@jax.custom_vjp
def dot_tn(a, b):
    return _dot(a, b, TN)


dot_nn.defvjp(lambda a, b: (_dot(a, b, NN), (a, b)), lambda r, g: (dot_nt(g, r[1]), dot_tn(r[0], g)))
dot_nt.defvjp(lambda a, b: (_dot(a, b, NT), (a, b)), lambda r, g: (dot_nn(g, r[1]), dot_tn(g, r[0])))
dot_tn.defvjp(lambda a, b: (_dot(a, b, TN), (a, b)), lambda r, g: (dot_nt(r[1], g), dot_nn(r[0], g)))


def _bdot(a, b, dims):
    dn = (tuple((d[0] + 1,) for d in dims), ((0,), (0,)))
    return lax.dot_general(a.astype(MXU_DTYPE), b.astype(MXU_DTYPE), dn, preferred_element_type=F32)


@jax.custom_vjp
def bdot_nn(a, b):
    return _bdot(a, b, NN)


@jax.custom_vjp
def bdot_nt(a, b):
    return _bdot(a, b, NT)


@jax.custom_vjp
def bdot_tn(a, b):
    return _bdot(a, b, TN)


bdot_nn.defvjp(lambda a, b: (_bdot(a, b, NN), (a, b)), lambda r, g: (bdot_nt(g, r[1]), bdot_tn(r[0], g)))
bdot_nt.defvjp(lambda a, b: (_bdot(a, b, NT), (a, b)), lambda r, g: (bdot_nn(g, r[1]), bdot_tn(g, r[0])))
bdot_tn.defvjp(lambda a, b: (_bdot(a, b, TN), (a, b)), lambda r, g: (bdot_nt(r[1], g), bdot_nn(r[0], g)))


def _dot_f32(a, b):
    return lax.dot_general(a, b, (NN, ((), ())), precision=lax.Precision.HIGHEST, preferred_element_type=F32)


def _tri(n, upper):
    r = lax.broadcasted_iota(jnp.int32, (n, n), 0)
    c = lax.broadcasted_iota(jnp.int32, (n, n), 1)
    return jnp.where((c >= r) if upper else (c <= r), 1.0, 0.0).astype(F32)


def _chunk_cumsum(x, reverse):
    tri = jnp.broadcast_to(_tri(x.shape[1], reverse), (x.shape[0], x.shape[1], x.shape[1]))
    return lax.dot_general(tri, x, (((2,), (1,)), ((0,), (0,))), precision=lax.Precision.HIGHEST, preferred_element_type=F32)


@functools.partial(jax.custom_vjp, nondiff_argnums=(1,))
def cumsum_chunks(x, reverse):
    return _chunk_cumsum(x, reverse)


cumsum_chunks.defvjp(lambda x, reverse: (_chunk_cumsum(x, reverse), None), lambda reverse, _, g: (_chunk_cumsum(g, not reverse),))


def _swap32_raw(x):
    lane = lax.broadcasted_iota(jnp.int32, x.shape, 1)
    return jnp.where((lane % 64) < 32, pltpu.roll(x, 96, 1), pltpu.roll(x, 32, 1))


@jax.custom_vjp
def swap32(x):
    return _swap32_raw(x)


swap32.defvjp(lambda x: (_swap32_raw(x), None), lambda _, g: (_swap32_raw(g),))


def _rms(x, gain):
    return x * lax.rsqrt(jnp.mean(x * x, axis=-1, keepdims=True) + RMS_EPS) * gain


def _sigmoid(x):
    return 1.0 / (1.0 + jnp.exp(-x))


def _log_sigmoid(x):
    return jnp.minimum(x, 0.0) - jnp.log(1.0 + jnp.exp(-jnp.abs(x)))


def _qk_prep(x, tab, gain):
    lo = lax.broadcasted_iota(jnp.int32, (1, LANES), 1) < A_HEAD_DIM
    x2 = x * x
    s_lo = jnp.sum(jnp.where(lo, x2, 0.0), axis=-1, keepdims=True)
    s_hi = jnp.sum(jnp.where(lo, 0.0, x2), axis=-1, keepdims=True)
    xn = (x * lax.rsqrt(jnp.where(lo, s_lo, s_hi) / A_HEAD_DIM + RMS_EPS)) * gain
    return xn * tab[:, :LANES] + swap32(xn) * tab[:, LANES:]


def _stack_heads(x):
    lo = lax.broadcasted_iota(jnp.int32, (1, LANES), 1) < A_HEAD_DIM
    return jnp.concatenate([jnp.where(lo, x, 0.0), jnp.where(lo, 0.0, x)], axis=0)


def _unstack_heads(x):
    t = x.shape[0] // 2
    lo = lax.broadcasted_iota(jnp.int32, (1, LANES), 1) < A_HEAD_DIM
    return jnp.where(lo, x[:t], x[t:])


def _attn_scores(q, k, valid):
    s = _dot(_stack_heads(q), k, NT) * (A_HEAD_DIM ** -0.5)
    return jnp.where(valid, s, NEG_INF)


def _attn_job(q, k, v, valid):
    s = _attn_scores(q, k, valid)
    mx = jnp.max(s, axis=-1, keepdims=True)
    p = jnp.exp(s - mx)
    l = jnp.sum(p, axis=-1, keepdims=True)
    out = _unstack_heads(_dot(p, v, NN) / l)
    lse = mx + jnp.log(l)
    t = q.shape[0]
    lo = lax.broadcasted_iota(jnp.int32, (1, LANES), 1) < A_HEAD_DIM
    return out, jnp.where(lo, lse[:t], lse[t:])


def _attn_job_bwd(q, k, v, valid, d_out, out, lse, d_lse):
    qs = _stack_heads(q)
    lo = lax.broadcasted_iota(jnp.int32, (1, LANES), 1) < A_HEAD_DIM
    other = pltpu.roll(lse, A_HEAD_DIM, 1)
    row_lse = jnp.concatenate([jnp.where(lo, lse, other), jnp.where(lo, other, lse)], axis=0)
    p = jnp.exp(_attn_scores(q, k, valid) - jnp.concatenate([row_lse, row_lse], axis=1))
    dos = _stack_heads(d_out)
    dv = _dot(p, dos, TN)
    dp = _dot(dos, v, NT)
    inner = jnp.sum(dos * _stack_heads(out) - _stack_heads(d_lse), axis=-1, keepdims=True)
    ds = p * (dp - inner) * (A_HEAD_DIM ** -0.5)
    return _unstack_heads(_dot(ds, k, NN)), _dot(ds, qs, TN), dv


def _merge_groups(o0, o1, o2, l0, l1, l2):
    mx = lax.stop_gradient(jnp.maximum(jnp.maximum(l0, l1), l2))
    e0, e1, e2 = jnp.exp(l0 - mx), jnp.exp(l1 - mx), jnp.exp(l2 - mx)
    den = e0 + e1 + e2
    return (e0 / den) * o0 + (e1 / den) * o1 + (e2 / den) * o2


def _gla_tile(q, k, v, la, st, reverse):
    t = q.shape[0]
    nc = t // B_CHUNK
    split = lambda x: x.reshape(nc, B_CHUNK, x.shape[1])
    q, k, v, la = split(q * (B_KEY_DIM ** -0.5)), split(k), split(v), split(la)
    r = lax.broadcasted_iota(jnp.int32, (1, B_CHUNK, B_CHUNK), 1)
    c = lax.broadcasted_iota(jnp.int32, (1, B_CHUNK, B_CHUNK), 2)
    mask = (c > r) if reverse else (c <= r)
    b = cumsum_chunks(la, reverse)
    tot = jnp.sum(la, axis=1, keepdims=True)
    q_t = q * jnp.exp(b)
    k_t = k * jnp.exp(-b)
    k_end = k * jnp.exp(tot - b)
    attn = jnp.where(mask, bdot_nt(q_t, k_t), 0.0)
    kv = bdot_tn(v, k_end)
    decay = jnp.exp(tot)
    entering = [None] * nc
    for ci in (range(nc - 1, -1, -1) if reverse else range(nc)):
        entering[ci] = st
        st = st * decay[ci] + kv[ci]
    out = bdot_nn(attn, v) + bdot_nt(q_t, jnp.stack(entering))
    return out.reshape(t, out.shape[2]), st


def _gla_post(o_f, o_b, r, gain):
    o = o_f + o_b
    heads = [_rms(o[:, h * B_VAL_DIM:(h + 1) * B_VAL_DIM], gain[:, h * B_VAL_DIM:(h + 1) * B_VAL_DIM]) for h in range(B_HEADS)]
    return jnp.concatenate(heads, axis=1) * (r * _sigmoid(r))


def _gate(z, wg, bias):
    return _log_sigmoid(dot_nn(z, wg) + bias) / B_GATE_TAU


def _swiglu_act(g, u):
    return (g * _sigmoid(g)) * u


def _adamw(w, g, m, v):
    m = ADAM_B1 * m + (1.0 - ADAM_B1) * g
    v = ADAM_B2 * v + (1.0 - ADAM_B2) * jnp.square(g)
    m_hat = m / (1.0 - ADAM_B1 ** ADAM_STEP)
    v_hat = v / (1.0 - ADAM_B2 ** ADAM_STEP)
    delta = -ADAM_LR * (m_hat / (jnp.sqrt(v_hat) + ADAM_EPS) + ADAM_WD * w)
    return delta, m, v


def _params(sem=None):
    return pltpu.CompilerParams(dimension_semantics=sem, vmem_limit_bytes=V7X_VMEM_LIMIT)


def _row_tile(s, want=512):
    t = min(want, s)
    assert s % t == 0
    return t


def _matmul(name, a, b, *, dims, grid, a_spec, b_spec, o_spec, out_shape, red_axis=None, res=None, res_spec=None, acc_shape=None):
    n_red = grid[red_axis] if red_axis is not None else 1

    def body(*refs):
        a_ref, b_ref = refs[0], refs[1]
        r_ref = refs[2] if res is not None else None
        o_ref = refs[3] if res is not None else refs[2]
        prod = lax.dot_general(a_ref[...].astype(MXU_DTYPE), b_ref[...].astype(MXU_DTYPE), (dims, ((), ())),
                               preferred_element_type=F32)
        if red_axis is None:
            if r_ref is not None:
                prod = prod + r_ref[...]
            o_ref[...] = prod.astype(o_ref.dtype)
            return
        acc = refs[-1] if acc_shape is not None else o_ref
        k = pl.program_id(red_axis)

        @pl.when(k == 0)
        def _():
            acc[...] = prod + r_ref[...] if r_ref is not None else prod

        @pl.when(k > 0)
        def _():
            acc[...] += prod

        if acc_shape is not None:
            @pl.when(k == n_red - 1)
            def _():
                o_ref[...] = acc[...].astype(o_ref.dtype)

    ins = [a, b] + ([res] if res is not None else [])
    specs = [a_spec, b_spec] + ([res_spec] if res is not None else [])
    sem = tuple("arbitrary" if i == red_axis else "parallel" for i in range(len(grid)))
    return pl.pallas_call(body, name=name, grid=grid, in_specs=specs, out_specs=o_spec, out_shape=out_shape,
                          scratch_shapes=[pltpu.VMEM(acc_shape, F32)] if acc_shape is not None else [],
                          compiler_params=_params(sem))(*ins)


def mm_nn(name, x, w, *, res=None, out_dtype=F32, tn=1024):
    m, k = x.shape
    n = w.shape[1]
    tm, tn = _row_tile(m, 1024), min(tn, n)
    return _matmul(name, x, w, dims=NN, grid=(n // tn, m // tm),
                   a_spec=pl.BlockSpec((tm, k), lambda j, i: (i, 0)), b_spec=pl.BlockSpec((k, tn), lambda j, i: (0, j)),
                   o_spec=pl.BlockSpec((tm, tn), lambda j, i: (i, j)), out_shape=jax.ShapeDtypeStruct((m, n), out_dtype),
                   res=res, res_spec=pl.BlockSpec((tm, tn), lambda j, i: (i, j)))


def mm_nt(name, dy, w, *, res=None, tn=1024):
    m, n = dy.shape
    k = w.shape[0]
    tm, tn = _row_tile(m, 1024), min(tn, n)
    return _matmul(name, dy, w, dims=NT, grid=(m // tm, n // tn), red_axis=1,
                   a_spec=pl.BlockSpec((tm, tn), lambda i, j: (i, j)), b_spec=pl.BlockSpec((k, tn), lambda i, j: (0, j)),
                   o_spec=pl.BlockSpec((tm, k), lambda i, j: (i, 0)), out_shape=jax.ShapeDtypeStruct((m, k), F32),
                   res=res, res_spec=pl.BlockSpec((tm, k), lambda i, j: (i, 0)))


def mm_tn(name, x, dy, *, tn=1024, by_block=False):
    m, k = x.shape
    n = dy.shape[1]
    tm, tn = _row_tile(m, 1024), min(tn, n)
    if by_block:
        o_spec, out_shape, acc = pl.BlockSpec((None, k, tn), lambda j, i: (j, 0, 0)), jax.ShapeDtypeStruct((n // tn, k, tn), BF16), (k, tn)
    else:
        o_spec, out_shape, acc = pl.BlockSpec((k, tn), lambda j, i: (0, j)), jax.ShapeDtypeStruct((k, n), F32), None
    return _matmul(name, x, dy, dims=TN, grid=(n // tn, m // tm), red_axis=1,
                   a_spec=pl.BlockSpec((tm, k), lambda j, i: (i, 0)), b_spec=pl.BlockSpec((tm, tn), lambda j, i: (i, j)),
                   o_spec=o_spec, out_shape=out_shape, acc_shape=acc)


def _rows_call(name, body, ins, outs, s, tile):
    in_specs = []
    for a, kind in ins:
        if kind == "row":
            in_specs.append(pl.BlockSpec((tile, a.shape[1]), lambda i: (i, 0)))
        else:
            in_specs.append(pl.BlockSpec(a.shape, lambda i, nd=a.ndim: (0,) * nd))
    out_specs, out_shape = [], []
    for cols, dt, kind in outs:
        if kind == "row":
            out_specs.append(pl.BlockSpec((tile, cols), lambda i: (i, 0)))
            out_shape.append(jax.ShapeDtypeStruct((s, cols), dt))
        else:
            out_specs.append(pl.BlockSpec((8, cols), lambda i: (0, 0)))
            out_shape.append(jax.ShapeDtypeStruct((8, cols), dt))
    has_acc = any(kind == "acc" for _, _, kind in outs)
    return pl.pallas_call(body, name=name, grid=(s // tile,), in_specs=in_specs, out_specs=out_specs, out_shape=out_shape,
                          compiler_params=_params(("arbitrary",) if has_acc else ("parallel",)))(*[a for a, _ in ins])


def _acc_rows(ref, val):
    @pl.when(pl.program_id(0) == 0)
    def _():
        ref[...] = jnp.zeros_like(ref)

    ref[...] += jnp.broadcast_to(val, ref.shape)


def rmsnorm_fwd(name, h, gain):
    s = h.shape[0]

    def body(h_ref, g_ref, o_ref):
        o_ref[...] = _rms(h_ref[...], g_ref[...]).astype(o_ref.dtype)

    return _rows_call(name, body, [(h, "row"), (gain, "full")], [(D_MODEL, BF16, "row")], s, _row_tile(s))[0]


def rmsnorm_bwd(name, h, gain, d_hn, d_res):
    s = h.shape[0]

    def body(h_ref, g_ref, dy_ref, dr_ref, dh_ref, dg_ref):
        _, vjp = jax.vjp(_rms, h_ref[...], g_ref[...])
        dh, dg = vjp(dy_ref[...])
        dh_ref[...] = dh + dr_ref[...]
        _acc_rows(dg_ref, dg)

    return _rows_call(name, body, [(h, "row"), (gain, "full"), (d_hn, "row"), (d_res, "row")],
                      [(D_MODEL, F32, "row"), (D_MODEL, F32, "acc")], s, _row_tile(s))


def loss_and_grad(y, target):
    s = y.shape[0]
    tile = _row_tile(s)

    def body(y_ref, t_ref, dy_ref, acc_ref):
        diff = y_ref[...] - t_ref[...]
        dy_ref[...] = diff * (1.0 / D_MODEL)

        @pl.when(pl.program_id(0) == 0)
        def _():
            acc_ref[...] = jnp.zeros_like(acc_ref)

        acc_ref[...] += jnp.sum((diff * diff).reshape(tile // 8, 8, D_MODEL), axis=0)

    return _rows_call("loss_head", body, [(y, "row"), (target, "row")], [(D_MODEL, F32, "row"), (D_MODEL, F32, "acc")], s, tile)


def _rope_table(s):
    half = A_HEAD_DIM // 2
    inv_freq = ROPE_THETA ** (-jnp.arange(half, dtype=F32) / half)
    ang = jnp.arange(s).astype(F32)[:, None] * inv_freq[None, :]
    cos, sin = jnp.cos(ang), jnp.sin(ang)
    return jnp.concatenate([cos, cos, cos, cos, -sin, sin, -sin, sin], axis=1)


def _attn_geometry(s, dil):
    tile = min(ATT_TILE, s)
    halo = A_HALF * dil
    assert s % tile == 0 and tile % (ATT_T * dil) == 0 and tile % halo == 0
    return tile, halo, tile // (ATT_T * dil)


def _attn_in_specs(grp, s, tile, halo):
    hb, n_hb = tile // halo, s // halo
    cq, ck, cv = (24 * grp + 8 * t for t in range(3))

    def main(col, per_pair, width=LANES):
        return pl.BlockSpec((tile, width), lambda i, j: (i, col + per_pair * j))

    def prev(col, per_pair, width=LANES):
        return pl.BlockSpec((halo, width), lambda i, j: (jnp.maximum(i * hb - 1, 0), col + per_pair * j))

    def nxt(col, per_pair, width=LANES):
        return pl.BlockSpec((halo, width), lambda i, j: (jnp.minimum((i + 1) * hb, n_hb - 1), col + per_pair * j))

    return [main(cq, 1), prev(ck, 1), main(ck, 1), nxt(ck, 1), prev(cv, 1), main(cv, 1), nxt(cv, 1),
            prev(0, 0, 256), main(0, 0, 256), nxt(0, 0, 256)]


def _attn_band():
    r = lax.broadcasted_iota(jnp.int32, (2 * ATT_T, 1), 0)
    tq = jnp.where(r >= ATT_T, r - ATT_T, r)
    tk = lax.broadcasted_iota(jnp.int32, (1, ATT_T + 2 * A_HALF), 1) - A_HALF
    return jnp.abs(tk - tq) <= A_HALF


def _attn_valid(band, base, length):
    tk = base - A_HALF + lax.broadcasted_iota(jnp.int32, (1, ATT_T + 2 * A_HALF), 1)
    return band & (tk >= 0) & (tk < length)


def _jobs(tile, dil, n_sub):
    return [(u * ATT_T * dil + p, u * ATT_T) for u in range(n_sub) for p in range(dil)]


def _rows(start, size, dil):
    return pl.ds(start, size, stride=dil) if dil > 1 else pl.ds(start, size)


def attn_fwd(name, qkv, tab, gq, gk, grp, dil):
    s = qkv.shape[0]
    tile, halo, n_sub = _attn_geometry(s, dil)
    length, per_tile = s // dil, tile // dil
    nk = ATT_T + 2 * A_HALF

    def body(q_ref, kp_ref, km_ref, kn_ref, vp_ref, vm_ref, vn_ref, tp_ref, tm_ref, tn_ref, gq_ref, gk_ref, o_ref, lse_ref,
             q_buf, k_buf, v_buf):
        i, pair = pl.program_id(0), pl.program_id(1)
        q_buf[...] = _qk_prep(q_ref[...], tm_ref[...], gq_ref[...])
        for ref, t_ref, lo, n in ((kp_ref, tp_ref, 0, halo), (km_ref, tm_ref, halo, tile), (kn_ref, tn_ref, halo + tile, halo)):
            k_buf[lo:lo + n, :] = _qk_prep(ref[...], t_ref[...], gk_ref[...])
        for ref, lo, n in ((vp_ref, 0, halo), (vm_ref, halo, tile), (vn_ref, halo + tile, halo)):
            v_buf[lo:lo + n, :] = ref[...]

        band = _attn_band()
        for start, t0 in _jobs(tile, dil, n_sub):
            valid = _attn_valid(band, i * per_tile + t0, length)
            o, lse = _attn_job(q_buf[_rows(start, ATT_T, dil), :], k_buf[_rows(start, nk, dil), :],
                               v_buf[_rows(start, nk, dil), :], valid)
            o_ref[_rows(start, ATT_T, dil), :] = o
            lse_ref[_rows(start, ATT_T, dil), :] = lse

    full = lambda a: pl.BlockSpec(a.shape, lambda i, j: (0, 0))
    return pl.pallas_call(
        body, name=name, grid=(s // tile, D_MODEL // LANES),
        in_specs=_attn_in_specs(grp, s, tile, halo) + [full(gq), full(gk)],
        out_specs=[pl.BlockSpec((tile, LANES), lambda i, j: (i, j)), pl.BlockSpec((tile, LANES), lambda i, j: (i, j))],
        out_shape=[jax.ShapeDtypeStruct((s, D_MODEL), F32), jax.ShapeDtypeStruct((s, D_MODEL), F32)],
        scratch_shapes=[pltpu.VMEM((tile, LANES), F32), pltpu.VMEM((tile + 2 * halo, LANES), F32), pltpu.VMEM((tile + 2 * halo, LANES), F32)],
        compiler_params=_params(("parallel", "parallel")),
    )(qkv, qkv, qkv, qkv, qkv, qkv, qkv, tab, tab, tab, gq, gk)


def attn_bwd(name, qkv, tab, gq, gk, o, lse, d_o, d_lse, grp, dil, d_qkv):
    s = qkv.shape[0]
    tile, halo, n_sub = _attn_geometry(s, dil)
    length, per_tile = s // dil, tile // dil
    nt = s // tile
    nk = ATT_T + 2 * A_HALF
    pieces = ((0, halo), (halo, tile), (halo + tile, halo))

    def body(q_ref, kp_ref, km_ref, kn_ref, vp_ref, vm_ref, vn_ref, tp_ref, tm_ref, tn_ref, gq_ref, gk_ref, o_ref, l_ref, do_ref, dl_ref,
             _, dq_ref, dkp_ref, dkm_ref, dkn_ref, dvp_ref, dvm_ref, dvn_ref, dgq_ref, dgk_ref, q_buf, k_buf, v_buf, dq_buf, dk_buf, dv_buf):
        i, pair = pl.program_id(0), pl.program_id(1)
        k_refs, t_refs = (kp_ref, km_ref, kn_ref), (tp_ref, tm_ref, tn_ref)
        q_buf[...] = _qk_prep(q_ref[...], tm_ref[...], gq_ref[...])
        for ref, t_ref, (lo, n) in zip(k_refs, t_refs, pieces):
            k_buf[lo:lo + n, :] = _qk_prep(ref[...], t_ref[...], gk_ref[...])
        for ref, (lo, n) in zip((vp_ref, vm_ref, vn_ref), pieces):
            v_buf[lo:lo + n, :] = ref[...]
        dk_buf[...] = jnp.zeros_like(dk_buf)
        dv_buf[...] = jnp.zeros_like(dv_buf)
        band = _attn_band()
        for start, t0 in _jobs(tile, dil, n_sub):
            valid = _attn_valid(band, i * per_tile + t0, length)
            rows = _rows(start, ATT_T, dil)
            dq, dk, dv = _attn_job_bwd(q_buf[rows, :], k_buf[_rows(start, nk, dil), :], v_buf[_rows(start, nk, dil), :], valid,
                                       do_ref[rows, :], o_ref[rows, :], l_ref[rows, :], dl_ref[rows, :])
            dq_buf[_rows(start, ATT_T, dil), :] = dq
            dk_buf[_rows(start, nk, dil), :] += dk
            dv_buf[_rows(start, nk, dil), :] += dv
        _, vjp = jax.vjp(lambda x, g: _qk_prep(x, tm_ref[...], g), q_ref[...], gq_ref[...])
        dq_ref[...], dgq = vjp(dq_buf[...])
        dgk = jnp.zeros((1, LANES), F32)
        for ref, t_ref, out_ref, (lo, n) in zip(k_refs, t_refs, (dkp_ref, dkm_ref, dkn_ref), pieces):
            _, vjp = jax.vjp(lambda x, g: _qk_prep(x, t_ref[...], g), ref[...], gk_ref[...])
            out_ref[...], dgk_piece = vjp(dk_buf[lo:lo + n, :])
            dgk = dgk + dgk_piece
        for out_ref, (lo, n) in zip((dvp_ref, dvm_ref, dvn_ref), pieces):
            out_ref[...] = dv_buf[lo:lo + n, :]

        @pl.when((i == 0) & (pair == 0))
        def _():
            dgq_ref[...] = jnp.zeros_like(dgq_ref)
            dgk_ref[...] = jnp.zeros_like(dgk_ref)

        dgq_ref[...] += jnp.broadcast_to(dgq + pltpu.roll(dgq, A_HEAD_DIM, 1), dgq_ref.shape)
        dgk_ref[...] += jnp.broadcast_to(dgk + pltpu.roll(dgk, A_HEAD_DIM, 1), dgk_ref.shape)

    full = lambda a: pl.BlockSpec(a.shape, lambda i, j: (0, 0))
    main_o = pl.BlockSpec((tile, LANES), lambda i, j: (i, j))
    edge_o = pl.BlockSpec((None, halo, LANES), lambda i, j: (i, 0, j))
    main_s = jax.ShapeDtypeStruct((s, D_MODEL), F32)
    edge_s = jax.ShapeDtypeStruct((nt, halo, D_MODEL), F32)
    acc_o = pl.BlockSpec((8, LANES), lambda i, j: (0, 0))
    acc_s = jax.ShapeDtypeStruct((8, LANES), F32)
    big = pltpu.VMEM((tile + 2 * halo, LANES), F32)
    own = pltpu.VMEM((tile, LANES), F32)
    outs = pl.pallas_call(
        body, name=name, grid=(nt, D_MODEL // LANES),
        in_specs=_attn_in_specs(grp, s, tile, halo) + [full(gq), full(gk), main_o, main_o, main_o, main_o, ANY],
        out_specs=[pl.BlockSpec((tile, LANES), lambda i, j: (i, 24 * grp + j)), edge_o, main_o, edge_o, edge_o, main_o, edge_o, acc_o, acc_o],
        out_shape=[jax.ShapeDtypeStruct(d_qkv.shape, F32), edge_s, main_s, edge_s, edge_s, main_s, edge_s, acc_s, acc_s],
        scratch_shapes=[own, big, big, own, big, big],
        input_output_aliases={16: 0},
        compiler_params=_params(("arbitrary", "arbitrary")),
    )(qkv, qkv, qkv, qkv, qkv, qkv, qkv, tab, tab, tab, gq, gk, o, lse, d_o, d_lse, d_qkv)
    d_qkv, dkp, dkm, dkn, dvp, dvm, dvn, dgq, dgk = outs
    return d_qkv, (dkp, dkm, dkn), (dvp, dvm, dvn), dgq, dgk


def attn_combine(name, parts, s, dil, d_qkv, col):
    prev_part, main_part, next_part = parts
    tile, halo, _ = _attn_geometry(s, dil)
    nt = s // tile
    cols = D_MODEL // 2

    def body(m_ref, from_prev_ref, from_next_ref, _, o_ref):
        i = pl.program_id(0)
        o_ref[...] = m_ref[...]
        head = o_ref[0:halo, :] + jnp.where(i > 0, from_prev_ref[...], 0.0)
        o_ref[0:halo, :] = head
        tail = o_ref[tile - halo:tile, :] + jnp.where(i < nt - 1, from_next_ref[...], 0.0)
        o_ref[tile - halo:tile, :] = tail

    return pl.pallas_call(
        body, name=name, grid=(nt, D_MODEL // cols),
        in_specs=[pl.BlockSpec((tile, cols), lambda i, c: (i, c)),
                  pl.BlockSpec((None, halo, cols), lambda i, c: (jnp.maximum(i - 1, 0), 0, c)),
                  pl.BlockSpec((None, halo, cols), lambda i, c: (jnp.minimum(i + 1, nt - 1), 0, c)), ANY],
        out_specs=pl.BlockSpec((tile, cols), lambda i, c: (i, (D_MODEL // cols) * col + c)),
        out_shape=jax.ShapeDtypeStruct(d_qkv.shape, F32), input_output_aliases={3: 0},
        compiler_params=_params(("parallel", "parallel")),
    )(main_part, next_part, prev_part, d_qkv)


def attn_merge_fwd(name, os_, lses):
    s = os_[0].shape[0]

    def body(o0, o1, o2, l0, l1, l2, out_ref):
        out_ref[...] = _merge_groups(o0[...], o1[...], o2[...], l0[...], l1[...], l2[...]).astype(out_ref.dtype)

    return _rows_call(name, body, [(a, "row") for a in (*os_, *lses)], [(D_MODEL, BF16, "row")], s, _row_tile(s, 256))[0]


def attn_merge_bwd(name, os_, lses, d_out):
    s = os_[0].shape[0]

    def body(o0, o1, o2, l0, l1, l2, d_ref, *outs):
        _, vjp = jax.vjp(_merge_groups, o0[...], o1[...], o2[...], l0[...], l1[...], l2[...])
        for ref, val in zip(outs, vjp(d_ref[...])):
            ref[...] = val

    return _rows_call(name, body, [(a, "row") for a in (*os_, *lses, d_out)],
                      [(D_MODEL, F32, "row")] * 6, s, _row_tile(s, 256))


def gla_gate_fwd(name, z, wg, bias):
    s = z.shape[0]

    def body(z_ref, w_ref, b_ref, o_ref):
        o_ref[...] = _gate(z_ref[...], w_ref[...], b_ref[...])

    return _rows_call(name, body, [(z, "row"), (wg, "full"), (bias, "full")], [(D_MODEL, F32, "row")], s, _row_tile(s))[0]


def gla_gate_bwd(name, z, wg, bias, d_la_f, d_la_b):
    s = z.shape[0]
    tile = _row_tile(s)

    def body(z_ref, w_ref, b_ref, df_ref, db_ref, dz_ref, dw_ref, dbias_ref):
        _, vjp = jax.vjp(_gate, z_ref[...], w_ref[...], b_ref[...])
        dz, dw, dbias = vjp(jnp.concatenate([df_ref[...], db_ref[...]], axis=1))
        dz_ref[...] = dz

        @pl.when(pl.program_id(0) == 0)
        def _():
            dw_ref[...] = jnp.zeros_like(dw_ref)

        dw_ref[...] += dw
        _acc_rows(dbias_ref, dbias)

    return pl.pallas_call(
        body, name=name, grid=(s // tile,),
        in_specs=[pl.BlockSpec((tile, LANES), lambda i: (i, 0)), pl.BlockSpec(wg.shape, lambda i: (0, 0)),
                  pl.BlockSpec(bias.shape, lambda i: (0, 0)), pl.BlockSpec((tile, 512), lambda i: (i, 0)),
                  pl.BlockSpec((tile, 512), lambda i: (i, 0))],
        out_specs=[pl.BlockSpec((tile, LANES), lambda i: (i, 0)), pl.BlockSpec(wg.shape, lambda i: (0, 0)),
                   pl.BlockSpec((8, D_MODEL), lambda i: (0, 0))],
        out_shape=[jax.ShapeDtypeStruct((s, LANES), F32), jax.ShapeDtypeStruct(wg.shape, F32), jax.ShapeDtypeStruct((8, D_MODEL), F32)],
        compiler_params=_params(("arbitrary",)),
    )(z, wg, bias, d_la_f, d_la_b)


def _gla_in_specs(tile, order, la_col0):
    t = order
    return [pl.BlockSpec((tile, B_KEY_DIM), lambda h, n: (t(n), h)),
            pl.BlockSpec((tile, B_KEY_DIM), lambda h, n: (t(n), B_HEADS + h)),
            pl.BlockSpec((tile, B_VAL_DIM), lambda h, n: (t(n), B_HEADS + h)),
            pl.BlockSpec((tile, B_KEY_DIM), lambda h, n: (t(n), la_col0 + h))]


def gla_fwd(name, proj, la, reverse):
    s = proj.shape[0]
    tile = _row_tile(s)
    nt = s // tile
    order = (lambda n: nt - 1 - n) if reverse else (lambda n: n)

    def body(q_ref, k_ref, v_ref, la_ref, o_ref, st_ref, st_scr):
        @pl.when(pl.program_id(1) == 0)
        def _():
            st_scr[...] = jnp.zeros_like(st_scr)

        st_ref[...] = st_scr[...]
        o, st = _gla_tile(q_ref[...], k_ref[...], v_ref[...], la_ref[...], st_scr[...], reverse)
        o_ref[...] = o
        st_scr[...] = st

    return pl.pallas_call(
        body, name=name, grid=(B_HEADS, nt), in_specs=_gla_in_specs(tile, order, B_HEADS if reverse else 0),
        out_specs=[pl.BlockSpec((tile, B_VAL_DIM), lambda h, n: (order(n), h)),
                   pl.BlockSpec((None, None, B_VAL_DIM, B_KEY_DIM), lambda h, n: (h, order(n), 0, 0))],
        out_shape=[jax.ShapeDtypeStruct((s, D_MODEL), F32), jax.ShapeDtypeStruct((B_HEADS, nt, B_VAL_DIM, B_KEY_DIM), F32)],
        scratch_shapes=[pltpu.VMEM((B_VAL_DIM, B_KEY_DIM), F32)],
        compiler_params=_params(("parallel", "arbitrary")),
    )(proj, proj, proj, la)


def gla_bwd(name, proj, la, states, d_o, reverse, prev=None):
    s = proj.shape[0]
    tile = _row_tile(s)
    nt = s // tile
    order = (lambda n: n) if reverse else (lambda n: nt - 1 - n)

    def body(*refs):
        q_ref, k_ref, v_ref, la_ref, st_ref, do_ref = refs[:6]
        rest = refs[6:]
        prev_refs = rest[:3] if prev is not None else None
        dq_ref, dk_ref, dv_ref, dla_ref, dst_scr = rest[3:] if prev is not None else rest

        @pl.when(pl.program_id(1) == 0)
        def _():
            dst_scr[...] = jnp.zeros_like(dst_scr)

        _, vjp = jax.vjp(functools.partial(_gla_tile, reverse=reverse), q_ref[...], k_ref[...], v_ref[...], la_ref[...], st_ref[...])
        dq, dk, dv, dla, dst = vjp((do_ref[...], dst_scr[...]))
        if prev_refs is not None:
            dq, dk, dv = dq + prev_refs[0][...], dk + prev_refs[1][...], dv + prev_refs[2][...]
        dq_ref[...], dk_ref[...], dv_ref[...], dla_ref[...] = dq, dk, dv, dla
        dst_scr[...] = dst

    key_spec = pl.BlockSpec((tile, B_KEY_DIM), lambda h, n: (order(n), h))
    val_spec = pl.BlockSpec((tile, B_VAL_DIM), lambda h, n: (order(n), h))
    in_specs = _gla_in_specs(tile, order, B_HEADS if reverse else 0) + [
        pl.BlockSpec((None, None, B_VAL_DIM, B_KEY_DIM), lambda h, n: (h, order(n), 0, 0)), val_spec]
    ins = [proj, proj, proj, la, states, d_o]
    if prev is not None:
        in_specs += [key_spec, key_spec, val_spec]
        ins += list(prev)
    return pl.pallas_call(
        body, name=name, grid=(B_HEADS, nt), in_specs=in_specs,
        out_specs=[key_spec, key_spec, val_spec, key_spec],
        out_shape=[jax.ShapeDtypeStruct((s, 512), F32), jax.ShapeDtypeStruct((s, 512), F32),
                   jax.ShapeDtypeStruct((s, D_MODEL), F32), jax.ShapeDtypeStruct((s, 512), F32)],
        scratch_shapes=[pltpu.VMEM((B_VAL_DIM, B_KEY_DIM), F32)],
        compiler_params=_params(("parallel", "arbitrary")),
    )(*ins)


def _r_spec(tile):
    return pl.BlockSpec((tile, D_MODEL), lambda i: (i, 2))


def gla_post_fwd(name, o_f, o_b, proj, gain):
    s = o_f.shape[0]
    tile = _row_tile(s)

    def body(of_ref, ob_ref, r_ref, g_ref, out_ref):
        out_ref[...] = _gla_post(of_ref[...], ob_ref[...], r_ref[...], g_ref[...]).astype(out_ref.dtype)

    row = pl.BlockSpec((tile, D_MODEL), lambda i: (i, 0))
    return pl.pallas_call(body, name=name, grid=(s // tile,),
                          in_specs=[row, row, _r_spec(tile), pl.BlockSpec(gain.shape, lambda i: (0, 0))], out_specs=row,
                          out_shape=jax.ShapeDtypeStruct((s, D_MODEL), BF16), compiler_params=_params(("parallel",)))(o_f, o_b, proj, gain)


def gla_post_bwd(name, o_f, o_b, proj, gain, d_out):
    s = o_f.shape[0]
    tile = _row_tile(s)

    def body(of_ref, ob_ref, r_ref, g_ref, d_ref, do_ref, dr_ref, dg_ref):
        _, vjp = jax.vjp(_gla_post, of_ref[...], ob_ref[...], r_ref[...], g_ref[...])
        d_of, _, dr, dg = vjp(d_ref[...])
        do_ref[...] = d_of
        dr_ref[...] = dr
        _acc_rows(dg_ref, dg)

    row = pl.BlockSpec((tile, D_MODEL), lambda i: (i, 0))
    return pl.pallas_call(
        body, name=name, grid=(s // tile,),
        in_specs=[row, row, _r_spec(tile), pl.BlockSpec(gain.shape, lambda i: (0, 0)), row],
        out_specs=[row, row, pl.BlockSpec((8, D_MODEL), lambda i: (0, 0))],
        out_shape=[jax.ShapeDtypeStruct((s, D_MODEL), F32), jax.ShapeDtypeStruct((s, D_MODEL), F32), jax.ShapeDtypeStruct((8, D_MODEL), F32)],
        compiler_params=_params(("arbitrary",)))(o_f, o_b, proj, gain, d_out)


def _hid(tile, where):
    return pl.BlockSpec((None, tile, FFN_BLK), where)


def _pair(tile, where):
    return pl.BlockSpec((2, None, tile, FFN_BLK), where)


def _w_gu_spec(layer, where_j):
    return pl.BlockSpec((2, None, None, D_MODEL, FFN_BLK), lambda *g: (0, where_j(*g), layer, 0, 0))


def ffn_fwd(name, h_mid, hn2, w_gu, w_down, layer):
    s = hn2.shape[0]
    tm = _row_tile(s, 1024)
    nt = s // tm

    def gu_body(x_ref, w_ref, gu_ref, act_ref):
        x = x_ref[...]
        g = _dot(x, w_ref[0], NN)
        u = _dot(x, w_ref[1], NN)
        gu_ref[0] = g
        gu_ref[1] = u
        act_ref[...] = _swiglu_act(g, u).astype(act_ref.dtype)

    gu, act = pl.pallas_call(
        gu_body, name=name + "_gu", grid=(4, nt),
        in_specs=[pl.BlockSpec((tm, D_MODEL), lambda j, i: (i, 0)), _w_gu_spec(layer, lambda j, i: j)],
        out_specs=[_pair(tm, lambda j, i: (0, j, i, 0)), _hid(tm, lambda j, i: (j, i, 0))],
        out_shape=[jax.ShapeDtypeStruct((2, 4, s, FFN_BLK), F32), jax.ShapeDtypeStruct((4, s, FFN_BLK), BF16)],
        compiler_params=_params(("parallel", "parallel")))(hn2, w_gu)
    row = pl.BlockSpec((tm, D_MODEL), lambda i, j: (i, 0))
    h_next = _matmul(name + "_down", act, w_down, dims=NN, grid=(nt, 4), red_axis=1,
                     a_spec=_hid(tm, lambda i, j: (j, i, 0)), b_spec=pl.BlockSpec((None, FFN_BLK, D_MODEL), lambda i, j: (j, 0, 0)),
                     o_spec=row, out_shape=jax.ShapeDtypeStruct((s, D_MODEL), F32), res=h_mid, res_spec=row)
    return h_next, gu, act


def ffn_bwd(name, dh_next, hn2, gu, act, w_gu, w_down, layer):
    s = hn2.shape[0]
    tm = _row_tile(s, 1024)
    nt = s // tm
    d_wd = _matmul(name + "_dwd", act, dh_next, dims=TN, grid=(4, nt), red_axis=1,
                   a_spec=_hid(tm, lambda j, i: (j, i, 0)), b_spec=pl.BlockSpec((tm, D_MODEL), lambda j, i: (i, 0)),
                   o_spec=pl.BlockSpec((None, FFN_BLK, D_MODEL), lambda j, i: (j, 0, 0)),
                   out_shape=jax.ShapeDtypeStruct((4, FFN_BLK, D_MODEL), BF16), acc_shape=(FFN_BLK, D_MODEL))

    def dgu_body(dy_ref, wd_ref, gu_ref, dgu_ref):
        d_act = _dot(dy_ref[...], wd_ref[...], NT)
        _, vjp = jax.vjp(_swiglu_act, gu_ref[0], gu_ref[1])
        dg, du = vjp(d_act)
        dgu_ref[0] = dg.astype(dgu_ref.dtype)
        dgu_ref[1] = du.astype(dgu_ref.dtype)

    d_gu = pl.pallas_call(
        dgu_body, name=name + "_dgu", grid=(4, nt),
        in_specs=[pl.BlockSpec((tm, D_MODEL), lambda j, i: (i, 0)), pl.BlockSpec((None, FFN_BLK, D_MODEL), lambda j, i: (j, 0, 0)),
                  _pair(tm, lambda j, i: (0, j, i, 0))],
        out_specs=_pair(tm, lambda j, i: (0, j, i, 0)), out_shape=jax.ShapeDtypeStruct((2, 4, s, FFN_BLK), BF16),
        compiler_params=_params(("parallel", "parallel")))(dh_next, w_down, gu)

    def dx_body(d_ref, w_ref, o_ref):
        prod = _dot(d_ref[0], w_ref[0], NT) + _dot(d_ref[1], w_ref[1], NT)

        @pl.when(pl.program_id(1) == 0)
        def _():
            o_ref[...] = prod

        @pl.when(pl.program_id(1) > 0)
        def _():
            o_ref[...] += prod

    d_hn2 = pl.pallas_call(
        dx_body, name=name + "_dx", grid=(nt, 4),
        in_specs=[_pair(tm, lambda i, j: (0, j, i, 0)), _w_gu_spec(layer, lambda i, j: j)],
        out_specs=pl.BlockSpec((tm, D_MODEL), lambda i, j: (i, 0)), out_shape=jax.ShapeDtypeStruct((s, D_MODEL), F32),
        compiler_params=_params(("parallel", "arbitrary")))(d_gu, w_gu)

    def dw_body(x_ref, d_ref, o_ref, acc_ref):
        x = x_ref[...]
        k = pl.program_id(1)
        for t in range(2):
            prod = _dot(x, d_ref[t], TN)

            @pl.when(k == 0)
            def _():
                acc_ref[t] = prod

            @pl.when(k > 0)
            def _():
                acc_ref[t] += prod

        @pl.when(k == nt - 1)
        def _():
            o_ref[...] = acc_ref[...].astype(o_ref.dtype)

    d_wgu = pl.pallas_call(
        dw_body, name=name + "_dwgu", grid=(4, nt),
        in_specs=[pl.BlockSpec((tm, D_MODEL), lambda j, i: (i, 0)), _pair(tm, lambda j, i: (0, j, i, 0))],
        out_specs=pl.BlockSpec((2, None, D_MODEL, FFN_BLK), lambda j, i: (0, j, 0, 0)),
        out_shape=jax.ShapeDtypeStruct((2, 4, D_MODEL, FFN_BLK), BF16),
        scratch_shapes=[pltpu.VMEM((2, D_MODEL, FFN_BLK), F32)],
        compiler_params=_params(("parallel", "arbitrary")))(hn2, d_gu)
    return d_hn2, d_wgu, d_wd


def _my_place():
    return lax.axis_index("x"), lax.axis_index("y"), lax.axis_index("c")


def _flip(place, k):
    x, y, c = place
    return (1 - x if k & 4 else x, 1 - y if k & 2 else y, 1 - c if k & 1 else c)


def _index(place):
    return 4 * place[0] + 2 * place[1] + place[2]


def all_gather(arrs):
    n = len(arrs)

    def body(*refs):
        ins, outs = refs[:n], refs[n:2 * n]
        send_sems, recv_sems, local_sems = refs[2 * n:]
        me = _my_place()
        sibling = _flip(me, 1)
        chips = (4, 2, 6)

        def copy(a, k, block, to, src=None):
            dst = outs[a].at[_index(block)]
            return pltpu.make_async_remote_copy(src_ref=dst if src is None else src, dst_ref=dst, send_sem=send_sems.at[a, k],
                                                recv_sem=recv_sems.at[a, k], device_id=to, device_id_type=MESH)

        started = []
        for a in range(n):
            mine = pltpu.make_async_copy(ins[a], outs[a].at[_index(me)], local_sems.at[a])
            mine.start()
            started.append(mine)
        first = []
        for a in range(n):
            first.append(copy(a, 0, me, sibling, src=ins[a]))
            first += [copy(a, 1 + j, me, _flip(me, k), src=ins[a]) for j, k in enumerate(chips)]
        for cp in first:
            cp.start()
        passed = []
        for a in range(n):
            for j, k in enumerate(chips):
                copy(a, 1 + j, _flip(me, k), me).wait_recv()
                fwd = copy(a, 4 + j, _flip(me, k), sibling)
                fwd.start()
                passed.append(fwd)
        for a in range(n):
            copy(a, 0, sibling, me).wait_recv()
            for j, k in enumerate(chips):
                copy(a, 4 + j, _flip(sibling, k), me).wait_recv()
        for cp in first + passed:
            cp.wait_send()
        for cp in started:
            cp.wait()

    return pl.pallas_call(
        body, name="all_gather_weights", in_specs=[ANY] * n, out_specs=[ANY] * n,
        out_shape=[jax.ShapeDtypeStruct((N_DEV,) + a.shape, a.dtype) for a in arrs],
        scratch_shapes=[pltpu.SemaphoreType.DMA((n, 7)), pltpu.SemaphoreType.DMA((n, 7)), pltpu.SemaphoreType.DMA((n,))],
    )(*arrs)


def exchange_partials(arrs):
    n = len(arrs)

    def body(*refs):
        ins, outs = refs[:n], refs[n:2 * n]
        send_sems, recv_sems, local_sems = refs[2 * n:]
        me = _my_place()
        local = []
        for a in range(n):
            cp = pltpu.make_async_copy(ins[a].at[_index(me)], outs[a].at[_index(me)], local_sems.at[a])
            cp.start()
            local.append(cp)

        def copy(a, k, src_block, dst_block):
            return pltpu.make_async_remote_copy(src_ref=ins[a].at[_index(src_block)], dst_ref=outs[a].at[_index(dst_block)],
                                                send_sem=send_sems.at[a, k - 1], recv_sem=recv_sems.at[a, k - 1],
                                                device_id=_flip(me, k), device_id_type=MESH)

        sent = []
        for a in range(n):
            for k in range(1, N_DEV):
                cp = copy(a, k, _flip(me, k), me)
                cp.start()
                sent.append(cp)
        for a in range(n):
            for k in range(1, N_DEV):
                copy(a, k, me, _flip(me, k)).wait_recv()
        for cp in sent:
            cp.wait_send()
        for cp in local:
            cp.wait()

    return pl.pallas_call(
        body, name="exchange_weight_grads", in_specs=[ANY] * n, out_specs=[ANY] * n,
        out_shape=[jax.ShapeDtypeStruct(a.shape, a.dtype) for a in arrs],
        scratch_shapes=[pltpu.SemaphoreType.DMA((n, 7)), pltpu.SemaphoreType.DMA((n, 7)), pltpu.SemaphoreType.DMA((n,))],
    )(*arrs)


def adamw_shard(name, parts, w, m, v, layer, tile):
    rows, cols = parts.shape[1:]
    assert rows % tile == 0
    off = layer * (rows // tile)

    def body(p_ref, w_ref, m_ref, v_ref, g_ref, d_ref, nm_ref, nv_ref):
        g = p_ref[0].astype(F32)
        for src in range(1, N_DEV):
            g = g + p_ref[src].astype(F32)
        g_ref[...] = g
        d_ref[...], nm_ref[...], nv_ref[...] = _adamw(w_ref[...], g, m_ref[...], v_ref[...])

    src_row = pl.BlockSpec((tile, cols), lambda i: (off + i, 0))
    row = pl.BlockSpec((tile, cols), lambda i: (i, 0))
    shape = jax.ShapeDtypeStruct((rows, cols), F32)
    return pl.pallas_call(body, name=name, grid=(rows // tile,),
                          in_specs=[pl.BlockSpec((N_DEV, tile, cols), lambda i: (0, i, 0)), src_row, src_row, src_row],
                          out_specs=[row] * 4, out_shape=[shape] * 4, compiler_params=_params(("parallel",)))(parts, w, m, v)


def allreduce_adamw_replicated(partial, w, m, v, n_loss_rows):
    rows = partial.shape[0]

    def body(p_ref, w_ref, m_ref, v_ref, g_ref, d_ref, nm_ref, nv_ref, loss_ref, recv_ref, send_sems, recv_sems):
        me = _my_place()
        recv_ref[_index(me)] = p_ref[...]
        copies = []
        for k in range(1, N_DEV):
            peer = _flip(me, k)
            cp = pltpu.make_async_remote_copy(src_ref=p_ref, dst_ref=recv_ref.at[_index(me)], send_sem=send_sems.at[k - 1],
                                              recv_sem=recv_sems.at[k - 1], device_id=peer, device_id_type=MESH)
            cp.start()
            copies.append((cp, peer))
        for k, (cp, peer) in enumerate(copies):
            pltpu.make_async_remote_copy(src_ref=p_ref, dst_ref=recv_ref.at[_index(peer)], send_sem=send_sems.at[k],
                                         recv_sem=recv_sems.at[k], device_id=peer, device_id_type=MESH).wait_recv()
        for cp, _ in copies:
            cp.wait_send()
        g = recv_ref[0]
        for src in range(1, N_DEV):
            g = g + recv_ref[src]
        g_ref[...] = g
        d_ref[...], nm_ref[...], nv_ref[...] = _adamw(w_ref[...], g, m_ref[...], v_ref[...])
        loss = (0.5 / D_MODEL) * jnp.sum(g[rows - n_loss_rows:, :])
        loss_ref[...] = jnp.full(loss_ref.shape, loss, F32)

    shape = jax.ShapeDtypeStruct((rows, LANES), F32)
    return pl.pallas_call(
        body, name="allreduce_adamw_replicated", in_specs=[VMEM_SPEC] * 4, out_specs=[VMEM_SPEC] * 5,
        out_shape=[shape] * 4 + [jax.ShapeDtypeStruct((8, LANES), F32)],
        scratch_shapes=[pltpu.VMEM((N_DEV, rows, LANES), F32), pltpu.SemaphoreType.DMA((7,)), pltpu.SemaphoreType.DMA((7,))],
    )(partial, w, m, v)


def _pack_rows(flat, cols):
    n = flat.shape[-1]
    rows = -(-n // cols)
    rows = -(-rows // 48) * 48
    flat = jnp.pad(flat, [(0, 0)] * (flat.ndim - 1) + [(0, rows * cols - n)])
    return flat.reshape(flat.shape[:-1] + (rows, cols))


def _unpack(flat, shapes):
    out, off = [], 0
    for shp in shapes:
        n = 1
        for d in shp:
            n *= d
        out.append(flat[off:off + n].reshape(shp))
        off += n
    return out


def _to_dev_cols(a):
    w = a.shape[-1] // N_DEV
    return jnp.moveaxis(a.reshape(a.shape[:-1] + (N_DEV, w)), -2, 0)


def _from_dev_cols(a):
    a = jnp.moveaxis(a, 0, -2)
    return a.reshape(a.shape[:-2] + (a.shape[-2] * a.shape[-1],))


def kernel(x, attn_norm, ffn_norm, a_w_in, a_q_norm, a_k_norm, a_w_out, b_w_in, b_w_gate_f, b_gate_bias_f, b_w_gate_b, b_gate_bias_b, b_out_norm, b_w_out, ffn_w_gate_up, ffn_w_down, loss_target, m_attn_norm, m_ffn_norm, m_a_w_in, m_a_q_norm, m_a_k_norm, m_a_w_out, m_b_w_in, m_b_w_gate_f, m_b_gate_bias_f, m_b_w_gate_b, m_b_gate_bias_b, m_b_out_norm, m_b_w_out, m_ffn_w_gate_up, m_ffn_w_down, v_attn_norm, v_ffn_norm, v_a_w_in, v_a_q_norm, v_a_k_norm, v_a_w_out, v_b_w_in, v_b_w_gate_f, v_b_gate_bias_f, v_b_w_gate_b, v_b_gate_bias_b, v_b_out_norm, v_b_w_out, v_ffn_w_gate_up, v_ffn_w_down):
    seq = x.shape[1]
    depth = attn_norm.shape[0]
    h = x.reshape(seq, D_MODEL)
    target = loss_target.reshape(seq, D_MODEL)
    n_a, n_b = a_w_in.shape[0], b_w_in.shape[0]

    small = jnp.concatenate([t.reshape(-1) for t in (b_w_gate_f, b_gate_bias_f, b_w_gate_b, b_gate_bias_b, b_out_norm)])
    small = _pack_rows(small, LANES)
    g_a_in, g_a_out, g_b_in, g_b_out, g_gu, g_down, g_small = all_gather(
        [a_w_in.astype(BF16), a_w_out.astype(BF16), b_w_in.astype(BF16), b_w_out.astype(BF16),
         ffn_w_gate_up.astype(BF16), ffn_w_down.astype(BF16), small])
    w_a_in = _from_dev_cols(g_a_in)
    w_a_out = jnp.moveaxis(g_a_out, 0, 1).reshape(n_a, D_MODEL, D_MODEL)
    w_b_in = _from_dev_cols(g_b_in)
    w_b_out = jnp.moveaxis(g_b_out, 0, 1).reshape(n_b, D_MODEL, D_MODEL)
    w_down = jnp.moveaxis(g_down, 0, 1).reshape(depth, 4, FFN_BLK, D_MODEL)
    w_gu = g_gu.reshape(2, 4, depth, D_MODEL, FFN_BLK)
    small_shapes = [t.shape for t in (b_w_gate_f, b_gate_bias_f, b_w_gate_b, b_gate_bias_b, b_out_norm)]
    per_dev = [_unpack(g_small[d].reshape(-1), small_shapes) for d in range(N_DEV)]
    wgf, bgf, wgb, bgb, onorm = [_from_dev_cols(jnp.stack([per_dev[d][t] for d in range(N_DEV)])) for t in range(5)]
    w_gate = jnp.zeros((n_b, LANES, D_MODEL), F32)
    w_gate = w_gate.at[:, 0:16, 0:512].set(wgf).at[:, 16:32, 512:1024].set(wgb)
    gate_bias = jnp.concatenate([bgf, bgb], axis=1).reshape(n_b, 1, D_MODEL)
    out_gain = onorm.reshape(n_b, 1, D_MODEL)
    w_b_main = w_b_in[:, :, :3072]
    w_b_z = jnp.pad(w_b_in[:, :, 3072:], ((0, 0), (0, 0), (0, LANES - 32)))

    dh, sq_err, grads = _forward_backward(h, target, attn_norm, ffn_norm, a_q_norm, a_k_norm, w_a_in, w_a_out, w_b_main, w_b_z,
                                          w_gate, gate_bias, out_gain, w_b_out, w_gu, w_down)
    sharded_w = (a_w_in, a_w_out, b_w_in, b_w_gate_f, b_gate_bias_f, b_w_gate_b, b_gate_bias_b, b_out_norm, b_w_out, ffn_w_gate_up, ffn_w_down)
    sharded_m = (m_a_w_in, m_a_w_out, m_b_w_in, m_b_w_gate_f, m_b_gate_bias_f, m_b_w_gate_b, m_b_gate_bias_b, m_b_out_norm, m_b_w_out, m_ffn_w_gate_up, m_ffn_w_down)
    sharded_v = (v_a_w_in, v_a_w_out, v_b_w_in, v_b_w_gate_f, v_b_gate_bias_f, v_b_w_gate_b, v_b_gate_bias_b, v_b_out_norm, v_b_w_out, v_ffn_w_gate_up, v_ffn_w_down)
    rep_w = (attn_norm, ffn_norm, a_q_norm, a_k_norm)
    rep_m = (m_attn_norm, m_ffn_norm, m_a_q_norm, m_a_k_norm)
    rep_v = (v_attn_norm, v_ffn_norm, v_a_q_norm, v_a_k_norm)
    loss, outs = _reduce_and_update(grads, sq_err, sharded_w, sharded_m, sharded_v, rep_w, rep_m, rep_v)
    return (loss, dh.reshape(x.shape), *outs)


def _forward_backward(h, target, attn_norm, ffn_norm, a_q_norm, a_k_norm, w_a_in, w_a_out, w_b_main, w_b_z, w_gate, gate_bias,
                      out_gain, w_b_out, w_gu, w_down):
    seq = h.shape[0]
    depth = attn_norm.shape[0]
    n_a, n_b = w_a_in.shape[0], w_b_main.shape[0]
    tab = _rope_table(seq)
    pair_gain = lambda g: jnp.concatenate([g, g]).reshape(1, LANES)

    saved = []
    for i in range(depth):
        j = i // 2
        nm = f"l{i}"
        hn = rmsnorm_fwd(nm + "_norm1", h, attn_norm[i].reshape(1, D_MODEL))
        if i % 2 == 0:
            qkv = mm_nn(nm + "_qkv", hn, w_a_in[j])
            os_, lses = [], []
            for g, (_, dil) in enumerate(A_GROUPS):
                o, lse = attn_fwd(f"{nm}_attn{g}", qkv, tab, pair_gain(a_q_norm[j, g]), pair_gain(a_k_norm[j, g]), g, dil)
                os_.append(o)
                lses.append(lse)
            mixed = attn_merge_fwd(nm + "_merge", os_, lses)
            h_mid = mm_nn(nm + "_out", mixed, w_a_out[j], res=h)
            mix_saved = (qkv, os_, lses, mixed)
        else:
            proj = mm_nn(nm + "_proj", hn, w_b_main[j])
            z = mm_nn(nm + "_z", hn, w_b_z[j])
            la = gla_gate_fwd(nm + "_gate", z, w_gate[j], gate_bias[j])
            o_f, st_f = gla_fwd(nm + "_gla_f", proj, la, False)
            o_b, st_b = gla_fwd(nm + "_gla_b", proj, la, True)
            mixed = gla_post_fwd(nm + "_post", o_f, o_b, proj, out_gain[j])
            h_mid = mm_nn(nm + "_out", mixed, w_b_out[j], res=h)
            mix_saved = (proj, z, la, o_f, st_f, o_b, st_b, mixed)
        hn2 = rmsnorm_fwd(nm + "_norm2", h_mid, ffn_norm[i].reshape(1, D_MODEL))
        h_next, gu, act = ffn_fwd(nm + "_ffn", h_mid, hn2, w_gu, w_down[i], i)
        saved.append((h, hn, mix_saved, h_mid, hn2, gu, act))
        h = h_next

    dh, sq_err = loss_and_grad(h, target)

    g_attn_norm, g_ffn_norm = [None] * depth, [None] * depth
    g_a_w_in, g_a_w_out, g_a_q, g_a_k = [None] * n_a, [None] * n_a, [None] * n_a, [None] * n_a
    g_b_w_in, g_b_w_out, g_w_gate, g_gate_bias, g_out_gain = ([None] * n_b for _ in range(5))
    g_w_gu, g_w_down = [None] * depth, [None] * depth
    for i in reversed(range(depth)):
        j = i // 2
        nm = f"l{i}b"
        h_in, hn, mix_saved, h_mid, hn2, gu, act = saved[i]
        d_hn2, g_w_gu[i], g_w_down[i] = ffn_bwd(nm + "_ffn", dh, hn2, gu, act, w_gu, w_down[i], i)
        dh_mid, g_ffn_norm[i] = rmsnorm_bwd(nm + "_norm2", h_mid, ffn_norm[i].reshape(1, D_MODEL), d_hn2, dh)
        if i % 2 == 0:
            qkv, os_, lses, mixed = mix_saved
            d_mixed = mm_nt(nm + "_dmixed", dh_mid, w_a_out[j])
            g_a_w_out[j] = mm_tn(nm + "_dwout", mixed, dh_mid, by_block=True)
            d_parts = attn_merge_bwd(nm + "_merge", os_, lses, d_mixed)
            gq_l, gk_l = [], []
            d_qkv = lax.empty(qkv.shape, F32)
            for g, (_, dil) in enumerate(A_GROUPS):
                d_qkv, dk_parts, dv_parts, dgq, dgk = attn_bwd(f"{nm}_attn{g}", qkv, tab, pair_gain(a_q_norm[j, g]),
                                                               pair_gain(a_k_norm[j, g]), os_[g], lses[g], d_parts[g], d_parts[3 + g], g, dil, d_qkv)
                d_qkv = attn_combine(f"{nm}_dk{g}", dk_parts, seq, dil, d_qkv, 3 * g + 1)
                d_qkv = attn_combine(f"{nm}_dv{g}", dv_parts, seq, dil, d_qkv, 3 * g + 2)
                gq_l.append(dgq[0, :A_HEAD_DIM])
                gk_l.append(dgk[0, :A_HEAD_DIM])
            g_a_q[j], g_a_k[j] = jnp.stack(gq_l), jnp.stack(gk_l)
            d_hn = mm_nt(nm + "_dhn", d_qkv, w_a_in[j])
            g_a_w_in[j] = mm_tn(nm + "_dwin", hn, d_qkv, tn=D_MODEL * 9 // N_DEV, by_block=True)
        else:
            proj, z, la, o_f, st_f, o_b, st_b, mixed = mix_saved
            d_mixed = mm_nt(nm + "_dmixed", dh_mid, w_b_out[j])
            g_b_w_out[j] = mm_tn(nm + "_dwout", mixed, dh_mid, by_block=True)
            d_o, d_r, dgain = gla_post_bwd(nm + "_post", o_f, o_b, proj, out_gain[j], d_mixed)
            g_out_gain[j] = dgain[0]
            dq, dk, dv, dla_f = gla_bwd(nm + "_gla_f", proj, la, st_f, d_o, False)
            dq, dk, dv, dla_b = gla_bwd(nm + "_gla_b", proj, la, st_b, d_o, True, prev=(dq, dk, dv))
            d_z, g_w_gate[j], dbias = gla_gate_bwd(nm + "_gate", z, w_gate[j], gate_bias[j], dla_f, dla_b)
            g_gate_bias[j] = dbias[0]
            d_proj = jnp.concatenate([dq, dk, dv, d_r], axis=1)
            d_hn = mm_nt(nm + "_dhn_z", d_z, w_b_z[j])
            d_hn = mm_nt(nm + "_dhn", d_proj, w_b_main[j], res=d_hn)
            g_b_w_in[j] = jnp.concatenate([mm_tn(nm + "_dwin", hn, d_proj), mm_tn(nm + "_dwz", hn, d_z)[:, :32]], axis=1)
        dh, g_attn_norm[i] = rmsnorm_bwd(nm + "_norm1", h_in, attn_norm[i].reshape(1, D_MODEL), d_hn, dh_mid)
    return dh, sq_err, (g_attn_norm, g_ffn_norm, g_a_w_in, g_a_w_out, g_a_q, g_a_k, g_b_w_in, g_b_w_out, g_w_gate, g_gate_bias,
                        g_out_gain, g_w_gu, g_w_down)


def _reduce_and_update(grads, sq_err, sharded_w, sharded_m, sharded_v, rep_w, rep_m, rep_v):
    (g_attn_norm, g_ffn_norm, g_a_w_in, g_a_w_out, g_a_q, g_a_k, g_b_w_in, g_b_w_out, g_w_gate, g_gate_bias, g_out_gain,
     g_w_gu, g_w_down) = grads
    depth, n_a, n_b = len(g_w_gu), len(g_a_w_in), len(g_b_w_in)

    g_w_gate = jnp.stack(g_w_gate)
    g_gate_bias = jnp.stack(g_gate_bias)
    small_parts = [_to_dev_cols(g_w_gate[:, 0:16, 0:512]), _to_dev_cols(g_gate_bias[:, 0:512]),
                   _to_dev_cols(g_w_gate[:, 16:32, 512:1024]), _to_dev_cols(g_gate_bias[:, 512:1024]),
                   _to_dev_cols(jnp.stack(g_out_gain).reshape(n_b, B_HEADS, B_VAL_DIM))]
    small_part = _pack_rows(jnp.concatenate([t.reshape(N_DEV, -1) for t in small_parts], axis=1), LANES).astype(BF16)
    families = [
        (0, g_a_w_in, 256),
        (1, [t.reshape(N_DEV, -1, D_MODEL) for t in g_a_w_out], 128),
        (2, [_to_dev_cols(t).astype(BF16) for t in g_b_w_in], 256),
        (8, [t.reshape(N_DEV, -1, D_MODEL) for t in g_b_w_out], 128),
        (9, [t.reshape(N_DEV, D_MODEL, FFN_BLK) for t in g_w_gu], 256),
        (10, [t.reshape(N_DEV, -1, D_MODEL) for t in g_w_down], 176),
    ]
    flat_parts = [p for _, parts, _ in families for p in parts] + [small_part]
    received = exchange_partials(flat_parts)
    sh_out = [None] * len(sharded_w)
    pos = 0
    for fam, parts, tile in families:
        w = sharded_w[fam]
        two_d = lambda t: t.reshape(-1, t.shape[-1])
        per_layer = []
        for layer in range(len(parts)):
            per_layer.append(adamw_shard(f"adamw_p{fam}_l{layer}", received[pos], two_d(w), two_d(sharded_m[fam]),
                                         two_d(sharded_v[fam]), layer, tile))
            pos += 1
        sh_out[fam] = [jnp.stack([per_layer[l][t] for l in range(len(parts))]).reshape(w.shape) for t in range(4)]
    small_ids = (3, 4, 5, 6, 7)
    pack_small = lambda ts: _pack_rows(jnp.concatenate([ts[i].reshape(-1) for i in small_ids]), LANES)
    small_out = adamw_shard("adamw_small", received[pos], pack_small(sharded_w), pack_small(sharded_m), pack_small(sharded_v), 0, 48)
    small_shapes = [sharded_w[i].shape for i in small_ids]
    for t in range(4):
        for i, val in zip(small_ids, _unpack(small_out[t].reshape(-1), small_shapes)):
            if sh_out[i] is None:
                sh_out[i] = [None] * 4
            sh_out[i][t] = val
    sh_grad, sh_delta, sh_m, sh_v = [[sh_out[i][t] for i in range(len(sharded_w))] for t in range(4)]

    rep_g = (jnp.stack([t[0] for t in g_attn_norm]), jnp.stack([t[0] for t in g_ffn_norm]), jnp.stack(g_a_q), jnp.stack(g_a_k))
    n_rep = sum(t.size for t in rep_w)
    n_rep_rows = -(-n_rep // (8 * LANES)) * 8
    n_loss_rows = 8 * D_MODEL // LANES

    def pack_rep(ts, tail):
        flat = jnp.concatenate([t.reshape(-1) for t in ts])
        flat = jnp.pad(flat, (0, n_rep_rows * LANES - n_rep))
        return jnp.concatenate([flat.reshape(n_rep_rows, LANES), tail], axis=0)

    zeros_tail = jnp.zeros((n_loss_rows, LANES), F32)
    rep_out = allreduce_adamw_replicated(pack_rep(rep_g, sq_err.reshape(n_loss_rows, LANES)), pack_rep(rep_w, zeros_tail),
                                         pack_rep(rep_m, zeros_tail), pack_rep(rep_v, zeros_tail + 1.0), n_loss_rows)
    rep_shapes = [t.shape for t in rep_w]
    r_grad, r_delta, r_m, r_v = [_unpack(p.reshape(-1), rep_shapes) for p in rep_out[:4]]
    loss = rep_out[4][0, 0]

    def ordered(rep, sh):
        return [rep[0], rep[1], sh[0], rep[2], rep[3]] + list(sh[1:])

    return loss, (*ordered(r_grad, sh_grad), *ordered(r_delta, sh_delta), *ordered(r_m, sh_m), *ordered(r_v, sh_v))
```

```python
import functools

import jax
import jax.numpy as jnp
from jax import lax
from jax.experimental import pallas as pl
from jax.experimental.pallas import tpu as pltpu

F32 = jnp.float32
BF16 = jnp.bfloat16
MXU_DTYPE = jnp.bfloat16

D_MODEL = 1024
N_DEV = 8
RMS_EPS = 1e-6
NEG_INF = -1e30
A_GROUPS = ((128, 1), (512, 4), (2048, 16))
A_HEAD_DIM = 64
A_HALF = 64
ATT_T = 128
ATT_TILE = 2048
B_HEADS = 4
B_KEY_DIM = 128
B_VAL_DIM = 256
B_CHUNK = 64
B_GATE_TAU = 16.0
FFN_HIDDEN = 2816
FFN_BLK = 2 * FFN_HIDDEN // N_DEV
ADAM_LR, ADAM_B1, ADAM_B2, ADAM_EPS, ADAM_WD, ADAM_STEP = 0.001, 0.9, 0.999, 1e-08, 0.01, 10
ROPE_THETA = 10000.0

V7X_VMEM_LIMIT = 56 * 1024 * 1024
LANES = 128
MESH = pl.DeviceIdType.MESH
ANY = pl.BlockSpec(memory_space=pl.ANY)
VMEM_SPEC = pl.BlockSpec(memory_space=pltpu.VMEM)

NN = ((1,), (0,))
NT = ((1,), (1,))
TN = ((0,), (0,))


def _dot(a, b, dims):
    return lax.dot_general(a.astype(MXU_DTYPE), b.astype(MXU_DTYPE), (dims, ((), ())), preferred_element_type=F32)


@jax.custom_vjp
def dot_nn(a, b):
    return _dot(a, b, NN)


@jax.custom_vjp
def dot_nt(a, b):
    return _dot(a, b, NT)


@jax.custom_vjp
def dot_tn(a, b):
    return _dot(a, b, TN)


dot_nn.defvjp(lambda a, b: (_dot(a, b, NN), (a, b)), lambda r, g: (dot_nt(g, r[1]), dot_tn(r[0], g)))
dot_nt.defvjp(lambda a, b: (_dot(a, b, NT), (a, b)), lambda r, g: (dot_nn(g, r[1]), dot_tn(g, r[0])))
dot_tn.defvjp(lambda a, b: (_dot(a, b, TN), (a, b)), lambda r, g: (dot_nt(r[1], g), dot_nn(r[0], g)))


def _bdot(a, b, dims):
    dn = (tuple((d[0] + 1,) for d in dims), ((0,), (0,)))
    return lax.dot_general(a.astype(MXU_DTYPE), b.astype(MXU_DTYPE), dn, preferred_element_type=F32)


@jax.custom_vjp
def bdot_nn(a, b):
    return _bdot(a, b, NN)


@jax.custom_vjp
def bdot_nt(a, b):
    return _bdot(a, b, NT)


@jax.custom_vjp
def bdot_tn(a, b):
    return _bdot(a, b, TN)


bdot_nn.defvjp(lambda a, b: (_bdot(a, b, NN), (a, b)), lambda r, g: (bdot_nt(g, r[1]), bdot_tn(r[0], g)))
bdot_nt.defvjp(lambda a, b: (_bdot(a, b, NT), (a, b)), lambda r, g: (bdot_nn(g, r[1]), bdot_tn(g, r[0])))
bdot_tn.defvjp(lambda a, b: (_bdot(a, b, TN), (a, b)), lambda r, g: (bdot_nt(r[1], g), bdot_nn(r[0], g)))


def _dot_f32(a, b):
    return lax.dot_general(a, b, (NN, ((), ())), precision=lax.Precision.HIGHEST, preferred_element_type=F32)


def _tri(n, upper):
    r = lax.broadcasted_iota(jnp.int32, (n, n), 0)
    c = lax.broadcasted_iota(jnp.int32, (n, n), 1)
    return jnp.where((c >= r) if upper else (c <= r), 1.0, 0.0).astype(F32)


def _chunk_cumsum(x, reverse):
    tri = jnp.broadcast_to(_tri(x.shape[1], reverse), (x.shape[0], x.shape[1], x.shape[1]))
    return lax.dot_general(tri, x, (((2,), (1,)), ((0,), (0,))), precision=lax.Precision.HIGHEST, preferred_element_type=F32)


@functools.partial(jax.custom_vjp, nondiff_argnums=(1,))
def cumsum_chunks(x, reverse):
    return _chunk_cumsum(x, reverse)


cumsum_chunks.defvjp(lambda x, reverse: (_chunk_cumsum(x, reverse), None), lambda reverse, _, g: (_chunk_cumsum(g, not reverse),))


def _swap32_raw(x):
    lane = lax.broadcasted_iota(jnp.int32, x.shape, 1)
    return jnp.where((lane % 64) < 32, pltpu.roll(x, 96, 1), pltpu.roll(x, 32, 1))


@jax.custom_vjp
def swap32(x):
    return _swap32_raw(x)


swap32.defvjp(lambda x: (_swap32_raw(x), None), lambda _, g: (_swap32_raw(g),))


def _rms(x, gain):
    return x * lax.rsqrt(jnp.mean(x * x, axis=-1, keepdims=True) + RMS_EPS) * gain


def _sigmoid(x):
    return 1.0 / (1.0 + jnp.exp(-x))


def _log_sigmoid(x):
    return jnp.minimum(x, 0.0) - jnp.log(1.0 + jnp.exp(-jnp.abs(x)))


def _qk_prep(x, tab, gain):
    lo = lax.broadcasted_iota(jnp.int32, (1, LANES), 1) < A_HEAD_DIM
    x2 = x * x
    s_lo = jnp.sum(jnp.where(lo, x2, 0.0), axis=-1, keepdims=True)
    s_hi = jnp.sum(jnp.where(lo, 0.0, x2), axis=-1, keepdims=True)
    xn = (x * lax.rsqrt(jnp.where(lo, s_lo, s_hi) / A_HEAD_DIM + RMS_EPS)) * gain
    return xn * tab[:, :LANES] + swap32(xn) * tab[:, LANES:]


def _stack_heads(x):
    lo = lax.broadcasted_iota(jnp.int32, (1, LANES), 1) < A_HEAD_DIM
    return jnp.concatenate([jnp.where(lo, x, 0.0), jnp.where(lo, 0.0, x)], axis=0)


def _unstack_heads(x):
    t = x.shape[0] // 2
    lo = lax.broadcasted_iota(jnp.int32, (1, LANES), 1) < A_HEAD_DIM
    return jnp.where(lo, x[:t], x[t:])


def _attn_scores(q, k, valid):
    s = _dot(_stack_heads(q), k, NT) * (A_HEAD_DIM ** -0.5)
    return jnp.where(valid, s, NEG_INF)


def _attn_job(q, k, v, valid):
    s = _attn_scores(q, k, valid)
    mx = jnp.max(s, axis=-1, keepdims=True)
    p = jnp.exp(s - mx)
    l = jnp.sum(p, axis=-1, keepdims=True)
    out = _unstack_heads(_dot(p, v, NN) / l)
    lse = mx + jnp.log(l)
    t = q.shape[0]
    lo = lax.broadcasted_iota(jnp.int32, (1, LANES), 1) < A_HEAD_DIM
    return out, jnp.where(lo, lse[:t], lse[t:])


def _attn_job_bwd(q, k, v, valid, d_out, out, lse, d_lse):
    qs = _stack_heads(q)
    lo = lax.broadcasted_iota(jnp.int32, (1, LANES), 1) < A_HEAD_DIM
    other = pltpu.roll(lse, A_HEAD_DIM, 1)
    row_lse = jnp.concatenate([jnp.where(lo, lse, other), jnp.where(lo, other, lse)], axis=0)
    p = jnp.exp(_attn_scores(q, k, valid) - jnp.concatenate([row_lse, row_lse], axis=1))
    dos = _stack_heads(d_out)
    dv = _dot(p, dos, TN)
    dp = _dot(dos, v, NT)
    inner = jnp.sum(dos * _stack_heads(out) - _stack_heads(d_lse), axis=-1, keepdims=True)
    ds = p * (dp - inner) * (A_HEAD_DIM ** -0.5)
    return _unstack_heads(_dot(ds, k, NN)), _dot(ds, qs, TN), dv


def _merge_groups(o0, o1, o2, l0, l1, l2):
    mx = lax.stop_gradient(jnp.maximum(jnp.maximum(l0, l1), l2))
    e0, e1, e2 = jnp.exp(l0 - mx), jnp.exp(l1 - mx), jnp.exp(l2 - mx)
    den = e0 + e1 + e2
    return (e0 / den) * o0 + (e1 / den) * o1 + (e2 / den) * o2


def _gla_tile(q, k, v, la, st, reverse):
    t = q.shape[0]
    nc = t // B_CHUNK
    split = lambda x: x.reshape(nc, B_CHUNK, x.shape[1])
    q, k, v, la = split(q * (B_KEY_DIM ** -0.5)), split(k), split(v), split(la)
    r = lax.broadcasted_iota(jnp.int32, (1, B_CHUNK, B_CHUNK), 1)
    c = lax.broadcasted_iota(jnp.int32, (1, B_CHUNK, B_CHUNK), 2)
    mask = (c > r) if reverse else (c <= r)
    b = cumsum_chunks(la, reverse)
    tot = jnp.sum(la, axis=1, keepdims=True)
    q_t = q * jnp.exp(b)
    k_t = k * jnp.exp(-b)
    k_end = k * jnp.exp(tot - b)
    attn = jnp.where(mask, bdot_nt(q_t, k_t), 0.0)
    kv = bdot_tn(v, k_end)
    decay = jnp.exp(tot)
    entering = [None] * nc
    for ci in (range(nc - 1, -1, -1) if reverse else range(nc)):
        entering[ci] = st
        st = st * decay[ci] + kv[ci]
    out = bdot_nn(attn, v) + bdot_nt(q_t, jnp.concatenate([e[None] for e in entering], axis=0))
    return out.reshape(t, out.shape[2]), st


def _gla_post(o_f, o_b, r, gain):
    o = o_f + o_b
    heads = [_rms(o[:, h * B_VAL_DIM:(h + 1) * B_VAL_DIM], gain[:, h * B_VAL_DIM:(h + 1) * B_VAL_DIM]) for h in range(B_HEADS)]
    return jnp.concatenate(heads, axis=1) * (r * _sigmoid(r))


def _gate(z, wg, bias):
    return _log_sigmoid(dot_nn(z, wg) + bias) / B_GATE_TAU


def _swiglu_act(g, u):
    return (g * _sigmoid(g)) * u


def _adamw(w, g, m, v):
    m = ADAM_B1 * m + (1.0 - ADAM_B1) * g
    v = ADAM_B2 * v + (1.0 - ADAM_B2) * jnp.square(g)
    m_hat = m / (1.0 - ADAM_B1 ** ADAM_STEP)
    v_hat = v / (1.0 - ADAM_B2 ** ADAM_STEP)
    delta = -ADAM_LR * (m_hat / (jnp.sqrt(v_hat) + ADAM_EPS) + ADAM_WD * w)
    return delta, m, v


def _params(sem=None):
    return pltpu.CompilerParams(dimension_semantics=sem, vmem_limit_bytes=V7X_VMEM_LIMIT)


def _row_tile(s, want=512):
    t = min(want, s)
    assert s % t == 0
    return t


def _matmul(name, a, b, *, dims, grid, a_spec, b_spec, o_spec, out_shape, red_axis=None, res=None, res_spec=None, acc_shape=None):
    n_red = grid[red_axis] if red_axis is not None else 1

    def body(*refs):
        a_ref, b_ref = refs[0], refs[1]
        r_ref = refs[2] if res is not None else None
        o_ref = refs[3] if res is not None else refs[2]
        prod = lax.dot_general(a_ref[...].astype(MXU_DTYPE), b_ref[...].astype(MXU_DTYPE), (dims, ((), ())),
                               preferred_element_type=F32)
        if red_axis is None:
            if r_ref is not None:
                prod = prod + r_ref[...]
            o_ref[...] = prod.astype(o_ref.dtype)
            return
        acc = refs[-1] if acc_shape is not None else o_ref
        k = pl.program_id(red_axis)

        @pl.when(k == 0)
        def _():
            acc[...] = prod + r_ref[...] if r_ref is not None else prod

        @pl.when(k > 0)
        def _():
            acc[...] += prod

        if acc_shape is not None:
            @pl.when(k == n_red - 1)
            def _():
                o_ref[...] = acc[...].astype(o_ref.dtype)

    ins = [a, b] + ([res] if res is not None else [])
    specs = [a_spec, b_spec] + ([res_spec] if res is not None else [])
    sem = tuple("arbitrary" if i == red_axis else "parallel" for i in range(len(grid)))
    return pl.pallas_call(body, name=name, grid=grid, in_specs=specs, out_specs=o_spec, out_shape=out_shape,
                          scratch_shapes=[pltpu.VMEM(acc_shape, F32)] if acc_shape is not None else [],
                          compiler_params=_params(sem))(*ins)


def mm_nn(name, x, w, *, res=None, out_dtype=F32, tn=1024):
    m, k = x.shape
    n = w.shape[1]
    tm, tn = _row_tile(m, 1024), min(tn, n)
    return _matmul(name, x, w, dims=NN, grid=(n // tn, m // tm),
                   a_spec=pl.BlockSpec((tm, k), lambda j, i: (i, 0)), b_spec=pl.BlockSpec((k, tn), lambda j, i: (0, j)),
                   o_spec=pl.BlockSpec((tm, tn), lambda j, i: (i, j)), out_shape=jax.ShapeDtypeStruct((m, n), out_dtype),
                   res=res, res_spec=pl.BlockSpec((tm, tn), lambda j, i: (i, j)))


def mm_nt(name, dy, w, *, res=None, tn=2304):
    m, n = dy.shape
    k = w.shape[0]
    tm, tn = _row_tile(m, 1024), (tn if n % tn == 0 else min(1024, n))
    return _matmul(name, dy, w, dims=NT, grid=(m // tm, n // tn), red_axis=1,
                   a_spec=pl.BlockSpec((tm, tn), lambda i, j: (i, j)), b_spec=pl.BlockSpec((k, tn), lambda i, j: (0, j)),
                   o_spec=pl.BlockSpec((tm, k), lambda i, j: (i, 0)), out_shape=jax.ShapeDtypeStruct((m, k), F32),
                   res=res, res_spec=pl.BlockSpec((tm, k), lambda i, j: (i, 0)))


def mm_tn(name, x, dy, *, tn=1024, by_block=False):
    m, k = x.shape
    n = dy.shape[1]
    tm, tn = _row_tile(m, 2048), min(tn, n)
    if by_block:
        o_spec, out_shape, acc = pl.BlockSpec((None, k, tn), lambda j, i: (j, 0, 0)), jax.ShapeDtypeStruct((n // tn, k, tn), BF16), (k, tn)
    else:
        o_spec, out_shape, acc = pl.BlockSpec((k, tn), lambda j, i: (0, j)), jax.ShapeDtypeStruct((k, n), F32), None
    return _matmul(name, x, dy, dims=TN, grid=(n // tn, m // tm), red_axis=1,
                   a_spec=pl.BlockSpec((tm, k), lambda j, i: (i, 0)), b_spec=pl.BlockSpec((tm, tn), lambda j, i: (i, j)),
                   o_spec=o_spec, out_shape=out_shape, acc_shape=acc)


def _rows_call(name, body, ins, outs, s, tile):
    in_specs = []
    for a, kind in ins:
        if kind == "row":
            in_specs.append(pl.BlockSpec((tile, a.shape[1]), lambda i: (i, 0)))
        else:
            in_specs.append(pl.BlockSpec(a.shape, lambda i, nd=a.ndim: (0,) * nd))
    out_specs, out_shape = [], []
    for cols, dt, kind in outs:
        if kind == "row":
            out_specs.append(pl.BlockSpec((tile, cols), lambda i: (i, 0)))
            out_shape.append(jax.ShapeDtypeStruct((s, cols), dt))
        else:
            out_specs.append(pl.BlockSpec((8, cols), lambda i: (0, 0)))
            out_shape.append(jax.ShapeDtypeStruct((8, cols), dt))
    has_acc = any(kind == "acc" for _, _, kind in outs)
    return pl.pallas_call(body, name=name, grid=(s // tile,), in_specs=in_specs, out_specs=out_specs, out_shape=out_shape,
                          compiler_params=_params(("arbitrary",) if has_acc else ("parallel",)))(*[a for a, _ in ins])


def _acc_rows(ref, val):
    @pl.when(pl.program_id(0) == 0)
    def _():
        ref[...] = jnp.zeros_like(ref)

    ref[...] += jnp.broadcast_to(val, ref.shape)


def rmsnorm_fwd(name, h, gain):
    s = h.shape[0]

    def body(h_ref, g_ref, o_ref):
        o_ref[...] = _rms(h_ref[...], g_ref[...]).astype(o_ref.dtype)

    return _rows_call(name, body, [(h, "row"), (gain, "full")], [(D_MODEL, BF16, "row")], s, _row_tile(s))[0]


def rmsnorm_bwd(name, h, gain, d_hn, d_res):
    s = h.shape[0]

    def body(h_ref, g_ref, dy_ref, dr_ref, dh_ref, dg_ref):
        _, vjp = jax.vjp(_rms, h_ref[...], g_ref[...])
        dh, dg = vjp(dy_ref[...])
        dh_ref[...] = dh + dr_ref[...]
        _acc_rows(dg_ref, dg)

    return _rows_call(name, body, [(h, "row"), (gain, "full"), (d_hn, "row"), (d_res, "row")],
                      [(D_MODEL, F32, "row"), (D_MODEL, F32, "acc")], s, _row_tile(s))


def loss_and_grad(y, target):
    s = y.shape[0]
    tile = _row_tile(s)

    def body(y_ref, t_ref, dy_ref, acc_ref):
        diff = y_ref[...] - t_ref[...]
        dy_ref[...] = diff * (1.0 / D_MODEL)

        @pl.when(pl.program_id(0) == 0)
        def _():
            acc_ref[...] = jnp.zeros_like(acc_ref)

        acc_ref[...] += jnp.sum((diff * diff).reshape(tile // 8, 8, D_MODEL), axis=0)

    return _rows_call("loss_head", body, [(y, "row"), (target, "row")], [(D_MODEL, F32, "row"), (D_MODEL, F32, "acc")], s, tile)


def _rope_table(s):
    half = A_HEAD_DIM // 2
    inv_freq = ROPE_THETA ** (-jnp.arange(half, dtype=F32) / half)
    ang = jnp.arange(s).astype(F32)[:, None] * inv_freq[None, :]
    cos, sin = jnp.cos(ang), jnp.sin(ang)
    return jnp.concatenate([cos, cos, cos, cos, -sin, sin, -sin, sin], axis=1)


def _attn_geometry(s, dil):
    tile = min(ATT_TILE, s)
    halo = A_HALF * dil
    assert s % tile == 0 and tile % (ATT_T * dil) == 0 and tile % halo == 0
    return tile, halo, tile // (ATT_T * dil)


def _attn_in_specs(grp, s, tile, halo):
    hb, n_hb = tile // halo, s // halo
    cq, ck, cv = (24 * grp + 8 * t for t in range(3))

    def main(col, per_pair, width=LANES):
        return pl.BlockSpec((tile, width), lambda i, j: (i, col + per_pair * j))

    def prev(col, per_pair, width=LANES):
        return pl.BlockSpec((halo, width), lambda i, j: (jnp.maximum(i * hb - 1, 0), col + per_pair * j))

    def nxt(col, per_pair, width=LANES):
        return pl.BlockSpec((halo, width), lambda i, j: (jnp.minimum((i + 1) * hb, n_hb - 1), col + per_pair * j))

    return [main(cq, 1), prev(ck, 1), main(ck, 1), nxt(ck, 1), prev(cv, 1), main(cv, 1), nxt(cv, 1),
            prev(0, 0, 256), main(0, 0, 256), nxt(0, 0, 256)]


def _attn_band():
    r = lax.broadcasted_iota(jnp.int32, (2 * ATT_T, 1), 0)
    tq = jnp.where(r >= ATT_T, r - ATT_T, r)
    tk = lax.broadcasted_iota(jnp.int32, (1, ATT_T + 2 * A_HALF), 1) - A_HALF
    return jnp.abs(tk - tq) <= A_HALF


def _attn_valid(band, base, length):
    tk = base - A_HALF + lax.broadcasted_iota(jnp.int32, (1, ATT_T + 2 * A_HALF), 1)
    return band & (tk >= 0) & (tk < length)


def _jobs(tile, dil, n_sub):
    return [(u * ATT_T * dil + p, u * ATT_T) for u in range(n_sub) for p in range(dil)]


def _rows(start, size, dil):
    return pl.ds(start, size, stride=dil) if dil > 1 else pl.ds(start, size)


def attn_fwd(name, qkv, tab, gq, gk, grp, dil):
    s = qkv.shape[0]
    tile, halo, n_sub = _attn_geometry(s, dil)
    length, per_tile = s // dil, tile // dil
    nk = ATT_T + 2 * A_HALF

    def body(q_ref, kp_ref, km_ref, kn_ref, vp_ref, vm_ref, vn_ref, tp_ref, tm_ref, tn_ref, gq_ref, gk_ref, o_ref, lse_ref,
             q_buf, k_buf, v_buf):
        i, pair = pl.program_id(0), pl.program_id(1)
        q_buf[...] = _qk_prep(q_ref[...], tm_ref[...], gq_ref[...])
        for ref, t_ref, lo, n in ((kp_ref, tp_ref, 0, halo), (km_ref, tm_ref, halo, tile), (kn_ref, tn_ref, halo + tile, halo)):
            k_buf[lo:lo + n, :] = _qk_prep(ref[...], t_ref[...], gk_ref[...])
        for ref, lo, n in ((vp_ref, 0, halo), (vm_ref, halo, tile), (vn_ref, halo + tile, halo)):
            v_buf[lo:lo + n, :] = ref[...]

        band = _attn_band()
        for start, t0 in _jobs(tile, dil, n_sub):
            valid = _attn_valid(band, i * per_tile + t0, length)
            o, lse = _attn_job(q_buf[_rows(start, ATT_T, dil), :], k_buf[_rows(start, nk, dil), :],
                               v_buf[_rows(start, nk, dil), :], valid)
            o_ref[_rows(start, ATT_T, dil), :] = o
            lse_ref[_rows(start, ATT_T, dil), :] = lse

    full = lambda a: pl.BlockSpec(a.shape, lambda i, j: (0, 0))
    return pl.pallas_call(
        body, name=name, grid=(s // tile, D_MODEL // LANES),
        in_specs=_attn_in_specs(grp, s, tile, halo) + [full(gq), full(gk)],
        out_specs=[pl.BlockSpec((tile, LANES), lambda i, j: (i, j)), pl.BlockSpec((tile, LANES), lambda i, j: (i, j))],
        out_shape=[jax.ShapeDtypeStruct((s, D_MODEL), F32), jax.ShapeDtypeStruct((s, D_MODEL), F32)],
        scratch_shapes=[pltpu.VMEM((tile, LANES), F32), pltpu.VMEM((tile + 2 * halo, LANES), F32), pltpu.VMEM((tile + 2 * halo, LANES), F32)],
        compiler_params=_params(("parallel", "parallel")),
    )(qkv, qkv, qkv, qkv, qkv, qkv, qkv, tab, tab, tab, gq, gk)


def attn_bwd(name, qkv, tab, gq, gk, o, lse, d_o, d_lse, grp, dil, d_qkv):
    s = qkv.shape[0]
    tile, halo, n_sub = _attn_geometry(s, dil)
    length, per_tile = s // dil, tile // dil
    nt = s // tile
    nk = ATT_T + 2 * A_HALF
    pieces = ((0, halo), (halo, tile), (halo + tile, halo))

    def body(q_ref, kp_ref, km_ref, kn_ref, vp_ref, vm_ref, vn_ref, tp_ref, tm_ref, tn_ref, gq_ref, gk_ref, o_ref, l_ref, do_ref, dl_ref,
             _, dq_ref, dkp_ref, dkm_ref, dkn_ref, dvp_ref, dvm_ref, dvn_ref, dgq_ref, dgk_ref, q_buf, k_buf, v_buf, dq_buf, dk_buf, dv_buf):
        i, pair = pl.program_id(0), pl.program_id(1)
        k_refs, t_refs = (kp_ref, km_ref, kn_ref), (tp_ref, tm_ref, tn_ref)
        q_buf[...] = _qk_prep(q_ref[...], tm_ref[...], gq_ref[...])
        for ref, t_ref, (lo, n) in zip(k_refs, t_refs, pieces):
            k_buf[lo:lo + n, :] = _qk_prep(ref[...], t_ref[...], gk_ref[...])
        for ref, (lo, n) in zip((vp_ref, vm_ref, vn_ref), pieces):
            v_buf[lo:lo + n, :] = ref[...]
        dk_buf[...] = jnp.zeros_like(dk_buf)
        dv_buf[...] = jnp.zeros_like(dv_buf)
        band = _attn_band()
        for start, t0 in _jobs(tile, dil, n_sub):
            valid = _attn_valid(band, i * per_tile + t0, length)
            rows = _rows(start, ATT_T, dil)
            dq, dk, dv = _attn_job_bwd(q_buf[rows, :], k_buf[_rows(start, nk, dil), :], v_buf[_rows(start, nk, dil), :], valid,
                                       do_ref[rows, :], o_ref[rows, :], l_ref[rows, :], dl_ref[rows, :])
            dq_buf[_rows(start, ATT_T, dil), :] = dq
            dk_buf[_rows(start, nk, dil), :] += dk
            dv_buf[_rows(start, nk, dil), :] += dv
        _, vjp = jax.vjp(lambda x, g: _qk_prep(x, tm_ref[...], g), q_ref[...], gq_ref[...])
        dq, dgq = vjp(dq_buf[...])
        dq_ref[...] = dq.astype(dq_ref.dtype)
        dgk = jnp.zeros((1, LANES), F32)
        for ref, t_ref, out_ref, (lo, n) in zip(k_refs, t_refs, (dkp_ref, dkm_ref, dkn_ref), pieces):
            _, vjp = jax.vjp(lambda x, g: _qk_prep(x, t_ref[...], g), ref[...], gk_ref[...])
            out_ref[...], dgk_piece = vjp(dk_buf[lo:lo + n, :])
            dgk = dgk + dgk_piece
        for out_ref, (lo, n) in zip((dvp_ref, dvm_ref, dvn_ref), pieces):
            out_ref[...] = dv_buf[lo:lo + n, :]

        @pl.when((i == 0) & (pair == 0))
        def _():
            dgq_ref[...] = jnp.zeros_like(dgq_ref)
            dgk_ref[...] = jnp.zeros_like(dgk_ref)

        dgq_ref[...] += jnp.broadcast_to(dgq + pltpu.roll(dgq, A_HEAD_DIM, 1), dgq_ref.shape)
        dgk_ref[...] += jnp.broadcast_to(dgk + pltpu.roll(dgk, A_HEAD_DIM, 1), dgk_ref.shape)

    full = lambda a: pl.BlockSpec(a.shape, lambda i, j: (0, 0))
    main_o = pl.BlockSpec((tile, LANES), lambda i, j: (i, j))
    edge_o = pl.BlockSpec((None, halo, LANES), lambda i, j: (i, 0, j))
    main_s = jax.ShapeDtypeStruct((s, D_MODEL), F32)
    edge_s = jax.ShapeDtypeStruct((nt, halo, D_MODEL), F32)
    acc_o = pl.BlockSpec((8, LANES), lambda i, j: (0, 0))
    acc_s = jax.ShapeDtypeStruct((8, LANES), F32)
    big = pltpu.VMEM((tile + 2 * halo, LANES), F32)
    own = pltpu.VMEM((tile, LANES), F32)
    outs = pl.pallas_call(
        body, name=name, grid=(nt, D_MODEL // LANES),
        in_specs=_attn_in_specs(grp, s, tile, halo) + [full(gq), full(gk), main_o, main_o, main_o, main_o, ANY],
        out_specs=[pl.BlockSpec((tile, LANES), lambda i, j: (i, 24 * grp + j)), edge_o, main_o, edge_o, edge_o, main_o, edge_o, acc_o, acc_o],
        out_shape=[jax.ShapeDtypeStruct(d_qkv.shape, d_qkv.dtype), edge_s, main_s, edge_s, edge_s, main_s, edge_s, acc_s, acc_s],
        scratch_shapes=[own, big, big, own, big, big],
        input_output_aliases={16: 0},
        compiler_params=_params(("arbitrary", "arbitrary")),
    )(qkv, qkv, qkv, qkv, qkv, qkv, qkv, tab, tab, tab, gq, gk, o, lse, d_o, d_lse, d_qkv)
    d_qkv, dkp, dkm, dkn, dvp, dvm, dvn, dgq, dgk = outs
    return d_qkv, (dkp, dkm, dkn), (dvp, dvm, dvn), dgq, dgk


def attn_combine(name, parts, s, dil, d_qkv, col):
    prev_part, main_part, next_part = parts
    tile, halo, _ = _attn_geometry(s, dil)
    nt = s // tile
    cols = D_MODEL // 2

    def body(m_ref, from_prev_ref, from_next_ref, _, o_ref):
        i = pl.program_id(0)
        o_ref[...] = m_ref[...].astype(o_ref.dtype)
        head = m_ref[0:halo, :] + jnp.where(i > 0, from_prev_ref[...], 0.0)
        o_ref[0:halo, :] = head.astype(o_ref.dtype)
        tail = m_ref[tile - halo:tile, :] + jnp.where(i < nt - 1, from_next_ref[...], 0.0)
        o_ref[tile - halo:tile, :] = tail.astype(o_ref.dtype)

    return pl.pallas_call(
        body, name=name, grid=(nt, D_MODEL // cols),
        in_specs=[pl.BlockSpec((tile, cols), lambda i, c: (i, c)),
                  pl.BlockSpec((None, halo, cols), lambda i, c: (jnp.maximum(i - 1, 0), 0, c)),
                  pl.BlockSpec((None, halo, cols), lambda i, c: (jnp.minimum(i + 1, nt - 1), 0, c)), ANY],
        out_specs=pl.BlockSpec((tile, cols), lambda i, c: (i, (D_MODEL // cols) * col + c)),
        out_shape=jax.ShapeDtypeStruct(d_qkv.shape, d_qkv.dtype), input_output_aliases={3: 0},
        compiler_params=_params(("parallel", "parallel")),
    )(main_part, next_part, prev_part, d_qkv)


def attn_merge_fwd(name, os_, lses):
    s = os_[0].shape[0]

    def body(o0, o1, o2, l0, l1, l2, out_ref):
        out_ref[...] = _merge_groups(o0[...], o1[...], o2[...], l0[...], l1[...], l2[...]).astype(out_ref.dtype)

    return _rows_call(name, body, [(a, "row") for a in (*os_, *lses)], [(D_MODEL, BF16, "row")], s, _row_tile(s, 256))[0]


def attn_merge_bwd(name, os_, lses, d_out):
    s = os_[0].shape[0]

    def body(o0, o1, o2, l0, l1, l2, d_ref, *outs):
        _, vjp = jax.vjp(_merge_groups, o0[...], o1[...], o2[...], l0[...], l1[...], l2[...])
        for ref, val in zip(outs, vjp(d_ref[...])):
            ref[...] = val

    return _rows_call(name, body, [(a, "row") for a in (*os_, *lses, d_out)],
                      [(D_MODEL, F32, "row")] * 6, s, _row_tile(s, 256))


def gla_gate_fwd(name, z, wg, bias):
    s = z.shape[0]

    def body(z_ref, w_ref, b_ref, o_ref):
        o_ref[...] = _gate(z_ref[...], w_ref[...], b_ref[...])

    return _rows_call(name, body, [(z, "row"), (wg, "full"), (bias, "full")], [(D_MODEL, F32, "row")], s, _row_tile(s))[0]


def gla_gate_bwd(name, z, wg, bias, d_la_f, d_la_b):
    s = z.shape[0]
    tile = _row_tile(s)

    def body(z_ref, w_ref, b_ref, df_ref, db_ref, dz_ref, dw_ref, dbias_ref):
        _, vjp = jax.vjp(_gate, z_ref[...], w_ref[...], b_ref[...])
        dz, dw, dbias = vjp(jnp.concatenate([df_ref[...], db_ref[...]], axis=1))
        dz_ref[...] = dz

        @pl.when(pl.program_id(0) == 0)
        def _():
            dw_ref[...] = jnp.zeros_like(dw_ref)

        dw_ref[...] += dw
        _acc_rows(dbias_ref, dbias)

    return pl.pallas_call(
        body, name=name, grid=(s // tile,),
        in_specs=[pl.BlockSpec((tile, LANES), lambda i: (i, 0)), pl.BlockSpec(wg.shape, lambda i: (0, 0)),
                  pl.BlockSpec(bias.shape, lambda i: (0, 0)), pl.BlockSpec((tile, 512), lambda i: (i, 0)),
                  pl.BlockSpec((tile, 512), lambda i: (i, 0))],
        out_specs=[pl.BlockSpec((tile, LANES), lambda i: (i, 0)), pl.BlockSpec(wg.shape, lambda i: (0, 0)),
                   pl.BlockSpec((8, D_MODEL), lambda i: (0, 0))],
        out_shape=[jax.ShapeDtypeStruct((s, LANES), F32), jax.ShapeDtypeStruct(wg.shape, F32), jax.ShapeDtypeStruct((8, D_MODEL), F32)],
        compiler_params=_params(("arbitrary",)),
    )(z, wg, bias, d_la_f, d_la_b)


def _gla_in_specs(tile, order, la_col0):
    t = order
    return [pl.BlockSpec((tile, B_KEY_DIM), lambda h, n: (t(n), h)),
            pl.BlockSpec((tile, B_KEY_DIM), lambda h, n: (t(n), B_HEADS + h)),
            pl.BlockSpec((tile, B_VAL_DIM), lambda h, n: (t(n), B_HEADS + h)),
            pl.BlockSpec((tile, B_KEY_DIM), lambda h, n: (t(n), la_col0 + h))]


def gla_fwd(name, proj, la, reverse):
    s = proj.shape[0]
    tile = _row_tile(s)
    nt = s // tile
    order = (lambda n: nt - 1 - n) if reverse else (lambda n: n)

    def body(q_ref, k_ref, v_ref, la_ref, o_ref, st_ref, st_scr):
        @pl.when(pl.program_id(1) == 0)
        def _():
            st_scr[...] = jnp.zeros_like(st_scr)

        st_ref[...] = st_scr[...]
        o, st = _gla_tile(q_ref[...], k_ref[...], v_ref[...], la_ref[...], st_scr[...], reverse)
        o_ref[...] = o
        st_scr[...] = st

    return pl.pallas_call(
        body, name=name, grid=(B_HEADS, nt), in_specs=_gla_in_specs(tile, order, B_HEADS if reverse else 0),
        out_specs=[pl.BlockSpec((tile, B_VAL_DIM), lambda h, n: (order(n), h)),
                   pl.BlockSpec((None, None, B_VAL_DIM, B_KEY_DIM), lambda h, n: (h, order(n), 0, 0))],
        out_shape=[jax.ShapeDtypeStruct((s, D_MODEL), F32), jax.ShapeDtypeStruct((B_HEADS, nt, B_VAL_DIM, B_KEY_DIM), F32)],
        scratch_shapes=[pltpu.VMEM((B_VAL_DIM, B_KEY_DIM), F32)],
        compiler_params=_params(("parallel", "arbitrary")),
    )(proj, proj, proj, la)


def gla_bwd(name, proj, la, states, d_o, reverse, prev=None):
    s = proj.shape[0]
    tile = _row_tile(s)
    nt = s // tile
    order = (lambda n: n) if reverse else (lambda n: nt - 1 - n)

    def body(*refs):
        q_ref, k_ref, v_ref, la_ref, st_ref, do_ref = refs[:6]
        rest = refs[6:]
        prev_refs = rest[:3] if prev is not None else None
        dq_ref, dk_ref, dv_ref, dla_ref, dst_scr = rest[3:] if prev is not None else rest

        @pl.when(pl.program_id(1) == 0)
        def _():
            dst_scr[...] = jnp.zeros_like(dst_scr)

        _, vjp = jax.vjp(functools.partial(_gla_tile, reverse=reverse), q_ref[...], k_ref[...], v_ref[...], la_ref[...], st_ref[...])
        dq, dk, dv, dla, dst = vjp((do_ref[...], dst_scr[...]))
        if prev_refs is not None:
            dq, dk, dv = dq + prev_refs[0][...], dk + prev_refs[1][...], dv + prev_refs[2][...]
        dq_ref[...], dk_ref[...], dv_ref[...], dla_ref[...] = dq, dk, dv, dla
        dst_scr[...] = dst

    key_spec = pl.BlockSpec((tile, B_KEY_DIM), lambda h, n: (order(n), h))
    val_spec = pl.BlockSpec((tile, B_VAL_DIM), lambda h, n: (order(n), h))
    in_specs = _gla_in_specs(tile, order, B_HEADS if reverse else 0) + [
        pl.BlockSpec((None, None, B_VAL_DIM, B_KEY_DIM), lambda h, n: (h, order(n), 0, 0)), val_spec]
    ins = [proj, proj, proj, la, states, d_o]
    if prev is not None:
        in_specs += [key_spec, key_spec, val_spec]
        ins += list(prev)
    return pl.pallas_call(
        body, name=name, grid=(B_HEADS, nt), in_specs=in_specs,
        out_specs=[key_spec, key_spec, val_spec, key_spec],
        out_shape=[jax.ShapeDtypeStruct((s, 512), F32), jax.ShapeDtypeStruct((s, 512), F32),
                   jax.ShapeDtypeStruct((s, D_MODEL), F32), jax.ShapeDtypeStruct((s, 512), F32)],
        scratch_shapes=[pltpu.VMEM((B_VAL_DIM, B_KEY_DIM), F32)],
        compiler_params=_params(("parallel", "arbitrary")),
    )(*ins)


def _r_spec(tile):
    return pl.BlockSpec((tile, D_MODEL), lambda i: (i, 2))


def gla_post_fwd(name, o_f, o_b, proj, gain):
    s = o_f.shape[0]
    tile = _row_tile(s)

    def body(of_ref, ob_ref, r_ref, g_ref, out_ref):
        out_ref[...] = _gla_post(of_ref[...], ob_ref[...], r_ref[...], g_ref[...]).astype(out_ref.dtype)

    row = pl.BlockSpec((tile, D_MODEL), lambda i: (i, 0))
    return pl.pallas_call(body, name=name, grid=(s // tile,),
                          in_specs=[row, row, _r_spec(tile), pl.BlockSpec(gain.shape, lambda i: (0, 0))], out_specs=row,
                          out_shape=jax.ShapeDtypeStruct((s, D_MODEL), BF16), compiler_params=_params(("parallel",)))(o_f, o_b, proj, gain)


def gla_post_bwd(name, o_f, o_b, proj, gain, d_out):
    s = o_f.shape[0]
    tile = _row_tile(s)

    def body(of_ref, ob_ref, r_ref, g_ref, d_ref, do_ref, dr_ref, dg_ref):
        _, vjp = jax.vjp(_gla_post, of_ref[...], ob_ref[...], r_ref[...], g_ref[...])
        d_of, _, dr, dg = vjp(d_ref[...])
        do_ref[...] = d_of
        dr_ref[...] = dr
        _acc_rows(dg_ref, dg)

    row = pl.BlockSpec((tile, D_MODEL), lambda i: (i, 0))
    return pl.pallas_call(
        body, name=name, grid=(s // tile,),
        in_specs=[row, row, _r_spec(tile), pl.BlockSpec(gain.shape, lambda i: (0, 0)), row],
        out_specs=[row, row, pl.BlockSpec((8, D_MODEL), lambda i: (0, 0))],
        out_shape=[jax.ShapeDtypeStruct((s, D_MODEL), F32), jax.ShapeDtypeStruct((s, D_MODEL), F32), jax.ShapeDtypeStruct((8, D_MODEL), F32)],
        compiler_params=_params(("arbitrary",)))(o_f, o_b, proj, gain, d_out)


def _hid(tile, where):
    return pl.BlockSpec((None, tile, FFN_BLK), where)


def _pair(tile, where):
    return pl.BlockSpec((2, None, tile, FFN_BLK), where)


def _w_gu_spec(layer, where_j):
    return pl.BlockSpec((2, None, None, D_MODEL, FFN_BLK), lambda *g: (0, where_j(*g), layer, 0, 0))


def ffn_fwd(name, h_mid, hn2, w_gu, w_down, layer):
    s = hn2.shape[0]
    tm = _row_tile(s, 1024)
    nt = s // tm

    def gu_body(x_ref, w_ref, gu_ref, act_ref):
        x = x_ref[...]
        g = _dot(x, w_ref[0], NN)
        u = _dot(x, w_ref[1], NN)
        gu_ref[0] = g
        gu_ref[1] = u
        act_ref[...] = _swiglu_act(g, u).astype(act_ref.dtype)

    gu, act = pl.pallas_call(
        gu_body, name=name + "_gu", grid=(4, nt),
        in_specs=[pl.BlockSpec((tm, D_MODEL), lambda j, i: (i, 0)), _w_gu_spec(layer, lambda j, i: j)],
        out_specs=[_pair(tm, lambda j, i: (0, j, i, 0)), _hid(tm, lambda j, i: (j, i, 0))],
        out_shape=[jax.ShapeDtypeStruct((2, 4, s, FFN_BLK), F32), jax.ShapeDtypeStruct((4, s, FFN_BLK), BF16)],
        compiler_params=_params(("parallel", "parallel")))(hn2, w_gu)
    row = pl.BlockSpec((tm, D_MODEL), lambda i, j: (i, 0))
    h_next = _matmul(name + "_down", act, w_down, dims=NN, grid=(nt, 4), red_axis=1,
                     a_spec=_hid(tm, lambda i, j: (j, i, 0)), b_spec=pl.BlockSpec((None, FFN_BLK, D_MODEL), lambda i, j: (j, 0, 0)),
                     o_spec=row, out_shape=jax.ShapeDtypeStruct((s, D_MODEL), F32), res=h_mid, res_spec=row)
    return h_next, gu, act


def ffn_bwd(name, dh_next, hn2, gu, act, w_gu, w_down, layer):
    s = hn2.shape[0]
    tm = _row_tile(s, 1024)
    nt = s // tm
    tw = _row_tile(s, 2048)
    ntw = s // tw
    d_wd = _matmul(name + "_dwd", act, dh_next, dims=TN, grid=(4, ntw), red_axis=1,
                   a_spec=_hid(tw, lambda j, i: (j, i, 0)), b_spec=pl.BlockSpec((tw, D_MODEL), lambda j, i: (i, 0)),
                   o_spec=pl.BlockSpec((None, FFN_BLK, D_MODEL), lambda j, i: (j, 0, 0)),
                   out_shape=jax.ShapeDtypeStruct((4, FFN_BLK, D_MODEL), BF16), acc_shape=(FFN_BLK, D_MODEL))

    def dgu_body(dy_ref, wd_ref, gu_ref, dgu_ref):
        d_act = _dot(dy_ref[...], wd_ref[...], NT)
        _, vjp = jax.vjp(_swiglu_act, gu_ref[0], gu_ref[1])
        dg, du = vjp(d_act)
        dgu_ref[0] = dg.astype(dgu_ref.dtype)
        dgu_ref[1] = du.astype(dgu_ref.dtype)

    d_gu = pl.pallas_call(
        dgu_body, name=name + "_dgu", grid=(4, nt),
        in_specs=[pl.BlockSpec((tm, D_MODEL), lambda j, i: (i, 0)), pl.BlockSpec((None, FFN_BLK, D_MODEL), lambda j, i: (j, 0, 0)),
                  _pair(tm, lambda j, i: (0, j, i, 0))],
        out_specs=_pair(tm, lambda j, i: (0, j, i, 0)), out_shape=jax.ShapeDtypeStruct((2, 4, s, FFN_BLK), BF16),
        compiler_params=_params(("parallel", "parallel")))(dh_next, w_down, gu)

    def dx_body(d_ref, w_ref, o_ref):
        prod = _dot(d_ref[0], w_ref[0], NT) + _dot(d_ref[1], w_ref[1], NT)

        @pl.when(pl.program_id(1) == 0)
        def _():
            o_ref[...] = prod

        @pl.when(pl.program_id(1) > 0)
        def _():
            o_ref[...] += prod

    d_hn2 = pl.pallas_call(
        dx_body, name=name + "_dx", grid=(nt, 4),
        in_specs=[_pair(tm, lambda i, j: (0, j, i, 0)), _w_gu_spec(layer, lambda i, j: j)],
        out_specs=pl.BlockSpec((tm, D_MODEL), lambda i, j: (i, 0)), out_shape=jax.ShapeDtypeStruct((s, D_MODEL), F32),
        compiler_params=_params(("parallel", "arbitrary")))(d_gu, w_gu)

    def dw_body(x_ref, d_ref, o_ref, acc_ref):
        x = x_ref[...]
        k = pl.program_id(1)
        for t in range(2):
            prod = _dot(x, d_ref[t], TN)

            @pl.when(k == 0)
            def _():
                acc_ref[t] = prod

            @pl.when(k > 0)
            def _():
                acc_ref[t] += prod

        @pl.when(k == ntw - 1)
        def _():
            o_ref[...] = acc_ref[...].astype(o_ref.dtype)

    d_wgu = pl.pallas_call(
        dw_body, name=name + "_dwgu", grid=(4, ntw),
        in_specs=[pl.BlockSpec((tw, D_MODEL), lambda j, i: (i, 0)), _pair(tw, lambda j, i: (0, j, i, 0))],
        out_specs=pl.BlockSpec((2, None, D_MODEL, FFN_BLK), lambda j, i: (0, j, 0, 0)),
        out_shape=jax.ShapeDtypeStruct((2, 4, D_MODEL, FFN_BLK), BF16),
        scratch_shapes=[pltpu.VMEM((2, D_MODEL, FFN_BLK), F32)],
        compiler_params=_params(("parallel", "arbitrary")))(hn2, d_gu)
    return d_hn2, d_wgu, d_wd


def _my_place():
    return lax.axis_index("x"), lax.axis_index("y"), lax.axis_index("c")


def _flip(place, k):
    x, y, c = place
    return (1 - x if k & 4 else x, 1 - y if k & 2 else y, 1 - c if k & 1 else c)


def _index(place):
    return 4 * place[0] + 2 * place[1] + place[2]


def all_gather(arrs):
    n = len(arrs)

    def body(*refs):
        ins, outs = refs[:n], refs[n:2 * n]
        send_sems, recv_sems, local_sems = refs[2 * n:]
        me = _my_place()
        sibling = _flip(me, 1)
        chips = (4, 2, 6)

        def copy(a, k, block, to, src=None):
            dst = outs[a].at[_index(block)]
            return pltpu.make_async_remote_copy(src_ref=dst if src is None else src, dst_ref=dst, send_sem=send_sems.at[a, k],
                                                recv_sem=recv_sems.at[a, k], device_id=to, device_id_type=MESH)

        started = []
        for a in range(n):
            mine = pltpu.make_async_copy(ins[a], outs[a].at[_index(me)], local_sems.at[a])
            mine.start()
            started.append(mine)
        first = []
        for a in range(n):
            first.append(copy(a, 0, me, sibling, src=ins[a]))
            first += [copy(a, 1 + j, me, _flip(me, k), src=ins[a]) for j, k in enumerate(chips)]
        for cp in first:
            cp.start()
        passed = []
        for a in range(n):
            for j, k in enumerate(chips):
                copy(a, 1 + j, _flip(me, k), me).wait_recv()
                fwd = copy(a, 4 + j, _flip(me, k), sibling)
                fwd.start()
                passed.append(fwd)
        for a in range(n):
            copy(a, 0, sibling, me).wait_recv()
            for j, k in enumerate(chips):
                copy(a, 4 + j, _flip(sibling, k), me).wait_recv()
        for cp in first + passed:
            cp.wait_send()
        for cp in started:
            cp.wait()

    return pl.pallas_call(
        body, name="all_gather_weights", in_specs=[ANY] * n, out_specs=[ANY] * n,
        out_shape=[jax.ShapeDtypeStruct((N_DEV,) + a.shape, a.dtype) for a in arrs],
        scratch_shapes=[pltpu.SemaphoreType.DMA((n, 7)), pltpu.SemaphoreType.DMA((n, 7)), pltpu.SemaphoreType.DMA((n,))],
    )(*arrs)


def exchange_partials(arrs):
    n = len(arrs)

    def body(*refs):
        ins, outs = refs[:n], refs[n:2 * n]
        send_sems, recv_sems, local_sems = refs[2 * n:]
        me = _my_place()
        local = []
        for a in range(n):
            cp = pltpu.make_async_copy(ins[a].at[_index(me)], outs[a].at[_index(me)], local_sems.at[a])
            cp.start()
            local.append(cp)

        def copy(a, k, src_block, dst_block):
            return pltpu.make_async_remote_copy(src_ref=ins[a].at[_index(src_block)], dst_ref=outs[a].at[_index(dst_block)],
                                                send_sem=send_sems.at[a, k - 1], recv_sem=recv_sems.at[a, k - 1],
                                                device_id=_flip(me, k), device_id_type=MESH)

        sent = []
        for a in range(n):
            for k in range(1, N_DEV):
                cp = copy(a, k, _flip(me, k), me)
                cp.start()
                sent.append(cp)
        for a in range(n):
            for k in range(1, N_DEV):
                copy(a, k, me, _flip(me, k)).wait_recv()
        for cp in sent:
            cp.wait_send()
        for cp in local:
            cp.wait()

    return pl.pallas_call(
        body, name="exchange_weight_grads", in_specs=[ANY] * n, out_specs=[ANY] * n,
        out_shape=[jax.ShapeDtypeStruct(a.shape, a.dtype) for a in arrs],
        scratch_shapes=[pltpu.SemaphoreType.DMA((n, 7)), pltpu.SemaphoreType.DMA((n, 7)), pltpu.SemaphoreType.DMA((n,))],
    )(*arrs)


def adamw_shard(name, parts, w, m, v, layer, tile):
    rows, cols = parts.shape[1:]
    assert rows % tile == 0
    off = layer * (rows // tile)

    def body(p_ref, w_ref, m_ref, v_ref, g_ref, d_ref, nm_ref, nv_ref):
        g = p_ref[0].astype(F32)
        for src in range(1, N_DEV):
            g = g + p_ref[src].astype(F32)
        g_ref[...] = g
        d_ref[...], nm_ref[...], nv_ref[...] = _adamw(w_ref[...], g, m_ref[...], v_ref[...])

    src_row = pl.BlockSpec((tile, cols), lambda i: (off + i, 0))
    row = pl.BlockSpec((tile, cols), lambda i: (i, 0))
    shape = jax.ShapeDtypeStruct((rows, cols), F32)
    return pl.pallas_call(body, name=name, grid=(rows // tile,),
                          in_specs=[pl.BlockSpec((N_DEV, tile, cols), lambda i: (0, i, 0)), src_row, src_row, src_row],
                          out_specs=[row] * 4, out_shape=[shape] * 4, compiler_params=_params(("parallel",)))(parts, w, m, v)


def allreduce_adamw_replicated(partial, w, m, v, n_loss_rows):
    rows = partial.shape[0]

    def body(p_ref, w_ref, m_ref, v_ref, g_ref, d_ref, nm_ref, nv_ref, loss_ref, recv_ref, send_sems, recv_sems):
        me = _my_place()
        recv_ref[_index(me)] = p_ref[...]
        copies = []
        for k in range(1, N_DEV):
            peer = _flip(me, k)
            cp = pltpu.make_async_remote_copy(src_ref=p_ref, dst_ref=recv_ref.at[_index(me)], send_sem=send_sems.at[k - 1],
                                              recv_sem=recv_sems.at[k - 1], device_id=peer, device_id_type=MESH)
            cp.start()
            copies.append((cp, peer))
        for k, (cp, peer) in enumerate(copies):
            pltpu.make_async_remote_copy(src_ref=p_ref, dst_ref=recv_ref.at[_index(peer)], send_sem=send_sems.at[k],
                                         recv_sem=recv_sems.at[k], device_id=peer, device_id_type=MESH).wait_recv()
        for cp, _ in copies:
            cp.wait_send()
        g = recv_ref[0]
        for src in range(1, N_DEV):
            g = g + recv_ref[src]
        g_ref[...] = g
        d_ref[...], nm_ref[...], nv_ref[...] = _adamw(w_ref[...], g, m_ref[...], v_ref[...])
        loss = (0.5 / D_MODEL) * jnp.sum(g[rows - n_loss_rows:, :])
        loss_ref[...] = jnp.full(loss_ref.shape, loss, F32)

    shape = jax.ShapeDtypeStruct((rows, LANES), F32)
    return pl.pallas_call(
        body, name="allreduce_adamw_replicated", in_specs=[VMEM_SPEC] * 4, out_specs=[VMEM_SPEC] * 5,
        out_shape=[shape] * 4 + [jax.ShapeDtypeStruct((8, LANES), F32)],
        scratch_shapes=[pltpu.VMEM((N_DEV, rows, LANES), F32), pltpu.SemaphoreType.DMA((7,)), pltpu.SemaphoreType.DMA((7,))],
    )(partial, w, m, v)


def _pack_rows(flat, cols):
    n = flat.shape[-1]
    rows = -(-n // cols)
    rows = -(-rows // 48) * 48
    flat = jnp.pad(flat, [(0, 0)] * (flat.ndim - 1) + [(0, rows * cols - n)])
    return flat.reshape(flat.shape[:-1] + (rows, cols))


def _unpack(flat, shapes):
    out, off = [], 0
    for shp in shapes:
        n = 1
        for d in shp:
            n *= d
        out.append(flat[off:off + n].reshape(shp))
        off += n
    return out


def _to_dev_cols(a):
    w = a.shape[-1] // N_DEV
    return jnp.moveaxis(a.reshape(a.shape[:-1] + (N_DEV, w)), -2, 0)


def _from_dev_cols(a):
    a = jnp.moveaxis(a, 0, -2)
    return a.reshape(a.shape[:-2] + (a.shape[-2] * a.shape[-1],))


def kernel(x, attn_norm, ffn_norm, a_w_in, a_q_norm, a_k_norm, a_w_out, b_w_in, b_w_gate_f, b_gate_bias_f, b_w_gate_b, b_gate_bias_b, b_out_norm, b_w_out, ffn_w_gate_up, ffn_w_down, loss_target, m_attn_norm, m_ffn_norm, m_a_w_in, m_a_q_norm, m_a_k_norm, m_a_w_out, m_b_w_in, m_b_w_gate_f, m_b_gate_bias_f, m_b_w_gate_b, m_b_gate_bias_b, m_b_out_norm, m_b_w_out, m_ffn_w_gate_up, m_ffn_w_down, v_attn_norm, v_ffn_norm, v_a_w_in, v_a_q_norm, v_a_k_norm, v_a_w_out, v_b_w_in, v_b_w_gate_f, v_b_gate_bias_f, v_b_w_gate_b, v_b_gate_bias_b, v_b_out_norm, v_b_w_out, v_ffn_w_gate_up, v_ffn_w_down):
    seq = x.shape[1]
    depth = attn_norm.shape[0]
    h = x.reshape(seq, D_MODEL)
    target = loss_target.reshape(seq, D_MODEL)
    n_a, n_b = a_w_in.shape[0], b_w_in.shape[0]

    small = jnp.concatenate([t.reshape(-1) for t in (b_w_gate_f, b_gate_bias_f, b_w_gate_b, b_gate_bias_b, b_out_norm)])
    small = _pack_rows(small, LANES)
    g_a_in, g_a_out, g_b_in, g_b_out, g_gu, g_down, g_small = all_gather(
        [a_w_in.astype(BF16), a_w_out.astype(BF16), b_w_in.astype(BF16), b_w_out.astype(BF16),
         ffn_w_gate_up.astype(BF16), ffn_w_down.astype(BF16), small])
    w_a_in = _from_dev_cols(g_a_in)
    w_a_out = jnp.moveaxis(g_a_out, 0, 1).reshape(n_a, D_MODEL, D_MODEL)
    w_b_in = _from_dev_cols(g_b_in)
    w_b_out = jnp.moveaxis(g_b_out, 0, 1).reshape(n_b, D_MODEL, D_MODEL)
    w_down = jnp.moveaxis(g_down, 0, 1).reshape(depth, 4, FFN_BLK, D_MODEL)
    w_gu = g_gu.reshape(2, 4, depth, D_MODEL, FFN_BLK)
    small_shapes = [t.shape for t in (b_w_gate_f, b_gate_bias_f, b_w_gate_b, b_gate_bias_b, b_out_norm)]
    per_dev = [_unpack(g_small[d].reshape(-1), small_shapes) for d in range(N_DEV)]
    wgf, bgf, wgb, bgb, onorm = [_from_dev_cols(jnp.stack([per_dev[d][t] for d in range(N_DEV)])) for t in range(5)]
    w_gate = jnp.zeros((n_b, LANES, D_MODEL), F32)
    w_gate = w_gate.at[:, 0:16, 0:512].set(wgf).at[:, 16:32, 512:1024].set(wgb)
    gate_bias = jnp.concatenate([bgf, bgb], axis=1).reshape(n_b, 1, D_MODEL)
    out_gain = onorm.reshape(n_b, 1, D_MODEL)
    w_b_main = w_b_in[:, :, :3072]
    w_b_z = jnp.pad(w_b_in[:, :, 3072:], ((0, 0), (0, 0), (0, LANES - 32)))

    dh, sq_err, grads = _forward_backward(h, target, attn_norm, ffn_norm, a_q_norm, a_k_norm, w_a_in, w_a_out, w_b_main, w_b_z,
                                          w_gate, gate_bias, out_gain, w_b_out, w_gu, w_down)
    sharded_w = (a_w_in, a_w_out, b_w_in, b_w_gate_f, b_gate_bias_f, b_w_gate_b, b_gate_bias_b, b_out_norm, b_w_out, ffn_w_gate_up, ffn_w_down)
    sharded_m = (m_a_w_in, m_a_w_out, m_b_w_in, m_b_w_gate_f, m_b_gate_bias_f, m_b_w_gate_b, m_b_gate_bias_b, m_b_out_norm, m_b_w_out, m_ffn_w_gate_up, m_ffn_w_down)
    sharded_v = (v_a_w_in, v_a_w_out, v_b_w_in, v_b_w_gate_f, v_b_gate_bias_f, v_b_w_gate_b, v_b_gate_bias_b, v_b_out_norm, v_b_w_out, v_ffn_w_gate_up, v_ffn_w_down)
    rep_w = (attn_norm, ffn_norm, a_q_norm, a_k_norm)
    rep_m = (m_attn_norm, m_ffn_norm, m_a_q_norm, m_a_k_norm)
    rep_v = (v_attn_norm, v_ffn_norm, v_a_q_norm, v_a_k_norm)
    loss, outs = _reduce_and_update(grads, sq_err, sharded_w, sharded_m, sharded_v, rep_w, rep_m, rep_v)
    return (loss, dh.reshape(x.shape), *outs)


def _forward_backward(h, target, attn_norm, ffn_norm, a_q_norm, a_k_norm, w_a_in, w_a_out, w_b_main, w_b_z, w_gate, gate_bias,
                      out_gain, w_b_out, w_gu, w_down):
    seq = h.shape[0]
    depth = attn_norm.shape[0]
    n_a, n_b = w_a_in.shape[0], w_b_main.shape[0]
    tab = _rope_table(seq)
    pair_gain = lambda g: jnp.concatenate([g, g]).reshape(1, LANES)

    saved = []
    for i in range(depth):
        j = i // 2
        nm = f"l{i}"
        hn = rmsnorm_fwd(nm + "_norm1", h, attn_norm[i].reshape(1, D_MODEL))
        if i % 2 == 0:
            qkv = mm_nn(nm + "_qkv", hn, w_a_in[j])
            os_, lses = [], []
            for g, (_, dil) in enumerate(A_GROUPS):
                o, lse = attn_fwd(f"{nm}_attn{g}", qkv, tab, pair_gain(a_q_norm[j, g]), pair_gain(a_k_norm[j, g]), g, dil)
                os_.append(o)
                lses.append(lse)
            mixed = attn_merge_fwd(nm + "_merge", os_, lses)
            h_mid = mm_nn(nm + "_out", mixed, w_a_out[j], res=h)
            mix_saved = (qkv, os_, lses, mixed)
        else:
            proj = mm_nn(nm + "_proj", hn, w_b_main[j])
            z = mm_nn(nm + "_z", hn, w_b_z[j])
            la = gla_gate_fwd(nm + "_gate", z, w_gate[j], gate_bias[j])
            o_f, st_f = gla_fwd(nm + "_gla_f", proj, la, False)
            o_b, st_b = gla_fwd(nm + "_gla_b", proj, la, True)
            mixed = gla_post_fwd(nm + "_post", o_f, o_b, proj, out_gain[j])
            h_mid = mm_nn(nm + "_out", mixed, w_b_out[j], res=h)
            mix_saved = (proj, z, la, o_f, st_f, o_b, st_b, mixed)
        hn2 = rmsnorm_fwd(nm + "_norm2", h_mid, ffn_norm[i].reshape(1, D_MODEL))
        h_next, gu, act = ffn_fwd(nm + "_ffn", h_mid, hn2, w_gu, w_down[i], i)
        saved.append((h, hn, mix_saved, h_mid, hn2, gu, act))
        h = h_next

    dh, sq_err = loss_and_grad(h, target)

    g_attn_norm, g_ffn_norm = [None] * depth, [None] * depth
    g_a_w_in, g_a_w_out, g_a_q, g_a_k = [None] * n_a, [None] * n_a, [None] * n_a, [None] * n_a
    g_b_w_in, g_b_w_out, g_w_gate, g_gate_bias, g_out_gain = ([None] * n_b for _ in range(5))
    g_w_gu, g_w_down = [None] * depth, [None] * depth
    for i in reversed(range(depth)):
        j = i // 2
        nm = f"l{i}b"
        h_in, hn, mix_saved, h_mid, hn2, gu, act = saved[i]
        d_hn2, g_w_gu[i], g_w_down[i] = ffn_bwd(nm + "_ffn", dh, hn2, gu, act, w_gu, w_down[i], i)
        dh_mid, g_ffn_norm[i] = rmsnorm_bwd(nm + "_norm2", h_mid, ffn_norm[i].reshape(1, D_MODEL), d_hn2, dh)
        if i % 2 == 0:
            qkv, os_, lses, mixed = mix_saved
            d_mixed = mm_nt(nm + "_dmixed", dh_mid, w_a_out[j])
            g_a_w_out[j] = mm_tn(nm + "_dwout", mixed, dh_mid, by_block=True)
            d_parts = attn_merge_bwd(nm + "_merge", os_, lses, d_mixed)
            gq_l, gk_l = [], []
            d_qkv = lax.empty(qkv.shape, BF16)
            for g, (_, dil) in enumerate(A_GROUPS):
                d_qkv, dk_parts, dv_parts, dgq, dgk = attn_bwd(f"{nm}_attn{g}", qkv, tab, pair_gain(a_q_norm[j, g]),
                                                               pair_gain(a_k_norm[j, g]), os_[g], lses[g], d_parts[g], d_parts[3 + g], g, dil, d_qkv)
                d_qkv = attn_combine(f"{nm}_dk{g}", dk_parts, seq, dil, d_qkv, 3 * g + 1)
                d_qkv = attn_combine(f"{nm}_dv{g}", dv_parts, seq, dil, d_qkv, 3 * g + 2)
                gq_l.append(dgq[0, :A_HEAD_DIM])
                gk_l.append(dgk[0, :A_HEAD_DIM])
            g_a_q[j], g_a_k[j] = jnp.stack(gq_l), jnp.stack(gk_l)
            d_hn = mm_nt(nm + "_dhn", d_qkv, w_a_in[j])
            g_a_w_in[j] = mm_tn(nm + "_dwin", hn, d_qkv, tn=D_MODEL * 9 // N_DEV, by_block=True)
        else:
            proj, z, la, o_f, st_f, o_b, st_b, mixed = mix_saved
            d_mixed = mm_nt(nm + "_dmixed", dh_mid, w_b_out[j])
            g_b_w_out[j] = mm_tn(nm + "_dwout", mixed, dh_mid, by_block=True)
            d_o, d_r, dgain = gla_post_bwd(nm + "_post", o_f, o_b, proj, out_gain[j], d_mixed)
            g_out_gain[j] = dgain[0]
            dq, dk, dv, dla_f = gla_bwd(nm + "_gla_f", proj, la, st_f, d_o, False)
            dq, dk, dv, dla_b = gla_bwd(nm + "_gla_b", proj, la, st_b, d_o, True, prev=(dq, dk, dv))
            d_z, g_w_gate[j], dbias = gla_gate_bwd(nm + "_gate", z, w_gate[j], gate_bias[j], dla_f, dla_b)
            g_gate_bias[j] = dbias[0]
            d_proj = jnp.concatenate([dq, dk, dv, d_r], axis=1)
            d_hn = mm_nt(nm + "_dhn_z", d_z, w_b_z[j])
            d_hn = mm_nt(nm + "_dhn", d_proj, w_b_main[j], res=d_hn)
            g_b_w_in[j] = jnp.concatenate([mm_tn(nm + "_dwin", hn, d_proj), mm_tn(nm + "_dwz", hn, d_z)[:, :32]], axis=1)
        dh, g_attn_norm[i] = rmsnorm_bwd(nm + "_norm1", h_in, attn_norm[i].reshape(1, D_MODEL), d_hn, dh_mid)
    return dh, sq_err, (g_attn_norm, g_ffn_norm, g_a_w_in, g_a_w_out, g_a_q, g_a_k, g_b_w_in, g_b_w_out, g_w_gate, g_gate_bias,
                        g_out_gain, g_w_gu, g_w_down)


def _reduce_and_update(grads, sq_err, sharded_w, sharded_m, sharded_v, rep_w, rep_m, rep_v):
    (g_attn_norm, g_ffn_norm, g_a_w_in, g_a_w_out, g_a_q, g_a_k, g_b_w_in, g_b_w_out, g_w_gate, g_gate_bias, g_out_gain,
     g_w_gu, g_w_down) = grads
    depth, n_a, n_b = len(g_w_gu), len(g_a_w_in), len(g_b_w_in)

    g_w_gate = jnp.stack(g_w_gate)
    g_gate_bias = jnp.stack(g_gate_bias)
    small_parts = [_to_dev_cols(g_w_gate[:, 0:16, 0:512]), _to_dev_cols(g_gate_bias[:, 0:512]),
                   _to_dev_cols(g_w_gate[:, 16:32, 512:1024]), _to_dev_cols(g_gate_bias[:, 512:1024]),
                   _to_dev_cols(jnp.stack(g_out_gain).reshape(n_b, B_HEADS, B_VAL_DIM))]
    small_part = _pack_rows(jnp.concatenate([t.reshape(N_DEV, -1) for t in small_parts], axis=1), LANES).astype(BF16)
    families = [
        (0, g_a_w_in, 256),
        (1, [t.reshape(N_DEV, -1, D_MODEL) for t in g_a_w_out], 128),
        (2, [_to_dev_cols(t).astype(BF16) for t in g_b_w_in], 256),
        (8, [t.reshape(N_DEV, -1, D_MODEL) for t in g_b_w_out], 128),
        (9, [t.reshape(N_DEV, D_MODEL, FFN_BLK) for t in g_w_gu], 256),
        (10, [t.reshape(N_DEV, -1, D_MODEL) for t in g_w_down], 176),
    ]
    flat_parts = [p for _, parts, _ in families for p in parts] + [small_part]
    received = exchange_partials(flat_parts)
    sh_out = [None] * len(sharded_w)
    pos = 0
    for fam, parts, tile in families:
        w = sharded_w[fam]
        two_d = lambda t: t.reshape(-1, t.shape[-1])
        per_layer = []
        for layer in range(len(parts)):
            per_layer.append(adamw_shard(f"adamw_p{fam}_l{layer}", received[pos], two_d(w), two_d(sharded_m[fam]),
                                         two_d(sharded_v[fam]), layer, tile))
            pos += 1
        sh_out[fam] = [jnp.stack([per_layer[l][t] for l in range(len(parts))]).reshape(w.shape) for t in range(4)]
    small_ids = (3, 4, 5, 6, 7)
    pack_small = lambda ts: _pack_rows(jnp.concatenate([ts[i].reshape(-1) for i in small_ids]), LANES)
    small_out = adamw_shard("adamw_small", received[pos], pack_small(sharded_w), pack_small(sharded_m), pack_small(sharded_v), 0, 48)
    small_shapes = [sharded_w[i].shape for i in small_ids]
    for t in range(4):
        for i, val in zip(small_ids, _unpack(small_out[t].reshape(-1), small_shapes)):
            if sh_out[i] is None:
                sh_out[i] = [None] * 4
            sh_out[i][t] = val
    sh_grad, sh_delta, sh_m, sh_v = [[sh_out[i][t] for i in range(len(sharded_w))] for t in range(4)]

    rep_g = (jnp.stack([t[0] for t in g_attn_norm]), jnp.stack([t[0] for t in g_ffn_norm]), jnp.stack(g_a_q), jnp.stack(g_a_k))
    n_rep = sum(t.size for t in rep_w)
    n_rep_rows = -(-n_rep // (8 * LANES)) * 8
    n_loss_rows = 8 * D_MODEL // LANES

    def pack_rep(ts, tail):
        flat = jnp.concatenate([t.reshape(-1) for t in ts])
        flat = jnp.pad(flat, (0, n_rep_rows * LANES - n_rep))
        return jnp.concatenate([flat.reshape(n_rep_rows, LANES), tail], axis=0)

    zeros_tail = jnp.zeros((n_loss_rows, LANES), F32)
    rep_out = allreduce_adamw_replicated(pack_rep(rep_g, sq_err.reshape(n_loss_rows, LANES)), pack_rep(rep_w, zeros_tail),
                                         pack_rep(rep_m, zeros_tail), pack_rep(rep_v, zeros_tail + 1.0), n_loss_rows)
    rep_shapes = [t.shape for t in rep_w]
    r_grad, r_delta, r_m, r_v = [_unpack(p.reshape(-1), rep_shapes) for p in rep_out[:4]]
    loss = rep_out[4][0, 0]

    def ordered(rep, sh):
        return [rep[0], rep[1], sh[0], rep[2], rep[3]] + list(sh[1:])

    return loss, (*ordered(r_grad, sh_grad), *ordered(r_delta, sh_delta), *ordered(r_m, sh_m), *ordered(r_v, sh_v))
```

```python
import functools

import jax
import jax.numpy as jnp
from jax import lax
from jax.experimental import pallas as pl
from jax.experimental.pallas import tpu as pltpu

F32 = jnp.float32
BF16 = jnp.bfloat16
MXU_DTYPE = jnp.bfloat16

D_MODEL = 1024
N_DEV = 8
RMS_EPS = 1e-6
NEG_INF = -1e30
A_GROUPS = ((128, 1), (512, 4), (2048, 16))
A_HEAD_DIM = 64
A_HALF = 64
ATT_T = 128
ATT_TILE = 2048
ATT_FWD_BATCH = 8
ATT_BWD_BATCH = 4
B_HEADS = 4
B_KEY_DIM = 128
B_VAL_DIM = 256
B_CHUNK = 64
B_GATE_TAU = 16.0
FFN_HIDDEN = 2816
FFN_BLK = 2 * FFN_HIDDEN // N_DEV
ADAM_LR, ADAM_B1, ADAM_B2, ADAM_EPS, ADAM_WD, ADAM_STEP = 0.001, 0.9, 0.999, 1e-08, 0.01, 10
ROPE_THETA = 10000.0

V7X_VMEM_LIMIT = 56 * 1024 * 1024
LANES = 128
MESH = pl.DeviceIdType.MESH
ANY = pl.BlockSpec(memory_space=pl.ANY)
VMEM_SPEC = pl.BlockSpec(memory_space=pltpu.VMEM)

NN = ((1,), (0,))
NT = ((1,), (1,))
TN = ((0,), (0,))


def _dot(a, b, dims):
    return lax.dot_general(a.astype(MXU_DTYPE), b.astype(MXU_DTYPE), (dims, ((), ())), preferred_element_type=F32)


@jax.custom_vjp
def dot_nn(a, b):
    return _dot(a, b, NN)


@jax.custom_vjp
def dot_nt(a, b):
    return _dot(a, b, NT)


@jax.custom_vjp
def dot_tn(a, b):
    return _dot(a, b, TN)


dot_nn.defvjp(lambda a, b: (_dot(a, b, NN), (a, b)), lambda r, g: (dot_nt(g, r[1]), dot_tn(r[0], g)))
dot_nt.defvjp(lambda a, b: (_dot(a, b, NT), (a, b)), lambda r, g: (dot_nn(g, r[1]), dot_tn(g, r[0])))
dot_tn.defvjp(lambda a, b: (_dot(a, b, TN), (a, b)), lambda r, g: (dot_nt(r[1], g), dot_nn(r[0], g)))


def _bdot(a, b, dims):
    dn = (tuple((d[0] + 1,) for d in dims), ((0,), (0,)))
    return lax.dot_general(a.astype(MXU_DTYPE), b.astype(MXU_DTYPE), dn, preferred_element_type=F32)


@jax.custom_vjp
def bdot_nn(a, b):
    return _bdot(a, b, NN)


@jax.custom_vjp
def bdot_nt(a, b):
    return _bdot(a, b, NT)


@jax.custom_vjp
def bdot_tn(a, b):
    return _bdot(a, b, TN)


bdot_nn.defvjp(lambda a, b: (_bdot(a, b, NN), (a, b)), lambda r, g: (bdot_nt(g, r[1]), bdot_tn(r[0], g)))
bdot_nt.defvjp(lambda a, b: (_bdot(a, b, NT), (a, b)), lambda r, g: (bdot_nn(g, r[1]), bdot_tn(g, r[0])))
bdot_tn.defvjp(lambda a, b: (_bdot(a, b, TN), (a, b)), lambda r, g: (bdot_nt(r[1], g), bdot_nn(r[0], g)))


def _dot_f32(a, b):
    return lax.dot_general(a, b, (NN, ((), ())), precision=lax.Precision.HIGHEST, preferred_element_type=F32)


def _tri(n, upper):
    r = lax.broadcasted_iota(jnp.int32, (n, n), 0)
    c = lax.broadcasted_iota(jnp.int32, (n, n), 1)
    return jnp.where((c >= r) if upper else (c <= r), 1.0, 0.0).astype(F32)


def _chunk_cumsum(x, reverse):
    tri = jnp.broadcast_to(_tri(x.shape[1], reverse), (x.shape[0], x.shape[1], x.shape[1]))
    return lax.dot_general(tri, x, (((2,), (1,)), ((0,), (0,))), precision=lax.Precision.HIGHEST, preferred_element_type=F32)


@functools.partial(jax.custom_vjp, nondiff_argnums=(1,))
def cumsum_chunks(x, reverse):
    return _chunk_cumsum(x, reverse)


cumsum_chunks.defvjp(lambda x, reverse: (_chunk_cumsum(x, reverse), None), lambda reverse, _, g: (_chunk_cumsum(g, not reverse),))


def _swap32_raw(x):
    lane = lax.broadcasted_iota(jnp.int32, x.shape, 1)
    return jnp.where((lane % 64) < 32, pltpu.roll(x, 96, 1), pltpu.roll(x, 32, 1))


@jax.custom_vjp
def swap32(x):
    return _swap32_raw(x)


swap32.defvjp(lambda x: (_swap32_raw(x), None), lambda _, g: (_swap32_raw(g),))


def _rms(x, gain):
    return x * lax.rsqrt(jnp.mean(x * x, axis=-1, keepdims=True) + RMS_EPS) * gain


def _sigmoid(x):
    return 1.0 / (1.0 + jnp.exp(-x))


def _log_sigmoid(x):
    return jnp.minimum(x, 0.0) - jnp.log(1.0 + jnp.exp(-jnp.abs(x)))


def _qk_prep(x, tab, gain):
    lo = lax.broadcasted_iota(jnp.int32, (1, LANES), 1) < A_HEAD_DIM
    x2 = x * x
    s_lo = jnp.sum(jnp.where(lo, x2, 0.0), axis=-1, keepdims=True)
    s_hi = jnp.sum(jnp.where(lo, 0.0, x2), axis=-1, keepdims=True)
    xn = (x * lax.rsqrt(jnp.where(lo, s_lo, s_hi) / A_HEAD_DIM + RMS_EPS)) * gain
    return xn * tab[:, :LANES] + swap32(xn) * tab[:, LANES:]


def _stack_heads(x):
    lo = lax.broadcasted_iota(jnp.int32, (1, LANES), 1) < A_HEAD_DIM
    return jnp.concatenate([jnp.where(lo, x, 0.0), jnp.where(lo, 0.0, x)], axis=-2)


def _unstack_heads(x):
    t = x.shape[-2] // 2
    lo = lax.broadcasted_iota(jnp.int32, (1, LANES), 1) < A_HEAD_DIM
    return jnp.where(lo, x[..., :t, :], x[..., t:, :])


def _attn_scores(q, k, valid):
    s = _bdot(_stack_heads(q), k, NT) * (A_HEAD_DIM ** -0.5)
    return jnp.where(valid, s, NEG_INF)


def _attn_job(q, k, v, valid):
    s = _attn_scores(q, k, valid)
    mx = jnp.max(s, axis=-1, keepdims=True)
    p = jnp.exp(s - mx)
    l = jnp.sum(p, axis=-1, keepdims=True)
    out = _unstack_heads(_bdot(p, v, NN) / l)
    lse = mx + jnp.log(l)
    t = q.shape[1]
    lo = lax.broadcasted_iota(jnp.int32, (1, LANES), 1) < A_HEAD_DIM
    return out, jnp.where(lo, lse[:, :t], lse[:, t:])


def _attn_job_bwd(q, k, v, valid, d_out, out, lse, d_lse):
    qs = _stack_heads(q)
    lo = lax.broadcasted_iota(jnp.int32, (1, LANES), 1) < A_HEAD_DIM
    other = pltpu.roll(lse, A_HEAD_DIM, 2)
    row_lse = jnp.concatenate([jnp.where(lo, lse, other), jnp.where(lo, other, lse)], axis=1)
    p = jnp.exp(_attn_scores(q, k, valid) - jnp.concatenate([row_lse, row_lse], axis=2))
    dos = _stack_heads(d_out)
    dv = _bdot(p, dos, TN)
    dp = _bdot(dos, v, NT)
    inner = jnp.sum(dos * _stack_heads(out) - _stack_heads(d_lse), axis=-1, keepdims=True)
    ds = p * (dp - inner) * (A_HEAD_DIM ** -0.5)
    return _unstack_heads(_bdot(ds, k, NN)), _bdot(ds, qs, TN), dv


def _merge_groups(o0, o1, o2, l0, l1, l2):
    mx = lax.stop_gradient(jnp.maximum(jnp.maximum(l0, l1), l2))
    e0, e1, e2 = jnp.exp(l0 - mx), jnp.exp(l1 - mx), jnp.exp(l2 - mx)
    den = e0 + e1 + e2
    return (e0 / den) * o0 + (e1 / den) * o1 + (e2 / den) * o2


def _gla_tile(q, k, v, la, st, reverse):
    t = q.shape[0]
    nc = t // B_CHUNK
    split = lambda x: x.reshape(nc, B_CHUNK, x.shape[1])
    q, k, v, la = split(q * (B_KEY_DIM ** -0.5)), split(k), split(v), split(la)
    r = lax.broadcasted_iota(jnp.int32, (1, B_CHUNK, B_CHUNK), 1)
    c = lax.broadcasted_iota(jnp.int32, (1, B_CHUNK, B_CHUNK), 2)
    mask = (c > r) if reverse else (c <= r)
    b = cumsum_chunks(la, reverse)
    tot = jnp.sum(la, axis=1, keepdims=True)
    q_t = q * jnp.exp(b)
    k_t = k * jnp.exp(-b)
    k_end = k * jnp.exp(tot - b)
    attn = jnp.where(mask, bdot_nt(q_t, k_t), 0.0)
    kv = bdot_tn(v, k_end)
    decay = jnp.exp(tot)
    entering = [None] * nc
    for ci in (range(nc - 1, -1, -1) if reverse else range(nc)):
        entering[ci] = st
        st = st * decay[ci] + kv[ci]
    out = bdot_nn(attn, v) + bdot_nt(q_t, jnp.concatenate([e[None] for e in entering], axis=0))
    return out.reshape(t, out.shape[2]), st


def _gla_post(o_f, o_b, r, gain):
    o = o_f + o_b
    heads = [_rms(o[:, h * B_VAL_DIM:(h + 1) * B_VAL_DIM], gain[:, h * B_VAL_DIM:(h + 1) * B_VAL_DIM]) for h in range(B_HEADS)]
    return jnp.concatenate(heads, axis=1) * (r * _sigmoid(r))


def _gate(z, wg, bias):
    return _log_sigmoid(dot_nn(z, wg) + bias) / B_GATE_TAU


def _swiglu_act(g, u):
    return (g * _sigmoid(g)) * u


def _adamw(w, g, m, v):
    m = ADAM_B1 * m + (1.0 - ADAM_B1) * g
    v = ADAM_B2 * v + (1.0 - ADAM_B2) * jnp.square(g)
    m_hat = m / (1.0 - ADAM_B1 ** ADAM_STEP)
    v_hat = v / (1.0 - ADAM_B2 ** ADAM_STEP)
    delta = -ADAM_LR * (m_hat / (jnp.sqrt(v_hat) + ADAM_EPS) + ADAM_WD * w)
    return delta, m, v


def _params(sem=None):
    return pltpu.CompilerParams(dimension_semantics=sem, vmem_limit_bytes=V7X_VMEM_LIMIT)


def _row_tile(s, want=512):
    t = min(want, s)
    assert s % t == 0
    return t


def _matmul(name, a, b, *, dims, grid, a_spec, b_spec, o_spec, out_shape, red_axis=None, res=None, res_spec=None, acc_shape=None):
    n_red = grid[red_axis] if red_axis is not None else 1

    def body(*refs):
        a_ref, b_ref = refs[0], refs[1]
        r_ref = refs[2] if res is not None else None
        o_ref = refs[3] if res is not None else refs[2]
        prod = lax.dot_general(a_ref[...].astype(MXU_DTYPE), b_ref[...].astype(MXU_DTYPE), (dims, ((), ())),
                               preferred_element_type=F32)
        if red_axis is None:
            if r_ref is not None:
                prod = prod + r_ref[...]
            o_ref[...] = prod.astype(o_ref.dtype)
            return
        acc = refs[-1] if acc_shape is not None else o_ref
        k = pl.program_id(red_axis)

        @pl.when(k == 0)
        def _():
            acc[...] = prod + r_ref[...] if r_ref is not None else prod

        @pl.when(k > 0)
        def _():
            acc[...] += prod

        if acc_shape is not None:
            @pl.when(k == n_red - 1)
            def _():
                o_ref[...] = acc[...].astype(o_ref.dtype)

    ins = [a, b] + ([res] if res is not None else [])
    specs = [a_spec, b_spec] + ([res_spec] if res is not None else [])
    sem = tuple("arbitrary" if i == red_axis else "parallel" for i in range(len(grid)))
    return pl.pallas_call(body, name=name, grid=grid, in_specs=specs, out_specs=o_spec, out_shape=out_shape,
                          scratch_shapes=[pltpu.VMEM(acc_shape, F32)] if acc_shape is not None else [],
                          compiler_params=_params(sem))(*ins)


def mm_nn(name, x, w, *, res=None, out_dtype=F32, tn=1024):
    m, k = x.shape
    n = w.shape[1]
    tm, tn = _row_tile(m, 1024), min(tn, n)
    return _matmul(name, x, w, dims=NN, grid=(n // tn, m // tm),
                   a_spec=pl.BlockSpec((tm, k), lambda j, i: (i, 0)), b_spec=pl.BlockSpec((k, tn), lambda j, i: (0, j)),
                   o_spec=pl.BlockSpec((tm, tn), lambda j, i: (i, j)), out_shape=jax.ShapeDtypeStruct((m, n), out_dtype),
                   res=res, res_spec=pl.BlockSpec((tm, tn), lambda j, i: (i, j)))


def mm_nt(name, dy, w, *, res=None, tn=2304):
    m, n = dy.shape
    k = w.shape[0]
    tm, tn = _row_tile(m, 1024), (tn if n % tn == 0 else min(1024, n))
    return _matmul(name, dy, w, dims=NT, grid=(m // tm, n // tn), red_axis=1,
                   a_spec=pl.BlockSpec((tm, tn), lambda i, j: (i, j)), b_spec=pl.BlockSpec((k, tn), lambda i, j: (0, j)),
                   o_spec=pl.BlockSpec((tm, k), lambda i, j: (i, 0)), out_shape=jax.ShapeDtypeStruct((m, k), F32),
                   res=res, res_spec=pl.BlockSpec((tm, k), lambda i, j: (i, 0)))


def mm_tn(name, x, dy, *, tn=1024, by_block=False):
    m, k = x.shape
    n = dy.shape[1]
    tm, tn = _row_tile(m, 2048), min(tn, n)
    if by_block:
        o_spec, out_shape, acc = pl.BlockSpec((None, k, tn), lambda j, i: (j, 0, 0)), jax.ShapeDtypeStruct((n // tn, k, tn), BF16), (k, tn)
    else:
        o_spec, out_shape, acc = pl.BlockSpec((k, tn), lambda j, i: (0, j)), jax.ShapeDtypeStruct((k, n), F32), None
    return _matmul(name, x, dy, dims=TN, grid=(n // tn, m // tm), red_axis=1,
                   a_spec=pl.BlockSpec((tm, k), lambda j, i: (i, 0)), b_spec=pl.BlockSpec((tm, tn), lambda j, i: (i, j)),
                   o_spec=o_spec, out_shape=out_shape, acc_shape=acc)


def _rows_call(name, body, ins, outs, s, tile):
    in_specs = []
    for a, kind in ins:
        if kind == "row":
            in_specs.append(pl.BlockSpec((tile, a.shape[1]), lambda i: (i, 0)))
        else:
            in_specs.append(pl.BlockSpec(a.shape, lambda i, nd=a.ndim: (0,) * nd))
    out_specs, out_shape = [], []
    for cols, dt, kind in outs:
        if kind == "row":
            out_specs.append(pl.BlockSpec((tile, cols), lambda i: (i, 0)))
            out_shape.append(jax.ShapeDtypeStruct((s, cols), dt))
        else:
            out_specs.append(pl.BlockSpec((8, cols), lambda i: (0, 0)))
            out_shape.append(jax.ShapeDtypeStruct((8, cols), dt))
    has_acc = any(kind == "acc" for _, _, kind in outs)
    return pl.pallas_call(body, name=name, grid=(s // tile,), in_specs=in_specs, out_specs=out_specs, out_shape=out_shape,
                          compiler_params=_params(("arbitrary",) if has_acc else ("parallel",)))(*[a for a, _ in ins])


def _acc_rows(ref, val):
    @pl.when(pl.program_id(0) == 0)
    def _():
        ref[...] = jnp.zeros_like(ref)

    ref[...] += jnp.broadcast_to(val, ref.shape)


def rmsnorm_fwd(name, h, gain):
    s = h.shape[0]

    def body(h_ref, g_ref, o_ref):
        o_ref[...] = _rms(h_ref[...], g_ref[...]).astype(o_ref.dtype)

    return _rows_call(name, body, [(h, "row"), (gain, "full")], [(D_MODEL, BF16, "row")], s, _row_tile(s))[0]


def rmsnorm_bwd(name, h, gain, d_hn, d_res):
    s = h.shape[0]

    def body(h_ref, g_ref, dy_ref, dr_ref, dh_ref, dg_ref):
        _, vjp = jax.vjp(_rms, h_ref[...], g_ref[...])
        dh, dg = vjp(dy_ref[...])
        dh_ref[...] = dh + dr_ref[...]
        _acc_rows(dg_ref, dg)

    return _rows_call(name, body, [(h, "row"), (gain, "full"), (d_hn, "row"), (d_res, "row")],
                      [(D_MODEL, F32, "row"), (D_MODEL, F32, "acc")], s, _row_tile(s))


def loss_and_grad(y, target):
    s = y.shape[0]
    tile = _row_tile(s)

    def body(y_ref, t_ref, dy_ref, acc_ref):
        diff = y_ref[...] - t_ref[...]
        dy_ref[...] = diff * (1.0 / D_MODEL)

        @pl.when(pl.program_id(0) == 0)
        def _():
            acc_ref[...] = jnp.zeros_like(acc_ref)

        acc_ref[...] += jnp.sum((diff * diff).reshape(tile // 8, 8, D_MODEL), axis=0)

    return _rows_call("loss_head", body, [(y, "row"), (target, "row")], [(D_MODEL, F32, "row"), (D_MODEL, F32, "acc")], s, tile)


def _rope_table(s):
    half = A_HEAD_DIM // 2
    inv_freq = ROPE_THETA ** (-jnp.arange(half, dtype=F32) / half)
    ang = jnp.arange(s).astype(F32)[:, None] * inv_freq[None, :]
    cos, sin = jnp.cos(ang), jnp.sin(ang)
    return jnp.concatenate([cos, cos, cos, cos, -sin, sin, -sin, sin], axis=1)


def _attn_geometry(s, dil):
    tile = min(ATT_TILE, s)
    halo = A_HALF * dil
    assert s % tile == 0 and tile % (ATT_T * dil) == 0 and tile % halo == 0
    return tile, halo, tile // (ATT_T * dil)


def _attn_in_specs(grp, s, tile, halo):
    hb, n_hb = tile // halo, s // halo
    cq, ck, cv = (24 * grp + 8 * t for t in range(3))

    def main(col, per_pair, width=LANES):
        return pl.BlockSpec((tile, width), lambda i, j: (i, col + per_pair * j))

    def prev(col, per_pair, width=LANES):
        return pl.BlockSpec((halo, width), lambda i, j: (jnp.maximum(i * hb - 1, 0), col + per_pair * j))

    def nxt(col, per_pair, width=LANES):
        return pl.BlockSpec((halo, width), lambda i, j: (jnp.minimum((i + 1) * hb, n_hb - 1), col + per_pair * j))

    return [main(cq, 1), prev(ck, 1), main(ck, 1), nxt(ck, 1), prev(cv, 1), main(cv, 1), nxt(cv, 1),
            prev(0, 0, 256), main(0, 0, 256), nxt(0, 0, 256)]


def _attn_valid(first_job, n_jobs, dil, tile_base, length):
    r = lax.broadcasted_iota(jnp.int32, (1, 2 * ATT_T, 1), 1)
    tq = jnp.where(r >= ATT_T, r - ATT_T, r)
    rel = lax.broadcasted_iota(jnp.int32, (1, 1, ATT_T + 2 * A_HALF), 2) - A_HALF
    job = first_job + lax.broadcasted_iota(jnp.int32, (n_jobs, 1, 1), 0)
    tk = tile_base + (job // dil) * ATT_T + rel
    return (jnp.abs(rel - tq) <= A_HALF) & (tk >= 0) & (tk < length)


def _jobs(dil, n_sub):
    return [u * ATT_T * dil + p for u in range(n_sub) for p in range(dil)]


def _gather_jobs(ref, starts, size, dil):
    return jnp.concatenate([ref[_rows(st, size, dil), :][None] for st in starts], axis=0)


def _rows(start, size, dil):
    return pl.ds(start, size, stride=dil) if dil > 1 else pl.ds(start, size)


def attn_fwd(name, qkv, tab, gq, gk, grp, dil):
    s = qkv.shape[0]
    tile, halo, n_sub = _attn_geometry(s, dil)
    length, per_tile = s // dil, tile // dil
    nk = ATT_T + 2 * A_HALF

    def body(q_ref, kp_ref, km_ref, kn_ref, vp_ref, vm_ref, vn_ref, tp_ref, tm_ref, tn_ref, gq_ref, gk_ref, o_ref, lse_ref,
             q_buf, k_buf, v_buf):
        i, pair = pl.program_id(0), pl.program_id(1)
        q_buf[...] = _qk_prep(q_ref[...], tm_ref[...], gq_ref[...])
        for ref, t_ref, lo, n in ((kp_ref, tp_ref, 0, halo), (km_ref, tm_ref, halo, tile), (kn_ref, tn_ref, halo + tile, halo)):
            k_buf[lo:lo + n, :] = _qk_prep(ref[...], t_ref[...], gk_ref[...])
        for ref, lo, n in ((vp_ref, 0, halo), (vm_ref, halo, tile), (vn_ref, halo + tile, halo)):
            v_buf[lo:lo + n, :] = ref[...]

        jobs = _jobs(dil, n_sub)
        for g0 in range(0, len(jobs), ATT_FWD_BATCH):
            starts = jobs[g0:g0 + ATT_FWD_BATCH]
            valid = _attn_valid(g0, len(starts), dil, i * per_tile, length)
            o, lse = _attn_job(_gather_jobs(q_buf, starts, ATT_T, dil), _gather_jobs(k_buf, starts, nk, dil),
                               _gather_jobs(v_buf, starts, nk, dil), valid)
            for n, st in enumerate(starts):
                o_ref[_rows(st, ATT_T, dil), :] = o[n]
                lse_ref[_rows(st, ATT_T, dil), :] = lse[n]

    full = lambda a: pl.BlockSpec(a.shape, lambda i, j: (0, 0))
    return pl.pallas_call(
        body, name=name, grid=(s // tile, D_MODEL // LANES),
        in_specs=_attn_in_specs(grp, s, tile, halo) + [full(gq), full(gk)],
        out_specs=[pl.BlockSpec((tile, LANES), lambda i, j: (i, j)), pl.BlockSpec((tile, LANES), lambda i, j: (i, j))],
        out_shape=[jax.ShapeDtypeStruct((s, D_MODEL), F32), jax.ShapeDtypeStruct((s, D_MODEL), F32)],
        scratch_shapes=[pltpu.VMEM((tile, LANES), F32), pltpu.VMEM((tile + 2 * halo, LANES), F32), pltpu.VMEM((tile + 2 * halo, LANES), F32)],
        compiler_params=_params(("parallel", "parallel")),
    )(qkv, qkv, qkv, qkv, qkv, qkv, qkv, tab, tab, tab, gq, gk)


def attn_bwd(name, qkv, tab, gq, gk, o, lse, d_o, d_lse, grp, dil, d_qkv):
    s = qkv.shape[0]
    tile, halo, n_sub = _attn_geometry(s, dil)
    length, per_tile = s // dil, tile // dil
    nt = s // tile
    nk = ATT_T + 2 * A_HALF
    pieces = ((0, halo), (halo, tile), (halo + tile, halo))

    def body(q_ref, kp_ref, km_ref, kn_ref, vp_ref, vm_ref, vn_ref, tp_ref, tm_ref, tn_ref, gq_ref, gk_ref, o_ref, l_ref, do_ref, dl_ref,
             _, dq_ref, dkp_ref, dkm_ref, dkn_ref, dvp_ref, dvm_ref, dvn_ref, dgq_ref, dgk_ref, q_buf, k_buf, v_buf, dq_buf, dk_buf, dv_buf):
        i, pair = pl.program_id(0), pl.program_id(1)
        k_refs, t_refs = (kp_ref, km_ref, kn_ref), (tp_ref, tm_ref, tn_ref)
        q_buf[...] = _qk_prep(q_ref[...], tm_ref[...], gq_ref[...])
        for ref, t_ref, (lo, n) in zip(k_refs, t_refs, pieces):
            k_buf[lo:lo + n, :] = _qk_prep(ref[...], t_ref[...], gk_ref[...])
        for ref, (lo, n) in zip((vp_ref, vm_ref, vn_ref), pieces):
            v_buf[lo:lo + n, :] = ref[...]
        dk_buf[...] = jnp.zeros_like(dk_buf)
        dv_buf[...] = jnp.zeros_like(dv_buf)
        jobs = _jobs(dil, n_sub)
        for g0 in range(0, len(jobs), ATT_BWD_BATCH):
            starts = jobs[g0:g0 + ATT_BWD_BATCH]
            valid = _attn_valid(g0, len(starts), dil, i * per_tile, length)
            own = lambda ref: _gather_jobs(ref, starts, ATT_T, dil)
            dq, dk, dv = _attn_job_bwd(own(q_buf), _gather_jobs(k_buf, starts, nk, dil), _gather_jobs(v_buf, starts, nk, dil), valid,
                                       own(do_ref), own(o_ref), own(l_ref), own(dl_ref))
            for n, st in enumerate(starts):
                dq_buf[_rows(st, ATT_T, dil), :] = dq[n]
                dk_buf[_rows(st, nk, dil), :] += dk[n]
                dv_buf[_rows(st, nk, dil), :] += dv[n]
        _, vjp = jax.vjp(lambda x, g: _qk_prep(x, tm_ref[...], g), q_ref[...], gq_ref[...])
        dq, dgq = vjp(dq_buf[...])
        dq_ref[...] = dq.astype(dq_ref.dtype)
        dgk = jnp.zeros((1, LANES), F32)
        for ref, t_ref, out_ref, (lo, n) in zip(k_refs, t_refs, (dkp_ref, dkm_ref, dkn_ref), pieces):
            _, vjp = jax.vjp(lambda x, g: _qk_prep(x, t_ref[...], g), ref[...], gk_ref[...])
            out_ref[...], dgk_piece = vjp(dk_buf[lo:lo + n, :])
            dgk = dgk + dgk_piece
        for out_ref, (lo, n) in zip((dvp_ref, dvm_ref, dvn_ref), pieces):
            out_ref[...] = dv_buf[lo:lo + n, :]

        @pl.when((i == 0) & (pair == 0))
        def _():
            dgq_ref[...] = jnp.zeros_like(dgq_ref)
            dgk_ref[...] = jnp.zeros_like(dgk_ref)

        dgq_ref[...] += jnp.broadcast_to(dgq + pltpu.roll(dgq, A_HEAD_DIM, 1), dgq_ref.shape)
        dgk_ref[...] += jnp.broadcast_to(dgk + pltpu.roll(dgk, A_HEAD_DIM, 1), dgk_ref.shape)

    full = lambda a: pl.BlockSpec(a.shape, lambda i, j: (0, 0))
    main_o = pl.BlockSpec((tile, LANES), lambda i, j: (i, j))
    edge_o = pl.BlockSpec((None, halo, LANES), lambda i, j: (i, 0, j))
    main_s = jax.ShapeDtypeStruct((s, D_MODEL), F32)
    edge_s = jax.ShapeDtypeStruct((nt, halo, D_MODEL), F32)
    acc_o = pl.BlockSpec((8, LANES), lambda i, j: (0, 0))
    acc_s = jax.ShapeDtypeStruct((8, LANES), F32)
    big = pltpu.VMEM((tile + 2 * halo, LANES), F32)
    own = pltpu.VMEM((tile, LANES), F32)
    outs = pl.pallas_call(
        body, name=name, grid=(nt, D_MODEL // LANES),
        in_specs=_attn_in_specs(grp, s, tile, halo) + [full(gq), full(gk), main_o, main_o, main_o, main_o, ANY],
        out_specs=[pl.BlockSpec((tile, LANES), lambda i, j: (i, 24 * grp + j)), edge_o, main_o, edge_o, edge_o, main_o, edge_o, acc_o, acc_o],
        out_shape=[jax.ShapeDtypeStruct(d_qkv.shape, d_qkv.dtype), edge_s, main_s, edge_s, edge_s, main_s, edge_s, acc_s, acc_s],
        scratch_shapes=[own, big, big, own, big, big],
        input_output_aliases={16: 0},
        compiler_params=_params(("arbitrary", "arbitrary")),
    )(qkv, qkv, qkv, qkv, qkv, qkv, qkv, tab, tab, tab, gq, gk, o, lse, d_o, d_lse, d_qkv)
    d_qkv, dkp, dkm, dkn, dvp, dvm, dvn, dgq, dgk = outs
    return d_qkv, (dkp, dkm, dkn), (dvp, dvm, dvn), dgq, dgk


def attn_combine(name, parts, s, dil, d_qkv, col):
    prev_part, main_part, next_part = parts
    tile, halo, _ = _attn_geometry(s, dil)
    nt = s // tile
    cols = D_MODEL // 2

    def body(m_ref, from_prev_ref, from_next_ref, _, o_ref):
        i = pl.program_id(0)
        o_ref[...] = m_ref[...].astype(o_ref.dtype)
        head = m_ref[0:halo, :] + jnp.where(i > 0, from_prev_ref[...], 0.0)
        o_ref[0:halo, :] = head.astype(o_ref.dtype)
        tail = m_ref[tile - halo:tile, :] + jnp.where(i < nt - 1, from_next_ref[...], 0.0)
        o_ref[tile - halo:tile, :] = tail.astype(o_ref.dtype)

    return pl.pallas_call(
        body, name=name, grid=(nt, D_MODEL // cols),
        in_specs=[pl.BlockSpec((tile, cols), lambda i, c: (i, c)),
                  pl.BlockSpec((None, halo, cols), lambda i, c: (jnp.maximum(i - 1, 0), 0, c)),
                  pl.BlockSpec((None, halo, cols), lambda i, c: (jnp.minimum(i + 1, nt - 1), 0, c)), ANY],
        out_specs=pl.BlockSpec((tile, cols), lambda i, c: (i, (D_MODEL // cols) * col + c)),
        out_shape=jax.ShapeDtypeStruct(d_qkv.shape, d_qkv.dtype), input_output_aliases={3: 0},
        compiler_params=_params(("parallel", "parallel")),
    )(main_part, next_part, prev_part, d_qkv)


def attn_merge_fwd(name, os_, lses):
    s = os_[0].shape[0]

    def body(o0, o1, o2, l0, l1, l2, out_ref):
        out_ref[...] = _merge_groups(o0[...], o1[...], o2[...], l0[...], l1[...], l2[...]).astype(out_ref.dtype)

    return _rows_call(name, body, [(a, "row") for a in (*os_, *lses)], [(D_MODEL, BF16, "row")], s, _row_tile(s, 256))[0]


def attn_merge_bwd(name, os_, lses, d_out):
    s = os_[0].shape[0]

    def body(o0, o1, o2, l0, l1, l2, d_ref, *outs):
        _, vjp = jax.vjp(_merge_groups, o0[...], o1[...], o2[...], l0[...], l1[...], l2[...])
        for ref, val in zip(outs, vjp(d_ref[...])):
            ref[...] = val

    return _rows_call(name, body, [(a, "row") for a in (*os_, *lses, d_out)],
                      [(D_MODEL, F32, "row")] * 6, s, _row_tile(s, 256))


def gla_gate_fwd(name, z, wg, bias):
    s = z.shape[0]

    def body(z_ref, w_ref, b_ref, o_ref):
        o_ref[...] = _gate(z_ref[...], w_ref[...], b_ref[...])

    return _rows_call(name, body, [(z, "row"), (wg, "full"), (bias, "full")], [(D_MODEL, F32, "row")], s, _row_tile(s))[0]


def gla_gate_bwd(name, z, wg, bias, d_la_f, d_la_b):
    s = z.shape[0]
    tile = _row_tile(s)

    def body(z_ref, w_ref, b_ref, df_ref, db_ref, dz_ref, dw_ref, dbias_ref):
        _, vjp = jax.vjp(_gate, z_ref[...], w_ref[...], b_ref[...])
        dz, dw, dbias = vjp(jnp.concatenate([df_ref[...], db_ref[...]], axis=1))
        dz_ref[...] = dz

        @pl.when(pl.program_id(0) == 0)
        def _():
            dw_ref[...] = jnp.zeros_like(dw_ref)

        dw_ref[...] += dw
        _acc_rows(dbias_ref, dbias)

    return pl.pallas_call(
        body, name=name, grid=(s // tile,),
        in_specs=[pl.BlockSpec((tile, LANES), lambda i: (i, 0)), pl.BlockSpec(wg.shape, lambda i: (0, 0)),
                  pl.BlockSpec(bias.shape, lambda i: (0, 0)), pl.BlockSpec((tile, 512), lambda i: (i, 0)),
                  pl.BlockSpec((tile, 512), lambda i: (i, 0))],
        out_specs=[pl.BlockSpec((tile, LANES), lambda i: (i, 0)), pl.BlockSpec(wg.shape, lambda i: (0, 0)),
                   pl.BlockSpec((8, D_MODEL), lambda i: (0, 0))],
        out_shape=[jax.ShapeDtypeStruct((s, LANES), F32), jax.ShapeDtypeStruct(wg.shape, F32), jax.ShapeDtypeStruct((8, D_MODEL), F32)],
        compiler_params=_params(("arbitrary",)),
    )(z, wg, bias, d_la_f, d_la_b)


def _gla_in_specs(tile, order, la_col0):
    t = order
    return [pl.BlockSpec((tile, B_KEY_DIM), lambda h, n: (t(n), h)),
            pl.BlockSpec((tile, B_KEY_DIM), lambda h, n: (t(n), B_HEADS + h)),
            pl.BlockSpec((tile, B_VAL_DIM), lambda h, n: (t(n), B_HEADS + h)),
            pl.BlockSpec((tile, B_KEY_DIM), lambda h, n: (t(n), la_col0 + h))]


def gla_fwd(name, proj, la, reverse):
    s = proj.shape[0]
    tile = _row_tile(s)
    nt = s // tile
    order = (lambda n: nt - 1 - n) if reverse else (lambda n: n)

    def body(q_ref, k_ref, v_ref, la_ref, o_ref, st_ref, st_scr):
        @pl.when(pl.program_id(1) == 0)
        def _():
            st_scr[...] = jnp.zeros_like(st_scr)

        st_ref[...] = st_scr[...]
        o, st = _gla_tile(q_ref[...], k_ref[...], v_ref[...], la_ref[...], st_scr[...], reverse)
        o_ref[...] = o
        st_scr[...] = st

    return pl.pallas_call(
        body, name=name, grid=(B_HEADS, nt), in_specs=_gla_in_specs(tile, order, B_HEADS if reverse else 0),
        out_specs=[pl.BlockSpec((tile, B_VAL_DIM), lambda h, n: (order(n), h)),
                   pl.BlockSpec((None, None, B_VAL_DIM, B_KEY_DIM), lambda h, n: (h, order(n), 0, 0))],
        out_shape=[jax.ShapeDtypeStruct((s, D_MODEL), F32), jax.ShapeDtypeStruct((B_HEADS, nt, B_VAL_DIM, B_KEY_DIM), F32)],
        scratch_shapes=[pltpu.VMEM((B_VAL_DIM, B_KEY_DIM), F32)],
        compiler_params=_params(("parallel", "arbitrary")),
    )(proj, proj, proj, la)


def gla_bwd(name, proj, la, states, d_o, reverse, prev=None):
    s = proj.shape[0]
    tile = _row_tile(s)
    nt = s // tile
    order = (lambda n: n) if reverse else (lambda n: nt - 1 - n)

    def body(*refs):
        q_ref, k_ref, v_ref, la_ref, st_ref, do_ref = refs[:6]
        rest = refs[6:]
        prev_refs = rest[:3] if prev is not None else None
        dq_ref, dk_ref, dv_ref, dla_ref, dst_scr = rest[3:] if prev is not None else rest

        @pl.when(pl.program_id(1) == 0)
        def _():
            dst_scr[...] = jnp.zeros_like(dst_scr)

        _, vjp = jax.vjp(functools.partial(_gla_tile, reverse=reverse), q_ref[...], k_ref[...], v_ref[...], la_ref[...], st_ref[...])
        dq, dk, dv, dla, dst = vjp((do_ref[...], dst_scr[...]))
        if prev_refs is not None:
            dq, dk, dv = dq + prev_refs[0][...], dk + prev_refs[1][...], dv + prev_refs[2][...]
        dq_ref[...], dk_ref[...], dv_ref[...], dla_ref[...] = dq, dk, dv, dla
        dst_scr[...] = dst

    key_spec = pl.BlockSpec((tile, B_KEY_DIM), lambda h, n: (order(n), h))
    val_spec = pl.BlockSpec((tile, B_VAL_DIM), lambda h, n: (order(n), h))
    in_specs = _gla_in_specs(tile, order, B_HEADS if reverse else 0) + [
        pl.BlockSpec((None, None, B_VAL_DIM, B_KEY_DIM), lambda h, n: (h, order(n), 0, 0)), val_spec]
    ins = [proj, proj, proj, la, states, d_o]
    if prev is not None:
        in_specs += [key_spec, key_spec, val_spec]
        ins += list(prev)
    return pl.pallas_call(
        body, name=name, grid=(B_HEADS, nt), in_specs=in_specs,
        out_specs=[key_spec, key_spec, val_spec, key_spec],
        out_shape=[jax.ShapeDtypeStruct((s, 512), F32), jax.ShapeDtypeStruct((s, 512), F32),
                   jax.ShapeDtypeStruct((s, D_MODEL), F32), jax.ShapeDtypeStruct((s, 512), F32)],
        scratch_shapes=[pltpu.VMEM((B_VAL_DIM, B_KEY_DIM), F32)],
        compiler_params=_params(("parallel", "arbitrary")),
    )(*ins)


def _r_spec(tile):
    return pl.BlockSpec((tile, D_MODEL), lambda i: (i, 2))


def gla_post_fwd(name, o_f, o_b, proj, gain):
    s = o_f.shape[0]
    tile = _row_tile(s)

    def body(of_ref, ob_ref, r_ref, g_ref, out_ref):
        out_ref[...] = _gla_post(of_ref[...], ob_ref[...], r_ref[...], g_ref[...]).astype(out_ref.dtype)

    row = pl.BlockSpec((tile, D_MODEL), lambda i: (i, 0))
    return pl.pallas_call(body, name=name, grid=(s // tile,),
                          in_specs=[row, row, _r_spec(tile), pl.BlockSpec(gain.shape, lambda i: (0, 0))], out_specs=row,
                          out_shape=jax.ShapeDtypeStruct((s, D_MODEL), BF16), compiler_params=_params(("parallel",)))(o_f, o_b, proj, gain)


def gla_post_bwd(name, o_f, o_b, proj, gain, d_out):
    s = o_f.shape[0]
    tile = _row_tile(s)

    def body(of_ref, ob_ref, r_ref, g_ref, d_ref, do_ref, dr_ref, dg_ref):
        _, vjp = jax.vjp(_gla_post, of_ref[...], ob_ref[...], r_ref[...], g_ref[...])
        d_of, _, dr, dg = vjp(d_ref[...])
        do_ref[...] = d_of
        dr_ref[...] = dr
        _acc_rows(dg_ref, dg)

    row = pl.BlockSpec((tile, D_MODEL), lambda i: (i, 0))
    return pl.pallas_call(
        body, name=name, grid=(s // tile,),
        in_specs=[row, row, _r_spec(tile), pl.BlockSpec(gain.shape, lambda i: (0, 0)), row],
        out_specs=[row, row, pl.BlockSpec((8, D_MODEL), lambda i: (0, 0))],
        out_shape=[jax.ShapeDtypeStruct((s, D_MODEL), F32), jax.ShapeDtypeStruct((s, D_MODEL), F32), jax.ShapeDtypeStruct((8, D_MODEL), F32)],
        compiler_params=_params(("arbitrary",)))(o_f, o_b, proj, gain, d_out)


def _hid(tile, where):
    return pl.BlockSpec((None, tile, FFN_BLK), where)


def _pair(tile, where):
    return pl.BlockSpec((2, None, tile, FFN_BLK), where)


def _w_gu_spec(layer, where_j):
    return pl.BlockSpec((2, None, None, D_MODEL, FFN_BLK), lambda *g: (0, where_j(*g), layer, 0, 0))


def ffn_fwd(name, h_mid, hn2, w_gu, w_down, layer):
    s = hn2.shape[0]
    tm = _row_tile(s, 1024)
    nt = s // tm

    def gu_body(x_ref, w_ref, gu_ref, act_ref):
        x = x_ref[...]
        g = _dot(x, w_ref[0], NN)
        u = _dot(x, w_ref[1], NN)
        gu_ref[0] = g
        gu_ref[1] = u
        act_ref[...] = _swiglu_act(g, u).astype(act_ref.dtype)

    gu, act = pl.pallas_call(
        gu_body, name=name + "_gu", grid=(4, nt),
        in_specs=[pl.BlockSpec((tm, D_MODEL), lambda j, i: (i, 0)), _w_gu_spec(layer, lambda j, i: j)],
        out_specs=[_pair(tm, lambda j, i: (0, j, i, 0)), _hid(tm, lambda j, i: (j, i, 0))],
        out_shape=[jax.ShapeDtypeStruct((2, 4, s, FFN_BLK), F32), jax.ShapeDtypeStruct((4, s, FFN_BLK), BF16)],
        compiler_params=_params(("parallel", "parallel")))(hn2, w_gu)
    row = pl.BlockSpec((tm, D_MODEL), lambda i, j: (i, 0))
    h_next = _matmul(name + "_down", act, w_down, dims=NN, grid=(nt, 4), red_axis=1,
                     a_spec=_hid(tm, lambda i, j: (j, i, 0)), b_spec=pl.BlockSpec((None, FFN_BLK, D_MODEL), lambda i, j: (j, 0, 0)),
                     o_spec=row, out_shape=jax.ShapeDtypeStruct((s, D_MODEL), F32), res=h_mid, res_spec=row)
    return h_next, gu, act


def ffn_bwd(name, dh_next, hn2, gu, act, w_gu, w_down, layer):
    s = hn2.shape[0]
    tm = _row_tile(s, 1024)
    nt = s // tm
    tw = _row_tile(s, 2048)
    ntw = s // tw
    d_wd = _matmul(name + "_dwd", act, dh_next, dims=TN, grid=(4, ntw), red_axis=1,
                   a_spec=_hid(tw, lambda j, i: (j, i, 0)), b_spec=pl.BlockSpec((tw, D_MODEL), lambda j, i: (i, 0)),
                   o_spec=pl.BlockSpec((None, FFN_BLK, D_MODEL), lambda j, i: (j, 0, 0)),
                   out_shape=jax.ShapeDtypeStruct((4, FFN_BLK, D_MODEL), BF16), acc_shape=(FFN_BLK, D_MODEL))

    def dgu_body(dy_ref, wd_ref, gu_ref, dgu_ref):
        d_act = _dot(dy_ref[...], wd_ref[...], NT)
        _, vjp = jax.vjp(_swiglu_act, gu_ref[0], gu_ref[1])
        dg, du = vjp(d_act)
        dgu_ref[0] = dg.astype(dgu_ref.dtype)
        dgu_ref[1] = du.astype(dgu_ref.dtype)

    d_gu = pl.pallas_call(
        dgu_body, name=name + "_dgu", grid=(4, nt),
        in_specs=[pl.BlockSpec((tm, D_MODEL), lambda j, i: (i, 0)), pl.BlockSpec((None, FFN_BLK, D_MODEL), lambda j, i: (j, 0, 0)),
                  _pair(tm, lambda j, i: (0, j, i, 0))],
        out_specs=_pair(tm, lambda j, i: (0, j, i, 0)), out_shape=jax.ShapeDtypeStruct((2, 4, s, FFN_BLK), BF16),
        compiler_params=_params(("parallel", "parallel")))(dh_next, w_down, gu)

    def dx_body(d_ref, w_ref, o_ref):
        prod = _dot(d_ref[0], w_ref[0], NT) + _dot(d_ref[1], w_ref[1], NT)

        @pl.when(pl.program_id(1) == 0)
        def _():
            o_ref[...] = prod

        @pl.when(pl.program_id(1) > 0)
        def _():
            o_ref[...] += prod

    d_hn2 = pl.pallas_call(
        dx_body, name=name + "_dx", grid=(nt, 4),
        in_specs=[_pair(tm, lambda i, j: (0, j, i, 0)), _w_gu_spec(layer, lambda i, j: j)],
        out_specs=pl.BlockSpec((tm, D_MODEL), lambda i, j: (i, 0)), out_shape=jax.ShapeDtypeStruct((s, D_MODEL), F32),
        compiler_params=_params(("parallel", "arbitrary")))(d_gu, w_gu)

    def dw_body(x_ref, d_ref, o_ref, acc_ref):
        x = x_ref[...]
        k = pl.program_id(1)
        for t in range(2):
            prod = _dot(x, d_ref[t], TN)

            @pl.when(k == 0)
            def _():
                acc_ref[t] = prod

            @pl.when(k > 0)
            def _():
                acc_ref[t] += prod

        @pl.when(k == ntw - 1)
        def _():
            o_ref[...] = acc_ref[...].astype(o_ref.dtype)

    d_wgu = pl.pallas_call(
        dw_body, name=name + "_dwgu", grid=(4, ntw),
        in_specs=[pl.BlockSpec((tw, D_MODEL), lambda j, i: (i, 0)), _pair(tw, lambda j, i: (0, j, i, 0))],
        out_specs=pl.BlockSpec((2, None, D_MODEL, FFN_BLK), lambda j, i: (0, j, 0, 0)),
        out_shape=jax.ShapeDtypeStruct((2, 4, D_MODEL, FFN_BLK), BF16),
        scratch_shapes=[pltpu.VMEM((2, D_MODEL, FFN_BLK), F32)],
        compiler_params=_params(("parallel", "arbitrary")))(hn2, d_gu)
    return d_hn2, d_wgu, d_wd


def _my_place():
    return lax.axis_index("x"), lax.axis_index("y"), lax.axis_index("c")


def _flip(place, k):
    x, y, c = place
    return (1 - x if k & 4 else x, 1 - y if k & 2 else y, 1 - c if k & 1 else c)


def _index(place):
    return 4 * place[0] + 2 * place[1] + place[2]


def all_gather(arrs):
    n = len(arrs)

    def body(*refs):
        ins, outs = refs[:n], refs[n:2 * n]
        send_sems, recv_sems, local_sems = refs[2 * n:]
        me = _my_place()
        sibling = _flip(me, 1)
        chips = (4, 2, 6)

        def copy(a, k, block, to, src=None):
            dst = outs[a].at[_index(block)]
            return pltpu.make_async_remote_copy(src_ref=dst if src is None else src, dst_ref=dst, send_sem=send_sems.at[a, k],
                                                recv_sem=recv_sems.at[a, k], device_id=to, device_id_type=MESH)

        started = []
        for a in range(n):
            mine = pltpu.make_async_copy(ins[a], outs[a].at[_index(me)], local_sems.at[a])
            mine.start()
            started.append(mine)
        first = []
        for a in range(n):
            first.append(copy(a, 0, me, sibling, src=ins[a]))
            first += [copy(a, 1 + j, me, _flip(me, k), src=ins[a]) for j, k in enumerate(chips)]
        for cp in first:
            cp.start()
        passed = []
        for a in range(n):
            for j, k in enumerate(chips):
                copy(a, 1 + j, _flip(me, k), me).wait_recv()
                fwd = copy(a, 4 + j, _flip(me, k), sibling)
                fwd.start()
                passed.append(fwd)
        for a in range(n):
            copy(a, 0, sibling, me).wait_recv()
            for j, k in enumerate(chips):
                copy(a, 4 + j, _flip(sibling, k), me).wait_recv()
        for cp in first + passed:
            cp.wait_send()
        for cp in started:
            cp.wait()

    return pl.pallas_call(
        body, name="all_gather_weights", in_specs=[ANY] * n, out_specs=[ANY] * n,
        out_shape=[jax.ShapeDtypeStruct((N_DEV,) + a.shape, a.dtype) for a in arrs],
        scratch_shapes=[pltpu.SemaphoreType.DMA((n, 7)), pltpu.SemaphoreType.DMA((n, 7)), pltpu.SemaphoreType.DMA((n,))],
    )(*arrs)


def exchange_partials(arrs):
    n = len(arrs)

    def body(*refs):
        ins, outs = refs[:n], refs[n:2 * n]
        send_sems, recv_sems, local_sems = refs[2 * n:]
        me = _my_place()
        local = []
        for a in range(n):
            cp = pltpu.make_async_copy(ins[a].at[_index(me)], outs[a].at[_index(me)], local_sems.at[a])
            cp.start()
            local.append(cp)

        def copy(a, k, src_block, dst_block):
            return pltpu.make_async_remote_copy(src_ref=ins[a].at[_index(src_block)], dst_ref=outs[a].at[_index(dst_block)],
                                                send_sem=send_sems.at[a, k - 1], recv_sem=recv_sems.at[a, k - 1],
                                                device_id=_flip(me, k), device_id_type=MESH)

        sent = []
        for a in range(n):
            for k in range(1, N_DEV):
                cp = copy(a, k, _flip(me, k), me)
                cp.start()
                sent.append(cp)
        for a in range(n):
            for k in range(1, N_DEV):
                copy(a, k, me, _flip(me, k)).wait_recv()
        for cp in sent:
            cp.wait_send()
        for cp in local:
            cp.wait()

    return pl.pallas_call(
        body, name="exchange_weight_grads", in_specs=[ANY] * n, out_specs=[ANY] * n,
        out_shape=[jax.ShapeDtypeStruct(a.shape, a.dtype) for a in arrs],
        scratch_shapes=[pltpu.SemaphoreType.DMA((n, 7)), pltpu.SemaphoreType.DMA((n, 7)), pltpu.SemaphoreType.DMA((n,))],
    )(*arrs)


def adamw_shard(name, parts, w, m, v, layer, tile):
    rows, cols = parts.shape[1:]
    assert rows % tile == 0
    off = layer * (rows // tile)

    def body(p_ref, w_ref, m_ref, v_ref, g_ref, d_ref, nm_ref, nv_ref):
        g = p_ref[0].astype(F32)
        for src in range(1, N_DEV):
            g = g + p_ref[src].astype(F32)
        g_ref[...] = g
        d_ref[...], nm_ref[...], nv_ref[...] = _adamw(w_ref[...], g, m_ref[...], v_ref[...])

    src_row = pl.BlockSpec((tile, cols), lambda i: (off + i, 0))
    row = pl.BlockSpec((tile, cols), lambda i: (i, 0))
    shape = jax.ShapeDtypeStruct((rows, cols), F32)
    return pl.pallas_call(body, name=name, grid=(rows // tile,),
                          in_specs=[pl.BlockSpec((N_DEV, tile, cols), lambda i: (0, i, 0)), src_row, src_row, src_row],
                          out_specs=[row] * 4, out_shape=[shape] * 4, compiler_params=_params(("parallel",)))(parts, w, m, v)


def allreduce_adamw_replicated(partial, w, m, v, n_loss_rows):
    rows = partial.shape[0]

    def body(p_ref, w_ref, m_ref, v_ref, g_ref, d_ref, nm_ref, nv_ref, loss_ref, recv_ref, send_sems, recv_sems):
        me = _my_place()
        recv_ref[_index(me)] = p_ref[...]
        copies = []
        for k in range(1, N_DEV):
            peer = _flip(me, k)
            cp = pltpu.make_async_remote_copy(src_ref=p_ref, dst_ref=recv_ref.at[_index(me)], send_sem=send_sems.at[k - 1],
                                              recv_sem=recv_sems.at[k - 1], device_id=peer, device_id_type=MESH)
            cp.start()
            copies.append((cp, peer))
        for k, (cp, peer) in enumerate(copies):
            pltpu.make_async_remote_copy(src_ref=p_ref, dst_ref=recv_ref.at[_index(peer)], send_sem=send_sems.at[k],
                                         recv_sem=recv_sems.at[k], device_id=peer, device_id_type=MESH).wait_recv()
        for cp, _ in copies:
            cp.wait_send()
        g = recv_ref[0]
        for src in range(1, N_DEV):
            g = g + recv_ref[src]
        g_ref[...] = g
        d_ref[...], nm_ref[...], nv_ref[...] = _adamw(w_ref[...], g, m_ref[...], v_ref[...])
        loss = (0.5 / D_MODEL) * jnp.sum(g[rows - n_loss_rows:, :])
        loss_ref[...] = jnp.full(loss_ref.shape, loss, F32)

    shape = jax.ShapeDtypeStruct((rows, LANES), F32)
    return pl.pallas_call(
        body, name="allreduce_adamw_replicated", in_specs=[VMEM_SPEC] * 4, out_specs=[VMEM_SPEC] * 5,
        out_shape=[shape] * 4 + [jax.ShapeDtypeStruct((8, LANES), F32)],
        scratch_shapes=[pltpu.VMEM((N_DEV, rows, LANES), F32), pltpu.SemaphoreType.DMA((7,)), pltpu.SemaphoreType.DMA((7,))],
    )(partial, w, m, v)


def _pack_rows(flat, cols):
    n = flat.shape[-1]
    rows = -(-n // cols)
    rows = -(-rows // 48) * 48
    flat = jnp.pad(flat, [(0, 0)] * (flat.ndim - 1) + [(0, rows * cols - n)])
    return flat.reshape(flat.shape[:-1] + (rows, cols))


def _unpack(flat, shapes):
    out, off = [], 0
    for shp in shapes:
        n = 1
        for d in shp:
            n *= d
        out.append(flat[off:off + n].reshape(shp))
        off += n
    return out


def _to_dev_cols(a):
    w = a.shape[-1] // N_DEV
    return jnp.moveaxis(a.reshape(a.shape[:-1] + (N_DEV, w)), -2, 0)


def _from_dev_cols(a):
    a = jnp.moveaxis(a, 0, -2)
    return a.reshape(a.shape[:-2] + (a.shape[-2] * a.shape[-1],))


def kernel(x, attn_norm, ffn_norm, a_w_in, a_q_norm, a_k_norm, a_w_out, b_w_in, b_w_gate_f, b_gate_bias_f, b_w_gate_b, b_gate_bias_b, b_out_norm, b_w_out, ffn_w_gate_up, ffn_w_down, loss_target, m_attn_norm, m_ffn_norm, m_a_w_in, m_a_q_norm, m_a_k_norm, m_a_w_out, m_b_w_in, m_b_w_gate_f, m_b_gate_bias_f, m_b_w_gate_b, m_b_gate_bias_b, m_b_out_norm, m_b_w_out, m_ffn_w_gate_up, m_ffn_w_down, v_attn_norm, v_ffn_norm, v_a_w_in, v_a_q_norm, v_a_k_norm, v_a_w_out, v_b_w_in, v_b_w_gate_f, v_b_gate_bias_f, v_b_w_gate_b, v_b_gate_bias_b, v_b_out_norm, v_b_w_out, v_ffn_w_gate_up, v_ffn_w_down):
    seq = x.shape[1]
    depth = attn_norm.shape[0]
    h = x.reshape(seq, D_MODEL)
    target = loss_target.reshape(seq, D_MODEL)
    n_a, n_b = a_w_in.shape[0], b_w_in.shape[0]

    small = jnp.concatenate([t.reshape(-1) for t in (b_w_gate_f, b_gate_bias_f, b_w_gate_b, b_gate_bias_b, b_out_norm)])
    small = _pack_rows(small, LANES)
    g_a_in, g_a_out, g_b_in, g_b_out, g_gu, g_down, g_small = all_gather(
        [a_w_in.astype(BF16), a_w_out.astype(BF16), b_w_in.astype(BF16), b_w_out.astype(BF16),
         ffn_w_gate_up.astype(BF16), ffn_w_down.astype(BF16), small])
    w_a_in = _from_dev_cols(g_a_in)
    w_a_out = jnp.moveaxis(g_a_out, 0, 1).reshape(n_a, D_MODEL, D_MODEL)
    w_b_in = _from_dev_cols(g_b_in)
    w_b_out = jnp.moveaxis(g_b_out, 0, 1).reshape(n_b, D_MODEL, D_MODEL)
    w_down = jnp.moveaxis(g_down, 0, 1).reshape(depth, 4, FFN_BLK, D_MODEL)
    w_gu = g_gu.reshape(2, 4, depth, D_MODEL, FFN_BLK)
    small_shapes = [t.shape for t in (b_w_gate_f, b_gate_bias_f, b_w_gate_b, b_gate_bias_b, b_out_norm)]
    per_dev = [_unpack(g_small[d].reshape(-1), small_shapes) for d in range(N_DEV)]
    wgf, bgf, wgb, bgb, onorm = [_from_dev_cols(jnp.stack([per_dev[d][t] for d in range(N_DEV)])) for t in range(5)]
    w_gate = jnp.zeros((n_b, LANES, D_MODEL), F32)
    w_gate = w_gate.at[:, 0:16, 0:512].set(wgf).at[:, 16:32, 512:1024].set(wgb)
    gate_bias = jnp.concatenate([bgf, bgb], axis=1).reshape(n_b, 1, D_MODEL)
    out_gain = onorm.reshape(n_b, 1, D_MODEL)
    w_b_main = w_b_in[:, :, :3072]
    w_b_z = jnp.pad(w_b_in[:, :, 3072:], ((0, 0), (0, 0), (0, LANES - 32)))

    dh, sq_err, grads = _forward_backward(h, target, attn_norm, ffn_norm, a_q_norm, a_k_norm, w_a_in, w_a_out, w_b_main, w_b_z,
                                          w_gate, gate_bias, out_gain, w_b_out, w_gu, w_down)
    sharded_w = (a_w_in, a_w_out, b_w_in, b_w_gate_f, b_gate_bias_f, b_w_gate_b, b_gate_bias_b, b_out_norm, b_w_out, ffn_w_gate_up, ffn_w_down)
    sharded_m = (m_a_w_in, m_a_w_out, m_b_w_in, m_b_w_gate_f, m_b_gate_bias_f, m_b_w_gate_b, m_b_gate_bias_b, m_b_out_norm, m_b_w_out, m_ffn_w_gate_up, m_ffn_w_down)
    sharded_v = (v_a_w_in, v_a_w_out, v_b_w_in, v_b_w_gate_f, v_b_gate_bias_f, v_b_w_gate_b, v_b_gate_bias_b, v_b_out_norm, v_b_w_out, v_ffn_w_gate_up, v_ffn_w_down)
    rep_w = (attn_norm, ffn_norm, a_q_norm, a_k_norm)
    rep_m = (m_attn_norm, m_ffn_norm, m_a_q_norm, m_a_k_norm)
    rep_v = (v_attn_norm, v_ffn_norm, v_a_q_norm, v_a_k_norm)
    loss, outs = _reduce_and_update(grads, sq_err, sharded_w, sharded_m, sharded_v, rep_w, rep_m, rep_v)
    return (loss, dh.reshape(x.shape), *outs)


def _forward_backward(h, target, attn_norm, ffn_norm, a_q_norm, a_k_norm, w_a_in, w_a_out, w_b_main, w_b_z, w_gate, gate_bias,
                      out_gain, w_b_out, w_gu, w_down):
    seq = h.shape[0]
    depth = attn_norm.shape[0]
    n_a, n_b = w_a_in.shape[0], w_b_main.shape[0]
    tab = _rope_table(seq)
    pair_gain = lambda g: jnp.concatenate([g, g]).reshape(1, LANES)

    saved = []
    for i in range(depth):
        j = i // 2
        nm = f"l{i}"
        hn = rmsnorm_fwd(nm + "_norm1", h, attn_norm[i].reshape(1, D_MODEL))
        if i % 2 == 0:
            qkv = mm_nn(nm + "_qkv", hn, w_a_in[j])
            os_, lses = [], []
            for g, (_, dil) in enumerate(A_GROUPS):
                o, lse = attn_fwd(f"{nm}_attn{g}", qkv, tab, pair_gain(a_q_norm[j, g]), pair_gain(a_k_norm[j, g]), g, dil)
                os_.append(o)
                lses.append(lse)
            mixed = attn_merge_fwd(nm + "_merge", os_, lses)
            h_mid = mm_nn(nm + "_out", mixed, w_a_out[j], res=h)
            mix_saved = (qkv, os_, lses, mixed)
        else:
            proj = mm_nn(nm + "_proj", hn, w_b_main[j])
            z = mm_nn(nm + "_z", hn, w_b_z[j])
            la = gla_gate_fwd(nm + "_gate", z, w_gate[j], gate_bias[j])
            o_f, st_f = gla_fwd(nm + "_gla_f", proj, la, False)
            o_b, st_b = gla_fwd(nm + "_gla_b", proj, la, True)
            mixed = gla_post_fwd(nm + "_post", o_f, o_b, proj, out_gain[j])
            h_mid = mm_nn(nm + "_out", mixed, w_b_out[j], res=h)
            mix_saved = (proj, z, la, o_f, st_f, o_b, st_b, mixed)
        hn2 = rmsnorm_fwd(nm + "_norm2", h_mid, ffn_norm[i].reshape(1, D_MODEL))
        h_next, gu, act = ffn_fwd(nm + "_ffn", h_mid, hn2, w_gu, w_down[i], i)
        saved.append((h, hn, mix_saved, h_mid, hn2, gu, act))
        h = h_next

    dh, sq_err = loss_and_grad(h, target)

    g_attn_norm, g_ffn_norm = [None] * depth, [None] * depth
    g_a_w_in, g_a_w_out, g_a_q, g_a_k = [None] * n_a, [None] * n_a, [None] * n_a, [None] * n_a
    g_b_w_in, g_b_w_out, g_w_gate, g_gate_bias, g_out_gain = ([None] * n_b for _ in range(5))
    g_w_gu, g_w_down = [None] * depth, [None] * depth
    for i in reversed(range(depth)):
        j = i // 2
        nm = f"l{i}b"
        h_in, hn, mix_saved, h_mid, hn2, gu, act = saved[i]
        d_hn2, g_w_gu[i], g_w_down[i] = ffn_bwd(nm + "_ffn", dh, hn2, gu, act, w_gu, w_down[i], i)
        dh_mid, g_ffn_norm[i] = rmsnorm_bwd(nm + "_norm2", h_mid, ffn_norm[i].reshape(1, D_MODEL), d_hn2, dh)
        if i % 2 == 0:
            qkv, os_, lses, mixed = mix_saved
            d_mixed = mm_nt(nm + "_dmixed", dh_mid, w_a_out[j])
            g_a_w_out[j] = mm_tn(nm + "_dwout", mixed, dh_mid, by_block=True)
            d_parts = attn_merge_bwd(nm + "_merge", os_, lses, d_mixed)
            gq_l, gk_l = [], []
            d_qkv = lax.empty(qkv.shape, BF16)
            for g, (_, dil) in enumerate(A_GROUPS):
                d_qkv, dk_parts, dv_parts, dgq, dgk = attn_bwd(f"{nm}_attn{g}", qkv, tab, pair_gain(a_q_norm[j, g]),
                                                               pair_gain(a_k_norm[j, g]), os_[g], lses[g], d_parts[g], d_parts[3 + g], g, dil, d_qkv)
                d_qkv = attn_combine(f"{nm}_dk{g}", dk_parts, seq, dil, d_qkv, 3 * g + 1)
                d_qkv = attn_combine(f"{nm}_dv{g}", dv_parts, seq, dil, d_qkv, 3 * g + 2)
                gq_l.append(dgq[0, :A_HEAD_DIM])
                gk_l.append(dgk[0, :A_HEAD_DIM])
            g_a_q[j], g_a_k[j] = jnp.stack(gq_l), jnp.stack(gk_l)
            d_hn = mm_nt(nm + "_dhn", d_qkv, w_a_in[j])
            g_a_w_in[j] = mm_tn(nm + "_dwin", hn, d_qkv, tn=D_MODEL * 9 // N_DEV, by_block=True)
        else:
            proj, z, la, o_f, st_f, o_b, st_b, mixed = mix_saved
            d_mixed = mm_nt(nm + "_dmixed", dh_mid, w_b_out[j])
            g_b_w_out[j] = mm_tn(nm + "_dwout", mixed, dh_mid, by_block=True)
            d_o, d_r, dgain = gla_post_bwd(nm + "_post", o_f, o_b, proj, out_gain[j], d_mixed)
            g_out_gain[j] = dgain[0]
            dq, dk, dv, dla_f = gla_bwd(nm + "_gla_f", proj, la, st_f, d_o, False)
            dq, dk, dv, dla_b = gla_bwd(nm + "_gla_b", proj, la, st_b, d_o, True, prev=(dq, dk, dv))
            d_z, g_w_gate[j], dbias = gla_gate_bwd(nm + "_gate", z, w_gate[j], gate_bias[j], dla_f, dla_b)
            g_gate_bias[j] = dbias[0]
            d_proj = jnp.concatenate([dq, dk, dv, d_r], axis=1)
            d_hn = mm_nt(nm + "_dhn_z", d_z, w_b_z[j])
            d_hn = mm_nt(nm + "_dhn", d_proj, w_b_main[j], res=d_hn)
            g_b_w_in[j] = jnp.concatenate([mm_tn(nm + "_dwin", hn, d_proj), mm_tn(nm + "_dwz", hn, d_z)[:, :32]], axis=1)
        dh, g_attn_norm[i] = rmsnorm_bwd(nm + "_norm1", h_in, attn_norm[i].reshape(1, D_MODEL), d_hn, dh_mid)
    return dh, sq_err, (g_attn_norm, g_ffn_norm, g_a_w_in, g_a_w_out, g_a_q, g_a_k, g_b_w_in, g_b_w_out, g_w_gate, g_gate_bias,
                        g_out_gain, g_w_gu, g_w_down)


def _reduce_and_update(grads, sq_err, sharded_w, sharded_m, sharded_v, rep_w, rep_m, rep_v):
    (g_attn_norm, g_ffn_norm, g_a_w_in, g_a_w_out, g_a_q, g_a_k, g_b_w_in, g_b_w_out, g_w_gate, g_gate_bias, g_out_gain,
     g_w_gu, g_w_down) = grads
    depth, n_a, n_b = len(g_w_gu), len(g_a_w_in), len(g_b_w_in)

    g_w_gate = jnp.stack(g_w_gate)
    g_gate_bias = jnp.stack(g_gate_bias)
    small_parts = [_to_dev_cols(g_w_gate[:, 0:16, 0:512]), _to_dev_cols(g_gate_bias[:, 0:512]),
                   _to_dev_cols(g_w_gate[:, 16:32, 512:1024]), _to_dev_cols(g_gate_bias[:, 512:1024]),
                   _to_dev_cols(jnp.stack(g_out_gain).reshape(n_b, B_HEADS, B_VAL_DIM))]
    small_part = _pack_rows(jnp.concatenate([t.reshape(N_DEV, -1) for t in small_parts], axis=1), LANES).astype(BF16)
    families = [
        (0, g_a_w_in, 256),
        (1, [t.reshape(N_DEV, -1, D_MODEL) for t in g_a_w_out], 128),
        (2, [_to_dev_cols(t).astype(BF16) for t in g_b_w_in], 256),
        (8, [t.reshape(N_DEV, -1, D_MODEL) for t in g_b_w_out], 128),
        (9, [t.reshape(N_DEV, D_MODEL, FFN_BLK) for t in g_w_gu], 256),
        (10, [t.reshape(N_DEV, -1, D_MODEL) for t in g_w_down], 176),
    ]
    flat_parts = [p for _, parts, _ in families for p in parts] + [small_part]
    received = exchange_partials(flat_parts)
    sh_out = [None] * len(sharded_w)
    pos = 0
    for fam, parts, tile in families:
        w = sharded_w[fam]
        two_d = lambda t: t.reshape(-1, t.shape[-1])
        per_layer = []
        for layer in range(len(parts)):
            per_layer.append(adamw_shard(f"adamw_p{fam}_l{layer}", received[pos], two_d(w), two_d(sharded_m[fam]),
                                         two_d(sharded_v[fam]), layer, tile))
            pos += 1
        sh_out[fam] = [jnp.stack([per_layer[l][t] for l in range(len(parts))]).reshape(w.shape) for t in range(4)]
    small_ids = (3, 4, 5, 6, 7)
    pack_small = lambda ts: _pack_rows(jnp.concatenate([ts[i].reshape(-1) for i in small_ids]), LANES)
    small_out = adamw_shard("adamw_small", received[pos], pack_small(sharded_w), pack_small(sharded_m), pack_small(sharded_v), 0, 48)
    small_shapes = [sharded_w[i].shape for i in small_ids]
    for t in range(4):
        for i, val in zip(small_ids, _unpack(small_out[t].reshape(-1), small_shapes)):
            if sh_out[i] is None:
                sh_out[i] = [None] * 4
            sh_out[i][t] = val
    sh_grad, sh_delta, sh_m, sh_v = [[sh_out[i][t] for i in range(len(sharded_w))] for t in range(4)]

    rep_g = (jnp.stack([t[0] for t in g_attn_norm]), jnp.stack([t[0] for t in g_ffn_norm]), jnp.stack(g_a_q), jnp.stack(g_a_k))
    n_rep = sum(t.size for t in rep_w)
    n_rep_rows = -(-n_rep // (8 * LANES)) * 8
    n_loss_rows = 8 * D_MODEL // LANES

    def pack_rep(ts, tail):
        flat = jnp.concatenate([t.reshape(-1) for t in ts])
        flat = jnp.pad(flat, (0, n_rep_rows * LANES - n_rep))
        return jnp.concatenate([flat.reshape(n_rep_rows, LANES), tail], axis=0)

    zeros_tail = jnp.zeros((n_loss_rows, LANES), F32)
    rep_out = allreduce_adamw_replicated(pack_rep(rep_g, sq_err.reshape(n_loss_rows, LANES)), pack_rep(rep_w, zeros_tail),
                                         pack_rep(rep_m, zeros_tail), pack_rep(rep_v, zeros_tail + 1.0), n_loss_rows)
    rep_shapes = [t.shape for t in rep_w]
    r_grad, r_delta, r_m, r_v = [_unpack(p.reshape(-1), rep_shapes) for p in rep_out[:4]]
    loss = rep_out[4][0, 0]

    def ordered(rep, sh):
        return [rep[0], rep[1], sh[0], rep[2], rep[3]] + list(sh[1:])

    return loss, (*ordered(r_grad, sh_grad), *ordered(r_delta, sh_delta), *ordered(r_m, sh_m), *ordered(r_v, sh_v))
```

```python
import functools

import jax
import jax.numpy as jnp
from jax import lax
from jax.experimental import pallas as pl
from jax.experimental.pallas import tpu as pltpu

F32 = jnp.float32
BF16 = jnp.bfloat16
MXU_DTYPE = jnp.bfloat16

D_MODEL = 1024
N_DEV = 8
RMS_EPS = 1e-6
NEG_INF = -1e30
A_GROUPS = ((128, 1), (512, 4), (2048, 16))
A_HEAD_DIM = 64
A_HALF = 64
ATT_T = 128
ATT_TILE = 2048
ATT_FWD_BATCH = 8
ATT_BWD_BATCH = 4
B_HEADS = 4
B_KEY_DIM = 128
B_VAL_DIM = 256
B_CHUNK = 64
B_GATE_TAU = 16.0
FFN_HIDDEN = 2816
FFN_BLK = 2 * FFN_HIDDEN // N_DEV
ADAM_LR, ADAM_B1, ADAM_B2, ADAM_EPS, ADAM_WD, ADAM_STEP = 0.001, 0.9, 0.999, 1e-08, 0.01, 10
ROPE_THETA = 10000.0

V7X_VMEM_LIMIT = 60 * 1024 * 1024
LANES = 128
MESH = pl.DeviceIdType.MESH
ANY = pl.BlockSpec(memory_space=pl.ANY)
VMEM_SPEC = pl.BlockSpec(memory_space=pltpu.VMEM)

NN = ((1,), (0,))
NT = ((1,), (1,))
TN = ((0,), (0,))


def _dot(a, b, dims):
    return lax.dot_general(a.astype(MXU_DTYPE), b.astype(MXU_DTYPE), (dims, ((), ())), preferred_element_type=F32)


@jax.custom_vjp
def dot_nn(a, b):
    return _dot(a, b, NN)


@jax.custom_vjp
def dot_nt(a, b):
    return _dot(a, b, NT)


@jax.custom_vjp
def dot_tn(a, b):
    return _dot(a, b, TN)


dot_nn.defvjp(lambda a, b: (_dot(a, b, NN), (a, b)), lambda r, g: (dot_nt(g, r[1]), dot_tn(r[0], g)))
dot_nt.defvjp(lambda a, b: (_dot(a, b, NT), (a, b)), lambda r, g: (dot_nn(g, r[1]), dot_tn(g, r[0])))
dot_tn.defvjp(lambda a, b: (_dot(a, b, TN), (a, b)), lambda r, g: (dot_nt(r[1], g), dot_nn(r[0], g)))


def _bdot(a, b, dims):
    dn = (tuple((d[0] + 1,) for d in dims), ((0,), (0,)))
    return lax.dot_general(a.astype(MXU_DTYPE), b.astype(MXU_DTYPE), dn, preferred_element_type=F32)


@jax.custom_vjp
def bdot_nn(a, b):
    return _bdot(a, b, NN)


@jax.custom_vjp
def bdot_nt(a, b):
    return _bdot(a, b, NT)


@jax.custom_vjp
def bdot_tn(a, b):
    return _bdot(a, b, TN)


bdot_nn.defvjp(lambda a, b: (_bdot(a, b, NN), (a, b)), lambda r, g: (bdot_nt(g, r[1]), bdot_tn(r[0], g)))
bdot_nt.defvjp(lambda a, b: (_bdot(a, b, NT), (a, b)), lambda r, g: (bdot_nn(g, r[1]), bdot_tn(g, r[0])))
bdot_tn.defvjp(lambda a, b: (_bdot(a, b, TN), (a, b)), lambda r, g: (bdot_nt(r[1], g), bdot_nn(r[0], g)))


def _dot_f32(a, b):
    return lax.dot_general(a, b, (NN, ((), ())), precision=lax.Precision.HIGHEST, preferred_element_type=F32)


def _tri(n, upper):
    r = lax.broadcasted_iota(jnp.int32, (n, n), 0)
    c = lax.broadcasted_iota(jnp.int32, (n, n), 1)
    return jnp.where((c >= r) if upper else (c <= r), 1.0, 0.0).astype(F32)


def _chunk_cumsum(x, reverse):
    tri = jnp.broadcast_to(_tri(x.shape[1], reverse), (x.shape[0], x.shape[1], x.shape[1]))
    return lax.dot_general(tri, x, (((2,), (1,)), ((0,), (0,))), precision=lax.Precision.HIGHEST, preferred_element_type=F32)


@functools.partial(jax.custom_vjp, nondiff_argnums=(1,))
def cumsum_chunks(x, reverse):
    return _chunk_cumsum(x, reverse)


cumsum_chunks.defvjp(lambda x, reverse: (_chunk_cumsum(x, reverse), None), lambda reverse, _, g: (_chunk_cumsum(g, not reverse),))


def _swap32_raw(x):
    lane = lax.broadcasted_iota(jnp.int32, x.shape, 1)
    return jnp.where((lane % 64) < 32, pltpu.roll(x, 96, 1), pltpu.roll(x, 32, 1))


@jax.custom_vjp
def swap32(x):
    return _swap32_raw(x)


swap32.defvjp(lambda x: (_swap32_raw(x), None), lambda _, g: (_swap32_raw(g),))


def _rms(x, gain):
    return x * lax.rsqrt(jnp.mean(x * x, axis=-1, keepdims=True) + RMS_EPS) * gain


def _sigmoid(x):
    return 1.0 / (1.0 + jnp.exp(-x))


def _log_sigmoid(x):
    return jnp.minimum(x, 0.0) - jnp.log(1.0 + jnp.exp(-jnp.abs(x)))


def _qk_prep(x, tab, gain):
    lo = lax.broadcasted_iota(jnp.int32, (1, LANES), 1) < A_HEAD_DIM
    x2 = x * x
    s_lo = jnp.sum(jnp.where(lo, x2, 0.0), axis=-1, keepdims=True)
    s_hi = jnp.sum(jnp.where(lo, 0.0, x2), axis=-1, keepdims=True)
    xn = (x * lax.rsqrt(jnp.where(lo, s_lo, s_hi) / A_HEAD_DIM + RMS_EPS)) * gain
    return xn * tab[:, :LANES] + swap32(xn) * tab[:, LANES:]


def _stack_heads(x):
    lo = lax.broadcasted_iota(jnp.int32, (1, LANES), 1) < A_HEAD_DIM
    return jnp.concatenate([jnp.where(lo, x, 0.0), jnp.where(lo, 0.0, x)], axis=-2)


def _unstack_heads(x):
    t = x.shape[-2] // 2
    lo = lax.broadcasted_iota(jnp.int32, (1, LANES), 1) < A_HEAD_DIM
    return jnp.where(lo, x[..., :t, :], x[..., t:, :])


def _attn_scores(q, k, valid):
    s = _bdot(_stack_heads(q), k, NT) * (A_HEAD_DIM ** -0.5)
    return jnp.where(valid, s, NEG_INF)


def _attn_job(q, k, v, valid):
    s = _attn_scores(q, k, valid)
    mx = jnp.max(s, axis=-1, keepdims=True)
    p = jnp.exp(s - mx)
    l = jnp.sum(p, axis=-1, keepdims=True)
    out = _unstack_heads(_bdot(p, v, NN) / l)
    lse = mx + jnp.log(l)
    t = q.shape[1]
    lo = lax.broadcasted_iota(jnp.int32, (1, LANES), 1) < A_HEAD_DIM
    return out, jnp.where(lo, lse[:, :t], lse[:, t:])


def _attn_job_bwd(q, k, v, valid, d_out, out, lse, d_lse):
    qs = _stack_heads(q)
    lo = lax.broadcasted_iota(jnp.int32, (1, LANES), 1) < A_HEAD_DIM
    other = pltpu.roll(lse, A_HEAD_DIM, 2)
    row_lse = jnp.concatenate([jnp.where(lo, lse, other), jnp.where(lo, other, lse)], axis=1)
    p = jnp.exp(_attn_scores(q, k, valid) - jnp.concatenate([row_lse, row_lse], axis=2))
    dos = _stack_heads(d_out)
    dv = _bdot(p, dos, TN)
    dp = _bdot(dos, v, NT)
    inner = jnp.sum(dos * _stack_heads(out) - _stack_heads(d_lse), axis=-1, keepdims=True)
    ds = p * (dp - inner) * (A_HEAD_DIM ** -0.5)
    return _unstack_heads(_bdot(ds, k, NN)), _bdot(ds, qs, TN), dv


def _merge_groups(o0, o1, o2, l0, l1, l2):
    mx = lax.stop_gradient(jnp.maximum(jnp.maximum(l0, l1), l2))
    e0, e1, e2 = jnp.exp(l0 - mx), jnp.exp(l1 - mx), jnp.exp(l2 - mx)
    den = e0 + e1 + e2
    return (e0 / den) * o0 + (e1 / den) * o1 + (e2 / den) * o2


def _gla_tile(q, k, v, la, st, reverse):
    t = q.shape[0]
    nc = t // B_CHUNK
    split = lambda x: x.reshape(nc, B_CHUNK, x.shape[1])
    q, k, v, la = split(q * (B_KEY_DIM ** -0.5)), split(k), split(v), split(la)
    r = lax.broadcasted_iota(jnp.int32, (1, B_CHUNK, B_CHUNK), 1)
    c = lax.broadcasted_iota(jnp.int32, (1, B_CHUNK, B_CHUNK), 2)
    mask = (c > r) if reverse else (c <= r)
    b = cumsum_chunks(la, reverse)
    tot = jnp.sum(la, axis=1, keepdims=True)
    q_t = q * jnp.exp(b)
    k_t = k * jnp.exp(-b)
    k_end = k * jnp.exp(tot - b)
    attn = jnp.where(mask, bdot_nt(q_t, k_t), 0.0)
    kv = bdot_tn(v, k_end)
    decay = jnp.exp(tot)
    entering = [None] * nc
    for ci in (range(nc - 1, -1, -1) if reverse else range(nc)):
        entering[ci] = st
        st = st * decay[ci] + kv[ci]
    out = bdot_nn(attn, v) + bdot_nt(q_t, jnp.concatenate([e[None] for e in entering], axis=0))
    return out.reshape(t, out.shape[2]), st


def _gla_post(o_f, o_b, r, gain):
    o = o_f + o_b
    heads = [_rms(o[:, h * B_VAL_DIM:(h + 1) * B_VAL_DIM], gain[:, h * B_VAL_DIM:(h + 1) * B_VAL_DIM]) for h in range(B_HEADS)]
    return jnp.concatenate(heads, axis=1) * (r * _sigmoid(r))


def _gate(z, wg, bias):
    return _log_sigmoid(dot_nn(z, wg) + bias) / B_GATE_TAU


def _swiglu_act(g, u):
    return (g * _sigmoid(g)) * u


def _adamw(w, g, m, v):
    m = ADAM_B1 * m + (1.0 - ADAM_B1) * g
    v = ADAM_B2 * v + (1.0 - ADAM_B2) * jnp.square(g)
    m_hat = m / (1.0 - ADAM_B1 ** ADAM_STEP)
    v_hat = v / (1.0 - ADAM_B2 ** ADAM_STEP)
    delta = -ADAM_LR * (m_hat / (jnp.sqrt(v_hat) + ADAM_EPS) + ADAM_WD * w)
    return delta, m, v


def _params(sem=None):
    return pltpu.CompilerParams(dimension_semantics=sem, vmem_limit_bytes=V7X_VMEM_LIMIT)


def _row_tile(s, want=512):
    t = min(want, s)
    assert s % t == 0
    return t


def _matmul(name, a, b, *, dims, grid, a_spec, b_spec, o_spec, out_shape, red_axis=None, res=None, res_spec=None, acc_shape=None,
            norm_gain=None):
    n_red = grid[red_axis] if red_axis is not None else 1
    n_in = 2 + (res is not None) + (norm_gain is not None)

    def body(*refs):
        a_ref, b_ref = refs[0], refs[1]
        r_ref = refs[2] if res is not None else None
        g_ref = refs[n_in - 1] if norm_gain is not None else None
        o_ref = refs[n_in]
        n_ref = refs[n_in + 1] if norm_gain is not None else None
        prod = lax.dot_general(a_ref[...].astype(MXU_DTYPE), b_ref[...].astype(MXU_DTYPE), (dims, ((), ())),
                               preferred_element_type=F32)
        if red_axis is None:
            if r_ref is not None:
                prod = prod + r_ref[...]
            o_ref[...] = prod.astype(o_ref.dtype)
            if n_ref is not None:
                n_ref[...] = _rms(prod, g_ref[...]).astype(n_ref.dtype)
            return
        acc = refs[-1] if acc_shape is not None else o_ref
        k = pl.program_id(red_axis)

        @pl.when(k == 0)
        def _():
            acc[...] = prod + r_ref[...] if r_ref is not None else prod

        @pl.when(k > 0)
        def _():
            acc[...] += prod

        if acc_shape is not None or n_ref is not None:
            @pl.when(k == n_red - 1)
            def _():
                if acc_shape is not None:
                    o_ref[...] = acc[...].astype(o_ref.dtype)
                if n_ref is not None:
                    n_ref[...] = _rms(acc[...], g_ref[...]).astype(n_ref.dtype)

    ins = [a, b] + ([res] if res is not None else []) + ([norm_gain] if norm_gain is not None else [])
    specs = [a_spec, b_spec] + ([res_spec] if res is not None else [])
    out_specs, out_shapes = o_spec, out_shape
    if norm_gain is not None:
        specs.append(pl.BlockSpec(norm_gain.shape, lambda *g: (0, 0)))
        out_specs, out_shapes = [o_spec, o_spec], [out_shape, jax.ShapeDtypeStruct(out_shape.shape, BF16)]
    sem = tuple("arbitrary" if i == red_axis else "parallel" for i in range(len(grid)))
    return pl.pallas_call(body, name=name, grid=grid, in_specs=specs, out_specs=out_specs, out_shape=out_shapes,
                          scratch_shapes=[pltpu.VMEM(acc_shape, F32)] if acc_shape is not None else [],
                          compiler_params=_params(sem))(*ins)


def mm_nn(name, x, w, *, res=None, out_dtype=F32, tn=1024, norm_gain=None):
    m, k = x.shape
    n = w.shape[1]
    tm, tn = _row_tile(m, 1024), min(tn, n)
    return _matmul(name, x, w, dims=NN, grid=(n // tn, m // tm),
                   a_spec=pl.BlockSpec((tm, k), lambda j, i: (i, 0)), b_spec=pl.BlockSpec((k, tn), lambda j, i: (0, j)),
                   o_spec=pl.BlockSpec((tm, tn), lambda j, i: (i, j)), out_shape=jax.ShapeDtypeStruct((m, n), out_dtype),
                   res=res, res_spec=pl.BlockSpec((tm, tn), lambda j, i: (i, j)), norm_gain=norm_gain)


def mm_nt(name, dy, w, *, res=None, tn=2304):
    m, n = dy.shape
    k = w.shape[0]
    tm, tn = _row_tile(m, 1024), (tn if n % tn == 0 else min(1024, n))
    return _matmul(name, dy, w, dims=NT, grid=(m // tm, n // tn), red_axis=1,
                   a_spec=pl.BlockSpec((tm, tn), lambda i, j: (i, j)), b_spec=pl.BlockSpec((k, tn), lambda i, j: (0, j)),
                   o_spec=pl.BlockSpec((tm, k), lambda i, j: (i, 0)), out_shape=jax.ShapeDtypeStruct((m, k), F32),
                   res=res, res_spec=pl.BlockSpec((tm, k), lambda i, j: (i, 0)))


def mm_tn(name, x, dy, *, tn=1024, by_block=False):
    m, k = x.shape
    n = dy.shape[1]
    tm, tn = _row_tile(m, 2048), min(tn, n)
    if by_block:
        o_spec, out_shape, acc = pl.BlockSpec((None, k, tn), lambda j, i: (j, 0, 0)), jax.ShapeDtypeStruct((n // tn, k, tn), BF16), (k, tn)
    else:
        o_spec, out_shape, acc = pl.BlockSpec((k, tn), lambda j, i: (0, j)), jax.ShapeDtypeStruct((k, n), F32), None
    return _matmul(name, x, dy, dims=TN, grid=(n // tn, m // tm), red_axis=1,
                   a_spec=pl.BlockSpec((tm, k), lambda j, i: (i, 0)), b_spec=pl.BlockSpec((tm, tn), lambda j, i: (i, j)),
                   o_spec=o_spec, out_shape=out_shape, acc_shape=acc)


def _rows_call(name, body, ins, outs, s, tile):
    in_specs = []
    for a, kind in ins:
        if kind == "row":
            in_specs.append(pl.BlockSpec((tile, a.shape[1]), lambda i: (i, 0)))
        else:
            in_specs.append(pl.BlockSpec(a.shape, lambda i, nd=a.ndim: (0,) * nd))
    out_specs, out_shape = [], []
    for cols, dt, kind in outs:
        if kind == "row":
            out_specs.append(pl.BlockSpec((tile, cols), lambda i: (i, 0)))
            out_shape.append(jax.ShapeDtypeStruct((s, cols), dt))
        else:
            out_specs.append(pl.BlockSpec((8, cols), lambda i: (0, 0)))
            out_shape.append(jax.ShapeDtypeStruct((8, cols), dt))
    has_acc = any(kind == "acc" for _, _, kind in outs)
    return pl.pallas_call(body, name=name, grid=(s // tile,), in_specs=in_specs, out_specs=out_specs, out_shape=out_shape,
                          compiler_params=_params(("arbitrary",) if has_acc else ("parallel",)))(*[a for a, _ in ins])


def _acc_rows(ref, val):
    @pl.when(pl.program_id(0) == 0)
    def _():
        ref[...] = jnp.zeros_like(ref)

    ref[...] += jnp.broadcast_to(val, ref.shape)


def rmsnorm_fwd(name, h, gain):
    s = h.shape[0]

    def body(h_ref, g_ref, o_ref):
        o_ref[...] = _rms(h_ref[...], g_ref[...]).astype(o_ref.dtype)

    return _rows_call(name, body, [(h, "row"), (gain, "full")], [(D_MODEL, BF16, "row")], s, _row_tile(s))[0]


def rmsnorm_bwd(name, h, gain, d_hn, d_res):
    s = h.shape[0]

    def body(h_ref, g_ref, dy_ref, dr_ref, dh_ref, dg_ref):
        _, vjp = jax.vjp(_rms, h_ref[...], g_ref[...])
        dh, dg = vjp(dy_ref[...])
        dh_ref[...] = dh + dr_ref[...]
        _acc_rows(dg_ref, dg)

    return _rows_call(name, body, [(h, "row"), (gain, "full"), (d_hn, "row"), (d_res, "row")],
                      [(D_MODEL, F32, "row"), (D_MODEL, F32, "acc")], s, _row_tile(s))


def loss_and_grad(y, target):
    s = y.shape[0]
    tile = _row_tile(s)

    def body(y_ref, t_ref, dy_ref, acc_ref):
        diff = y_ref[...] - t_ref[...]
        dy_ref[...] = diff * (1.0 / D_MODEL)

        @pl.when(pl.program_id(0) == 0)
        def _():
            acc_ref[...] = jnp.zeros_like(acc_ref)

        acc_ref[...] += jnp.sum((diff * diff).reshape(tile // 8, 8, D_MODEL), axis=0)

    return _rows_call("loss_head", body, [(y, "row"), (target, "row")], [(D_MODEL, F32, "row"), (D_MODEL, F32, "acc")], s, tile)


def _rope_table(s):
    half = A_HEAD_DIM // 2
    inv_freq = ROPE_THETA ** (-jnp.arange(half, dtype=F32) / half)
    ang = jnp.arange(s).astype(F32)[:, None] * inv_freq[None, :]
    cos, sin = jnp.cos(ang), jnp.sin(ang)
    return jnp.concatenate([cos, cos, cos, cos, -sin, sin, -sin, sin], axis=1)


def _attn_geometry(s, dil):
    tile = min(ATT_TILE, s)
    halo = A_HALF * dil
    assert s % tile == 0 and tile % (ATT_T * dil) == 0 and tile % halo == 0
    return tile, halo, tile // (ATT_T * dil)


def _attn_in_specs(grp, s, tile, halo):
    hb, n_hb = tile // halo, s // halo
    cq, ck, cv = (24 * grp + 8 * t for t in range(3))

    def main(col, per_pair, width=LANES):
        return pl.BlockSpec((tile, width), lambda i, j: (i, col + per_pair * j))

    def prev(col, per_pair, width=LANES):
        return pl.BlockSpec((halo, width), lambda i, j: (jnp.maximum(i * hb - 1, 0), col + per_pair * j))

    def nxt(col, per_pair, width=LANES):
        return pl.BlockSpec((halo, width), lambda i, j: (jnp.minimum((i + 1) * hb, n_hb - 1), col + per_pair * j))

    return [main(cq, 1), prev(ck, 1), main(ck, 1), nxt(ck, 1), prev(cv, 1), main(cv, 1), nxt(cv, 1),
            prev(0, 0, 256), main(0, 0, 256), nxt(0, 0, 256)]


def _attn_valid(first_job, n_jobs, dil, tile_base, length):
    r = lax.broadcasted_iota(jnp.int32, (1, 2 * ATT_T, 1), 1)
    tq = jnp.where(r >= ATT_T, r - ATT_T, r)
    rel = lax.broadcasted_iota(jnp.int32, (1, 1, ATT_T + 2 * A_HALF), 2) - A_HALF
    job = first_job + lax.broadcasted_iota(jnp.int32, (n_jobs, 1, 1), 0)
    tk = tile_base + (job // dil) * ATT_T + rel
    return (jnp.abs(rel - tq) <= A_HALF) & (tk >= 0) & (tk < length)


def _jobs(dil, n_sub):
    return [u * ATT_T * dil + p for u in range(n_sub) for p in range(dil)]


def _gather_jobs(ref, starts, size, dil):
    return jnp.concatenate([ref[_rows(st, size, dil), :][None] for st in starts], axis=0)


def _rows(start, size, dil):
    return pl.ds(start, size, stride=dil) if dil > 1 else pl.ds(start, size)


def attn_fwd(name, qkv, tab, gq, gk, grp, dil):
    s = qkv.shape[0]
    tile, halo, n_sub = _attn_geometry(s, dil)
    length, per_tile = s // dil, tile // dil
    nk = ATT_T + 2 * A_HALF

    def body(q_ref, kp_ref, km_ref, kn_ref, vp_ref, vm_ref, vn_ref, tp_ref, tm_ref, tn_ref, gq_ref, gk_ref, o_ref, lse_ref,
             q_buf, k_buf, v_buf):
        i, pair = pl.program_id(0), pl.program_id(1)
        q_buf[...] = _qk_prep(q_ref[...], tm_ref[...], gq_ref[...])
        for ref, t_ref, lo, n in ((kp_ref, tp_ref, 0, halo), (km_ref, tm_ref, halo, tile), (kn_ref, tn_ref, halo + tile, halo)):
            k_buf[lo:lo + n, :] = _qk_prep(ref[...], t_ref[...], gk_ref[...])
        for ref, lo, n in ((vp_ref, 0, halo), (vm_ref, halo, tile), (vn_ref, halo + tile, halo)):
            v_buf[lo:lo + n, :] = ref[...]

        jobs = _jobs(dil, n_sub)
        for g0 in range(0, len(jobs), ATT_FWD_BATCH):
            starts = jobs[g0:g0 + ATT_FWD_BATCH]
            valid = _attn_valid(g0, len(starts), dil, i * per_tile, length)
            o, lse = _attn_job(_gather_jobs(q_buf, starts, ATT_T, dil), _gather_jobs(k_buf, starts, nk, dil),
                               _gather_jobs(v_buf, starts, nk, dil), valid)
            for n, st in enumerate(starts):
                o_ref[_rows(st, ATT_T, dil), :] = o[n]
                lse_ref[_rows(st, ATT_T, dil), :] = lse[n]

    full = lambda a: pl.BlockSpec(a.shape, lambda i, j: (0, 0))
    return pl.pallas_call(
        body, name=name, grid=(s // tile, D_MODEL // LANES),
        in_specs=_attn_in_specs(grp, s, tile, halo) + [full(gq), full(gk)],
        out_specs=[pl.BlockSpec((tile, LANES), lambda i, j: (i, j)), pl.BlockSpec((tile, LANES), lambda i, j: (i, j))],
        out_shape=[jax.ShapeDtypeStruct((s, D_MODEL), F32), jax.ShapeDtypeStruct((s, D_MODEL), F32)],
        scratch_shapes=[pltpu.VMEM((tile, LANES), F32), pltpu.VMEM((tile + 2 * halo, LANES), F32), pltpu.VMEM((tile + 2 * halo, LANES), F32)],
        compiler_params=_params(("parallel", "parallel")),
    )(qkv, qkv, qkv, qkv, qkv, qkv, qkv, tab, tab, tab, gq, gk)


def attn_bwd(name, qkv, tab, gq, gk, o, lse, d_o, d_lse, grp, dil, d_qkv):
    s = qkv.shape[0]
    tile, halo, n_sub = _attn_geometry(s, dil)
    length, per_tile = s // dil, tile // dil
    nt = s // tile
    nk = ATT_T + 2 * A_HALF
    pieces = ((0, halo), (halo, tile), (halo + tile, halo))

    def body(q_ref, kp_ref, km_ref, kn_ref, vp_ref, vm_ref, vn_ref, tp_ref, tm_ref, tn_ref, gq_ref, gk_ref, o_ref, l_ref, do_ref, dl_ref,
             _, dq_ref, dkp_ref, dkm_ref, dkn_ref, dvp_ref, dvm_ref, dvn_ref, dgq_ref, dgk_ref, q_buf, k_buf, v_buf, dq_buf, dk_buf, dv_buf):
        i, pair = pl.program_id(0), pl.program_id(1)
        k_refs, t_refs = (kp_ref, km_ref, kn_ref), (tp_ref, tm_ref, tn_ref)
        qn, q_vjp = jax.vjp(lambda x, g: _qk_prep(x, tm_ref[...], g), q_ref[...], gq_ref[...])
        q_buf[...] = qn
        k_vjps = []
        for ref, t_ref, (lo, n) in zip(k_refs, t_refs, pieces):
            kn, k_vjp = jax.vjp(lambda x, g: _qk_prep(x, t_ref[...], g), ref[...], gk_ref[...])
            k_buf[lo:lo + n, :] = kn
            k_vjps.append(k_vjp)
        for ref, (lo, n) in zip((vp_ref, vm_ref, vn_ref), pieces):
            v_buf[lo:lo + n, :] = ref[...]
        dk_buf[...] = jnp.zeros_like(dk_buf)
        dv_buf[...] = jnp.zeros_like(dv_buf)
        jobs = _jobs(dil, n_sub)
        for g0 in range(0, len(jobs), ATT_BWD_BATCH):
            starts = jobs[g0:g0 + ATT_BWD_BATCH]
            valid = _attn_valid(g0, len(starts), dil, i * per_tile, length)
            own = lambda ref: _gather_jobs(ref, starts, ATT_T, dil)
            dq, dk, dv = _attn_job_bwd(own(q_buf), _gather_jobs(k_buf, starts, nk, dil), _gather_jobs(v_buf, starts, nk, dil), valid,
                                       own(do_ref), own(o_ref), own(l_ref), own(dl_ref))
            for n, st in enumerate(starts):
                dq_buf[_rows(st, ATT_T, dil), :] = dq[n]
                dk_buf[_rows(st, nk, dil), :] += dk[n]
                dv_buf[_rows(st, nk, dil), :] += dv[n]
        dq, dgq = q_vjp(dq_buf[...])
        dq_ref[...] = dq.astype(dq_ref.dtype)
        dgk = jnp.zeros((1, LANES), F32)
        for k_vjp, out_ref, (lo, n) in zip(k_vjps, (dkp_ref, dkm_ref, dkn_ref), pieces):
            out_ref[...], dgk_piece = k_vjp(dk_buf[lo:lo + n, :])
            dgk = dgk + dgk_piece
        for out_ref, (lo, n) in zip((dvp_ref, dvm_ref, dvn_ref), pieces):
            out_ref[...] = dv_buf[lo:lo + n, :]

        @pl.when((i == 0) & (pair == 0))
        def _():
            dgq_ref[...] = jnp.zeros_like(dgq_ref)
            dgk_ref[...] = jnp.zeros_like(dgk_ref)

        dgq_ref[...] += jnp.broadcast_to(dgq + pltpu.roll(dgq, A_HEAD_DIM, 1), dgq_ref.shape)
        dgk_ref[...] += jnp.broadcast_to(dgk + pltpu.roll(dgk, A_HEAD_DIM, 1), dgk_ref.shape)

    full = lambda a: pl.BlockSpec(a.shape, lambda i, j: (0, 0))
    main_o = pl.BlockSpec((tile, LANES), lambda i, j: (i, j))
    edge_o = pl.BlockSpec((None, halo, LANES), lambda i, j: (i, 0, j))
    main_s = jax.ShapeDtypeStruct((s, D_MODEL), F32)
    edge_s = jax.ShapeDtypeStruct((nt, halo, D_MODEL), F32)
    acc_o = pl.BlockSpec((8, LANES), lambda i, j: (0, 0))
    acc_s = jax.ShapeDtypeStruct((8, LANES), F32)
    big = pltpu.VMEM((tile + 2 * halo, LANES), F32)
    own = pltpu.VMEM((tile, LANES), F32)
    outs = pl.pallas_call(
        body, name=name, grid=(nt, D_MODEL // LANES),
        in_specs=_attn_in_specs(grp, s, tile, halo) + [full(gq), full(gk), main_o, main_o, main_o, main_o, ANY],
        out_specs=[pl.BlockSpec((tile, LANES), lambda i, j: (i, 24 * grp + j)), edge_o, main_o, edge_o, edge_o, main_o, edge_o, acc_o, acc_o],
        out_shape=[jax.ShapeDtypeStruct(d_qkv.shape, d_qkv.dtype), edge_s, main_s, edge_s, edge_s, main_s, edge_s, acc_s, acc_s],
        scratch_shapes=[own, big, big, own, big, big],
        input_output_aliases={16: 0},
        compiler_params=_params(("arbitrary", "arbitrary")),
    )(qkv, qkv, qkv, qkv, qkv, qkv, qkv, tab, tab, tab, gq, gk, o, lse, d_o, d_lse, d_qkv)
    d_qkv, dkp, dkm, dkn, dvp, dvm, dvn, dgq, dgk = outs
    return d_qkv, (dkp, dkm, dkn), (dvp, dvm, dvn), dgq, dgk


def attn_combine(name, parts, s, dil, d_qkv, col):
    prev_part, main_part, next_part = parts
    tile, halo, _ = _attn_geometry(s, dil)
    nt = s // tile
    cols = D_MODEL // 2

    def body(m_ref, from_prev_ref, from_next_ref, _, o_ref):
        i = pl.program_id(0)
        o_ref[...] = m_ref[...].astype(o_ref.dtype)
        head = m_ref[0:halo, :] + jnp.where(i > 0, from_prev_ref[...], 0.0)
        o_ref[0:halo, :] = head.astype(o_ref.dtype)
        tail = m_ref[tile - halo:tile, :] + jnp.where(i < nt - 1, from_next_ref[...], 0.0)
        o_ref[tile - halo:tile, :] = tail.astype(o_ref.dtype)

    return pl.pallas_call(
        body, name=name, grid=(nt, D_MODEL // cols),
        in_specs=[pl.BlockSpec((tile, cols), lambda i, c: (i, c)),
                  pl.BlockSpec((None, halo, cols), lambda i, c: (jnp.maximum(i - 1, 0), 0, c)),
                  pl.BlockSpec((None, halo, cols), lambda i, c: (jnp.minimum(i + 1, nt - 1), 0, c)), ANY],
        out_specs=pl.BlockSpec((tile, cols), lambda i, c: (i, (D_MODEL // cols) * col + c)),
        out_shape=jax.ShapeDtypeStruct(d_qkv.shape, d_qkv.dtype), input_output_aliases={3: 0},
        compiler_params=_params(("parallel", "parallel")),
    )(main_part, next_part, prev_part, d_qkv)


def attn_merge_fwd(name, os_, lses):
    s = os_[0].shape[0]

    def body(o0, o1, o2, l0, l1, l2, out_ref):
        out_ref[...] = _merge_groups(o0[...], o1[...], o2[...], l0[...], l1[...], l2[...]).astype(out_ref.dtype)

    return _rows_call(name, body, [(a, "row") for a in (*os_, *lses)], [(D_MODEL, BF16, "row")], s, _row_tile(s, 256))[0]


def attn_merge_bwd(name, os_, lses, d_out):
    s = os_[0].shape[0]

    def body(o0, o1, o2, l0, l1, l2, d_ref, *outs):
        _, vjp = jax.vjp(_merge_groups, o0[...], o1[...], o2[...], l0[...], l1[...], l2[...])
        for ref, val in zip(outs, vjp(d_ref[...])):
            ref[...] = val

    return _rows_call(name, body, [(a, "row") for a in (*os_, *lses, d_out)],
                      [(D_MODEL, F32, "row")] * 6, s, _row_tile(s, 256))


def gla_gate_fwd(name, z, wg, bias):
    s = z.shape[0]

    def body(z_ref, w_ref, b_ref, o_ref):
        o_ref[...] = _gate(z_ref[...], w_ref[...], b_ref[...])

    return _rows_call(name, body, [(z, "row"), (wg, "full"), (bias, "full")], [(D_MODEL, F32, "row")], s, _row_tile(s))[0]


def gla_gate_bwd(name, z, wg, bias, d_la_f, d_la_b):
    s = z.shape[0]
    tile = _row_tile(s)

    def body(z_ref, w_ref, b_ref, df_ref, db_ref, dz_ref, dw_ref, dbias_ref):
        _, vjp = jax.vjp(_gate, z_ref[...], w_ref[...], b_ref[...])
        dz, dw, dbias = vjp(jnp.concatenate([df_ref[...], db_ref[...]], axis=1))
        dz_ref[...] = dz

        @pl.when(pl.program_id(0) == 0)
        def _():
            dw_ref[...] = jnp.zeros_like(dw_ref)

        dw_ref[...] += dw
        _acc_rows(dbias_ref, dbias)

    return pl.pallas_call(
        body, name=name, grid=(s // tile,),
        in_specs=[pl.BlockSpec((tile, LANES), lambda i: (i, 0)), pl.BlockSpec(wg.shape, lambda i: (0, 0)),
                  pl.BlockSpec(bias.shape, lambda i: (0, 0)), pl.BlockSpec((tile, 512), lambda i: (i, 0)),
                  pl.BlockSpec((tile, 512), lambda i: (i, 0))],
        out_specs=[pl.BlockSpec((tile, LANES), lambda i: (i, 0)), pl.BlockSpec(wg.shape, lambda i: (0, 0)),
                   pl.BlockSpec((8, D_MODEL), lambda i: (0, 0))],
        out_shape=[jax.ShapeDtypeStruct((s, LANES), F32), jax.ShapeDtypeStruct(wg.shape, F32), jax.ShapeDtypeStruct((8, D_MODEL), F32)],
        compiler_params=_params(("arbitrary",)),
    )(z, wg, bias, d_la_f, d_la_b)


def _gla_in_specs(tile, order, la_col0):
    t = order
    return [pl.BlockSpec((tile, B_KEY_DIM), lambda h, n: (t(n), h)),
            pl.BlockSpec((tile, B_KEY_DIM), lambda h, n: (t(n), B_HEADS + h)),
            pl.BlockSpec((tile, B_VAL_DIM), lambda h, n: (t(n), B_HEADS + h)),
            pl.BlockSpec((tile, B_KEY_DIM), lambda h, n: (t(n), la_col0 + h))]


def gla_fwd(name, proj, la, reverse):
    s = proj.shape[0]
    tile = _row_tile(s)
    nt = s // tile
    order = (lambda n: nt - 1 - n) if reverse else (lambda n: n)

    def body(q_ref, k_ref, v_ref, la_ref, o_ref, st_ref, st_scr):
        @pl.when(pl.program_id(1) == 0)
        def _():
            st_scr[...] = jnp.zeros_like(st_scr)

        st_ref[...] = st_scr[...]
        o, st = _gla_tile(q_ref[...], k_ref[...], v_ref[...], la_ref[...], st_scr[...], reverse)
        o_ref[...] = o
        st_scr[...] = st

    return pl.pallas_call(
        body, name=name, grid=(B_HEADS, nt), in_specs=_gla_in_specs(tile, order, B_HEADS if reverse else 0),
        out_specs=[pl.BlockSpec((tile, B_VAL_DIM), lambda h, n: (order(n), h)),
                   pl.BlockSpec((None, None, B_VAL_DIM, B_KEY_DIM), lambda h, n: (h, order(n), 0, 0))],
        out_shape=[jax.ShapeDtypeStruct((s, D_MODEL), F32), jax.ShapeDtypeStruct((B_HEADS, nt, B_VAL_DIM, B_KEY_DIM), F32)],
        scratch_shapes=[pltpu.VMEM((B_VAL_DIM, B_KEY_DIM), F32)],
        compiler_params=_params(("parallel", "arbitrary")),
    )(proj, proj, proj, la)


def gla_bwd(name, proj, la, states, d_o, reverse, prev=None):
    s = proj.shape[0]
    tile = _row_tile(s)
    nt = s // tile
    order = (lambda n: n) if reverse else (lambda n: nt - 1 - n)

    def body(*refs):
        q_ref, k_ref, v_ref, la_ref, st_ref, do_ref = refs[:6]
        rest = refs[6:]
        prev_refs = rest[:3] if prev is not None else None
        dq_ref, dk_ref, dv_ref, dla_ref, dst_scr = rest[3:] if prev is not None else rest

        @pl.when(pl.program_id(1) == 0)
        def _():
            dst_scr[...] = jnp.zeros_like(dst_scr)

        _, vjp = jax.vjp(functools.partial(_gla_tile, reverse=reverse), q_ref[...], k_ref[...], v_ref[...], la_ref[...], st_ref[...])
        dq, dk, dv, dla, dst = vjp((do_ref[...], dst_scr[...]))
        if prev_refs is not None:
            dq, dk, dv = dq + prev_refs[0][...], dk + prev_refs[1][...], dv + prev_refs[2][...]
        dq_ref[...], dk_ref[...], dv_ref[...], dla_ref[...] = dq, dk, dv, dla
        dst_scr[...] = dst

    key_spec = pl.BlockSpec((tile, B_KEY_DIM), lambda h, n: (order(n), h))
    val_spec = pl.BlockSpec((tile, B_VAL_DIM), lambda h, n: (order(n), h))
    in_specs = _gla_in_specs(tile, order, B_HEADS if reverse else 0) + [
        pl.BlockSpec((None, None, B_VAL_DIM, B_KEY_DIM), lambda h, n: (h, order(n), 0, 0)), val_spec]
    ins = [proj, proj, proj, la, states, d_o]
    if prev is not None:
        in_specs += [key_spec, key_spec, val_spec]
        ins += list(prev)
    return pl.pallas_call(
        body, name=name, grid=(B_HEADS, nt), in_specs=in_specs,
        out_specs=[key_spec, key_spec, val_spec, key_spec],
        out_shape=[jax.ShapeDtypeStruct((s, 512), F32), jax.ShapeDtypeStruct((s, 512), F32),
                   jax.ShapeDtypeStruct((s, D_MODEL), F32), jax.ShapeDtypeStruct((s, 512), F32)],
        scratch_shapes=[pltpu.VMEM((B_VAL_DIM, B_KEY_DIM), F32)],
        compiler_params=_params(("parallel", "arbitrary")),
    )(*ins)


def _r_spec(tile):
    return pl.BlockSpec((tile, D_MODEL), lambda i: (i, 2))


def gla_post_fwd(name, o_f, o_b, proj, gain):
    s = o_f.shape[0]
    tile = _row_tile(s)

    def body(of_ref, ob_ref, r_ref, g_ref, out_ref):
        out_ref[...] = _gla_post(of_ref[...], ob_ref[...], r_ref[...], g_ref[...]).astype(out_ref.dtype)

    row = pl.BlockSpec((tile, D_MODEL), lambda i: (i, 0))
    return pl.pallas_call(body, name=name, grid=(s // tile,),
                          in_specs=[row, row, _r_spec(tile), pl.BlockSpec(gain.shape, lambda i: (0, 0))], out_specs=row,
                          out_shape=jax.ShapeDtypeStruct((s, D_MODEL), BF16), compiler_params=_params(("parallel",)))(o_f, o_b, proj, gain)


def gla_post_bwd(name, o_f, o_b, proj, gain, d_out):
    s = o_f.shape[0]
    tile = _row_tile(s)

    def body(of_ref, ob_ref, r_ref, g_ref, d_ref, do_ref, dr_ref, dg_ref):
        _, vjp = jax.vjp(_gla_post, of_ref[...], ob_ref[...], r_ref[...], g_ref[...])
        d_of, _, dr, dg = vjp(d_ref[...])
        do_ref[...] = d_of
        dr_ref[...] = dr
        _acc_rows(dg_ref, dg)

    row = pl.BlockSpec((tile, D_MODEL), lambda i: (i, 0))
    return pl.pallas_call(
        body, name=name, grid=(s // tile,),
        in_specs=[row, row, _r_spec(tile), pl.BlockSpec(gain.shape, lambda i: (0, 0)), row],
        out_specs=[row, row, pl.BlockSpec((8, D_MODEL), lambda i: (0, 0))],
        out_shape=[jax.ShapeDtypeStruct((s, D_MODEL), F32), jax.ShapeDtypeStruct((s, D_MODEL), F32), jax.ShapeDtypeStruct((8, D_MODEL), F32)],
        compiler_params=_params(("arbitrary",)))(o_f, o_b, proj, gain, d_out)


def _hid(tile, where):
    return pl.BlockSpec((None, tile, FFN_BLK), where)


def _pair(tile, where):
    return pl.BlockSpec((2, None, tile, FFN_BLK), where)


def _w_gu_spec(layer, where_j):
    return pl.BlockSpec((2, None, None, D_MODEL, FFN_BLK), lambda *g: (0, where_j(*g), layer, 0, 0))


def ffn_fwd(name, h_mid, hn2, w_gu, w_down, layer, next_gain=None):
    s = hn2.shape[0]
    tm = _row_tile(s, 1024)
    nt = s // tm

    def gu_body(x_ref, w_ref, gu_ref, act_ref):
        x = x_ref[...]
        g = _dot(x, w_ref[0], NN)
        u = _dot(x, w_ref[1], NN)
        gu_ref[0] = g
        gu_ref[1] = u
        act_ref[...] = _swiglu_act(g, u).astype(act_ref.dtype)

    gu, act = pl.pallas_call(
        gu_body, name=name + "_gu", grid=(4, nt),
        in_specs=[pl.BlockSpec((tm, D_MODEL), lambda j, i: (i, 0)), _w_gu_spec(layer, lambda j, i: j)],
        out_specs=[_pair(tm, lambda j, i: (0, j, i, 0)), _hid(tm, lambda j, i: (j, i, 0))],
        out_shape=[jax.ShapeDtypeStruct((2, 4, s, FFN_BLK), F32), jax.ShapeDtypeStruct((4, s, FFN_BLK), BF16)],
        compiler_params=_params(("parallel", "parallel")))(hn2, w_gu)
    row = pl.BlockSpec((tm, D_MODEL), lambda i, j: (i, 0))
    h_next = _matmul(name + "_down", act, w_down, dims=NN, grid=(nt, 4), red_axis=1,
                     a_spec=_hid(tm, lambda i, j: (j, i, 0)), b_spec=pl.BlockSpec((None, FFN_BLK, D_MODEL), lambda i, j: (j, 0, 0)),
                     o_spec=row, out_shape=jax.ShapeDtypeStruct((s, D_MODEL), F32), res=h_mid, res_spec=row, norm_gain=next_gain)
    return h_next, gu, act


def ffn_bwd(name, dh_next, hn2, gu, act, w_gu, w_down, layer):
    s = hn2.shape[0]
    tm = _row_tile(s, 1024)
    nt = s // tm
    tw = _row_tile(s, 2048)
    ntw = s // tw
    d_wd = _matmul(name + "_dwd", act, dh_next, dims=TN, grid=(4, ntw), red_axis=1,
                   a_spec=_hid(tw, lambda j, i: (j, i, 0)), b_spec=pl.BlockSpec((tw, D_MODEL), lambda j, i: (i, 0)),
                   o_spec=pl.BlockSpec((None, FFN_BLK, D_MODEL), lambda j, i: (j, 0, 0)),
                   out_shape=jax.ShapeDtypeStruct((4, FFN_BLK, D_MODEL), BF16), acc_shape=(FFN_BLK, D_MODEL))

    def dgu_body(dy_ref, wd_ref, gu_ref, dgu_ref):
        d_act = _dot(dy_ref[...], wd_ref[...], NT)
        g, u = gu_ref[0], gu_ref[1]
        sg = _sigmoid(g)
        silu = g * sg
        dgu_ref[0] = (d_act * u * (sg + silu * (1.0 - sg))).astype(dgu_ref.dtype)
        dgu_ref[1] = (d_act * silu).astype(dgu_ref.dtype)

    d_gu = pl.pallas_call(
        dgu_body, name=name + "_dgu", grid=(4, nt),
        in_specs=[pl.BlockSpec((tm, D_MODEL), lambda j, i: (i, 0)), pl.BlockSpec((None, FFN_BLK, D_MODEL), lambda j, i: (j, 0, 0)),
                  _pair(tm, lambda j, i: (0, j, i, 0))],
        out_specs=_pair(tm, lambda j, i: (0, j, i, 0)), out_shape=jax.ShapeDtypeStruct((2, 4, s, FFN_BLK), BF16),
        compiler_params=_params(("parallel", "parallel")))(dh_next, w_down, gu)

    def dx_body(d_ref, w_ref, o_ref):
        prod = _dot(d_ref[0], w_ref[0], NT) + _dot(d_ref[1], w_ref[1], NT)

        @pl.when(pl.program_id(1) == 0)
        def _():
            o_ref[...] = prod

        @pl.when(pl.program_id(1) > 0)
        def _():
            o_ref[...] += prod

    d_hn2 = pl.pallas_call(
        dx_body, name=name + "_dx", grid=(nt, 4),
        in_specs=[_pair(tm, lambda i, j: (0, j, i, 0)), _w_gu_spec(layer, lambda i, j: j)],
        out_specs=pl.BlockSpec((tm, D_MODEL), lambda i, j: (i, 0)), out_shape=jax.ShapeDtypeStruct((s, D_MODEL), F32),
        compiler_params=_params(("parallel", "arbitrary")))(d_gu, w_gu)

    def dw_body(x_ref, d_ref, o_ref, acc_ref):
        x = x_ref[...]
        k = pl.program_id(1)
        for t in range(2):
            prod = _dot(x, d_ref[t], TN)

            @pl.when(k == 0)
            def _():
                acc_ref[t] = prod

            @pl.when(k > 0)
            def _():
                acc_ref[t] += prod

        @pl.when(k == ntw - 1)
        def _():
            o_ref[...] = acc_ref[...].astype(o_ref.dtype)

    d_wgu = pl.pallas_call(
        dw_body, name=name + "_dwgu", grid=(4, ntw),
        in_specs=[pl.BlockSpec((tw, D_MODEL), lambda j, i: (i, 0)), _pair(tw, lambda j, i: (0, j, i, 0))],
        out_specs=pl.BlockSpec((2, None, D_MODEL, FFN_BLK), lambda j, i: (0, j, 0, 0)),
        out_shape=jax.ShapeDtypeStruct((2, 4, D_MODEL, FFN_BLK), BF16),
        scratch_shapes=[pltpu.VMEM((2, D_MODEL, FFN_BLK), F32)],
        compiler_params=_params(("parallel", "arbitrary")))(hn2, d_gu)
    return d_hn2, d_wgu, d_wd


def _my_place():
    return lax.axis_index("x"), lax.axis_index("y"), lax.axis_index("c")


def _flip(place, k):
    x, y, c = place
    return (1 - x if k & 4 else x, 1 - y if k & 2 else y, 1 - c if k & 1 else c)


def _index(place):
    return 4 * place[0] + 2 * place[1] + place[2]


def all_gather(arrs):
    n = len(arrs)

    def body(*refs):
        ins, outs = refs[:n], refs[n:2 * n]
        send_sems, recv_sems, local_sems = refs[2 * n:]
        me = _my_place()
        sibling = _flip(me, 1)
        chips = (4, 2, 6)

        def copy(a, k, block, to, src=None):
            dst = outs[a].at[_index(block)]
            return pltpu.make_async_remote_copy(src_ref=dst if src is None else src, dst_ref=dst, send_sem=send_sems.at[a, k],
                                                recv_sem=recv_sems.at[a, k], device_id=to, device_id_type=MESH)

        started = []
        for a in range(n):
            mine = pltpu.make_async_copy(ins[a], outs[a].at[_index(me)], local_sems.at[a])
            mine.start()
            started.append(mine)
        first = []
        for a in range(n):
            first.append(copy(a, 0, me, sibling, src=ins[a]))
            first += [copy(a, 1 + j, me, _flip(me, k), src=ins[a]) for j, k in enumerate(chips)]
        for cp in first:
            cp.start()
        passed = []
        for a in range(n):
            for j, k in enumerate(chips):
                copy(a, 1 + j, _flip(me, k), me).wait_recv()
                fwd = copy(a, 4 + j, _flip(me, k), sibling)
                fwd.start()
                passed.append(fwd)
        for a in range(n):
            copy(a, 0, sibling, me).wait_recv()
            for j, k in enumerate(chips):
                copy(a, 4 + j, _flip(sibling, k), me).wait_recv()
        for cp in first + passed:
            cp.wait_send()
        for cp in started:
            cp.wait()

    return pl.pallas_call(
        body, name="all_gather_weights", in_specs=[ANY] * n, out_specs=[ANY] * n,
        out_shape=[jax.ShapeDtypeStruct((N_DEV,) + a.shape, a.dtype) for a in arrs],
        scratch_shapes=[pltpu.SemaphoreType.DMA((n, 7)), pltpu.SemaphoreType.DMA((n, 7)), pltpu.SemaphoreType.DMA((n,))],
    )(*arrs)


def exchange_partials(arrs):
    n = len(arrs)

    def body(*refs):
        ins, outs = refs[:n], refs[n:2 * n]
        send_sems, recv_sems, local_sems = refs[2 * n:]
        me = _my_place()
        local = []
        for a in range(n):
            cp = pltpu.make_async_copy(ins[a].at[_index(me)], outs[a].at[_index(me)], local_sems.at[a])
            cp.start()
            local.append(cp)

        def copy(a, k, src_block, dst_block):
            return pltpu.make_async_remote_copy(src_ref=ins[a].at[_index(src_block)], dst_ref=outs[a].at[_index(dst_block)],
                                                send_sem=send_sems.at[a, k - 1], recv_sem=recv_sems.at[a, k - 1],
                                                device_id=_flip(me, k), device_id_type=MESH)

        sent = []
        for a in range(n):
            for k in range(1, N_DEV):
                cp = copy(a, k, _flip(me, k), me)
                cp.start()
                sent.append(cp)
        for a in range(n):
            for k in range(1, N_DEV):
                copy(a, k, me, _flip(me, k)).wait_recv()
        for cp in sent:
            cp.wait_send()
        for cp in local:
            cp.wait()

    return pl.pallas_call(
        body, name="exchange_weight_grads", in_specs=[ANY] * n, out_specs=[ANY] * n,
        out_shape=[jax.ShapeDtypeStruct(a.shape, a.dtype) for a in arrs],
        scratch_shapes=[pltpu.SemaphoreType.DMA((n, 7)), pltpu.SemaphoreType.DMA((n, 7)), pltpu.SemaphoreType.DMA((n,))],
    )(*arrs)


def adamw_shard(name, parts, w, m, v, layer, tile):
    rows, cols = parts.shape[1:]
    assert rows % tile == 0
    off = layer * (rows // tile)

    def body(p_ref, w_ref, m_ref, v_ref, g_ref, d_ref, nm_ref, nv_ref):
        g = p_ref[0].astype(F32)
        for src in range(1, N_DEV):
            g = g + p_ref[src].astype(F32)
        g_ref[...] = g
        d_ref[...], nm_ref[...], nv_ref[...] = _adamw(w_ref[...], g, m_ref[...], v_ref[...])

    src_row = pl.BlockSpec((tile, cols), lambda i: (off + i, 0))
    row = pl.BlockSpec((tile, cols), lambda i: (i, 0))
    shape = jax.ShapeDtypeStruct((rows, cols), F32)
    return pl.pallas_call(body, name=name, grid=(rows // tile,),
                          in_specs=[pl.BlockSpec((N_DEV, tile, cols), lambda i: (0, i, 0)), src_row, src_row, src_row],
                          out_specs=[row] * 4, out_shape=[shape] * 4, compiler_params=_params(("parallel",)))(parts, w, m, v)


def allreduce_adamw_replicated(partial, w, m, v, n_loss_rows):
    rows = partial.shape[0]

    def body(p_ref, w_ref, m_ref, v_ref, g_ref, d_ref, nm_ref, nv_ref, loss_ref, recv_ref, send_sems, recv_sems):
        me = _my_place()
        recv_ref[_index(me)] = p_ref[...]
        copies = []
        for k in range(1, N_DEV):
            peer = _flip(me, k)
            cp = pltpu.make_async_remote_copy(src_ref=p_ref, dst_ref=recv_ref.at[_index(me)], send_sem=send_sems.at[k - 1],
                                              recv_sem=recv_sems.at[k - 1], device_id=peer, device_id_type=MESH)
            cp.start()
            copies.append((cp, peer))
        for k, (cp, peer) in enumerate(copies):
            pltpu.make_async_remote_copy(src_ref=p_ref, dst_ref=recv_ref.at[_index(peer)], send_sem=send_sems.at[k],
                                         recv_sem=recv_sems.at[k], device_id=peer, device_id_type=MESH).wait_recv()
        for cp, _ in copies:
            cp.wait_send()
        g = recv_ref[0]
        for src in range(1, N_DEV):
            g = g + recv_ref[src]
        g_ref[...] = g
        d_ref[...], nm_ref[...], nv_ref[...] = _adamw(w_ref[...], g, m_ref[...], v_ref[...])
        loss = (0.5 / D_MODEL) * jnp.sum(g[rows - n_loss_rows:, :])
        loss_ref[...] = jnp.full(loss_ref.shape, loss, F32)

    shape = jax.ShapeDtypeStruct((rows, LANES), F32)
    return pl.pallas_call(
        body, name="allreduce_adamw_replicated", in_specs=[VMEM_SPEC] * 4, out_specs=[VMEM_SPEC] * 5,
        out_shape=[shape] * 4 + [jax.ShapeDtypeStruct((8, LANES), F32)],
        scratch_shapes=[pltpu.VMEM((N_DEV, rows, LANES), F32), pltpu.SemaphoreType.DMA((7,)), pltpu.SemaphoreType.DMA((7,))],
    )(partial, w, m, v)


def _pack_rows(flat, cols):
    n = flat.shape[-1]
    rows = -(-n // cols)
    rows = -(-rows // 48) * 48
    flat = jnp.pad(flat, [(0, 0)] * (flat.ndim - 1) + [(0, rows * cols - n)])
    return flat.reshape(flat.shape[:-1] + (rows, cols))


def _unpack(flat, shapes):
    out, off = [], 0
    for shp in shapes:
        n = 1
        for d in shp:
            n *= d
        out.append(flat[off:off + n].reshape(shp))
        off += n
    return out


def _to_dev_cols(a):
    w = a.shape[-1] // N_DEV
    return jnp.moveaxis(a.reshape(a.shape[:-1] + (N_DEV, w)), -2, 0)


def _from_dev_cols(a):
    a = jnp.moveaxis(a, 0, -2)
    return a.reshape(a.shape[:-2] + (a.shape[-2] * a.shape[-1],))


def kernel(x, attn_norm, ffn_norm, a_w_in, a_q_norm, a_k_norm, a_w_out, b_w_in, b_w_gate_f, b_gate_bias_f, b_w_gate_b, b_gate_bias_b, b_out_norm, b_w_out, ffn_w_gate_up, ffn_w_down, loss_target, m_attn_norm, m_ffn_norm, m_a_w_in, m_a_q_norm, m_a_k_norm, m_a_w_out, m_b_w_in, m_b_w_gate_f, m_b_gate_bias_f, m_b_w_gate_b, m_b_gate_bias_b, m_b_out_norm, m_b_w_out, m_ffn_w_gate_up, m_ffn_w_down, v_attn_norm, v_ffn_norm, v_a_w_in, v_a_q_norm, v_a_k_norm, v_a_w_out, v_b_w_in, v_b_w_gate_f, v_b_gate_bias_f, v_b_w_gate_b, v_b_gate_bias_b, v_b_out_norm, v_b_w_out, v_ffn_w_gate_up, v_ffn_w_down):
    seq = x.shape[1]
    depth = attn_norm.shape[0]
    h = x.reshape(seq, D_MODEL)
    target = loss_target.reshape(seq, D_MODEL)
    n_a, n_b = a_w_in.shape[0], b_w_in.shape[0]

    small = jnp.concatenate([t.reshape(-1) for t in (b_w_gate_f, b_gate_bias_f, b_w_gate_b, b_gate_bias_b, b_out_norm)])
    small = _pack_rows(small, LANES)
    g_a_in, g_a_out, g_b_in, g_b_out, g_gu, g_down, g_small = all_gather(
        [a_w_in.astype(BF16), a_w_out.astype(BF16), b_w_in.astype(BF16), b_w_out.astype(BF16),
         ffn_w_gate_up.astype(BF16), ffn_w_down.astype(BF16), small])
    w_a_in = _from_dev_cols(g_a_in)
    w_a_out = jnp.moveaxis(g_a_out, 0, 1).reshape(n_a, D_MODEL, D_MODEL)
    w_b_in = _from_dev_cols(g_b_in)
    w_b_out = jnp.moveaxis(g_b_out, 0, 1).reshape(n_b, D_MODEL, D_MODEL)
    w_down = jnp.moveaxis(g_down, 0, 1).reshape(depth, 4, FFN_BLK, D_MODEL)
    w_gu = g_gu.reshape(2, 4, depth, D_MODEL, FFN_BLK)
    small_shapes = [t.shape for t in (b_w_gate_f, b_gate_bias_f, b_w_gate_b, b_gate_bias_b, b_out_norm)]
    per_dev = [_unpack(g_small[d].reshape(-1), small_shapes) for d in range(N_DEV)]
    wgf, bgf, wgb, bgb, onorm = [_from_dev_cols(jnp.stack([per_dev[d][t] for d in range(N_DEV)])) for t in range(5)]
    w_gate = jnp.zeros((n_b, LANES, D_MODEL), F32)
    w_gate = w_gate.at[:, 0:16, 0:512].set(wgf).at[:, 16:32, 512:1024].set(wgb)
    gate_bias = jnp.concatenate([bgf, bgb], axis=1).reshape(n_b, 1, D_MODEL)
    out_gain = onorm.reshape(n_b, 1, D_MODEL)
    w_b_main = w_b_in[:, :, :3072]
    w_b_z = jnp.pad(w_b_in[:, :, 3072:], ((0, 0), (0, 0), (0, LANES - 32)))

    dh, sq_err, grads = _forward_backward(h, target, attn_norm, ffn_norm, a_q_norm, a_k_norm, w_a_in, w_a_out, w_b_main, w_b_z,
                                          w_gate, gate_bias, out_gain, w_b_out, w_gu, w_down)
    sharded_w = (a_w_in, a_w_out, b_w_in, b_w_gate_f, b_gate_bias_f, b_w_gate_b, b_gate_bias_b, b_out_norm, b_w_out, ffn_w_gate_up, ffn_w_down)
    sharded_m = (m_a_w_in, m_a_w_out, m_b_w_in, m_b_w_gate_f, m_b_gate_bias_f, m_b_w_gate_b, m_b_gate_bias_b, m_b_out_norm, m_b_w_out, m_ffn_w_gate_up, m_ffn_w_down)
    sharded_v = (v_a_w_in, v_a_w_out, v_b_w_in, v_b_w_gate_f, v_b_gate_bias_f, v_b_w_gate_b, v_b_gate_bias_b, v_b_out_norm, v_b_w_out, v_ffn_w_gate_up, v_ffn_w_down)
    rep_w = (attn_norm, ffn_norm, a_q_norm, a_k_norm)
    rep_m = (m_attn_norm, m_ffn_norm, m_a_q_norm, m_a_k_norm)
    rep_v = (v_attn_norm, v_ffn_norm, v_a_q_norm, v_a_k_norm)
    loss, outs = _reduce_and_update(grads, sq_err, sharded_w, sharded_m, sharded_v, rep_w, rep_m, rep_v)
    return (loss, dh.reshape(x.shape), *outs)


def _forward_backward(h, target, attn_norm, ffn_norm, a_q_norm, a_k_norm, w_a_in, w_a_out, w_b_main, w_b_z, w_gate, gate_bias,
                      out_gain, w_b_out, w_gu, w_down):
    seq = h.shape[0]
    depth = attn_norm.shape[0]
    n_a, n_b = w_a_in.shape[0], w_b_main.shape[0]
    tab = _rope_table(seq)
    pair_gain = lambda g: jnp.concatenate([g, g]).reshape(1, LANES)

    saved = []
    for i in range(depth):
        j = i // 2
        nm = f"l{i}"
        if i == 0:
            hn = rmsnorm_fwd(nm + "_norm1", h, attn_norm[i].reshape(1, D_MODEL))
        if i % 2 == 0:
            qkv = mm_nn(nm + "_qkv", hn, w_a_in[j])
            os_, lses = [], []
            for g, (_, dil) in enumerate(A_GROUPS):
                o, lse = attn_fwd(f"{nm}_attn{g}", qkv, tab, pair_gain(a_q_norm[j, g]), pair_gain(a_k_norm[j, g]), g, dil)
                os_.append(o)
                lses.append(lse)
            mixed = attn_merge_fwd(nm + "_merge", os_, lses)
            h_mid, hn2 = mm_nn(nm + "_out", mixed, w_a_out[j], res=h, norm_gain=ffn_norm[i].reshape(1, D_MODEL))
            mix_saved = (qkv, os_, lses, mixed)
        else:
            proj = mm_nn(nm + "_proj", hn, w_b_main[j])
            z = mm_nn(nm + "_z", hn, w_b_z[j])
            la = gla_gate_fwd(nm + "_gate", z, w_gate[j], gate_bias[j])
            o_f, st_f = gla_fwd(nm + "_gla_f", proj, la, False)
            o_b, st_b = gla_fwd(nm + "_gla_b", proj, la, True)
            mixed = gla_post_fwd(nm + "_post", o_f, o_b, proj, out_gain[j])
            h_mid, hn2 = mm_nn(nm + "_out", mixed, w_b_out[j], res=h, norm_gain=ffn_norm[i].reshape(1, D_MODEL))
            mix_saved = (proj, z, la, o_f, st_f, o_b, st_b, mixed)
        next_gain = attn_norm[i + 1].reshape(1, D_MODEL) if i + 1 < depth else None
        h_next, gu, act = ffn_fwd(nm + "_ffn", h_mid, hn2, w_gu, w_down[i], i, next_gain)
        saved.append((h, hn, mix_saved, h_mid, hn2, gu, act))
        if next_gain is not None:
            h_next, hn = h_next
        h = h_next

    dh, sq_err = loss_and_grad(h, target)

    g_attn_norm, g_ffn_norm = [None] * depth, [None] * depth
    g_a_w_in, g_a_w_out, g_a_q, g_a_k = [None] * n_a, [None] * n_a, [None] * n_a, [None] * n_a
    g_b_w_in, g_b_w_out, g_w_gate, g_gate_bias, g_out_gain = ([None] * n_b for _ in range(5))
    g_w_gu, g_w_down = [None] * depth, [None] * depth
    for i in reversed(range(depth)):
        j = i // 2
        nm = f"l{i}b"
        h_in, hn, mix_saved, h_mid, hn2, gu, act = saved[i]
        d_hn2, g_w_gu[i], g_w_down[i] = ffn_bwd(nm + "_ffn", dh, hn2, gu, act, w_gu, w_down[i], i)
        dh_mid, g_ffn_norm[i] = rmsnorm_bwd(nm + "_norm2", h_mid, ffn_norm[i].reshape(1, D_MODEL), d_hn2, dh)
        if i % 2 == 0:
            qkv, os_, lses, mixed = mix_saved
            d_mixed = mm_nt(nm + "_dmixed", dh_mid, w_a_out[j])
            g_a_w_out[j] = mm_tn(nm + "_dwout", mixed, dh_mid, by_block=True)
            d_parts = attn_merge_bwd(nm + "_merge", os_, lses, d_mixed)
            gq_l, gk_l = [], []
            d_qkv = lax.empty(qkv.shape, BF16)
            for g, (_, dil) in enumerate(A_GROUPS):
                d_qkv, dk_parts, dv_parts, dgq, dgk = attn_bwd(f"{nm}_attn{g}", qkv, tab, pair_gain(a_q_norm[j, g]),
                                                               pair_gain(a_k_norm[j, g]), os_[g], lses[g], d_parts[g], d_parts[3 + g], g, dil, d_qkv)
                d_qkv = attn_combine(f"{nm}_dk{g}", dk_parts, seq, dil, d_qkv, 3 * g + 1)
                d_qkv = attn_combine(f"{nm}_dv{g}", dv_parts, seq, dil, d_qkv, 3 * g + 2)
                gq_l.append(dgq[0, :A_HEAD_DIM])
                gk_l.append(dgk[0, :A_HEAD_DIM])
            g_a_q[j], g_a_k[j] = jnp.stack(gq_l), jnp.stack(gk_l)
            d_hn = mm_nt(nm + "_dhn", d_qkv, w_a_in[j])
            g_a_w_in[j] = mm_tn(nm + "_dwin", hn, d_qkv, tn=D_MODEL * 9 // N_DEV, by_block=True)
        else:
            proj, z, la, o_f, st_f, o_b, st_b, mixed = mix_saved
            d_mixed = mm_nt(nm + "_dmixed", dh_mid, w_b_out[j])
            g_b_w_out[j] = mm_tn(nm + "_dwout", mixed, dh_mid, by_block=True)
            d_o, d_r, dgain = gla_post_bwd(nm + "_post", o_f, o_b, proj, out_gain[j], d_mixed)
            g_out_gain[j] = dgain[0]
            dq, dk, dv, dla_f = gla_bwd(nm + "_gla_f", proj, la, st_f, d_o, False)
            dq, dk, dv, dla_b = gla_bwd(nm + "_gla_b", proj, la, st_b, d_o, True, prev=(dq, dk, dv))
            d_z, g_w_gate[j], dbias = gla_gate_bwd(nm + "_gate", z, w_gate[j], gate_bias[j], dla_f, dla_b)
            g_gate_bias[j] = dbias[0]
            d_proj = jnp.concatenate([dq, dk, dv, d_r], axis=1)
            d_hn = mm_nt(nm + "_dhn_z", d_z, w_b_z[j])
            d_hn = mm_nt(nm + "_dhn", d_proj, w_b_main[j], res=d_hn)
            g_b_w_in[j] = jnp.concatenate([mm_tn(nm + "_dwin", hn, d_proj), mm_tn(nm + "_dwz", hn, d_z)[:, :32]], axis=1)
        dh, g_attn_norm[i] = rmsnorm_bwd(nm + "_norm1", h_in, attn_norm[i].reshape(1, D_MODEL), d_hn, dh_mid)
    return dh, sq_err, (g_attn_norm, g_ffn_norm, g_a_w_in, g_a_w_out, g_a_q, g_a_k, g_b_w_in, g_b_w_out, g_w_gate, g_gate_bias,
                        g_out_gain, g_w_gu, g_w_down)


def _reduce_and_update(grads, sq_err, sharded_w, sharded_m, sharded_v, rep_w, rep_m, rep_v):
    (g_attn_norm, g_ffn_norm, g_a_w_in, g_a_w_out, g_a_q, g_a_k, g_b_w_in, g_b_w_out, g_w_gate, g_gate_bias, g_out_gain,
     g_w_gu, g_w_down) = grads
    depth, n_a, n_b = len(g_w_gu), len(g_a_w_in), len(g_b_w_in)

    g_w_gate = jnp.stack(g_w_gate)
    g_gate_bias = jnp.stack(g_gate_bias)
    small_parts = [_to_dev_cols(g_w_gate[:, 0:16, 0:512]), _to_dev_cols(g_gate_bias[:, 0:512]),
                   _to_dev_cols(g_w_gate[:, 16:32, 512:1024]), _to_dev_cols(g_gate_bias[:, 512:1024]),
                   _to_dev_cols(jnp.stack(g_out_gain).reshape(n_b, B_HEADS, B_VAL_DIM))]
    small_part = _pack_rows(jnp.concatenate([t.reshape(N_DEV, -1) for t in small_parts], axis=1), LANES).astype(BF16)
    families = [
        (0, g_a_w_in, 256),
        (1, [t.reshape(N_DEV, -1, D_MODEL) for t in g_a_w_out], 128),
        (2, [_to_dev_cols(t).astype(BF16) for t in g_b_w_in], 256),
        (8, [t.reshape(N_DEV, -1, D_MODEL) for t in g_b_w_out], 128),
        (9, [t.reshape(N_DEV, D_MODEL, FFN_BLK) for t in g_w_gu], 256),
        (10, [t.reshape(N_DEV, -1, D_MODEL) for t in g_w_down], 176),
    ]
    flat_parts = [p for _, parts, _ in families for p in parts] + [small_part]
    received = exchange_partials(flat_parts)
    sh_out = [None] * len(sharded_w)
    pos = 0
    for fam, parts, tile in families:
        w = sharded_w[fam]
        two_d = lambda t: t.reshape(-1, t.shape[-1])
        per_layer = []
        for layer in range(len(parts)):
            per_layer.append(adamw_shard(f"adamw_p{fam}_l{layer}", received[pos], two_d(w), two_d(sharded_m[fam]),
                                         two_d(sharded_v[fam]), layer, tile))
            pos += 1
        sh_out[fam] = [jnp.stack([per_layer[l][t] for l in range(len(parts))]).reshape(w.shape) for t in range(4)]
    small_ids = (3, 4, 5, 6, 7)
    pack_small = lambda ts: _pack_rows(jnp.concatenate([ts[i].reshape(-1) for i in small_ids]), LANES)
    small_out = adamw_shard("adamw_small", received[pos], pack_small(sharded_w), pack_small(sharded_m), pack_small(sharded_v), 0, 48)
    small_shapes = [sharded_w[i].shape for i in small_ids]
    for t in range(4):
        for i, val in zip(small_ids, _unpack(small_out[t].reshape(-1), small_shapes)):
            if sh_out[i] is None:
                sh_out[i] = [None] * 4
            sh_out[i][t] = val
    sh_grad, sh_delta, sh_m, sh_v = [[sh_out[i][t] for i in range(len(sharded_w))] for t in range(4)]

    rep_g = (jnp.stack([t[0] for t in g_attn_norm]), jnp.stack([t[0] for t in g_ffn_norm]), jnp.stack(g_a_q), jnp.stack(g_a_k))
    n_rep = sum(t.size for t in rep_w)
    n_rep_rows = -(-n_rep // (8 * LANES)) * 8
    n_loss_rows = 8 * D_MODEL // LANES

    def pack_rep(ts, tail):
        flat = jnp.concatenate([t.reshape(-1) for t in ts])
        flat = jnp.pad(flat, (0, n_rep_rows * LANES - n_rep))
        return jnp.concatenate([flat.reshape(n_rep_rows, LANES), tail], axis=0)

    zeros_tail = jnp.zeros((n_loss_rows, LANES), F32)
    rep_out = allreduce_adamw_replicated(pack_rep(rep_g, sq_err.reshape(n_loss_rows, LANES)), pack_rep(rep_w, zeros_tail),
                                         pack_rep(rep_m, zeros_tail), pack_rep(rep_v, zeros_tail + 1.0), n_loss_rows)
    rep_shapes = [t.shape for t in rep_w]
    r_grad, r_delta, r_m, r_v = [_unpack(p.reshape(-1), rep_shapes) for p in rep_out[:4]]
    loss = rep_out[4][0, 0]

    def ordered(rep, sh):
        return [rep[0], rep[1], sh[0], rep[2], rep[3]] + list(sh[1:])

    return loss, (*ordered(r_grad, sh_grad), *ordered(r_delta, sh_delta), *ordered(r_m, sh_m), *ordered(r_v, sh_v))
```

```python
import functools

import jax
import jax.numpy as jnp
from jax import lax
from jax.experimental import pallas as pl
from jax.experimental.pallas import tpu as pltpu

F32 = jnp.float32
BF16 = jnp.bfloat16
MXU_DTYPE = jnp.bfloat16

D_MODEL = 1024
N_DEV = 8
RMS_EPS = 1e-6
NEG_INF = -1e30
A_GROUPS = ((128, 1), (512, 4), (2048, 16))
A_HEAD_DIM = 64
A_HALF = 64
ATT_T = 128
ATT_TILE = 2048
ATT_FWD_BATCH = 8
ATT_BWD_BATCH = 4
B_HEADS = 4
B_KEY_DIM = 128
B_VAL_DIM = 256
B_CHUNK = 64
B_GATE_TAU = 16.0
FFN_HIDDEN = 2816
FFN_BLK = 2 * FFN_HIDDEN // N_DEV
ADAM_LR, ADAM_B1, ADAM_B2, ADAM_EPS, ADAM_WD, ADAM_STEP = 0.001, 0.9, 0.999, 1e-08, 0.01, 10
ROPE_THETA = 10000.0

V7X_VMEM_LIMIT = 60 * 1024 * 1024
LANES = 128
MESH = pl.DeviceIdType.MESH
ANY = pl.BlockSpec(memory_space=pl.ANY)
VMEM_SPEC = pl.BlockSpec(memory_space=pltpu.VMEM)

NN = ((1,), (0,))
NT = ((1,), (1,))
TN = ((0,), (0,))


def _dot(a, b, dims):
    return lax.dot_general(a.astype(MXU_DTYPE), b.astype(MXU_DTYPE), (dims, ((), ())), preferred_element_type=F32)


@jax.custom_vjp
def dot_nn(a, b):
    return _dot(a, b, NN)


@jax.custom_vjp
def dot_nt(a, b):
    return _dot(a, b, NT)


@jax.custom_vjp
def dot_tn(a, b):
    return _dot(a, b, TN)


dot_nn.defvjp(lambda a, b: (_dot(a, b, NN), (a, b)), lambda r, g: (dot_nt(g, r[1]), dot_tn(r[0], g)))
dot_nt.defvjp(lambda a, b: (_dot(a, b, NT), (a, b)), lambda r, g: (dot_nn(g, r[1]), dot_tn(g, r[0])))
dot_tn.defvjp(lambda a, b: (_dot(a, b, TN), (a, b)), lambda r, g: (dot_nt(r[1], g), dot_nn(r[0], g)))


def _bdot(a, b, dims):
    dn = (tuple((d[0] + 1,) for d in dims), ((0,), (0,)))
    return lax.dot_general(a.astype(MXU_DTYPE), b.astype(MXU_DTYPE), dn, preferred_element_type=F32)


@jax.custom_vjp
def bdot_nn(a, b):
    return _bdot(a, b, NN)


@jax.custom_vjp
def bdot_nt(a, b):
    return _bdot(a, b, NT)


@jax.custom_vjp
def bdot_tn(a, b):
    return _bdot(a, b, TN)


bdot_nn.defvjp(lambda a, b: (_bdot(a, b, NN), (a, b)), lambda r, g: (bdot_nt(g, r[1]), bdot_tn(r[0], g)))
bdot_nt.defvjp(lambda a, b: (_bdot(a, b, NT), (a, b)), lambda r, g: (bdot_nn(g, r[1]), bdot_tn(g, r[0])))
bdot_tn.defvjp(lambda a, b: (_bdot(a, b, TN), (a, b)), lambda r, g: (bdot_nt(r[1], g), bdot_nn(r[0], g)))


def _dot_f32(a, b):
    return lax.dot_general(a, b, (NN, ((), ())), precision=lax.Precision.HIGHEST, preferred_element_type=F32)


def _tri(n, upper):
    r = lax.broadcasted_iota(jnp.int32, (n, n), 0)
    c = lax.broadcasted_iota(jnp.int32, (n, n), 1)
    return jnp.where((c >= r) if upper else (c <= r), 1.0, 0.0).astype(F32)


def _chunk_cumsum(x, reverse):
    tri = jnp.broadcast_to(_tri(x.shape[1], reverse), (x.shape[0], x.shape[1], x.shape[1]))
    return lax.dot_general(tri, x, (((2,), (1,)), ((0,), (0,))), precision=lax.Precision.HIGHEST, preferred_element_type=F32)


@functools.partial(jax.custom_vjp, nondiff_argnums=(1,))
def cumsum_chunks(x, reverse):
    return _chunk_cumsum(x, reverse)


cumsum_chunks.defvjp(lambda x, reverse: (_chunk_cumsum(x, reverse), None), lambda reverse, _, g: (_chunk_cumsum(g, not reverse),))


def _swap32_raw(x):
    lane = lax.broadcasted_iota(jnp.int32, x.shape, 1)
    return jnp.where((lane % 64) < 32, pltpu.roll(x, 96, 1), pltpu.roll(x, 32, 1))


@jax.custom_vjp
def swap32(x):
    return _swap32_raw(x)


swap32.defvjp(lambda x: (_swap32_raw(x), None), lambda _, g: (_swap32_raw(g),))


def _rms(x, gain):
    return x * lax.rsqrt(jnp.mean(x * x, axis=-1, keepdims=True) + RMS_EPS) * gain


def _sigmoid(x):
    return 1.0 / (1.0 + jnp.exp(-x))


def _log_sigmoid(x):
    return jnp.minimum(x, 0.0) - jnp.log(1.0 + jnp.exp(-jnp.abs(x)))


def _qk_prep(x, tab, gain):
    lo = lax.broadcasted_iota(jnp.int32, (1, LANES), 1) < A_HEAD_DIM
    x2 = x * x
    s_lo = jnp.sum(jnp.where(lo, x2, 0.0), axis=-1, keepdims=True)
    s_hi = jnp.sum(jnp.where(lo, 0.0, x2), axis=-1, keepdims=True)
    xn = (x * lax.rsqrt(jnp.where(lo, s_lo, s_hi) / A_HEAD_DIM + RMS_EPS)) * gain
    return xn * tab[:, :LANES] + swap32(xn) * tab[:, LANES:]


def _stack_heads(x):
    lo = lax.broadcasted_iota(jnp.int32, (1, LANES), 1) < A_HEAD_DIM
    return jnp.concatenate([jnp.where(lo, x, 0.0), jnp.where(lo, 0.0, x)], axis=-2)


def _unstack_heads(x):
    t = x.shape[-2] // 2
    lo = lax.broadcasted_iota(jnp.int32, (1, LANES), 1) < A_HEAD_DIM
    return jnp.where(lo, x[..., :t, :], x[..., t:, :])


def _attn_scores(q, k, valid):
    s = _bdot(_stack_heads(q), k, NT) * (A_HEAD_DIM ** -0.5)
    return jnp.where(valid, s, NEG_INF)


def _attn_job(q, k, v, valid):
    s = _attn_scores(q, k, valid)
    mx = jnp.max(s, axis=-1, keepdims=True)
    p = jnp.exp(s - mx)
    l = jnp.sum(p, axis=-1, keepdims=True)
    out = _unstack_heads(_bdot(p, v, NN) / l)
    lse = mx + jnp.log(l)
    t = q.shape[1]
    lo = lax.broadcasted_iota(jnp.int32, (1, LANES), 1) < A_HEAD_DIM
    return out, jnp.where(lo, lse[:, :t], lse[:, t:])


def _attn_job_bwd(q, k, v, valid, d_out, out, lse, d_lse):
    qs = _stack_heads(q)
    lo = lax.broadcasted_iota(jnp.int32, (1, LANES), 1) < A_HEAD_DIM
    other = pltpu.roll(lse, A_HEAD_DIM, 2)
    row_lse = jnp.concatenate([jnp.where(lo, lse, other), jnp.where(lo, other, lse)], axis=1)
    p = jnp.exp(_attn_scores(q, k, valid) - jnp.concatenate([row_lse, row_lse], axis=2))
    dos = _stack_heads(d_out)
    dv = _bdot(p, dos, TN)
    dp = _bdot(dos, v, NT)
    inner = jnp.sum(dos * _stack_heads(out) - _stack_heads(d_lse), axis=-1, keepdims=True)
    ds = p * (dp - inner) * (A_HEAD_DIM ** -0.5)
    return _unstack_heads(_bdot(ds, k, NN)), _bdot(ds, qs, TN), dv


def _merge_groups(o0, o1, o2, l0, l1, l2):
    mx = lax.stop_gradient(jnp.maximum(jnp.maximum(l0, l1), l2))
    e0, e1, e2 = jnp.exp(l0 - mx), jnp.exp(l1 - mx), jnp.exp(l2 - mx)
    den = e0 + e1 + e2
    return (e0 / den) * o0 + (e1 / den) * o1 + (e2 / den) * o2


def _gla_tile(q, k, v, la, st, reverse):
    t = q.shape[0]
    nc = t // B_CHUNK
    split = lambda x: x.reshape(nc, B_CHUNK, x.shape[1])
    q, k, v, la = split(q * (B_KEY_DIM ** -0.5)), split(k), split(v), split(la)
    r = lax.broadcasted_iota(jnp.int32, (1, B_CHUNK, B_CHUNK), 1)
    c = lax.broadcasted_iota(jnp.int32, (1, B_CHUNK, B_CHUNK), 2)
    mask = (c > r) if reverse else (c <= r)
    b = cumsum_chunks(la, reverse)
    tot = jnp.sum(la, axis=1, keepdims=True)
    q_t = q * jnp.exp(b)
    k_t = k * jnp.exp(-b)
    k_end = k * jnp.exp(tot - b)
    attn = jnp.where(mask, bdot_nt(q_t, k_t), 0.0)
    kv = bdot_tn(v, k_end)
    decay = jnp.exp(tot)
    entering = [None] * nc
    for ci in (range(nc - 1, -1, -1) if reverse else range(nc)):
        entering[ci] = st
        st = st * decay[ci] + kv[ci]
    out = bdot_nn(attn, v) + bdot_nt(q_t, jnp.concatenate([e[None] for e in entering], axis=0))
    return out.reshape(t, out.shape[2]), st


def _gla_post(o_f, o_b, r, gain):
    o = o_f + o_b
    heads = [_rms(o[:, h * B_VAL_DIM:(h + 1) * B_VAL_DIM], gain[:, h * B_VAL_DIM:(h + 1) * B_VAL_DIM]) for h in range(B_HEADS)]
    return jnp.concatenate(heads, axis=1) * (r * _sigmoid(r))


def _gate(z, wg, bias):
    return _log_sigmoid(dot_nn(z, wg) + bias) / B_GATE_TAU


def _swiglu_act(g, u):
    return (g * _sigmoid(g)) * u


def _adamw(w, g, m, v):
    m = ADAM_B1 * m + (1.0 - ADAM_B1) * g
    v = ADAM_B2 * v + (1.0 - ADAM_B2) * jnp.square(g)
    m_hat = m / (1.0 - ADAM_B1 ** ADAM_STEP)
    v_hat = v / (1.0 - ADAM_B2 ** ADAM_STEP)
    delta = -ADAM_LR * (m_hat / (jnp.sqrt(v_hat) + ADAM_EPS) + ADAM_WD * w)
    return delta, m, v


def _params(sem=None):
    return pltpu.CompilerParams(dimension_semantics=sem, vmem_limit_bytes=V7X_VMEM_LIMIT)


def _row_tile(s, want=512):
    t = min(want, s)
    assert s % t == 0
    return t


def _matmul(name, a, b, *, dims, grid, a_spec, b_spec, o_spec, out_shape, red_axis=None, res=None, res_spec=None, acc_shape=None,
            norm_gain=None):
    n_red = grid[red_axis] if red_axis is not None else 1
    n_in = 2 + (res is not None) + (norm_gain is not None)

    def body(*refs):
        a_ref, b_ref = refs[0], refs[1]
        r_ref = refs[2] if res is not None else None
        g_ref = refs[n_in - 1] if norm_gain is not None else None
        o_ref = refs[n_in]
        n_ref = refs[n_in + 1] if norm_gain is not None else None
        prod = lax.dot_general(a_ref[...].astype(MXU_DTYPE), b_ref[...].astype(MXU_DTYPE), (dims, ((), ())),
                               preferred_element_type=F32)
        if red_axis is None:
            if r_ref is not None:
                prod = prod + r_ref[...]
            o_ref[...] = prod.astype(o_ref.dtype)
            if n_ref is not None:
                n_ref[...] = _rms(prod, g_ref[...]).astype(n_ref.dtype)
            return
        acc = refs[-1] if acc_shape is not None else o_ref
        k = pl.program_id(red_axis)

        @pl.when(k == 0)
        def _():
            acc[...] = prod + r_ref[...] if r_ref is not None else prod

        @pl.when(k > 0)
        def _():
            acc[...] += prod

        if acc_shape is not None or n_ref is not None:
            @pl.when(k == n_red - 1)
            def _():
                if acc_shape is not None:
                    o_ref[...] = acc[...].astype(o_ref.dtype)
                if n_ref is not None:
                    n_ref[...] = _rms(acc[...], g_ref[...]).astype(n_ref.dtype)

    ins = [a, b] + ([res] if res is not None else []) + ([norm_gain] if norm_gain is not None else [])
    specs = [a_spec, b_spec] + ([res_spec] if res is not None else [])
    out_specs, out_shapes = o_spec, out_shape
    if norm_gain is not None:
        specs.append(pl.BlockSpec(norm_gain.shape, lambda *g: (0, 0)))
        out_specs, out_shapes = [o_spec, o_spec], [out_shape, jax.ShapeDtypeStruct(out_shape.shape, BF16)]
    sem = tuple("arbitrary" if i == red_axis else "parallel" for i in range(len(grid)))
    return pl.pallas_call(body, name=name, grid=grid, in_specs=specs, out_specs=out_specs, out_shape=out_shapes,
                          scratch_shapes=[pltpu.VMEM(acc_shape, F32)] if acc_shape is not None else [],
                          compiler_params=_params(sem))(*ins)


def mm_nn(name, x, w, *, res=None, out_dtype=F32, tn=1024, norm_gain=None):
    m, k = x.shape
    n = w.shape[1]
    tm, tn = _row_tile(m, 1024), min(tn, n)
    return _matmul(name, x, w, dims=NN, grid=(n // tn, m // tm),
                   a_spec=pl.BlockSpec((tm, k), lambda j, i: (i, 0)), b_spec=pl.BlockSpec((k, tn), lambda j, i: (0, j)),
                   o_spec=pl.BlockSpec((tm, tn), lambda j, i: (i, j)), out_shape=jax.ShapeDtypeStruct((m, n), out_dtype),
                   res=res, res_spec=pl.BlockSpec((tm, tn), lambda j, i: (i, j)), norm_gain=norm_gain)


def mm_nt(name, dy, w, *, res=None, tn=2304):
    m, n = dy.shape
    k = w.shape[0]
    tm, tn = _row_tile(m, 1024), (tn if n % tn == 0 else min(1024, n))
    return _matmul(name, dy, w, dims=NT, grid=(m // tm, n // tn), red_axis=1,
                   a_spec=pl.BlockSpec((tm, tn), lambda i, j: (i, j)), b_spec=pl.BlockSpec((k, tn), lambda i, j: (0, j)),
                   o_spec=pl.BlockSpec((tm, k), lambda i, j: (i, 0)), out_shape=jax.ShapeDtypeStruct((m, k), F32),
                   res=res, res_spec=pl.BlockSpec((tm, k), lambda i, j: (i, 0)))


def mm_tn(name, x, dy, *, tn=1024, by_block=False):
    m, k = x.shape
    n = dy.shape[1]
    tm, tn = _row_tile(m, 2048), min(tn, n)
    if by_block:
        o_spec, out_shape, acc = pl.BlockSpec((None, k, tn), lambda j, i: (j, 0, 0)), jax.ShapeDtypeStruct((n // tn, k, tn), BF16), (k, tn)
    else:
        o_spec, out_shape, acc = pl.BlockSpec((k, tn), lambda j, i: (0, j)), jax.ShapeDtypeStruct((k, n), F32), None
    return _matmul(name, x, dy, dims=TN, grid=(n // tn, m // tm), red_axis=1,
                   a_spec=pl.BlockSpec((tm, k), lambda j, i: (i, 0)), b_spec=pl.BlockSpec((tm, tn), lambda j, i: (i, j)),
                   o_spec=o_spec, out_shape=out_shape, acc_shape=acc)


def _rows_call(name, body, ins, outs, s, tile):
    in_specs = []
    for a, kind in ins:
        if kind == "row":
            in_specs.append(pl.BlockSpec((tile, a.shape[1]), lambda i: (i, 0)))
        else:
            in_specs.append(pl.BlockSpec(a.shape, lambda i, nd=a.ndim: (0,) * nd))
    out_specs, out_shape = [], []
    for cols, dt, kind in outs:
        if kind == "row":
            out_specs.append(pl.BlockSpec((tile, cols), lambda i: (i, 0)))
            out_shape.append(jax.ShapeDtypeStruct((s, cols), dt))
        else:
            out_specs.append(pl.BlockSpec((8, cols), lambda i: (0, 0)))
            out_shape.append(jax.ShapeDtypeStruct((8, cols), dt))
    has_acc = any(kind == "acc" for _, _, kind in outs)
    return pl.pallas_call(body, name=name, grid=(s // tile,), in_specs=in_specs, out_specs=out_specs, out_shape=out_shape,
                          compiler_params=_params(("arbitrary",) if has_acc else ("parallel",)))(*[a for a, _ in ins])


def _acc_rows(ref, val):
    @pl.when(pl.program_id(0) == 0)
    def _():
        ref[...] = jnp.zeros_like(ref)

    ref[...] += jnp.broadcast_to(val, ref.shape)


def rmsnorm_fwd(name, h, gain):
    s = h.shape[0]

    def body(h_ref, g_ref, o_ref):
        o_ref[...] = _rms(h_ref[...], g_ref[...]).astype(o_ref.dtype)

    return _rows_call(name, body, [(h, "row"), (gain, "full")], [(D_MODEL, BF16, "row")], s, _row_tile(s))[0]


def rmsnorm_bwd(name, h, gain, d_hn, d_res):
    s = h.shape[0]

    def body(h_ref, g_ref, dy_ref, dr_ref, dh_ref, dg_ref):
        _, vjp = jax.vjp(_rms, h_ref[...], g_ref[...])
        dh, dg = vjp(dy_ref[...])
        dh_ref[...] = dh + dr_ref[...]
        _acc_rows(dg_ref, dg)

    return _rows_call(name, body, [(h, "row"), (gain, "full"), (d_hn, "row"), (d_res, "row")],
                      [(D_MODEL, F32, "row"), (D_MODEL, F32, "acc")], s, _row_tile(s))


def loss_and_grad(y, target):
    s = y.shape[0]
    tile = _row_tile(s)

    def body(y_ref, t_ref, dy_ref, acc_ref):
        diff = y_ref[...] - t_ref[...]
        dy_ref[...] = diff * (1.0 / D_MODEL)

        @pl.when(pl.program_id(0) == 0)
        def _():
            acc_ref[...] = jnp.zeros_like(acc_ref)

        acc_ref[...] += jnp.sum((diff * diff).reshape(tile // 8, 8, D_MODEL), axis=0)

    return _rows_call("loss_head", body, [(y, "row"), (target, "row")], [(D_MODEL, F32, "row"), (D_MODEL, F32, "acc")], s, tile)


def _rope_table(s):
    half = A_HEAD_DIM // 2
    inv_freq = ROPE_THETA ** (-jnp.arange(half, dtype=F32) / half)
    ang = jnp.arange(s).astype(F32)[:, None] * inv_freq[None, :]
    cos, sin = jnp.cos(ang), jnp.sin(ang)
    return jnp.concatenate([cos, cos, cos, cos, -sin, sin, -sin, sin], axis=1)


def _attn_geometry(s, dil):
    tile = min(ATT_TILE, s)
    halo = A_HALF * dil
    assert s % tile == 0 and tile % (ATT_T * dil) == 0 and tile % halo == 0
    return tile, halo, tile // (ATT_T * dil)


def _attn_in_specs(grp, s, tile, halo):
    hb, n_hb = tile // halo, s // halo
    cq, ck, cv = (24 * grp + 8 * t for t in range(3))

    def main(col, per_pair, width=LANES):
        return pl.BlockSpec((tile, width), lambda i, j: (i, col + per_pair * j))

    def prev(col, per_pair, width=LANES):
        return pl.BlockSpec((halo, width), lambda i, j: (jnp.maximum(i * hb - 1, 0), col + per_pair * j))

    def nxt(col, per_pair, width=LANES):
        return pl.BlockSpec((halo, width), lambda i, j: (jnp.minimum((i + 1) * hb, n_hb - 1), col + per_pair * j))

    return [main(cq, 1), prev(ck, 1), main(ck, 1), nxt(ck, 1), prev(cv, 1), main(cv, 1), nxt(cv, 1),
            prev(0, 0, 256), main(0, 0, 256), nxt(0, 0, 256)]


def _attn_valid(first_job, n_jobs, dil, tile_base, length):
    r = lax.broadcasted_iota(jnp.int32, (1, 2 * ATT_T, 1), 1)
    tq = jnp.where(r >= ATT_T, r - ATT_T, r)
    rel = lax.broadcasted_iota(jnp.int32, (1, 1, ATT_T + 2 * A_HALF), 2) - A_HALF
    job = first_job + lax.broadcasted_iota(jnp.int32, (n_jobs, 1, 1), 0)
    tk = tile_base + (job // dil) * ATT_T + rel
    return (jnp.abs(rel - tq) <= A_HALF) & (tk >= 0) & (tk < length)


def _jobs(dil, n_sub):
    return [u * ATT_T * dil + p for u in range(n_sub) for p in range(dil)]


def _gather_jobs(ref, starts, size, dil):
    return jnp.concatenate([ref[_rows(st, size, dil), :][None] for st in starts], axis=0)


def _rows(start, size, dil):
    return pl.ds(start, size, stride=dil) if dil > 1 else pl.ds(start, size)


def attn_fwd(name, qkv, tab, gq, gk, grp, dil):
    s = qkv.shape[0]
    tile, halo, n_sub = _attn_geometry(s, dil)
    length, per_tile = s // dil, tile // dil
    nk = ATT_T + 2 * A_HALF

    def body(q_ref, kp_ref, km_ref, kn_ref, vp_ref, vm_ref, vn_ref, tp_ref, tm_ref, tn_ref, gq_ref, gk_ref, o_ref, lse_ref,
             q_buf, k_buf, v_buf):
        i, pair = pl.program_id(0), pl.program_id(1)
        q_buf[...] = _qk_prep(q_ref[...], tm_ref[...], gq_ref[...])
        for ref, t_ref, lo, n in ((kp_ref, tp_ref, 0, halo), (km_ref, tm_ref, halo, tile), (kn_ref, tn_ref, halo + tile, halo)):
            k_buf[lo:lo + n, :] = _qk_prep(ref[...], t_ref[...], gk_ref[...])
        for ref, lo, n in ((vp_ref, 0, halo), (vm_ref, halo, tile), (vn_ref, halo + tile, halo)):
            v_buf[lo:lo + n, :] = ref[...]

        jobs = _jobs(dil, n_sub)
        for g0 in range(0, len(jobs), ATT_FWD_BATCH):
            starts = jobs[g0:g0 + ATT_FWD_BATCH]
            valid = _attn_valid(g0, len(starts), dil, i * per_tile, length)
            o, lse = _attn_job(_gather_jobs(q_buf, starts, ATT_T, dil), _gather_jobs(k_buf, starts, nk, dil),
                               _gather_jobs(v_buf, starts, nk, dil), valid)
            for n, st in enumerate(starts):
                o_ref[_rows(st, ATT_T, dil), :] = o[n]
                lse_ref[_rows(st, ATT_T, dil), :] = lse[n]

    full = lambda a: pl.BlockSpec(a.shape, lambda i, j: (0, 0))
    return pl.pallas_call(
        body, name=name, grid=(s // tile, D_MODEL // LANES),
        in_specs=_attn_in_specs(grp, s, tile, halo) + [full(gq), full(gk)],
        out_specs=[pl.BlockSpec((tile, LANES), lambda i, j: (i, j)), pl.BlockSpec((tile, LANES), lambda i, j: (i, j))],
        out_shape=[jax.ShapeDtypeStruct((s, D_MODEL), F32), jax.ShapeDtypeStruct((s, D_MODEL), F32)],
        scratch_shapes=[pltpu.VMEM((tile, LANES), F32), pltpu.VMEM((tile + 2 * halo, LANES), F32), pltpu.VMEM((tile + 2 * halo, LANES), F32)],
        compiler_params=_params(("parallel", "parallel")),
    )(qkv, qkv, qkv, qkv, qkv, qkv, qkv, tab, tab, tab, gq, gk)


def attn_bwd(name, qkv, tab, gq, gk, o, lse, d_o, d_lse, grp, dil, d_qkv):
    s = qkv.shape[0]
    tile, halo, n_sub = _attn_geometry(s, dil)
    length, per_tile = s // dil, tile // dil
    nt = s // tile
    nk = ATT_T + 2 * A_HALF
    pieces = ((0, halo), (halo, tile), (halo + tile, halo))

    def body(q_ref, kp_ref, km_ref, kn_ref, vp_ref, vm_ref, vn_ref, tp_ref, tm_ref, tn_ref, gq_ref, gk_ref, o_ref, l_ref, do_ref, dl_ref,
             _, dq_ref, dkp_ref, dkm_ref, dkn_ref, dvp_ref, dvm_ref, dvn_ref, dgq_ref, dgk_ref, q_buf, k_buf, v_buf, dq_buf, dk_buf, dv_buf):
        i, pair = pl.program_id(0), pl.program_id(1)
        k_refs, t_refs = (kp_ref, km_ref, kn_ref), (tp_ref, tm_ref, tn_ref)
        qn, q_vjp = jax.vjp(lambda x, g: _qk_prep(x, tm_ref[...], g), q_ref[...], gq_ref[...])
        q_buf[...] = qn
        k_vjps = []
        for ref, t_ref, (lo, n) in zip(k_refs, t_refs, pieces):
            kn, k_vjp = jax.vjp(lambda x, g: _qk_prep(x, t_ref[...], g), ref[...], gk_ref[...])
            k_buf[lo:lo + n, :] = kn
            k_vjps.append(k_vjp)
        for ref, (lo, n) in zip((vp_ref, vm_ref, vn_ref), pieces):
            v_buf[lo:lo + n, :] = ref[...]
        dk_buf[...] = jnp.zeros_like(dk_buf)
        dv_buf[...] = jnp.zeros_like(dv_buf)
        jobs = _jobs(dil, n_sub)
        for g0 in range(0, len(jobs), ATT_BWD_BATCH):
            starts = jobs[g0:g0 + ATT_BWD_BATCH]
            valid = _attn_valid(g0, len(starts), dil, i * per_tile, length)
            own = lambda ref: _gather_jobs(ref, starts, ATT_T, dil)
            dq, dk, dv = _attn_job_bwd(own(q_buf), _gather_jobs(k_buf, starts, nk, dil), _gather_jobs(v_buf, starts, nk, dil), valid,
                                       own(do_ref), own(o_ref), own(l_ref), own(dl_ref))
            for n, st in enumerate(starts):
                dq_buf[_rows(st, ATT_T, dil), :] = dq[n]
                dk_buf[_rows(st, nk, dil), :] += dk[n]
                dv_buf[_rows(st, nk, dil), :] += dv[n]
        dq, dgq = q_vjp(dq_buf[...])
        dq_ref[...] = dq.astype(dq_ref.dtype)
        dgk = jnp.zeros((1, LANES), F32)
        for k_vjp, out_ref, (lo, n) in zip(k_vjps, (dkp_ref, dkm_ref, dkn_ref), pieces):
            out_ref[...], dgk_piece = k_vjp(dk_buf[lo:lo + n, :])
            dgk = dgk + dgk_piece
        for out_ref, (lo, n) in zip((dvp_ref, dvm_ref, dvn_ref), pieces):
            out_ref[...] = dv_buf[lo:lo + n, :]

        @pl.when((i == 0) & (pair == 0))
        def _():
            dgq_ref[...] = jnp.zeros_like(dgq_ref)
            dgk_ref[...] = jnp.zeros_like(dgk_ref)

        dgq_ref[...] += jnp.broadcast_to(dgq + pltpu.roll(dgq, A_HEAD_DIM, 1), dgq_ref.shape)
        dgk_ref[...] += jnp.broadcast_to(dgk + pltpu.roll(dgk, A_HEAD_DIM, 1), dgk_ref.shape)

    full = lambda a: pl.BlockSpec(a.shape, lambda i, j: (0, 0))
    main_o = pl.BlockSpec((tile, LANES), lambda i, j: (i, j))
    edge_o = pl.BlockSpec((None, halo, LANES), lambda i, j: (i, 0, j))
    main_s = jax.ShapeDtypeStruct((s, D_MODEL), F32)
    edge_s = jax.ShapeDtypeStruct((nt, halo, D_MODEL), F32)
    acc_o = pl.BlockSpec((8, LANES), lambda i, j: (0, 0))
    acc_s = jax.ShapeDtypeStruct((8, LANES), F32)
    big = pltpu.VMEM((tile + 2 * halo, LANES), F32)
    own = pltpu.VMEM((tile, LANES), F32)
    outs = pl.pallas_call(
        body, name=name, grid=(nt, D_MODEL // LANES),
        in_specs=_attn_in_specs(grp, s, tile, halo) + [full(gq), full(gk), main_o, main_o, main_o, main_o, ANY],
        out_specs=[pl.BlockSpec((tile, LANES), lambda i, j: (i, 24 * grp + j)), edge_o, main_o, edge_o, edge_o, main_o, edge_o, acc_o, acc_o],
        out_shape=[jax.ShapeDtypeStruct(d_qkv.shape, d_qkv.dtype), edge_s, main_s, edge_s, edge_s, main_s, edge_s, acc_s, acc_s],
        scratch_shapes=[own, big, big, own, big, big],
        input_output_aliases={16: 0},
        compiler_params=_params(("arbitrary", "arbitrary")),
    )(qkv, qkv, qkv, qkv, qkv, qkv, qkv, tab, tab, tab, gq, gk, o, lse, d_o, d_lse, d_qkv)
    d_qkv, dkp, dkm, dkn, dvp, dvm, dvn, dgq, dgk = outs
    return d_qkv, (dkp, dkm, dkn), (dvp, dvm, dvn), dgq, dgk


def attn_combine(name, parts, s, dil, d_qkv, col):
    prev_part, main_part, next_part = parts
    tile, halo, _ = _attn_geometry(s, dil)
    nt = s // tile
    cols = D_MODEL // 2

    def body(m_ref, from_prev_ref, from_next_ref, _, o_ref):
        i = pl.program_id(0)
        o_ref[...] = m_ref[...].astype(o_ref.dtype)
        head = m_ref[0:halo, :] + jnp.where(i > 0, from_prev_ref[...], 0.0)
        o_ref[0:halo, :] = head.astype(o_ref.dtype)
        tail = m_ref[tile - halo:tile, :] + jnp.where(i < nt - 1, from_next_ref[...], 0.0)
        o_ref[tile - halo:tile, :] = tail.astype(o_ref.dtype)

    return pl.pallas_call(
        body, name=name, grid=(nt, D_MODEL // cols),
        in_specs=[pl.BlockSpec((tile, cols), lambda i, c: (i, c)),
                  pl.BlockSpec((None, halo, cols), lambda i, c: (jnp.maximum(i - 1, 0), 0, c)),
                  pl.BlockSpec((None, halo, cols), lambda i, c: (jnp.minimum(i + 1, nt - 1), 0, c)), ANY],
        out_specs=pl.BlockSpec((tile, cols), lambda i, c: (i, (D_MODEL // cols) * col + c)),
        out_shape=jax.ShapeDtypeStruct(d_qkv.shape, d_qkv.dtype), input_output_aliases={3: 0},
        compiler_params=_params(("parallel", "parallel")),
    )(main_part, next_part, prev_part, d_qkv)


def attn_merge_fwd(name, os_, lses):
    s = os_[0].shape[0]

    def body(o0, o1, o2, l0, l1, l2, out_ref):
        out_ref[...] = _merge_groups(o0[...], o1[...], o2[...], l0[...], l1[...], l2[...]).astype(out_ref.dtype)

    return _rows_call(name, body, [(a, "row") for a in (*os_, *lses)], [(D_MODEL, BF16, "row")], s, _row_tile(s, 256))[0]


def attn_merge_bwd(name, os_, lses, d_out):
    s = os_[0].shape[0]

    def body(o0, o1, o2, l0, l1, l2, d_ref, *outs):
        _, vjp = jax.vjp(_merge_groups, o0[...], o1[...], o2[...], l0[...], l1[...], l2[...])
        for ref, val in zip(outs, vjp(d_ref[...])):
            ref[...] = val

    return _rows_call(name, body, [(a, "row") for a in (*os_, *lses, d_out)],
                      [(D_MODEL, F32, "row")] * 6, s, _row_tile(s, 256))


def gla_gate_fwd(name, z, wg, bias):
    s = z.shape[0]

    def body(z_ref, w_ref, b_ref, o_ref):
        o_ref[...] = _gate(z_ref[...], w_ref[...], b_ref[...])

    return _rows_call(name, body, [(z, "row"), (wg, "full"), (bias, "full")], [(D_MODEL, F32, "row")], s, _row_tile(s))[0]


def gla_gate_bwd(name, z, wg, bias, d_la_f, d_la_b):
    s = z.shape[0]
    tile = _row_tile(s)

    def body(z_ref, w_ref, b_ref, df_ref, db_ref, dz_ref, dw_ref, dbias_ref):
        _, vjp = jax.vjp(_gate, z_ref[...], w_ref[...], b_ref[...])
        dz, dw, dbias = vjp(jnp.concatenate([df_ref[...], db_ref[...]], axis=1))
        dz_ref[...] = dz

        @pl.when(pl.program_id(0) == 0)
        def _():
            dw_ref[...] = jnp.zeros_like(dw_ref)

        dw_ref[...] += dw
        _acc_rows(dbias_ref, dbias)

    return pl.pallas_call(
        body, name=name, grid=(s // tile,),
        in_specs=[pl.BlockSpec((tile, LANES), lambda i: (i, 0)), pl.BlockSpec(wg.shape, lambda i: (0, 0)),
                  pl.BlockSpec(bias.shape, lambda i: (0, 0)), pl.BlockSpec((tile, 512), lambda i: (i, 0)),
                  pl.BlockSpec((tile, 512), lambda i: (i, 0))],
        out_specs=[pl.BlockSpec((tile, LANES), lambda i: (i, 0)), pl.BlockSpec(wg.shape, lambda i: (0, 0)),
                   pl.BlockSpec((8, D_MODEL), lambda i: (0, 0))],
        out_shape=[jax.ShapeDtypeStruct((s, LANES), F32), jax.ShapeDtypeStruct(wg.shape, F32), jax.ShapeDtypeStruct((8, D_MODEL), F32)],
        compiler_params=_params(("arbitrary",)),
    )(z, wg, bias, d_la_f, d_la_b)


def _gla_in_specs(tile, order, la_col0):
    t = order
    return [pl.BlockSpec((tile, B_KEY_DIM), lambda h, n: (t(n), h)),
            pl.BlockSpec((tile, B_KEY_DIM), lambda h, n: (t(n), B_HEADS + h)),
            pl.BlockSpec((tile, B_VAL_DIM), lambda h, n: (t(n), B_HEADS + h)),
            pl.BlockSpec((tile, B_KEY_DIM), lambda h, n: (t(n), la_col0 + h))]


def gla_fwd(name, proj, la, reverse):
    s = proj.shape[0]
    tile = _row_tile(s)
    nt = s // tile
    order = (lambda n: nt - 1 - n) if reverse else (lambda n: n)

    def body(q_ref, k_ref, v_ref, la_ref, o_ref, st_ref, st_scr):
        @pl.when(pl.program_id(1) == 0)
        def _():
            st_scr[...] = jnp.zeros_like(st_scr)

        st_ref[...] = st_scr[...]
        o, st = _gla_tile(q_ref[...], k_ref[...], v_ref[...], la_ref[...], st_scr[...], reverse)
        o_ref[...] = o
        st_scr[...] = st

    return pl.pallas_call(
        body, name=name, grid=(B_HEADS, nt), in_specs=_gla_in_specs(tile, order, B_HEADS if reverse else 0),
        out_specs=[pl.BlockSpec((tile, B_VAL_DIM), lambda h, n: (order(n), h)),
                   pl.BlockSpec((None, None, B_VAL_DIM, B_KEY_DIM), lambda h, n: (h, order(n), 0, 0))],
        out_shape=[jax.ShapeDtypeStruct((s, D_MODEL), F32), jax.ShapeDtypeStruct((B_HEADS, nt, B_VAL_DIM, B_KEY_DIM), F32)],
        scratch_shapes=[pltpu.VMEM((B_VAL_DIM, B_KEY_DIM), F32)],
        compiler_params=_params(("parallel", "arbitrary")),
    )(proj, proj, proj, la)


def gla_bwd(name, proj, la, states, d_o, reverse, prev=None):
    s = proj.shape[0]
    tile = _row_tile(s)
    nt = s // tile
    order = (lambda n: n) if reverse else (lambda n: nt - 1 - n)

    def body(*refs):
        q_ref, k_ref, v_ref, la_ref, st_ref, do_ref = refs[:6]
        rest = refs[6:]
        prev_refs = rest[:3] if prev is not None else None
        dq_ref, dk_ref, dv_ref, dla_ref, dst_scr = rest[3:] if prev is not None else rest

        @pl.when(pl.program_id(1) == 0)
        def _():
            dst_scr[...] = jnp.zeros_like(dst_scr)

        _, vjp = jax.vjp(functools.partial(_gla_tile, reverse=reverse), q_ref[...], k_ref[...], v_ref[...], la_ref[...], st_ref[...])
        dq, dk, dv, dla, dst = vjp((do_ref[...], dst_scr[...]))
        if prev_refs is not None:
            dq, dk, dv = dq + prev_refs[0][...], dk + prev_refs[1][...], dv + prev_refs[2][...]
        dq_ref[...], dk_ref[...], dv_ref[...], dla_ref[...] = dq, dk, dv, dla
        dst_scr[...] = dst

    key_spec = pl.BlockSpec((tile, B_KEY_DIM), lambda h, n: (order(n), h))
    val_spec = pl.BlockSpec((tile, B_VAL_DIM), lambda h, n: (order(n), h))
    in_specs = _gla_in_specs(tile, order, B_HEADS if reverse else 0) + [
        pl.BlockSpec((None, None, B_VAL_DIM, B_KEY_DIM), lambda h, n: (h, order(n), 0, 0)), val_spec]
    ins = [proj, proj, proj, la, states, d_o]
    if prev is not None:
        in_specs += [key_spec, key_spec, val_spec]
        ins += list(prev)
    return pl.pallas_call(
        body, name=name, grid=(B_HEADS, nt), in_specs=in_specs,
        out_specs=[key_spec, key_spec, val_spec, key_spec],
        out_shape=[jax.ShapeDtypeStruct((s, 512), F32), jax.ShapeDtypeStruct((s, 512), F32),
                   jax.ShapeDtypeStruct((s, D_MODEL), F32), jax.ShapeDtypeStruct((s, 512), F32)],
        scratch_shapes=[pltpu.VMEM((B_VAL_DIM, B_KEY_DIM), F32)],
        compiler_params=_params(("parallel", "arbitrary")),
    )(*ins)


def _r_spec(tile):
    return pl.BlockSpec((tile, D_MODEL), lambda i: (i, 2))


def gla_post_fwd(name, o_f, o_b, proj, gain):
    s = o_f.shape[0]
    tile = _row_tile(s)

    def body(of_ref, ob_ref, r_ref, g_ref, out_ref):
        out_ref[...] = _gla_post(of_ref[...], ob_ref[...], r_ref[...], g_ref[...]).astype(out_ref.dtype)

    row = pl.BlockSpec((tile, D_MODEL), lambda i: (i, 0))
    return pl.pallas_call(body, name=name, grid=(s // tile,),
                          in_specs=[row, row, _r_spec(tile), pl.BlockSpec(gain.shape, lambda i: (0, 0))], out_specs=row,
                          out_shape=jax.ShapeDtypeStruct((s, D_MODEL), BF16), compiler_params=_params(("parallel",)))(o_f, o_b, proj, gain)


def gla_post_bwd(name, o_f, o_b, proj, gain, d_out):
    s = o_f.shape[0]
    tile = _row_tile(s)

    def body(of_ref, ob_ref, r_ref, g_ref, d_ref, do_ref, dr_ref, dg_ref):
        _, vjp = jax.vjp(_gla_post, of_ref[...], ob_ref[...], r_ref[...], g_ref[...])
        d_of, _, dr, dg = vjp(d_ref[...])
        do_ref[...] = d_of
        dr_ref[...] = dr
        _acc_rows(dg_ref, dg)

    row = pl.BlockSpec((tile, D_MODEL), lambda i: (i, 0))
    return pl.pallas_call(
        body, name=name, grid=(s // tile,),
        in_specs=[row, row, _r_spec(tile), pl.BlockSpec(gain.shape, lambda i: (0, 0)), row],
        out_specs=[row, row, pl.BlockSpec((8, D_MODEL), lambda i: (0, 0))],
        out_shape=[jax.ShapeDtypeStruct((s, D_MODEL), F32), jax.ShapeDtypeStruct((s, D_MODEL), F32), jax.ShapeDtypeStruct((8, D_MODEL), F32)],
        compiler_params=_params(("arbitrary",)))(o_f, o_b, proj, gain, d_out)


def _hid(tile, where):
    return pl.BlockSpec((None, tile, FFN_BLK), where)


def _pair(tile, where):
    return pl.BlockSpec((2, None, tile, FFN_BLK), where)


def _w_gu_spec(layer, where_j):
    return pl.BlockSpec((2, None, None, D_MODEL, FFN_BLK), lambda *g: (0, where_j(*g), layer, 0, 0))


def ffn_fwd(name, h_mid, hn2, w_gu, w_down, layer, next_gain=None):
    s = hn2.shape[0]
    tm = _row_tile(s, 1024)
    nt = s // tm

    def gu_body(x_ref, w_ref, gu_ref, act_ref):
        x = x_ref[...]
        g = _dot(x, w_ref[0], NN)
        u = _dot(x, w_ref[1], NN)
        gu_ref[0] = g
        gu_ref[1] = u
        act_ref[...] = _swiglu_act(g, u).astype(act_ref.dtype)

    gu, act = pl.pallas_call(
        gu_body, name=name + "_gu", grid=(4, nt),
        in_specs=[pl.BlockSpec((tm, D_MODEL), lambda j, i: (i, 0)), _w_gu_spec(layer, lambda j, i: j)],
        out_specs=[_pair(tm, lambda j, i: (0, j, i, 0)), _hid(tm, lambda j, i: (j, i, 0))],
        out_shape=[jax.ShapeDtypeStruct((2, 4, s, FFN_BLK), F32), jax.ShapeDtypeStruct((4, s, FFN_BLK), BF16)],
        compiler_params=_params(("parallel", "parallel")))(hn2, w_gu)
    def down_body(*refs):
        act_ref, w_ref, res_ref = refs[:3]
        g_ref = refs[3] if next_gain is not None else None
        o_ref = refs[4] if next_gain is not None else refs[3]
        out = res_ref[...]
        for j in range(4):
            out = out + _dot(act_ref[j], w_ref[j], NN)
        o_ref[...] = out
        if next_gain is not None:
            refs[5][...] = _rms(out, g_ref[...]).astype(BF16)

    row = pl.BlockSpec((tm, D_MODEL), lambda i: (i, 0))
    in_specs = [pl.BlockSpec((4, tm, FFN_BLK), lambda i: (0, i, 0)), pl.BlockSpec((4, FFN_BLK, D_MODEL), lambda i: (0, 0, 0)), row]
    ins = [act, w_down, h_mid]
    out_specs, out_shape = row, jax.ShapeDtypeStruct((s, D_MODEL), F32)
    if next_gain is not None:
        in_specs.append(pl.BlockSpec(next_gain.shape, lambda i: (0, 0)))
        ins.append(next_gain)
        out_specs, out_shape = [row, row], [out_shape, jax.ShapeDtypeStruct((s, D_MODEL), BF16)]
    h_next = pl.pallas_call(down_body, name=name + "_down", grid=(nt,), in_specs=in_specs, out_specs=out_specs,
                            out_shape=out_shape, compiler_params=_params(("parallel",)))(*ins)
    return h_next, gu, act


def ffn_bwd(name, dh_next, hn2, gu, act, w_gu, w_down, layer):
    s = hn2.shape[0]
    tm = _row_tile(s, 1024)
    nt = s // tm
    tw = _row_tile(s, 2048)
    ntw = s // tw
    d_wd = _matmul(name + "_dwd", act, dh_next, dims=TN, grid=(4, ntw), red_axis=1,
                   a_spec=_hid(tw, lambda j, i: (j, i, 0)), b_spec=pl.BlockSpec((tw, D_MODEL), lambda j, i: (i, 0)),
                   o_spec=pl.BlockSpec((None, FFN_BLK, D_MODEL), lambda j, i: (j, 0, 0)),
                   out_shape=jax.ShapeDtypeStruct((4, FFN_BLK, D_MODEL), BF16), acc_shape=(FFN_BLK, D_MODEL))

    def dgu_body(dy_ref, wd_ref, gu_ref, dgu_ref):
        d_act = _dot(dy_ref[...], wd_ref[...], NT)
        g, u = gu_ref[0], gu_ref[1]
        sg = _sigmoid(g)
        silu = g * sg
        dgu_ref[0] = (d_act * u * (sg + silu * (1.0 - sg))).astype(dgu_ref.dtype)
        dgu_ref[1] = (d_act * silu).astype(dgu_ref.dtype)

    d_gu = pl.pallas_call(
        dgu_body, name=name + "_dgu", grid=(4, nt),
        in_specs=[pl.BlockSpec((tm, D_MODEL), lambda j, i: (i, 0)), pl.BlockSpec((None, FFN_BLK, D_MODEL), lambda j, i: (j, 0, 0)),
                  _pair(tm, lambda j, i: (0, j, i, 0))],
        out_specs=_pair(tm, lambda j, i: (0, j, i, 0)), out_shape=jax.ShapeDtypeStruct((2, 4, s, FFN_BLK), BF16),
        compiler_params=_params(("parallel", "parallel")))(dh_next, w_down, gu)

    tx = _row_tile(s, 512)

    def dx_body(d_ref, w_ref, o_ref):
        out = None
        for t in range(2):
            for j in range(4):
                prod = _dot(d_ref[t, j], w_ref[t, j], NT)
                out = prod if out is None else out + prod
        o_ref[...] = out

    d_hn2 = pl.pallas_call(
        dx_body, name=name + "_dx", grid=(s // tx,),
        in_specs=[pl.BlockSpec((2, 4, tx, FFN_BLK), lambda i: (0, 0, i, 0)),
                  pl.BlockSpec((2, 4, None, D_MODEL, FFN_BLK), lambda i: (0, 0, layer, 0, 0))],
        out_specs=pl.BlockSpec((tx, D_MODEL), lambda i: (i, 0)), out_shape=jax.ShapeDtypeStruct((s, D_MODEL), F32),
        compiler_params=_params(("parallel",)))(d_gu, w_gu)

    def dw_body(x_ref, d_ref, o_ref, acc_ref):
        x = x_ref[...]
        k = pl.program_id(1)
        for t in range(2):
            prod = _dot(x, d_ref[t], TN)

            @pl.when(k == 0)
            def _():
                acc_ref[t] = prod

            @pl.when(k > 0)
            def _():
                acc_ref[t] += prod

        @pl.when(k == ntw - 1)
        def _():
            o_ref[...] = acc_ref[...].astype(o_ref.dtype)

    d_wgu = pl.pallas_call(
        dw_body, name=name + "_dwgu", grid=(4, ntw),
        in_specs=[pl.BlockSpec((tw, D_MODEL), lambda j, i: (i, 0)), _pair(tw, lambda j, i: (0, j, i, 0))],
        out_specs=pl.BlockSpec((2, None, D_MODEL, FFN_BLK), lambda j, i: (0, j, 0, 0)),
        out_shape=jax.ShapeDtypeStruct((2, 4, D_MODEL, FFN_BLK), BF16),
        scratch_shapes=[pltpu.VMEM((2, D_MODEL, FFN_BLK), F32)],
        compiler_params=_params(("parallel", "arbitrary")))(hn2, d_gu)
    return d_hn2, d_wgu, d_wd


def _my_place():
    return lax.axis_index("x"), lax.axis_index("y"), lax.axis_index("c")


def _flip(place, k):
    x, y, c = place
    return (1 - x if k & 4 else x, 1 - y if k & 2 else y, 1 - c if k & 1 else c)


def _index(place):
    return 4 * place[0] + 2 * place[1] + place[2]


def all_gather(arrs):
    n = len(arrs)

    def body(*refs):
        ins, outs = refs[:n], refs[n:2 * n]
        send_sems, recv_sems, local_sems = refs[2 * n:]
        me = _my_place()
        sibling = _flip(me, 1)
        chips = (4, 2, 6)

        def copy(a, k, block, to, src=None):
            dst = outs[a].at[_index(block)]
            return pltpu.make_async_remote_copy(src_ref=dst if src is None else src, dst_ref=dst, send_sem=send_sems.at[a, k],
                                                recv_sem=recv_sems.at[a, k], device_id=to, device_id_type=MESH)

        started = []
        for a in range(n):
            mine = pltpu.make_async_copy(ins[a], outs[a].at[_index(me)], local_sems.at[a])
            mine.start()
            started.append(mine)
        first = []
        for a in range(n):
            first.append(copy(a, 0, me, sibling, src=ins[a]))
            first += [copy(a, 1 + j, me, _flip(me, k), src=ins[a]) for j, k in enumerate(chips)]
        for cp in first:
            cp.start()
        passed = []
        for a in range(n):
            for j, k in enumerate(chips):
                copy(a, 1 + j, _flip(me, k), me).wait_recv()
                fwd = copy(a, 4 + j, _flip(me, k), sibling)
                fwd.start()
                passed.append(fwd)
        for a in range(n):
            copy(a, 0, sibling, me).wait_recv()
            for j, k in enumerate(chips):
                copy(a, 4 + j, _flip(sibling, k), me).wait_recv()
        for cp in first + passed:
            cp.wait_send()
        for cp in started:
            cp.wait()

    return pl.pallas_call(
        body, name="all_gather_weights", in_specs=[ANY] * n, out_specs=[ANY] * n,
        out_shape=[jax.ShapeDtypeStruct((N_DEV,) + a.shape, a.dtype) for a in arrs],
        scratch_shapes=[pltpu.SemaphoreType.DMA((n, 7)), pltpu.SemaphoreType.DMA((n, 7)), pltpu.SemaphoreType.DMA((n,))],
    )(*arrs)


def exchange_partials(arrs):
    n = len(arrs)

    def body(*refs):
        ins, outs = refs[:n], refs[n:2 * n]
        send_sems, recv_sems, local_sems = refs[2 * n:]
        me = _my_place()
        local = []
        for a in range(n):
            cp = pltpu.make_async_copy(ins[a].at[_index(me)], outs[a].at[_index(me)], local_sems.at[a])
            cp.start()
            local.append(cp)

        def copy(a, k, src_block, dst_block):
            return pltpu.make_async_remote_copy(src_ref=ins[a].at[_index(src_block)], dst_ref=outs[a].at[_index(dst_block)],
                                                send_sem=send_sems.at[a, k - 1], recv_sem=recv_sems.at[a, k - 1],
                                                device_id=_flip(me, k), device_id_type=MESH)

        sent = []
        for a in range(n):
            for k in range(1, N_DEV):
                cp = copy(a, k, _flip(me, k), me)
                cp.start()
                sent.append(cp)
        for a in range(n):
            for k in range(1, N_DEV):
                copy(a, k, me, _flip(me, k)).wait_recv()
        for cp in sent:
            cp.wait_send()
        for cp in local:
            cp.wait()

    return pl.pallas_call(
        body, name="exchange_weight_grads", in_specs=[ANY] * n, out_specs=[ANY] * n,
        out_shape=[jax.ShapeDtypeStruct(a.shape, a.dtype) for a in arrs],
        scratch_shapes=[pltpu.SemaphoreType.DMA((n, 7)), pltpu.SemaphoreType.DMA((n, 7)), pltpu.SemaphoreType.DMA((n,))],
    )(*arrs)


def adamw_shard(name, parts, w, m, v, layer, tile):
    rows, cols = parts.shape[1:]
    assert rows % tile == 0
    off = layer * (rows // tile)

    def body(p_ref, w_ref, m_ref, v_ref, g_ref, d_ref, nm_ref, nv_ref):
        g = p_ref[0].astype(F32)
        for src in range(1, N_DEV):
            g = g + p_ref[src].astype(F32)
        g_ref[...] = g
        d_ref[...], nm_ref[...], nv_ref[...] = _adamw(w_ref[...], g, m_ref[...], v_ref[...])

    src_row = pl.BlockSpec((tile, cols), lambda i: (off + i, 0))
    row = pl.BlockSpec((tile, cols), lambda i: (i, 0))
    shape = jax.ShapeDtypeStruct((rows, cols), F32)
    return pl.pallas_call(body, name=name, grid=(rows // tile,),
                          in_specs=[pl.BlockSpec((N_DEV, tile, cols), lambda i: (0, i, 0)), src_row, src_row, src_row],
                          out_specs=[row] * 4, out_shape=[shape] * 4, compiler_params=_params(("parallel",)))(parts, w, m, v)


def allreduce_adamw_replicated(partial, w, m, v, n_loss_rows):
    rows = partial.shape[0]

    def body(p_ref, w_ref, m_ref, v_ref, g_ref, d_ref, nm_ref, nv_ref, loss_ref, recv_ref, send_sems, recv_sems):
        me = _my_place()
        recv_ref[_index(me)] = p_ref[...]
        copies = []
        for k in range(1, N_DEV):
            peer = _flip(me, k)
            cp = pltpu.make_async_remote_copy(src_ref=p_ref, dst_ref=recv_ref.at[_index(me)], send_sem=send_sems.at[k - 1],
                                              recv_sem=recv_sems.at[k - 1], device_id=peer, device_id_type=MESH)
            cp.start()
            copies.append((cp, peer))
        for k, (cp, peer) in enumerate(copies):
            pltpu.make_async_remote_copy(src_ref=p_ref, dst_ref=recv_ref.at[_index(peer)], send_sem=send_sems.at[k],
                                         recv_sem=recv_sems.at[k], device_id=peer, device_id_type=MESH).wait_recv()
        for cp, _ in copies:
            cp.wait_send()
        g = recv_ref[0]
        for src in range(1, N_DEV):
            g = g + recv_ref[src]
        g_ref[...] = g
        d_ref[...], nm_ref[...], nv_ref[...] = _adamw(w_ref[...], g, m_ref[...], v_ref[...])
        loss = (0.5 / D_MODEL) * jnp.sum(g[rows - n_loss_rows:, :])
        loss_ref[...] = jnp.full(loss_ref.shape, loss, F32)

    shape = jax.ShapeDtypeStruct((rows, LANES), F32)
    return pl.pallas_call(
        body, name="allreduce_adamw_replicated", in_specs=[VMEM_SPEC] * 4, out_specs=[VMEM_SPEC] * 5,
        out_shape=[shape] * 4 + [jax.ShapeDtypeStruct((8, LANES), F32)],
        scratch_shapes=[pltpu.VMEM((N_DEV, rows, LANES), F32), pltpu.SemaphoreType.DMA((7,)), pltpu.SemaphoreType.DMA((7,))],
    )(partial, w, m, v)


def _pack_rows(flat, cols):
    n = flat.shape[-1]
    rows = -(-n // cols)
    rows = -(-rows // 48) * 48
    flat = jnp.pad(flat, [(0, 0)] * (flat.ndim - 1) + [(0, rows * cols - n)])
    return flat.reshape(flat.shape[:-1] + (rows, cols))


def _unpack(flat, shapes):
    out, off = [], 0
    for shp in shapes:
        n = 1
        for d in shp:
            n *= d
        out.append(flat[off:off + n].reshape(shp))
        off += n
    return out


def _to_dev_cols(a):
    w = a.shape[-1] // N_DEV
    return jnp.moveaxis(a.reshape(a.shape[:-1] + (N_DEV, w)), -2, 0)


def _from_dev_cols(a):
    a = jnp.moveaxis(a, 0, -2)
    return a.reshape(a.shape[:-2] + (a.shape[-2] * a.shape[-1],))


def kernel(x, attn_norm, ffn_norm, a_w_in, a_q_norm, a_k_norm, a_w_out, b_w_in, b_w_gate_f, b_gate_bias_f, b_w_gate_b, b_gate_bias_b, b_out_norm, b_w_out, ffn_w_gate_up, ffn_w_down, loss_target, m_attn_norm, m_ffn_norm, m_a_w_in, m_a_q_norm, m_a_k_norm, m_a_w_out, m_b_w_in, m_b_w_gate_f, m_b_gate_bias_f, m_b_w_gate_b, m_b_gate_bias_b, m_b_out_norm, m_b_w_out, m_ffn_w_gate_up, m_ffn_w_down, v_attn_norm, v_ffn_norm, v_a_w_in, v_a_q_norm, v_a_k_norm, v_a_w_out, v_b_w_in, v_b_w_gate_f, v_b_gate_bias_f, v_b_w_gate_b, v_b_gate_bias_b, v_b_out_norm, v_b_w_out, v_ffn_w_gate_up, v_ffn_w_down):
    seq = x.shape[1]
    depth = attn_norm.shape[0]
    h = x.reshape(seq, D_MODEL)
    target = loss_target.reshape(seq, D_MODEL)
    n_a, n_b = a_w_in.shape[0], b_w_in.shape[0]

    small = jnp.concatenate([t.reshape(-1) for t in (b_w_gate_f, b_gate_bias_f, b_w_gate_b, b_gate_bias_b, b_out_norm)])
    small = _pack_rows(small, LANES)
    g_a_in, g_a_out, g_b_in, g_b_out, g_gu, g_down, g_small = all_gather(
        [a_w_in.astype(BF16), a_w_out.astype(BF16), b_w_in.astype(BF16), b_w_out.astype(BF16),
         ffn_w_gate_up.astype(BF16), ffn_w_down.astype(BF16), small])
    w_a_in = _from_dev_cols(g_a_in)
    w_a_out = jnp.moveaxis(g_a_out, 0, 1).reshape(n_a, D_MODEL, D_MODEL)
    w_b_in = _from_dev_cols(g_b_in)
    w_b_out = jnp.moveaxis(g_b_out, 0, 1).reshape(n_b, D_MODEL, D_MODEL)
    w_down = jnp.moveaxis(g_down, 0, 1).reshape(depth, 4, FFN_BLK, D_MODEL)
    w_gu = g_gu.reshape(2, 4, depth, D_MODEL, FFN_BLK)
    small_shapes = [t.shape for t in (b_w_gate_f, b_gate_bias_f, b_w_gate_b, b_gate_bias_b, b_out_norm)]
    per_dev = [_unpack(g_small[d].reshape(-1), small_shapes) for d in range(N_DEV)]
    wgf, bgf, wgb, bgb, onorm = [_from_dev_cols(jnp.stack([per_dev[d][t] for d in range(N_DEV)])) for t in range(5)]
    w_gate = jnp.zeros((n_b, LANES, D_MODEL), F32)
    w_gate = w_gate.at[:, 0:16, 0:512].set(wgf).at[:, 16:32, 512:1024].set(wgb)
    gate_bias = jnp.concatenate([bgf, bgb], axis=1).reshape(n_b, 1, D_MODEL)
    out_gain = onorm.reshape(n_b, 1, D_MODEL)
    w_b_main = w_b_in[:, :, :3072]
    w_b_z = jnp.pad(w_b_in[:, :, 3072:], ((0, 0), (0, 0), (0, LANES - 32)))

    dh, sq_err, grads = _forward_backward(h, target, attn_norm, ffn_norm, a_q_norm, a_k_norm, w_a_in, w_a_out, w_b_main, w_b_z,
                                          w_gate, gate_bias, out_gain, w_b_out, w_gu, w_down)
    sharded_w = (a_w_in, a_w_out, b_w_in, b_w_gate_f, b_gate_bias_f, b_w_gate_b, b_gate_bias_b, b_out_norm, b_w_out, ffn_w_gate_up, ffn_w_down)
    sharded_m = (m_a_w_in, m_a_w_out, m_b_w_in, m_b_w_gate_f, m_b_gate_bias_f, m_b_w_gate_b, m_b_gate_bias_b, m_b_out_norm, m_b_w_out, m_ffn_w_gate_up, m_ffn_w_down)
    sharded_v = (v_a_w_in, v_a_w_out, v_b_w_in, v_b_w_gate_f, v_b_gate_bias_f, v_b_w_gate_b, v_b_gate_bias_b, v_b_out_norm, v_b_w_out, v_ffn_w_gate_up, v_ffn_w_down)
    rep_w = (attn_norm, ffn_norm, a_q_norm, a_k_norm)
    rep_m = (m_attn_norm, m_ffn_norm, m_a_q_norm, m_a_k_norm)
    rep_v = (v_attn_norm, v_ffn_norm, v_a_q_norm, v_a_k_norm)
    loss, outs = _reduce_and_update(grads, sq_err, sharded_w, sharded_m, sharded_v, rep_w, rep_m, rep_v)
    return (loss, dh.reshape(x.shape), *outs)


def _forward_backward(h, target, attn_norm, ffn_norm, a_q_norm, a_k_norm, w_a_in, w_a_out, w_b_main, w_b_z, w_gate, gate_bias,
                      out_gain, w_b_out, w_gu, w_down):
    seq = h.shape[0]
    depth = attn_norm.shape[0]
    n_a, n_b = w_a_in.shape[0], w_b_main.shape[0]
    tab = _rope_table(seq)
    pair_gain = lambda g: jnp.concatenate([g, g]).reshape(1, LANES)

    saved = []
    for i in range(depth):
        j = i // 2
        nm = f"l{i}"
        if i == 0:
            hn = rmsnorm_fwd(nm + "_norm1", h, attn_norm[i].reshape(1, D_MODEL))
        if i % 2 == 0:
            qkv = mm_nn(nm + "_qkv", hn, w_a_in[j])
            os_, lses = [], []
            for g, (_, dil) in enumerate(A_GROUPS):
                o, lse = attn_fwd(f"{nm}_attn{g}", qkv, tab, pair_gain(a_q_norm[j, g]), pair_gain(a_k_norm[j, g]), g, dil)
                os_.append(o)
                lses.append(lse)
            mixed = attn_merge_fwd(nm + "_merge", os_, lses)
            h_mid, hn2 = mm_nn(nm + "_out", mixed, w_a_out[j], res=h, norm_gain=ffn_norm[i].reshape(1, D_MODEL))
            mix_saved = (qkv, os_, lses, mixed)
        else:
            proj = mm_nn(nm + "_proj", hn, w_b_main[j])
            z = mm_nn(nm + "_z", hn, w_b_z[j])
            la = gla_gate_fwd(nm + "_gate", z, w_gate[j], gate_bias[j])
            o_f, st_f = gla_fwd(nm + "_gla_f", proj, la, False)
            o_b, st_b = gla_fwd(nm + "_gla_b", proj, la, True)
            mixed = gla_post_fwd(nm + "_post", o_f, o_b, proj, out_gain[j])
            h_mid, hn2 = mm_nn(nm + "_out", mixed, w_b_out[j], res=h, norm_gain=ffn_norm[i].reshape(1, D_MODEL))
            mix_saved = (proj, z, la, o_f, st_f, o_b, st_b, mixed)
        next_gain = attn_norm[i + 1].reshape(1, D_MODEL) if i + 1 < depth else None
        h_next, gu, act = ffn_fwd(nm + "_ffn", h_mid, hn2, w_gu, w_down[i], i, next_gain)
        saved.append((h, hn, mix_saved, h_mid, hn2, gu, act))
        if next_gain is not None:
            h_next, hn = h_next
        h = h_next

    dh, sq_err = loss_and_grad(h, target)

    g_attn_norm, g_ffn_norm = [None] * depth, [None] * depth
    g_a_w_in, g_a_w_out, g_a_q, g_a_k = [None] * n_a, [None] * n_a, [None] * n_a, [None] * n_a
    g_b_w_in, g_b_w_out, g_w_gate, g_gate_bias, g_out_gain = ([None] * n_b for _ in range(5))
    g_w_gu, g_w_down = [None] * depth, [None] * depth
    for i in reversed(range(depth)):
        j = i // 2
        nm = f"l{i}b"
        h_in, hn, mix_saved, h_mid, hn2, gu, act = saved[i]
        d_hn2, g_w_gu[i], g_w_down[i] = ffn_bwd(nm + "_ffn", dh, hn2, gu, act, w_gu, w_down[i], i)
        dh_mid, g_ffn_norm[i] = rmsnorm_bwd(nm + "_norm2", h_mid, ffn_norm[i].reshape(1, D_MODEL), d_hn2, dh)
        if i % 2 == 0:
            qkv, os_, lses, mixed = mix_saved
            d_mixed = mm_nt(nm + "_dmixed", dh_mid, w_a_out[j])
            g_a_w_out[j] = mm_tn(nm + "_dwout", mixed, dh_mid, by_block=True)
            d_parts = attn_merge_bwd(nm + "_merge", os_, lses, d_mixed)
            gq_l, gk_l = [], []
            d_qkv = lax.empty(qkv.shape, BF16)
            for g, (_, dil) in enumerate(A_GROUPS):
                d_qkv, dk_parts, dv_parts, dgq, dgk = attn_bwd(f"{nm}_attn{g}", qkv, tab, pair_gain(a_q_norm[j, g]),
                                                               pair_gain(a_k_norm[j, g]), os_[g], lses[g], d_parts[g], d_parts[3 + g], g, dil, d_qkv)
                d_qkv = attn_combine(f"{nm}_dk{g}", dk_parts, seq, dil, d_qkv, 3 * g + 1)
                d_qkv = attn_combine(f"{nm}_dv{g}", dv_parts, seq, dil, d_qkv, 3 * g + 2)
                gq_l.append(dgq[0, :A_HEAD_DIM])
                gk_l.append(dgk[0, :A_HEAD_DIM])
            g_a_q[j], g_a_k[j] = jnp.stack(gq_l), jnp.stack(gk_l)
            d_hn = mm_nt(nm + "_dhn", d_qkv, w_a_in[j])
            g_a_w_in[j] = mm_tn(nm + "_dwin", hn, d_qkv, tn=D_MODEL * 9 // N_DEV, by_block=True)
        else:
            proj, z, la, o_f, st_f, o_b, st_b, mixed = mix_saved
            d_mixed = mm_nt(nm + "_dmixed", dh_mid, w_b_out[j])
            g_b_w_out[j] = mm_tn(nm + "_dwout", mixed, dh_mid, by_block=True)
            d_o, d_r, dgain = gla_post_bwd(nm + "_post", o_f, o_b, proj, out_gain[j], d_mixed)
            g_out_gain[j] = dgain[0]
            dq, dk, dv, dla_f = gla_bwd(nm + "_gla_f", proj, la, st_f, d_o, False)
            dq, dk, dv, dla_b = gla_bwd(nm + "_gla_b", proj, la, st_b, d_o, True, prev=(dq, dk, dv))
            d_z, g_w_gate[j], dbias = gla_gate_bwd(nm + "_gate", z, w_gate[j], gate_bias[j], dla_f, dla_b)
            g_gate_bias[j] = dbias[0]
            d_proj = jnp.concatenate([dq, dk, dv, d_r], axis=1)
            d_hn = mm_nt(nm + "_dhn_z", d_z, w_b_z[j])
            d_hn = mm_nt(nm + "_dhn", d_proj, w_b_main[j], res=d_hn)
            g_b_w_in[j] = jnp.concatenate([mm_tn(nm + "_dwin", hn, d_proj), mm_tn(nm + "_dwz", hn, d_z)[:, :32]], axis=1)
        dh, g_attn_norm[i] = rmsnorm_bwd(nm + "_norm1", h_in, attn_norm[i].reshape(1, D_MODEL), d_hn, dh_mid)
    return dh, sq_err, (g_attn_norm, g_ffn_norm, g_a_w_in, g_a_w_out, g_a_q, g_a_k, g_b_w_in, g_b_w_out, g_w_gate, g_gate_bias,
                        g_out_gain, g_w_gu, g_w_down)


def _reduce_and_update(grads, sq_err, sharded_w, sharded_m, sharded_v, rep_w, rep_m, rep_v):
    (g_attn_norm, g_ffn_norm, g_a_w_in, g_a_w_out, g_a_q, g_a_k, g_b_w_in, g_b_w_out, g_w_gate, g_gate_bias, g_out_gain,
     g_w_gu, g_w_down) = grads
    depth, n_a, n_b = len(g_w_gu), len(g_a_w_in), len(g_b_w_in)

    g_w_gate = jnp.stack(g_w_gate)
    g_gate_bias = jnp.stack(g_gate_bias)
    small_parts = [_to_dev_cols(g_w_gate[:, 0:16, 0:512]), _to_dev_cols(g_gate_bias[:, 0:512]),
                   _to_dev_cols(g_w_gate[:, 16:32, 512:1024]), _to_dev_cols(g_gate_bias[:, 512:1024]),
                   _to_dev_cols(jnp.stack(g_out_gain).reshape(n_b, B_HEADS, B_VAL_DIM))]
    small_part = _pack_rows(jnp.concatenate([t.reshape(N_DEV, -1) for t in small_parts], axis=1), LANES).astype(BF16)
    families = [
        (0, g_a_w_in, 256),
        (1, [t.reshape(N_DEV, -1, D_MODEL) for t in g_a_w_out], 128),
        (2, [_to_dev_cols(t).astype(BF16) for t in g_b_w_in], 256),
        (8, [t.reshape(N_DEV, -1, D_MODEL) for t in g_b_w_out], 128),
        (9, [t.reshape(N_DEV, D_MODEL, FFN_BLK) for t in g_w_gu], 256),
        (10, [t.reshape(N_DEV, -1, D_MODEL) for t in g_w_down], 176),
    ]
    flat_parts = [p for _, parts, _ in families for p in parts] + [small_part]
    received = exchange_partials(flat_parts)
    sh_out = [None] * len(sharded_w)
    pos = 0
    for fam, parts, tile in families:
        w = sharded_w[fam]
        two_d = lambda t: t.reshape(-1, t.shape[-1])
        per_layer = []
        for layer in range(len(parts)):
            per_layer.append(adamw_shard(f"adamw_p{fam}_l{layer}", received[pos], two_d(w), two_d(sharded_m[fam]),
                                         two_d(sharded_v[fam]), layer, tile))
            pos += 1
        sh_out[fam] = [jnp.stack([per_layer[l][t] for l in range(len(parts))]).reshape(w.shape) for t in range(4)]
    small_ids = (3, 4, 5, 6, 7)
    pack_small = lambda ts: _pack_rows(jnp.concatenate([ts[i].reshape(-1) for i in small_ids]), LANES)
    small_out = adamw_shard("adamw_small", received[pos], pack_small(sharded_w), pack_small(sharded_m), pack_small(sharded_v), 0, 48)
    small_shapes = [sharded_w[i].shape for i in small_ids]
    for t in range(4):
        for i, val in zip(small_ids, _unpack(small_out[t].reshape(-1), small_shapes)):
            if sh_out[i] is None:
                sh_out[i] = [None] * 4
            sh_out[i][t] = val
    sh_grad, sh_delta, sh_m, sh_v = [[sh_out[i][t] for i in range(len(sharded_w))] for t in range(4)]

    rep_g = (jnp.stack([t[0] for t in g_attn_norm]), jnp.stack([t[0] for t in g_ffn_norm]), jnp.stack(g_a_q), jnp.stack(g_a_k))
    n_rep = sum(t.size for t in rep_w)
    n_rep_rows = -(-n_rep // (8 * LANES)) * 8
    n_loss_rows = 8 * D_MODEL // LANES

    def pack_rep(ts, tail):
        flat = jnp.concatenate([t.reshape(-1) for t in ts])
        flat = jnp.pad(flat, (0, n_rep_rows * LANES - n_rep))
        return jnp.concatenate([flat.reshape(n_rep_rows, LANES), tail], axis=0)

    zeros_tail = jnp.zeros((n_loss_rows, LANES), F32)
    rep_out = allreduce_adamw_replicated(pack_rep(rep_g, sq_err.reshape(n_loss_rows, LANES)), pack_rep(rep_w, zeros_tail),
                                         pack_rep(rep_m, zeros_tail), pack_rep(rep_v, zeros_tail + 1.0), n_loss_rows)
    rep_shapes = [t.shape for t in rep_w]
    r_grad, r_delta, r_m, r_v = [_unpack(p.reshape(-1), rep_shapes) for p in rep_out[:4]]
    loss = rep_out[4][0, 0]

    def ordered(rep, sh):
        return [rep[0], rep[1], sh[0], rep[2], rep[3]] + list(sh[1:])

    return loss, (*ordered(r_grad, sh_grad), *ordered(r_delta, sh_delta), *ordered(r_m, sh_m), *ordered(r_v, sh_v))
```

```python
import functools

import jax
import jax.numpy as jnp
from jax import lax
from jax.experimental import pallas as pl
from jax.experimental.pallas import tpu as pltpu

F32 = jnp.float32
BF16 = jnp.bfloat16
MXU_DTYPE = jnp.bfloat16

D_MODEL = 1024
N_DEV = 8
RMS_EPS = 1e-6
NEG_INF = -1e30
A_GROUPS = ((128, 1), (512, 4), (2048, 16))
A_HEAD_DIM = 64
A_HALF = 64
ATT_T = 128
ATT_TILE = 2048
ATT_FWD_BATCH = 8
ATT_BWD_BATCH = 4
B_HEADS = 4
B_KEY_DIM = 128
B_VAL_DIM = 256
B_CHUNK = 64
B_GATE_TAU = 16.0
FFN_HIDDEN = 2816
FFN_BLK = 2 * FFN_HIDDEN // N_DEV
ADAM_LR, ADAM_B1, ADAM_B2, ADAM_EPS, ADAM_WD, ADAM_STEP = 0.001, 0.9, 0.999, 1e-08, 0.01, 10
ROPE_THETA = 10000.0

V7X_VMEM_LIMIT = 60 * 1024 * 1024
LANES = 128
MESH = pl.DeviceIdType.MESH
ANY = pl.BlockSpec(memory_space=pl.ANY)
VMEM_SPEC = pl.BlockSpec(memory_space=pltpu.VMEM)

NN = ((1,), (0,))
NT = ((1,), (1,))
TN = ((0,), (0,))


def _dot(a, b, dims):
    return lax.dot_general(a.astype(MXU_DTYPE), b.astype(MXU_DTYPE), (dims, ((), ())), preferred_element_type=F32)


@jax.custom_vjp
def dot_nn(a, b):
    return _dot(a, b, NN)


@jax.custom_vjp
def dot_nt(a, b):
    return _dot(a, b, NT)


@jax.custom_vjp
def dot_tn(a, b):
    return _dot(a, b, TN)


dot_nn.defvjp(lambda a, b: (_dot(a, b, NN), (a, b)), lambda r, g: (dot_nt(g, r[1]), dot_tn(r[0], g)))
dot_nt.defvjp(lambda a, b: (_dot(a, b, NT), (a, b)), lambda r, g: (dot_nn(g, r[1]), dot_tn(g, r[0])))
dot_tn.defvjp(lambda a, b: (_dot(a, b, TN), (a, b)), lambda r, g: (dot_nt(r[1], g), dot_nn(r[0], g)))


def _bdot(a, b, dims):
    dn = (tuple((d[0] + 1,) for d in dims), ((0,), (0,)))
    return lax.dot_general(a.astype(MXU_DTYPE), b.astype(MXU_DTYPE), dn, preferred_element_type=F32)


@jax.custom_vjp
def bdot_nn(a, b):
    return _bdot(a, b, NN)


@jax.custom_vjp
def bdot_nt(a, b):
    return _bdot(a, b, NT)


@jax.custom_vjp
def bdot_tn(a, b):
    return _bdot(a, b, TN)


bdot_nn.defvjp(lambda a, b: (_bdot(a, b, NN), (a, b)), lambda r, g: (bdot_nt(g, r[1]), bdot_tn(r[0], g)))
bdot_nt.defvjp(lambda a, b: (_bdot(a, b, NT), (a, b)), lambda r, g: (bdot_nn(g, r[1]), bdot_tn(g, r[0])))
bdot_tn.defvjp(lambda a, b: (_bdot(a, b, TN), (a, b)), lambda r, g: (bdot_nt(r[1], g), bdot_nn(r[0], g)))


def _dot_f32(a, b):
    return lax.dot_general(a, b, (NN, ((), ())), precision=lax.Precision.HIGHEST, preferred_element_type=F32)


def _tri(n, upper):
    r = lax.broadcasted_iota(jnp.int32, (n, n), 0)
    c = lax.broadcasted_iota(jnp.int32, (n, n), 1)
    return jnp.where((c >= r) if upper else (c <= r), 1.0, 0.0).astype(F32)


def _chunk_cumsum(x, reverse):
    tri = jnp.broadcast_to(_tri(x.shape[1], reverse), (x.shape[0], x.shape[1], x.shape[1]))
    return lax.dot_general(tri, x, (((2,), (1,)), ((0,), (0,))), precision=lax.Precision.HIGHEST, preferred_element_type=F32)


@functools.partial(jax.custom_vjp, nondiff_argnums=(1,))
def cumsum_chunks(x, reverse):
    return _chunk_cumsum(x, reverse)


cumsum_chunks.defvjp(lambda x, reverse: (_chunk_cumsum(x, reverse), None), lambda reverse, _, g: (_chunk_cumsum(g, not reverse),))


def _swap32_raw(x):
    lane = lax.broadcasted_iota(jnp.int32, x.shape, 1)
    return jnp.where((lane % 64) < 32, pltpu.roll(x, 96, 1), pltpu.roll(x, 32, 1))


@jax.custom_vjp
def swap32(x):
    return _swap32_raw(x)


swap32.defvjp(lambda x: (_swap32_raw(x), None), lambda _, g: (_swap32_raw(g),))


def _rms(x, gain):
    return x * lax.rsqrt(jnp.mean(x * x, axis=-1, keepdims=True) + RMS_EPS) * gain


def _sigmoid(x):
    return 1.0 / (1.0 + jnp.exp(-x))


def _log_sigmoid(x):
    return jnp.minimum(x, 0.0) - jnp.log(1.0 + jnp.exp(-jnp.abs(x)))


def _qk_prep(x, tab, gain):
    lo = lax.broadcasted_iota(jnp.int32, (1, LANES), 1) < A_HEAD_DIM
    x2 = x * x
    s_lo = jnp.sum(jnp.where(lo, x2, 0.0), axis=-1, keepdims=True)
    s_hi = jnp.sum(jnp.where(lo, 0.0, x2), axis=-1, keepdims=True)
    xn = (x * lax.rsqrt(jnp.where(lo, s_lo, s_hi) / A_HEAD_DIM + RMS_EPS)) * gain
    return xn * tab[:, :LANES] + swap32(xn) * tab[:, LANES:]


def _stack_heads(x):
    lo = lax.broadcasted_iota(jnp.int32, (1, LANES), 1) < A_HEAD_DIM
    return jnp.concatenate([jnp.where(lo, x, 0.0), jnp.where(lo, 0.0, x)], axis=-2)


def _unstack_heads(x):
    t = x.shape[-2] // 2
    lo = lax.broadcasted_iota(jnp.int32, (1, LANES), 1) < A_HEAD_DIM
    return jnp.where(lo, x[..., :t, :], x[..., t:, :])


def _attn_scores(q, k, valid):
    s = _bdot(_stack_heads(q), k, NT) * (A_HEAD_DIM ** -0.5)
    return jnp.where(valid, s, NEG_INF)


def _attn_job(q, k, v, valid):
    s = _attn_scores(q, k, valid)
    mx = jnp.max(s, axis=-1, keepdims=True)
    p = jnp.exp(s - mx)
    l = jnp.sum(p, axis=-1, keepdims=True)
    out = _unstack_heads(_bdot(p, v, NN) / l)
    lse = mx + jnp.log(l)
    t = q.shape[1]
    lo = lax.broadcasted_iota(jnp.int32, (1, LANES), 1) < A_HEAD_DIM
    return out, jnp.where(lo, lse[:, :t], lse[:, t:])


def _attn_job_bwd(q, k, v, valid, d_out, out, lse, d_lse):
    qs = _stack_heads(q)
    lo = lax.broadcasted_iota(jnp.int32, (1, LANES), 1) < A_HEAD_DIM
    other = pltpu.roll(lse, A_HEAD_DIM, 2)
    row_lse = jnp.concatenate([jnp.where(lo, lse, other), jnp.where(lo, other, lse)], axis=1)
    p = jnp.exp(_attn_scores(q, k, valid) - jnp.concatenate([row_lse, row_lse], axis=2))
    dos = _stack_heads(d_out)
    dv = _bdot(p, dos, TN)
    dp = _bdot(dos, v, NT)
    inner = jnp.sum(dos * _stack_heads(out) - _stack_heads(d_lse), axis=-1, keepdims=True)
    ds = p * (dp - inner) * (A_HEAD_DIM ** -0.5)
    return _unstack_heads(_bdot(ds, k, NN)), _bdot(ds, qs, TN), dv


def _merge_groups(o0, o1, o2, l0, l1, l2):
    mx = lax.stop_gradient(jnp.maximum(jnp.maximum(l0, l1), l2))
    e0, e1, e2 = jnp.exp(l0 - mx), jnp.exp(l1 - mx), jnp.exp(l2 - mx)
    den = e0 + e1 + e2
    return (e0 / den) * o0 + (e1 / den) * o1 + (e2 / den) * o2


def _gla_tile(q, k, v, la, st, reverse):
    t = q.shape[0]
    nc = t // B_CHUNK
    split = lambda x: x.reshape(nc, B_CHUNK, x.shape[1])
    q, k, v, la = split(q * (B_KEY_DIM ** -0.5)), split(k), split(v), split(la)
    r = lax.broadcasted_iota(jnp.int32, (1, B_CHUNK, B_CHUNK), 1)
    c = lax.broadcasted_iota(jnp.int32, (1, B_CHUNK, B_CHUNK), 2)
    mask = (c > r) if reverse else (c <= r)
    b = cumsum_chunks(la, reverse)
    tot = jnp.sum(la, axis=1, keepdims=True)
    q_t = q * jnp.exp(b)
    k_t = k * jnp.exp(-b)
    k_end = k * jnp.exp(tot - b)
    attn = jnp.where(mask, bdot_nt(q_t, k_t), 0.0)
    kv = bdot_tn(v, k_end)
    decay = jnp.exp(tot)
    entering = [None] * nc
    for ci in (range(nc - 1, -1, -1) if reverse else range(nc)):
        entering[ci] = st
        st = st * decay[ci] + kv[ci]
    out = bdot_nn(attn, v) + bdot_nt(q_t, jnp.concatenate([e[None] for e in entering], axis=0))
    return out.reshape(t, out.shape[2]), st


def _gla_post(o_f, o_b, r, gain):
    o = o_f + o_b
    heads = [_rms(o[:, h * B_VAL_DIM:(h + 1) * B_VAL_DIM], gain[:, h * B_VAL_DIM:(h + 1) * B_VAL_DIM]) for h in range(B_HEADS)]
    return jnp.concatenate(heads, axis=1) * (r * _sigmoid(r))


def _gate(z, wg, bias):
    return _log_sigmoid(dot_nn(z, wg) + bias) / B_GATE_TAU


def _swiglu_act(g, u):
    return (g * _sigmoid(g)) * u


def _adamw(w, g, m, v):
    m = ADAM_B1 * m + (1.0 - ADAM_B1) * g
    v = ADAM_B2 * v + (1.0 - ADAM_B2) * jnp.square(g)
    m_hat = m / (1.0 - ADAM_B1 ** ADAM_STEP)
    v_hat = v / (1.0 - ADAM_B2 ** ADAM_STEP)
    delta = -ADAM_LR * (m_hat / (jnp.sqrt(v_hat) + ADAM_EPS) + ADAM_WD * w)
    return delta, m, v


def _params(sem=None):
    return pltpu.CompilerParams(dimension_semantics=sem, vmem_limit_bytes=V7X_VMEM_LIMIT)


def _row_tile(s, want=512):
    t = min(want, s)
    assert s % t == 0
    return t


def _matmul(name, a, b, *, dims, grid, a_spec, b_spec, o_spec, out_shape, red_axis=None, res=None, res_spec=None, acc_shape=None,
            norm_gain=None):
    n_red = grid[red_axis] if red_axis is not None else 1
    n_in = 2 + (res is not None) + (norm_gain is not None)

    def body(*refs):
        a_ref, b_ref = refs[0], refs[1]
        r_ref = refs[2] if res is not None else None
        g_ref = refs[n_in - 1] if norm_gain is not None else None
        o_ref = refs[n_in]
        n_ref = refs[n_in + 1] if norm_gain is not None else None
        prod = lax.dot_general(a_ref[...].astype(MXU_DTYPE), b_ref[...].astype(MXU_DTYPE), (dims, ((), ())),
                               preferred_element_type=F32)
        if red_axis is None:
            if r_ref is not None:
                prod = prod + r_ref[...]
            o_ref[...] = prod.astype(o_ref.dtype)
            if n_ref is not None:
                n_ref[...] = _rms(prod, g_ref[...]).astype(n_ref.dtype)
            return
        acc = refs[-1] if acc_shape is not None else o_ref
        k = pl.program_id(red_axis)

        @pl.when(k == 0)
        def _():
            acc[...] = prod + r_ref[...] if r_ref is not None else prod

        @pl.when(k > 0)
        def _():
            acc[...] += prod

        if acc_shape is not None or n_ref is not None:
            @pl.when(k == n_red - 1)
            def _():
                if acc_shape is not None:
                    o_ref[...] = acc[...].astype(o_ref.dtype)
                if n_ref is not None:
                    n_ref[...] = _rms(acc[...], g_ref[...]).astype(n_ref.dtype)

    ins = [a, b] + ([res] if res is not None else []) + ([norm_gain] if norm_gain is not None else [])
    specs = [a_spec, b_spec] + ([res_spec] if res is not None else [])
    out_specs, out_shapes = o_spec, out_shape
    if norm_gain is not None:
        specs.append(pl.BlockSpec(norm_gain.shape, lambda *g: (0, 0)))
        out_specs, out_shapes = [o_spec, o_spec], [out_shape, jax.ShapeDtypeStruct(out_shape.shape, BF16)]
    sem = tuple("arbitrary" if i == red_axis else "parallel" for i in range(len(grid)))
    return pl.pallas_call(body, name=name, grid=grid, in_specs=specs, out_specs=out_specs, out_shape=out_shapes,
                          scratch_shapes=[pltpu.VMEM(acc_shape, F32)] if acc_shape is not None else [],
                          compiler_params=_params(sem))(*ins)


def mm_nn(name, x, w, *, res=None, out_dtype=F32, tn=1024, norm_gain=None):
    m, k = x.shape
    n = w.shape[1]
    tm, tn = _row_tile(m, 1024), min(tn, n)
    return _matmul(name, x, w, dims=NN, grid=(n // tn, m // tm),
                   a_spec=pl.BlockSpec((tm, k), lambda j, i: (i, 0)), b_spec=pl.BlockSpec((k, tn), lambda j, i: (0, j)),
                   o_spec=pl.BlockSpec((tm, tn), lambda j, i: (i, j)), out_shape=jax.ShapeDtypeStruct((m, n), out_dtype),
                   res=res, res_spec=pl.BlockSpec((tm, tn), lambda j, i: (i, j)), norm_gain=norm_gain)


def mm_nt(name, dy, w, *, res=None, tn=2304):
    m, n = dy.shape
    k = w.shape[0]
    tm, tn = _row_tile(m, 1024), (tn if n % tn == 0 else min(1024, n))
    return _matmul(name, dy, w, dims=NT, grid=(m // tm, n // tn), red_axis=1,
                   a_spec=pl.BlockSpec((tm, tn), lambda i, j: (i, j)), b_spec=pl.BlockSpec((k, tn), lambda i, j: (0, j)),
                   o_spec=pl.BlockSpec((tm, k), lambda i, j: (i, 0)), out_shape=jax.ShapeDtypeStruct((m, k), F32),
                   res=res, res_spec=pl.BlockSpec((tm, k), lambda i, j: (i, 0)))


def mm_tn(name, x, dy, *, tn=1024, by_block=False):
    m, k = x.shape
    n = dy.shape[1]
    tm, tn = _row_tile(m, 2048), min(tn, n)
    if by_block:
        o_spec, out_shape, acc = pl.BlockSpec((None, k, tn), lambda j, i: (j, 0, 0)), jax.ShapeDtypeStruct((n // tn, k, tn), BF16), (k, tn)
    else:
        o_spec, out_shape, acc = pl.BlockSpec((k, tn), lambda j, i: (0, j)), jax.ShapeDtypeStruct((k, n), F32), None
    return _matmul(name, x, dy, dims=TN, grid=(n // tn, m // tm), red_axis=1,
                   a_spec=pl.BlockSpec((tm, k), lambda j, i: (i, 0)), b_spec=pl.BlockSpec((tm, tn), lambda j, i: (i, j)),
                   o_spec=o_spec, out_shape=out_shape, acc_shape=acc)


def _rows_call(name, body, ins, outs, s, tile):
    in_specs = []
    for a, kind in ins:
        if kind == "row":
            in_specs.append(pl.BlockSpec((tile, a.shape[1]), lambda i: (i, 0)))
        else:
            in_specs.append(pl.BlockSpec(a.shape, lambda i, nd=a.ndim: (0,) * nd))
    out_specs, out_shape = [], []
    for cols, dt, kind in outs:
        if kind == "row":
            out_specs.append(pl.BlockSpec((tile, cols), lambda i: (i, 0)))
            out_shape.append(jax.ShapeDtypeStruct((s, cols), dt))
        else:
            out_specs.append(pl.BlockSpec((8, cols), lambda i: (0, 0)))
            out_shape.append(jax.ShapeDtypeStruct((8, cols), dt))
    has_acc = any(kind == "acc" for _, _, kind in outs)
    return pl.pallas_call(body, name=name, grid=(s // tile,), in_specs=in_specs, out_specs=out_specs, out_shape=out_shape,
                          compiler_params=_params(("arbitrary",) if has_acc else ("parallel",)))(*[a for a, _ in ins])


def _acc_rows(ref, val):
    @pl.when(pl.program_id(0) == 0)
    def _():
        ref[...] = jnp.zeros_like(ref)

    ref[...] += jnp.broadcast_to(val, ref.shape)


def rmsnorm_fwd(name, h, gain):
    s = h.shape[0]

    def body(h_ref, g_ref, o_ref):
        o_ref[...] = _rms(h_ref[...], g_ref[...]).astype(o_ref.dtype)

    return _rows_call(name, body, [(h, "row"), (gain, "full")], [(D_MODEL, BF16, "row")], s, _row_tile(s))[0]


def rmsnorm_bwd(name, h, gain, d_hn, d_res):
    s = h.shape[0]

    def body(h_ref, g_ref, dy_ref, dr_ref, dh_ref, dg_ref):
        _, vjp = jax.vjp(_rms, h_ref[...], g_ref[...])
        dh, dg = vjp(dy_ref[...])
        dh_ref[...] = dh + dr_ref[...]
        _acc_rows(dg_ref, dg)

    return _rows_call(name, body, [(h, "row"), (gain, "full"), (d_hn, "row"), (d_res, "row")],
                      [(D_MODEL, F32, "row"), (D_MODEL, F32, "acc")], s, _row_tile(s))


def loss_and_grad(y, target):
    s = y.shape[0]
    tile = _row_tile(s)

    def body(y_ref, t_ref, dy_ref, acc_ref):
        diff = y_ref[...] - t_ref[...]
        dy_ref[...] = diff * (1.0 / D_MODEL)

        @pl.when(pl.program_id(0) == 0)
        def _():
            acc_ref[...] = jnp.zeros_like(acc_ref)

        acc_ref[...] += jnp.sum((diff * diff).reshape(tile // 8, 8, D_MODEL), axis=0)

    return _rows_call("loss_head", body, [(y, "row"), (target, "row")], [(D_MODEL, F32, "row"), (D_MODEL, F32, "acc")], s, tile)


def _rope_table(s):
    half = A_HEAD_DIM // 2
    inv_freq = ROPE_THETA ** (-jnp.arange(half, dtype=F32) / half)
    ang = jnp.arange(s).astype(F32)[:, None] * inv_freq[None, :]
    cos, sin = jnp.cos(ang), jnp.sin(ang)
    return jnp.concatenate([cos, cos, cos, cos, -sin, sin, -sin, sin], axis=1)


def _attn_geometry(s, dil):
    tile = min(ATT_TILE, s)
    halo = A_HALF * dil
    assert s % tile == 0 and tile % (ATT_T * dil) == 0 and tile % halo == 0
    return tile, halo, tile // (ATT_T * dil)


def _attn_in_specs(grp, s, tile, halo):
    hb, n_hb = tile // halo, s // halo
    cq, ck, cv = (24 * grp + 8 * t for t in range(3))

    def main(col, per_pair, width=LANES):
        return pl.BlockSpec((tile, width), lambda i, j: (i, col + per_pair * j))

    def prev(col, per_pair, width=LANES):
        return pl.BlockSpec((halo, width), lambda i, j: (jnp.maximum(i * hb - 1, 0), col + per_pair * j))

    def nxt(col, per_pair, width=LANES):
        return pl.BlockSpec((halo, width), lambda i, j: (jnp.minimum((i + 1) * hb, n_hb - 1), col + per_pair * j))

    return [main(cq, 1), prev(ck, 1), main(ck, 1), nxt(ck, 1), prev(cv, 1), main(cv, 1), nxt(cv, 1),
            prev(0, 0, 256), main(0, 0, 256), nxt(0, 0, 256)]


def _attn_valid(first_job, n_jobs, dil, tile_base, length):
    r = lax.broadcasted_iota(jnp.int32, (1, 2 * ATT_T, 1), 1)
    tq = jnp.where(r >= ATT_T, r - ATT_T, r)
    rel = lax.broadcasted_iota(jnp.int32, (1, 1, ATT_T + 2 * A_HALF), 2) - A_HALF
    job = first_job + lax.broadcasted_iota(jnp.int32, (n_jobs, 1, 1), 0)
    tk = tile_base + (job // dil) * ATT_T + rel
    return (jnp.abs(rel - tq) <= A_HALF) & (tk >= 0) & (tk < length)


def _jobs(dil, n_sub):
    return [u * ATT_T * dil + p for u in range(n_sub) for p in range(dil)]


def _gather_jobs(ref, starts, size, dil):
    return jnp.concatenate([ref[_rows(st, size, dil), :][None] for st in starts], axis=0)


def _rows(start, size, dil):
    return pl.ds(start, size, stride=dil) if dil > 1 else pl.ds(start, size)


def attn_fwd(name, qkv, tab, gq, gk, grp, dil):
    s = qkv.shape[0]
    tile, halo, n_sub = _attn_geometry(s, dil)
    length, per_tile = s // dil, tile // dil
    nk = ATT_T + 2 * A_HALF

    def body(q_ref, kp_ref, km_ref, kn_ref, vp_ref, vm_ref, vn_ref, tp_ref, tm_ref, tn_ref, gq_ref, gk_ref, o_ref, lse_ref,
             q_buf, k_buf, v_buf):
        i, pair = pl.program_id(0), pl.program_id(1)
        q_buf[...] = _qk_prep(q_ref[...], tm_ref[...], gq_ref[...])
        for ref, t_ref, lo, n in ((kp_ref, tp_ref, 0, halo), (km_ref, tm_ref, halo, tile), (kn_ref, tn_ref, halo + tile, halo)):
            k_buf[lo:lo + n, :] = _qk_prep(ref[...], t_ref[...], gk_ref[...])
        for ref, lo, n in ((vp_ref, 0, halo), (vm_ref, halo, tile), (vn_ref, halo + tile, halo)):
            v_buf[lo:lo + n, :] = ref[...]

        jobs = _jobs(dil, n_sub)
        for g0 in range(0, len(jobs), ATT_FWD_BATCH):
            starts = jobs[g0:g0 + ATT_FWD_BATCH]
            valid = _attn_valid(g0, len(starts), dil, i * per_tile, length)
            o, lse = _attn_job(_gather_jobs(q_buf, starts, ATT_T, dil), _gather_jobs(k_buf, starts, nk, dil),
                               _gather_jobs(v_buf, starts, nk, dil), valid)
            for n, st in enumerate(starts):
                o_ref[_rows(st, ATT_T, dil), :] = o[n]
                lse_ref[_rows(st, ATT_T, dil), :] = lse[n]

    full = lambda a: pl.BlockSpec(a.shape, lambda i, j: (0, 0))
    return pl.pallas_call(
        body, name=name, grid=(s // tile, D_MODEL // LANES),
        in_specs=_attn_in_specs(grp, s, tile, halo) + [full(gq), full(gk)],
        out_specs=[pl.BlockSpec((tile, LANES), lambda i, j: (i, j)), pl.BlockSpec((tile, LANES), lambda i, j: (i, j))],
        out_shape=[jax.ShapeDtypeStruct((s, D_MODEL), F32), jax.ShapeDtypeStruct((s, D_MODEL), F32)],
        scratch_shapes=[pltpu.VMEM((tile, LANES), F32), pltpu.VMEM((tile + 2 * halo, LANES), F32), pltpu.VMEM((tile + 2 * halo, LANES), F32)],
        compiler_params=_params(("parallel", "parallel")),
    )(qkv, qkv, qkv, qkv, qkv, qkv, qkv, tab, tab, tab, gq, gk)


def attn_bwd(name, qkv, tab, gq, gk, o, lse, d_o, d_lse, grp, dil, d_qkv):
    s = qkv.shape[0]
    tile, halo, n_sub = _attn_geometry(s, dil)
    length, per_tile = s // dil, tile // dil
    nt = s // tile
    nk = ATT_T + 2 * A_HALF
    pieces = ((0, halo), (halo, tile), (halo + tile, halo))

    def body(q_ref, kp_ref, km_ref, kn_ref, vp_ref, vm_ref, vn_ref, tp_ref, tm_ref, tn_ref, gq_ref, gk_ref, o_ref, l_ref, do_ref, dl_ref,
             _, dq_ref, dkp_ref, dkm_ref, dkn_ref, dvp_ref, dvm_ref, dvn_ref, dgq_ref, dgk_ref, q_buf, k_buf, v_buf, dq_buf, dk_buf, dv_buf):
        i, pair = pl.program_id(0), pl.program_id(1)
        k_refs, t_refs = (kp_ref, km_ref, kn_ref), (tp_ref, tm_ref, tn_ref)
        qn, q_vjp = jax.vjp(lambda x, g: _qk_prep(x, tm_ref[...], g), q_ref[...], gq_ref[...])
        q_buf[...] = qn
        k_vjps = []
        for ref, t_ref, (lo, n) in zip(k_refs, t_refs, pieces):
            kn, k_vjp = jax.vjp(lambda x, g: _qk_prep(x, t_ref[...], g), ref[...], gk_ref[...])
            k_buf[lo:lo + n, :] = kn
            k_vjps.append(k_vjp)
        for ref, (lo, n) in zip((vp_ref, vm_ref, vn_ref), pieces):
            v_buf[lo:lo + n, :] = ref[...]
        dk_buf[...] = jnp.zeros_like(dk_buf)
        dv_buf[...] = jnp.zeros_like(dv_buf)
        jobs = _jobs(dil, n_sub)
        for g0 in range(0, len(jobs), ATT_BWD_BATCH):
            starts = jobs[g0:g0 + ATT_BWD_BATCH]
            valid = _attn_valid(g0, len(starts), dil, i * per_tile, length)
            own = lambda ref: _gather_jobs(ref, starts, ATT_T, dil)
            dq, dk, dv = _attn_job_bwd(own(q_buf), _gather_jobs(k_buf, starts, nk, dil), _gather_jobs(v_buf, starts, nk, dil), valid,
                                       own(do_ref), own(o_ref), own(l_ref), own(dl_ref))
            for n, st in enumerate(starts):
                dq_buf[_rows(st, ATT_T, dil), :] = dq[n]
                dk_buf[_rows(st, nk, dil), :] += dk[n]
                dv_buf[_rows(st, nk, dil), :] += dv[n]
        dq, dgq = q_vjp(dq_buf[...])
        dq_ref[...] = dq.astype(dq_ref.dtype)
        dgk = jnp.zeros((1, LANES), F32)
        for k_vjp, out_ref, (lo, n) in zip(k_vjps, (dkp_ref, dkm_ref, dkn_ref), pieces):
            out_ref[...], dgk_piece = k_vjp(dk_buf[lo:lo + n, :])
            dgk = dgk + dgk_piece
        for out_ref, (lo, n) in zip((dvp_ref, dvm_ref, dvn_ref), pieces):
            out_ref[...] = dv_buf[lo:lo + n, :]

        @pl.when((i == 0) & (pair == 0))
        def _():
            dgq_ref[...] = jnp.zeros_like(dgq_ref)
            dgk_ref[...] = jnp.zeros_like(dgk_ref)

        dgq_ref[...] += jnp.broadcast_to(dgq + pltpu.roll(dgq, A_HEAD_DIM, 1), dgq_ref.shape)
        dgk_ref[...] += jnp.broadcast_to(dgk + pltpu.roll(dgk, A_HEAD_DIM, 1), dgk_ref.shape)

    full = lambda a: pl.BlockSpec(a.shape, lambda i, j: (0, 0))
    main_o = pl.BlockSpec((tile, LANES), lambda i, j: (i, j))
    edge_o = pl.BlockSpec((None, halo, LANES), lambda i, j: (i, 0, j))
    main_s = jax.ShapeDtypeStruct((s, D_MODEL), F32)
    edge_s = jax.ShapeDtypeStruct((nt, halo, D_MODEL), F32)
    acc_o = pl.BlockSpec((8, LANES), lambda i, j: (0, 0))
    acc_s = jax.ShapeDtypeStruct((8, LANES), F32)
    big = pltpu.VMEM((tile + 2 * halo, LANES), F32)
    own = pltpu.VMEM((tile, LANES), F32)
    outs = pl.pallas_call(
        body, name=name, grid=(nt, D_MODEL // LANES),
        in_specs=_attn_in_specs(grp, s, tile, halo) + [full(gq), full(gk), main_o, main_o, main_o, main_o, ANY],
        out_specs=[pl.BlockSpec((tile, LANES), lambda i, j: (i, 24 * grp + j)), edge_o, main_o, edge_o, edge_o, main_o, edge_o, acc_o, acc_o],
        out_shape=[jax.ShapeDtypeStruct(d_qkv.shape, d_qkv.dtype), edge_s, main_s, edge_s, edge_s, main_s, edge_s, acc_s, acc_s],
        scratch_shapes=[own, big, big, own, big, big],
        input_output_aliases={16: 0},
        compiler_params=_params(("arbitrary", "arbitrary")),
    )(qkv, qkv, qkv, qkv, qkv, qkv, qkv, tab, tab, tab, gq, gk, o, lse, d_o, d_lse, d_qkv)
    d_qkv, dkp, dkm, dkn, dvp, dvm, dvn, dgq, dgk = outs
    return d_qkv, (dkp, dkm, dkn), (dvp, dvm, dvn), dgq, dgk


def attn_combine(name, parts, s, dil, d_qkv, col):
    prev_part, main_part, next_part = parts
    tile, halo, _ = _attn_geometry(s, dil)
    nt = s // tile
    cols = D_MODEL // 2

    def body(m_ref, from_prev_ref, from_next_ref, _, o_ref):
        i = pl.program_id(0)
        o_ref[...] = m_ref[...].astype(o_ref.dtype)
        head = m_ref[0:halo, :] + jnp.where(i > 0, from_prev_ref[...], 0.0)
        o_ref[0:halo, :] = head.astype(o_ref.dtype)
        tail = m_ref[tile - halo:tile, :] + jnp.where(i < nt - 1, from_next_ref[...], 0.0)
        o_ref[tile - halo:tile, :] = tail.astype(o_ref.dtype)

    return pl.pallas_call(
        body, name=name, grid=(nt, D_MODEL // cols),
        in_specs=[pl.BlockSpec((tile, cols), lambda i, c: (i, c)),
                  pl.BlockSpec((None, halo, cols), lambda i, c: (jnp.maximum(i - 1, 0), 0, c)),
                  pl.BlockSpec((None, halo, cols), lambda i, c: (jnp.minimum(i + 1, nt - 1), 0, c)), ANY],
        out_specs=pl.BlockSpec((tile, cols), lambda i, c: (i, (D_MODEL // cols) * col + c)),
        out_shape=jax.ShapeDtypeStruct(d_qkv.shape, d_qkv.dtype), input_output_aliases={3: 0},
        compiler_params=_params(("parallel", "parallel")),
    )(main_part, next_part, prev_part, d_qkv)


def attn_merge_fwd(name, os_, lses):
    s = os_[0].shape[0]

    def body(o0, o1, o2, l0, l1, l2, out_ref):
        out_ref[...] = _merge_groups(o0[...], o1[...], o2[...], l0[...], l1[...], l2[...]).astype(out_ref.dtype)

    return _rows_call(name, body, [(a, "row") for a in (*os_, *lses)], [(D_MODEL, BF16, "row")], s, _row_tile(s, 256))[0]


def attn_merge_bwd(name, os_, lses, d_out):
    s = os_[0].shape[0]

    def body(o0, o1, o2, l0, l1, l2, d_ref, *outs):
        _, vjp = jax.vjp(_merge_groups, o0[...], o1[...], o2[...], l0[...], l1[...], l2[...])
        for ref, val in zip(outs, vjp(d_ref[...])):
            ref[...] = val

    return _rows_call(name, body, [(a, "row") for a in (*os_, *lses, d_out)],
                      [(D_MODEL, F32, "row")] * 6, s, _row_tile(s, 256))


def gla_gate_fwd(name, z, wg, bias):
    s = z.shape[0]

    def body(z_ref, w_ref, b_ref, o_ref):
        o_ref[...] = _gate(z_ref[...], w_ref[...], b_ref[...])

    return _rows_call(name, body, [(z, "row"), (wg, "full"), (bias, "full")], [(D_MODEL, F32, "row")], s, _row_tile(s))[0]


def gla_gate_bwd(name, z, wg, bias, d_la_f, d_la_b):
    s = z.shape[0]
    tile = _row_tile(s)

    def body(z_ref, w_ref, b_ref, df_ref, db_ref, dz_ref, dw_ref, dbias_ref):
        _, vjp = jax.vjp(_gate, z_ref[...], w_ref[...], b_ref[...])
        dz, dw, dbias = vjp(jnp.concatenate([df_ref[...], db_ref[...]], axis=1))
        dz_ref[...] = dz

        @pl.when(pl.program_id(0) == 0)
        def _():
            dw_ref[...] = jnp.zeros_like(dw_ref)

        dw_ref[...] += dw
        _acc_rows(dbias_ref, dbias)

    return pl.pallas_call(
        body, name=name, grid=(s // tile,),
        in_specs=[pl.BlockSpec((tile, LANES), lambda i: (i, 0)), pl.BlockSpec(wg.shape, lambda i: (0, 0)),
                  pl.BlockSpec(bias.shape, lambda i: (0, 0)), pl.BlockSpec((tile, 512), lambda i: (i, 0)),
                  pl.BlockSpec((tile, 512), lambda i: (i, 0))],
        out_specs=[pl.BlockSpec((tile, LANES), lambda i: (i, 0)), pl.BlockSpec(wg.shape, lambda i: (0, 0)),
                   pl.BlockSpec((8, D_MODEL), lambda i: (0, 0))],
        out_shape=[jax.ShapeDtypeStruct((s, LANES), F32), jax.ShapeDtypeStruct(wg.shape, F32), jax.ShapeDtypeStruct((8, D_MODEL), F32)],
        compiler_params=_params(("arbitrary",)),
    )(z, wg, bias, d_la_f, d_la_b)


def _gla_in_specs(tile, order, la_col0):
    t = order
    return [pl.BlockSpec((tile, B_KEY_DIM), lambda h, n: (t(n), h)),
            pl.BlockSpec((tile, B_KEY_DIM), lambda h, n: (t(n), B_HEADS + h)),
            pl.BlockSpec((tile, B_VAL_DIM), lambda h, n: (t(n), B_HEADS + h)),
            pl.BlockSpec((tile, B_KEY_DIM), lambda h, n: (t(n), la_col0 + h))]


def gla_fwd(name, proj, la, reverse):
    s = proj.shape[0]
    tile = _row_tile(s)
    nt = s // tile
    order = (lambda n: nt - 1 - n) if reverse else (lambda n: n)

    def body(q_ref, k_ref, v_ref, la_ref, o_ref, st_ref, st_scr):
        @pl.when(pl.program_id(1) == 0)
        def _():
            st_scr[...] = jnp.zeros_like(st_scr)

        st_ref[...] = st_scr[...]
        o, st = _gla_tile(q_ref[...], k_ref[...], v_ref[...], la_ref[...], st_scr[...], reverse)
        o_ref[...] = o
        st_scr[...] = st

    return pl.pallas_call(
        body, name=name, grid=(B_HEADS, nt), in_specs=_gla_in_specs(tile, order, B_HEADS if reverse else 0),
        out_specs=[pl.BlockSpec((tile, B_VAL_DIM), lambda h, n: (order(n), h)),
                   pl.BlockSpec((None, None, B_VAL_DIM, B_KEY_DIM), lambda h, n: (h, order(n), 0, 0))],
        out_shape=[jax.ShapeDtypeStruct((s, D_MODEL), F32), jax.ShapeDtypeStruct((B_HEADS, nt, B_VAL_DIM, B_KEY_DIM), F32)],
        scratch_shapes=[pltpu.VMEM((B_VAL_DIM, B_KEY_DIM), F32)],
        compiler_params=_params(("parallel", "arbitrary")),
    )(proj, proj, proj, la)


def gla_bwd(name, proj, la, states, d_o, reverse, prev=None):
    s = proj.shape[0]
    tile = _row_tile(s)
    nt = s // tile
    order = (lambda n: n) if reverse else (lambda n: nt - 1 - n)

    def body(*refs):
        q_ref, k_ref, v_ref, la_ref, st_ref, do_ref = refs[:6]
        rest = refs[6:]
        prev_refs = rest[:3] if prev is not None else None
        dq_ref, dk_ref, dv_ref, dla_ref, dst_scr = rest[3:] if prev is not None else rest

        @pl.when(pl.program_id(1) == 0)
        def _():
            dst_scr[...] = jnp.zeros_like(dst_scr)

        _, vjp = jax.vjp(functools.partial(_gla_tile, reverse=reverse), q_ref[...], k_ref[...], v_ref[...], la_ref[...], st_ref[...])
        dq, dk, dv, dla, dst = vjp((do_ref[...], dst_scr[...]))
        if prev_refs is not None:
            dq, dk, dv = dq + prev_refs[0][...], dk + prev_refs[1][...], dv + prev_refs[2][...]
        dq_ref[...], dk_ref[...], dv_ref[...], dla_ref[...] = dq, dk, dv, dla
        dst_scr[...] = dst

    key_spec = pl.BlockSpec((tile, B_KEY_DIM), lambda h, n: (order(n), h))
    val_spec = pl.BlockSpec((tile, B_VAL_DIM), lambda h, n: (order(n), h))
    in_specs = _gla_in_specs(tile, order, B_HEADS if reverse else 0) + [
        pl.BlockSpec((None, None, B_VAL_DIM, B_KEY_DIM), lambda h, n: (h, order(n), 0, 0)), val_spec]
    ins = [proj, proj, proj, la, states, d_o]
    if prev is not None:
        in_specs += [key_spec, key_spec, val_spec]
        ins += list(prev)
    return pl.pallas_call(
        body, name=name, grid=(B_HEADS, nt), in_specs=in_specs,
        out_specs=[key_spec, key_spec, val_spec, key_spec],
        out_shape=[jax.ShapeDtypeStruct((s, 512), F32), jax.ShapeDtypeStruct((s, 512), F32),
                   jax.ShapeDtypeStruct((s, D_MODEL), F32), jax.ShapeDtypeStruct((s, 512), F32)],
        scratch_shapes=[pltpu.VMEM((B_VAL_DIM, B_KEY_DIM), F32)],
        compiler_params=_params(("parallel", "arbitrary")),
    )(*ins)


def _r_spec(tile):
    return pl.BlockSpec((tile, D_MODEL), lambda i: (i, 2))


def gla_post_fwd(name, o_f, o_b, proj, gain):
    s = o_f.shape[0]
    tile = _row_tile(s)

    def body(of_ref, ob_ref, r_ref, g_ref, out_ref):
        out_ref[...] = _gla_post(of_ref[...], ob_ref[...], r_ref[...], g_ref[...]).astype(out_ref.dtype)

    row = pl.BlockSpec((tile, D_MODEL), lambda i: (i, 0))
    return pl.pallas_call(body, name=name, grid=(s // tile,),
                          in_specs=[row, row, _r_spec(tile), pl.BlockSpec(gain.shape, lambda i: (0, 0))], out_specs=row,
                          out_shape=jax.ShapeDtypeStruct((s, D_MODEL), BF16), compiler_params=_params(("parallel",)))(o_f, o_b, proj, gain)


def gla_post_bwd(name, o_f, o_b, proj, gain, d_out):
    s = o_f.shape[0]
    tile = _row_tile(s)

    def body(of_ref, ob_ref, r_ref, g_ref, d_ref, do_ref, dr_ref, dg_ref):
        _, vjp = jax.vjp(_gla_post, of_ref[...], ob_ref[...], r_ref[...], g_ref[...])
        d_of, _, dr, dg = vjp(d_ref[...])
        do_ref[...] = d_of
        dr_ref[...] = dr
        _acc_rows(dg_ref, dg)

    row = pl.BlockSpec((tile, D_MODEL), lambda i: (i, 0))
    return pl.pallas_call(
        body, name=name, grid=(s // tile,),
        in_specs=[row, row, _r_spec(tile), pl.BlockSpec(gain.shape, lambda i: (0, 0)), row],
        out_specs=[row, row, pl.BlockSpec((8, D_MODEL), lambda i: (0, 0))],
        out_shape=[jax.ShapeDtypeStruct((s, D_MODEL), F32), jax.ShapeDtypeStruct((s, D_MODEL), F32), jax.ShapeDtypeStruct((8, D_MODEL), F32)],
        compiler_params=_params(("arbitrary",)))(o_f, o_b, proj, gain, d_out)


def _hid(tile, where):
    return pl.BlockSpec((None, tile, FFN_BLK), where)


def _pair(tile, where):
    return pl.BlockSpec((2, None, tile, FFN_BLK), where)


def _w_gu_spec(layer, where_j):
    return pl.BlockSpec((2, None, None, D_MODEL, FFN_BLK), lambda *g: (0, where_j(*g), layer, 0, 0))


def ffn_fwd(name, h_mid, hn2, w_gu, w_down, layer, next_gain=None):
    s = hn2.shape[0]
    tm = _row_tile(s, 1024)
    nt = s // tm

    def gu_body(x_ref, w_ref, gu_ref, act_ref):
        x = x_ref[...]
        g = _dot(x, w_ref[0], NN)
        u = _dot(x, w_ref[1], NN)
        gu_ref[0] = g
        gu_ref[1] = u
        act_ref[...] = _swiglu_act(g, u).astype(act_ref.dtype)

    gu, act = pl.pallas_call(
        gu_body, name=name + "_gu", grid=(4, nt),
        in_specs=[pl.BlockSpec((tm, D_MODEL), lambda j, i: (i, 0)), _w_gu_spec(layer, lambda j, i: j)],
        out_specs=[_pair(tm, lambda j, i: (0, j, i, 0)), _hid(tm, lambda j, i: (j, i, 0))],
        out_shape=[jax.ShapeDtypeStruct((2, 4, s, FFN_BLK), F32), jax.ShapeDtypeStruct((4, s, FFN_BLK), BF16)],
        compiler_params=_params(("parallel", "parallel")))(hn2, w_gu)
    def down_body(*refs):
        act_ref, w_ref, res_ref = refs[:3]
        g_ref = refs[3] if next_gain is not None else None
        o_ref = refs[4] if next_gain is not None else refs[3]
        out = res_ref[...]
        for j in range(4):
            out = out + _dot(act_ref[j], w_ref[j], NN)
        o_ref[...] = out
        if next_gain is not None:
            refs[5][...] = _rms(out, g_ref[...]).astype(BF16)

    row = pl.BlockSpec((tm, D_MODEL), lambda i: (i, 0))
    in_specs = [pl.BlockSpec((4, tm, FFN_BLK), lambda i: (0, i, 0)), pl.BlockSpec((4, FFN_BLK, D_MODEL), lambda i: (0, 0, 0)), row]
    ins = [act, w_down, h_mid]
    out_specs, out_shape = row, jax.ShapeDtypeStruct((s, D_MODEL), F32)
    if next_gain is not None:
        in_specs.append(pl.BlockSpec(next_gain.shape, lambda i: (0, 0)))
        ins.append(next_gain)
        out_specs, out_shape = [row, row], [out_shape, jax.ShapeDtypeStruct((s, D_MODEL), BF16)]
    h_next = pl.pallas_call(down_body, name=name + "_down", grid=(nt,), in_specs=in_specs, out_specs=out_specs,
                            out_shape=out_shape, compiler_params=_params(("parallel",)))(*ins)
    return h_next, gu, act


def ffn_bwd(name, dh_next, hn2, gu, act, w_gu, w_down, layer):
    s = hn2.shape[0]
    tm = _row_tile(s, 1024)
    nt = s // tm
    tw = _row_tile(s, 2048)
    ntw = s // tw
    d_wd = _matmul(name + "_dwd", act, dh_next, dims=TN, grid=(4, ntw), red_axis=1,
                   a_spec=_hid(tw, lambda j, i: (j, i, 0)), b_spec=pl.BlockSpec((tw, D_MODEL), lambda j, i: (i, 0)),
                   o_spec=pl.BlockSpec((None, FFN_BLK, D_MODEL), lambda j, i: (j, 0, 0)),
                   out_shape=jax.ShapeDtypeStruct((4, FFN_BLK, D_MODEL), BF16), acc_shape=(FFN_BLK, D_MODEL))

    def dgu_body(dy_ref, wd_ref, gu_ref, dgu_ref):
        d_act = _dot(dy_ref[...], wd_ref[...], NT)
        g, u = gu_ref[0], gu_ref[1]
        sg = _sigmoid(g)
        silu = g * sg
        dgu_ref[0] = (d_act * u * (sg + silu * (1.0 - sg))).astype(dgu_ref.dtype)
        dgu_ref[1] = (d_act * silu).astype(dgu_ref.dtype)

    d_gu = pl.pallas_call(
        dgu_body, name=name + "_dgu", grid=(nt, 4),
        in_specs=[pl.BlockSpec((tm, D_MODEL), lambda i, j: (i, 0)), pl.BlockSpec((None, FFN_BLK, D_MODEL), lambda i, j: (j, 0, 0)),
                  _pair(tm, lambda i, j: (0, j, i, 0))],
        out_specs=_pair(tm, lambda i, j: (0, j, i, 0)), out_shape=jax.ShapeDtypeStruct((2, 4, s, FFN_BLK), BF16),
        compiler_params=_params(("parallel", "parallel")))(dh_next, w_down, gu)

    tx = _row_tile(s, 512)

    def dx_body(d_ref, w_ref, o_ref):
        out = None
        for t in range(2):
            for j in range(4):
                prod = _dot(d_ref[t, j], w_ref[t, j], NT)
                out = prod if out is None else out + prod
        o_ref[...] = out

    d_hn2 = pl.pallas_call(
        dx_body, name=name + "_dx", grid=(s // tx,),
        in_specs=[pl.BlockSpec((2, 4, tx, FFN_BLK), lambda i: (0, 0, i, 0)),
                  pl.BlockSpec((2, 4, None, D_MODEL, FFN_BLK), lambda i: (0, 0, layer, 0, 0))],
        out_specs=pl.BlockSpec((tx, D_MODEL), lambda i: (i, 0)), out_shape=jax.ShapeDtypeStruct((s, D_MODEL), F32),
        compiler_params=_params(("parallel",)))(d_gu, w_gu)

    def dw_body(x_ref, d_ref, o_ref, acc_ref):
        x = x_ref[...]
        k = pl.program_id(1)
        for t in range(2):
            prod = _dot(x, d_ref[t], TN)

            @pl.when(k == 0)
            def _():
                acc_ref[t] = prod

            @pl.when(k > 0)
            def _():
                acc_ref[t] += prod

        @pl.when(k == ntw - 1)
        def _():
            o_ref[...] = acc_ref[...].astype(o_ref.dtype)

    d_wgu = pl.pallas_call(
        dw_body, name=name + "_dwgu", grid=(4, ntw),
        in_specs=[pl.BlockSpec((tw, D_MODEL), lambda j, i: (i, 0)), _pair(tw, lambda j, i: (0, j, i, 0))],
        out_specs=pl.BlockSpec((2, None, D_MODEL, FFN_BLK), lambda j, i: (0, j, 0, 0)),
        out_shape=jax.ShapeDtypeStruct((2, 4, D_MODEL, FFN_BLK), BF16),
        scratch_shapes=[pltpu.VMEM((2, D_MODEL, FFN_BLK), F32)],
        compiler_params=_params(("parallel", "arbitrary")))(hn2, d_gu)
    return d_hn2, d_wgu, d_wd


def _my_place():
    return lax.axis_index("x"), lax.axis_index("y"), lax.axis_index("c")


def _flip(place, k):
    x, y, c = place
    return (1 - x if k & 4 else x, 1 - y if k & 2 else y, 1 - c if k & 1 else c)


def _index(place):
    return 4 * place[0] + 2 * place[1] + place[2]


def all_gather(arrs):
    n = len(arrs)

    def body(*refs):
        ins, outs = refs[:n], refs[n:2 * n]
        send_sems, recv_sems, local_sems = refs[2 * n:]
        me = _my_place()
        sibling = _flip(me, 1)
        chips = (4, 2, 6)

        def copy(a, k, block, to, src=None):
            dst = outs[a].at[_index(block)]
            return pltpu.make_async_remote_copy(src_ref=dst if src is None else src, dst_ref=dst, send_sem=send_sems.at[a, k],
                                                recv_sem=recv_sems.at[a, k], device_id=to, device_id_type=MESH)

        started = []
        for a in range(n):
            mine = pltpu.make_async_copy(ins[a], outs[a].at[_index(me)], local_sems.at[a])
            mine.start()
            started.append(mine)
        first = []
        for a in range(n):
            first.append(copy(a, 0, me, sibling, src=ins[a]))
            first += [copy(a, 1 + j, me, _flip(me, k), src=ins[a]) for j, k in enumerate(chips)]
        for cp in first:
            cp.start()
        passed = []
        for a in range(n):
            for j, k in enumerate(chips):
                copy(a, 1 + j, _flip(me, k), me).wait_recv()
                fwd = copy(a, 4 + j, _flip(me, k), sibling)
                fwd.start()
                passed.append(fwd)
        for a in range(n):
            copy(a, 0, sibling, me).wait_recv()
            for j, k in enumerate(chips):
                copy(a, 4 + j, _flip(sibling, k), me).wait_recv()
        for cp in first + passed:
            cp.wait_send()
        for cp in started:
            cp.wait()

    return pl.pallas_call(
        body, name="all_gather_weights", in_specs=[ANY] * n, out_specs=[ANY] * n,
        out_shape=[jax.ShapeDtypeStruct((N_DEV,) + a.shape, a.dtype) for a in arrs],
        scratch_shapes=[pltpu.SemaphoreType.DMA((n, 7)), pltpu.SemaphoreType.DMA((n, 7)), pltpu.SemaphoreType.DMA((n,))],
    )(*arrs)


def exchange_partials(arrs):
    n = len(arrs)

    def body(*refs):
        ins, outs = refs[:n], refs[n:2 * n]
        send_sems, recv_sems, local_sems = refs[2 * n:]
        me = _my_place()
        local = []
        for a in range(n):
            cp = pltpu.make_async_copy(ins[a].at[_index(me)], outs[a].at[_index(me)], local_sems.at[a])
            cp.start()
            local.append(cp)

        def copy(a, k, src_block, dst_block):
            return pltpu.make_async_remote_copy(src_ref=ins[a].at[_index(src_block)], dst_ref=outs[a].at[_index(dst_block)],
                                                send_sem=send_sems.at[a, k - 1], recv_sem=recv_sems.at[a, k - 1],
                                                device_id=_flip(me, k), device_id_type=MESH)

        sent = []
        for a in range(n):
            for k in range(1, N_DEV):
                cp = copy(a, k, _flip(me, k), me)
                cp.start()
                sent.append(cp)
        for a in range(n):
            for k in range(1, N_DEV):
                copy(a, k, me, _flip(me, k)).wait_recv()
        for cp in sent:
            cp.wait_send()
        for cp in local:
            cp.wait()

    return pl.pallas_call(
        body, name="exchange_weight_grads", in_specs=[ANY] * n, out_specs=[ANY] * n,
        out_shape=[jax.ShapeDtypeStruct(a.shape, a.dtype) for a in arrs],
        scratch_shapes=[pltpu.SemaphoreType.DMA((n, 7)), pltpu.SemaphoreType.DMA((n, 7)), pltpu.SemaphoreType.DMA((n,))],
    )(*arrs)


def adamw_shard(name, parts, w, m, v, layer, tile):
    rows, cols = parts.shape[1:]
    assert rows % tile == 0
    off = layer * (rows // tile)

    def body(p_ref, w_ref, m_ref, v_ref, g_ref, d_ref, nm_ref, nv_ref):
        g = p_ref[0].astype(F32)
        for src in range(1, N_DEV):
            g = g + p_ref[src].astype(F32)
        g_ref[...] = g
        d_ref[...], nm_ref[...], nv_ref[...] = _adamw(w_ref[...], g, m_ref[...], v_ref[...])

    src_row = pl.BlockSpec((tile, cols), lambda i: (off + i, 0))
    row = pl.BlockSpec((tile, cols), lambda i: (i, 0))
    shape = jax.ShapeDtypeStruct((rows, cols), F32)
    return pl.pallas_call(body, name=name, grid=(rows // tile,),
                          in_specs=[pl.BlockSpec((N_DEV, tile, cols), lambda i: (0, i, 0)), src_row, src_row, src_row],
                          out_specs=[row] * 4, out_shape=[shape] * 4, compiler_params=_params(("parallel",)))(parts, w, m, v)


def allreduce_adamw_replicated(partial, w, m, v, n_loss_rows):
    rows = partial.shape[0]

    def body(p_ref, w_ref, m_ref, v_ref, g_ref, d_ref, nm_ref, nv_ref, loss_ref, recv_ref, send_sems, recv_sems):
        me = _my_place()
        recv_ref[_index(me)] = p_ref[...]
        copies = []
        for k in range(1, N_DEV):
            peer = _flip(me, k)
            cp = pltpu.make_async_remote_copy(src_ref=p_ref, dst_ref=recv_ref.at[_index(me)], send_sem=send_sems.at[k - 1],
                                              recv_sem=recv_sems.at[k - 1], device_id=peer, device_id_type=MESH)
            cp.start()
            copies.append((cp, peer))
        for k, (cp, peer) in enumerate(copies):
            pltpu.make_async_remote_copy(src_ref=p_ref, dst_ref=recv_ref.at[_index(peer)], send_sem=send_sems.at[k],
                                         recv_sem=recv_sems.at[k], device_id=peer, device_id_type=MESH).wait_recv()
        for cp, _ in copies:
            cp.wait_send()
        g = recv_ref[0]
        for src in range(1, N_DEV):
            g = g + recv_ref[src]
        g_ref[...] = g
        d_ref[...], nm_ref[...], nv_ref[...] = _adamw(w_ref[...], g, m_ref[...], v_ref[...])
        loss = (0.5 / D_MODEL) * jnp.sum(g[rows - n_loss_rows:, :])
        loss_ref[...] = jnp.full(loss_ref.shape, loss, F32)

    shape = jax.ShapeDtypeStruct((rows, LANES), F32)
    return pl.pallas_call(
        body, name="allreduce_adamw_replicated", in_specs=[VMEM_SPEC] * 4, out_specs=[VMEM_SPEC] * 5,
        out_shape=[shape] * 4 + [jax.ShapeDtypeStruct((8, LANES), F32)],
        scratch_shapes=[pltpu.VMEM((N_DEV, rows, LANES), F32), pltpu.SemaphoreType.DMA((7,)), pltpu.SemaphoreType.DMA((7,))],
    )(partial, w, m, v)


def _pack_rows(flat, cols):
    n = flat.shape[-1]
    rows = -(-n // cols)
    rows = -(-rows // 48) * 48
    flat = jnp.pad(flat, [(0, 0)] * (flat.ndim - 1) + [(0, rows * cols - n)])
    return flat.reshape(flat.shape[:-1] + (rows, cols))


def _unpack(flat, shapes):
    out, off = [], 0
    for shp in shapes:
        n = 1
        for d in shp:
            n *= d
        out.append(flat[off:off + n].reshape(shp))
        off += n
    return out


def _to_dev_cols(a):
    w = a.shape[-1] // N_DEV
    return jnp.moveaxis(a.reshape(a.shape[:-1] + (N_DEV, w)), -2, 0)


def _from_dev_cols(a):
    a = jnp.moveaxis(a, 0, -2)
    return a.reshape(a.shape[:-2] + (a.shape[-2] * a.shape[-1],))


def kernel(x, attn_norm, ffn_norm, a_w_in, a_q_norm, a_k_norm, a_w_out, b_w_in, b_w_gate_f, b_gate_bias_f, b_w_gate_b, b_gate_bias_b, b_out_norm, b_w_out, ffn_w_gate_up, ffn_w_down, loss_target, m_attn_norm, m_ffn_norm, m_a_w_in, m_a_q_norm, m_a_k_norm, m_a_w_out, m_b_w_in, m_b_w_gate_f, m_b_gate_bias_f, m_b_w_gate_b, m_b_gate_bias_b, m_b_out_norm, m_b_w_out, m_ffn_w_gate_up, m_ffn_w_down, v_attn_norm, v_ffn_norm, v_a_w_in, v_a_q_norm, v_a_k_norm, v_a_w_out, v_b_w_in, v_b_w_gate_f, v_b_gate_bias_f, v_b_w_gate_b, v_b_gate_bias_b, v_b_out_norm, v_b_w_out, v_ffn_w_gate_up, v_ffn_w_down):
    seq = x.shape[1]
    depth = attn_norm.shape[0]
    h = x.reshape(seq, D_MODEL)
    target = loss_target.reshape(seq, D_MODEL)
    n_a, n_b = a_w_in.shape[0], b_w_in.shape[0]

    small = jnp.concatenate([t.reshape(-1) for t in (b_w_gate_f, b_gate_bias_f, b_w_gate_b, b_gate_bias_b, b_out_norm)])
    small = _pack_rows(small, LANES)
    g_a_in, g_a_out, g_b_in, g_b_out, g_gu, g_down, g_small = all_gather(
        [a_w_in.astype(BF16), a_w_out.astype(BF16), b_w_in.astype(BF16), b_w_out.astype(BF16),
         ffn_w_gate_up.astype(BF16), ffn_w_down.astype(BF16), small])
    w_a_in = _from_dev_cols(g_a_in)
    w_a_out = jnp.moveaxis(g_a_out, 0, 1).reshape(n_a, D_MODEL, D_MODEL)
    w_b_in = _from_dev_cols(g_b_in)
    w_b_out = jnp.moveaxis(g_b_out, 0, 1).reshape(n_b, D_MODEL, D_MODEL)
    w_down = jnp.moveaxis(g_down, 0, 1).reshape(depth, 4, FFN_BLK, D_MODEL)
    w_gu = g_gu.reshape(2, 4, depth, D_MODEL, FFN_BLK)
    small_shapes = [t.shape for t in (b_w_gate_f, b_gate_bias_f, b_w_gate_b, b_gate_bias_b, b_out_norm)]
    per_dev = [_unpack(g_small[d].reshape(-1), small_shapes) for d in range(N_DEV)]
    wgf, bgf, wgb, bgb, onorm = [_from_dev_cols(jnp.stack([per_dev[d][t] for d in range(N_DEV)])) for t in range(5)]
    w_gate = jnp.zeros((n_b, LANES, D_MODEL), F32)
    w_gate = w_gate.at[:, 0:16, 0:512].set(wgf).at[:, 16:32, 512:1024].set(wgb)
    gate_bias = jnp.concatenate([bgf, bgb], axis=1).reshape(n_b, 1, D_MODEL)
    out_gain = onorm.reshape(n_b, 1, D_MODEL)
    w_b_main = w_b_in[:, :, :3072]
    w_b_z = jnp.pad(w_b_in[:, :, 3072:], ((0, 0), (0, 0), (0, LANES - 32)))

    dh, sq_err, grads = _forward_backward(h, target, attn_norm, ffn_norm, a_q_norm, a_k_norm, w_a_in, w_a_out, w_b_main, w_b_z,
                                          w_gate, gate_bias, out_gain, w_b_out, w_gu, w_down)
    sharded_w = (a_w_in, a_w_out, b_w_in, b_w_gate_f, b_gate_bias_f, b_w_gate_b, b_gate_bias_b, b_out_norm, b_w_out, ffn_w_gate_up, ffn_w_down)
    sharded_m = (m_a_w_in, m_a_w_out, m_b_w_in, m_b_w_gate_f, m_b_gate_bias_f, m_b_w_gate_b, m_b_gate_bias_b, m_b_out_norm, m_b_w_out, m_ffn_w_gate_up, m_ffn_w_down)
    sharded_v = (v_a_w_in, v_a_w_out, v_b_w_in, v_b_w_gate_f, v_b_gate_bias_f, v_b_w_gate_b, v_b_gate_bias_b, v_b_out_norm, v_b_w_out, v_ffn_w_gate_up, v_ffn_w_down)
    rep_w = (attn_norm, ffn_norm, a_q_norm, a_k_norm)
    rep_m = (m_attn_norm, m_ffn_norm, m_a_q_norm, m_a_k_norm)
    rep_v = (v_attn_norm, v_ffn_norm, v_a_q_norm, v_a_k_norm)
    loss, outs = _reduce_and_update(grads, sq_err, sharded_w, sharded_m, sharded_v, rep_w, rep_m, rep_v)
    return (loss, dh.reshape(x.shape), *outs)


def _forward_backward(h, target, attn_norm, ffn_norm, a_q_norm, a_k_norm, w_a_in, w_a_out, w_b_main, w_b_z, w_gate, gate_bias,
                      out_gain, w_b_out, w_gu, w_down):
    seq = h.shape[0]
    depth = attn_norm.shape[0]
    n_a, n_b = w_a_in.shape[0], w_b_main.shape[0]
    tab = _rope_table(seq)
    pair_gain = lambda g: jnp.concatenate([g, g]).reshape(1, LANES)

    saved = []
    for i in range(depth):
        j = i // 2
        nm = f"l{i}"
        if i == 0:
            hn = rmsnorm_fwd(nm + "_norm1", h, attn_norm[i].reshape(1, D_MODEL))
        if i % 2 == 0:
            qkv = mm_nn(nm + "_qkv", hn, w_a_in[j])
            os_, lses = [], []
            for g, (_, dil) in enumerate(A_GROUPS):
                o, lse = attn_fwd(f"{nm}_attn{g}", qkv, tab, pair_gain(a_q_norm[j, g]), pair_gain(a_k_norm[j, g]), g, dil)
                os_.append(o)
                lses.append(lse)
            mixed = attn_merge_fwd(nm + "_merge", os_, lses)
            h_mid, hn2 = mm_nn(nm + "_out", mixed, w_a_out[j], res=h, norm_gain=ffn_norm[i].reshape(1, D_MODEL))
            mix_saved = (qkv, os_, lses, mixed)
        else:
            proj = mm_nn(nm + "_proj", hn, w_b_main[j])
            z = mm_nn(nm + "_z", hn, w_b_z[j])
            la = gla_gate_fwd(nm + "_gate", z, w_gate[j], gate_bias[j])
            o_f, st_f = gla_fwd(nm + "_gla_f", proj, la, False)
            o_b, st_b = gla_fwd(nm + "_gla_b", proj, la, True)
            mixed = gla_post_fwd(nm + "_post", o_f, o_b, proj, out_gain[j])
            h_mid, hn2 = mm_nn(nm + "_out", mixed, w_b_out[j], res=h, norm_gain=ffn_norm[i].reshape(1, D_MODEL))
            mix_saved = (proj, z, la, o_f, st_f, o_b, st_b, mixed)
        next_gain = attn_norm[i + 1].reshape(1, D_MODEL) if i + 1 < depth else None
        h_next, gu, act = ffn_fwd(nm + "_ffn", h_mid, hn2, w_gu, w_down[i], i, next_gain)
        saved.append((h, hn, mix_saved, h_mid, hn2, gu, act))
        if next_gain is not None:
            h_next, hn = h_next
        h = h_next

    dh, sq_err = loss_and_grad(h, target)

    g_attn_norm, g_ffn_norm = [None] * depth, [None] * depth
    g_a_w_in, g_a_w_out, g_a_q, g_a_k = [None] * n_a, [None] * n_a, [None] * n_a, [None] * n_a
    g_b_w_in, g_b_w_out, g_w_gate, g_gate_bias, g_out_gain = ([None] * n_b for _ in range(5))
    g_w_gu, g_w_down = [None] * depth, [None] * depth
    for i in reversed(range(depth)):
        j = i // 2
        nm = f"l{i}b"
        h_in, hn, mix_saved, h_mid, hn2, gu, act = saved[i]
        d_hn2, g_w_gu[i], g_w_down[i] = ffn_bwd(nm + "_ffn", dh, hn2, gu, act, w_gu, w_down[i], i)
        dh_mid, g_ffn_norm[i] = rmsnorm_bwd(nm + "_norm2", h_mid, ffn_norm[i].reshape(1, D_MODEL), d_hn2, dh)
        if i % 2 == 0:
            qkv, os_, lses, mixed = mix_saved
            d_mixed = mm_nt(nm + "_dmixed", dh_mid, w_a_out[j])
            g_a_w_out[j] = mm_tn(nm + "_dwout", mixed, dh_mid, by_block=True)
            d_parts = attn_merge_bwd(nm + "_merge", os_, lses, d_mixed)
            gq_l, gk_l = [], []
            d_qkv = lax.empty(qkv.shape, BF16)
            for g, (_, dil) in enumerate(A_GROUPS):
                d_qkv, dk_parts, dv_parts, dgq, dgk = attn_bwd(f"{nm}_attn{g}", qkv, tab, pair_gain(a_q_norm[j, g]),
                                                               pair_gain(a_k_norm[j, g]), os_[g], lses[g], d_parts[g], d_parts[3 + g], g, dil, d_qkv)
                d_qkv = attn_combine(f"{nm}_dk{g}", dk_parts, seq, dil, d_qkv, 3 * g + 1)
                d_qkv = attn_combine(f"{nm}_dv{g}", dv_parts, seq, dil, d_qkv, 3 * g + 2)
                gq_l.append(dgq[0, :A_HEAD_DIM])
                gk_l.append(dgk[0, :A_HEAD_DIM])
            g_a_q[j], g_a_k[j] = jnp.stack(gq_l), jnp.stack(gk_l)
            d_hn = mm_nt(nm + "_dhn", d_qkv, w_a_in[j])
            g_a_w_in[j] = mm_tn(nm + "_dwin", hn, d_qkv, tn=D_MODEL * 9 // N_DEV, by_block=True)
        else:
            proj, z, la, o_f, st_f, o_b, st_b, mixed = mix_saved
            d_mixed = mm_nt(nm + "_dmixed", dh_mid, w_b_out[j])
            g_b_w_out[j] = mm_tn(nm + "_dwout", mixed, dh_mid, by_block=True)
            d_o, d_r, dgain = gla_post_bwd(nm + "_post", o_f, o_b, proj, out_gain[j], d_mixed)
            g_out_gain[j] = dgain[0]
            dq, dk, dv, dla_f = gla_bwd(nm + "_gla_f", proj, la, st_f, d_o, False)
            dq, dk, dv, dla_b = gla_bwd(nm + "_gla_b", proj, la, st_b, d_o, True, prev=(dq, dk, dv))
            d_z, g_w_gate[j], dbias = gla_gate_bwd(nm + "_gate", z, w_gate[j], gate_bias[j], dla_f, dla_b)
            g_gate_bias[j] = dbias[0]
            d_proj = jnp.concatenate([dq, dk, dv, d_r], axis=1).astype(BF16)
            d_hn = mm_nt(nm + "_dhn_z", d_z, w_b_z[j])
            d_hn = mm_nt(nm + "_dhn", d_proj, w_b_main[j], res=d_hn)
            g_b_w_in[j] = jnp.concatenate([mm_tn(nm + "_dwin", hn, d_proj), mm_tn(nm + "_dwz", hn, d_z)[:, :32]], axis=1)
        dh, g_attn_norm[i] = rmsnorm_bwd(nm + "_norm1", h_in, attn_norm[i].reshape(1, D_MODEL), d_hn, dh_mid)
    return dh, sq_err, (g_attn_norm, g_ffn_norm, g_a_w_in, g_a_w_out, g_a_q, g_a_k, g_b_w_in, g_b_w_out, g_w_gate, g_gate_bias,
                        g_out_gain, g_w_gu, g_w_down)


def _reduce_and_update(grads, sq_err, sharded_w, sharded_m, sharded_v, rep_w, rep_m, rep_v):
    (g_attn_norm, g_ffn_norm, g_a_w_in, g_a_w_out, g_a_q, g_a_k, g_b_w_in, g_b_w_out, g_w_gate, g_gate_bias, g_out_gain,
     g_w_gu, g_w_down) = grads
    depth, n_a, n_b = len(g_w_gu), len(g_a_w_in), len(g_b_w_in)

    g_w_gate = jnp.stack(g_w_gate)
    g_gate_bias = jnp.stack(g_gate_bias)
    small_parts = [_to_dev_cols(g_w_gate[:, 0:16, 0:512]), _to_dev_cols(g_gate_bias[:, 0:512]),
                   _to_dev_cols(g_w_gate[:, 16:32, 512:1024]), _to_dev_cols(g_gate_bias[:, 512:1024]),
                   _to_dev_cols(jnp.stack(g_out_gain).reshape(n_b, B_HEADS, B_VAL_DIM))]
    small_part = _pack_rows(jnp.concatenate([t.reshape(N_DEV, -1) for t in small_parts], axis=1), LANES).astype(BF16)
    families = [
        (0, g_a_w_in, 256),
        (1, [t.reshape(N_DEV, -1, D_MODEL) for t in g_a_w_out], 128),
        (2, [_to_dev_cols(t).astype(BF16) for t in g_b_w_in], 256),
        (8, [t.reshape(N_DEV, -1, D_MODEL) for t in g_b_w_out], 128),
        (9, [t.reshape(N_DEV, D_MODEL, FFN_BLK) for t in g_w_gu], 256),
        (10, [t.reshape(N_DEV, -1, D_MODEL) for t in g_w_down], 176),
    ]
    flat_parts = [p for _, parts, _ in families for p in parts] + [small_part]
    received = exchange_partials(flat_parts)
    sh_out = [None] * len(sharded_w)
    pos = 0
    for fam, parts, tile in families:
        w = sharded_w[fam]
        two_d = lambda t: t.reshape(-1, t.shape[-1])
        per_layer = []
        for layer in range(len(parts)):
            per_layer.append(adamw_shard(f"adamw_p{fam}_l{layer}", received[pos], two_d(w), two_d(sharded_m[fam]),
                                         two_d(sharded_v[fam]), layer, tile))
            pos += 1
        sh_out[fam] = [jnp.stack([per_layer[l][t] for l in range(len(parts))]).reshape(w.shape) for t in range(4)]
    small_ids = (3, 4, 5, 6, 7)
    pack_small = lambda ts: _pack_rows(jnp.concatenate([ts[i].reshape(-1) for i in small_ids]), LANES)
    small_out = adamw_shard("adamw_small", received[pos], pack_small(sharded_w), pack_small(sharded_m), pack_small(sharded_v), 0, 48)
    small_shapes = [sharded_w[i].shape for i in small_ids]
    for t in range(4):
        for i, val in zip(small_ids, _unpack(small_out[t].reshape(-1), small_shapes)):
            if sh_out[i] is None:
                sh_out[i] = [None] * 4
            sh_out[i][t] = val
    sh_grad, sh_delta, sh_m, sh_v = [[sh_out[i][t] for i in range(len(sharded_w))] for t in range(4)]

    rep_g = (jnp.stack([t[0] for t in g_attn_norm]), jnp.stack([t[0] for t in g_ffn_norm]), jnp.stack(g_a_q), jnp.stack(g_a_k))
    n_rep = sum(t.size for t in rep_w)
    n_rep_rows = -(-n_rep // (8 * LANES)) * 8
    n_loss_rows = 8 * D_MODEL // LANES

    def pack_rep(ts, tail):
        flat = jnp.concatenate([t.reshape(-1) for t in ts])
        flat = jnp.pad(flat, (0, n_rep_rows * LANES - n_rep))
        return jnp.concatenate([flat.reshape(n_rep_rows, LANES), tail], axis=0)

    zeros_tail = jnp.zeros((n_loss_rows, LANES), F32)
    rep_out = allreduce_adamw_replicated(pack_rep(rep_g, sq_err.reshape(n_loss_rows, LANES)), pack_rep(rep_w, zeros_tail),
                                         pack_rep(rep_m, zeros_tail), pack_rep(rep_v, zeros_tail + 1.0), n_loss_rows)
    rep_shapes = [t.shape for t in rep_w]
    r_grad, r_delta, r_m, r_v = [_unpack(p.reshape(-1), rep_shapes) for p in rep_out[:4]]
    loss = rep_out[4][0, 0]

    def ordered(rep, sh):
        return [rep[0], rep[1], sh[0], rep[2], rep[3]] + list(sh[1:])

    return loss, (*ordered(r_grad, sh_grad), *ordered(r_delta, sh_delta), *ordered(r_m, sh_m), *ordered(r_v, sh_v))
```

```python
import functools

import jax
import jax.numpy as jnp
from jax import lax
from jax.experimental import pallas as pl
from jax.experimental.pallas import tpu as pltpu

F32 = jnp.float32
BF16 = jnp.bfloat16
MXU_DTYPE = jnp.bfloat16

D_MODEL = 1024
N_DEV = 8
RMS_EPS = 1e-6
NEG_INF = -1e30
A_GROUPS = ((128, 1), (512, 4), (2048, 16))
A_HEAD_DIM = 64
A_HALF = 64
ATT_T = 128
ATT_TILE = 2048
ATT_FWD_BATCH = 8
ATT_BWD_BATCH = 4
B_HEADS = 4
B_KEY_DIM = 128
B_VAL_DIM = 256
B_CHUNK = 64
B_GATE_TAU = 16.0
GLA_TILE = 1024
FFN_HIDDEN = 2816
FFN_BLK = 2 * FFN_HIDDEN // N_DEV
ADAM_LR, ADAM_B1, ADAM_B2, ADAM_EPS, ADAM_WD, ADAM_STEP = 0.001, 0.9, 0.999, 1e-08, 0.01, 10
ROPE_THETA = 10000.0

V7X_VMEM_LIMIT = 60 * 1024 * 1024
LANES = 128
MESH = pl.DeviceIdType.MESH
ANY = pl.BlockSpec(memory_space=pl.ANY)
VMEM_SPEC = pl.BlockSpec(memory_space=pltpu.VMEM)

NN = ((1,), (0,))
NT = ((1,), (1,))
TN = ((0,), (0,))


def _dot(a, b, dims):
    return lax.dot_general(a.astype(MXU_DTYPE), b.astype(MXU_DTYPE), (dims, ((), ())), preferred_element_type=F32)


@jax.custom_vjp
def dot_nn(a, b):
    return _dot(a, b, NN)


@jax.custom_vjp
def dot_nt(a, b):
    return _dot(a, b, NT)


@jax.custom_vjp
def dot_tn(a, b):
    return _dot(a, b, TN)


dot_nn.defvjp(lambda a, b: (_dot(a, b, NN), (a, b)), lambda r, g: (dot_nt(g, r[1]), dot_tn(r[0], g)))
dot_nt.defvjp(lambda a, b: (_dot(a, b, NT), (a, b)), lambda r, g: (dot_nn(g, r[1]), dot_tn(g, r[0])))
dot_tn.defvjp(lambda a, b: (_dot(a, b, TN), (a, b)), lambda r, g: (dot_nt(r[1], g), dot_nn(r[0], g)))


def _bdot(a, b, dims):
    dn = (tuple((d[0] + 1,) for d in dims), ((0,), (0,)))
    return lax.dot_general(a.astype(MXU_DTYPE), b.astype(MXU_DTYPE), dn, preferred_element_type=F32)


@jax.custom_vjp
def bdot_nn(a, b):
    return _bdot(a, b, NN)


@jax.custom_vjp
def bdot_nt(a, b):
    return _bdot(a, b, NT)


@jax.custom_vjp
def bdot_tn(a, b):
    return _bdot(a, b, TN)


bdot_nn.defvjp(lambda a, b: (_bdot(a, b, NN), (a, b)), lambda r, g: (bdot_nt(g, r[1]), bdot_tn(r[0], g)))
bdot_nt.defvjp(lambda a, b: (_bdot(a, b, NT), (a, b)), lambda r, g: (bdot_nn(g, r[1]), bdot_tn(g, r[0])))
bdot_tn.defvjp(lambda a, b: (_bdot(a, b, TN), (a, b)), lambda r, g: (bdot_nt(r[1], g), bdot_nn(r[0], g)))


def _dot_f32(a, b):
    return lax.dot_general(a, b, (NN, ((), ())), precision=lax.Precision.HIGHEST, preferred_element_type=F32)


def _tri(n, upper):
    r = lax.broadcasted_iota(jnp.int32, (n, n), 0)
    c = lax.broadcasted_iota(jnp.int32, (n, n), 1)
    return jnp.where((c >= r) if upper else (c <= r), 1.0, 0.0).astype(F32)


def _chunk_cumsum(x, reverse):
    tri = jnp.broadcast_to(_tri(x.shape[1], reverse), (x.shape[0], x.shape[1], x.shape[1]))
    return lax.dot_general(tri, x, (((2,), (1,)), ((0,), (0,))), precision=lax.Precision.HIGHEST, preferred_element_type=F32)


@functools.partial(jax.custom_vjp, nondiff_argnums=(1,))
def cumsum_chunks(x, reverse):
    return _chunk_cumsum(x, reverse)


cumsum_chunks.defvjp(lambda x, reverse: (_chunk_cumsum(x, reverse), None), lambda reverse, _, g: (_chunk_cumsum(g, not reverse),))


def _swap32_raw(x):
    lane = lax.broadcasted_iota(jnp.int32, x.shape, 1)
    return jnp.where((lane % 64) < 32, pltpu.roll(x, 96, 1), pltpu.roll(x, 32, 1))


@jax.custom_vjp
def swap32(x):
    return _swap32_raw(x)


swap32.defvjp(lambda x: (_swap32_raw(x), None), lambda _, g: (_swap32_raw(g),))


def _rms(x, gain):
    return x * lax.rsqrt(jnp.mean(x * x, axis=-1, keepdims=True) + RMS_EPS) * gain


def _sigmoid(x):
    return 1.0 / (1.0 + jnp.exp(-x))


def _log_sigmoid(x):
    return jnp.minimum(x, 0.0) - jnp.log(1.0 + jnp.exp(-jnp.abs(x)))


def _qk_prep(x, tab, gain):
    lo = lax.broadcasted_iota(jnp.int32, (1, LANES), 1) < A_HEAD_DIM
    x2 = x * x
    s_lo = jnp.sum(jnp.where(lo, x2, 0.0), axis=-1, keepdims=True)
    s_hi = jnp.sum(jnp.where(lo, 0.0, x2), axis=-1, keepdims=True)
    xn = (x * lax.rsqrt(jnp.where(lo, s_lo, s_hi) / A_HEAD_DIM + RMS_EPS)) * gain
    return xn * tab[:, :LANES] + swap32(xn) * tab[:, LANES:]


def _stack_heads(x):
    lo = lax.broadcasted_iota(jnp.int32, (1, LANES), 1) < A_HEAD_DIM
    return jnp.concatenate([jnp.where(lo, x, 0.0), jnp.where(lo, 0.0, x)], axis=-2)


def _unstack_heads(x):
    t = x.shape[-2] // 2
    lo = lax.broadcasted_iota(jnp.int32, (1, LANES), 1) < A_HEAD_DIM
    return jnp.where(lo, x[..., :t, :], x[..., t:, :])


def _attn_scores(q, k, valid):
    s = _bdot(_stack_heads(q), k, NT) * (A_HEAD_DIM ** -0.5)
    return jnp.where(valid, s, NEG_INF)


def _attn_job(q, k, v, valid):
    s = _attn_scores(q, k, valid)
    mx = jnp.max(s, axis=-1, keepdims=True)
    p = jnp.exp(s - mx)
    l = jnp.sum(p, axis=-1, keepdims=True)
    out = _unstack_heads(_bdot(p, v, NN) / l)
    lse = mx + jnp.log(l)
    t = q.shape[1]
    lo = lax.broadcasted_iota(jnp.int32, (1, LANES), 1) < A_HEAD_DIM
    return out, jnp.where(lo, lse[:, :t], lse[:, t:])


def _attn_job_bwd(q, k, v, valid, d_out, out, lse, d_lse):
    qs = _stack_heads(q)
    lo = lax.broadcasted_iota(jnp.int32, (1, LANES), 1) < A_HEAD_DIM
    other = pltpu.roll(lse, A_HEAD_DIM, 2)
    row_lse = jnp.concatenate([jnp.where(lo, lse, other), jnp.where(lo, other, lse)], axis=1)
    p = jnp.exp(_attn_scores(q, k, valid) - jnp.concatenate([row_lse, row_lse], axis=2))
    dos = _stack_heads(d_out)
    dv = _bdot(p, dos, TN)
    dp = _bdot(dos, v, NT)
    inner = jnp.sum(dos * _stack_heads(out) - _stack_heads(d_lse), axis=-1, keepdims=True)
    ds = p * (dp - inner) * (A_HEAD_DIM ** -0.5)
    return _unstack_heads(_bdot(ds, k, NN)), _bdot(ds, qs, TN), dv


def _merge_groups(o0, o1, o2, l0, l1, l2):
    mx = lax.stop_gradient(jnp.maximum(jnp.maximum(l0, l1), l2))
    e0, e1, e2 = jnp.exp(l0 - mx), jnp.exp(l1 - mx), jnp.exp(l2 - mx)
    den = e0 + e1 + e2
    return (e0 / den) * o0 + (e1 / den) * o1 + (e2 / den) * o2


def _gla_tile(q, k, v, la, st, reverse):
    t = q.shape[0]
    nc = t // B_CHUNK
    split = lambda x: x.reshape(nc, B_CHUNK, x.shape[1])
    q, k, v, la = split(q * (B_KEY_DIM ** -0.5)), split(k), split(v), split(la)
    r = lax.broadcasted_iota(jnp.int32, (1, B_CHUNK, B_CHUNK), 1)
    c = lax.broadcasted_iota(jnp.int32, (1, B_CHUNK, B_CHUNK), 2)
    mask = (c > r) if reverse else (c <= r)
    b = cumsum_chunks(la, reverse)
    tot = jnp.sum(la, axis=1, keepdims=True)
    q_t = q * jnp.exp(b)
    k_t = k * jnp.exp(-b)
    k_end = k * jnp.exp(tot - b)
    attn = jnp.where(mask, bdot_nt(q_t, k_t), 0.0)
    kv = bdot_tn(v, k_end)
    decay = jnp.exp(tot)
    entering = [None] * nc
    for ci in (range(nc - 1, -1, -1) if reverse else range(nc)):
        entering[ci] = st
        st = st * decay[ci] + kv[ci]
    out = bdot_nn(attn, v) + bdot_nt(q_t, jnp.concatenate([e[None] for e in entering], axis=0))
    return out.reshape(t, out.shape[2]), st


def _gla_post(o_f, o_b, r, gain):
    o = o_f + o_b
    heads = [_rms(o[:, h * B_VAL_DIM:(h + 1) * B_VAL_DIM], gain[:, h * B_VAL_DIM:(h + 1) * B_VAL_DIM]) for h in range(B_HEADS)]
    return jnp.concatenate(heads, axis=1) * (r * _sigmoid(r))


def _gate(z, wg, bias):
    return _log_sigmoid(dot_nn(z, wg) + bias) / B_GATE_TAU


def _swiglu_act(g, u):
    return (g * _sigmoid(g)) * u


def _adamw(w, g, m, v):
    m = ADAM_B1 * m + (1.0 - ADAM_B1) * g
    v = ADAM_B2 * v + (1.0 - ADAM_B2) * jnp.square(g)
    m_hat = m / (1.0 - ADAM_B1 ** ADAM_STEP)
    v_hat = v / (1.0 - ADAM_B2 ** ADAM_STEP)
    delta = -ADAM_LR * (m_hat / (jnp.sqrt(v_hat) + ADAM_EPS) + ADAM_WD * w)
    return delta, m, v


def _params(sem=None):
    return pltpu.CompilerParams(dimension_semantics=sem, vmem_limit_bytes=V7X_VMEM_LIMIT)


def _row_tile(s, want=512):
    t = min(want, s)
    assert s % t == 0
    return t


def _matmul(name, a, b, *, dims, grid, a_spec, b_spec, o_spec, out_shape, red_axis=None, res=None, res_spec=None, acc_shape=None,
            norm_gain=None):
    n_red = grid[red_axis] if red_axis is not None else 1
    n_in = 2 + (res is not None) + (norm_gain is not None)

    def body(*refs):
        a_ref, b_ref = refs[0], refs[1]
        r_ref = refs[2] if res is not None else None
        g_ref = refs[n_in - 1] if norm_gain is not None else None
        o_ref = refs[n_in]
        n_ref = refs[n_in + 1] if norm_gain is not None else None
        prod = lax.dot_general(a_ref[...].astype(MXU_DTYPE), b_ref[...].astype(MXU_DTYPE), (dims, ((), ())),
                               preferred_element_type=F32)
        if red_axis is None:
            if r_ref is not None:
                prod = prod + r_ref[...]
            o_ref[...] = prod.astype(o_ref.dtype)
            if n_ref is not None:
                n_ref[...] = _rms(prod, g_ref[...]).astype(n_ref.dtype)
            return
        acc = refs[-1] if acc_shape is not None else o_ref
        k = pl.program_id(red_axis)

        @pl.when(k == 0)
        def _():
            acc[...] = prod + r_ref[...] if r_ref is not None else prod

        @pl.when(k > 0)
        def _():
            acc[...] += prod

        if acc_shape is not None or n_ref is not None:
            @pl.when(k == n_red - 1)
            def _():
                if acc_shape is not None:
                    o_ref[...] = acc[...].astype(o_ref.dtype)
                if n_ref is not None:
                    n_ref[...] = _rms(acc[...], g_ref[...]).astype(n_ref.dtype)

    ins = [a, b] + ([res] if res is not None else []) + ([norm_gain] if norm_gain is not None else [])
    specs = [a_spec, b_spec] + ([res_spec] if res is not None else [])
    out_specs, out_shapes = o_spec, out_shape
    if norm_gain is not None:
        specs.append(pl.BlockSpec(norm_gain.shape, lambda *g: (0, 0)))
        out_specs, out_shapes = [o_spec, o_spec], [out_shape, jax.ShapeDtypeStruct(out_shape.shape, BF16)]
    sem = tuple("arbitrary" if i == red_axis else "parallel" for i in range(len(grid)))
    return pl.pallas_call(body, name=name, grid=grid, in_specs=specs, out_specs=out_specs, out_shape=out_shapes,
                          scratch_shapes=[pltpu.VMEM(acc_shape, F32)] if acc_shape is not None else [],
                          compiler_params=_params(sem))(*ins)


def mm_nn(name, x, w, *, res=None, out_dtype=F32, tn=1024, norm_gain=None):
    m, k = x.shape
    n = w.shape[1]
    tm, tn = _row_tile(m, 1024), min(tn, n)
    return _matmul(name, x, w, dims=NN, grid=(n // tn, m // tm),
                   a_spec=pl.BlockSpec((tm, k), lambda j, i: (i, 0)), b_spec=pl.BlockSpec((k, tn), lambda j, i: (0, j)),
                   o_spec=pl.BlockSpec((tm, tn), lambda j, i: (i, j)), out_shape=jax.ShapeDtypeStruct((m, n), out_dtype),
                   res=res, res_spec=pl.BlockSpec((tm, tn), lambda j, i: (i, j)), norm_gain=norm_gain)


def mm_nt(name, dy, w, *, res=None, tn=2304):
    m, n = dy.shape
    k = w.shape[0]
    tm, tn = _row_tile(m, 1024), (tn if n % tn == 0 else min(1024, n))
    return _matmul(name, dy, w, dims=NT, grid=(m // tm, n // tn), red_axis=1,
                   a_spec=pl.BlockSpec((tm, tn), lambda i, j: (i, j)), b_spec=pl.BlockSpec((k, tn), lambda i, j: (0, j)),
                   o_spec=pl.BlockSpec((tm, k), lambda i, j: (i, 0)), out_shape=jax.ShapeDtypeStruct((m, k), F32),
                   res=res, res_spec=pl.BlockSpec((tm, k), lambda i, j: (i, 0)))


def mm_tn(name, x, dy, *, tn=1024, by_block=False):
    m, k = x.shape
    n = dy.shape[1]
    tm, tn = _row_tile(m, 2048), min(tn, n)
    if by_block:
        o_spec, out_shape, acc = pl.BlockSpec((None, k, tn), lambda j, i: (j, 0, 0)), jax.ShapeDtypeStruct((n // tn, k, tn), BF16), (k, tn)
    else:
        o_spec, out_shape, acc = pl.BlockSpec((k, tn), lambda j, i: (0, j)), jax.ShapeDtypeStruct((k, n), F32), None
    return _matmul(name, x, dy, dims=TN, grid=(n // tn, m // tm), red_axis=1,
                   a_spec=pl.BlockSpec((tm, k), lambda j, i: (i, 0)), b_spec=pl.BlockSpec((tm, tn), lambda j, i: (i, j)),
                   o_spec=o_spec, out_shape=out_shape, acc_shape=acc)


def _rows_call(name, body, ins, outs, s, tile):
    in_specs = []
    for a, kind in ins:
        if kind == "row":
            in_specs.append(pl.BlockSpec((tile, a.shape[1]), lambda i: (i, 0)))
        else:
            in_specs.append(pl.BlockSpec(a.shape, lambda i, nd=a.ndim: (0,) * nd))
    out_specs, out_shape = [], []
    for cols, dt, kind in outs:
        if kind == "row":
            out_specs.append(pl.BlockSpec((tile, cols), lambda i: (i, 0)))
            out_shape.append(jax.ShapeDtypeStruct((s, cols), dt))
        else:
            out_specs.append(pl.BlockSpec((8, cols), lambda i: (0, 0)))
            out_shape.append(jax.ShapeDtypeStruct((8, cols), dt))
    has_acc = any(kind == "acc" for _, _, kind in outs)
    return pl.pallas_call(body, name=name, grid=(s // tile,), in_specs=in_specs, out_specs=out_specs, out_shape=out_shape,
                          compiler_params=_params(("arbitrary",) if has_acc else ("parallel",)))(*[a for a, _ in ins])


def _acc_rows(ref, val):
    @pl.when(pl.program_id(0) == 0)
    def _():
        ref[...] = jnp.zeros_like(ref)

    ref[...] += jnp.broadcast_to(val, ref.shape)


def rmsnorm_fwd(name, h, gain):
    s = h.shape[0]

    def body(h_ref, g_ref, o_ref):
        o_ref[...] = _rms(h_ref[...], g_ref[...]).astype(o_ref.dtype)

    return _rows_call(name, body, [(h, "row"), (gain, "full")], [(D_MODEL, BF16, "row")], s, _row_tile(s))[0]


def rmsnorm_bwd(name, h, gain, d_hn, d_res):
    s = h.shape[0]

    def body(h_ref, g_ref, dy_ref, dr_ref, dh_ref, dg_ref):
        _, vjp = jax.vjp(_rms, h_ref[...], g_ref[...])
        dh, dg = vjp(dy_ref[...])
        dh_ref[...] = dh + dr_ref[...]
        _acc_rows(dg_ref, dg)

    return _rows_call(name, body, [(h, "row"), (gain, "full"), (d_hn, "row"), (d_res, "row")],
                      [(D_MODEL, F32, "row"), (D_MODEL, F32, "acc")], s, _row_tile(s))


def loss_and_grad(y, target):
    s = y.shape[0]
    tile = _row_tile(s)

    def body(y_ref, t_ref, dy_ref, acc_ref):
        diff = y_ref[...] - t_ref[...]
        dy_ref[...] = diff * (1.0 / D_MODEL)

        @pl.when(pl.program_id(0) == 0)
        def _():
            acc_ref[...] = jnp.zeros_like(acc_ref)

        acc_ref[...] += jnp.sum((diff * diff).reshape(tile // 8, 8, D_MODEL), axis=0)

    return _rows_call("loss_head", body, [(y, "row"), (target, "row")], [(D_MODEL, F32, "row"), (D_MODEL, F32, "acc")], s, tile)


def _rope_table(s):
    half = A_HEAD_DIM // 2
    inv_freq = ROPE_THETA ** (-jnp.arange(half, dtype=F32) / half)
    ang = jnp.arange(s).astype(F32)[:, None] * inv_freq[None, :]
    cos, sin = jnp.cos(ang), jnp.sin(ang)
    return jnp.concatenate([cos, cos, cos, cos, -sin, sin, -sin, sin], axis=1)


def _attn_geometry(s, dil):
    tile = min(ATT_TILE, s)
    halo = A_HALF * dil
    assert s % tile == 0 and tile % (ATT_T * dil) == 0 and tile % halo == 0
    return tile, halo, tile // (ATT_T * dil)


def _attn_in_specs(grp, s, tile, halo):
    hb, n_hb = tile // halo, s // halo
    cq, ck, cv = (24 * grp + 8 * t for t in range(3))

    def main(col, per_pair, width=LANES):
        return pl.BlockSpec((tile, width), lambda i, j: (i, col + per_pair * j))

    def prev(col, per_pair, width=LANES):
        return pl.BlockSpec((halo, width), lambda i, j: (jnp.maximum(i * hb - 1, 0), col + per_pair * j))

    def nxt(col, per_pair, width=LANES):
        return pl.BlockSpec((halo, width), lambda i, j: (jnp.minimum((i + 1) * hb, n_hb - 1), col + per_pair * j))

    return [main(cq, 1), prev(ck, 1), main(ck, 1), nxt(ck, 1), prev(cv, 1), main(cv, 1), nxt(cv, 1),
            prev(0, 0, 256), main(0, 0, 256), nxt(0, 0, 256)]


def _attn_valid(first_job, n_jobs, dil, tile_base, length):
    r = lax.broadcasted_iota(jnp.int32, (1, 2 * ATT_T, 1), 1)
    tq = jnp.where(r >= ATT_T, r - ATT_T, r)
    rel = lax.broadcasted_iota(jnp.int32, (1, 1, ATT_T + 2 * A_HALF), 2) - A_HALF
    job = first_job + lax.broadcasted_iota(jnp.int32, (n_jobs, 1, 1), 0)
    tk = tile_base + (job // dil) * ATT_T + rel
    return (jnp.abs(rel - tq) <= A_HALF) & (tk >= 0) & (tk < length)


def _jobs(dil, n_sub):
    return [u * ATT_T * dil + p for u in range(n_sub) for p in range(dil)]


def _gather_jobs(ref, starts, size, dil):
    return jnp.concatenate([ref[_rows(st, size, dil), :][None] for st in starts], axis=0)


def _rows(start, size, dil):
    return pl.ds(start, size, stride=dil) if dil > 1 else pl.ds(start, size)


def attn_fwd(name, qkv, tab, gq, gk, grp, dil):
    s = qkv.shape[0]
    tile, halo, n_sub = _attn_geometry(s, dil)
    length, per_tile = s // dil, tile // dil
    nk = ATT_T + 2 * A_HALF

    def body(q_ref, kp_ref, km_ref, kn_ref, vp_ref, vm_ref, vn_ref, tp_ref, tm_ref, tn_ref, gq_ref, gk_ref, o_ref, lse_ref,
             q_buf, k_buf, v_buf):
        i, pair = pl.program_id(0), pl.program_id(1)
        q_buf[...] = _qk_prep(q_ref[...], tm_ref[...], gq_ref[...])
        for ref, t_ref, lo, n in ((kp_ref, tp_ref, 0, halo), (km_ref, tm_ref, halo, tile), (kn_ref, tn_ref, halo + tile, halo)):
            k_buf[lo:lo + n, :] = _qk_prep(ref[...], t_ref[...], gk_ref[...])
        for ref, lo, n in ((vp_ref, 0, halo), (vm_ref, halo, tile), (vn_ref, halo + tile, halo)):
            v_buf[lo:lo + n, :] = ref[...]

        jobs = _jobs(dil, n_sub)
        for g0 in range(0, len(jobs), ATT_FWD_BATCH):
            starts = jobs[g0:g0 + ATT_FWD_BATCH]
            valid = _attn_valid(g0, len(starts), dil, i * per_tile, length)
            o, lse = _attn_job(_gather_jobs(q_buf, starts, ATT_T, dil), _gather_jobs(k_buf, starts, nk, dil),
                               _gather_jobs(v_buf, starts, nk, dil), valid)
            for n, st in enumerate(starts):
                o_ref[_rows(st, ATT_T, dil), :] = o[n]
                lse_ref[_rows(st, ATT_T, dil), :] = lse[n]

    full = lambda a: pl.BlockSpec(a.shape, lambda i, j: (0, 0))
    return pl.pallas_call(
        body, name=name, grid=(s // tile, D_MODEL // LANES),
        in_specs=_attn_in_specs(grp, s, tile, halo) + [full(gq), full(gk)],
        out_specs=[pl.BlockSpec((tile, LANES), lambda i, j: (i, j)), pl.BlockSpec((tile, LANES), lambda i, j: (i, j))],
        out_shape=[jax.ShapeDtypeStruct((s, D_MODEL), F32), jax.ShapeDtypeStruct((s, D_MODEL), F32)],
        scratch_shapes=[pltpu.VMEM((tile, LANES), F32), pltpu.VMEM((tile + 2 * halo, LANES), F32), pltpu.VMEM((tile + 2 * halo, LANES), F32)],
        compiler_params=_params(("parallel", "parallel")),
    )(qkv, qkv, qkv, qkv, qkv, qkv, qkv, tab, tab, tab, gq, gk)


def attn_bwd(name, qkv, tab, gq, gk, o, lse, d_o, d_lse, grp, dil, d_qkv):
    s = qkv.shape[0]
    tile, halo, n_sub = _attn_geometry(s, dil)
    length, per_tile = s // dil, tile // dil
    nt = s // tile
    nk = ATT_T + 2 * A_HALF
    pieces = ((0, halo), (halo, tile), (halo + tile, halo))

    def body(q_ref, kp_ref, km_ref, kn_ref, vp_ref, vm_ref, vn_ref, tp_ref, tm_ref, tn_ref, gq_ref, gk_ref, o_ref, l_ref, do_ref, dl_ref,
             _, dq_ref, dkp_ref, dkm_ref, dkn_ref, dvp_ref, dvm_ref, dvn_ref, dgq_ref, dgk_ref, q_buf, k_buf, v_buf, dq_buf, dk_buf, dv_buf):
        i, pair = pl.program_id(0), pl.program_id(1)
        k_refs, t_refs = (kp_ref, km_ref, kn_ref), (tp_ref, tm_ref, tn_ref)
        qn, q_vjp = jax.vjp(lambda x, g: _qk_prep(x, tm_ref[...], g), q_ref[...], gq_ref[...])
        q_buf[...] = qn
        k_vjps = []
        for ref, t_ref, (lo, n) in zip(k_refs, t_refs, pieces):
            kn, k_vjp = jax.vjp(lambda x, g: _qk_prep(x, t_ref[...], g), ref[...], gk_ref[...])
            k_buf[lo:lo + n, :] = kn
            k_vjps.append(k_vjp)
        for ref, (lo, n) in zip((vp_ref, vm_ref, vn_ref), pieces):
            v_buf[lo:lo + n, :] = ref[...]
        dk_buf[...] = jnp.zeros_like(dk_buf)
        dv_buf[...] = jnp.zeros_like(dv_buf)
        jobs = _jobs(dil, n_sub)
        for g0 in range(0, len(jobs), ATT_BWD_BATCH):
            starts = jobs[g0:g0 + ATT_BWD_BATCH]
            valid = _attn_valid(g0, len(starts), dil, i * per_tile, length)
            own = lambda ref: _gather_jobs(ref, starts, ATT_T, dil)
            dq, dk, dv = _attn_job_bwd(own(q_buf), _gather_jobs(k_buf, starts, nk, dil), _gather_jobs(v_buf, starts, nk, dil), valid,
                                       own(do_ref), own(o_ref), own(l_ref), own(dl_ref))
            for n, st in enumerate(starts):
                dq_buf[_rows(st, ATT_T, dil), :] = dq[n]
                dk_buf[_rows(st, nk, dil), :] += dk[n]
                dv_buf[_rows(st, nk, dil), :] += dv[n]
        dq, dgq = q_vjp(dq_buf[...])
        dq_ref[...] = dq.astype(dq_ref.dtype)
        dgk = jnp.zeros((1, LANES), F32)
        for k_vjp, out_ref, (lo, n) in zip(k_vjps, (dkp_ref, dkm_ref, dkn_ref), pieces):
            out_ref[...], dgk_piece = k_vjp(dk_buf[lo:lo + n, :])
            dgk = dgk + dgk_piece
        for out_ref, (lo, n) in zip((dvp_ref, dvm_ref, dvn_ref), pieces):
            out_ref[...] = dv_buf[lo:lo + n, :]

        @pl.when((i == 0) & (pair == 0))
        def _():
            dgq_ref[...] = jnp.zeros_like(dgq_ref)
            dgk_ref[...] = jnp.zeros_like(dgk_ref)

        dgq_ref[...] += jnp.broadcast_to(dgq + pltpu.roll(dgq, A_HEAD_DIM, 1), dgq_ref.shape)
        dgk_ref[...] += jnp.broadcast_to(dgk + pltpu.roll(dgk, A_HEAD_DIM, 1), dgk_ref.shape)

    full = lambda a: pl.BlockSpec(a.shape, lambda i, j: (0, 0))
    main_o = pl.BlockSpec((tile, LANES), lambda i, j: (i, j))
    edge_o = pl.BlockSpec((None, halo, LANES), lambda i, j: (i, 0, j))
    main_s = jax.ShapeDtypeStruct((s, D_MODEL), F32)
    edge_s = jax.ShapeDtypeStruct((nt, halo, D_MODEL), F32)
    acc_o = pl.BlockSpec((8, LANES), lambda i, j: (0, 0))
    acc_s = jax.ShapeDtypeStruct((8, LANES), F32)
    big = pltpu.VMEM((tile + 2 * halo, LANES), F32)
    own = pltpu.VMEM((tile, LANES), F32)
    outs = pl.pallas_call(
        body, name=name, grid=(nt, D_MODEL // LANES),
        in_specs=_attn_in_specs(grp, s, tile, halo) + [full(gq), full(gk), main_o, main_o, main_o, main_o, ANY],
        out_specs=[pl.BlockSpec((tile, LANES), lambda i, j: (i, 24 * grp + j)), edge_o, main_o, edge_o, edge_o, main_o, edge_o, acc_o, acc_o],
        out_shape=[jax.ShapeDtypeStruct(d_qkv.shape, d_qkv.dtype), edge_s, main_s, edge_s, edge_s, main_s, edge_s, acc_s, acc_s],
        scratch_shapes=[own, big, big, own, big, big],
        input_output_aliases={16: 0},
        compiler_params=_params(("arbitrary", "arbitrary")),
    )(qkv, qkv, qkv, qkv, qkv, qkv, qkv, tab, tab, tab, gq, gk, o, lse, d_o, d_lse, d_qkv)
    d_qkv, dkp, dkm, dkn, dvp, dvm, dvn, dgq, dgk = outs
    return d_qkv, (dkp, dkm, dkn), (dvp, dvm, dvn), dgq, dgk


def attn_combine(name, parts, s, dil, d_qkv, col):
    prev_part, main_part, next_part = parts
    tile, halo, _ = _attn_geometry(s, dil)
    nt = s // tile
    cols = D_MODEL // 2

    def body(m_ref, from_prev_ref, from_next_ref, _, o_ref):
        i = pl.program_id(0)
        o_ref[...] = m_ref[...].astype(o_ref.dtype)
        head = m_ref[0:halo, :] + jnp.where(i > 0, from_prev_ref[...], 0.0)
        o_ref[0:halo, :] = head.astype(o_ref.dtype)
        tail = m_ref[tile - halo:tile, :] + jnp.where(i < nt - 1, from_next_ref[...], 0.0)
        o_ref[tile - halo:tile, :] = tail.astype(o_ref.dtype)

    return pl.pallas_call(
        body, name=name, grid=(nt, D_MODEL // cols),
        in_specs=[pl.BlockSpec((tile, cols), lambda i, c: (i, c)),
                  pl.BlockSpec((None, halo, cols), lambda i, c: (jnp.maximum(i - 1, 0), 0, c)),
                  pl.BlockSpec((None, halo, cols), lambda i, c: (jnp.minimum(i + 1, nt - 1), 0, c)), ANY],
        out_specs=pl.BlockSpec((tile, cols), lambda i, c: (i, (D_MODEL // cols) * col + c)),
        out_shape=jax.ShapeDtypeStruct(d_qkv.shape, d_qkv.dtype), input_output_aliases={3: 0},
        compiler_params=_params(("parallel", "parallel")),
    )(main_part, next_part, prev_part, d_qkv)


def attn_merge_fwd(name, os_, lses):
    s = os_[0].shape[0]

    def body(o0, o1, o2, l0, l1, l2, out_ref):
        out_ref[...] = _merge_groups(o0[...], o1[...], o2[...], l0[...], l1[...], l2[...]).astype(out_ref.dtype)

    return _rows_call(name, body, [(a, "row") for a in (*os_, *lses)], [(D_MODEL, BF16, "row")], s, _row_tile(s, 256))[0]


def attn_merge_bwd(name, os_, lses, d_out):
    s = os_[0].shape[0]

    def body(o0, o1, o2, l0, l1, l2, d_ref, *outs):
        _, vjp = jax.vjp(_merge_groups, o0[...], o1[...], o2[...], l0[...], l1[...], l2[...])
        for ref, val in zip(outs, vjp(d_ref[...])):
            ref[...] = val

    return _rows_call(name, body, [(a, "row") for a in (*os_, *lses, d_out)],
                      [(D_MODEL, F32, "row")] * 6, s, _row_tile(s, 256))


def gla_gate_fwd(name, z, wg, bias):
    s = z.shape[0]

    def body(z_ref, w_ref, b_ref, o_ref):
        o_ref[...] = _gate(z_ref[...], w_ref[...], b_ref[...])

    return _rows_call(name, body, [(z, "row"), (wg, "full"), (bias, "full")], [(D_MODEL, F32, "row")], s, _row_tile(s))[0]


def gla_gate_bwd(name, z, wg, bias, d_la_f, d_la_b):
    s = z.shape[0]
    tile = _row_tile(s)

    def body(z_ref, w_ref, b_ref, df_ref, db_ref, dz_ref, dw_ref, dbias_ref):
        _, vjp = jax.vjp(_gate, z_ref[...], w_ref[...], b_ref[...])
        dz, dw, dbias = vjp(jnp.concatenate([df_ref[...], db_ref[...]], axis=1))
        dz_ref[...] = dz

        @pl.when(pl.program_id(0) == 0)
        def _():
            dw_ref[...] = jnp.zeros_like(dw_ref)

        dw_ref[...] += dw
        _acc_rows(dbias_ref, dbias)

    return pl.pallas_call(
        body, name=name, grid=(s // tile,),
        in_specs=[pl.BlockSpec((tile, LANES), lambda i: (i, 0)), pl.BlockSpec(wg.shape, lambda i: (0, 0)),
                  pl.BlockSpec(bias.shape, lambda i: (0, 0)), pl.BlockSpec((tile, 512), lambda i: (i, 0)),
                  pl.BlockSpec((tile, 512), lambda i: (i, 0))],
        out_specs=[pl.BlockSpec((tile, LANES), lambda i: (i, 0)), pl.BlockSpec(wg.shape, lambda i: (0, 0)),
                   pl.BlockSpec((8, D_MODEL), lambda i: (0, 0))],
        out_shape=[jax.ShapeDtypeStruct((s, LANES), F32), jax.ShapeDtypeStruct(wg.shape, F32), jax.ShapeDtypeStruct((8, D_MODEL), F32)],
        compiler_params=_params(("arbitrary",)),
    )(z, wg, bias, d_la_f, d_la_b)


def _gla_in_specs(tile, order, la_col0):
    t = order
    return [pl.BlockSpec((tile, B_KEY_DIM), lambda h, n: (t(n), h)),
            pl.BlockSpec((tile, B_KEY_DIM), lambda h, n: (t(n), B_HEADS + h)),
            pl.BlockSpec((tile, B_VAL_DIM), lambda h, n: (t(n), B_HEADS + h)),
            pl.BlockSpec((tile, B_KEY_DIM), lambda h, n: (t(n), la_col0 + h))]


def gla_fwd(name, proj, la, reverse):
    s = proj.shape[0]
    tile = _row_tile(s, GLA_TILE)
    nt = s // tile
    order = (lambda n: nt - 1 - n) if reverse else (lambda n: n)

    def body(q_ref, k_ref, v_ref, la_ref, o_ref, st_ref, st_scr):
        @pl.when(pl.program_id(1) == 0)
        def _():
            st_scr[...] = jnp.zeros_like(st_scr)

        st_ref[...] = st_scr[...]
        o, st = _gla_tile(q_ref[...], k_ref[...], v_ref[...], la_ref[...], st_scr[...], reverse)
        o_ref[...] = o
        st_scr[...] = st

    return pl.pallas_call(
        body, name=name, grid=(B_HEADS, nt), in_specs=_gla_in_specs(tile, order, B_HEADS if reverse else 0),
        out_specs=[pl.BlockSpec((tile, B_VAL_DIM), lambda h, n: (order(n), h)),
                   pl.BlockSpec((None, None, B_VAL_DIM, B_KEY_DIM), lambda h, n: (h, order(n), 0, 0))],
        out_shape=[jax.ShapeDtypeStruct((s, D_MODEL), F32), jax.ShapeDtypeStruct((B_HEADS, nt, B_VAL_DIM, B_KEY_DIM), F32)],
        scratch_shapes=[pltpu.VMEM((B_VAL_DIM, B_KEY_DIM), F32)],
        compiler_params=_params(("parallel", "arbitrary")),
    )(proj, proj, proj, la)


def gla_bwd(name, proj, la, states, d_o, reverse, prev=None):
    s = proj.shape[0]
    tile = _row_tile(s, GLA_TILE)
    nt = s // tile
    order = (lambda n: n) if reverse else (lambda n: nt - 1 - n)

    def body(*refs):
        q_ref, k_ref, v_ref, la_ref, st_ref, do_ref = refs[:6]
        rest = refs[6:]
        prev_refs = rest[:3] if prev is not None else None
        dq_ref, dk_ref, dv_ref, dla_ref, dst_scr = rest[3:] if prev is not None else rest

        @pl.when(pl.program_id(1) == 0)
        def _():
            dst_scr[...] = jnp.zeros_like(dst_scr)

        _, vjp = jax.vjp(functools.partial(_gla_tile, reverse=reverse), q_ref[...], k_ref[...], v_ref[...], la_ref[...], st_ref[...])
        dq, dk, dv, dla, dst = vjp((do_ref[...], dst_scr[...]))
        if prev_refs is not None:
            dq, dk, dv = dq + prev_refs[0][...], dk + prev_refs[1][...], dv + prev_refs[2][...]
        dq_ref[...], dk_ref[...], dv_ref[...], dla_ref[...] = dq, dk, dv, dla
        dst_scr[...] = dst

    key_spec = pl.BlockSpec((tile, B_KEY_DIM), lambda h, n: (order(n), h))
    val_spec = pl.BlockSpec((tile, B_VAL_DIM), lambda h, n: (order(n), h))
    in_specs = _gla_in_specs(tile, order, B_HEADS if reverse else 0) + [
        pl.BlockSpec((None, None, B_VAL_DIM, B_KEY_DIM), lambda h, n: (h, order(n), 0, 0)), val_spec]
    ins = [proj, proj, proj, la, states, d_o]
    if prev is not None:
        in_specs += [key_spec, key_spec, val_spec]
        ins += list(prev)
    return pl.pallas_call(
        body, name=name, grid=(B_HEADS, nt), in_specs=in_specs,
        out_specs=[key_spec, key_spec, val_spec, key_spec],
        out_shape=[jax.ShapeDtypeStruct((s, 512), F32), jax.ShapeDtypeStruct((s, 512), F32),
                   jax.ShapeDtypeStruct((s, D_MODEL), F32), jax.ShapeDtypeStruct((s, 512), F32)],
        scratch_shapes=[pltpu.VMEM((B_VAL_DIM, B_KEY_DIM), F32)],
        compiler_params=_params(("parallel", "arbitrary")),
    )(*ins)


def _r_spec(tile):
    return pl.BlockSpec((tile, D_MODEL), lambda i: (i, 2))


def gla_post_fwd(name, o_f, o_b, proj, gain):
    s = o_f.shape[0]
    tile = _row_tile(s)

    def body(of_ref, ob_ref, r_ref, g_ref, out_ref):
        out_ref[...] = _gla_post(of_ref[...], ob_ref[...], r_ref[...], g_ref[...]).astype(out_ref.dtype)

    row = pl.BlockSpec((tile, D_MODEL), lambda i: (i, 0))
    return pl.pallas_call(body, name=name, grid=(s // tile,),
                          in_specs=[row, row, _r_spec(tile), pl.BlockSpec(gain.shape, lambda i: (0, 0))], out_specs=row,
                          out_shape=jax.ShapeDtypeStruct((s, D_MODEL), BF16), compiler_params=_params(("parallel",)))(o_f, o_b, proj, gain)


def gla_post_bwd(name, o_f, o_b, proj, gain, d_out):
    s = o_f.shape[0]
    tile = _row_tile(s)

    def body(of_ref, ob_ref, r_ref, g_ref, d_ref, do_ref, dr_ref, dg_ref):
        _, vjp = jax.vjp(_gla_post, of_ref[...], ob_ref[...], r_ref[...], g_ref[...])
        d_of, _, dr, dg = vjp(d_ref[...])
        do_ref[...] = d_of
        dr_ref[...] = dr
        _acc_rows(dg_ref, dg)

    row = pl.BlockSpec((tile, D_MODEL), lambda i: (i, 0))
    return pl.pallas_call(
        body, name=name, grid=(s // tile,),
        in_specs=[row, row, _r_spec(tile), pl.BlockSpec(gain.shape, lambda i: (0, 0)), row],
        out_specs=[row, row, pl.BlockSpec((8, D_MODEL), lambda i: (0, 0))],
        out_shape=[jax.ShapeDtypeStruct((s, D_MODEL), F32), jax.ShapeDtypeStruct((s, D_MODEL), F32), jax.ShapeDtypeStruct((8, D_MODEL), F32)],
        compiler_params=_params(("arbitrary",)))(o_f, o_b, proj, gain, d_out)


def _hid(tile, where):
    return pl.BlockSpec((None, tile, FFN_BLK), where)


def _pair(tile, where):
    return pl.BlockSpec((2, None, tile, FFN_BLK), where)


def _w_gu_spec(layer, where_j):
    return pl.BlockSpec((2, None, None, D_MODEL, FFN_BLK), lambda *g: (0, where_j(*g), layer, 0, 0))


def ffn_fwd(name, h_mid, hn2, w_gu, w_down, layer, next_gain=None):
    s = hn2.shape[0]
    tm = _row_tile(s, 1024)
    nt = s // tm

    def gu_body(x_ref, w_ref, gu_ref, act_ref):
        x = x_ref[...]
        g = _dot(x, w_ref[0], NN)
        u = _dot(x, w_ref[1], NN)
        gu_ref[0] = g
        gu_ref[1] = u
        act_ref[...] = _swiglu_act(g, u).astype(act_ref.dtype)

    gu, act = pl.pallas_call(
        gu_body, name=name + "_gu", grid=(4, nt),
        in_specs=[pl.BlockSpec((tm, D_MODEL), lambda j, i: (i, 0)), _w_gu_spec(layer, lambda j, i: j)],
        out_specs=[_pair(tm, lambda j, i: (0, j, i, 0)), _hid(tm, lambda j, i: (j, i, 0))],
        out_shape=[jax.ShapeDtypeStruct((2, 4, s, FFN_BLK), F32), jax.ShapeDtypeStruct((4, s, FFN_BLK), BF16)],
        compiler_params=_params(("parallel", "parallel")))(hn2, w_gu)
    def down_body(*refs):
        act_ref, w_ref, res_ref = refs[:3]
        g_ref = refs[3] if next_gain is not None else None
        o_ref = refs[4] if next_gain is not None else refs[3]
        out = res_ref[...]
        for j in range(4):
            out = out + _dot(act_ref[j], w_ref[j], NN)
        o_ref[...] = out
        if next_gain is not None:
            refs[5][...] = _rms(out, g_ref[...]).astype(BF16)

    row = pl.BlockSpec((tm, D_MODEL), lambda i: (i, 0))
    in_specs = [pl.BlockSpec((4, tm, FFN_BLK), lambda i: (0, i, 0)), pl.BlockSpec((4, FFN_BLK, D_MODEL), lambda i: (0, 0, 0)), row]
    ins = [act, w_down, h_mid]
    out_specs, out_shape = row, jax.ShapeDtypeStruct((s, D_MODEL), F32)
    if next_gain is not None:
        in_specs.append(pl.BlockSpec(next_gain.shape, lambda i: (0, 0)))
        ins.append(next_gain)
        out_specs, out_shape = [row, row], [out_shape, jax.ShapeDtypeStruct((s, D_MODEL), BF16)]
    h_next = pl.pallas_call(down_body, name=name + "_down", grid=(nt,), in_specs=in_specs, out_specs=out_specs,
                            out_shape=out_shape, compiler_params=_params(("parallel",)))(*ins)
    return h_next, gu, act


def ffn_bwd(name, dh_next, hn2, gu, act, w_gu, w_down, layer):
    s = hn2.shape[0]
    tm = _row_tile(s, 1024)
    nt = s // tm
    tw = _row_tile(s, 2048)
    ntw = s // tw
    d_wd = _matmul(name + "_dwd", act, dh_next, dims=TN, grid=(4, ntw), red_axis=1,
                   a_spec=_hid(tw, lambda j, i: (j, i, 0)), b_spec=pl.BlockSpec((tw, D_MODEL), lambda j, i: (i, 0)),
                   o_spec=pl.BlockSpec((None, FFN_BLK, D_MODEL), lambda j, i: (j, 0, 0)),
                   out_shape=jax.ShapeDtypeStruct((4, FFN_BLK, D_MODEL), BF16), acc_shape=(FFN_BLK, D_MODEL))

    def dgu_body(dy_ref, wd_ref, gu_ref, dgu_ref):
        d_act = _dot(dy_ref[...], wd_ref[...], NT)
        g, u = gu_ref[0], gu_ref[1]
        sg = _sigmoid(g)
        silu = g * sg
        dgu_ref[0] = (d_act * u * (sg + silu * (1.0 - sg))).astype(dgu_ref.dtype)
        dgu_ref[1] = (d_act * silu).astype(dgu_ref.dtype)

    d_gu = pl.pallas_call(
        dgu_body, name=name + "_dgu", grid=(nt, 4),
        in_specs=[pl.BlockSpec((tm, D_MODEL), lambda i, j: (i, 0)), pl.BlockSpec((None, FFN_BLK, D_MODEL), lambda i, j: (j, 0, 0)),
                  _pair(tm, lambda i, j: (0, j, i, 0))],
        out_specs=_pair(tm, lambda i, j: (0, j, i, 0)), out_shape=jax.ShapeDtypeStruct((2, 4, s, FFN_BLK), BF16),
        compiler_params=_params(("parallel", "parallel")))(dh_next, w_down, gu)

    tx = _row_tile(s, 512)

    def dx_body(d_ref, w_ref, o_ref):
        out = None
        for t in range(2):
            for j in range(4):
                prod = _dot(d_ref[t, j], w_ref[t, j], NT)
                out = prod if out is None else out + prod
        o_ref[...] = out

    d_hn2 = pl.pallas_call(
        dx_body, name=name + "_dx", grid=(s // tx,),
        in_specs=[pl.BlockSpec((2, 4, tx, FFN_BLK), lambda i: (0, 0, i, 0)),
                  pl.BlockSpec((2, 4, None, D_MODEL, FFN_BLK), lambda i: (0, 0, layer, 0, 0))],
        out_specs=pl.BlockSpec((tx, D_MODEL), lambda i: (i, 0)), out_shape=jax.ShapeDtypeStruct((s, D_MODEL), F32),
        compiler_params=_params(("parallel",)))(d_gu, w_gu)

    def dw_body(x_ref, d_ref, o_ref, acc_ref):
        x = x_ref[...]
        k = pl.program_id(1)
        for t in range(2):
            prod = _dot(x, d_ref[t], TN)

            @pl.when(k == 0)
            def _():
                acc_ref[t] = prod

            @pl.when(k > 0)
            def _():
                acc_ref[t] += prod

        @pl.when(k == ntw - 1)
        def _():
            o_ref[...] = acc_ref[...].astype(o_ref.dtype)

    d_wgu = pl.pallas_call(
        dw_body, name=name + "_dwgu", grid=(4, ntw),
        in_specs=[pl.BlockSpec((tw, D_MODEL), lambda j, i: (i, 0)), _pair(tw, lambda j, i: (0, j, i, 0))],
        out_specs=pl.BlockSpec((2, None, D_MODEL, FFN_BLK), lambda j, i: (0, j, 0, 0)),
        out_shape=jax.ShapeDtypeStruct((2, 4, D_MODEL, FFN_BLK), BF16),
        scratch_shapes=[pltpu.VMEM((2, D_MODEL, FFN_BLK), F32)],
        compiler_params=_params(("parallel", "arbitrary")))(hn2, d_gu)
    return d_hn2, d_wgu, d_wd


def _my_place():
    return lax.axis_index("x"), lax.axis_index("y"), lax.axis_index("c")


def _flip(place, k):
    x, y, c = place
    return (1 - x if k & 4 else x, 1 - y if k & 2 else y, 1 - c if k & 1 else c)


def _index(place):
    return 4 * place[0] + 2 * place[1] + place[2]


def all_gather(arrs):
    n = len(arrs)

    def body(*refs):
        ins, outs = refs[:n], refs[n:2 * n]
        send_sems, recv_sems, local_sems = refs[2 * n:]
        me = _my_place()
        sibling = _flip(me, 1)
        chips = (4, 2, 6)

        def copy(a, k, block, to, src=None):
            dst = outs[a].at[_index(block)]
            return pltpu.make_async_remote_copy(src_ref=dst if src is None else src, dst_ref=dst, send_sem=send_sems.at[a, k],
                                                recv_sem=recv_sems.at[a, k], device_id=to, device_id_type=MESH)

        started = []
        for a in range(n):
            mine = pltpu.make_async_copy(ins[a], outs[a].at[_index(me)], local_sems.at[a])
            mine.start()
            started.append(mine)
        first = []
        for a in range(n):
            first.append(copy(a, 0, me, sibling, src=ins[a]))
            first += [copy(a, 1 + j, me, _flip(me, k), src=ins[a]) for j, k in enumerate(chips)]
        for cp in first:
            cp.start()
        passed = []
        for a in range(n):
            for j, k in enumerate(chips):
                copy(a, 1 + j, _flip(me, k), me).wait_recv()
                fwd = copy(a, 4 + j, _flip(me, k), sibling)
                fwd.start()
                passed.append(fwd)
        for a in range(n):
            copy(a, 0, sibling, me).wait_recv()
            for j, k in enumerate(chips):
                copy(a, 4 + j, _flip(sibling, k), me).wait_recv()
        for cp in first + passed:
            cp.wait_send()
        for cp in started:
            cp.wait()

    return pl.pallas_call(
        body, name="all_gather_weights", in_specs=[ANY] * n, out_specs=[ANY] * n,
        out_shape=[jax.ShapeDtypeStruct((N_DEV,) + a.shape, a.dtype) for a in arrs],
        scratch_shapes=[pltpu.SemaphoreType.DMA((n, 7)), pltpu.SemaphoreType.DMA((n, 7)), pltpu.SemaphoreType.DMA((n,))],
    )(*arrs)


def exchange_partials(arrs):
    n = len(arrs)

    def body(*refs):
        ins, outs = refs[:n], refs[n:2 * n]
        send_sems, recv_sems, local_sems = refs[2 * n:]
        me = _my_place()
        local = []
        for a in range(n):
            cp = pltpu.make_async_copy(ins[a].at[_index(me)], outs[a].at[_index(me)], local_sems.at[a])
            cp.start()
            local.append(cp)

        def copy(a, k, src_block, dst_block):
            return pltpu.make_async_remote_copy(src_ref=ins[a].at[_index(src_block)], dst_ref=outs[a].at[_index(dst_block)],
                                                send_sem=send_sems.at[a, k - 1], recv_sem=recv_sems.at[a, k - 1],
                                                device_id=_flip(me, k), device_id_type=MESH)

        sent = []
        for a in range(n):
            for k in range(1, N_DEV):
                cp = copy(a, k, _flip(me, k), me)
                cp.start()
                sent.append(cp)
        for a in range(n):
            for k in range(1, N_DEV):
                copy(a, k, me, _flip(me, k)).wait_recv()
        for cp in sent:
            cp.wait_send()
        for cp in local:
            cp.wait()

    return pl.pallas_call(
        body, name="exchange_weight_grads", in_specs=[ANY] * n, out_specs=[ANY] * n,
        out_shape=[jax.ShapeDtypeStruct(a.shape, a.dtype) for a in arrs],
        scratch_shapes=[pltpu.SemaphoreType.DMA((n, 7)), pltpu.SemaphoreType.DMA((n, 7)), pltpu.SemaphoreType.DMA((n,))],
    )(*arrs)


def adamw_shard(name, parts, w, m, v, layer, tile):
    rows, cols = parts.shape[1:]
    assert rows % tile == 0
    off = layer * (rows // tile)

    def body(p_ref, w_ref, m_ref, v_ref, g_ref, d_ref, nm_ref, nv_ref):
        g = p_ref[0].astype(F32)
        for src in range(1, N_DEV):
            g = g + p_ref[src].astype(F32)
        g_ref[...] = g
        d_ref[...], nm_ref[...], nv_ref[...] = _adamw(w_ref[...], g, m_ref[...], v_ref[...])

    src_row = pl.BlockSpec((tile, cols), lambda i: (off + i, 0))
    row = pl.BlockSpec((tile, cols), lambda i: (i, 0))
    shape = jax.ShapeDtypeStruct((rows, cols), F32)
    return pl.pallas_call(body, name=name, grid=(rows // tile,),
                          in_specs=[pl.BlockSpec((N_DEV, tile, cols), lambda i: (0, i, 0)), src_row, src_row, src_row],
                          out_specs=[row] * 4, out_shape=[shape] * 4, compiler_params=_params(("parallel",)))(parts, w, m, v)


def allreduce_adamw_replicated(partial, w, m, v, n_loss_rows):
    rows = partial.shape[0]

    def body(p_ref, w_ref, m_ref, v_ref, g_ref, d_ref, nm_ref, nv_ref, loss_ref, recv_ref, send_sems, recv_sems):
        me = _my_place()
        recv_ref[_index(me)] = p_ref[...]
        copies = []
        for k in range(1, N_DEV):
            peer = _flip(me, k)
            cp = pltpu.make_async_remote_copy(src_ref=p_ref, dst_ref=recv_ref.at[_index(me)], send_sem=send_sems.at[k - 1],
                                              recv_sem=recv_sems.at[k - 1], device_id=peer, device_id_type=MESH)
            cp.start()
            copies.append((cp, peer))
        for k, (cp, peer) in enumerate(copies):
            pltpu.make_async_remote_copy(src_ref=p_ref, dst_ref=recv_ref.at[_index(peer)], send_sem=send_sems.at[k],
                                         recv_sem=recv_sems.at[k], device_id=peer, device_id_type=MESH).wait_recv()
        for cp, _ in copies:
            cp.wait_send()
        g = recv_ref[0]
        for src in range(1, N_DEV):
            g = g + recv_ref[src]
        g_ref[...] = g
        d_ref[...], nm_ref[...], nv_ref[...] = _adamw(w_ref[...], g, m_ref[...], v_ref[...])
        loss = (0.5 / D_MODEL) * jnp.sum(g[rows - n_loss_rows:, :])
        loss_ref[...] = jnp.full(loss_ref.shape, loss, F32)

    shape = jax.ShapeDtypeStruct((rows, LANES), F32)
    return pl.pallas_call(
        body, name="allreduce_adamw_replicated", in_specs=[VMEM_SPEC] * 4, out_specs=[VMEM_SPEC] * 5,
        out_shape=[shape] * 4 + [jax.ShapeDtypeStruct((8, LANES), F32)],
        scratch_shapes=[pltpu.VMEM((N_DEV, rows, LANES), F32), pltpu.SemaphoreType.DMA((7,)), pltpu.SemaphoreType.DMA((7,))],
    )(partial, w, m, v)


def _pack_rows(flat, cols):
    n = flat.shape[-1]
    rows = -(-n // cols)
    rows = -(-rows // 48) * 48
    flat = jnp.pad(flat, [(0, 0)] * (flat.ndim - 1) + [(0, rows * cols - n)])
    return flat.reshape(flat.shape[:-1] + (rows, cols))


def _unpack(flat, shapes):
    out, off = [], 0
    for shp in shapes:
        n = 1
        for d in shp:
            n *= d
        out.append(flat[off:off + n].reshape(shp))
        off += n
    return out


def _to_dev_cols(a):
    w = a.shape[-1] // N_DEV
    return jnp.moveaxis(a.reshape(a.shape[:-1] + (N_DEV, w)), -2, 0)


def _from_dev_cols(a):
    a = jnp.moveaxis(a, 0, -2)
    return a.reshape(a.shape[:-2] + (a.shape[-2] * a.shape[-1],))


def kernel(x, attn_norm, ffn_norm, a_w_in, a_q_norm, a_k_norm, a_w_out, b_w_in, b_w_gate_f, b_gate_bias_f, b_w_gate_b, b_gate_bias_b, b_out_norm, b_w_out, ffn_w_gate_up, ffn_w_down, loss_target, m_attn_norm, m_ffn_norm, m_a_w_in, m_a_q_norm, m_a_k_norm, m_a_w_out, m_b_w_in, m_b_w_gate_f, m_b_gate_bias_f, m_b_w_gate_b, m_b_gate_bias_b, m_b_out_norm, m_b_w_out, m_ffn_w_gate_up, m_ffn_w_down, v_attn_norm, v_ffn_norm, v_a_w_in, v_a_q_norm, v_a_k_norm, v_a_w_out, v_b_w_in, v_b_w_gate_f, v_b_gate_bias_f, v_b_w_gate_b, v_b_gate_bias_b, v_b_out_norm, v_b_w_out, v_ffn_w_gate_up, v_ffn_w_down):
    seq = x.shape[1]
    depth = attn_norm.shape[0]
    h = x.reshape(seq, D_MODEL)
    target = loss_target.reshape(seq, D_MODEL)
    n_a, n_b = a_w_in.shape[0], b_w_in.shape[0]

    small = jnp.concatenate([t.reshape(-1) for t in (b_w_gate_f, b_gate_bias_f, b_w_gate_b, b_gate_bias_b, b_out_norm)])
    small = _pack_rows(small, LANES)
    g_a_in, g_a_out, g_b_in, g_b_out, g_gu, g_down, g_small = all_gather(
        [a_w_in.astype(BF16), a_w_out.astype(BF16), b_w_in.astype(BF16), b_w_out.astype(BF16),
         ffn_w_gate_up.astype(BF16), ffn_w_down.astype(BF16), small])
    w_a_in = _from_dev_cols(g_a_in)
    w_a_out = jnp.moveaxis(g_a_out, 0, 1).reshape(n_a, D_MODEL, D_MODEL)
    w_b_in = _from_dev_cols(g_b_in)
    w_b_out = jnp.moveaxis(g_b_out, 0, 1).reshape(n_b, D_MODEL, D_MODEL)
    w_down = jnp.moveaxis(g_down, 0, 1).reshape(depth, 4, FFN_BLK, D_MODEL)
    w_gu = g_gu.reshape(2, 4, depth, D_MODEL, FFN_BLK)
    small_shapes = [t.shape for t in (b_w_gate_f, b_gate_bias_f, b_w_gate_b, b_gate_bias_b, b_out_norm)]
    per_dev = [_unpack(g_small[d].reshape(-1), small_shapes) for d in range(N_DEV)]
    wgf, bgf, wgb, bgb, onorm = [_from_dev_cols(jnp.stack([per_dev[d][t] for d in range(N_DEV)])) for t in range(5)]
    w_gate = jnp.zeros((n_b, LANES, D_MODEL), F32)
    w_gate = w_gate.at[:, 0:16, 0:512].set(wgf).at[:, 16:32, 512:1024].set(wgb)
    gate_bias = jnp.concatenate([bgf, bgb], axis=1).reshape(n_b, 1, D_MODEL)
    out_gain = onorm.reshape(n_b, 1, D_MODEL)
    w_b_main = w_b_in[:, :, :3072]
    w_b_z = jnp.pad(w_b_in[:, :, 3072:], ((0, 0), (0, 0), (0, LANES - 32)))

    dh, sq_err, grads = _forward_backward(h, target, attn_norm, ffn_norm, a_q_norm, a_k_norm, w_a_in, w_a_out, w_b_main, w_b_z,
                                          w_gate, gate_bias, out_gain, w_b_out, w_gu, w_down)
    sharded_w = (a_w_in, a_w_out, b_w_in, b_w_gate_f, b_gate_bias_f, b_w_gate_b, b_gate_bias_b, b_out_norm, b_w_out, ffn_w_gate_up, ffn_w_down)
    sharded_m = (m_a_w_in, m_a_w_out, m_b_w_in, m_b_w_gate_f, m_b_gate_bias_f, m_b_w_gate_b, m_b_gate_bias_b, m_b_out_norm, m_b_w_out, m_ffn_w_gate_up, m_ffn_w_down)
    sharded_v = (v_a_w_in, v_a_w_out, v_b_w_in, v_b_w_gate_f, v_b_gate_bias_f, v_b_w_gate_b, v_b_gate_bias_b, v_b_out_norm, v_b_w_out, v_ffn_w_gate_up, v_ffn_w_down)
    rep_w = (attn_norm, ffn_norm, a_q_norm, a_k_norm)
    rep_m = (m_attn_norm, m_ffn_norm, m_a_q_norm, m_a_k_norm)
    rep_v = (v_attn_norm, v_ffn_norm, v_a_q_norm, v_a_k_norm)
    loss, outs = _reduce_and_update(grads, sq_err, sharded_w, sharded_m, sharded_v, rep_w, rep_m, rep_v)
    return (loss, dh.reshape(x.shape), *outs)


def _forward_backward(h, target, attn_norm, ffn_norm, a_q_norm, a_k_norm, w_a_in, w_a_out, w_b_main, w_b_z, w_gate, gate_bias,
                      out_gain, w_b_out, w_gu, w_down):
    seq = h.shape[0]
    depth = attn_norm.shape[0]
    n_a, n_b = w_a_in.shape[0], w_b_main.shape[0]
    tab = _rope_table(seq)
    pair_gain = lambda g: jnp.concatenate([g, g]).reshape(1, LANES)

    saved = []
    for i in range(depth):
        j = i // 2
        nm = f"l{i}"
        if i == 0:
            hn = rmsnorm_fwd(nm + "_norm1", h, attn_norm[i].reshape(1, D_MODEL))
        if i % 2 == 0:
            qkv = mm_nn(nm + "_qkv", hn, w_a_in[j])
            os_, lses = [], []
            for g, (_, dil) in enumerate(A_GROUPS):
                o, lse = attn_fwd(f"{nm}_attn{g}", qkv, tab, pair_gain(a_q_norm[j, g]), pair_gain(a_k_norm[j, g]), g, dil)
                os_.append(o)
                lses.append(lse)
            mixed = attn_merge_fwd(nm + "_merge", os_, lses)
            h_mid, hn2 = mm_nn(nm + "_out", mixed, w_a_out[j], res=h, norm_gain=ffn_norm[i].reshape(1, D_MODEL))
            mix_saved = (qkv, os_, lses, mixed)
        else:
            proj = mm_nn(nm + "_proj", hn, w_b_main[j])
            z = mm_nn(nm + "_z", hn, w_b_z[j])
            la = gla_gate_fwd(nm + "_gate", z, w_gate[j], gate_bias[j])
            o_f, st_f = gla_fwd(nm + "_gla_f", proj, la, False)
            o_b, st_b = gla_fwd(nm + "_gla_b", proj, la, True)
            mixed = gla_post_fwd(nm + "_post", o_f, o_b, proj, out_gain[j])
            h_mid, hn2 = mm_nn(nm + "_out", mixed, w_b_out[j], res=h, norm_gain=ffn_norm[i].reshape(1, D_MODEL))
            mix_saved = (proj, z, la, o_f, st_f, o_b, st_b, mixed)
        next_gain = attn_norm[i + 1].reshape(1, D_MODEL) if i + 1 < depth else None
        h_next, gu, act = ffn_fwd(nm + "_ffn", h_mid, hn2, w_gu, w_down[i], i, next_gain)
        saved.append((h, hn, mix_saved, h_mid, hn2, gu, act))
        if next_gain is not None:
            h_next, hn = h_next
        h = h_next

    dh, sq_err = loss_and_grad(h, target)

    g_attn_norm, g_ffn_norm = [None] * depth, [None] * depth
    g_a_w_in, g_a_w_out, g_a_q, g_a_k = [None] * n_a, [None] * n_a, [None] * n_a, [None] * n_a
    g_b_w_in, g_b_w_out, g_w_gate, g_gate_bias, g_out_gain = ([None] * n_b for _ in range(5))
    g_w_gu, g_w_down = [None] * depth, [None] * depth
    for i in reversed(range(depth)):
        j = i // 2
        nm = f"l{i}b"
        h_in, hn, mix_saved, h_mid, hn2, gu, act = saved[i]
        d_hn2, g_w_gu[i], g_w_down[i] = ffn_bwd(nm + "_ffn", dh, hn2, gu, act, w_gu, w_down[i], i)
        dh_mid, g_ffn_norm[i] = rmsnorm_bwd(nm + "_norm2", h_mid, ffn_norm[i].reshape(1, D_MODEL), d_hn2, dh)
        if i % 2 == 0:
            qkv, os_, lses, mixed = mix_saved
            d_mixed = mm_nt(nm + "_dmixed", dh_mid, w_a_out[j])
            g_a_w_out[j] = mm_tn(nm + "_dwout", mixed, dh_mid, by_block=True)
            d_parts = attn_merge_bwd(nm + "_merge", os_, lses, d_mixed)
            gq_l, gk_l = [], []
            d_qkv = lax.empty(qkv.shape, BF16)
            for g, (_, dil) in enumerate(A_GROUPS):
                d_qkv, dk_parts, dv_parts, dgq, dgk = attn_bwd(f"{nm}_attn{g}", qkv, tab, pair_gain(a_q_norm[j, g]),
                                                               pair_gain(a_k_norm[j, g]), os_[g], lses[g], d_parts[g], d_parts[3 + g], g, dil, d_qkv)
                d_qkv = attn_combine(f"{nm}_dk{g}", dk_parts, seq, dil, d_qkv, 3 * g + 1)
                d_qkv = attn_combine(f"{nm}_dv{g}", dv_parts, seq, dil, d_qkv, 3 * g + 2)
                gq_l.append(dgq[0, :A_HEAD_DIM])
                gk_l.append(dgk[0, :A_HEAD_DIM])
            g_a_q[j], g_a_k[j] = jnp.stack(gq_l), jnp.stack(gk_l)
            d_hn = mm_nt(nm + "_dhn", d_qkv, w_a_in[j])
            g_a_w_in[j] = mm_tn(nm + "_dwin", hn, d_qkv, tn=D_MODEL * 9 // N_DEV, by_block=True)
        else:
            proj, z, la, o_f, st_f, o_b, st_b, mixed = mix_saved
            d_mixed = mm_nt(nm + "_dmixed", dh_mid, w_b_out[j])
            g_b_w_out[j] = mm_tn(nm + "_dwout", mixed, dh_mid, by_block=True)
            d_o, d_r, dgain = gla_post_bwd(nm + "_post", o_f, o_b, proj, out_gain[j], d_mixed)
            g_out_gain[j] = dgain[0]
            dq, dk, dv, dla_f = gla_bwd(nm + "_gla_f", proj, la, st_f, d_o, False)
            dq, dk, dv, dla_b = gla_bwd(nm + "_gla_b", proj, la, st_b, d_o, True, prev=(dq, dk, dv))
            d_z, g_w_gate[j], dbias = gla_gate_bwd(nm + "_gate", z, w_gate[j], gate_bias[j], dla_f, dla_b)
            g_gate_bias[j] = dbias[0]
            d_proj = jnp.concatenate([dq, dk, dv, d_r], axis=1)
            d_hn = mm_nt(nm + "_dhn_z", d_z, w_b_z[j])
            d_hn = mm_nt(nm + "_dhn", d_proj, w_b_main[j], res=d_hn)
            g_b_w_in[j] = jnp.concatenate([mm_tn(nm + "_dwin", hn, d_proj), mm_tn(nm + "_dwz", hn, d_z)[:, :32]], axis=1)
        dh, g_attn_norm[i] = rmsnorm_bwd(nm + "_norm1", h_in, attn_norm[i].reshape(1, D_MODEL), d_hn, dh_mid)
    return dh, sq_err, (g_attn_norm, g_ffn_norm, g_a_w_in, g_a_w_out, g_a_q, g_a_k, g_b_w_in, g_b_w_out, g_w_gate, g_gate_bias,
                        g_out_gain, g_w_gu, g_w_down)


def _reduce_and_update(grads, sq_err, sharded_w, sharded_m, sharded_v, rep_w, rep_m, rep_v):
    (g_attn_norm, g_ffn_norm, g_a_w_in, g_a_w_out, g_a_q, g_a_k, g_b_w_in, g_b_w_out, g_w_gate, g_gate_bias, g_out_gain,
     g_w_gu, g_w_down) = grads
    depth, n_a, n_b = len(g_w_gu), len(g_a_w_in), len(g_b_w_in)

    g_w_gate = jnp.stack(g_w_gate)
    g_gate_bias = jnp.stack(g_gate_bias)
    small_parts = [_to_dev_cols(g_w_gate[:, 0:16, 0:512]), _to_dev_cols(g_gate_bias[:, 0:512]),
                   _to_dev_cols(g_w_gate[:, 16:32, 512:1024]), _to_dev_cols(g_gate_bias[:, 512:1024]),
                   _to_dev_cols(jnp.stack(g_out_gain).reshape(n_b, B_HEADS, B_VAL_DIM))]
    small_part = _pack_rows(jnp.concatenate([t.reshape(N_DEV, -1) for t in small_parts], axis=1), LANES).astype(BF16)
    families = [
        (0, g_a_w_in, 256),
        (1, [t.reshape(N_DEV, -1, D_MODEL) for t in g_a_w_out], 128),
        (2, [_to_dev_cols(t).astype(BF16) for t in g_b_w_in], 256),
        (8, [t.reshape(N_DEV, -1, D_MODEL) for t in g_b_w_out], 128),
        (9, [t.reshape(N_DEV, D_MODEL, FFN_BLK) for t in g_w_gu], 256),
        (10, [t.reshape(N_DEV, -1, D_MODEL) for t in g_w_down], 176),
    ]
    flat_parts = [p for _, parts, _ in families for p in parts] + [small_part]
    received = exchange_partials(flat_parts)
    sh_out = [None] * len(sharded_w)
    pos = 0
    for fam, parts, tile in families:
        w = sharded_w[fam]
        two_d = lambda t: t.reshape(-1, t.shape[-1])
        per_layer = []
        for layer in range(len(parts)):
            per_layer.append(adamw_shard(f"adamw_p{fam}_l{layer}", received[pos], two_d(w), two_d(sharded_m[fam]),
                                         two_d(sharded_v[fam]), layer, tile))
            pos += 1
        sh_out[fam] = [jnp.stack([per_layer[l][t] for l in range(len(parts))]).reshape(w.shape) for t in range(4)]
    small_ids = (3, 4, 5, 6, 7)
    pack_small = lambda ts: _pack_rows(jnp.concatenate([ts[i].reshape(-1) for i in small_ids]), LANES)
    small_out = adamw_shard("adamw_small", received[pos], pack_small(sharded_w), pack_small(sharded_m), pack_small(sharded_v), 0, 48)
    small_shapes = [sharded_w[i].shape for i in small_ids]
    for t in range(4):
        for i, val in zip(small_ids, _unpack(small_out[t].reshape(-1), small_shapes)):
            if sh_out[i] is None:
                sh_out[i] = [None] * 4
            sh_out[i][t] = val
    sh_grad, sh_delta, sh_m, sh_v = [[sh_out[i][t] for i in range(len(sharded_w))] for t in range(4)]

    rep_g = (jnp.stack([t[0] for t in g_attn_norm]), jnp.stack([t[0] for t in g_ffn_norm]), jnp.stack(g_a_q), jnp.stack(g_a_k))
    n_rep = sum(t.size for t in rep_w)
    n_rep_rows = -(-n_rep // (8 * LANES)) * 8
    n_loss_rows = 8 * D_MODEL // LANES

    def pack_rep(ts, tail):
        flat = jnp.concatenate([t.reshape(-1) for t in ts])
        flat = jnp.pad(flat, (0, n_rep_rows * LANES - n_rep))
        return jnp.concatenate([flat.reshape(n_rep_rows, LANES), tail], axis=0)

    zeros_tail = jnp.zeros((n_loss_rows, LANES), F32)
    rep_out = allreduce_adamw_replicated(pack_rep(rep_g, sq_err.reshape(n_loss_rows, LANES)), pack_rep(rep_w, zeros_tail),
                                         pack_rep(rep_m, zeros_tail), pack_rep(rep_v, zeros_tail + 1.0), n_loss_rows)
    rep_shapes = [t.shape for t in rep_w]
    r_grad, r_delta, r_m, r_v = [_unpack(p.reshape(-1), rep_shapes) for p in rep_out[:4]]
    loss = rep_out[4][0, 0]

    def ordered(rep, sh):
        return [rep[0], rep[1], sh[0], rep[2], rep[3]] + list(sh[1:])

    return loss, (*ordered(r_grad, sh_grad), *ordered(r_delta, sh_delta), *ordered(r_m, sh_m), *ordered(r_v, sh_v))
```

```python
import functools

import jax
import jax.numpy as jnp
from jax import lax
from jax.experimental import pallas as pl
from jax.experimental.pallas import tpu as pltpu

F32 = jnp.float32
BF16 = jnp.bfloat16
MXU_DTYPE = jnp.bfloat16

D_MODEL = 1024
N_DEV = 8
RMS_EPS = 1e-6
NEG_INF = -1e30
A_GROUPS = ((128, 1), (512, 4), (2048, 16))
A_HEAD_DIM = 64
A_HALF = 64
ATT_T = 128
ATT_TILE = 2048
ATT_FWD_BATCH = 8
ATT_BWD_BATCH = 4
B_HEADS = 4
B_KEY_DIM = 128
B_VAL_DIM = 256
B_CHUNK = 64
B_GATE_TAU = 16.0
GLA_TILE = 2048
FFN_HIDDEN = 2816
FFN_BLK = 2 * FFN_HIDDEN // N_DEV
ADAM_LR, ADAM_B1, ADAM_B2, ADAM_EPS, ADAM_WD, ADAM_STEP = 0.001, 0.9, 0.999, 1e-08, 0.01, 10
ROPE_THETA = 10000.0

V7X_VMEM_LIMIT = 60 * 1024 * 1024
LANES = 128
MESH = pl.DeviceIdType.MESH
ANY = pl.BlockSpec(memory_space=pl.ANY)
VMEM_SPEC = pl.BlockSpec(memory_space=pltpu.VMEM)

NN = ((1,), (0,))
NT = ((1,), (1,))
TN = ((0,), (0,))


def _dot(a, b, dims):
    return lax.dot_general(a.astype(MXU_DTYPE), b.astype(MXU_DTYPE), (dims, ((), ())), preferred_element_type=F32)


@jax.custom_vjp
def dot_nn(a, b):
    return _dot(a, b, NN)


@jax.custom_vjp
def dot_nt(a, b):
    return _dot(a, b, NT)


@jax.custom_vjp
def dot_tn(a, b):
    return _dot(a, b, TN)


dot_nn.defvjp(lambda a, b: (_dot(a, b, NN), (a, b)), lambda r, g: (dot_nt(g, r[1]), dot_tn(r[0], g)))
dot_nt.defvjp(lambda a, b: (_dot(a, b, NT), (a, b)), lambda r, g: (dot_nn(g, r[1]), dot_tn(g, r[0])))
dot_tn.defvjp(lambda a, b: (_dot(a, b, TN), (a, b)), lambda r, g: (dot_nt(r[1], g), dot_nn(r[0], g)))


def _bdot(a, b, dims):
    dn = (tuple((d[0] + 1,) for d in dims), ((0,), (0,)))
    return lax.dot_general(a.astype(MXU_DTYPE), b.astype(MXU_DTYPE), dn, preferred_element_type=F32)


@jax.custom_vjp
def bdot_nn(a, b):
    return _bdot(a, b, NN)


@jax.custom_vjp
def bdot_nt(a, b):
    return _bdot(a, b, NT)


@jax.custom_vjp
def bdot_tn(a, b):
    return _bdot(a, b, TN)


bdot_nn.defvjp(lambda a, b: (_bdot(a, b, NN), (a, b)), lambda r, g: (bdot_nt(g, r[1]), bdot_tn(r[0], g)))
bdot_nt.defvjp(lambda a, b: (_bdot(a, b, NT), (a, b)), lambda r, g: (bdot_nn(g, r[1]), bdot_tn(g, r[0])))
bdot_tn.defvjp(lambda a, b: (_bdot(a, b, TN), (a, b)), lambda r, g: (bdot_nt(r[1], g), bdot_nn(r[0], g)))


def _dot_f32(a, b):
    return lax.dot_general(a, b, (NN, ((), ())), precision=lax.Precision.HIGHEST, preferred_element_type=F32)


def _tri(n, upper):
    r = lax.broadcasted_iota(jnp.int32, (n, n), 0)
    c = lax.broadcasted_iota(jnp.int32, (n, n), 1)
    return jnp.where((c >= r) if upper else (c <= r), 1.0, 0.0).astype(F32)


def _chunk_cumsum(x, reverse):
    tri = jnp.broadcast_to(_tri(x.shape[1], reverse), (x.shape[0], x.shape[1], x.shape[1]))
    return lax.dot_general(tri, x, (((2,), (1,)), ((0,), (0,))), precision=lax.Precision.HIGHEST, preferred_element_type=F32)


@functools.partial(jax.custom_vjp, nondiff_argnums=(1,))
def cumsum_chunks(x, reverse):
    return _chunk_cumsum(x, reverse)


cumsum_chunks.defvjp(lambda x, reverse: (_chunk_cumsum(x, reverse), None), lambda reverse, _, g: (_chunk_cumsum(g, not reverse),))


def _swap32_raw(x):
    lane = lax.broadcasted_iota(jnp.int32, x.shape, 1)
    return jnp.where((lane % 64) < 32, pltpu.roll(x, 96, 1), pltpu.roll(x, 32, 1))


@jax.custom_vjp
def swap32(x):
    return _swap32_raw(x)


swap32.defvjp(lambda x: (_swap32_raw(x), None), lambda _, g: (_swap32_raw(g),))


def _rms(x, gain):
    return x * lax.rsqrt(jnp.mean(x * x, axis=-1, keepdims=True) + RMS_EPS) * gain


def _sigmoid(x):
    return 1.0 / (1.0 + jnp.exp(-x))


def _log_sigmoid(x):
    return jnp.minimum(x, 0.0) - jnp.log(1.0 + jnp.exp(-jnp.abs(x)))


def _qk_prep(x, tab, gain):
    lo = lax.broadcasted_iota(jnp.int32, (1, LANES), 1) < A_HEAD_DIM
    x2 = x * x
    s_lo = jnp.sum(jnp.where(lo, x2, 0.0), axis=-1, keepdims=True)
    s_hi = jnp.sum(jnp.where(lo, 0.0, x2), axis=-1, keepdims=True)
    xn = (x * lax.rsqrt(jnp.where(lo, s_lo, s_hi) / A_HEAD_DIM + RMS_EPS)) * gain
    return xn * tab[:, :LANES] + swap32(xn) * tab[:, LANES:]


def _stack_heads(x):
    lo = lax.broadcasted_iota(jnp.int32, (1, LANES), 1) < A_HEAD_DIM
    return jnp.concatenate([jnp.where(lo, x, 0.0), jnp.where(lo, 0.0, x)], axis=-2)


def _unstack_heads(x):
    t = x.shape[-2] // 2
    lo = lax.broadcasted_iota(jnp.int32, (1, LANES), 1) < A_HEAD_DIM
    return jnp.where(lo, x[..., :t, :], x[..., t:, :])


def _attn_scores(q, k, valid):
    s = _bdot(_stack_heads(q), k, NT) * (A_HEAD_DIM ** -0.5)
    return jnp.where(valid, s, NEG_INF)


def _attn_job(q, k, v, valid):
    s = _attn_scores(q, k, valid)
    mx = jnp.max(s, axis=-1, keepdims=True)
    p = jnp.exp(s - mx)
    l = jnp.sum(p, axis=-1, keepdims=True)
    out = _unstack_heads(_bdot(p, v, NN) / l)
    lse = mx + jnp.log(l)
    t = q.shape[1]
    lo = lax.broadcasted_iota(jnp.int32, (1, LANES), 1) < A_HEAD_DIM
    return out, jnp.where(lo, lse[:, :t], lse[:, t:])


def _attn_job_bwd(q, k, v, valid, d_out, out, lse, d_lse):
    qs = _stack_heads(q)
    lo = lax.broadcasted_iota(jnp.int32, (1, LANES), 1) < A_HEAD_DIM
    other = pltpu.roll(lse, A_HEAD_DIM, 2)
    row_lse = jnp.concatenate([jnp.where(lo, lse, other), jnp.where(lo, other, lse)], axis=1)
    p = jnp.exp(_attn_scores(q, k, valid) - jnp.concatenate([row_lse, row_lse], axis=2))
    dos = _stack_heads(d_out)
    dv = _bdot(p, dos, TN)
    dp = _bdot(dos, v, NT)
    inner = jnp.sum(dos * _stack_heads(out) - _stack_heads(d_lse), axis=-1, keepdims=True)
    ds = p * (dp - inner) * (A_HEAD_DIM ** -0.5)
    return _unstack_heads(_bdot(ds, k, NN)), _bdot(ds, qs, TN), dv


def _merge_groups(o0, o1, o2, l0, l1, l2):
    mx = lax.stop_gradient(jnp.maximum(jnp.maximum(l0, l1), l2))
    e0, e1, e2 = jnp.exp(l0 - mx), jnp.exp(l1 - mx), jnp.exp(l2 - mx)
    den = e0 + e1 + e2
    return (e0 / den) * o0 + (e1 / den) * o1 + (e2 / den) * o2


def _gla_tile(q, k, v, la, st, reverse):
    t = q.shape[0]
    nc = t // B_CHUNK
    split = lambda x: x.reshape(nc, B_CHUNK, x.shape[1])
    q, k, v, la = split(q * (B_KEY_DIM ** -0.5)), split(k), split(v), split(la)
    r = lax.broadcasted_iota(jnp.int32, (1, B_CHUNK, B_CHUNK), 1)
    c = lax.broadcasted_iota(jnp.int32, (1, B_CHUNK, B_CHUNK), 2)
    mask = (c > r) if reverse else (c <= r)
    b = cumsum_chunks(la, reverse)
    tot = jnp.sum(la, axis=1, keepdims=True)
    q_t = q * jnp.exp(b)
    k_t = k * jnp.exp(-b)
    k_end = k * jnp.exp(tot - b)
    attn = jnp.where(mask, bdot_nt(q_t, k_t), 0.0)
    kv = bdot_tn(v, k_end)
    decay = jnp.exp(tot)
    entering = [None] * nc
    for ci in (range(nc - 1, -1, -1) if reverse else range(nc)):
        entering[ci] = st
        st = st * decay[ci] + kv[ci]
    out = bdot_nn(attn, v) + bdot_nt(q_t, jnp.concatenate([e[None] for e in entering], axis=0))
    return out.reshape(t, out.shape[2]), st


def _gla_post(o_f, o_b, r, gain):
    o = o_f + o_b
    heads = [_rms(o[:, h * B_VAL_DIM:(h + 1) * B_VAL_DIM], gain[:, h * B_VAL_DIM:(h + 1) * B_VAL_DIM]) for h in range(B_HEADS)]
    return jnp.concatenate(heads, axis=1) * (r * _sigmoid(r))


def _gate(z, wg, bias):
    return _log_sigmoid(dot_nn(z, wg) + bias) / B_GATE_TAU


def _swiglu_act(g, u):
    return (g * _sigmoid(g)) * u


def _adamw(w, g, m, v):
    m = ADAM_B1 * m + (1.0 - ADAM_B1) * g
    v = ADAM_B2 * v + (1.0 - ADAM_B2) * jnp.square(g)
    m_hat = m / (1.0 - ADAM_B1 ** ADAM_STEP)
    v_hat = v / (1.0 - ADAM_B2 ** ADAM_STEP)
    delta = -ADAM_LR * (m_hat / (jnp.sqrt(v_hat) + ADAM_EPS) + ADAM_WD * w)
    return delta, m, v


def _params(sem=None):
    return pltpu.CompilerParams(dimension_semantics=sem, vmem_limit_bytes=V7X_VMEM_LIMIT)


def _row_tile(s, want=512):
    t = min(want, s)
    assert s % t == 0
    return t


def _matmul(name, a, b, *, dims, grid, a_spec, b_spec, o_spec, out_shape, red_axis=None, res=None, res_spec=None, acc_shape=None,
            norm_gain=None):
    n_red = grid[red_axis] if red_axis is not None else 1
    n_in = 2 + (res is not None) + (norm_gain is not None)

    def body(*refs):
        a_ref, b_ref = refs[0], refs[1]
        r_ref = refs[2] if res is not None else None
        g_ref = refs[n_in - 1] if norm_gain is not None else None
        o_ref = refs[n_in]
        n_ref = refs[n_in + 1] if norm_gain is not None else None
        prod = lax.dot_general(a_ref[...].astype(MXU_DTYPE), b_ref[...].astype(MXU_DTYPE), (dims, ((), ())),
                               preferred_element_type=F32)
        if red_axis is None:
            if r_ref is not None:
                prod = prod + r_ref[...]
            o_ref[...] = prod.astype(o_ref.dtype)
            if n_ref is not None:
                n_ref[...] = _rms(prod, g_ref[...]).astype(n_ref.dtype)
            return
        acc = refs[-1] if acc_shape is not None else o_ref
        k = pl.program_id(red_axis)

        @pl.when(k == 0)
        def _():
            acc[...] = prod + r_ref[...] if r_ref is not None else prod

        @pl.when(k > 0)
        def _():
            acc[...] += prod

        if acc_shape is not None or n_ref is not None:
            @pl.when(k == n_red - 1)
            def _():
                if acc_shape is not None:
                    o_ref[...] = acc[...].astype(o_ref.dtype)
                if n_ref is not None:
                    n_ref[...] = _rms(acc[...], g_ref[...]).astype(n_ref.dtype)

    ins = [a, b] + ([res] if res is not None else []) + ([norm_gain] if norm_gain is not None else [])
    specs = [a_spec, b_spec] + ([res_spec] if res is not None else [])
    out_specs, out_shapes = o_spec, out_shape
    if norm_gain is not None:
        specs.append(pl.BlockSpec(norm_gain.shape, lambda *g: (0, 0)))
        out_specs, out_shapes = [o_spec, o_spec], [out_shape, jax.ShapeDtypeStruct(out_shape.shape, BF16)]
    sem = tuple("arbitrary" if i == red_axis else "parallel" for i in range(len(grid)))
    return pl.pallas_call(body, name=name, grid=grid, in_specs=specs, out_specs=out_specs, out_shape=out_shapes,
                          scratch_shapes=[pltpu.VMEM(acc_shape, F32)] if acc_shape is not None else [],
                          compiler_params=_params(sem))(*ins)


def mm_nn(name, x, w, *, res=None, out_dtype=F32, tn=1024, norm_gain=None):
    m, k = x.shape
    n = w.shape[1]
    tm, tn = _row_tile(m, 1024), min(tn, n)
    return _matmul(name, x, w, dims=NN, grid=(n // tn, m // tm),
                   a_spec=pl.BlockSpec((tm, k), lambda j, i: (i, 0)), b_spec=pl.BlockSpec((k, tn), lambda j, i: (0, j)),
                   o_spec=pl.BlockSpec((tm, tn), lambda j, i: (i, j)), out_shape=jax.ShapeDtypeStruct((m, n), out_dtype),
                   res=res, res_spec=pl.BlockSpec((tm, tn), lambda j, i: (i, j)), norm_gain=norm_gain)


def mm_nt(name, dy, w, *, res=None, tn=2304):
    m, n = dy.shape
    k = w.shape[0]
    tm, tn = _row_tile(m, 1024), (tn if n % tn == 0 else min(1024, n))
    return _matmul(name, dy, w, dims=NT, grid=(m // tm, n // tn), red_axis=1,
                   a_spec=pl.BlockSpec((tm, tn), lambda i, j: (i, j)), b_spec=pl.BlockSpec((k, tn), lambda i, j: (0, j)),
                   o_spec=pl.BlockSpec((tm, k), lambda i, j: (i, 0)), out_shape=jax.ShapeDtypeStruct((m, k), F32),
                   res=res, res_spec=pl.BlockSpec((tm, k), lambda i, j: (i, 0)))


def mm_tn(name, x, dy, *, tn=1024, by_block=False):
    m, k = x.shape
    n = dy.shape[1]
    tm, tn = _row_tile(m, 2048), min(tn, n)
    if by_block:
        o_spec, out_shape, acc = pl.BlockSpec((None, k, tn), lambda j, i: (j, 0, 0)), jax.ShapeDtypeStruct((n // tn, k, tn), BF16), (k, tn)
    else:
        o_spec, out_shape, acc = pl.BlockSpec((k, tn), lambda j, i: (0, j)), jax.ShapeDtypeStruct((k, n), F32), None
    return _matmul(name, x, dy, dims=TN, grid=(n // tn, m // tm), red_axis=1,
                   a_spec=pl.BlockSpec((tm, k), lambda j, i: (i, 0)), b_spec=pl.BlockSpec((tm, tn), lambda j, i: (i, j)),
                   o_spec=o_spec, out_shape=out_shape, acc_shape=acc)


def _rows_call(name, body, ins, outs, s, tile):
    in_specs = []
    for a, kind in ins:
        if kind == "row":
            in_specs.append(pl.BlockSpec((tile, a.shape[1]), lambda i: (i, 0)))
        else:
            in_specs.append(pl.BlockSpec(a.shape, lambda i, nd=a.ndim: (0,) * nd))
    out_specs, out_shape = [], []
    for cols, dt, kind in outs:
        if kind == "row":
            out_specs.append(pl.BlockSpec((tile, cols), lambda i: (i, 0)))
            out_shape.append(jax.ShapeDtypeStruct((s, cols), dt))
        else:
            out_specs.append(pl.BlockSpec((8, cols), lambda i: (0, 0)))
            out_shape.append(jax.ShapeDtypeStruct((8, cols), dt))
    has_acc = any(kind == "acc" for _, _, kind in outs)
    return pl.pallas_call(body, name=name, grid=(s // tile,), in_specs=in_specs, out_specs=out_specs, out_shape=out_shape,
                          compiler_params=_params(("arbitrary",) if has_acc else ("parallel",)))(*[a for a, _ in ins])


def _acc_rows(ref, val):
    @pl.when(pl.program_id(0) == 0)
    def _():
        ref[...] = jnp.zeros_like(ref)

    ref[...] += jnp.broadcast_to(val, ref.shape)


def rmsnorm_fwd(name, h, gain):
    s = h.shape[0]

    def body(h_ref, g_ref, o_ref):
        o_ref[...] = _rms(h_ref[...], g_ref[...]).astype(o_ref.dtype)

    return _rows_call(name, body, [(h, "row"), (gain, "full")], [(D_MODEL, BF16, "row")], s, _row_tile(s))[0]


def rmsnorm_bwd(name, h, gain, d_hn, d_res):
    s = h.shape[0]

    def body(h_ref, g_ref, dy_ref, dr_ref, dh_ref, dg_ref):
        _, vjp = jax.vjp(_rms, h_ref[...], g_ref[...])
        dh, dg = vjp(dy_ref[...])
        dh_ref[...] = dh + dr_ref[...]
        _acc_rows(dg_ref, dg)

    return _rows_call(name, body, [(h, "row"), (gain, "full"), (d_hn, "row"), (d_res, "row")],
                      [(D_MODEL, F32, "row"), (D_MODEL, F32, "acc")], s, _row_tile(s, 1024))


def loss_and_grad(y, target):
    s = y.shape[0]
    tile = _row_tile(s)

    def body(y_ref, t_ref, dy_ref, acc_ref):
        diff = y_ref[...] - t_ref[...]
        dy_ref[...] = diff * (1.0 / D_MODEL)

        @pl.when(pl.program_id(0) == 0)
        def _():
            acc_ref[...] = jnp.zeros_like(acc_ref)

        acc_ref[...] += jnp.sum((diff * diff).reshape(tile // 8, 8, D_MODEL), axis=0)

    return _rows_call("loss_head", body, [(y, "row"), (target, "row")], [(D_MODEL, F32, "row"), (D_MODEL, F32, "acc")], s, tile)


def _rope_table(s):
    half = A_HEAD_DIM // 2
    inv_freq = ROPE_THETA ** (-jnp.arange(half, dtype=F32) / half)
    ang = jnp.arange(s).astype(F32)[:, None] * inv_freq[None, :]
    cos, sin = jnp.cos(ang), jnp.sin(ang)
    return jnp.concatenate([cos, cos, cos, cos, -sin, sin, -sin, sin], axis=1)


def _attn_geometry(s, dil):
    tile = min(ATT_TILE, s)
    halo = A_HALF * dil
    assert s % tile == 0 and tile % (ATT_T * dil) == 0 and tile % halo == 0
    return tile, halo, tile // (ATT_T * dil)


def _attn_in_specs(grp, s, tile, halo):
    hb, n_hb = tile // halo, s // halo
    cq, ck, cv = (24 * grp + 8 * t for t in range(3))

    def main(col, per_pair, width=LANES):
        return pl.BlockSpec((tile, width), lambda i, j: (i, col + per_pair * j))

    def prev(col, per_pair, width=LANES):
        return pl.BlockSpec((halo, width), lambda i, j: (jnp.maximum(i * hb - 1, 0), col + per_pair * j))

    def nxt(col, per_pair, width=LANES):
        return pl.BlockSpec((halo, width), lambda i, j: (jnp.minimum((i + 1) * hb, n_hb - 1), col + per_pair * j))

    return [main(cq, 1), prev(ck, 1), main(ck, 1), nxt(ck, 1), prev(cv, 1), main(cv, 1), nxt(cv, 1),
            prev(0, 0, 256), main(0, 0, 256), nxt(0, 0, 256)]


def _attn_valid(first_job, n_jobs, dil, tile_base, length):
    r = lax.broadcasted_iota(jnp.int32, (1, 2 * ATT_T, 1), 1)
    tq = jnp.where(r >= ATT_T, r - ATT_T, r)
    rel = lax.broadcasted_iota(jnp.int32, (1, 1, ATT_T + 2 * A_HALF), 2) - A_HALF
    job = first_job + lax.broadcasted_iota(jnp.int32, (n_jobs, 1, 1), 0)
    tk = tile_base + (job // dil) * ATT_T + rel
    return (jnp.abs(rel - tq) <= A_HALF) & (tk >= 0) & (tk < length)


def _jobs(dil, n_sub):
    return [u * ATT_T * dil + p for u in range(n_sub) for p in range(dil)]


def _gather_jobs(ref, starts, size, dil):
    return jnp.concatenate([ref[_rows(st, size, dil), :][None] for st in starts], axis=0)


def _rows(start, size, dil):
    return pl.ds(start, size, stride=dil) if dil > 1 else pl.ds(start, size)


def attn_fwd(name, qkv, tab, gq, gk, grp, dil):
    s = qkv.shape[0]
    tile, halo, n_sub = _attn_geometry(s, dil)
    length, per_tile = s // dil, tile // dil
    nk = ATT_T + 2 * A_HALF

    def body(q_ref, kp_ref, km_ref, kn_ref, vp_ref, vm_ref, vn_ref, tp_ref, tm_ref, tn_ref, gq_ref, gk_ref, o_ref, lse_ref,
             q_buf, k_buf, v_buf):
        i, pair = pl.program_id(0), pl.program_id(1)
        q_buf[...] = _qk_prep(q_ref[...], tm_ref[...], gq_ref[...])
        for ref, t_ref, lo, n in ((kp_ref, tp_ref, 0, halo), (km_ref, tm_ref, halo, tile), (kn_ref, tn_ref, halo + tile, halo)):
            k_buf[lo:lo + n, :] = _qk_prep(ref[...], t_ref[...], gk_ref[...])
        for ref, lo, n in ((vp_ref, 0, halo), (vm_ref, halo, tile), (vn_ref, halo + tile, halo)):
            v_buf[lo:lo + n, :] = ref[...]

        jobs = _jobs(dil, n_sub)
        for g0 in range(0, len(jobs), ATT_FWD_BATCH):
            starts = jobs[g0:g0 + ATT_FWD_BATCH]
            valid = _attn_valid(g0, len(starts), dil, i * per_tile, length)
            o, lse = _attn_job(_gather_jobs(q_buf, starts, ATT_T, dil), _gather_jobs(k_buf, starts, nk, dil),
                               _gather_jobs(v_buf, starts, nk, dil), valid)
            for n, st in enumerate(starts):
                o_ref[_rows(st, ATT_T, dil), :] = o[n]
                lse_ref[_rows(st, ATT_T, dil), :] = lse[n]

    full = lambda a: pl.BlockSpec(a.shape, lambda i, j: (0, 0))
    return pl.pallas_call(
        body, name=name, grid=(s // tile, D_MODEL // LANES),
        in_specs=_attn_in_specs(grp, s, tile, halo) + [full(gq), full(gk)],
        out_specs=[pl.BlockSpec((tile, LANES), lambda i, j: (i, j)), pl.BlockSpec((tile, LANES), lambda i, j: (i, j))],
        out_shape=[jax.ShapeDtypeStruct((s, D_MODEL), F32), jax.ShapeDtypeStruct((s, D_MODEL), F32)],
        scratch_shapes=[pltpu.VMEM((tile, LANES), F32), pltpu.VMEM((tile + 2 * halo, LANES), F32), pltpu.VMEM((tile + 2 * halo, LANES), F32)],
        compiler_params=_params(("parallel", "parallel")),
    )(qkv, qkv, qkv, qkv, qkv, qkv, qkv, tab, tab, tab, gq, gk)


def attn_bwd(name, qkv, tab, gq, gk, o, lse, d_o, d_lse, grp, dil, d_qkv):
    s = qkv.shape[0]
    tile, halo, n_sub = _attn_geometry(s, dil)
    length, per_tile = s // dil, tile // dil
    nt = s // tile
    nk = ATT_T + 2 * A_HALF
    pieces = ((0, halo), (halo, tile), (halo + tile, halo))

    def body(q_ref, kp_ref, km_ref, kn_ref, vp_ref, vm_ref, vn_ref, tp_ref, tm_ref, tn_ref, gq_ref, gk_ref, o_ref, l_ref, do_ref, dl_ref,
             _, dq_ref, dkp_ref, dkm_ref, dkn_ref, dvp_ref, dvm_ref, dvn_ref, dgq_ref, dgk_ref, q_buf, k_buf, v_buf, dq_buf, dk_buf, dv_buf):
        i, pair = pl.program_id(0), pl.program_id(1)
        k_refs, t_refs = (kp_ref, km_ref, kn_ref), (tp_ref, tm_ref, tn_ref)
        qn, q_vjp = jax.vjp(lambda x, g: _qk_prep(x, tm_ref[...], g), q_ref[...], gq_ref[...])
        q_buf[...] = qn
        k_vjps = []
        for ref, t_ref, (lo, n) in zip(k_refs, t_refs, pieces):
            kn, k_vjp = jax.vjp(lambda x, g: _qk_prep(x, t_ref[...], g), ref[...], gk_ref[...])
            k_buf[lo:lo + n, :] = kn
            k_vjps.append(k_vjp)
        for ref, (lo, n) in zip((vp_ref, vm_ref, vn_ref), pieces):
            v_buf[lo:lo + n, :] = ref[...]
        dk_buf[...] = jnp.zeros_like(dk_buf)
        dv_buf[...] = jnp.zeros_like(dv_buf)
        jobs = _jobs(dil, n_sub)
        for g0 in range(0, len(jobs), ATT_BWD_BATCH):
            starts = jobs[g0:g0 + ATT_BWD_BATCH]
            valid = _attn_valid(g0, len(starts), dil, i * per_tile, length)
            own = lambda ref: _gather_jobs(ref, starts, ATT_T, dil)
            dq, dk, dv = _attn_job_bwd(own(q_buf), _gather_jobs(k_buf, starts, nk, dil), _gather_jobs(v_buf, starts, nk, dil), valid,
                                       own(do_ref), own(o_ref), own(l_ref), own(dl_ref))
            for n, st in enumerate(starts):
                dq_buf[_rows(st, ATT_T, dil), :] = dq[n]
                dk_buf[_rows(st, nk, dil), :] += dk[n]
                dv_buf[_rows(st, nk, dil), :] += dv[n]
        dq, dgq = q_vjp(dq_buf[...])
        dq_ref[...] = dq.astype(dq_ref.dtype)
        dgk = jnp.zeros((1, LANES), F32)
        for k_vjp, out_ref, (lo, n) in zip(k_vjps, (dkp_ref, dkm_ref, dkn_ref), pieces):
            out_ref[...], dgk_piece = k_vjp(dk_buf[lo:lo + n, :])
            dgk = dgk + dgk_piece
        for out_ref, (lo, n) in zip((dvp_ref, dvm_ref, dvn_ref), pieces):
            out_ref[...] = dv_buf[lo:lo + n, :]

        @pl.when((i == 0) & (pair == 0))
        def _():
            dgq_ref[...] = jnp.zeros_like(dgq_ref)
            dgk_ref[...] = jnp.zeros_like(dgk_ref)

        dgq_ref[...] += jnp.broadcast_to(dgq + pltpu.roll(dgq, A_HEAD_DIM, 1), dgq_ref.shape)
        dgk_ref[...] += jnp.broadcast_to(dgk + pltpu.roll(dgk, A_HEAD_DIM, 1), dgk_ref.shape)

    full = lambda a: pl.BlockSpec(a.shape, lambda i, j: (0, 0))
    main_o = pl.BlockSpec((tile, LANES), lambda i, j: (i, j))
    edge_o = pl.BlockSpec((None, halo, LANES), lambda i, j: (i, 0, j))
    main_s = jax.ShapeDtypeStruct((s, D_MODEL), F32)
    edge_s = jax.ShapeDtypeStruct((nt, halo, D_MODEL), F32)
    acc_o = pl.BlockSpec((8, LANES), lambda i, j: (0, 0))
    acc_s = jax.ShapeDtypeStruct((8, LANES), F32)
    big = pltpu.VMEM((tile + 2 * halo, LANES), F32)
    own = pltpu.VMEM((tile, LANES), F32)
    outs = pl.pallas_call(
        body, name=name, grid=(nt, D_MODEL // LANES),
        in_specs=_attn_in_specs(grp, s, tile, halo) + [full(gq), full(gk), main_o, main_o, main_o, main_o, ANY],
        out_specs=[pl.BlockSpec((tile, LANES), lambda i, j: (i, 24 * grp + j)), edge_o, main_o, edge_o, edge_o, main_o, edge_o, acc_o, acc_o],
        out_shape=[jax.ShapeDtypeStruct(d_qkv.shape, d_qkv.dtype), edge_s, main_s, edge_s, edge_s, main_s, edge_s, acc_s, acc_s],
        scratch_shapes=[own, big, big, own, big, big],
        input_output_aliases={16: 0},
        compiler_params=_params(("arbitrary", "arbitrary")),
    )(qkv, qkv, qkv, qkv, qkv, qkv, qkv, tab, tab, tab, gq, gk, o, lse, d_o, d_lse, d_qkv)
    d_qkv, dkp, dkm, dkn, dvp, dvm, dvn, dgq, dgk = outs
    return d_qkv, (dkp, dkm, dkn), (dvp, dvm, dvn), dgq, dgk


def attn_combine(name, parts, s, dil, d_qkv, col):
    prev_part, main_part, next_part = parts
    tile, halo, _ = _attn_geometry(s, dil)
    nt = s // tile
    cols = D_MODEL // 2

    def body(m_ref, from_prev_ref, from_next_ref, _, o_ref):
        i = pl.program_id(0)
        o_ref[...] = m_ref[...].astype(o_ref.dtype)
        head = m_ref[0:halo, :] + jnp.where(i > 0, from_prev_ref[...], 0.0)
        o_ref[0:halo, :] = head.astype(o_ref.dtype)
        tail = m_ref[tile - halo:tile, :] + jnp.where(i < nt - 1, from_next_ref[...], 0.0)
        o_ref[tile - halo:tile, :] = tail.astype(o_ref.dtype)

    return pl.pallas_call(
        body, name=name, grid=(nt, D_MODEL // cols),
        in_specs=[pl.BlockSpec((tile, cols), lambda i, c: (i, c)),
                  pl.BlockSpec((None, halo, cols), lambda i, c: (jnp.maximum(i - 1, 0), 0, c)),
                  pl.BlockSpec((None, halo, cols), lambda i, c: (jnp.minimum(i + 1, nt - 1), 0, c)), ANY],
        out_specs=pl.BlockSpec((tile, cols), lambda i, c: (i, (D_MODEL // cols) * col + c)),
        out_shape=jax.ShapeDtypeStruct(d_qkv.shape, d_qkv.dtype), input_output_aliases={3: 0},
        compiler_params=_params(("parallel", "parallel")),
    )(main_part, next_part, prev_part, d_qkv)


def attn_merge_fwd(name, os_, lses):
    s = os_[0].shape[0]

    def body(o0, o1, o2, l0, l1, l2, out_ref):
        out_ref[...] = _merge_groups(o0[...], o1[...], o2[...], l0[...], l1[...], l2[...]).astype(out_ref.dtype)

    return _rows_call(name, body, [(a, "row") for a in (*os_, *lses)], [(D_MODEL, BF16, "row")], s, _row_tile(s, 256))[0]


def attn_merge_bwd(name, os_, lses, d_out):
    s = os_[0].shape[0]

    def body(o0, o1, o2, l0, l1, l2, d_ref, *outs):
        _, vjp = jax.vjp(_merge_groups, o0[...], o1[...], o2[...], l0[...], l1[...], l2[...])
        for ref, val in zip(outs, vjp(d_ref[...])):
            ref[...] = val

    return _rows_call(name, body, [(a, "row") for a in (*os_, *lses, d_out)],
                      [(D_MODEL, F32, "row")] * 6, s, _row_tile(s, 256))


def gla_gate_fwd(name, z, wg, bias):
    s = z.shape[0]

    def body(z_ref, w_ref, b_ref, o_ref):
        o_ref[...] = _gate(z_ref[...], w_ref[...], b_ref[...])

    return _rows_call(name, body, [(z, "row"), (wg, "full"), (bias, "full")], [(D_MODEL, F32, "row")], s, _row_tile(s))[0]


def gla_gate_bwd(name, z, wg, bias, d_la_f, d_la_b):
    s = z.shape[0]
    tile = _row_tile(s)

    def body(z_ref, w_ref, b_ref, df_ref, db_ref, dz_ref, dw_ref, dbias_ref):
        _, vjp = jax.vjp(_gate, z_ref[...], w_ref[...], b_ref[...])
        dz, dw, dbias = vjp(jnp.concatenate([df_ref[...], db_ref[...]], axis=1))
        dz_ref[...] = dz

        @pl.when(pl.program_id(0) == 0)
        def _():
            dw_ref[...] = jnp.zeros_like(dw_ref)

        dw_ref[...] += dw
        _acc_rows(dbias_ref, dbias)

    return pl.pallas_call(
        body, name=name, grid=(s // tile,),
        in_specs=[pl.BlockSpec((tile, LANES), lambda i: (i, 0)), pl.BlockSpec(wg.shape, lambda i: (0, 0)),
                  pl.BlockSpec(bias.shape, lambda i: (0, 0)), pl.BlockSpec((tile, 512), lambda i: (i, 0)),
                  pl.BlockSpec((tile, 512), lambda i: (i, 0))],
        out_specs=[pl.BlockSpec((tile, LANES), lambda i: (i, 0)), pl.BlockSpec(wg.shape, lambda i: (0, 0)),
                   pl.BlockSpec((8, D_MODEL), lambda i: (0, 0))],
        out_shape=[jax.ShapeDtypeStruct((s, LANES), F32), jax.ShapeDtypeStruct(wg.shape, F32), jax.ShapeDtypeStruct((8, D_MODEL), F32)],
        compiler_params=_params(("arbitrary",)),
    )(z, wg, bias, d_la_f, d_la_b)


def _gla_in_specs(tile, order, la_col0):
    t = order
    return [pl.BlockSpec((tile, B_KEY_DIM), lambda h, n: (t(n), h)),
            pl.BlockSpec((tile, B_KEY_DIM), lambda h, n: (t(n), B_HEADS + h)),
            pl.BlockSpec((tile, B_VAL_DIM), lambda h, n: (t(n), B_HEADS + h)),
            pl.BlockSpec((tile, B_KEY_DIM), lambda h, n: (t(n), la_col0 + h))]


def gla_fwd(name, proj, la, reverse):
    s = proj.shape[0]
    tile = _row_tile(s, GLA_TILE)
    nt = s // tile
    order = (lambda n: nt - 1 - n) if reverse else (lambda n: n)

    def body(q_ref, k_ref, v_ref, la_ref, o_ref, st_ref, st_scr):
        @pl.when(pl.program_id(1) == 0)
        def _():
            st_scr[...] = jnp.zeros_like(st_scr)

        st_ref[...] = st_scr[...]
        o, st = _gla_tile(q_ref[...], k_ref[...], v_ref[...], la_ref[...], st_scr[...], reverse)
        o_ref[...] = o
        st_scr[...] = st

    return pl.pallas_call(
        body, name=name, grid=(B_HEADS, nt), in_specs=_gla_in_specs(tile, order, B_HEADS if reverse else 0),
        out_specs=[pl.BlockSpec((tile, B_VAL_DIM), lambda h, n: (order(n), h)),
                   pl.BlockSpec((None, None, B_VAL_DIM, B_KEY_DIM), lambda h, n: (h, order(n), 0, 0))],
        out_shape=[jax.ShapeDtypeStruct((s, D_MODEL), F32), jax.ShapeDtypeStruct((B_HEADS, nt, B_VAL_DIM, B_KEY_DIM), F32)],
        scratch_shapes=[pltpu.VMEM((B_VAL_DIM, B_KEY_DIM), F32)],
        compiler_params=_params(("parallel", "arbitrary")),
    )(proj, proj, proj, la)


def gla_bwd(name, proj, la, states, d_o, reverse, prev=None):
    s = proj.shape[0]
    tile = _row_tile(s, GLA_TILE)
    nt = s // tile
    order = (lambda n: n) if reverse else (lambda n: nt - 1 - n)

    def body(*refs):
        q_ref, k_ref, v_ref, la_ref, st_ref, do_ref = refs[:6]
        rest = refs[6:]
        prev_refs = rest[:3] if prev is not None else None
        dq_ref, dk_ref, dv_ref, dla_ref, dst_scr = rest[3:] if prev is not None else rest

        @pl.when(pl.program_id(1) == 0)
        def _():
            dst_scr[...] = jnp.zeros_like(dst_scr)

        _, vjp = jax.vjp(functools.partial(_gla_tile, reverse=reverse), q_ref[...], k_ref[...], v_ref[...], la_ref[...], st_ref[...])
        dq, dk, dv, dla, dst = vjp((do_ref[...], dst_scr[...]))
        if prev_refs is not None:
            dq, dk, dv = dq + prev_refs[0][...], dk + prev_refs[1][...], dv + prev_refs[2][...]
        dq_ref[...], dk_ref[...], dv_ref[...], dla_ref[...] = dq, dk, dv, dla
        dst_scr[...] = dst

    key_spec = pl.BlockSpec((tile, B_KEY_DIM), lambda h, n: (order(n), h))
    val_spec = pl.BlockSpec((tile, B_VAL_DIM), lambda h, n: (order(n), h))
    in_specs = _gla_in_specs(tile, order, B_HEADS if reverse else 0) + [
        pl.BlockSpec((None, None, B_VAL_DIM, B_KEY_DIM), lambda h, n: (h, order(n), 0, 0)), val_spec]
    ins = [proj, proj, proj, la, states, d_o]
    if prev is not None:
        in_specs += [key_spec, key_spec, val_spec]
        ins += list(prev)
    return pl.pallas_call(
        body, name=name, grid=(B_HEADS, nt), in_specs=in_specs,
        out_specs=[key_spec, key_spec, val_spec, key_spec],
        out_shape=[jax.ShapeDtypeStruct((s, 512), F32), jax.ShapeDtypeStruct((s, 512), F32),
                   jax.ShapeDtypeStruct((s, D_MODEL), F32), jax.ShapeDtypeStruct((s, 512), F32)],
        scratch_shapes=[pltpu.VMEM((B_VAL_DIM, B_KEY_DIM), F32)],
        compiler_params=_params(("parallel", "arbitrary")),
    )(*ins)


def _r_spec(tile):
    return pl.BlockSpec((tile, D_MODEL), lambda i: (i, 2))


def gla_post_fwd(name, o_f, o_b, proj, gain):
    s = o_f.shape[0]
    tile = _row_tile(s)

    def body(of_ref, ob_ref, r_ref, g_ref, out_ref):
        out_ref[...] = _gla_post(of_ref[...], ob_ref[...], r_ref[...], g_ref[...]).astype(out_ref.dtype)

    row = pl.BlockSpec((tile, D_MODEL), lambda i: (i, 0))
    return pl.pallas_call(body, name=name, grid=(s // tile,),
                          in_specs=[row, row, _r_spec(tile), pl.BlockSpec(gain.shape, lambda i: (0, 0))], out_specs=row,
                          out_shape=jax.ShapeDtypeStruct((s, D_MODEL), BF16), compiler_params=_params(("parallel",)))(o_f, o_b, proj, gain)


def gla_post_bwd(name, o_f, o_b, proj, gain, d_out):
    s = o_f.shape[0]
    tile = _row_tile(s)

    def body(of_ref, ob_ref, r_ref, g_ref, d_ref, do_ref, dr_ref, dg_ref):
        _, vjp = jax.vjp(_gla_post, of_ref[...], ob_ref[...], r_ref[...], g_ref[...])
        d_of, _, dr, dg = vjp(d_ref[...])
        do_ref[...] = d_of
        dr_ref[...] = dr
        _acc_rows(dg_ref, dg)

    row = pl.BlockSpec((tile, D_MODEL), lambda i: (i, 0))
    return pl.pallas_call(
        body, name=name, grid=(s // tile,),
        in_specs=[row, row, _r_spec(tile), pl.BlockSpec(gain.shape, lambda i: (0, 0)), row],
        out_specs=[row, row, pl.BlockSpec((8, D_MODEL), lambda i: (0, 0))],
        out_shape=[jax.ShapeDtypeStruct((s, D_MODEL), F32), jax.ShapeDtypeStruct((s, D_MODEL), F32), jax.ShapeDtypeStruct((8, D_MODEL), F32)],
        compiler_params=_params(("arbitrary",)))(o_f, o_b, proj, gain, d_out)


def _hid(tile, where):
    return pl.BlockSpec((None, tile, FFN_BLK), where)


def _pair(tile, where):
    return pl.BlockSpec((2, None, tile, FFN_BLK), where)


def _w_gu_spec(layer, where_j):
    return pl.BlockSpec((2, None, None, D_MODEL, FFN_BLK), lambda *g: (0, where_j(*g), layer, 0, 0))


def ffn_fwd(name, h_mid, hn2, w_gu, w_down, layer, next_gain=None):
    s = hn2.shape[0]
    tm = _row_tile(s, 1024)
    nt = s // tm

    def gu_body(x_ref, w_ref, gu_ref, act_ref):
        x = x_ref[...]
        g = _dot(x, w_ref[0], NN)
        u = _dot(x, w_ref[1], NN)
        gu_ref[0] = g
        gu_ref[1] = u
        act_ref[...] = _swiglu_act(g, u).astype(act_ref.dtype)

    gu, act = pl.pallas_call(
        gu_body, name=name + "_gu", grid=(4, nt),
        in_specs=[pl.BlockSpec((tm, D_MODEL), lambda j, i: (i, 0)), _w_gu_spec(layer, lambda j, i: j)],
        out_specs=[_pair(tm, lambda j, i: (0, j, i, 0)), _hid(tm, lambda j, i: (j, i, 0))],
        out_shape=[jax.ShapeDtypeStruct((2, 4, s, FFN_BLK), F32), jax.ShapeDtypeStruct((4, s, FFN_BLK), BF16)],
        compiler_params=_params(("parallel", "parallel")))(hn2, w_gu)
    def down_body(*refs):
        act_ref, w_ref, res_ref = refs[:3]
        g_ref = refs[3] if next_gain is not None else None
        o_ref = refs[4] if next_gain is not None else refs[3]
        out = res_ref[...]
        for j in range(4):
            out = out + _dot(act_ref[j], w_ref[j], NN)
        o_ref[...] = out
        if next_gain is not None:
            refs[5][...] = _rms(out, g_ref[...]).astype(BF16)

    row = pl.BlockSpec((tm, D_MODEL), lambda i: (i, 0))
    in_specs = [pl.BlockSpec((4, tm, FFN_BLK), lambda i: (0, i, 0)), pl.BlockSpec((4, FFN_BLK, D_MODEL), lambda i: (0, 0, 0)), row]
    ins = [act, w_down, h_mid]
    out_specs, out_shape = row, jax.ShapeDtypeStruct((s, D_MODEL), F32)
    if next_gain is not None:
        in_specs.append(pl.BlockSpec(next_gain.shape, lambda i: (0, 0)))
        ins.append(next_gain)
        out_specs, out_shape = [row, row], [out_shape, jax.ShapeDtypeStruct((s, D_MODEL), BF16)]
    h_next = pl.pallas_call(down_body, name=name + "_down", grid=(nt,), in_specs=in_specs, out_specs=out_specs,
                            out_shape=out_shape, compiler_params=_params(("parallel",)))(*ins)
    return h_next, gu, act


def ffn_bwd(name, dh_next, hn2, gu, act, w_gu, w_down, layer):
    s = hn2.shape[0]
    tm = _row_tile(s, 1024)
    nt = s // tm
    tw = _row_tile(s, 2048)
    ntw = s // tw
    d_wd = _matmul(name + "_dwd", act, dh_next, dims=TN, grid=(4, ntw), red_axis=1,
                   a_spec=_hid(tw, lambda j, i: (j, i, 0)), b_spec=pl.BlockSpec((tw, D_MODEL), lambda j, i: (i, 0)),
                   o_spec=pl.BlockSpec((None, FFN_BLK, D_MODEL), lambda j, i: (j, 0, 0)),
                   out_shape=jax.ShapeDtypeStruct((4, FFN_BLK, D_MODEL), BF16), acc_shape=(FFN_BLK, D_MODEL))

    def dgu_body(dy_ref, wd_ref, gu_ref, dgu_ref):
        d_act = _dot(dy_ref[...], wd_ref[...], NT)
        g, u = gu_ref[0], gu_ref[1]
        sg = _sigmoid(g)
        silu = g * sg
        dgu_ref[0] = (d_act * u * (sg + silu * (1.0 - sg))).astype(dgu_ref.dtype)
        dgu_ref[1] = (d_act * silu).astype(dgu_ref.dtype)

    d_gu = pl.pallas_call(
        dgu_body, name=name + "_dgu", grid=(nt, 4),
        in_specs=[pl.BlockSpec((tm, D_MODEL), lambda i, j: (i, 0)), pl.BlockSpec((None, FFN_BLK, D_MODEL), lambda i, j: (j, 0, 0)),
                  _pair(tm, lambda i, j: (0, j, i, 0))],
        out_specs=_pair(tm, lambda i, j: (0, j, i, 0)), out_shape=jax.ShapeDtypeStruct((2, 4, s, FFN_BLK), BF16),
        compiler_params=_params(("parallel", "parallel")))(dh_next, w_down, gu)

    tx = _row_tile(s, 512)

    def dx_body(d_ref, w_ref, o_ref):
        out = None
        for t in range(2):
            for j in range(4):
                prod = _dot(d_ref[t, j], w_ref[t, j], NT)
                out = prod if out is None else out + prod
        o_ref[...] = out

    d_hn2 = pl.pallas_call(
        dx_body, name=name + "_dx", grid=(s // tx,),
        in_specs=[pl.BlockSpec((2, 4, tx, FFN_BLK), lambda i: (0, 0, i, 0)),
                  pl.BlockSpec((2, 4, None, D_MODEL, FFN_BLK), lambda i: (0, 0, layer, 0, 0))],
        out_specs=pl.BlockSpec((tx, D_MODEL), lambda i: (i, 0)), out_shape=jax.ShapeDtypeStruct((s, D_MODEL), F32),
        compiler_params=_params(("parallel",)))(d_gu, w_gu)

    def dw_body(x_ref, d_ref, o_ref, acc_ref):
        x = x_ref[...]
        k = pl.program_id(1)
        for t in range(2):
            prod = _dot(x, d_ref[t], TN)

            @pl.when(k == 0)
            def _():
                acc_ref[t] = prod

            @pl.when(k > 0)
            def _():
                acc_ref[t] += prod

        @pl.when(k == ntw - 1)
        def _():
            o_ref[...] = acc_ref[...].astype(o_ref.dtype)

    d_wgu = pl.pallas_call(
        dw_body, name=name + "_dwgu", grid=(4, ntw),
        in_specs=[pl.BlockSpec((tw, D_MODEL), lambda j, i: (i, 0)), _pair(tw, lambda j, i: (0, j, i, 0))],
        out_specs=pl.BlockSpec((2, None, D_MODEL, FFN_BLK), lambda j, i: (0, j, 0, 0)),
        out_shape=jax.ShapeDtypeStruct((2, 4, D_MODEL, FFN_BLK), BF16),
        scratch_shapes=[pltpu.VMEM((2, D_MODEL, FFN_BLK), F32)],
        compiler_params=_params(("parallel", "arbitrary")))(hn2, d_gu)
    return d_hn2, d_wgu, d_wd


def _my_place():
    return lax.axis_index("x"), lax.axis_index("y"), lax.axis_index("c")


def _flip(place, k):
    x, y, c = place
    return (1 - x if k & 4 else x, 1 - y if k & 2 else y, 1 - c if k & 1 else c)


def _index(place):
    return 4 * place[0] + 2 * place[1] + place[2]


def all_gather(arrs):
    n = len(arrs)

    def body(*refs):
        ins, outs = refs[:n], refs[n:2 * n]
        send_sems, recv_sems, local_sems = refs[2 * n:]
        me = _my_place()
        sibling = _flip(me, 1)
        chips = (4, 2, 6)

        def copy(a, k, block, to, src=None):
            dst = outs[a].at[_index(block)]
            return pltpu.make_async_remote_copy(src_ref=dst if src is None else src, dst_ref=dst, send_sem=send_sems.at[a, k],
                                                recv_sem=recv_sems.at[a, k], device_id=to, device_id_type=MESH)

        started = []
        for a in range(n):
            mine = pltpu.make_async_copy(ins[a], outs[a].at[_index(me)], local_sems.at[a])
            mine.start()
            started.append(mine)
        first = []
        for a in range(n):
            first.append(copy(a, 0, me, sibling, src=ins[a]))
            first += [copy(a, 1 + j, me, _flip(me, k), src=ins[a]) for j, k in enumerate(chips)]
        for cp in first:
            cp.start()
        passed = []
        for a in range(n):
            for j, k in enumerate(chips):
                copy(a, 1 + j, _flip(me, k), me).wait_recv()
                fwd = copy(a, 4 + j, _flip(me, k), sibling)
                fwd.start()
                passed.append(fwd)
        for a in range(n):
            copy(a, 0, sibling, me).wait_recv()
            for j, k in enumerate(chips):
                copy(a, 4 + j, _flip(sibling, k), me).wait_recv()
        for cp in first + passed:
            cp.wait_send()
        for cp in started:
            cp.wait()

    return pl.pallas_call(
        body, name="all_gather_weights", in_specs=[ANY] * n, out_specs=[ANY] * n,
        out_shape=[jax.ShapeDtypeStruct((N_DEV,) + a.shape, a.dtype) for a in arrs],
        scratch_shapes=[pltpu.SemaphoreType.DMA((n, 7)), pltpu.SemaphoreType.DMA((n, 7)), pltpu.SemaphoreType.DMA((n,))],
    )(*arrs)


def exchange_partials(arrs):
    n = len(arrs)

    def body(*refs):
        ins, outs = refs[:n], refs[n:2 * n]
        send_sems, recv_sems, local_sems = refs[2 * n:]
        me = _my_place()
        local = []
        for a in range(n):
            cp = pltpu.make_async_copy(ins[a].at[_index(me)], outs[a].at[_index(me)], local_sems.at[a])
            cp.start()
            local.append(cp)

        def copy(a, k, src_block, dst_block):
            return pltpu.make_async_remote_copy(src_ref=ins[a].at[_index(src_block)], dst_ref=outs[a].at[_index(dst_block)],
                                                send_sem=send_sems.at[a, k - 1], recv_sem=recv_sems.at[a, k - 1],
                                                device_id=_flip(me, k), device_id_type=MESH)

        sent = []
        for a in range(n):
            for k in range(1, N_DEV):
                cp = copy(a, k, _flip(me, k), me)
                cp.start()
                sent.append(cp)
        for a in range(n):
            for k in range(1, N_DEV):
                copy(a, k, me, _flip(me, k)).wait_recv()
        for cp in sent:
            cp.wait_send()
        for cp in local:
            cp.wait()

    return pl.pallas_call(
        body, name="exchange_weight_grads", in_specs=[ANY] * n, out_specs=[ANY] * n,
        out_shape=[jax.ShapeDtypeStruct(a.shape, a.dtype) for a in arrs],
        scratch_shapes=[pltpu.SemaphoreType.DMA((n, 7)), pltpu.SemaphoreType.DMA((n, 7)), pltpu.SemaphoreType.DMA((n,))],
    )(*arrs)


def adamw_shard(name, parts, w, m, v, layer, tile):
    rows, cols = parts.shape[1:]
    assert rows % tile == 0
    off = layer * (rows // tile)

    def body(p_ref, w_ref, m_ref, v_ref, g_ref, d_ref, nm_ref, nv_ref):
        g = p_ref[0].astype(F32)
        for src in range(1, N_DEV):
            g = g + p_ref[src].astype(F32)
        g_ref[...] = g
        d_ref[...], nm_ref[...], nv_ref[...] = _adamw(w_ref[...], g, m_ref[...], v_ref[...])

    src_row = pl.BlockSpec((tile, cols), lambda i: (off + i, 0))
    row = pl.BlockSpec((tile, cols), lambda i: (i, 0))
    shape = jax.ShapeDtypeStruct((rows, cols), F32)
    return pl.pallas_call(body, name=name, grid=(rows // tile,),
                          in_specs=[pl.BlockSpec((N_DEV, tile, cols), lambda i: (0, i, 0)), src_row, src_row, src_row],
                          out_specs=[row] * 4, out_shape=[shape] * 4, compiler_params=_params(("parallel",)))(parts, w, m, v)


def allreduce_adamw_replicated(partial, w, m, v, n_loss_rows):
    rows = partial.shape[0]

    def body(p_ref, w_ref, m_ref, v_ref, g_ref, d_ref, nm_ref, nv_ref, loss_ref, recv_ref, send_sems, recv_sems):
        me = _my_place()
        recv_ref[_index(me)] = p_ref[...]
        copies = []
        for k in range(1, N_DEV):
            peer = _flip(me, k)
            cp = pltpu.make_async_remote_copy(src_ref=p_ref, dst_ref=recv_ref.at[_index(me)], send_sem=send_sems.at[k - 1],
                                              recv_sem=recv_sems.at[k - 1], device_id=peer, device_id_type=MESH)
            cp.start()
            copies.append((cp, peer))
        for k, (cp, peer) in enumerate(copies):
            pltpu.make_async_remote_copy(src_ref=p_ref, dst_ref=recv_ref.at[_index(peer)], send_sem=send_sems.at[k],
                                         recv_sem=recv_sems.at[k], device_id=peer, device_id_type=MESH).wait_recv()
        for cp, _ in copies:
            cp.wait_send()
        g = recv_ref[0]
        for src in range(1, N_DEV):
            g = g + recv_ref[src]
        g_ref[...] = g
        d_ref[...], nm_ref[...], nv_ref[...] = _adamw(w_ref[...], g, m_ref[...], v_ref[...])
        loss = (0.5 / D_MODEL) * jnp.sum(g[rows - n_loss_rows:, :])
        loss_ref[...] = jnp.full(loss_ref.shape, loss, F32)

    shape = jax.ShapeDtypeStruct((rows, LANES), F32)
    return pl.pallas_call(
        body, name="allreduce_adamw_replicated", in_specs=[VMEM_SPEC] * 4, out_specs=[VMEM_SPEC] * 5,
        out_shape=[shape] * 4 + [jax.ShapeDtypeStruct((8, LANES), F32)],
        scratch_shapes=[pltpu.VMEM((N_DEV, rows, LANES), F32), pltpu.SemaphoreType.DMA((7,)), pltpu.SemaphoreType.DMA((7,))],
    )(partial, w, m, v)


def _pack_rows(flat, cols):
    n = flat.shape[-1]
    rows = -(-n // cols)
    rows = -(-rows // 48) * 48
    flat = jnp.pad(flat, [(0, 0)] * (flat.ndim - 1) + [(0, rows * cols - n)])
    return flat.reshape(flat.shape[:-1] + (rows, cols))


def _unpack(flat, shapes):
    out, off = [], 0
    for shp in shapes:
        n = 1
        for d in shp:
            n *= d
        out.append(flat[off:off + n].reshape(shp))
        off += n
    return out


def _to_dev_cols(a):
    w = a.shape[-1] // N_DEV
    return jnp.moveaxis(a.reshape(a.shape[:-1] + (N_DEV, w)), -2, 0)


def _from_dev_cols(a):
    a = jnp.moveaxis(a, 0, -2)
    return a.reshape(a.shape[:-2] + (a.shape[-2] * a.shape[-1],))


def kernel(x, attn_norm, ffn_norm, a_w_in, a_q_norm, a_k_norm, a_w_out, b_w_in, b_w_gate_f, b_gate_bias_f, b_w_gate_b, b_gate_bias_b, b_out_norm, b_w_out, ffn_w_gate_up, ffn_w_down, loss_target, m_attn_norm, m_ffn_norm, m_a_w_in, m_a_q_norm, m_a_k_norm, m_a_w_out, m_b_w_in, m_b_w_gate_f, m_b_gate_bias_f, m_b_w_gate_b, m_b_gate_bias_b, m_b_out_norm, m_b_w_out, m_ffn_w_gate_up, m_ffn_w_down, v_attn_norm, v_ffn_norm, v_a_w_in, v_a_q_norm, v_a_k_norm, v_a_w_out, v_b_w_in, v_b_w_gate_f, v_b_gate_bias_f, v_b_w_gate_b, v_b_gate_bias_b, v_b_out_norm, v_b_w_out, v_ffn_w_gate_up, v_ffn_w_down):
    seq = x.shape[1]
    depth = attn_norm.shape[0]
    h = x.reshape(seq, D_MODEL)
    target = loss_target.reshape(seq, D_MODEL)
    n_a, n_b = a_w_in.shape[0], b_w_in.shape[0]

    small = jnp.concatenate([t.reshape(-1) for t in (b_w_gate_f, b_gate_bias_f, b_w_gate_b, b_gate_bias_b, b_out_norm)])
    small = _pack_rows(small, LANES)
    g_a_in, g_a_out, g_b_in, g_b_out, g_gu, g_down, g_small = all_gather(
        [a_w_in.astype(BF16), a_w_out.astype(BF16), b_w_in.astype(BF16), b_w_out.astype(BF16),
         ffn_w_gate_up.astype(BF16), ffn_w_down.astype(BF16), small])
    w_a_in = _from_dev_cols(g_a_in)
    w_a_out = jnp.moveaxis(g_a_out, 0, 1).reshape(n_a, D_MODEL, D_MODEL)
    w_b_in = _from_dev_cols(g_b_in)
    w_b_out = jnp.moveaxis(g_b_out, 0, 1).reshape(n_b, D_MODEL, D_MODEL)
    w_down = jnp.moveaxis(g_down, 0, 1).reshape(depth, 4, FFN_BLK, D_MODEL)
    w_gu = g_gu.reshape(2, 4, depth, D_MODEL, FFN_BLK)
    small_shapes = [t.shape for t in (b_w_gate_f, b_gate_bias_f, b_w_gate_b, b_gate_bias_b, b_out_norm)]
    per_dev = [_unpack(g_small[d].reshape(-1), small_shapes) for d in range(N_DEV)]
    wgf, bgf, wgb, bgb, onorm = [_from_dev_cols(jnp.stack([per_dev[d][t] for d in range(N_DEV)])) for t in range(5)]
    w_gate = jnp.zeros((n_b, LANES, D_MODEL), F32)
    w_gate = w_gate.at[:, 0:16, 0:512].set(wgf).at[:, 16:32, 512:1024].set(wgb)
    gate_bias = jnp.concatenate([bgf, bgb], axis=1).reshape(n_b, 1, D_MODEL)
    out_gain = onorm.reshape(n_b, 1, D_MODEL)
    w_b_main = w_b_in[:, :, :3072]
    w_b_z = jnp.pad(w_b_in[:, :, 3072:], ((0, 0), (0, 0), (0, LANES - 32)))

    dh, sq_err, grads = _forward_backward(h, target, attn_norm, ffn_norm, a_q_norm, a_k_norm, w_a_in, w_a_out, w_b_main, w_b_z,
                                          w_gate, gate_bias, out_gain, w_b_out, w_gu, w_down)
    sharded_w = (a_w_in, a_w_out, b_w_in, b_w_gate_f, b_gate_bias_f, b_w_gate_b, b_gate_bias_b, b_out_norm, b_w_out, ffn_w_gate_up, ffn_w_down)
    sharded_m = (m_a_w_in, m_a_w_out, m_b_w_in, m_b_w_gate_f, m_b_gate_bias_f, m_b_w_gate_b, m_b_gate_bias_b, m_b_out_norm, m_b_w_out, m_ffn_w_gate_up, m_ffn_w_down)
    sharded_v = (v_a_w_in, v_a_w_out, v_b_w_in, v_b_w_gate_f, v_b_gate_bias_f, v_b_w_gate_b, v_b_gate_bias_b, v_b_out_norm, v_b_w_out, v_ffn_w_gate_up, v_ffn_w_down)
    rep_w = (attn_norm, ffn_norm, a_q_norm, a_k_norm)
    rep_m = (m_attn_norm, m_ffn_norm, m_a_q_norm, m_a_k_norm)
    rep_v = (v_attn_norm, v_ffn_norm, v_a_q_norm, v_a_k_norm)
    loss, outs = _reduce_and_update(grads, sq_err, sharded_w, sharded_m, sharded_v, rep_w, rep_m, rep_v)
    return (loss, dh.reshape(x.shape), *outs)


def _forward_backward(h, target, attn_norm, ffn_norm, a_q_norm, a_k_norm, w_a_in, w_a_out, w_b_main, w_b_z, w_gate, gate_bias,
                      out_gain, w_b_out, w_gu, w_down):
    seq = h.shape[0]
    depth = attn_norm.shape[0]
    n_a, n_b = w_a_in.shape[0], w_b_main.shape[0]
    tab = _rope_table(seq)
    pair_gain = lambda g: jnp.concatenate([g, g]).reshape(1, LANES)

    saved = []
    for i in range(depth):
        j = i // 2
        nm = f"l{i}"
        if i == 0:
            hn = rmsnorm_fwd(nm + "_norm1", h, attn_norm[i].reshape(1, D_MODEL))
        if i % 2 == 0:
            qkv = mm_nn(nm + "_qkv", hn, w_a_in[j])
            os_, lses = [], []
            for g, (_, dil) in enumerate(A_GROUPS):
                o, lse = attn_fwd(f"{nm}_attn{g}", qkv, tab, pair_gain(a_q_norm[j, g]), pair_gain(a_k_norm[j, g]), g, dil)
                os_.append(o)
                lses.append(lse)
            mixed = attn_merge_fwd(nm + "_merge", os_, lses)
            h_mid, hn2 = mm_nn(nm + "_out", mixed, w_a_out[j], res=h, norm_gain=ffn_norm[i].reshape(1, D_MODEL))
            mix_saved = (qkv, os_, lses, mixed)
        else:
            proj = mm_nn(nm + "_proj", hn, w_b_main[j])
            z = mm_nn(nm + "_z", hn, w_b_z[j])
            la = gla_gate_fwd(nm + "_gate", z, w_gate[j], gate_bias[j])
            o_f, st_f = gla_fwd(nm + "_gla_f", proj, la, False)
            o_b, st_b = gla_fwd(nm + "_gla_b", proj, la, True)
            mixed = gla_post_fwd(nm + "_post", o_f, o_b, proj, out_gain[j])
            h_mid, hn2 = mm_nn(nm + "_out", mixed, w_b_out[j], res=h, norm_gain=ffn_norm[i].reshape(1, D_MODEL))
            mix_saved = (proj, z, la, o_f, st_f, o_b, st_b, mixed)
        next_gain = attn_norm[i + 1].reshape(1, D_MODEL) if i + 1 < depth else None
        h_next, gu, act = ffn_fwd(nm + "_ffn", h_mid, hn2, w_gu, w_down[i], i, next_gain)
        saved.append((h, hn, mix_saved, h_mid, hn2, gu, act))
        if next_gain is not None:
            h_next, hn = h_next
        h = h_next

    dh, sq_err = loss_and_grad(h, target)

    g_attn_norm, g_ffn_norm = [None] * depth, [None] * depth
    g_a_w_in, g_a_w_out, g_a_q, g_a_k = [None] * n_a, [None] * n_a, [None] * n_a, [None] * n_a
    g_b_w_in, g_b_w_out, g_w_gate, g_gate_bias, g_out_gain = ([None] * n_b for _ in range(5))
    g_w_gu, g_w_down = [None] * depth, [None] * depth
    for i in reversed(range(depth)):
        j = i // 2
        nm = f"l{i}b"
        h_in, hn, mix_saved, h_mid, hn2, gu, act = saved[i]
        d_hn2, g_w_gu[i], g_w_down[i] = ffn_bwd(nm + "_ffn", dh, hn2, gu, act, w_gu, w_down[i], i)
        dh_mid, g_ffn_norm[i] = rmsnorm_bwd(nm + "_norm2", h_mid, ffn_norm[i].reshape(1, D_MODEL), d_hn2, dh)
        if i % 2 == 0:
            qkv, os_, lses, mixed = mix_saved
            d_mixed = mm_nt(nm + "_dmixed", dh_mid, w_a_out[j])
            g_a_w_out[j] = mm_tn(nm + "_dwout", mixed, dh_mid, by_block=True)
            d_parts = attn_merge_bwd(nm + "_merge", os_, lses, d_mixed)
            gq_l, gk_l = [], []
            d_qkv = lax.empty(qkv.shape, BF16)
            for g, (_, dil) in enumerate(A_GROUPS):
                d_qkv, dk_parts, dv_parts, dgq, dgk = attn_bwd(f"{nm}_attn{g}", qkv, tab, pair_gain(a_q_norm[j, g]),
                                                               pair_gain(a_k_norm[j, g]), os_[g], lses[g], d_parts[g], d_parts[3 + g], g, dil, d_qkv)
                d_qkv = attn_combine(f"{nm}_dk{g}", dk_parts, seq, dil, d_qkv, 3 * g + 1)
                d_qkv = attn_combine(f"{nm}_dv{g}", dv_parts, seq, dil, d_qkv, 3 * g + 2)
                gq_l.append(dgq[0, :A_HEAD_DIM])
                gk_l.append(dgk[0, :A_HEAD_DIM])
            g_a_q[j], g_a_k[j] = jnp.stack(gq_l), jnp.stack(gk_l)
            d_hn = mm_nt(nm + "_dhn", d_qkv, w_a_in[j])
            g_a_w_in[j] = mm_tn(nm + "_dwin", hn, d_qkv, tn=D_MODEL * 9 // N_DEV, by_block=True)
        else:
            proj, z, la, o_f, st_f, o_b, st_b, mixed = mix_saved
            d_mixed = mm_nt(nm + "_dmixed", dh_mid, w_b_out[j])
            g_b_w_out[j] = mm_tn(nm + "_dwout", mixed, dh_mid, by_block=True)
            d_o, d_r, dgain = gla_post_bwd(nm + "_post", o_f, o_b, proj, out_gain[j], d_mixed)
            g_out_gain[j] = dgain[0]
            dq, dk, dv, dla_f = gla_bwd(nm + "_gla_f", proj, la, st_f, d_o, False)
            dq, dk, dv, dla_b = gla_bwd(nm + "_gla_b", proj, la, st_b, d_o, True, prev=(dq, dk, dv))
            d_z, g_w_gate[j], dbias = gla_gate_bwd(nm + "_gate", z, w_gate[j], gate_bias[j], dla_f, dla_b)
            g_gate_bias[j] = dbias[0]
            d_proj = jnp.concatenate([dq, dk, dv, d_r], axis=1)
            d_hn = mm_nt(nm + "_dhn_z", d_z, w_b_z[j])
            d_hn = mm_nt(nm + "_dhn", d_proj, w_b_main[j], res=d_hn)
            g_b_w_in[j] = jnp.concatenate([mm_tn(nm + "_dwin", hn, d_proj), mm_tn(nm + "_dwz", hn, d_z)[:, :32]], axis=1)
        dh, g_attn_norm[i] = rmsnorm_bwd(nm + "_norm1", h_in, attn_norm[i].reshape(1, D_MODEL), d_hn, dh_mid)
    return dh, sq_err, (g_attn_norm, g_ffn_norm, g_a_w_in, g_a_w_out, g_a_q, g_a_k, g_b_w_in, g_b_w_out, g_w_gate, g_gate_bias,
                        g_out_gain, g_w_gu, g_w_down)


def _reduce_and_update(grads, sq_err, sharded_w, sharded_m, sharded_v, rep_w, rep_m, rep_v):
    (g_attn_norm, g_ffn_norm, g_a_w_in, g_a_w_out, g_a_q, g_a_k, g_b_w_in, g_b_w_out, g_w_gate, g_gate_bias, g_out_gain,
     g_w_gu, g_w_down) = grads
    depth, n_a, n_b = len(g_w_gu), len(g_a_w_in), len(g_b_w_in)

    g_w_gate = jnp.stack(g_w_gate)
    g_gate_bias = jnp.stack(g_gate_bias)
    small_parts = [_to_dev_cols(g_w_gate[:, 0:16, 0:512]), _to_dev_cols(g_gate_bias[:, 0:512]),
                   _to_dev_cols(g_w_gate[:, 16:32, 512:1024]), _to_dev_cols(g_gate_bias[:, 512:1024]),
                   _to_dev_cols(jnp.stack(g_out_gain).reshape(n_b, B_HEADS, B_VAL_DIM))]
    small_part = _pack_rows(jnp.concatenate([t.reshape(N_DEV, -1) for t in small_parts], axis=1), LANES).astype(BF16)
    families = [
        (0, g_a_w_in, 256),
        (1, [t.reshape(N_DEV, -1, D_MODEL) for t in g_a_w_out], 128),
        (2, [_to_dev_cols(t).astype(BF16) for t in g_b_w_in], 256),
        (8, [t.reshape(N_DEV, -1, D_MODEL) for t in g_b_w_out], 128),
        (9, [t.reshape(N_DEV, D_MODEL, FFN_BLK) for t in g_w_gu], 256),
        (10, [t.reshape(N_DEV, -1, D_MODEL) for t in g_w_down], 176),
    ]
    flat_parts = [p for _, parts, _ in families for p in parts] + [small_part]
    received = exchange_partials(flat_parts)
    sh_out = [None] * len(sharded_w)
    pos = 0
    for fam, parts, tile in families:
        w = sharded_w[fam]
        two_d = lambda t: t.reshape(-1, t.shape[-1])
        per_layer = []
        for layer in range(len(parts)):
            per_layer.append(adamw_shard(f"adamw_p{fam}_l{layer}", received[pos], two_d(w), two_d(sharded_m[fam]),
                                         two_d(sharded_v[fam]), layer, tile))
            pos += 1
        sh_out[fam] = [jnp.stack([per_layer[l][t] for l in range(len(parts))]).reshape(w.shape) for t in range(4)]
    small_ids = (3, 4, 5, 6, 7)
    pack_small = lambda ts: _pack_rows(jnp.concatenate([ts[i].reshape(-1) for i in small_ids]), LANES)
    small_out = adamw_shard("adamw_small", received[pos], pack_small(sharded_w), pack_small(sharded_m), pack_small(sharded_v), 0, 48)
    small_shapes = [sharded_w[i].shape for i in small_ids]
    for t in range(4):
        for i, val in zip(small_ids, _unpack(small_out[t].reshape(-1), small_shapes)):
            if sh_out[i] is None:
                sh_out[i] = [None] * 4
            sh_out[i][t] = val
    sh_grad, sh_delta, sh_m, sh_v = [[sh_out[i][t] for i in range(len(sharded_w))] for t in range(4)]

    rep_g = (jnp.stack([t[0] for t in g_attn_norm]), jnp.stack([t[0] for t in g_ffn_norm]), jnp.stack(g_a_q), jnp.stack(g_a_k))
    n_rep = sum(t.size for t in rep_w)
    n_rep_rows = -(-n_rep // (8 * LANES)) * 8
    n_loss_rows = 8 * D_MODEL // LANES

    def pack_rep(ts, tail):
        flat = jnp.concatenate([t.reshape(-1) for t in ts])
        flat = jnp.pad(flat, (0, n_rep_rows * LANES - n_rep))
        return jnp.concatenate([flat.reshape(n_rep_rows, LANES), tail], axis=0)

    zeros_tail = jnp.zeros((n_loss_rows, LANES), F32)
    rep_out = allreduce_adamw_replicated(pack_rep(rep_g, sq_err.reshape(n_loss_rows, LANES)), pack_rep(rep_w, zeros_tail),
                                         pack_rep(rep_m, zeros_tail), pack_rep(rep_v, zeros_tail + 1.0), n_loss_rows)
    rep_shapes = [t.shape for t in rep_w]
    r_grad, r_delta, r_m, r_v = [_unpack(p.reshape(-1), rep_shapes) for p in rep_out[:4]]
    loss = rep_out[4][0, 0]

    def ordered(rep, sh):
        return [rep[0], rep[1], sh[0], rep[2], rep[3]] + list(sh[1:])

    return loss, (*ordered(r_grad, sh_grad), *ordered(r_delta, sh_delta), *ordered(r_m, sh_m), *ordered(r_v, sh_v))
```

```python
import functools

import jax
import jax.numpy as jnp
from jax import lax
from jax.experimental import pallas as pl
from jax.experimental.pallas import tpu as pltpu

F32 = jnp.float32
BF16 = jnp.bfloat16
MXU_DTYPE = jnp.bfloat16

D_MODEL = 1024
N_DEV = 8
RMS_EPS = 1e-6
NEG_INF = -1e30
A_GROUPS = ((128, 1), (512, 4), (2048, 16))
A_HEAD_DIM = 64
A_HALF = 64
ATT_T = 128
ATT_TILE = 2048
ATT_FWD_BATCH = 8
ATT_BWD_BATCH = 4
B_HEADS = 4
B_KEY_DIM = 128
B_VAL_DIM = 256
B_CHUNK = 64
B_GATE_TAU = 16.0
GLA_TILE = 2048
FFN_HIDDEN = 2816
FFN_BLK = 2 * FFN_HIDDEN // N_DEV
ADAM_LR, ADAM_B1, ADAM_B2, ADAM_EPS, ADAM_WD, ADAM_STEP = 0.001, 0.9, 0.999, 1e-08, 0.01, 10
ROPE_THETA = 10000.0

V7X_VMEM_LIMIT = 60 * 1024 * 1024
LANES = 128
MESH = pl.DeviceIdType.MESH
ANY = pl.BlockSpec(memory_space=pl.ANY)
VMEM_SPEC = pl.BlockSpec(memory_space=pltpu.VMEM)

NN = ((1,), (0,))
NT = ((1,), (1,))
TN = ((0,), (0,))


def _dot(a, b, dims):
    return lax.dot_general(a.astype(MXU_DTYPE), b.astype(MXU_DTYPE), (dims, ((), ())), preferred_element_type=F32)


@jax.custom_vjp
def dot_nn(a, b):
    return _dot(a, b, NN)


@jax.custom_vjp
def dot_nt(a, b):
    return _dot(a, b, NT)


@jax.custom_vjp
def dot_tn(a, b):
    return _dot(a, b, TN)


dot_nn.defvjp(lambda a, b: (_dot(a, b, NN), (a, b)), lambda r, g: (dot_nt(g, r[1]), dot_tn(r[0], g)))
dot_nt.defvjp(lambda a, b: (_dot(a, b, NT), (a, b)), lambda r, g: (dot_nn(g, r[1]), dot_tn(g, r[0])))
dot_tn.defvjp(lambda a, b: (_dot(a, b, TN), (a, b)), lambda r, g: (dot_nt(r[1], g), dot_nn(r[0], g)))


def _bdot(a, b, dims):
    dn = (tuple((d[0] + 1,) for d in dims), ((0,), (0,)))
    return lax.dot_general(a.astype(MXU_DTYPE), b.astype(MXU_DTYPE), dn, preferred_element_type=F32)


@jax.custom_vjp
def bdot_nn(a, b):
    return _bdot(a, b, NN)


@jax.custom_vjp
def bdot_nt(a, b):
    return _bdot(a, b, NT)


@jax.custom_vjp
def bdot_tn(a, b):
    return _bdot(a, b, TN)


bdot_nn.defvjp(lambda a, b: (_bdot(a, b, NN), (a, b)), lambda r, g: (bdot_nt(g, r[1]), bdot_tn(r[0], g)))
bdot_nt.defvjp(lambda a, b: (_bdot(a, b, NT), (a, b)), lambda r, g: (bdot_nn(g, r[1]), bdot_tn(g, r[0])))
bdot_tn.defvjp(lambda a, b: (_bdot(a, b, TN), (a, b)), lambda r, g: (bdot_nt(r[1], g), bdot_nn(r[0], g)))


def _dot_f32(a, b):
    return lax.dot_general(a, b, (NN, ((), ())), precision=lax.Precision.HIGHEST, preferred_element_type=F32)


def _tri(n, upper):
    r = lax.broadcasted_iota(jnp.int32, (n, n), 0)
    c = lax.broadcasted_iota(jnp.int32, (n, n), 1)
    return jnp.where((c >= r) if upper else (c <= r), 1.0, 0.0).astype(F32)


def _chunk_cumsum(x, reverse):
    tri = jnp.broadcast_to(_tri(x.shape[1], reverse), (x.shape[0], x.shape[1], x.shape[1]))
    return lax.dot_general(tri, x, (((2,), (1,)), ((0,), (0,))), precision=lax.Precision.HIGHEST, preferred_element_type=F32)


@functools.partial(jax.custom_vjp, nondiff_argnums=(1,))
def cumsum_chunks(x, reverse):
    return _chunk_cumsum(x, reverse)


cumsum_chunks.defvjp(lambda x, reverse: (_chunk_cumsum(x, reverse), None), lambda reverse, _, g: (_chunk_cumsum(g, not reverse),))


def _swap32_raw(x):
    lane = lax.broadcasted_iota(jnp.int32, x.shape, 1)
    return jnp.where((lane % 64) < 32, pltpu.roll(x, 96, 1), pltpu.roll(x, 32, 1))


@jax.custom_vjp
def swap32(x):
    return _swap32_raw(x)


swap32.defvjp(lambda x: (_swap32_raw(x), None), lambda _, g: (_swap32_raw(g),))


def _rms(x, gain):
    return x * lax.rsqrt(jnp.mean(x * x, axis=-1, keepdims=True) + RMS_EPS) * gain


def _sigmoid(x):
    return 1.0 / (1.0 + jnp.exp(-x))


def _log_sigmoid(x):
    return jnp.minimum(x, 0.0) - jnp.log(1.0 + jnp.exp(-jnp.abs(x)))


def _qk_prep(x, tab, gain):
    lo = lax.broadcasted_iota(jnp.int32, (1, LANES), 1) < A_HEAD_DIM
    x2 = x * x
    s_lo = jnp.sum(jnp.where(lo, x2, 0.0), axis=-1, keepdims=True)
    s_hi = jnp.sum(jnp.where(lo, 0.0, x2), axis=-1, keepdims=True)
    xn = (x * lax.rsqrt(jnp.where(lo, s_lo, s_hi) / A_HEAD_DIM + RMS_EPS)) * gain
    return xn * tab[:, :LANES] + swap32(xn) * tab[:, LANES:]


def _stack_heads(x):
    lo = lax.broadcasted_iota(jnp.int32, (1, LANES), 1) < A_HEAD_DIM
    return jnp.concatenate([jnp.where(lo, x, 0.0), jnp.where(lo, 0.0, x)], axis=-2)


def _unstack_heads(x):
    t = x.shape[-2] // 2
    lo = lax.broadcasted_iota(jnp.int32, (1, LANES), 1) < A_HEAD_DIM
    return jnp.where(lo, x[..., :t, :], x[..., t:, :])


def _attn_scores(q, k, valid):
    s = _bdot(_stack_heads(q), k, NT) * (A_HEAD_DIM ** -0.5)
    return jnp.where(valid, s, NEG_INF)


def _attn_job(q, k, v, valid):
    s = _attn_scores(q, k, valid)
    mx = jnp.max(s, axis=-1, keepdims=True)
    p = jnp.exp(s - mx)
    l = jnp.sum(p, axis=-1, keepdims=True)
    out = _unstack_heads(_bdot(p, v, NN) / l)
    lse = mx + jnp.log(l)
    t = q.shape[1]
    lo = lax.broadcasted_iota(jnp.int32, (1, LANES), 1) < A_HEAD_DIM
    return out, jnp.where(lo, lse[:, :t], lse[:, t:])


def _attn_job_bwd(q, k, v, valid, d_out, out, lse, d_lse):
    qs = _stack_heads(q)
    lo = lax.broadcasted_iota(jnp.int32, (1, LANES), 1) < A_HEAD_DIM
    other = pltpu.roll(lse, A_HEAD_DIM, 2)
    row_lse = jnp.concatenate([jnp.where(lo, lse, other), jnp.where(lo, other, lse)], axis=1)
    p = jnp.exp(_attn_scores(q, k, valid) - jnp.concatenate([row_lse, row_lse], axis=2))
    dos = _stack_heads(d_out)
    dv = _bdot(p, dos, TN)
    dp = _bdot(dos, v, NT)
    inner = jnp.sum(dos * _stack_heads(out) - _stack_heads(d_lse), axis=-1, keepdims=True)
    ds = p * (dp - inner) * (A_HEAD_DIM ** -0.5)
    return _unstack_heads(_bdot(ds, k, NN)), _bdot(ds, qs, TN), dv


def _merge_groups(o0, o1, o2, l0, l1, l2):
    mx = lax.stop_gradient(jnp.maximum(jnp.maximum(l0, l1), l2))
    e0, e1, e2 = jnp.exp(l0 - mx), jnp.exp(l1 - mx), jnp.exp(l2 - mx)
    den = e0 + e1 + e2
    return (e0 / den) * o0 + (e1 / den) * o1 + (e2 / den) * o2


def _gla_tile(q, k, v, la, st, reverse):
    t = q.shape[0]
    nc = t // B_CHUNK
    split = lambda x: x.reshape(nc, B_CHUNK, x.shape[1])
    q, k, v, la = split(q * (B_KEY_DIM ** -0.5)), split(k), split(v), split(la)
    r = lax.broadcasted_iota(jnp.int32, (1, B_CHUNK, B_CHUNK), 1)
    c = lax.broadcasted_iota(jnp.int32, (1, B_CHUNK, B_CHUNK), 2)
    mask = (c > r) if reverse else (c <= r)
    b = cumsum_chunks(la, reverse)
    tot = jnp.sum(la, axis=1, keepdims=True)
    q_t = q * jnp.exp(b)
    k_t = k * jnp.exp(-b)
    k_end = k * jnp.exp(tot - b)
    attn = jnp.where(mask, bdot_nt(q_t, k_t), 0.0)
    kv = bdot_tn(v, k_end)
    decay = jnp.exp(tot)
    entering = [None] * nc
    for ci in (range(nc - 1, -1, -1) if reverse else range(nc)):
        entering[ci] = st
        st = st * decay[ci] + kv[ci]
    out = bdot_nn(attn, v) + bdot_nt(q_t, jnp.concatenate([e[None] for e in entering], axis=0))
    return out.reshape(t, out.shape[2]), st


def _gla_post(o_f, o_b, r, gain):
    o = o_f + o_b
    heads = [_rms(o[:, h * B_VAL_DIM:(h + 1) * B_VAL_DIM], gain[:, h * B_VAL_DIM:(h + 1) * B_VAL_DIM]) for h in range(B_HEADS)]
    return jnp.concatenate(heads, axis=1) * (r * _sigmoid(r))


def _gate(z, wg, bias):
    return _log_sigmoid(dot_nn(z, wg) + bias) / B_GATE_TAU


def _swiglu_act(g, u):
    return (g * _sigmoid(g)) * u


def _adamw(w, g, m, v):
    m = ADAM_B1 * m + (1.0 - ADAM_B1) * g
    v = ADAM_B2 * v + (1.0 - ADAM_B2) * jnp.square(g)
    m_hat = m / (1.0 - ADAM_B1 ** ADAM_STEP)
    v_hat = v / (1.0 - ADAM_B2 ** ADAM_STEP)
    delta = -ADAM_LR * (m_hat / (jnp.sqrt(v_hat) + ADAM_EPS) + ADAM_WD * w)
    return delta, m, v


def _params(sem=None):
    return pltpu.CompilerParams(dimension_semantics=sem, vmem_limit_bytes=V7X_VMEM_LIMIT)


def _row_tile(s, want=512):
    t = min(want, s)
    assert s % t == 0
    return t


def _matmul(name, a, b, *, dims, grid, a_spec, b_spec, o_spec, out_shape, red_axis=None, res=None, res_spec=None, acc_shape=None,
            norm_gain=None):
    n_red = grid[red_axis] if red_axis is not None else 1
    n_in = 2 + (res is not None) + (norm_gain is not None)

    def body(*refs):
        a_ref, b_ref = refs[0], refs[1]
        r_ref = refs[2] if res is not None else None
        g_ref = refs[n_in - 1] if norm_gain is not None else None
        o_ref = refs[n_in]
        n_ref = refs[n_in + 1] if norm_gain is not None else None
        prod = lax.dot_general(a_ref[...].astype(MXU_DTYPE), b_ref[...].astype(MXU_DTYPE), (dims, ((), ())),
                               preferred_element_type=F32)
        if red_axis is None:
            if r_ref is not None:
                prod = prod + r_ref[...]
            o_ref[...] = prod.astype(o_ref.dtype)
            if n_ref is not None:
                n_ref[...] = _rms(prod, g_ref[...]).astype(n_ref.dtype)
            return
        acc = refs[-1] if acc_shape is not None else o_ref
        k = pl.program_id(red_axis)

        @pl.when(k == 0)
        def _():
            acc[...] = prod + r_ref[...] if r_ref is not None else prod

        @pl.when(k > 0)
        def _():
            acc[...] += prod

        if acc_shape is not None or n_ref is not None:
            @pl.when(k == n_red - 1)
            def _():
                if acc_shape is not None:
                    o_ref[...] = acc[...].astype(o_ref.dtype)
                if n_ref is not None:
                    n_ref[...] = _rms(acc[...], g_ref[...]).astype(n_ref.dtype)

    ins = [a, b] + ([res] if res is not None else []) + ([norm_gain] if norm_gain is not None else [])
    specs = [a_spec, b_spec] + ([res_spec] if res is not None else [])
    out_specs, out_shapes = o_spec, out_shape
    if norm_gain is not None:
        specs.append(pl.BlockSpec(norm_gain.shape, lambda *g: (0, 0)))
        out_specs, out_shapes = [o_spec, o_spec], [out_shape, jax.ShapeDtypeStruct(out_shape.shape, BF16)]
    sem = tuple("arbitrary" if i == red_axis else "parallel" for i in range(len(grid)))
    return pl.pallas_call(body, name=name, grid=grid, in_specs=specs, out_specs=out_specs, out_shape=out_shapes,
                          scratch_shapes=[pltpu.VMEM(acc_shape, F32)] if acc_shape is not None else [],
                          compiler_params=_params(sem))(*ins)


def mm_nn(name, x, w, *, res=None, out_dtype=F32, tn=1024, norm_gain=None):
    m, k = x.shape
    n = w.shape[1]
    tm, tn = _row_tile(m, 1024), min(tn, n)
    return _matmul(name, x, w, dims=NN, grid=(n // tn, m // tm),
                   a_spec=pl.BlockSpec((tm, k), lambda j, i: (i, 0)), b_spec=pl.BlockSpec((k, tn), lambda j, i: (0, j)),
                   o_spec=pl.BlockSpec((tm, tn), lambda j, i: (i, j)), out_shape=jax.ShapeDtypeStruct((m, n), out_dtype),
                   res=res, res_spec=pl.BlockSpec((tm, tn), lambda j, i: (i, j)), norm_gain=norm_gain)


def mm_nt(name, dy, w, *, res=None, tn=2304):
    m, n = dy.shape
    k = w.shape[0]
    tm, tn = _row_tile(m, 1024), (tn if n % tn == 0 else min(1024, n))
    return _matmul(name, dy, w, dims=NT, grid=(m // tm, n // tn), red_axis=1,
                   a_spec=pl.BlockSpec((tm, tn), lambda i, j: (i, j)), b_spec=pl.BlockSpec((k, tn), lambda i, j: (0, j)),
                   o_spec=pl.BlockSpec((tm, k), lambda i, j: (i, 0)), out_shape=jax.ShapeDtypeStruct((m, k), F32),
                   res=res, res_spec=pl.BlockSpec((tm, k), lambda i, j: (i, 0)))


def mm_tn(name, x, dy, *, tn=1024, by_block=False):
    m, k = x.shape
    n = dy.shape[1]
    tm, tn = _row_tile(m, 2048), min(tn, n)
    if by_block:
        o_spec, out_shape, acc = pl.BlockSpec((None, k, tn), lambda j, i: (j, 0, 0)), jax.ShapeDtypeStruct((n // tn, k, tn), BF16), (k, tn)
    else:
        o_spec, out_shape, acc = pl.BlockSpec((k, tn), lambda j, i: (0, j)), jax.ShapeDtypeStruct((k, n), F32), None
    return _matmul(name, x, dy, dims=TN, grid=(n // tn, m // tm), red_axis=1,
                   a_spec=pl.BlockSpec((tm, k), lambda j, i: (i, 0)), b_spec=pl.BlockSpec((tm, tn), lambda j, i: (i, j)),
                   o_spec=o_spec, out_shape=out_shape, acc_shape=acc)


def _rows_call(name, body, ins, outs, s, tile):
    in_specs = []
    for a, kind in ins:
        if kind == "row":
            in_specs.append(pl.BlockSpec((tile, a.shape[1]), lambda i: (i, 0)))
        else:
            in_specs.append(pl.BlockSpec(a.shape, lambda i, nd=a.ndim: (0,) * nd))
    out_specs, out_shape = [], []
    for cols, dt, kind in outs:
        if kind == "row":
            out_specs.append(pl.BlockSpec((tile, cols), lambda i: (i, 0)))
            out_shape.append(jax.ShapeDtypeStruct((s, cols), dt))
        else:
            out_specs.append(pl.BlockSpec((8, cols), lambda i: (0, 0)))
            out_shape.append(jax.ShapeDtypeStruct((8, cols), dt))
    has_acc = any(kind == "acc" for _, _, kind in outs)
    return pl.pallas_call(body, name=name, grid=(s // tile,), in_specs=in_specs, out_specs=out_specs, out_shape=out_shape,
                          compiler_params=_params(("arbitrary",) if has_acc else ("parallel",)))(*[a for a, _ in ins])


def _acc_rows(ref, val):
    @pl.when(pl.program_id(0) == 0)
    def _():
        ref[...] = jnp.zeros_like(ref)

    ref[...] += jnp.broadcast_to(val, ref.shape)


def rmsnorm_fwd(name, h, gain):
    s = h.shape[0]

    def body(h_ref, g_ref, o_ref):
        o_ref[...] = _rms(h_ref[...], g_ref[...]).astype(o_ref.dtype)

    return _rows_call(name, body, [(h, "row"), (gain, "full")], [(D_MODEL, BF16, "row")], s, _row_tile(s))[0]


def rmsnorm_bwd(name, h, gain, d_hn, d_res):
    s = h.shape[0]

    def body(h_ref, g_ref, dy_ref, dr_ref, dh_ref, dg_ref):
        _, vjp = jax.vjp(_rms, h_ref[...], g_ref[...])
        dh, dg = vjp(dy_ref[...])
        dh_ref[...] = dh + dr_ref[...]
        _acc_rows(dg_ref, dg)

    return _rows_call(name, body, [(h, "row"), (gain, "full"), (d_hn, "row"), (d_res, "row")],
                      [(D_MODEL, F32, "row"), (D_MODEL, F32, "acc")], s, _row_tile(s, 1024))


def loss_and_grad(y, target):
    s = y.shape[0]
    tile = _row_tile(s)

    def body(y_ref, t_ref, dy_ref, acc_ref):
        diff = y_ref[...] - t_ref[...]
        dy_ref[...] = diff * (1.0 / D_MODEL)

        @pl.when(pl.program_id(0) == 0)
        def _():
            acc_ref[...] = jnp.zeros_like(acc_ref)

        acc_ref[...] += jnp.sum((diff * diff).reshape(tile // 8, 8, D_MODEL), axis=0)

    return _rows_call("loss_head", body, [(y, "row"), (target, "row")], [(D_MODEL, F32, "row"), (D_MODEL, F32, "acc")], s, tile)


def _rope_table(s):
    half = A_HEAD_DIM // 2
    inv_freq = ROPE_THETA ** (-jnp.arange(half, dtype=F32) / half)
    ang = jnp.arange(s).astype(F32)[:, None] * inv_freq[None, :]
    cos, sin = jnp.cos(ang), jnp.sin(ang)
    return jnp.concatenate([cos, cos, cos, cos, -sin, sin, -sin, sin], axis=1)


def _attn_geometry(s, dil):
    tile = min(ATT_TILE, s)
    halo = A_HALF * dil
    assert s % tile == 0 and tile % (ATT_T * dil) == 0 and tile % halo == 0
    return tile, halo, tile // (ATT_T * dil)


def _attn_in_specs(grp, s, tile, halo):
    hb, n_hb = tile // halo, s // halo
    cq, ck, cv = (24 * grp + 8 * t for t in range(3))

    def main(col, per_pair, width=LANES):
        return pl.BlockSpec((tile, width), lambda i, j: (i, col + per_pair * j))

    def prev(col, per_pair, width=LANES):
        return pl.BlockSpec((halo, width), lambda i, j: (jnp.maximum(i * hb - 1, 0), col + per_pair * j))

    def nxt(col, per_pair, width=LANES):
        return pl.BlockSpec((halo, width), lambda i, j: (jnp.minimum((i + 1) * hb, n_hb - 1), col + per_pair * j))

    return [main(cq, 1), prev(ck, 1), main(ck, 1), nxt(ck, 1), prev(cv, 1), main(cv, 1), nxt(cv, 1),
            prev(0, 0, 256), main(0, 0, 256), nxt(0, 0, 256)]


def _attn_valid(first_job, n_jobs, dil, tile_base, length):
    r = lax.broadcasted_iota(jnp.int32, (1, 2 * ATT_T, 1), 1)
    tq = jnp.where(r >= ATT_T, r - ATT_T, r)
    rel = lax.broadcasted_iota(jnp.int32, (1, 1, ATT_T + 2 * A_HALF), 2) - A_HALF
    job = first_job + lax.broadcasted_iota(jnp.int32, (n_jobs, 1, 1), 0)
    tk = tile_base + (job // dil) * ATT_T + rel
    return (jnp.abs(rel - tq) <= A_HALF) & (tk >= 0) & (tk < length)


def _jobs(dil, n_sub):
    return [u * ATT_T * dil + p for u in range(n_sub) for p in range(dil)]


def _gather_jobs(ref, starts, size, dil):
    return jnp.concatenate([ref[_rows(st, size, dil), :][None] for st in starts], axis=0)


def _rows(start, size, dil):
    return pl.ds(start, size, stride=dil) if dil > 1 else pl.ds(start, size)


def attn_fwd(name, qkv, tab, gq, gk, grp, dil):
    s = qkv.shape[0]
    tile, halo, n_sub = _attn_geometry(s, dil)
    length, per_tile = s // dil, tile // dil
    nk = ATT_T + 2 * A_HALF

    def body(q_ref, kp_ref, km_ref, kn_ref, vp_ref, vm_ref, vn_ref, tp_ref, tm_ref, tn_ref, gq_ref, gk_ref, o_ref, lse_ref,
             q_buf, k_buf, v_buf):
        i, pair = pl.program_id(0), pl.program_id(1)
        q_buf[...] = _qk_prep(q_ref[...], tm_ref[...], gq_ref[...])
        for ref, t_ref, lo, n in ((kp_ref, tp_ref, 0, halo), (km_ref, tm_ref, halo, tile), (kn_ref, tn_ref, halo + tile, halo)):
            k_buf[lo:lo + n, :] = _qk_prep(ref[...], t_ref[...], gk_ref[...])
        for ref, lo, n in ((vp_ref, 0, halo), (vm_ref, halo, tile), (vn_ref, halo + tile, halo)):
            v_buf[lo:lo + n, :] = ref[...]

        jobs = _jobs(dil, n_sub)
        for g0 in range(0, len(jobs), ATT_FWD_BATCH):
            starts = jobs[g0:g0 + ATT_FWD_BATCH]
            valid = _attn_valid(g0, len(starts), dil, i * per_tile, length)
            o, lse = _attn_job(_gather_jobs(q_buf, starts, ATT_T, dil), _gather_jobs(k_buf, starts, nk, dil),
                               _gather_jobs(v_buf, starts, nk, dil), valid)
            for n, st in enumerate(starts):
                o_ref[_rows(st, ATT_T, dil), :] = o[n]
                lse_ref[_rows(st, ATT_T, dil), :] = lse[n]

    full = lambda a: pl.BlockSpec(a.shape, lambda i, j: (0, 0))
    return pl.pallas_call(
        body, name=name, grid=(s // tile, D_MODEL // LANES),
        in_specs=_attn_in_specs(grp, s, tile, halo) + [full(gq), full(gk)],
        out_specs=[pl.BlockSpec((tile, LANES), lambda i, j: (i, j)), pl.BlockSpec((tile, LANES), lambda i, j: (i, j))],
        out_shape=[jax.ShapeDtypeStruct((s, D_MODEL), F32), jax.ShapeDtypeStruct((s, D_MODEL), F32)],
        scratch_shapes=[pltpu.VMEM((tile, LANES), F32), pltpu.VMEM((tile + 2 * halo, LANES), F32), pltpu.VMEM((tile + 2 * halo, LANES), F32)],
        compiler_params=_params(("parallel", "parallel")),
    )(qkv, qkv, qkv, qkv, qkv, qkv, qkv, tab, tab, tab, gq, gk)


def attn_bwd(name, qkv, tab, gq, gk, o, lse, d_o, d_lse, grp, dil, d_qkv):
    s = qkv.shape[0]
    tile, halo, n_sub = _attn_geometry(s, dil)
    length, per_tile = s // dil, tile // dil
    nt = s // tile
    nk = ATT_T + 2 * A_HALF
    pieces = ((0, halo), (halo, tile), (halo + tile, halo))

    def body(q_ref, kp_ref, km_ref, kn_ref, vp_ref, vm_ref, vn_ref, tp_ref, tm_ref, tn_ref, gq_ref, gk_ref, o_ref, l_ref, do_ref, dl_ref,
             _, dq_ref, dkp_ref, dkm_ref, dkn_ref, dvp_ref, dvm_ref, dvn_ref, dgq_ref, dgk_ref, q_buf, k_buf, v_buf, dq_buf, dk_buf, dv_buf):
        i, pair = pl.program_id(0), pl.program_id(1)
        k_refs, t_refs = (kp_ref, km_ref, kn_ref), (tp_ref, tm_ref, tn_ref)
        qn, q_vjp = jax.vjp(lambda x, g: _qk_prep(x, tm_ref[...], g), q_ref[...], gq_ref[...])
        q_buf[...] = qn
        k_vjps = []
        for ref, t_ref, (lo, n) in zip(k_refs, t_refs, pieces):
            kn, k_vjp = jax.vjp(lambda x, g: _qk_prep(x, t_ref[...], g), ref[...], gk_ref[...])
            k_buf[lo:lo + n, :] = kn
            k_vjps.append(k_vjp)
        for ref, (lo, n) in zip((vp_ref, vm_ref, vn_ref), pieces):
            v_buf[lo:lo + n, :] = ref[...]
        dk_buf[...] = jnp.zeros_like(dk_buf)
        dv_buf[...] = jnp.zeros_like(dv_buf)
        jobs = _jobs(dil, n_sub)
        for g0 in range(0, len(jobs), ATT_BWD_BATCH):
            starts = jobs[g0:g0 + ATT_BWD_BATCH]
            valid = _attn_valid(g0, len(starts), dil, i * per_tile, length)
            own = lambda ref: _gather_jobs(ref, starts, ATT_T, dil)
            dq, dk, dv = _attn_job_bwd(own(q_buf), _gather_jobs(k_buf, starts, nk, dil), _gather_jobs(v_buf, starts, nk, dil), valid,
                                       own(do_ref), own(o_ref), own(l_ref), own(dl_ref))
            for n, st in enumerate(starts):
                dq_buf[_rows(st, ATT_T, dil), :] = dq[n]
                dk_buf[_rows(st, nk, dil), :] += dk[n]
                dv_buf[_rows(st, nk, dil), :] += dv[n]
        dq, dgq = q_vjp(dq_buf[...])
        dq_ref[...] = dq.astype(dq_ref.dtype)
        dgk = jnp.zeros((1, LANES), F32)
        for k_vjp, out_ref, (lo, n) in zip(k_vjps, (dkp_ref, dkm_ref, dkn_ref), pieces):
            out_ref[...], dgk_piece = k_vjp(dk_buf[lo:lo + n, :])
            dgk = dgk + dgk_piece
        for out_ref, (lo, n) in zip((dvp_ref, dvm_ref, dvn_ref), pieces):
            out_ref[...] = dv_buf[lo:lo + n, :]

        @pl.when((i == 0) & (pair == 0))
        def _():
            dgq_ref[...] = jnp.zeros_like(dgq_ref)
            dgk_ref[...] = jnp.zeros_like(dgk_ref)

        dgq_ref[...] += jnp.broadcast_to(dgq + pltpu.roll(dgq, A_HEAD_DIM, 1), dgq_ref.shape)
        dgk_ref[...] += jnp.broadcast_to(dgk + pltpu.roll(dgk, A_HEAD_DIM, 1), dgk_ref.shape)

    full = lambda a: pl.BlockSpec(a.shape, lambda i, j: (0, 0))
    main_o = pl.BlockSpec((tile, LANES), lambda i, j: (i, j))
    edge_o = pl.BlockSpec((None, halo, LANES), lambda i, j: (i, 0, j))
    main_s = jax.ShapeDtypeStruct((s, D_MODEL), F32)
    edge_s = jax.ShapeDtypeStruct((nt, halo, D_MODEL), F32)
    acc_o = pl.BlockSpec((8, LANES), lambda i, j: (0, 0))
    acc_s = jax.ShapeDtypeStruct((8, LANES), F32)
    big = pltpu.VMEM((tile + 2 * halo, LANES), F32)
    own = pltpu.VMEM((tile, LANES), F32)
    outs = pl.pallas_call(
        body, name=name, grid=(nt, D_MODEL // LANES),
        in_specs=_attn_in_specs(grp, s, tile, halo) + [full(gq), full(gk), main_o, main_o, main_o, main_o, ANY],
        out_specs=[pl.BlockSpec((tile, LANES), lambda i, j: (i, 24 * grp + j)), edge_o, main_o, edge_o, edge_o, main_o, edge_o, acc_o, acc_o],
        out_shape=[jax.ShapeDtypeStruct(d_qkv.shape, d_qkv.dtype), edge_s, main_s, edge_s, edge_s, main_s, edge_s, acc_s, acc_s],
        scratch_shapes=[own, big, big, own, big, big],
        input_output_aliases={16: 0},
        compiler_params=_params(("arbitrary", "arbitrary")),
    )(qkv, qkv, qkv, qkv, qkv, qkv, qkv, tab, tab, tab, gq, gk, o, lse, d_o, d_lse, d_qkv)
    d_qkv, dkp, dkm, dkn, dvp, dvm, dvn, dgq, dgk = outs
    return d_qkv, (dkp, dkm, dkn), (dvp, dvm, dvn), dgq, dgk


def attn_combine(name, parts, s, dil, d_qkv, col):
    prev_part, main_part, next_part = parts
    tile, halo, _ = _attn_geometry(s, dil)
    nt = s // tile
    cols = D_MODEL // 2

    def body(m_ref, from_prev_ref, from_next_ref, _, o_ref):
        i = pl.program_id(0)
        o_ref[...] = m_ref[...].astype(o_ref.dtype)
        head = m_ref[0:halo, :] + jnp.where(i > 0, from_prev_ref[...], 0.0)
        o_ref[0:halo, :] = head.astype(o_ref.dtype)
        tail = m_ref[tile - halo:tile, :] + jnp.where(i < nt - 1, from_next_ref[...], 0.0)
        o_ref[tile - halo:tile, :] = tail.astype(o_ref.dtype)

    return pl.pallas_call(
        body, name=name, grid=(nt, D_MODEL // cols),
        in_specs=[pl.BlockSpec((tile, cols), lambda i, c: (i, c)),
                  pl.BlockSpec((None, halo, cols), lambda i, c: (jnp.maximum(i - 1, 0), 0, c)),
                  pl.BlockSpec((None, halo, cols), lambda i, c: (jnp.minimum(i + 1, nt - 1), 0, c)), ANY],
        out_specs=pl.BlockSpec((tile, cols), lambda i, c: (i, (D_MODEL // cols) * col + c)),
        out_shape=jax.ShapeDtypeStruct(d_qkv.shape, d_qkv.dtype), input_output_aliases={3: 0},
        compiler_params=_params(("parallel", "parallel")),
    )(main_part, next_part, prev_part, d_qkv)


def attn_merge_fwd(name, os_, lses):
    s = os_[0].shape[0]

    def body(o0, o1, o2, l0, l1, l2, out_ref):
        out_ref[...] = _merge_groups(o0[...], o1[...], o2[...], l0[...], l1[...], l2[...]).astype(out_ref.dtype)

    return _rows_call(name, body, [(a, "row") for a in (*os_, *lses)], [(D_MODEL, BF16, "row")], s, _row_tile(s, 256))[0]


def attn_merge_bwd(name, os_, lses, d_out):
    s = os_[0].shape[0]

    def body(o0, o1, o2, l0, l1, l2, d_ref, *outs):
        _, vjp = jax.vjp(_merge_groups, o0[...], o1[...], o2[...], l0[...], l1[...], l2[...])
        for ref, val in zip(outs, vjp(d_ref[...])):
            ref[...] = val

    return _rows_call(name, body, [(a, "row") for a in (*os_, *lses, d_out)],
                      [(D_MODEL, F32, "row")] * 6, s, _row_tile(s, 256))


def gla_gate_fwd(name, z, wg, bias):
    s = z.shape[0]

    def body(z_ref, w_ref, b_ref, o_ref):
        o_ref[...] = _gate(z_ref[...], w_ref[...], b_ref[...])

    return _rows_call(name, body, [(z, "row"), (wg, "full"), (bias, "full")], [(D_MODEL, F32, "row")], s, _row_tile(s))[0]


def gla_gate_bwd(name, z, wg, bias, d_la_f, d_la_b):
    s = z.shape[0]
    tile = _row_tile(s)

    def body(z_ref, w_ref, b_ref, df_ref, db_ref, dz_ref, dw_ref, dbias_ref):
        _, vjp = jax.vjp(_gate, z_ref[...], w_ref[...], b_ref[...])
        dz, dw, dbias = vjp(jnp.concatenate([df_ref[...], db_ref[...]], axis=1))
        dz_ref[...] = dz

        @pl.when(pl.program_id(0) == 0)
        def _():
            dw_ref[...] = jnp.zeros_like(dw_ref)

        dw_ref[...] += dw
        _acc_rows(dbias_ref, dbias)

    return pl.pallas_call(
        body, name=name, grid=(s // tile,),
        in_specs=[pl.BlockSpec((tile, LANES), lambda i: (i, 0)), pl.BlockSpec(wg.shape, lambda i: (0, 0)),
                  pl.BlockSpec(bias.shape, lambda i: (0, 0)), pl.BlockSpec((tile, 512), lambda i: (i, 0)),
                  pl.BlockSpec((tile, 512), lambda i: (i, 0))],
        out_specs=[pl.BlockSpec((tile, LANES), lambda i: (i, 0)), pl.BlockSpec(wg.shape, lambda i: (0, 0)),
                   pl.BlockSpec((8, D_MODEL), lambda i: (0, 0))],
        out_shape=[jax.ShapeDtypeStruct((s, LANES), F32), jax.ShapeDtypeStruct(wg.shape, F32), jax.ShapeDtypeStruct((8, D_MODEL), F32)],
        compiler_params=_params(("arbitrary",)),
    )(z, wg, bias, d_la_f, d_la_b)


def _gla_in_specs(tile, order, la_col0):
    t = order
    return [pl.BlockSpec((tile, B_KEY_DIM), lambda h, n: (t(n), h)),
            pl.BlockSpec((tile, B_KEY_DIM), lambda h, n: (t(n), B_HEADS + h)),
            pl.BlockSpec((tile, B_VAL_DIM), lambda h, n: (t(n), B_HEADS + h)),
            pl.BlockSpec((tile, B_KEY_DIM), lambda h, n: (t(n), la_col0 + h))]


def gla_fwd(name, proj, la, reverse):
    s = proj.shape[0]
    tile = _row_tile(s, GLA_TILE)
    nt = s // tile
    order = (lambda n: nt - 1 - n) if reverse else (lambda n: n)

    def body(q_ref, k_ref, v_ref, la_ref, o_ref, st_ref, st_scr):
        @pl.when(pl.program_id(1) == 0)
        def _():
            st_scr[...] = jnp.zeros_like(st_scr)

        st_ref[...] = st_scr[...]
        o, st = _gla_tile(q_ref[...], k_ref[...], v_ref[...], la_ref[...], st_scr[...], reverse)
        o_ref[...] = o
        st_scr[...] = st

    return pl.pallas_call(
        body, name=name, grid=(B_HEADS, nt), in_specs=_gla_in_specs(tile, order, B_HEADS if reverse else 0),
        out_specs=[pl.BlockSpec((tile, B_VAL_DIM), lambda h, n: (order(n), h)),
                   pl.BlockSpec((None, None, B_VAL_DIM, B_KEY_DIM), lambda h, n: (h, order(n), 0, 0))],
        out_shape=[jax.ShapeDtypeStruct((s, D_MODEL), F32), jax.ShapeDtypeStruct((B_HEADS, nt, B_VAL_DIM, B_KEY_DIM), F32)],
        scratch_shapes=[pltpu.VMEM((B_VAL_DIM, B_KEY_DIM), F32)],
        compiler_params=_params(("parallel", "arbitrary")),
    )(proj, proj, proj, la)


def gla_bwd(name, proj, la, states, d_o, reverse, prev=None):
    s = proj.shape[0]
    tile = _row_tile(s, GLA_TILE)
    nt = s // tile
    order = (lambda n: n) if reverse else (lambda n: nt - 1 - n)

    def body(*refs):
        q_ref, k_ref, v_ref, la_ref, st_ref, do_ref = refs[:6]
        rest = refs[6:]
        prev_refs = rest[:3] if prev is not None else None
        dq_ref, dk_ref, dv_ref, dla_ref, dst_scr = rest[3:] if prev is not None else rest

        @pl.when(pl.program_id(1) == 0)
        def _():
            dst_scr[...] = jnp.zeros_like(dst_scr)

        _, vjp = jax.vjp(functools.partial(_gla_tile, reverse=reverse), q_ref[...], k_ref[...], v_ref[...], la_ref[...], st_ref[...])
        dq, dk, dv, dla, dst = vjp((do_ref[...], dst_scr[...]))
        if prev_refs is not None:
            dq, dk, dv = dq + prev_refs[0][...], dk + prev_refs[1][...], dv + prev_refs[2][...]
        dq_ref[...], dk_ref[...], dv_ref[...], dla_ref[...] = dq, dk, dv, dla
        dst_scr[...] = dst

    key_spec = pl.BlockSpec((tile, B_KEY_DIM), lambda h, n: (order(n), h))
    val_spec = pl.BlockSpec((tile, B_VAL_DIM), lambda h, n: (order(n), h))
    in_specs = _gla_in_specs(tile, order, B_HEADS if reverse else 0) + [
        pl.BlockSpec((None, None, B_VAL_DIM, B_KEY_DIM), lambda h, n: (h, order(n), 0, 0)), val_spec]
    ins = [proj, proj, proj, la, states, d_o]
    if prev is not None:
        in_specs += [key_spec, key_spec, val_spec]
        ins += list(prev)
    return pl.pallas_call(
        body, name=name, grid=(B_HEADS, nt), in_specs=in_specs,
        out_specs=[key_spec, key_spec, val_spec, key_spec],
        out_shape=[jax.ShapeDtypeStruct((s, 512), F32), jax.ShapeDtypeStruct((s, 512), F32),
                   jax.ShapeDtypeStruct((s, D_MODEL), F32), jax.ShapeDtypeStruct((s, 512), F32)],
        scratch_shapes=[pltpu.VMEM((B_VAL_DIM, B_KEY_DIM), F32)],
        compiler_params=_params(("parallel", "arbitrary")),
    )(*ins)


def _r_spec(tile):
    return pl.BlockSpec((tile, D_MODEL), lambda i: (i, 2))


def gla_post_fwd(name, o_f, o_b, proj, gain):
    s = o_f.shape[0]
    tile = _row_tile(s)

    def body(of_ref, ob_ref, r_ref, g_ref, out_ref):
        out_ref[...] = _gla_post(of_ref[...], ob_ref[...], r_ref[...], g_ref[...]).astype(out_ref.dtype)

    row = pl.BlockSpec((tile, D_MODEL), lambda i: (i, 0))
    return pl.pallas_call(body, name=name, grid=(s // tile,),
                          in_specs=[row, row, _r_spec(tile), pl.BlockSpec(gain.shape, lambda i: (0, 0))], out_specs=row,
                          out_shape=jax.ShapeDtypeStruct((s, D_MODEL), BF16), compiler_params=_params(("parallel",)))(o_f, o_b, proj, gain)


def gla_post_bwd(name, o_f, o_b, proj, gain, d_out):
    s = o_f.shape[0]
    tile = _row_tile(s)

    def body(of_ref, ob_ref, r_ref, g_ref, d_ref, do_ref, dr_ref, dg_ref):
        _, vjp = jax.vjp(_gla_post, of_ref[...], ob_ref[...], r_ref[...], g_ref[...])
        d_of, _, dr, dg = vjp(d_ref[...])
        do_ref[...] = d_of
        dr_ref[...] = dr
        _acc_rows(dg_ref, dg)

    row = pl.BlockSpec((tile, D_MODEL), lambda i: (i, 0))
    return pl.pallas_call(
        body, name=name, grid=(s // tile,),
        in_specs=[row, row, _r_spec(tile), pl.BlockSpec(gain.shape, lambda i: (0, 0)), row],
        out_specs=[row, row, pl.BlockSpec((8, D_MODEL), lambda i: (0, 0))],
        out_shape=[jax.ShapeDtypeStruct((s, D_MODEL), F32), jax.ShapeDtypeStruct((s, D_MODEL), F32), jax.ShapeDtypeStruct((8, D_MODEL), F32)],
        compiler_params=_params(("arbitrary",)))(o_f, o_b, proj, gain, d_out)


def _hid(tile, where):
    return pl.BlockSpec((None, tile, FFN_BLK), where)


def _pair(tile, where):
    return pl.BlockSpec((2, None, tile, FFN_BLK), where)


def _w_gu_spec(layer, where_j):
    return pl.BlockSpec((2, None, None, D_MODEL, FFN_BLK), lambda *g: (0, where_j(*g), layer, 0, 0))


def ffn_fwd(name, h_mid, hn2, w_gu, w_down, layer, next_gain=None):
    s = hn2.shape[0]
    tm = _row_tile(s, 1024)
    nt = s // tm

    def gu_body(x_ref, w_ref, gu_ref, act_ref):
        x = x_ref[...]
        g = _dot(x, w_ref[0], NN)
        u = _dot(x, w_ref[1], NN)
        gu_ref[0] = g
        gu_ref[1] = u
        act_ref[...] = _swiglu_act(g, u).astype(act_ref.dtype)

    gu, act = pl.pallas_call(
        gu_body, name=name + "_gu", grid=(4, nt),
        in_specs=[pl.BlockSpec((tm, D_MODEL), lambda j, i: (i, 0)), _w_gu_spec(layer, lambda j, i: j)],
        out_specs=[_pair(tm, lambda j, i: (0, j, i, 0)), _hid(tm, lambda j, i: (j, i, 0))],
        out_shape=[jax.ShapeDtypeStruct((2, 4, s, FFN_BLK), F32), jax.ShapeDtypeStruct((4, s, FFN_BLK), BF16)],
        compiler_params=_params(("parallel", "parallel")))(hn2, w_gu)
    def down_body(*refs):
        act_ref, w_ref, res_ref = refs[:3]
        g_ref = refs[3] if next_gain is not None else None
        o_ref = refs[4] if next_gain is not None else refs[3]
        out = res_ref[...]
        for j in range(4):
            out = out + _dot(act_ref[j], w_ref[j], NN)
        o_ref[...] = out
        if next_gain is not None:
            refs[5][...] = _rms(out, g_ref[...]).astype(BF16)

    row = pl.BlockSpec((tm, D_MODEL), lambda i: (i, 0))
    in_specs = [pl.BlockSpec((4, tm, FFN_BLK), lambda i: (0, i, 0)), pl.BlockSpec((4, FFN_BLK, D_MODEL), lambda i: (0, 0, 0)), row]
    ins = [act, w_down, h_mid]
    out_specs, out_shape = row, jax.ShapeDtypeStruct((s, D_MODEL), F32)
    if next_gain is not None:
        in_specs.append(pl.BlockSpec(next_gain.shape, lambda i: (0, 0)))
        ins.append(next_gain)
        out_specs, out_shape = [row, row], [out_shape, jax.ShapeDtypeStruct((s, D_MODEL), BF16)]
    h_next = pl.pallas_call(down_body, name=name + "_down", grid=(nt,), in_specs=in_specs, out_specs=out_specs,
                            out_shape=out_shape, compiler_params=_params(("parallel",)))(*ins)
    return h_next, gu, act


def ffn_bwd(name, dh_next, hn2, gu, act, w_gu, w_down, layer):
    s = hn2.shape[0]
    tm = _row_tile(s, 1024)
    nt = s // tm
    tw = _row_tile(s, 2048)
    ntw = s // tw
    d_wd = _matmul(name + "_dwd", act, dh_next, dims=TN, grid=(4, ntw), red_axis=1,
                   a_spec=_hid(tw, lambda j, i: (j, i, 0)), b_spec=pl.BlockSpec((tw, D_MODEL), lambda j, i: (i, 0)),
                   o_spec=pl.BlockSpec((None, FFN_BLK, D_MODEL), lambda j, i: (j, 0, 0)),
                   out_shape=jax.ShapeDtypeStruct((4, FFN_BLK, D_MODEL), BF16), acc_shape=(FFN_BLK, D_MODEL))

    def dgu_body(dy_ref, wd_ref, gu_ref, dgu_ref):
        d_act = _dot(dy_ref[...], wd_ref[...], NT)
        g, u = gu_ref[0], gu_ref[1]
        sg = _sigmoid(g)
        silu = g * sg
        dgu_ref[0] = (d_act * u * (sg + silu * (1.0 - sg))).astype(dgu_ref.dtype)
        dgu_ref[1] = (d_act * silu).astype(dgu_ref.dtype)

    d_gu = pl.pallas_call(
        dgu_body, name=name + "_dgu", grid=(nt, 4),
        in_specs=[pl.BlockSpec((tm, D_MODEL), lambda i, j: (i, 0)), pl.BlockSpec((None, FFN_BLK, D_MODEL), lambda i, j: (j, 0, 0)),
                  _pair(tm, lambda i, j: (0, j, i, 0))],
        out_specs=_pair(tm, lambda i, j: (0, j, i, 0)), out_shape=jax.ShapeDtypeStruct((2, 4, s, FFN_BLK), BF16),
        compiler_params=_params(("parallel", "parallel")))(dh_next, w_down, gu)

    tx = _row_tile(s, 1024)

    def dx_body(d_ref, w_ref, o_ref):
        out = None
        for t in range(2):
            for j in range(4):
                prod = _dot(d_ref[t, j], w_ref[t, j], NT)
                out = prod if out is None else out + prod
        o_ref[...] = out

    d_hn2 = pl.pallas_call(
        dx_body, name=name + "_dx", grid=(s // tx,),
        in_specs=[pl.BlockSpec((2, 4, tx, FFN_BLK), lambda i: (0, 0, i, 0)),
                  pl.BlockSpec((2, 4, None, D_MODEL, FFN_BLK), lambda i: (0, 0, layer, 0, 0), pipeline_mode=pl.Buffered(1))],
        out_specs=pl.BlockSpec((tx, D_MODEL), lambda i: (i, 0)), out_shape=jax.ShapeDtypeStruct((s, D_MODEL), F32),
        compiler_params=_params(("parallel",)))(d_gu, w_gu)

    def dw_body(x_ref, d_ref, o_ref, acc_ref):
        x = x_ref[...]
        k = pl.program_id(1)
        for t in range(2):
            prod = _dot(x, d_ref[t], TN)

            @pl.when(k == 0)
            def _():
                acc_ref[t] = prod

            @pl.when(k > 0)
            def _():
                acc_ref[t] += prod

        @pl.when(k == ntw - 1)
        def _():
            o_ref[...] = acc_ref[...].astype(o_ref.dtype)

    d_wgu = pl.pallas_call(
        dw_body, name=name + "_dwgu", grid=(4, ntw),
        in_specs=[pl.BlockSpec((tw, D_MODEL), lambda j, i: (i, 0)), _pair(tw, lambda j, i: (0, j, i, 0))],
        out_specs=pl.BlockSpec((2, None, D_MODEL, FFN_BLK), lambda j, i: (0, j, 0, 0)),
        out_shape=jax.ShapeDtypeStruct((2, 4, D_MODEL, FFN_BLK), BF16),
        scratch_shapes=[pltpu.VMEM((2, D_MODEL, FFN_BLK), F32)],
        compiler_params=_params(("parallel", "arbitrary")))(hn2, d_gu)
    return d_hn2, d_wgu, d_wd


def _my_place():
    return lax.axis_index("x"), lax.axis_index("y"), lax.axis_index("c")


def _flip(place, k):
    x, y, c = place
    return (1 - x if k & 4 else x, 1 - y if k & 2 else y, 1 - c if k & 1 else c)


def _index(place):
    return 4 * place[0] + 2 * place[1] + place[2]


def all_gather(arrs):
    n = len(arrs)

    def body(*refs):
        ins, outs = refs[:n], refs[n:2 * n]
        send_sems, recv_sems, local_sems = refs[2 * n:]
        me = _my_place()
        sibling = _flip(me, 1)
        chips = (4, 2, 6)

        def copy(a, k, block, to, src=None):
            dst = outs[a].at[_index(block)]
            return pltpu.make_async_remote_copy(src_ref=dst if src is None else src, dst_ref=dst, send_sem=send_sems.at[a, k],
                                                recv_sem=recv_sems.at[a, k], device_id=to, device_id_type=MESH)

        started = []
        for a in range(n):
            mine = pltpu.make_async_copy(ins[a], outs[a].at[_index(me)], local_sems.at[a])
            mine.start()
            started.append(mine)
        first = []
        for a in range(n):
            first.append(copy(a, 0, me, sibling, src=ins[a]))
            first += [copy(a, 1 + j, me, _flip(me, k), src=ins[a]) for j, k in enumerate(chips)]
        for cp in first:
            cp.start()
        passed = []
        for a in range(n):
            for j, k in enumerate(chips):
                copy(a, 1 + j, _flip(me, k), me).wait_recv()
                fwd = copy(a, 4 + j, _flip(me, k), sibling)
                fwd.start()
                passed.append(fwd)
        for a in range(n):
            copy(a, 0, sibling, me).wait_recv()
            for j, k in enumerate(chips):
                copy(a, 4 + j, _flip(sibling, k), me).wait_recv()
        for cp in first + passed:
            cp.wait_send()
        for cp in started:
            cp.wait()

    return pl.pallas_call(
        body, name="all_gather_weights", in_specs=[ANY] * n, out_specs=[ANY] * n,
        out_shape=[jax.ShapeDtypeStruct((N_DEV,) + a.shape, a.dtype) for a in arrs],
        scratch_shapes=[pltpu.SemaphoreType.DMA((n, 7)), pltpu.SemaphoreType.DMA((n, 7)), pltpu.SemaphoreType.DMA((n,))],
    )(*arrs)


def exchange_partials(arrs):
    n = len(arrs)

    def body(*refs):
        ins, outs = refs[:n], refs[n:2 * n]
        send_sems, recv_sems, local_sems = refs[2 * n:]
        me = _my_place()
        local = []
        for a in range(n):
            cp = pltpu.make_async_copy(ins[a].at[_index(me)], outs[a].at[_index(me)], local_sems.at[a])
            cp.start()
            local.append(cp)

        def copy(a, k, src_block, dst_block):
            return pltpu.make_async_remote_copy(src_ref=ins[a].at[_index(src_block)], dst_ref=outs[a].at[_index(dst_block)],
                                                send_sem=send_sems.at[a, k - 1], recv_sem=recv_sems.at[a, k - 1],
                                                device_id=_flip(me, k), device_id_type=MESH)

        sent = []
        for a in range(n):
            for k in range(1, N_DEV):
                cp = copy(a, k, _flip(me, k), me)
                cp.start()
                sent.append(cp)
        for a in range(n):
            for k in range(1, N_DEV):
                copy(a, k, me, _flip(me, k)).wait_recv()
        for cp in sent:
            cp.wait_send()
        for cp in local:
            cp.wait()

    return pl.pallas_call(
        body, name="exchange_weight_grads", in_specs=[ANY] * n, out_specs=[ANY] * n,
        out_shape=[jax.ShapeDtypeStruct(a.shape, a.dtype) for a in arrs],
        scratch_shapes=[pltpu.SemaphoreType.DMA((n, 7)), pltpu.SemaphoreType.DMA((n, 7)), pltpu.SemaphoreType.DMA((n,))],
    )(*arrs)


def adamw_shard(name, parts, w, m, v, layer, tile):
    rows, cols = parts.shape[1:]
    assert rows % tile == 0
    off = layer * (rows // tile)

    def body(p_ref, w_ref, m_ref, v_ref, g_ref, d_ref, nm_ref, nv_ref):
        g = p_ref[0].astype(F32)
        for src in range(1, N_DEV):
            g = g + p_ref[src].astype(F32)
        g_ref[...] = g
        d_ref[...], nm_ref[...], nv_ref[...] = _adamw(w_ref[...], g, m_ref[...], v_ref[...])

    src_row = pl.BlockSpec((tile, cols), lambda i: (off + i, 0))
    row = pl.BlockSpec((tile, cols), lambda i: (i, 0))
    shape = jax.ShapeDtypeStruct((rows, cols), F32)
    return pl.pallas_call(body, name=name, grid=(rows // tile,),
                          in_specs=[pl.BlockSpec((N_DEV, tile, cols), lambda i: (0, i, 0)), src_row, src_row, src_row],
                          out_specs=[row] * 4, out_shape=[shape] * 4, compiler_params=_params(("parallel",)))(parts, w, m, v)


def allreduce_adamw_replicated(partial, w, m, v, n_loss_rows):
    rows = partial.shape[0]

    def body(p_ref, w_ref, m_ref, v_ref, g_ref, d_ref, nm_ref, nv_ref, loss_ref, recv_ref, send_sems, recv_sems):
        me = _my_place()
        recv_ref[_index(me)] = p_ref[...]
        copies = []
        for k in range(1, N_DEV):
            peer = _flip(me, k)
            cp = pltpu.make_async_remote_copy(src_ref=p_ref, dst_ref=recv_ref.at[_index(me)], send_sem=send_sems.at[k - 1],
                                              recv_sem=recv_sems.at[k - 1], device_id=peer, device_id_type=MESH)
            cp.start()
            copies.append((cp, peer))
        for k, (cp, peer) in enumerate(copies):
            pltpu.make_async_remote_copy(src_ref=p_ref, dst_ref=recv_ref.at[_index(peer)], send_sem=send_sems.at[k],
                                         recv_sem=recv_sems.at[k], device_id=peer, device_id_type=MESH).wait_recv()
        for cp, _ in copies:
            cp.wait_send()
        g = recv_ref[0]
        for src in range(1, N_DEV):
            g = g + recv_ref[src]
        g_ref[...] = g
        d_ref[...], nm_ref[...], nv_ref[...] = _adamw(w_ref[...], g, m_ref[...], v_ref[...])
        loss = (0.5 / D_MODEL) * jnp.sum(g[rows - n_loss_rows:, :])
        loss_ref[...] = jnp.full(loss_ref.shape, loss, F32)

    shape = jax.ShapeDtypeStruct((rows, LANES), F32)
    return pl.pallas_call(
        body, name="allreduce_adamw_replicated", in_specs=[VMEM_SPEC] * 4, out_specs=[VMEM_SPEC] * 5,
        out_shape=[shape] * 4 + [jax.ShapeDtypeStruct((8, LANES), F32)],
        scratch_shapes=[pltpu.VMEM((N_DEV, rows, LANES), F32), pltpu.SemaphoreType.DMA((7,)), pltpu.SemaphoreType.DMA((7,))],
    )(partial, w, m, v)


def _pack_rows(flat, cols):
    n = flat.shape[-1]
    rows = -(-n // cols)
    rows = -(-rows // 48) * 48
    flat = jnp.pad(flat, [(0, 0)] * (flat.ndim - 1) + [(0, rows * cols - n)])
    return flat.reshape(flat.shape[:-1] + (rows, cols))


def _unpack(flat, shapes):
    out, off = [], 0
    for shp in shapes:
        n = 1
        for d in shp:
            n *= d
        out.append(flat[off:off + n].reshape(shp))
        off += n
    return out


def _to_dev_cols(a):
    w = a.shape[-1] // N_DEV
    return jnp.moveaxis(a.reshape(a.shape[:-1] + (N_DEV, w)), -2, 0)


def _from_dev_cols(a):
    a = jnp.moveaxis(a, 0, -2)
    return a.reshape(a.shape[:-2] + (a.shape[-2] * a.shape[-1],))


def kernel(x, attn_norm, ffn_norm, a_w_in, a_q_norm, a_k_norm, a_w_out, b_w_in, b_w_gate_f, b_gate_bias_f, b_w_gate_b, b_gate_bias_b, b_out_norm, b_w_out, ffn_w_gate_up, ffn_w_down, loss_target, m_attn_norm, m_ffn_norm, m_a_w_in, m_a_q_norm, m_a_k_norm, m_a_w_out, m_b_w_in, m_b_w_gate_f, m_b_gate_bias_f, m_b_w_gate_b, m_b_gate_bias_b, m_b_out_norm, m_b_w_out, m_ffn_w_gate_up, m_ffn_w_down, v_attn_norm, v_ffn_norm, v_a_w_in, v_a_q_norm, v_a_k_norm, v_a_w_out, v_b_w_in, v_b_w_gate_f, v_b_gate_bias_f, v_b_w_gate_b, v_b_gate_bias_b, v_b_out_norm, v_b_w_out, v_ffn_w_gate_up, v_ffn_w_down):
    seq = x.shape[1]
    depth = attn_norm.shape[0]
    h = x.reshape(seq, D_MODEL)
    target = loss_target.reshape(seq, D_MODEL)
    n_a, n_b = a_w_in.shape[0], b_w_in.shape[0]

    small = jnp.concatenate([t.reshape(-1) for t in (b_w_gate_f, b_gate_bias_f, b_w_gate_b, b_gate_bias_b, b_out_norm)])
    small = _pack_rows(small, LANES)
    g_a_in, g_a_out, g_b_in, g_b_out, g_gu, g_down, g_small = all_gather(
        [a_w_in.astype(BF16), a_w_out.astype(BF16), b_w_in.astype(BF16), b_w_out.astype(BF16),
         ffn_w_gate_up.astype(BF16), ffn_w_down.astype(BF16), small])
    w_a_in = _from_dev_cols(g_a_in)
    w_a_out = jnp.moveaxis(g_a_out, 0, 1).reshape(n_a, D_MODEL, D_MODEL)
    w_b_in = _from_dev_cols(g_b_in)
    w_b_out = jnp.moveaxis(g_b_out, 0, 1).reshape(n_b, D_MODEL, D_MODEL)
    w_down = jnp.moveaxis(g_down, 0, 1).reshape(depth, 4, FFN_BLK, D_MODEL)
    w_gu = g_gu.reshape(2, 4, depth, D_MODEL, FFN_BLK)
    small_shapes = [t.shape for t in (b_w_gate_f, b_gate_bias_f, b_w_gate_b, b_gate_bias_b, b_out_norm)]
    per_dev = [_unpack(g_small[d].reshape(-1), small_shapes) for d in range(N_DEV)]
    wgf, bgf, wgb, bgb, onorm = [_from_dev_cols(jnp.stack([per_dev[d][t] for d in range(N_DEV)])) for t in range(5)]
    w_gate = jnp.zeros((n_b, LANES, D_MODEL), F32)
    w_gate = w_gate.at[:, 0:16, 0:512].set(wgf).at[:, 16:32, 512:1024].set(wgb)
    gate_bias = jnp.concatenate([bgf, bgb], axis=1).reshape(n_b, 1, D_MODEL)
    out_gain = onorm.reshape(n_b, 1, D_MODEL)
    w_b_main = w_b_in[:, :, :3072]
    w_b_z = jnp.pad(w_b_in[:, :, 3072:], ((0, 0), (0, 0), (0, LANES - 32)))

    dh, sq_err, grads = _forward_backward(h, target, attn_norm, ffn_norm, a_q_norm, a_k_norm, w_a_in, w_a_out, w_b_main, w_b_z,
                                          w_gate, gate_bias, out_gain, w_b_out, w_gu, w_down)
    sharded_w = (a_w_in, a_w_out, b_w_in, b_w_gate_f, b_gate_bias_f, b_w_gate_b, b_gate_bias_b, b_out_norm, b_w_out, ffn_w_gate_up, ffn_w_down)
    sharded_m = (m_a_w_in, m_a_w_out, m_b_w_in, m_b_w_gate_f, m_b_gate_bias_f, m_b_w_gate_b, m_b_gate_bias_b, m_b_out_norm, m_b_w_out, m_ffn_w_gate_up, m_ffn_w_down)
    sharded_v = (v_a_w_in, v_a_w_out, v_b_w_in, v_b_w_gate_f, v_b_gate_bias_f, v_b_w_gate_b, v_b_gate_bias_b, v_b_out_norm, v_b_w_out, v_ffn_w_gate_up, v_ffn_w_down)
    rep_w = (attn_norm, ffn_norm, a_q_norm, a_k_norm)
    rep_m = (m_attn_norm, m_ffn_norm, m_a_q_norm, m_a_k_norm)
    rep_v = (v_attn_norm, v_ffn_norm, v_a_q_norm, v_a_k_norm)
    loss, outs = _reduce_and_update(grads, sq_err, sharded_w, sharded_m, sharded_v, rep_w, rep_m, rep_v)
    return (loss, dh.reshape(x.shape), *outs)


def _forward_backward(h, target, attn_norm, ffn_norm, a_q_norm, a_k_norm, w_a_in, w_a_out, w_b_main, w_b_z, w_gate, gate_bias,
                      out_gain, w_b_out, w_gu, w_down):
    seq = h.shape[0]
    depth = attn_norm.shape[0]
    n_a, n_b = w_a_in.shape[0], w_b_main.shape[0]
    tab = _rope_table(seq)
    pair_gain = lambda g: jnp.concatenate([g, g]).reshape(1, LANES)

    saved = []
    for i in range(depth):
        j = i // 2
        nm = f"l{i}"
        if i == 0:
            hn = rmsnorm_fwd(nm + "_norm1", h, attn_norm[i].reshape(1, D_MODEL))
        if i % 2 == 0:
            qkv = mm_nn(nm + "_qkv", hn, w_a_in[j])
            os_, lses = [], []
            for g, (_, dil) in enumerate(A_GROUPS):
                o, lse = attn_fwd(f"{nm}_attn{g}", qkv, tab, pair_gain(a_q_norm[j, g]), pair_gain(a_k_norm[j, g]), g, dil)
                os_.append(o)
                lses.append(lse)
            mixed = attn_merge_fwd(nm + "_merge", os_, lses)
            h_mid, hn2 = mm_nn(nm + "_out", mixed, w_a_out[j], res=h, norm_gain=ffn_norm[i].reshape(1, D_MODEL))
            mix_saved = (qkv, os_, lses, mixed)
        else:
            proj = mm_nn(nm + "_proj", hn, w_b_main[j])
            z = mm_nn(nm + "_z", hn, w_b_z[j])
            la = gla_gate_fwd(nm + "_gate", z, w_gate[j], gate_bias[j])
            o_f, st_f = gla_fwd(nm + "_gla_f", proj, la, False)
            o_b, st_b = gla_fwd(nm + "_gla_b", proj, la, True)
            mixed = gla_post_fwd(nm + "_post", o_f, o_b, proj, out_gain[j])
            h_mid, hn2 = mm_nn(nm + "_out", mixed, w_b_out[j], res=h, norm_gain=ffn_norm[i].reshape(1, D_MODEL))
            mix_saved = (proj, z, la, o_f, st_f, o_b, st_b, mixed)
        next_gain = attn_norm[i + 1].reshape(1, D_MODEL) if i + 1 < depth else None
        h_next, gu, act = ffn_fwd(nm + "_ffn", h_mid, hn2, w_gu, w_down[i], i, next_gain)
        saved.append((h, hn, mix_saved, h_mid, hn2, gu, act))
        if next_gain is not None:
            h_next, hn = h_next
        h = h_next

    dh, sq_err = loss_and_grad(h, target)

    g_attn_norm, g_ffn_norm = [None] * depth, [None] * depth
    g_a_w_in, g_a_w_out, g_a_q, g_a_k = [None] * n_a, [None] * n_a, [None] * n_a, [None] * n_a
    g_b_w_in, g_b_w_out, g_w_gate, g_gate_bias, g_out_gain = ([None] * n_b for _ in range(5))
    g_w_gu, g_w_down = [None] * depth, [None] * depth
    for i in reversed(range(depth)):
        j = i // 2
        nm = f"l{i}b"
        h_in, hn, mix_saved, h_mid, hn2, gu, act = saved[i]
        d_hn2, g_w_gu[i], g_w_down[i] = ffn_bwd(nm + "_ffn", dh, hn2, gu, act, w_gu, w_down[i], i)
        dh_mid, g_ffn_norm[i] = rmsnorm_bwd(nm + "_norm2", h_mid, ffn_norm[i].reshape(1, D_MODEL), d_hn2, dh)
        if i % 2 == 0:
            qkv, os_, lses, mixed = mix_saved
            d_mixed = mm_nt(nm + "_dmixed", dh_mid, w_a_out[j])
            g_a_w_out[j] = mm_tn(nm + "_dwout", mixed, dh_mid, by_block=True)
            d_parts = attn_merge_bwd(nm + "_merge", os_, lses, d_mixed)
            gq_l, gk_l = [], []
            d_qkv = lax.empty(qkv.shape, BF16)
            for g, (_, dil) in enumerate(A_GROUPS):
                d_qkv, dk_parts, dv_parts, dgq, dgk = attn_bwd(f"{nm}_attn{g}", qkv, tab, pair_gain(a_q_norm[j, g]),
                                                               pair_gain(a_k_norm[j, g]), os_[g], lses[g], d_parts[g], d_parts[3 + g], g, dil, d_qkv)
                d_qkv = attn_combine(f"{nm}_dk{g}", dk_parts, seq, dil, d_qkv, 3 * g + 1)
                d_qkv = attn_combine(f"{nm}_dv{g}", dv_parts, seq, dil, d_qkv, 3 * g + 2)
                gq_l.append(dgq[0, :A_HEAD_DIM])
                gk_l.append(dgk[0, :A_HEAD_DIM])
            g_a_q[j], g_a_k[j] = jnp.stack(gq_l), jnp.stack(gk_l)
            d_hn = mm_nt(nm + "_dhn", d_qkv, w_a_in[j])
            g_a_w_in[j] = mm_tn(nm + "_dwin", hn, d_qkv, tn=D_MODEL * 9 // N_DEV, by_block=True)
        else:
            proj, z, la, o_f, st_f, o_b, st_b, mixed = mix_saved
            d_mixed = mm_nt(nm + "_dmixed", dh_mid, w_b_out[j])
            g_b_w_out[j] = mm_tn(nm + "_dwout", mixed, dh_mid, by_block=True)
            d_o, d_r, dgain = gla_post_bwd(nm + "_post", o_f, o_b, proj, out_gain[j], d_mixed)
            g_out_gain[j] = dgain[0]
            dq, dk, dv, dla_f = gla_bwd(nm + "_gla_f", proj, la, st_f, d_o, False)
            dq, dk, dv, dla_b = gla_bwd(nm + "_gla_b", proj, la, st_b, d_o, True, prev=(dq, dk, dv))
            d_z, g_w_gate[j], dbias = gla_gate_bwd(nm + "_gate", z, w_gate[j], gate_bias[j], dla_f, dla_b)
            g_gate_bias[j] = dbias[0]
            d_proj = jnp.concatenate([dq, dk, dv, d_r], axis=1)
            d_hn = mm_nt(nm + "_dhn_z", d_z, w_b_z[j])
            d_hn = mm_nt(nm + "_dhn", d_proj, w_b_main[j], res=d_hn)
            g_b_w_in[j] = jnp.concatenate([mm_tn(nm + "_dwin", hn, d_proj), mm_tn(nm + "_dwz", hn, d_z)[:, :32]], axis=1)
        dh, g_attn_norm[i] = rmsnorm_bwd(nm + "_norm1", h_in, attn_norm[i].reshape(1, D_MODEL), d_hn, dh_mid)
    return dh, sq_err, (g_attn_norm, g_ffn_norm, g_a_w_in, g_a_w_out, g_a_q, g_a_k, g_b_w_in, g_b_w_out, g_w_gate, g_gate_bias,
                        g_out_gain, g_w_gu, g_w_down)


def _reduce_and_update(grads, sq_err, sharded_w, sharded_m, sharded_v, rep_w, rep_m, rep_v):
    (g_attn_norm, g_ffn_norm, g_a_w_in, g_a_w_out, g_a_q, g_a_k, g_b_w_in, g_b_w_out, g_w_gate, g_gate_bias, g_out_gain,
     g_w_gu, g_w_down) = grads
    depth, n_a, n_b = len(g_w_gu), len(g_a_w_in), len(g_b_w_in)

    g_w_gate = jnp.stack(g_w_gate)
    g_gate_bias = jnp.stack(g_gate_bias)
    small_parts = [_to_dev_cols(g_w_gate[:, 0:16, 0:512]), _to_dev_cols(g_gate_bias[:, 0:512]),
                   _to_dev_cols(g_w_gate[:, 16:32, 512:1024]), _to_dev_cols(g_gate_bias[:, 512:1024]),
                   _to_dev_cols(jnp.stack(g_out_gain).reshape(n_b, B_HEADS, B_VAL_DIM))]
    small_part = _pack_rows(jnp.concatenate([t.reshape(N_DEV, -1) for t in small_parts], axis=1), LANES).astype(BF16)
    families = [
        (0, g_a_w_in, 256),
        (1, [t.reshape(N_DEV, -1, D_MODEL) for t in g_a_w_out], 128),
        (2, [_to_dev_cols(t).astype(BF16) for t in g_b_w_in], 256),
        (8, [t.reshape(N_DEV, -1, D_MODEL) for t in g_b_w_out], 128),
        (9, [t.reshape(N_DEV, D_MODEL, FFN_BLK) for t in g_w_gu], 256),
        (10, [t.reshape(N_DEV, -1, D_MODEL) for t in g_w_down], 176),
    ]
    flat_parts = [p for _, parts, _ in families for p in parts] + [small_part]
    received = exchange_partials(flat_parts)
    sh_out = [None] * len(sharded_w)
    pos = 0
    for fam, parts, tile in families:
        w = sharded_w[fam]
        two_d = lambda t: t.reshape(-1, t.shape[-1])
        per_layer = []
        for layer in range(len(parts)):
            per_layer.append(adamw_shard(f"adamw_p{fam}_l{layer}", received[pos], two_d(w), two_d(sharded_m[fam]),
                                         two_d(sharded_v[fam]), layer, tile))
            pos += 1
        sh_out[fam] = [jnp.stack([per_layer[l][t] for l in range(len(parts))]).reshape(w.shape) for t in range(4)]
    small_ids = (3, 4, 5, 6, 7)
    pack_small = lambda ts: _pack_rows(jnp.concatenate([ts[i].reshape(-1) for i in small_ids]), LANES)
    small_out = adamw_shard("adamw_small", received[pos], pack_small(sharded_w), pack_small(sharded_m), pack_small(sharded_v), 0, 48)
    small_shapes = [sharded_w[i].shape for i in small_ids]
    for t in range(4):
        for i, val in zip(small_ids, _unpack(small_out[t].reshape(-1), small_shapes)):
            if sh_out[i] is None:
                sh_out[i] = [None] * 4
            sh_out[i][t] = val
    sh_grad, sh_delta, sh_m, sh_v = [[sh_out[i][t] for i in range(len(sharded_w))] for t in range(4)]

    rep_g = (jnp.stack([t[0] for t in g_attn_norm]), jnp.stack([t[0] for t in g_ffn_norm]), jnp.stack(g_a_q), jnp.stack(g_a_k))
    n_rep = sum(t.size for t in rep_w)
    n_rep_rows = -(-n_rep // (8 * LANES)) * 8
    n_loss_rows = 8 * D_MODEL // LANES

    def pack_rep(ts, tail):
        flat = jnp.concatenate([t.reshape(-1) for t in ts])
        flat = jnp.pad(flat, (0, n_rep_rows * LANES - n_rep))
        return jnp.concatenate([flat.reshape(n_rep_rows, LANES), tail], axis=0)

    zeros_tail = jnp.zeros((n_loss_rows, LANES), F32)
    rep_out = allreduce_adamw_replicated(pack_rep(rep_g, sq_err.reshape(n_loss_rows, LANES)), pack_rep(rep_w, zeros_tail),
                                         pack_rep(rep_m, zeros_tail), pack_rep(rep_v, zeros_tail + 1.0), n_loss_rows)
    rep_shapes = [t.shape for t in rep_w]
    r_grad, r_delta, r_m, r_v = [_unpack(p.reshape(-1), rep_shapes) for p in rep_out[:4]]
    loss = rep_out[4][0, 0]

    def ordered(rep, sh):
        return [rep[0], rep[1], sh[0], rep[2], rep[3]] + list(sh[1:])

    return loss, (*ordered(r_grad, sh_grad), *ordered(r_delta, sh_delta), *ordered(r_m, sh_m), *ordered(r_v, sh_v))
```
